```python
import math
import jax, jax.numpy as jnp
from jax import lax
import numpy as np

D_MODEL = 2048
BATCH = 8
SEQ = 4096
DEPTH = 1

CHUNK = 64
Q_BLOCK = 128
HEAD_DIM = 128
SB_HEADS = 8
DN_HEADS = 8
SB_WIDTH = SB_HEADS * HEAD_DIM
DN_WIDTH = DN_HEADS * HEAD_DIM
MIX_WIDTH = SB_WIDTH + DN_WIDTH
SHORT_CONV = 4
FFN_CONV = 3
D_FF = 5632
IN_COLS = 3 * SB_WIDTH + 4 * DN_WIDTH + 2 * DN_HEADS
EPS = 1e-6

kernel_name = "sb_gdn_hybrid_convffn_block"


def rmsnorm(x, gain):
    xf = x.astype(jnp.float32)
    y = xf * lax.rsqrt(jnp.mean(xf * xf, axis=-1, keepdims=True) + EPS)
    return (y * gain.astype(jnp.float32)).astype(x.dtype)


def l2norm(x):
    xf = x.astype(jnp.float32)
    return xf * lax.rsqrt(jnp.sum(xf * xf, axis=-1, keepdims=True) + EPS)


def causal_dwconv(x, w):
    K = w.shape[0]
    T = x.shape[1]
    xp = jnp.pad(x, ((0, 0), (K - 1, 0), (0, 0)))
    out = xp[:, 0:T] * w[0]
    for j in range(1, K):
        out = out + xp[:, j:j + T] * w[j]
    return out


def _heads(t, n):
    B, T, _ = t.shape
    return t.reshape(B, T, n, HEAD_DIM).transpose(0, 2, 1, 3)


def stick_breaking_attention(q, k, v):
    B, H, T, Dh = q.shape
    scale = Dh ** -0.5
    outs = []
    for blk in range(T // Q_BLOCK):
        q0 = blk * Q_BLOCK
        kend = q0 + Q_BLOCK
        z = jnp.einsum('bhqd,bhkd->bhqk', q[:, :, q0:kend], k[:, :, :kend]).astype(jnp.float32) * scale
        t_idx = q0 + jnp.arange(Q_BLOCK)[:, None]
        s_idx = jnp.arange(kend)[None, :]
        valid = s_idx < t_idx
        log_beta = jax.nn.log_sigmoid(z)
        log_1m = jnp.where(valid, jax.nn.log_sigmoid(-z), 0.0)
        later = lax.cumsum(log_1m, axis=3, reverse=True) - log_1m
        att = jnp.where(valid, jnp.exp(log_beta + later), 0.0)
        outs.append(jnp.einsum('bhqk,bhkd->bhqd', att.astype(v.dtype), v[:, :, :kend]))
    return jnp.concatenate(outs, axis=2)


def gated_delta_rule(q, k, v, g, beta):
    out_dtype = v.dtype
    B, H, T, Dk = q.shape
    Dv = v.shape[-1]
    C = CHUNK
    N = T // C
    q = q.astype(jnp.float32) * (Dk ** -0.5)
    k = k.astype(jnp.float32)
    v = v.astype(jnp.float32)
    beta = beta.astype(jnp.float32)
    q = q.reshape(B, H, N, C, Dk)
    k = k.reshape(B, H, N, C, Dk)
    v = v.reshape(B, H, N, C, Dv)
    beta = beta.reshape(B, H, N, C)
    g = jnp.cumsum(g.astype(jnp.float32).reshape(B, H, N, C), axis=-1)

    causal = jnp.tril(jnp.ones((C, C), dtype=bool))
    strict = jnp.tril(jnp.ones((C, C), dtype=bool), k=-1)
    diff = g[..., :, None] - g[..., None, :]
    decay = jnp.where(causal, jnp.exp(jnp.where(causal, diff, 0.0)), 0.0)

    k_beta = k * beta[..., None]
    v_beta = v * beta[..., None]
    L = jnp.where(strict, jnp.einsum('bhncd,bhnmd->bhncm', k_beta, k) * decay, 0.0)
    eye = jnp.eye(C, dtype=jnp.float32)
    T_mat = lax.linalg.triangular_solve(eye + L, jnp.broadcast_to(eye, L.shape),
                                        left_side=True, lower=True, unit_diagonal=True)
    u = jnp.einsum('bhncm,bhnmd->bhncd', T_mat, v_beta)
    w = jnp.einsum('bhncm,bhnmd->bhncd', T_mat, k_beta * jnp.exp(g)[..., None])
    qk_intra = jnp.where(causal, jnp.einsum('bhncd,bhnmd->bhncm', q, k) * decay, 0.0)
    g_last = g[..., -1]
    k_to_end = k * jnp.exp(g_last[..., None] - g)[..., None]
    q_decay = q * jnp.exp(g)[..., None]

    def step(S, xs):
        u_i, w_i, qa_i, qd_i, kt_i, gl_i = xs
        v_new = u_i - jnp.einsum('bhcd,bhde->bhce', w_i, S)
        o = jnp.einsum('bhcd,bhde->bhce', qd_i, S) + jnp.einsum('bhcm,bhme->bhce', qa_i, v_new)
        S = S * jnp.exp(gl_i)[..., None, None] + jnp.einsum('bhcd,bhce->bhde', kt_i, v_new)
        return S, o

    to_scan = lambda t: jnp.moveaxis(t, 2, 0)
    xs = (to_scan(u), to_scan(w), to_scan(qk_intra), to_scan(q_decay), to_scan(k_to_end),
          jnp.moveaxis(g_last, 2, 0))
    S0 = jnp.zeros((B, H, Dk, Dv), dtype=jnp.float32)
    _, o = lax.scan(step, S0, xs)
    o = jnp.moveaxis(o, 0, 2).reshape(B, H, T, Dv)
    return o.astype(out_dtype)


def token_mixer(xn, w_in, sb_out_gain, dn_conv_w, dn_a_log, dn_dt_bias, dn_out_gain, w_out):
    B, T, _ = xn.shape
    proj = jnp.einsum('btd,dc->btc', xn, w_in)
    sizes = (SB_WIDTH, SB_WIDTH, SB_WIDTH, 3 * DN_WIDTH, DN_WIDTH, DN_HEADS, DN_HEADS)
    offs = np.cumsum(sizes)[:-1].tolist()
    sb_q, sb_k, sb_v, dn_qkv, dn_z, dn_b, dn_a = jnp.split(proj, offs, axis=-1)

    o_sb = stick_breaking_attention(_heads(sb_q, SB_HEADS), _heads(sb_k, SB_HEADS), _heads(sb_v, SB_HEADS))
    o_sb = rmsnorm(o_sb.transpose(0, 2, 1, 3), sb_out_gain).reshape(B, T, SB_WIDTH)

    dn_qkv = jax.nn.silu(causal_dwconv(dn_qkv, dn_conv_w))
    dq, dk, dv = jnp.split(dn_qkv, 3, axis=-1)
    q = l2norm(_heads(dq, DN_HEADS))
    k = l2norm(_heads(dk, DN_HEADS))
    v = _heads(dv, DN_HEADS)
    beta = jax.nn.sigmoid(dn_b.astype(jnp.float32)).transpose(0, 2, 1)
    g = -(jnp.exp(dn_a_log.astype(jnp.float32)) *
          jax.nn.softplus(dn_a.astype(jnp.float32) + dn_dt_bias.astype(jnp.float32))).transpose(0, 2, 1)
    o_dn = gated_delta_rule(q, k, v, g, beta).transpose(0, 2, 1, 3)
    o_dn = rmsnorm(o_dn, dn_out_gain) * jax.nn.silu(dn_z.reshape(B, T, DN_HEADS, HEAD_DIM))
    o_dn = o_dn.reshape(B, T, DN_WIDTH).astype(xn.dtype)

    mix = jnp.concatenate([o_sb.astype(xn.dtype), o_dn], axis=-1)
    return jnp.einsum('btc,cd->btd', mix, w_out)


def conv_ffn(xn, w_up, ffn_conv_w, ffn_conv_b, w_down):
    h = jnp.einsum('btd,df->btf', xn, w_up)
    h = causal_dwconv(h, ffn_conv_w) + ffn_conv_b
    gate, val = jnp.split(h, 2, axis=-1)
    return jnp.einsum('btf,fd->btd', jax.nn.gelu(gate, approximate=True) * val, w_down)


def _fwd_setup_inputs(seed: int = 0) -> dict:
    key = jax.random.key(seed)
    ks = jax.random.split(key, 20)
    f32 = jnp.float32
    nrm = lambda k, shape, s: jax.random.normal(k, shape, f32) * s
    gain = lambda k, shape: 1.0 + 0.05 * jax.random.normal(k, shape, f32)
    dt = jnp.exp(jax.random.uniform(ks[5], (DEPTH, DN_HEADS), f32, math.log(1e-3), math.log(1e-1)))
    return {
        "x": jax.random.normal(ks[0], (BATCH, SEQ, D_MODEL), f32),
        "w_in": nrm(ks[1], (DEPTH, D_MODEL, IN_COLS), D_MODEL ** -0.5),
        "sb_out_gain": gain(ks[2], (DEPTH, HEAD_DIM)),
        "dn_conv_w": nrm(ks[3], (DEPTH, SHORT_CONV, 3 * DN_WIDTH), SHORT_CONV ** -0.5),
        "dn_a_log": jnp.log(jax.random.uniform(ks[4], (DEPTH, DN_HEADS), f32, 1.0, 16.0)),
        "dn_dt_bias": dt + jnp.log(-jnp.expm1(-dt)),
        "dn_out_gain": gain(ks[6], (DEPTH, HEAD_DIM)),
        "w_out": nrm(ks[7], (DEPTH, MIX_WIDTH, D_MODEL), MIX_WIDTH ** -0.5),
        "ln_mix_pre": gain(ks[8], (DEPTH, D_MODEL)),
        "ln_mix_post": gain(ks[9], (DEPTH, D_MODEL)),
        "w_up": nrm(ks[10], (DEPTH, D_MODEL, 2 * D_FF), D_MODEL ** -0.5),
        "ffn_conv_w": nrm(ks[11], (DEPTH, FFN_CONV, 2 * D_FF), FFN_CONV ** -0.5),
        "ffn_conv_b": nrm(ks[12], (DEPTH, 2 * D_FF), 0.01),
        "w_down": nrm(ks[13], (DEPTH, D_FF, D_MODEL), D_FF ** -0.5),
        "ln_ffn_pre": gain(ks[14], (DEPTH, D_MODEL)),
        "ln_ffn_post": gain(ks[15], (DEPTH, D_MODEL)),
    }


def _fwd_reference(x, w_in, sb_out_gain, dn_conv_w, dn_a_log, dn_dt_bias, dn_out_gain, w_out,
              ln_mix_pre, ln_mix_post, w_up, ffn_conv_w, ffn_conv_b, w_down, ln_ffn_pre, ln_ffn_post):
    h = x
    for l in range(DEPTH):
        m = token_mixer(rmsnorm(h, ln_mix_pre[l]), w_in[l], sb_out_gain[l], dn_conv_w[l],
                        dn_a_log[l], dn_dt_bias[l], dn_out_gain[l], w_out[l])
        h = h + rmsnorm(m, ln_mix_post[l])
        f = conv_ffn(rmsnorm(h, ln_ffn_pre[l]), w_up[l], ffn_conv_w[l], ffn_conv_b[l], w_down[l])
        h = h + rmsnorm(f, ln_ffn_post[l])
    return h


import jax as _jax
import jax.numpy as _jnp

TWIN_FORMAT = 'train_step'
FWD_PARAMS = ['x', 'w_in', 'sb_out_gain', 'dn_conv_w', 'dn_a_log', 'dn_dt_bias', 'dn_out_gain', 'w_out', 'ln_mix_pre', 'ln_mix_post', 'w_up', 'ffn_conv_w', 'ffn_conv_b', 'w_down', 'ln_ffn_pre', 'ln_ffn_post']
TWIN_WEIGHTS = ['w_in', 'sb_out_gain', 'dn_conv_w', 'dn_a_log', 'dn_dt_bias', 'dn_out_gain', 'w_out', 'ln_mix_pre', 'ln_mix_post', 'w_up', 'ffn_conv_w', 'ffn_conv_b', 'w_down', 'ln_ffn_pre', 'ln_ffn_post']
TWIN_DIFF_INPUT = 'x'
TWIN_INPUTS = ['x', 'w_in', 'sb_out_gain', 'dn_conv_w', 'dn_a_log', 'dn_dt_bias', 'dn_out_gain', 'w_out', 'ln_mix_pre', 'ln_mix_post', 'w_up', 'ffn_conv_w', 'ffn_conv_b', 'w_down', 'ln_ffn_pre', 'ln_ffn_post', 'loss_target', 'm_w_in', 'm_sb_out_gain', 'm_dn_conv_w', 'm_dn_a_log', 'm_dn_dt_bias', 'm_dn_out_gain', 'm_w_out', 'm_ln_mix_pre', 'm_ln_mix_post', 'm_w_up', 'm_ffn_conv_w', 'm_ffn_conv_b', 'm_w_down', 'm_ln_ffn_pre', 'm_ln_ffn_post', 'v_w_in', 'v_sb_out_gain', 'v_dn_conv_w', 'v_dn_a_log', 'v_dn_dt_bias', 'v_dn_out_gain', 'v_w_out', 'v_ln_mix_pre', 'v_ln_mix_post', 'v_w_up', 'v_ffn_conv_w', 'v_ffn_conv_b', 'v_w_down', 'v_ln_ffn_pre', 'v_ln_ffn_post']
TWIN_OUTPUTS = ['loss', 'grad_x', 'grad_w_in', 'grad_sb_out_gain', 'grad_dn_conv_w', 'grad_dn_a_log', 'grad_dn_dt_bias', 'grad_dn_out_gain', 'grad_w_out', 'grad_ln_mix_pre', 'grad_ln_mix_post', 'grad_w_up', 'grad_ffn_conv_w', 'grad_ffn_conv_b', 'grad_w_down', 'grad_ln_ffn_pre', 'grad_ln_ffn_post', 'delta_w_in', 'delta_sb_out_gain', 'delta_dn_conv_w', 'delta_dn_a_log', 'delta_dn_dt_bias', 'delta_dn_out_gain', 'delta_w_out', 'delta_ln_mix_pre', 'delta_ln_mix_post', 'delta_w_up', 'delta_ffn_conv_w', 'delta_ffn_conv_b', 'delta_w_down', 'delta_ln_ffn_pre', 'delta_ln_ffn_post', 'new_m_w_in', 'new_m_sb_out_gain', 'new_m_dn_conv_w', 'new_m_dn_a_log', 'new_m_dn_dt_bias', 'new_m_dn_out_gain', 'new_m_w_out', 'new_m_ln_mix_pre', 'new_m_ln_mix_post', 'new_m_w_up', 'new_m_ffn_conv_w', 'new_m_ffn_conv_b', 'new_m_w_down', 'new_m_ln_ffn_pre', 'new_m_ln_ffn_post', 'new_v_w_in', 'new_v_sb_out_gain', 'new_v_dn_conv_w', 'new_v_dn_a_log', 'new_v_dn_dt_bias', 'new_v_dn_out_gain', 'new_v_w_out', 'new_v_ln_mix_pre', 'new_v_ln_mix_post', 'new_v_w_up', 'new_v_ffn_conv_w', 'new_v_ffn_conv_b', 'new_v_w_down', 'new_v_ln_ffn_pre', 'new_v_ln_ffn_post']
TWIN_LEAF_KINDS = {'loss': 'loss', 'grad_x': 'grad_x', 'grad_w_in': 'grad_w', 'grad_sb_out_gain': 'grad_w', 'grad_dn_conv_w': 'grad_w', 'grad_dn_a_log': 'grad_w', 'grad_dn_dt_bias': 'grad_w', 'grad_dn_out_gain': 'grad_w', 'grad_w_out': 'grad_w', 'grad_ln_mix_pre': 'grad_w', 'grad_ln_mix_post': 'grad_w', 'grad_w_up': 'grad_w', 'grad_ffn_conv_w': 'grad_w', 'grad_ffn_conv_b': 'grad_w', 'grad_w_down': 'grad_w', 'grad_ln_ffn_pre': 'grad_w', 'grad_ln_ffn_post': 'grad_w', 'delta_w_in': 'delta_w', 'delta_sb_out_gain': 'delta_w', 'delta_dn_conv_w': 'delta_w', 'delta_dn_a_log': 'delta_w', 'delta_dn_dt_bias': 'delta_w', 'delta_dn_out_gain': 'delta_w', 'delta_w_out': 'delta_w', 'delta_ln_mix_pre': 'delta_w', 'delta_ln_mix_post': 'delta_w', 'delta_w_up': 'delta_w', 'delta_ffn_conv_w': 'delta_w', 'delta_ffn_conv_b': 'delta_w', 'delta_w_down': 'delta_w', 'delta_ln_ffn_pre': 'delta_w', 'delta_ln_ffn_post': 'delta_w', 'new_m_w_in': 'new_m', 'new_m_sb_out_gain': 'new_m', 'new_m_dn_conv_w': 'new_m', 'new_m_dn_a_log': 'new_m', 'new_m_dn_dt_bias': 'new_m', 'new_m_dn_out_gain': 'new_m', 'new_m_w_out': 'new_m', 'new_m_ln_mix_pre': 'new_m', 'new_m_ln_mix_post': 'new_m', 'new_m_w_up': 'new_m', 'new_m_ffn_conv_w': 'new_m', 'new_m_ffn_conv_b': 'new_m', 'new_m_w_down': 'new_m', 'new_m_ln_ffn_pre': 'new_m', 'new_m_ln_ffn_post': 'new_m', 'new_v_w_in': 'new_v', 'new_v_sb_out_gain': 'new_v', 'new_v_dn_conv_w': 'new_v', 'new_v_dn_a_log': 'new_v', 'new_v_dn_dt_bias': 'new_v', 'new_v_dn_out_gain': 'new_v', 'new_v_w_out': 'new_v', 'new_v_ln_mix_pre': 'new_v', 'new_v_ln_mix_post': 'new_v', 'new_v_w_up': 'new_v', 'new_v_ffn_conv_w': 'new_v', 'new_v_ffn_conv_b': 'new_v', 'new_v_w_down': 'new_v', 'new_v_ln_ffn_pre': 'new_v', 'new_v_ln_ffn_post': 'new_v'}


def _forward(args):
    return _fwd_reference(*[args[k] for k in FWD_PARAMS])


def _output_shape():
    def fwd():
        inp = _fwd_setup_inputs(0)
        return _fwd_reference(*[inp[k] for k in FWD_PARAMS])
    out = _jax.eval_shape(fwd)
    return out.shape, out.dtype

N_MICROBATCH = 1
ADAM_LR = 0.001
ADAM_B1 = 0.9
ADAM_B2 = 0.999
ADAM_EPS = 1e-08
ADAM_WD = 0.01
ADAM_STEP = 10
PER_EXAMPLE_BATCH_AXIS = {'x': 0, 'loss_target': 0}
SHARED_INPUTS = []
_WEIGHT_DTYPES = {'w_in': _jnp.float32, 'sb_out_gain': _jnp.float32, 'dn_conv_w': _jnp.float32, 'dn_a_log': _jnp.float32, 'dn_dt_bias': _jnp.float32, 'dn_out_gain': _jnp.float32, 'w_out': _jnp.float32, 'ln_mix_pre': _jnp.float32, 'ln_mix_post': _jnp.float32, 'w_up': _jnp.float32, 'ffn_conv_w': _jnp.float32, 'ffn_conv_b': _jnp.float32, 'w_down': _jnp.float32, 'ln_ffn_pre': _jnp.float32, 'ln_ffn_post': _jnp.float32}
MOMENT_SCALE = {'w_in': 1.646480e-01, 'sb_out_gain': 8.458039e-01, 'dn_conv_w': 1.278105e-01, 'dn_a_log': 1.232422e+00, 'dn_dt_bias': 1.162792e+00, 'dn_out_gain': 6.006949e-01, 'w_out': 2.609484e-01, 'ln_mix_pre': 3.164819e-01, 'ln_mix_post': 1.598960e+01, 'w_up': 1.096403e-01, 'ffn_conv_w': 1.164140e-01, 'ffn_conv_b': 3.293766e-01, 'w_down': 2.050698e-01, 'ln_ffn_pre': 2.558815e-01, 'ln_ffn_post': 1.599562e+01}


def _to_microbatches(a, axis):
    t = _jnp.moveaxis(a, axis, 0)
    t = t.reshape((N_MICROBATCH, t.shape[0] // N_MICROBATCH) + t.shape[1:])
    return _jnp.moveaxis(t, 1, axis + 1)


def setup_inputs(seed: int = 0) -> dict:
    inp = _fwd_setup_inputs(seed)
    key = _jax.random.fold_in(_jax.random.key(seed), 7919)
    shape, _ = _output_shape()
    out = dict(inp)
    out["loss_target"] = _jax.random.normal(_jax.random.fold_in(key, 0), shape, _jnp.float32)
    for i, name in enumerate(TWIN_WEIGHTS):
        w = inp[name].astype(_jnp.float32)
        if MOMENT_SCALE is None:
            s = _jnp.sqrt(_jnp.mean(_jnp.square(w)) + 1e-30)
        else:
            s = MOMENT_SCALE[name]
        km, kv = _jax.random.split(_jax.random.fold_in(key, i + 1))
        out[name] = w
        out["m_" + name] = s * _jax.random.normal(km, w.shape, _jnp.float32)
        out["v_" + name] = (s * s) * _jax.random.uniform(kv, w.shape, _jnp.float32, 0.5, 1.5)
    if N_MICROBATCH > 1:
        for name, axis in PER_EXAMPLE_BATCH_AXIS.items():
            out[name] = _to_microbatches(out[name], axis)
    return {'x': out['x'], 'w_in': out['w_in'], 'sb_out_gain': out['sb_out_gain'], 'dn_conv_w': out['dn_conv_w'], 'dn_a_log': out['dn_a_log'], 'dn_dt_bias': out['dn_dt_bias'], 'dn_out_gain': out['dn_out_gain'], 'w_out': out['w_out'], 'ln_mix_pre': out['ln_mix_pre'], 'ln_mix_post': out['ln_mix_post'], 'w_up': out['w_up'], 'ffn_conv_w': out['ffn_conv_w'], 'ffn_conv_b': out['ffn_conv_b'], 'w_down': out['w_down'], 'ln_ffn_pre': out['ln_ffn_pre'], 'ln_ffn_post': out['ln_ffn_post'], 'loss_target': out['loss_target'], 'm_w_in': out['m_w_in'], 'm_sb_out_gain': out['m_sb_out_gain'], 'm_dn_conv_w': out['m_dn_conv_w'], 'm_dn_a_log': out['m_dn_a_log'], 'm_dn_dt_bias': out['m_dn_dt_bias'], 'm_dn_out_gain': out['m_dn_out_gain'], 'm_w_out': out['m_w_out'], 'm_ln_mix_pre': out['m_ln_mix_pre'], 'm_ln_mix_post': out['m_ln_mix_post'], 'm_w_up': out['m_w_up'], 'm_ffn_conv_w': out['m_ffn_conv_w'], 'm_ffn_conv_b': out['m_ffn_conv_b'], 'm_w_down': out['m_w_down'], 'm_ln_ffn_pre': out['m_ln_ffn_pre'], 'm_ln_ffn_post': out['m_ln_ffn_post'], 'v_w_in': out['v_w_in'], 'v_sb_out_gain': out['v_sb_out_gain'], 'v_dn_conv_w': out['v_dn_conv_w'], 'v_dn_a_log': out['v_dn_a_log'], 'v_dn_dt_bias': out['v_dn_dt_bias'], 'v_dn_out_gain': out['v_dn_out_gain'], 'v_w_out': out['v_w_out'], 'v_ln_mix_pre': out['v_ln_mix_pre'], 'v_ln_mix_post': out['v_ln_mix_post'], 'v_w_up': out['v_w_up'], 'v_ffn_conv_w': out['v_ffn_conv_w'], 'v_ffn_conv_b': out['v_ffn_conv_b'], 'v_w_down': out['v_w_down'], 'v_ln_ffn_pre': out['v_ln_ffn_pre'], 'v_ln_ffn_post': out['v_ln_ffn_post']}


def _loss(weights, diff, rest, loss_target):
    with _jax.named_scope("forward"):
        args = {**rest, TWIN_DIFF_INPUT: diff, **{k: w.astype(_WEIGHT_DTYPES[k]) for k, w in weights.items()}}
        y = _forward(args)
    with _jax.named_scope("loss_head"):
        err = _jnp.square(y.astype(_jnp.float32) - loss_target)
        return 0.5 * _jnp.sum(_jnp.mean(err, axis=-1)) if err.ndim else 0.5 * err


def _adamw(w, g, m, v):
    m = ADAM_B1 * m + (1.0 - ADAM_B1) * g
    v = ADAM_B2 * v + (1.0 - ADAM_B2) * _jnp.square(g)
    m_hat = m / (1.0 - ADAM_B1 ** ADAM_STEP)
    v_hat = v / (1.0 - ADAM_B2 ** ADAM_STEP)
    delta = -ADAM_LR * (m_hat / (_jnp.sqrt(v_hat) + ADAM_EPS) + ADAM_WD * w)
    return delta, m, v


def reference(x, w_in, sb_out_gain, dn_conv_w, dn_a_log, dn_dt_bias, dn_out_gain, w_out, ln_mix_pre, ln_mix_post, w_up, ffn_conv_w, ffn_conv_b, w_down, ln_ffn_pre, ln_ffn_post, loss_target, m_w_in, m_sb_out_gain, m_dn_conv_w, m_dn_a_log, m_dn_dt_bias, m_dn_out_gain, m_w_out, m_ln_mix_pre, m_ln_mix_post, m_w_up, m_ffn_conv_w, m_ffn_conv_b, m_w_down, m_ln_ffn_pre, m_ln_ffn_post, v_w_in, v_sb_out_gain, v_dn_conv_w, v_dn_a_log, v_dn_dt_bias, v_dn_out_gain, v_w_out, v_ln_mix_pre, v_ln_mix_post, v_w_up, v_ffn_conv_w, v_ffn_conv_b, v_w_down, v_ln_ffn_pre, v_ln_ffn_post):
    given = dict(x=x, w_in=w_in, sb_out_gain=sb_out_gain, dn_conv_w=dn_conv_w, dn_a_log=dn_a_log, dn_dt_bias=dn_dt_bias, dn_out_gain=dn_out_gain, w_out=w_out, ln_mix_pre=ln_mix_pre, ln_mix_post=ln_mix_post, w_up=w_up, ffn_conv_w=ffn_conv_w, ffn_conv_b=ffn_conv_b, w_down=w_down, ln_ffn_pre=ln_ffn_pre, ln_ffn_post=ln_ffn_post, loss_target=loss_target, m_w_in=m_w_in, m_sb_out_gain=m_sb_out_gain, m_dn_conv_w=m_dn_conv_w, m_dn_a_log=m_dn_a_log, m_dn_dt_bias=m_dn_dt_bias, m_dn_out_gain=m_dn_out_gain, m_w_out=m_w_out, m_ln_mix_pre=m_ln_mix_pre, m_ln_mix_post=m_ln_mix_post, m_w_up=m_w_up, m_ffn_conv_w=m_ffn_conv_w, m_ffn_conv_b=m_ffn_conv_b, m_w_down=m_w_down, m_ln_ffn_pre=m_ln_ffn_pre, m_ln_ffn_post=m_ln_ffn_post, v_w_in=v_w_in, v_sb_out_gain=v_sb_out_gain, v_dn_conv_w=v_dn_conv_w, v_dn_a_log=v_dn_a_log, v_dn_dt_bias=v_dn_dt_bias, v_dn_out_gain=v_dn_out_gain, v_w_out=v_w_out, v_ln_mix_pre=v_ln_mix_pre, v_ln_mix_post=v_ln_mix_post, v_w_up=v_w_up, v_ffn_conv_w=v_ffn_conv_w, v_ffn_conv_b=v_ffn_conv_b, v_w_down=v_w_down, v_ln_ffn_pre=v_ln_ffn_pre, v_ln_ffn_post=v_ln_ffn_post)
    weights = {n: given[n] for n in TWIN_WEIGHTS}
    shared = {n: given[n] for n in SHARED_INPUTS}
    per_example = {n: given[n] for n in ['x']}
    grad_fn = _jax.value_and_grad(_loss, argnums=(0, 1))

    def one_microbatch(ex, loss_target):
        ex = dict(ex)
        diff = ex.pop(TWIN_DIFF_INPUT)
        return grad_fn(weights, diff, {**shared, **ex}, loss_target)

    if N_MICROBATCH == 1:
        loss, (grad_w, grad_x) = one_microbatch(per_example, given["loss_target"])
    else:
        def body(carry, xs):
            loss_sum, grad_sum = carry
            l_k, (gw_k, gx_k) = one_microbatch(xs[0], xs[1])
            with _jax.named_scope("update"):
                return (loss_sum + l_k, _jax.tree.map(_jnp.add, grad_sum, gw_k)), gx_k

        init = (_jnp.zeros((), _jnp.float32), _jax.tree.map(_jnp.zeros_like, weights))
        (loss, grad_w), grad_x = _jax.lax.scan(body, init, (per_example, given["loss_target"]))
    with _jax.named_scope("update"):
        delta_w, new_m, new_v = {}, {}, {}
        for n in TWIN_WEIGHTS:
            delta_w[n], new_m[n], new_v[n] = _adamw(weights[n], grad_w[n], given["m_" + n], given["v_" + n])
    return (loss, grad_x, *[grad_w[n] for n in TWIN_WEIGHTS], *[delta_w[n] for n in TWIN_WEIGHTS],
            *[new_m[n] for n in TWIN_WEIGHTS], *[new_v[n] for n in TWIN_WEIGHTS])
```

```python
import functools
import math

import jax
import jax.numpy as jnp
from jax import lax
from jax.experimental import pallas as pl
from jax.experimental.pallas import tpu as pltpu

F32 = jnp.float32
BF16 = jnp.bfloat16

N_DEV = 8
HEAD_DIM = 128
SB_HEADS = 8
DN_HEADS = 8
DN_CHUNK = 128
INV_BLOCK = 16
Q_BLOCK = 128
SHORT_CONV = 4
FFN_CONV = 3
EPS = 1e-6
LANES = 128
SUBLANES = 8
VMEM_CAP = 56 * 2**20

ADAM_LR = 0.001
ADAM_B1 = 0.9
ADAM_B2 = 0.999
ADAM_EPS = 1e-08
ADAM_WD = 0.01
ADAM_STEP = 10

MESH = pl.DeviceIdType.MESH

assert Q_BLOCK == HEAD_DIM == DN_CHUNK == LANES


def _tile(n, cap, mult):
    if n <= cap:
        return n
    t = (cap // mult) * mult
    while t >= mult:
        if n % t == 0:
            return t
        t -= mult
    raise ValueError(f"no tile for {n} under {cap} in multiples of {mult}")


def _params(sem, vmem_bytes):
    limit = int(min(VMEM_CAP, max(vmem_bytes, 16 * 2**20)))
    if not sem:
        return pltpu.CompilerParams(vmem_limit_bytes=limit)
    return pltpu.CompilerParams(dimension_semantics=sem, vmem_limit_bytes=limit)


def _nbytes(shape, dtype):
    return math.prod(shape) * jnp.dtype(dtype).itemsize


_NN = (((1,), (0,)), ((), ()))
_NT = (((1,), (1,)), ((), ()))
_TN = (((0,), (0,)), ((), ()))


def _dot(a, b, dims=_NN):
    return lax.dot_general(a.astype(BF16), b.astype(BF16), dims, preferred_element_type=F32)


def _split2(x):
    hi = x.astype(BF16)
    lo = (x - hi.astype(F32)).astype(BF16)
    return hi, lo


def _split3(x):
    hi = x.astype(BF16)
    r = x - hi.astype(F32)
    mid = r.astype(BF16)
    lo = (r - mid.astype(F32)).astype(BF16)
    return hi, mid, lo


def _dot01(x, m01, passes=3):
    parts = _split3(x) if passes == 3 else _split2(x)
    out = None
    for p in parts:
        t = lax.dot_general(p, m01, _NN, preferred_element_type=F32)
        out = t if out is None else out + t
    return out


def _dot01_left(m01, x, passes=3):
    parts = _split3(x) if passes == 3 else _split2(x)
    out = None
    for p in parts:
        t = lax.dot_general(m01, p, _NN, preferred_element_type=F32)
        out = t if out is None else out + t
    return out


def _mm3(a, b, dims=_NN):
    ah, al = _split2(a)
    bh, bl = _split2(b)
    d = functools.partial(lax.dot_general, dimension_numbers=dims, preferred_element_type=F32)
    return d(ah, bh) + (d(ah, bl) + d(al, bh))


def _rowsum(x):
    return jnp.sum(x, axis=1, keepdims=True)


def _sigmoid(x):
    return 1.0 / (1.0 + jnp.exp(-x))


def _softplus(x):
    return jnp.maximum(x, 0.0) + jnp.log(1.0 + jnp.exp(-jnp.abs(x)))


def _silu(x):
    return x * _sigmoid(x)


def _silu_grad(x):
    s = _sigmoid(x)
    return s * (1.0 + x * (1.0 - s))


_GELU_C = math.sqrt(2.0 / math.pi)


def _gelu(x):
    return 0.5 * x * (1.0 + jnp.tanh(_GELU_C * (x + 0.044715 * x * x * x)))


def _gelu_grad(x):
    th = jnp.tanh(_GELU_C * (x + 0.044715 * x * x * x))
    return 0.5 * (1.0 + th) + 0.5 * x * (1.0 - th * th) * _GELU_C * (1.0 + 3.0 * 0.044715 * x * x)


def _rms(x, g):
    r = lax.rsqrt(jnp.mean(x * x, axis=-1, keepdims=True) + EPS)
    return x * r * g


def _rms_bwd(dy, x, g):
    r = lax.rsqrt(jnp.mean(x * x, axis=-1, keepdims=True) + EPS)
    xh = x * r
    gdy = dy * g
    dx = r * (gdy - xh * jnp.mean(gdy * xh, axis=-1, keepdims=True))
    return dx, jnp.sum(dy * xh, axis=0, keepdims=True)


def _iota2(shape, axis):
    return lax.broadcasted_iota(jnp.int32, shape, axis)


def _shift_down(cur, prev8, k):
    n = cur.shape[0]
    r = pltpu.roll(cur, k, 0)
    pr = pltpu.roll(prev8, k, 0)
    head = jnp.where(_iota2(pr.shape, 0) < k, pr, r[0:SUBLANES])
    if n == SUBLANES:
        return head
    return jnp.concatenate([head, r[SUBLANES:]], axis=0)


def _shift_up(cur, next8, k):
    n = cur.shape[0]
    r = pltpu.roll(cur, n - k, 0)
    nr = pltpu.roll(next8, SUBLANES - k, 0)
    tail = jnp.where(_iota2(nr.shape, 0) >= SUBLANES - k, nr, r[n - SUBLANES:])
    if n == SUBLANES:
        return tail
    return jnp.concatenate([r[:n - SUBLANES], tail], axis=0)


def _causal_conv(cur, prev8, w_ref, taps):
    out = cur * w_ref[taps - 1:taps, :]
    for j in range(taps - 1):
        out = out + _shift_down(cur, prev8, taps - 1 - j) * w_ref[j:j + 1, :]
    return out


def _anti_conv(cur, next8, w_ref, taps):
    out = cur * w_ref[taps - 1:taps, :]
    for j in range(taps - 1):
        out = out + _shift_up(cur, next8, taps - 1 - j) * w_ref[j:j + 1, :]
    return out


def _matmul(a, b, mode, out_dtype, name, tm_cap=512, tn_cap=1024, tk_cap=512):
    if mode == "nn":
        (M, K), N = a.shape, b.shape[1]
    elif mode == "nt":
        (M, K), N = a.shape, b.shape[0]
    else:
        (K, M), N = a.shape, b.shape[1]
    tm = _tile(M, tm_cap, LANES)
    tn = _tile(N, tn_cap, LANES)
    tk = _tile(K, tk_cap, LANES)
    nk = K // tk
    dims = {"nn": _NN, "nt": _NT, "tn": _TN}[mode]
    a_spec = pl.BlockSpec((tk, tm), lambda i, j, k: (k, i)) if mode == "tn" else pl.BlockSpec((tm, tk), lambda i, j, k: (i, k))
    b_spec = pl.BlockSpec((tn, tk), lambda i, j, k: (j, k)) if mode == "nt" else pl.BlockSpec((tk, tn), lambda i, j, k: (k, j))

    def body(a_ref, b_ref, o_ref, acc_ref):
        k = pl.program_id(2)

        @pl.when(k == 0)
        def _():
            acc_ref[...] = jnp.zeros_like(acc_ref)

        acc_ref[...] += lax.dot_general(a_ref[...], b_ref[...], dims, preferred_element_type=F32)

        @pl.when(k == nk - 1)
        def _():
            o_ref[...] = acc_ref[...].astype(o_ref.dtype)

    vmem = 2 * (_nbytes((tm, tk), a.dtype) + _nbytes((tk, tn), b.dtype) + _nbytes((tm, tn), out_dtype)) + _nbytes((tm, tn), F32)
    return pl.pallas_call(
        body, name=name, grid=(M // tm, N // tn, nk),
        in_specs=[a_spec, b_spec], out_specs=pl.BlockSpec((tm, tn), lambda i, j, k: (i, j)),
        out_shape=jax.ShapeDtypeStruct((M, N), out_dtype),
        scratch_shapes=[pltpu.VMEM((tm, tn), F32)],
        compiler_params=_params(("parallel", "parallel", "arbitrary"), vmem + 4 * 2**20),
    )(a, b)


def _row_call(body, name, T, D, ins, outs, tr, acc_outs=()):
    def spec(a, kind):
        if kind == "row":
            return pl.BlockSpec((tr, a.shape[1]), lambda i: (i, 0))
        return pl.BlockSpec(a.shape, lambda i: (0, 0))
    in_specs = [spec(a, k) for a, k in ins]
    out_specs = [spec(a, k) for a, k in outs] + [spec(a, "vec") for a in acc_outs]
    out_shape = [a for a, _ in outs] + list(acc_outs)
    vmem = 2 * sum(_nbytes((tr, a.shape[1]) if k == "row" else a.shape, a.dtype) for a, k in list(ins) + list(outs))
    return pl.pallas_call(
        body, name=name, grid=(T // tr,), in_specs=in_specs, out_specs=out_specs, out_shape=out_shape,
        compiler_params=_params(("arbitrary",), 3 * vmem + 8 * 2**20),
    )(*[a for a, _ in ins])


def _sds(shape, dtype):
    return jax.ShapeDtypeStruct(shape, dtype)


def _accumulate(ref, val):
    @pl.when(pl.program_id(0) == 0)
    def _():
        ref[...] = jnp.zeros_like(ref)
    ref[...] += val


def _norm_in(x, g):
    T, D = x.shape

    def body(x_ref, g_ref, o_ref):
        o_ref[...] = _rms(x_ref[...], g_ref[...]).astype(BF16)

    return _row_call(body, "norm_in", T, D, [(x, "row"), (g, "vec")], [(_sds((T, D), BF16), "row")], _tile(T, 256, 16))[0]


def _mix_residual(x, m, g_post, g_pre):
    T, D = x.shape

    def body(x_ref, m_ref, gp_ref, gn_ref, h_ref, hn_ref):
        h = x_ref[...] + _rms(m_ref[...], gp_ref[...])
        h_ref[...] = h
        hn_ref[...] = _rms(h, gn_ref[...]).astype(BF16)

    return _row_call(body, "mix_residual", T, D, [(x, "row"), (m, "row"), (g_post, "vec"), (g_pre, "vec")],
                     [(_sds((T, D), F32), "row"), (_sds((T, D), BF16), "row")], _tile(T, 256, 16))


def _loss_head(h, f, g_post, target):
    T, D = h.shape

    def body(h_ref, f_ref, g_ref, t_ref, dy_ref, df_ref, dg_ref, loss_ref):
        f = f_ref[...]
        g = g_ref[...]
        diff = h_ref[...] + _rms(f, g) - t_ref[...]
        dy = diff * (1.0 / D)
        dy_ref[...] = dy
        df, dg = _rms_bwd(dy, f, g)
        df_ref[...] = df.astype(BF16)
        _accumulate(dg_ref, dg)
        _accumulate(loss_ref, jnp.full((1, LANES), 0.5 / D, F32) * jnp.sum(diff * diff))

    return _row_call(body, "loss_head", T, D, [(h, "row"), (f, "row"), (g_post, "vec"), (target, "row")],
                     [(_sds((T, D), F32), "row"), (_sds((T, D), BF16), "row")], _tile(T, 256, 16),
                     acc_outs=[_sds((1, D), F32), _sds((1, LANES), F32)])


def _ffn_residual_bwd(dy, dhn, h, g_pre, m, g_post):
    T, D = h.shape

    def body(dy_ref, dhn_ref, h_ref, gn_ref, m_ref, gp_ref, dh_ref, dm_ref, dgn_ref, dgp_ref):
        dhh, dgn = _rms_bwd(dhn_ref[...], h_ref[...], gn_ref[...])
        dh = dy_ref[...] + dhh
        dh_ref[...] = dh
        dm, dgp = _rms_bwd(dh, m_ref[...], gp_ref[...])
        dm_ref[...] = dm.astype(BF16)
        _accumulate(dgn_ref, dgn)
        _accumulate(dgp_ref, dgp)

    return _row_call(body, "ffn_residual_bwd", T, D,
                     [(dy, "row"), (dhn, "row"), (h, "row"), (g_pre, "vec"), (m, "row"), (g_post, "vec")],
                     [(_sds((T, D), F32), "row"), (_sds((T, D), BF16), "row")], _tile(T, 128, 16),
                     acc_outs=[_sds((1, D), F32), _sds((1, D), F32)])


def _input_bwd(dh, dxn, x, g):
    T, D = x.shape

    def body(dh_ref, dxn_ref, x_ref, g_ref, dx_ref, dg_ref):
        dx, dg = _rms_bwd(dxn_ref[...], x_ref[...], g_ref[...])
        dx_ref[...] = dh_ref[...] + dx
        _accumulate(dg_ref, dg)

    return _row_call(body, "input_bwd", T, D, [(dh, "row"), (dxn, "row"), (x, "row"), (g, "vec")],
                     [(_sds((T, D), F32), "row")], _tile(T, 256, 16), acc_outs=[_sds((1, D), F32)])


def _ffn_act(u, conv_w, conv_b):
    T, F2 = u.shape
    F = F2 // 2
    tc = _tile(F, 512, LANES)
    tr = _tile(T, 512, SUBLANES)
    nc = F // tc
    r8 = tr // SUBLANES

    def body(ug_ref, ugp_ref, uv_ref, uvp_ref, wg_ref, wv_ref, bg_ref, bv_ref, a_ref):
        first = pl.program_id(1) == 0
        cg = _causal_conv(ug_ref[...], jnp.where(first, 0.0, ugp_ref[...]), wg_ref, FFN_CONV) + bg_ref[...]
        cv = _causal_conv(uv_ref[...], jnp.where(first, 0.0, uvp_ref[...]), wv_ref, FFN_CONV) + bv_ref[...]
        a_ref[...] = (_gelu(cg) * cv).astype(BF16)

    cur = lambda off: pl.BlockSpec((tr, tc), lambda j, i: (i, j + off))
    prev = lambda off: pl.BlockSpec((SUBLANES, tc), lambda j, i: (jnp.maximum(i * r8 - 1, 0), j + off))
    wsp = lambda off: pl.BlockSpec((FFN_CONV, tc), lambda j, i: (0, j + off))
    bsp = lambda off: pl.BlockSpec((1, tc), lambda j, i: (0, j + off))
    return pl.pallas_call(
        body, name="ffn_act", grid=(nc, T // tr),
        in_specs=[cur(0), prev(0), cur(nc), prev(nc), wsp(0), wsp(nc), bsp(0), bsp(nc)],
        out_specs=pl.BlockSpec((tr, tc), lambda j, i: (i, j)),
        out_shape=_sds((T, F), BF16),
        compiler_params=_params(("parallel", "arbitrary"), 12 * _nbytes((tr, tc), F32) + 8 * 2**20),
    )(u, u, u, u, conv_w, conv_w, conv_b, conv_b)


def _ffn_act_bwd(u, conv_w, conv_b, da):
    T, F2 = u.shape
    F = F2 // 2
    tc = _tile(F, 512, LANES)
    tr = _tile(T, 512, SUBLANES)
    nc = F // tc
    r8 = tr // SUBLANES
    n8 = T // SUBLANES
    K = FFN_CONV

    def body(uo_ref, uop_ref, uon_ref, up_ref, upp_ref, upn_ref, da_ref, dan_ref,
             wo_ref, wp_ref, bo_ref, bp_ref, du_ref, dwb_ref):
        j = pl.program_id(0)
        i = pl.program_id(1)
        first = i == 0
        last = i == pl.num_programs(1) - 1
        is_gate = j < nc

        def dconv(uo, uo_prev, up, up_prev, da_):
            co = _causal_conv(uo, uo_prev, wo_ref, K) + bo_ref[...]
            cp = _causal_conv(up, up_prev, wp_ref, K) + bp_ref[...]
            return jnp.where(is_gate, da_ * cp * _gelu_grad(co), da_ * _gelu(cp))

        uo = uo_ref[...]
        uo_prev = jnp.where(first, 0.0, uop_ref[...])
        dc = dconv(uo, uo_prev, up_ref[...], jnp.where(first, 0.0, upp_ref[...]), da_ref[...])
        dcn = dconv(uon_ref[...], uo[tr - SUBLANES:], upn_ref[...], up_ref[tr - SUBLANES:, :], dan_ref[...])
        dcn = jnp.where(last, 0.0, dcn)
        du_ref[...] = _anti_conv(dc, dcn, wo_ref, K).astype(BF16)
        rows = [jnp.sum(dc * _shift_down(uo, uo_prev, K - 1 - t), axis=0, keepdims=True) for t in range(K - 1)]
        rows += [jnp.sum(dc * uo, axis=0, keepdims=True), jnp.sum(dc, axis=0, keepdims=True)]
        rows += [jnp.zeros_like(rows[0])] * (SUBLANES - len(rows))
        upd = jnp.concatenate(rows, axis=0)

        @pl.when(first)
        def _():
            dwb_ref[...] = jnp.zeros_like(dwb_ref)
        dwb_ref[...] += upd

    part = lambda j: (j + nc) % (2 * nc)
    cur = lambda f: pl.BlockSpec((tr, tc), lambda j, i: (i, f(j)))
    prev = lambda f: pl.BlockSpec((SUBLANES, tc), lambda j, i: (jnp.maximum(i * r8 - 1, 0), f(j)))
    nxt = lambda f: pl.BlockSpec((SUBLANES, tc), lambda j, i: (jnp.minimum((i + 1) * r8, n8 - 1), f(j)))
    own = lambda j: j
    dac = lambda j: j % nc
    wsp = lambda f: pl.BlockSpec((K, tc), lambda j, i: (0, f(j)))
    bsp = lambda f: pl.BlockSpec((1, tc), lambda j, i: (0, f(j)))
    return pl.pallas_call(
        body, name="ffn_act_bwd", grid=(2 * nc, T // tr),
        in_specs=[cur(own), prev(own), nxt(own), cur(part), prev(part), nxt(part), cur(dac), nxt(dac),
                  wsp(own), wsp(part), bsp(own), bsp(part)],
        out_specs=[pl.BlockSpec((tr, tc), lambda j, i: (i, j)), pl.BlockSpec((SUBLANES, tc), lambda j, i: (0, j))],
        out_shape=[_sds((T, F2), BF16), _sds((SUBLANES, F2), F32)],
        compiler_params=_params(("parallel", "arbitrary"), 16 * _nbytes((tr, tc), F32) + 8 * 2**20),
    )(u, u, u, u, u, u, da, da, conv_w, conv_w, conv_b, conv_b)


def _l2norm(s, scale):
    return s * (lax.rsqrt(jnp.sum(s * s, axis=-1, keepdims=True) + EPS) * scale)


def _dn_branch(proj, col0, conv_w, wcol0, l2, scale):
    T = proj.shape[0]
    W = DN_HEADS * HEAD_DIM
    tr = _tile(T, 512, SUBLANES)
    r8 = tr // SUBLANES
    cb0, wb0 = col0 // HEAD_DIM, wcol0 // HEAD_DIM

    def body(u_ref, up_ref, w_ref, o_ref):
        first = pl.program_id(1) == 0
        s = _silu(_causal_conv(u_ref[...], jnp.where(first, 0.0, up_ref[...]), w_ref, SHORT_CONV))
        o_ref[...] = _l2norm(s, scale) if l2 else s

    return pl.pallas_call(
        body, name=f"dn_branch_{col0}", grid=(DN_HEADS, T // tr),
        in_specs=[pl.BlockSpec((tr, HEAD_DIM), lambda h, i: (i, cb0 + h)),
                  pl.BlockSpec((SUBLANES, HEAD_DIM), lambda h, i: (jnp.maximum(i * r8 - 1, 0), cb0 + h)),
                  pl.BlockSpec((SHORT_CONV, HEAD_DIM), lambda h, i: (0, wb0 + h))],
        out_specs=pl.BlockSpec((tr, HEAD_DIM), lambda h, i: (i, h)),
        out_shape=_sds((T, W), F32),
        compiler_params=_params(("parallel", "arbitrary"), 16 * 2**20),
    )(proj, proj, conv_w)


def _dn_branch_bwd(proj, col0, conv_w, wcol0, l2, scale, dy):
    T = proj.shape[0]
    W = DN_HEADS * HEAD_DIM
    tr = _tile(T, 512, SUBLANES)
    r8 = tr // SUBLANES
    n8 = T // SUBLANES
    cb0, wb0 = col0 // HEAD_DIM, wcol0 // HEAD_DIM
    K = SHORT_CONV

    def body(u_ref, up_ref, un_ref, dy_ref, dyn_ref, w_ref, du_ref, dw_ref):
        i = pl.program_id(1)
        first = i == 0
        last = i == pl.num_programs(1) - 1

        def dconv(u, u_prev, dy_):
            c = _causal_conv(u, u_prev, w_ref, K)
            if l2:
                s = _silu(c)
                r = lax.rsqrt(jnp.sum(s * s, axis=-1, keepdims=True) + EPS)
                n = s * r
                ds = (scale * r) * (dy_ - n * jnp.sum(dy_ * n, axis=-1, keepdims=True))
            else:
                ds = dy_
            return ds * _silu_grad(c)

        u = u_ref[...]
        u_prev = jnp.where(first, 0.0, up_ref[...])
        dc = dconv(u, u_prev, dy_ref[...])
        dcn = jnp.where(last, 0.0, dconv(un_ref[...], u[tr - SUBLANES:], dyn_ref[...]))
        du_ref[...] = _anti_conv(dc, dcn, w_ref, K).astype(BF16)
        rows = [jnp.sum(dc * _shift_down(u, u_prev, K - 1 - t), axis=0, keepdims=True) for t in range(K - 1)]
        rows += [jnp.sum(dc * u, axis=0, keepdims=True)]
        rows += [jnp.zeros_like(rows[0])] * (SUBLANES - len(rows))
        upd = jnp.concatenate(rows, axis=0)

        @pl.when(first)
        def _():
            dw_ref[...] = jnp.zeros_like(dw_ref)
        dw_ref[...] += upd

    return pl.pallas_call(
        body, name=f"dn_branch_bwd_{col0}", grid=(DN_HEADS, T // tr),
        in_specs=[pl.BlockSpec((tr, HEAD_DIM), lambda h, i: (i, cb0 + h)),
                  pl.BlockSpec((SUBLANES, HEAD_DIM), lambda h, i: (jnp.maximum(i * r8 - 1, 0), cb0 + h)),
                  pl.BlockSpec((SUBLANES, HEAD_DIM), lambda h, i: (jnp.minimum((i + 1) * r8, n8 - 1), cb0 + h)),
                  pl.BlockSpec((tr, HEAD_DIM), lambda h, i: (i, h)),
                  pl.BlockSpec((SUBLANES, HEAD_DIM), lambda h, i: (jnp.minimum((i + 1) * r8, n8 - 1), h)),
                  pl.BlockSpec((K, HEAD_DIM), lambda h, i: (0, wb0 + h))],
        out_specs=[pl.BlockSpec((tr, HEAD_DIM), lambda h, i: (i, h)),
                   pl.BlockSpec((SUBLANES, HEAD_DIM), lambda h, i: (0, h))],
        out_shape=[_sds((T, W), BF16), _sds((SUBLANES, W), F32)],
        compiler_params=_params(("parallel", "arbitrary"), 16 * 2**20),
    )(proj, proj, proj, dy, dy, conv_w)


def _lane_masks(shape):
    lane = _iota2(shape, 1)
    return lane < DN_HEADS, (lane >= DN_HEADS) & (lane < 2 * DN_HEADS)


def _expand01(off):
    r = _iota2((LANES, DN_HEADS * HEAD_DIM), 0)
    c = _iota2((LANES, DN_HEADS * HEAD_DIM), 1)
    return (r == jnp.right_shift(c, int(math.log2(HEAD_DIM))) + off).astype(BF16)


def _select01(off):
    r = _iota2((DN_HEADS * HEAD_DIM, LANES), 0)
    c = _iota2((DN_HEADS * HEAD_DIM, LANES), 1)
    return (r == (c - off) * HEAD_DIM).astype(BF16)


def _dn_gates(proj, gate_block, a_log_l, dt_bias_l):
    T = proj.shape[0]
    C = DN_CHUNK
    W = DN_HEADS * HEAD_DIM

    def body(ba_ref, al_ref, dt_ref, gc_ref, beta_ref):
        ba = ba_ref[...]
        is_b, is_a = _lane_masks(ba.shape)
        g = jnp.where(is_a, -jnp.exp(al_ref[...]) * _softplus(ba + dt_ref[...]), 0.0)
        beta = jnp.where(is_b, _sigmoid(ba), 0.0)
        tri = (_iota2((C, C), 0) >= _iota2((C, C), 1)).astype(BF16)
        gc = _dot01_left(tri, g)
        gc_ref[...] = _dot01(gc, _expand01(DN_HEADS))
        beta_ref[...] = _dot01(beta, _expand01(0))

    vec = pl.BlockSpec((1, LANES), lambda n: (0, 0))
    return pl.pallas_call(
        body, name="dn_gates", grid=(T // C,),
        in_specs=[pl.BlockSpec((C, LANES), lambda n: (n, gate_block)), vec, vec],
        out_specs=[pl.BlockSpec((C, W), lambda n: (n, 0))] * 2,
        out_shape=[_sds((T, W), F32)] * 2,
        compiler_params=_params(("parallel",), 16 * 2**20),
    )(proj, a_log_l, dt_bias_l)


def _dn_gates_bwd(proj, gate_block, a_log_l, dt_bias_l, dgc_full, dbeta_full):
    T = proj.shape[0]
    C = DN_CHUNK
    W = DN_HEADS * HEAD_DIM

    def body(ba_ref, al_ref, dt_ref, dgc_ref, dbeta_ref, dba_ref, dal_ref, ddt_ref):
        ba = ba_ref[...]
        is_b, is_a = _lane_masks(ba.shape)
        ea = jnp.exp(al_ref[...])
        pre = ba + dt_ref[...]
        g = jnp.where(is_a, -ea * _softplus(pre), 0.0)
        beta = _sigmoid(ba)
        dgc = _dot01(dgc_ref[...], _select01(DN_HEADS))
        dbeta = _dot01(dbeta_ref[...], _select01(0))
        triu = (_iota2((C, C), 0) <= _iota2((C, C), 1)).astype(BF16)
        dg = _dot01_left(triu, dgc)
        da = jnp.where(is_a, dg * (-ea) * _sigmoid(pre), 0.0)
        dba_ref[...] = (da + jnp.where(is_b, dbeta * beta * (1.0 - beta), 0.0)).astype(BF16)
        _accumulate(dal_ref, jnp.sum(dg * g, axis=0, keepdims=True))
        _accumulate(ddt_ref, jnp.sum(da, axis=0, keepdims=True))

    vec = pl.BlockSpec((1, LANES), lambda n: (0, 0))
    full = pl.BlockSpec((C, W), lambda n: (n, 0))
    return pl.pallas_call(
        body, name="dn_gates_bwd", grid=(T // C,),
        in_specs=[pl.BlockSpec((C, LANES), lambda n: (n, gate_block)), vec, vec, full, full],
        out_specs=[pl.BlockSpec((C, LANES), lambda n: (n, 0)), vec, vec],
        out_shape=[_sds((T, LANES), BF16), _sds((1, LANES), F32), _sds((1, LANES), F32)],
        compiler_params=_params(("arbitrary",), 16 * 2**20),
    )(proj, a_log_l, dt_bias_l, dgc_full, dbeta_full)


def _unit_lower_inverse(L):
    C = L.shape[0]
    row, col = _iota2((C, C), 0), _iota2((C, C), 1)
    eye = (row == col).astype(F32)
    sh = int(math.log2(INV_BLOCK))
    Ld = jnp.where(jnp.right_shift(row, sh) == jnp.right_shift(col, sh), L, 0.0)
    Lo = L - Ld
    X = eye - Ld
    P = Ld
    for _ in range(int(math.log2(INV_BLOCK)) - 1):
        P = _mm3(P, P)
        X = X + _mm3(X, P)
    N = _mm3(X, Lo)
    Y = eye - N
    P = N
    for _ in range(int(math.log2(C // INV_BLOCK)) - 1):
        P = _mm3(P, P)
        Y = Y + _mm3(Y, P)
    return _mm3(Y, X)


def _dn_chunk_common(q, k, v, gc, beta, gl):
    C = q.shape[0]
    row, col = _iota2((C, C), 0), _iota2((C, C), 1)
    causal, strict = row >= col, row > col
    eg = jnp.exp(gc)
    decay = jnp.where(causal, jnp.exp(jnp.where(causal, gc - gc.T, 0.0)), 0.0)
    kb, vb = k * beta, v * beta
    L = jnp.where(strict, _dot(kb, k, _NT) * decay, 0.0)
    Aqk = jnp.where(causal, _dot(q, k, _NT) * decay, 0.0)
    ektg = jnp.exp(gl - gc)
    return dict(causal=causal, strict=strict, eg=eg, decay=decay, kb=kb, vb=vb, L=L, Aqk=Aqk, ektg=ektg,
                kbg=kb * eg, kte=k * ektg, qd=q * eg, egl=jnp.exp(gl))


def _dn_scan(qn, kn, vn, gc_full, beta_full, proj, z_col0, gain):
    T, W = qn.shape
    C = DN_CHUNK
    N = T // C
    H = DN_HEADS
    zb0 = z_col0 // HEAD_DIM

    def body(q_ref, k_ref, v_ref, gc_ref, beta_ref, z_ref, gain_ref, o_ref, mix_ref, tm_ref, s_ref, S):
        @pl.when(pl.program_id(1) == 0)
        def _():
            S[...] = jnp.zeros_like(S)

        q, k, v, gc, beta = q_ref[...], k_ref[...], v_ref[...], gc_ref[...], beta_ref[...]
        gl = gc_ref[C - 1:C, :]
        c = _dn_chunk_common(q, k, v, gc, beta, gl)
        Tm = _unit_lower_inverse(c["L"])
        u = _dot(Tm, c["vb"])
        w = _dot(Tm, c["kbg"])
        S0 = S[...]
        vnew = u - _dot(w, S0)
        o = _dot(c["qd"], S0) + _dot(c["Aqk"], vnew)
        S[...] = S0 * c["egl"] + _dot(c["kte"], vnew, _TN)
        tm_ref[...] = Tm
        s_ref[...] = S0
        o_ref[...] = o
        mix_ref[...] = (_rms(o, gain_ref[...]) * _silu(z_ref[...])).astype(BF16)

    blk = pl.BlockSpec((C, HEAD_DIM), lambda h, n: (n, h))
    mat = pl.BlockSpec((None, None, C, C), lambda h, n: (h, n, 0, 0))
    return pl.pallas_call(
        body, name="dn_scan", grid=(H, N),
        in_specs=[blk, blk, blk, blk, blk, pl.BlockSpec((C, HEAD_DIM), lambda h, n: (n, zb0 + h)),
                  pl.BlockSpec((1, HEAD_DIM), lambda h, n: (0, 0))],
        out_specs=[blk, blk, mat, mat],
        out_shape=[_sds((T, W), F32), _sds((T, W), BF16), _sds((H, N, C, C), F32), _sds((H, N, C, C), F32)],
        scratch_shapes=[pltpu.VMEM((HEAD_DIM, HEAD_DIM), F32)],
        compiler_params=_params(("parallel", "arbitrary"), 24 * 2**20),
    )(qn, kn, vn, gc_full, beta_full, proj, gain)


def _dn_scan_bwd(qn, kn, vn, gc_full, beta_full, proj, z_col0, gain, o_raw, tm_all, s_all, dmix, dmix_col0):
    T, W = qn.shape
    C = DN_CHUNK
    N = T // C
    H = DN_HEADS
    zb0 = z_col0 // HEAD_DIM
    mb0 = dmix_col0 // HEAD_DIM

    def body(q_ref, k_ref, v_ref, gc_ref, beta_ref, z_ref, gain_ref, o_ref, tm_ref, s_ref, dmix_ref,
             dq_ref, dk_ref, dv_ref, dgc_ref, dbeta_ref, dz_ref, dgain_ref, dS):
        @pl.when(pl.program_id(1) == 0)
        def _():
            dS[...] = jnp.zeros_like(dS)

        @pl.when((pl.program_id(0) == 0) & (pl.program_id(1) == 0))
        def _():
            dgain_ref[...] = jnp.zeros_like(dgain_ref)

        o, z, gain, dmix = o_ref[...], z_ref[...], gain_ref[...], dmix_ref[...]
        dz_ref[...] = (dmix * _rms(o, gain) * _silu_grad(z)).astype(BF16)
        do, dgain = _rms_bwd(dmix * _silu(z), o, gain)
        dgain_ref[...] += dgain

        q, k, v, gc, beta = q_ref[...], k_ref[...], v_ref[...], gc_ref[...], beta_ref[...]
        gl = gc_ref[C - 1:C, :]
        c = _dn_chunk_common(q, k, v, gc, beta, gl)
        Tm, S0, dS1 = tm_ref[...], s_ref[...], dS[...]
        w = _dot(Tm, c["kbg"])
        vnew = _dot(Tm, c["vb"]) - _dot(w, S0)

        dvnew = _dot(c["Aqk"], do, _TN) + _dot(c["kte"], dS1)
        dAqk = jnp.where(c["causal"], _dot(do, vnew, _NT), 0.0)
        dqd = _dot(do, S0, _NT)
        dkte = _dot(vnew, dS1, _NT)
        dgl = jnp.sum(dS1 * S0) * c["egl"]
        dw = -_dot(dvnew, S0, _NT)
        dS[...] = dS1 * c["egl"] + _dot(c["qd"], do, _TN) - _dot(w, dvnew, _TN)

        dTm = _dot(dvnew, c["vb"], _NT) + _dot(dw, c["kbg"], _NT)
        dvb = _dot(Tm, dvnew, _TN)
        dkbg = _dot(Tm, dw, _TN)
        dL = jnp.where(c["strict"], -_mm3(_mm3(Tm, dTm, _TN), Tm, _NT), 0.0)
        dP = dL * c["decay"]
        dQ = dAqk * c["decay"]
        M = dL * c["L"] + dAqk * c["Aqk"]
        dkb = _dot(dP, k) + dkbg * c["eg"]
        dk = _dot(dP, c["kb"], _TN) + _dot(dQ, q, _TN) + dkte * c["ektg"] + dkb * beta
        dq = _dot(dQ, k) + dqd * c["eg"]
        tk = _rowsum(dkte * c["kte"])
        dgc = (_rowsum(M) - _rowsum(M.T) + _rowsum(dqd * c["qd"]) - tk + _rowsum(dkbg * c["kbg"]))
        dgl = dgl + jnp.sum(tk)
        dgc = jnp.broadcast_to(dgc, (C, HEAD_DIM)) + jnp.where(_iota2((C, HEAD_DIM), 0) == C - 1, dgl, 0.0)
        dq_ref[...] = dq
        dk_ref[...] = dk
        dv_ref[...] = dvb * beta
        dgc_ref[...] = dgc
        dbeta_ref[...] = jnp.broadcast_to(_rowsum(dkb * k) + _rowsum(dvb * v), (C, HEAD_DIM))

    rev = lambda off: pl.BlockSpec((C, HEAD_DIM), lambda h, n: (N - 1 - n, off + h))
    mat = pl.BlockSpec((None, None, C, C), lambda h, n: (h, N - 1 - n, 0, 0))
    vec = pl.BlockSpec((1, HEAD_DIM), lambda h, n: (0, 0))
    return pl.pallas_call(
        body, name="dn_scan_bwd", grid=(H, N),
        in_specs=[rev(0), rev(0), rev(0), rev(0), rev(0), rev(zb0), vec, rev(0), mat, mat, rev(mb0)],
        out_specs=[rev(0)] * 6 + [vec],
        out_shape=[_sds((T, W), F32)] * 5 + [_sds((T, W), BF16), _sds((1, HEAD_DIM), F32)],
        scratch_shapes=[pltpu.VMEM((HEAD_DIM, HEAD_DIM), F32)],
        compiler_params=_params(("arbitrary", "arbitrary"), 24 * 2**20),
    )(qn, kn, vn, gc_full, beta_full, proj, gain, o_raw, tm_all, s_all, dmix)


def _sb_block(q, kj, i, j):
    B = Q_BLOCK
    z = _dot(q, kj, _NT) * (HEAD_DIM ** -0.5)
    valid = (_iota2((B, B), 1) + j * B) < (_iota2((B, B), 0) + i * B)
    lb = jnp.minimum(z, 0.0) - jnp.log(1.0 + jnp.exp(-jnp.abs(z)))
    return valid, lb, jnp.where(valid, lb - z, 0.0)


def _sb_attention(qkv, gain):
    T = qkv.shape[0]
    H = SB_HEADS
    B = Q_BLOCK

    def body(q_ref, k_ref, v_ref, gain_ref, o_ref, mix_ref, ltot_ref):
        i = pl.program_id(1)
        q = q_ref[...]
        upper = (_iota2((B, B), 0) > _iota2((B, B), 1)).astype(BF16)

        def step(jj, carry):
            acc, R = carry
            j = i - jj
            rows = pl.ds(pl.multiple_of(j * B, B), B)
            valid, lb, l1m = _sb_block(q, k_ref[rows, :], i, j)
            att = jnp.where(valid, jnp.exp(lb + R + _dot01(l1m, upper, passes=2)), 0.0)
            return acc + _dot(att, v_ref[rows, :]), R + _rowsum(l1m)

        acc, R = lax.fori_loop(0, i + 1, step, (jnp.zeros((B, HEAD_DIM), F32), jnp.zeros((B, 1), F32)))
        o_ref[...] = acc
        mix_ref[...] = _rms(acc, gain_ref[...]).astype(BF16)
        ltot_ref[...] = jnp.broadcast_to(R, (B, HEAD_DIM))

    head = lambda off: pl.BlockSpec((T, HEAD_DIM), lambda h, i: (0, off + h))
    blk = pl.BlockSpec((B, HEAD_DIM), lambda h, i: (i, h))
    return pl.pallas_call(
        body, name="sb_attention", grid=(H, T // B),
        in_specs=[blk, head(H), head(2 * H), pl.BlockSpec((1, HEAD_DIM), lambda h, i: (0, 0))],
        out_specs=[blk, blk, blk],
        out_shape=[_sds((T, H * HEAD_DIM), F32), _sds((T, H * HEAD_DIM), BF16), _sds((T, H * HEAD_DIM), F32)],
        compiler_params=_params(("parallel", "arbitrary"), 8 * _nbytes((T, HEAD_DIM), BF16) + 8 * 2**20),
    )(qkv, qkv, qkv, gain)


def _sb_attention_bwd(qkv, gain, o_raw, ltot, dmix):
    T = qkv.shape[0]
    H = SB_HEADS
    B = Q_BLOCK
    scale = HEAD_DIM ** -0.5

    def body(q_ref, k_ref, v_ref, gain_ref, o_ref, ltot_ref, dmix_ref, dq_ref, dk_ref, dv_ref, dgain_ref):
        i = pl.program_id(1)

        @pl.when(i == 0)
        def _():
            dk_ref[...] = jnp.zeros_like(dk_ref)
            dv_ref[...] = jnp.zeros_like(dv_ref)

        @pl.when((pl.program_id(0) == 0) & (i == 0))
        def _():
            dgain_ref[...] = jnp.zeros_like(dgain_ref)

        q = q_ref[...]
        o = o_ref[...]
        do, dgain = _rms_bwd(dmix_ref[...], o, gain_ref[...])
        dgain_ref[...] += dgain
        ltot = ltot_ref[...]
        do_b = do.astype(BF16)
        upto = (_iota2((B, B), 0) <= _iota2((B, B), 1)).astype(BF16)
        before = (_iota2((B, B), 0) < _iota2((B, B), 1)).astype(BF16)

        def step(j, carry):
            dq, PL, PG = carry
            rows = pl.ds(pl.multiple_of(j * B, B), B)
            kj = k_ref[rows, :]
            valid, lb, l1m = _sb_block(q, kj, i, j)
            att = jnp.where(valid, jnp.exp(lb + (ltot - PL - _dot01(l1m, upto, passes=2))), 0.0)
            sig = jnp.exp(lb)
            G = _dot(do_b, v_ref[rows, :], _NT) * att
            dv_ref[rows, :] += _dot(att, do_b, _TN)
            cum = PG + _dot01(G, before, passes=2)
            dz = jnp.where(valid, G * (1.0 - sig) - sig * cum, 0.0) * scale
            dk_ref[rows, :] += _dot(dz, q, _TN)
            return dq + _dot(dz, kj), PL + _rowsum(l1m), PG + _rowsum(G)

        zero = jnp.zeros((B, 1), F32)
        dq, _, _ = lax.fori_loop(0, i + 1, step, (jnp.zeros((B, HEAD_DIM), F32), zero, zero))
        dq_ref[...] = dq.astype(BF16)

    head = lambda off: pl.BlockSpec((T, HEAD_DIM), lambda h, i: (0, off + h))
    blk = pl.BlockSpec((B, HEAD_DIM), lambda h, i: (i, h))
    vec = pl.BlockSpec((1, HEAD_DIM), lambda h, i: (0, 0))
    return pl.pallas_call(
        body, name="sb_attention_bwd", grid=(H, T // B),
        in_specs=[blk, head(H), head(2 * H), vec, blk, blk, blk],
        out_specs=[blk, head(0), head(0), vec],
        out_shape=[_sds((T, H * HEAD_DIM), BF16), _sds((T, H * HEAD_DIM), F32), _sds((T, H * HEAD_DIM), F32),
                   _sds((1, HEAD_DIM), F32)],
        compiler_params=_params(("arbitrary", "arbitrary"), 8 * _nbytes((T, HEAD_DIM), F32) + 8 * 2**20),
    )(qkv, qkv, qkv, gain, o_raw, ltot, dmix)


def _adamw_math(w, g, m, v):
    m = ADAM_B1 * m + (1.0 - ADAM_B1) * g
    v = ADAM_B2 * v + (1.0 - ADAM_B2) * (g * g)
    m_hat = m / (1.0 - ADAM_B1 ** ADAM_STEP)
    v_hat = v / (1.0 - ADAM_B2 ** ADAM_STEP)
    delta = -ADAM_LR * (m_hat / (jnp.sqrt(v_hat) + ADAM_EPS) + ADAM_WD * w)
    return delta, m, v


def _adamw_sharded(parts, w, m, v, name):
    R, C = w.shape
    tr = _tile(R, max(SUBLANES, (2**20 // (4 * C)) // SUBLANES * SUBLANES), SUBLANES)

    def body(p_ref, w_ref, m_ref, v_ref, g_ref, d_ref, nm_ref, nv_ref):
        g = p_ref[0].astype(F32)
        for d in range(1, N_DEV):
            g = g + p_ref[d].astype(F32)
        g_ref[...] = g
        d_ref[...], nm_ref[...], nv_ref[...] = _adamw_math(w_ref[...], g, m_ref[...], v_ref[...])

    blk = pl.BlockSpec((tr, C), lambda i: (i, 0))
    return pl.pallas_call(
        body, name=name, grid=(R // tr,),
        in_specs=[pl.BlockSpec((N_DEV, tr, C), lambda i: (0, i, 0)), blk, blk, blk],
        out_specs=[blk] * 4, out_shape=[_sds((R, C), F32)] * 4,
        compiler_params=_params(("parallel",), 40 * 2**20),
    )(parts, w, m, v)


def _adamw_packed(g, w, m, v):
    def body(g_ref, w_ref, m_ref, v_ref, d_ref, nm_ref, nv_ref):
        d_ref[...], nm_ref[...], nv_ref[...] = _adamw_math(w_ref[...], g_ref[...], m_ref[...], v_ref[...])

    return pl.pallas_call(body, name="adamw_packed", out_shape=[_sds(g.shape, F32)] * 3,
                          compiler_params=_params((), 16 * 2**20))(g, w, m, v)


def _my_place():
    x, y, c = lax.axis_index("x"), lax.axis_index("y"), lax.axis_index("c")
    return x, y, c


def _peer(place, k):
    x, y, c = place
    return (1 - x if k & 4 else x, 1 - y if k & 2 else y, 1 - c if k & 1 else c)


def _index(place):
    x, y, c = place
    return 4 * x + 2 * y + c


HBM_SPEC = pl.BlockSpec(memory_space=pltpu.HBM)


def _all_gather(block, name):
    R, C = block.shape

    def body(x_ref, out_ref, send_sems, recv_sems, local_sem):
        me = _my_place()
        sibling = _peer(me, 1)
        chips = [2, 4, 6]

        def copy(sem, origin, to, src=None):
            slot = out_ref.at[_index(origin)]
            return pltpu.make_async_remote_copy(
                src_ref=slot if src is None else src, dst_ref=slot, send_sem=send_sems.at[sem], recv_sem=recv_sems.at[sem],
                device_id=to, device_id_type=MESH)

        mine = pltpu.make_async_copy(x_ref, out_ref.at[_index(me)], local_sem)
        mine.start()
        first = [copy(0, me, sibling, src=x_ref)] + [copy(1 + n, me, _peer(me, k), src=x_ref) for n, k in enumerate(chips)]
        for cp in first:
            cp.start()
        passed = [copy(4 + n, _peer(me, k), sibling) for n, k in enumerate(chips)]
        for n, k in enumerate(chips):
            copy(1 + n, _peer(me, k), me).wait_recv()
            passed[n].start()
        copy(0, sibling, me).wait_recv()
        for n, k in enumerate(chips):
            copy(4 + n, _peer(sibling, k), me).wait_recv()
        for cp in first + passed:
            cp.wait_send()
        mine.wait()

    return pl.pallas_call(
        body, name=name, in_specs=[HBM_SPEC], out_specs=HBM_SPEC,
        out_shape=_sds((N_DEV, R, C), block.dtype),
        scratch_shapes=[pltpu.SemaphoreType.DMA((7,)), pltpu.SemaphoreType.DMA((7,)), pltpu.SemaphoreType.DMA],
    )(block)


def _exchange(blocks, name):
    _, R, C = blocks.shape

    def body(x_ref, out_ref, send_sems, recv_sems, local_sem):
        me = _my_place()
        mine = pltpu.make_async_copy(x_ref.at[_index(me)], out_ref.at[_index(me)], local_sem)
        mine.start()
        copies = []
        for k in range(1, N_DEV):
            to = _peer(me, k)
            cp = pltpu.make_async_remote_copy(
                src_ref=x_ref.at[_index(to)], dst_ref=out_ref.at[_index(me)],
                send_sem=send_sems.at[k - 1], recv_sem=recv_sems.at[k - 1], device_id=to, device_id_type=MESH)
            cp.start()
            copies.append(cp)
        for k in range(1, N_DEV):
            frm = _peer(me, k)
            pltpu.make_async_remote_copy(
                src_ref=x_ref.at[_index(me)], dst_ref=out_ref.at[_index(frm)],
                send_sem=send_sems.at[k - 1], recv_sem=recv_sems.at[k - 1], device_id=frm, device_id_type=MESH).wait_recv()
        for cp in copies:
            cp.wait_send()
        mine.wait()

    return pl.pallas_call(
        body, name=name, in_specs=[HBM_SPEC], out_specs=HBM_SPEC,
        out_shape=_sds(blocks.shape, blocks.dtype),
        scratch_shapes=[pltpu.SemaphoreType.DMA((7,)), pltpu.SemaphoreType.DMA((7,)), pltpu.SemaphoreType.DMA],
    )(blocks)


def _all_reduce_packed(vec):
    R, L = vec.shape

    def body(x_ref, out_ref, buf, send_sems, recv_sems):
        me = _my_place()
        buf[_index(me)] = x_ref[...]
        copies = []
        for k in range(1, N_DEV):
            to = _peer(me, k)
            cp = pltpu.make_async_remote_copy(
                src_ref=x_ref, dst_ref=buf.at[_index(me)],
                send_sem=send_sems.at[k - 1], recv_sem=recv_sems.at[k - 1], device_id=to, device_id_type=MESH)
            cp.start()
            copies.append(cp)
        for k in range(1, N_DEV):
            frm = _peer(me, k)
            pltpu.make_async_remote_copy(
                src_ref=x_ref, dst_ref=buf.at[_index(frm)],
                send_sem=send_sems.at[k - 1], recv_sem=recv_sems.at[k - 1], device_id=frm, device_id_type=MESH).wait_recv()
        for cp in copies:
            cp.wait_send()
        acc = buf[0]
        for d in range(1, N_DEV):
            acc = acc + buf[d]
        out_ref[...] = acc

    vm = pl.BlockSpec(memory_space=pltpu.VMEM)
    return pl.pallas_call(
        body, name="all_reduce_packed", in_specs=[vm], out_specs=vm, out_shape=_sds((R, L), F32),
        scratch_shapes=[pltpu.VMEM((N_DEV, R, L), F32), pltpu.SemaphoreType.DMA((7,)), pltpu.SemaphoreType.DMA((7,))],
        compiler_params=pltpu.CompilerParams(vmem_limit_bytes=32 * 2**20),
    )(vec)


def _pack(arrays):
    rows = []
    for a in arrays:
        f = a.reshape(-1).astype(F32)
        pad = (-f.shape[0]) % LANES
        rows.append(jnp.pad(f, (0, pad)).reshape(-1, LANES))
    out = jnp.concatenate(rows, axis=0)
    return jnp.pad(out, ((0, (-out.shape[0]) % SUBLANES), (0, 0)))


def _unpack(packed, shapes):
    out, r = [], 0
    for s in shapes:
        n = math.prod(s)
        nr = -(-n // LANES)
        out.append(packed[r:r + nr].reshape(-1)[:n].reshape(s))
        r += nr
    return out


def _gather_cols(w_local, name):
    R, Cs = w_local.shape
    g = _all_gather(w_local.astype(BF16), name)
    return jnp.transpose(g, (1, 0, 2)).reshape(R, N_DEV * Cs)


def _gather_rows(w_local, name):
    Rs, C = w_local.shape
    return _all_gather(w_local.astype(BF16), name).reshape(N_DEV * Rs, C)


def _col_blocks(g):
    R, C = g.shape
    return jnp.transpose(g.astype(BF16).reshape(R, N_DEV, C // N_DEV), (1, 0, 2))


def _row_blocks(g):
    R, C = g.shape
    return g.astype(BF16).reshape(N_DEV, R // N_DEV, C)


def kernel(x, w_in, sb_out_gain, dn_conv_w, dn_a_log, dn_dt_bias, dn_out_gain, w_out, ln_mix_pre, ln_mix_post, w_up, ffn_conv_w, ffn_conv_b, w_down, ln_ffn_pre, ln_ffn_post, loss_target, m_w_in, m_sb_out_gain, m_dn_conv_w, m_dn_a_log, m_dn_dt_bias, m_dn_out_gain, m_w_out, m_ln_mix_pre, m_ln_mix_post, m_w_up, m_ffn_conv_w, m_ffn_conv_b, m_w_down, m_ln_ffn_pre, m_ln_ffn_post, v_w_in, v_sb_out_gain, v_dn_conv_w, v_dn_a_log, v_dn_dt_bias, v_dn_out_gain, v_w_out, v_ln_mix_pre, v_ln_mix_post, v_w_up, v_ffn_conv_w, v_ffn_conv_b, v_w_down, v_ln_ffn_pre, v_ln_ffn_post):
    T, D = x.shape[1], x.shape[2]
    SBW = SB_HEADS * HEAD_DIM
    DNW = DN_HEADS * HEAD_DIM
    in_cols = 3 * SBW + 4 * DNW + 2 * DN_HEADS
    main_cols = 3 * SBW + 4 * DNW
    in_pad = main_cols + LANES
    qkv0, z0 = 3 * SBW, 3 * SBW + 3 * DNW
    gate_block = main_cols // LANES
    x2, tgt = x[0], loss_target[0]

    w_in_f = _gather_cols(w_in[0], "gather_w_in")
    w_in_f = jnp.pad(w_in_f, ((0, 0), (0, in_pad - in_cols)))
    w_out_f = _gather_rows(w_out[0], "gather_w_out")
    w_up_f = _gather_cols(w_up[0], "gather_w_up")
    w_down_f = _gather_rows(w_down[0], "gather_w_down")
    small_w = _all_gather(_pack([dn_conv_w[0], ffn_conv_w[0]]), "gather_conv_w")
    parts = [_unpack(small_w[d], [dn_conv_w.shape[1:], ffn_conv_w.shape[1:]]) for d in range(N_DEV)]
    dn_cw = jnp.concatenate([p[0] for p in parts], axis=1)
    ffn_cw = jnp.concatenate([p[1] for p in parts], axis=1)
    lane_pad = lambda a, off: jnp.pad(a, ((0, 0), (off, LANES - off - a.shape[1])))
    a_log_l, dt_bias_l = lane_pad(dn_a_log, DN_HEADS), lane_pad(dn_dt_bias, DN_HEADS)

    xn = _norm_in(x2, ln_mix_pre)
    proj = _matmul(xn, w_in_f, "nn", F32, "proj_in", tn_cap=2432)
    sb_qkv = proj[:, :3 * SBW].astype(BF16)
    o_sb, mix_sb, sb_ltot = _sb_attention(sb_qkv, sb_out_gain)
    qn = _dn_branch(proj, qkv0, dn_cw, 0, True, HEAD_DIM ** -0.5)
    kn = _dn_branch(proj, qkv0 + DNW, dn_cw, DNW, True, 1.0)
    vn = _dn_branch(proj, qkv0 + 2 * DNW, dn_cw, 2 * DNW, False, 1.0)
    gc_full, beta_full = _dn_gates(proj, gate_block, a_log_l, dt_bias_l)
    o_dn, mix_dn, tm_all, s_all = _dn_scan(qn, kn, vn, gc_full, beta_full, proj, z0, dn_out_gain)
    mix = jnp.concatenate([mix_sb, mix_dn], axis=1)
    m = _matmul(mix, w_out_f, "nn", F32, "proj_out")
    h, hn = _mix_residual(x2, m, ln_mix_post, ln_ffn_pre)
    u = _matmul(hn, w_up_f, "nn", F32, "ffn_up")
    act = _ffn_act(u, ffn_cw, ffn_conv_b)
    f = _matmul(act, w_down_f, "nn", F32, "ffn_down")
    dy, df, d_ln_ffn_post, loss_part = _loss_head(h, f, ln_ffn_post, tgt)

    d_w_down = _matmul(act, df, "tn", F32, "grad_w_down")
    da = _matmul(df, w_down_f, "nt", F32, "bwd_ffn_down")
    du, d_ffn_cwb = _ffn_act_bwd(u, ffn_cw, ffn_conv_b, da)
    d_w_up = _matmul(hn, du, "tn", F32, "grad_w_up")
    dhn = _matmul(du, w_up_f, "nt", F32, "bwd_ffn_up")
    dh, dm, d_ln_ffn_pre, d_ln_mix_post = _ffn_residual_bwd(dy, dhn, h, ln_ffn_pre, m, ln_mix_post)

    d_w_out = _matmul(mix, dm, "tn", F32, "grad_w_out")
    dmix = _matmul(dm, w_out_f, "nt", F32, "bwd_proj_out")
    dq_sb, dk_sb, dv_sb, d_sb_gain = _sb_attention_bwd(sb_qkv, sb_out_gain, o_sb, sb_ltot, dmix)
    dqn, dkn, dvn, dgc_full, dbeta_full, dz, d_dn_gain = _dn_scan_bwd(
        qn, kn, vn, gc_full, beta_full, proj, z0, dn_out_gain, o_dn, tm_all, s_all, dmix, SBW)
    du_q, dcw_q = _dn_branch_bwd(proj, qkv0, dn_cw, 0, True, HEAD_DIM ** -0.5, dqn)
    du_k, dcw_k = _dn_branch_bwd(proj, qkv0 + DNW, dn_cw, DNW, True, 1.0, dkn)
    du_v, dcw_v = _dn_branch_bwd(proj, qkv0 + 2 * DNW, dn_cw, 2 * DNW, False, 1.0, dvn)
    dba, d_a_log_l, d_dt_bias_l = _dn_gates_bwd(proj, gate_block, a_log_l, dt_bias_l, dgc_full, dbeta_full)
    dproj = jnp.concatenate([dq_sb, dk_sb.astype(BF16), dv_sb.astype(BF16), du_q, du_k, du_v, dz, dba], axis=1)
    d_w_in = _matmul(xn, dproj, "tn", F32, "grad_w_in", tn_cap=2432)[:, :in_cols]
    dxn = _matmul(dproj, w_in_f, "nt", F32, "bwd_proj_in", tk_cap=384)
    grad_x, d_ln_mix_pre = _input_bwd(dh, dxn, x2, ln_mix_pre)

    d_dn_cw = jnp.concatenate([dcw_q[:SHORT_CONV], dcw_k[:SHORT_CONV], dcw_v[:SHORT_CONV]], axis=1)
    small = [loss_part[:, :1], d_sb_gain, d_a_log_l[:, DN_HEADS:2 * DN_HEADS], d_dt_bias_l[:, DN_HEADS:2 * DN_HEADS], d_dn_gain,
             d_ln_mix_pre, d_ln_mix_post, d_ffn_cwb[FFN_CONV:FFN_CONV + 1], d_ln_ffn_pre, d_ln_ffn_post,
             d_dn_cw, d_ffn_cwb[:FFN_CONV]]
    shapes = [a.shape for a in small]
    red = _unpack(_all_reduce_packed(_pack(small)), shapes)
    loss = red[0].reshape(())
    me = _index(_my_place())
    g_dn_cw = lax.dynamic_slice_in_dim(red[10], me * dn_conv_w.shape[2], dn_conv_w.shape[2], axis=1)
    g_ffn_cw = lax.dynamic_slice_in_dim(red[11], me * ffn_conv_w.shape[2], ffn_conv_w.shape[2], axis=1)
    names = ["sb_out_gain", "dn_conv_w", "dn_a_log", "dn_dt_bias", "dn_out_gain", "ln_mix_pre", "ln_mix_post",
             "ffn_conv_w", "ffn_conv_b", "ln_ffn_pre", "ln_ffn_post"]
    g_small = dict(sb_out_gain=red[1], dn_conv_w=g_dn_cw[None], dn_a_log=red[2], dn_dt_bias=red[3], dn_out_gain=red[4],
                   ln_mix_pre=red[5], ln_mix_post=red[6], ffn_conv_w=g_ffn_cw[None], ffn_conv_b=red[7],
                   ln_ffn_pre=red[8], ln_ffn_post=red[9])
    w_small = dict(sb_out_gain=sb_out_gain, dn_conv_w=dn_conv_w, dn_a_log=dn_a_log, dn_dt_bias=dn_dt_bias,
                   dn_out_gain=dn_out_gain, ln_mix_pre=ln_mix_pre, ln_mix_post=ln_mix_post, ffn_conv_w=ffn_conv_w,
                   ffn_conv_b=ffn_conv_b, ln_ffn_pre=ln_ffn_pre, ln_ffn_post=ln_ffn_post)
    m_small = dict(sb_out_gain=m_sb_out_gain, dn_conv_w=m_dn_conv_w, dn_a_log=m_dn_a_log, dn_dt_bias=m_dn_dt_bias,
                   dn_out_gain=m_dn_out_gain, ln_mix_pre=m_ln_mix_pre, ln_mix_post=m_ln_mix_post, ffn_conv_w=m_ffn_conv_w,
                   ffn_conv_b=m_ffn_conv_b, ln_ffn_pre=m_ln_ffn_pre, ln_ffn_post=m_ln_ffn_post)
    v_small = dict(sb_out_gain=v_sb_out_gain, dn_conv_w=v_dn_conv_w, dn_a_log=v_dn_a_log, dn_dt_bias=v_dn_dt_bias,
                   dn_out_gain=v_dn_out_gain, ln_mix_pre=v_ln_mix_pre, ln_mix_post=v_ln_mix_post, ffn_conv_w=v_ffn_conv_w,
                   ffn_conv_b=v_ffn_conv_b, ln_ffn_pre=v_ln_ffn_pre, ln_ffn_post=v_ln_ffn_post)
    sshapes = [w_small[n].shape for n in names]
    upd = _adamw_packed(_pack([g_small[n] for n in names]), _pack([w_small[n] for n in names]),
                        _pack([m_small[n] for n in names]), _pack([v_small[n] for n in names]))
    d_small, nm_small, nv_small = [dict(zip(names, _unpack(p, sshapes))) for p in upd]

    big = {}
    for n, g, blocks, w_, m_, v_ in [
            ("w_in", d_w_in, _col_blocks, w_in, m_w_in, v_w_in), ("w_out", d_w_out, _row_blocks, w_out, m_w_out, v_w_out),
            ("w_up", d_w_up, _col_blocks, w_up, m_w_up, v_w_up), ("w_down", d_w_down, _row_blocks, w_down, m_w_down, v_w_down)]:
        got = _exchange(blocks(g), "exchange_" + n)
        big[n] = [a[None] for a in _adamw_sharded(got, w_[0], m_[0], v_[0], "adamw_" + n)]

    order = ["w_in", "sb_out_gain", "dn_conv_w", "dn_a_log", "dn_dt_bias", "dn_out_gain", "w_out", "ln_mix_pre",
             "ln_mix_post", "w_up", "ffn_conv_w", "ffn_conv_b", "w_down", "ln_ffn_pre", "ln_ffn_post"]
    pick = lambda n, i: big[n][i] if n in big else [g_small, d_small, nm_small, nv_small][i][n].reshape(w_small[n].shape)
    return (loss, grad_x[None], *[pick(n, 0) for n in order], *[pick(n, 1) for n in order],
            *[pick(n, 2) for n in order], *[pick(n, 3) for n in order])
```

```python
import functools
import math

import jax
import jax.numpy as jnp
from jax import lax
from jax.experimental import pallas as pl
from jax.experimental.pallas import tpu as pltpu

F32 = jnp.float32
BF16 = jnp.bfloat16

N_DEV = 8
HEAD_DIM = 128
SB_HEADS = 8
DN_HEADS = 8
DN_CHUNK = 128
DN_GROUP = 8
INV_BLOCK = 16
Q_BLOCK = 128
SHORT_CONV = 4
FFN_CONV = 3
EPS = 1e-6
LANES = 128
SUBLANES = 8
VMEM_CAP = 56 * 2**20

ADAM_LR = 0.001
ADAM_B1 = 0.9
ADAM_B2 = 0.999
ADAM_EPS = 1e-08
ADAM_WD = 0.01
ADAM_STEP = 10

MESH = pl.DeviceIdType.MESH

assert Q_BLOCK == HEAD_DIM == DN_CHUNK == LANES


def _tile(n, cap, mult):
    if n <= cap:
        return n
    t = (cap // mult) * mult
    while t >= mult:
        if n % t == 0:
            return t
        t -= mult
    raise ValueError(f"no tile for {n} under {cap} in multiples of {mult}")


def _params(sem, vmem_bytes):
    limit = int(min(VMEM_CAP, max(vmem_bytes, 16 * 2**20)))
    if not sem:
        return pltpu.CompilerParams(vmem_limit_bytes=limit)
    return pltpu.CompilerParams(dimension_semantics=sem, vmem_limit_bytes=limit)


def _nbytes(shape, dtype):
    return math.prod(shape) * jnp.dtype(dtype).itemsize


_NN = (((1,), (0,)), ((), ()))
_NT = (((1,), (1,)), ((), ()))
_TN = (((0,), (0,)), ((), ()))


def _batched(dims, ndim):
    if ndim == 2:
        return dims
    (ca,), (cb,) = dims[0]
    return (((ca + 1,), (cb + 1,)), ((0,), (0,)))


def _dot(a, b, dims=_NN):
    return lax.dot_general(a.astype(BF16), b.astype(BF16), _batched(dims, a.ndim), preferred_element_type=F32)


def _split2(x):
    hi = x.astype(BF16)
    lo = (x - hi.astype(F32)).astype(BF16)
    return hi, lo


def _split3(x):
    hi = x.astype(BF16)
    r = x - hi.astype(F32)
    mid = r.astype(BF16)
    lo = (r - mid.astype(F32)).astype(BF16)
    return hi, mid, lo


def _dot01(x, m01, passes=3):
    parts = _split3(x) if passes == 3 else _split2(x)
    out = None
    for p in parts:
        t = lax.dot_general(p, m01, _NN, preferred_element_type=F32)
        out = t if out is None else out + t
    return out


def _dot01_left(m01, x, passes=3):
    parts = _split3(x) if passes == 3 else _split2(x)
    out = None
    for p in parts:
        t = lax.dot_general(m01, p, _NN, preferred_element_type=F32)
        out = t if out is None else out + t
    return out


def _mm3(a, b, dims=_NN):
    ah, al = _split2(a)
    bh, bl = _split2(b)
    d = functools.partial(lax.dot_general, dimension_numbers=_batched(dims, a.ndim), preferred_element_type=F32)
    return d(ah, bh) + (d(ah, bl) + d(al, bh))


def _rowsum(x):
    return jnp.sum(x, axis=-1, keepdims=True)


def _t(x):
    return jnp.swapaxes(x, -1, -2)


def _sigmoid(x):
    return 1.0 / (1.0 + jnp.exp(-x))


def _softplus(x):
    return jnp.maximum(x, 0.0) + jnp.log(1.0 + jnp.exp(-jnp.abs(x)))


def _silu(x):
    return x * _sigmoid(x)


def _silu_grad(x):
    s = _sigmoid(x)
    return s * (1.0 + x * (1.0 - s))


_GELU_C = math.sqrt(2.0 / math.pi)


def _gelu(x):
    return 0.5 * x * (1.0 + jnp.tanh(_GELU_C * (x + 0.044715 * x * x * x)))


def _gelu_grad(x):
    th = jnp.tanh(_GELU_C * (x + 0.044715 * x * x * x))
    return 0.5 * (1.0 + th) + 0.5 * x * (1.0 - th * th) * _GELU_C * (1.0 + 3.0 * 0.044715 * x * x)


def _rms(x, g):
    r = lax.rsqrt(jnp.mean(x * x, axis=-1, keepdims=True) + EPS)
    return x * r * g


def _rms_bwd(dy, x, g):
    r = lax.rsqrt(jnp.mean(x * x, axis=-1, keepdims=True) + EPS)
    xh = x * r
    gdy = dy * g
    dx = r * (gdy - xh * jnp.mean(gdy * xh, axis=-1, keepdims=True))
    return dx, jnp.sum(dy * xh, axis=-2, keepdims=True)


def _iota2(shape, axis):
    return lax.broadcasted_iota(jnp.int32, shape, axis)


def _shift_down(cur, prev8, k):
    n = cur.shape[0]
    r = pltpu.roll(cur, k, 0)
    pr = pltpu.roll(prev8, k, 0)
    head = jnp.where(_iota2(pr.shape, 0) < k, pr, r[0:SUBLANES])
    if n == SUBLANES:
        return head
    return jnp.concatenate([head, r[SUBLANES:]], axis=0)


def _shift_up(cur, next8, k):
    n = cur.shape[0]
    r = pltpu.roll(cur, n - k, 0)
    nr = pltpu.roll(next8, SUBLANES - k, 0)
    tail = jnp.where(_iota2(nr.shape, 0) >= SUBLANES - k, nr, r[n - SUBLANES:])
    if n == SUBLANES:
        return tail
    return jnp.concatenate([r[:n - SUBLANES], tail], axis=0)


def _causal_conv(cur, prev8, w_ref, taps):
    out = cur * w_ref[taps - 1:taps, :]
    for j in range(taps - 1):
        out = out + _shift_down(cur, prev8, taps - 1 - j) * w_ref[j:j + 1, :]
    return out


def _anti_conv(cur, next8, w_ref, taps):
    out = cur * w_ref[taps - 1:taps, :]
    for j in range(taps - 1):
        out = out + _shift_up(cur, next8, taps - 1 - j) * w_ref[j:j + 1, :]
    return out


def _matmul(a, b, mode, out_dtype, name, tm_cap=1024, tn_cap=1024, tk_cap=2048):
    if mode == "nn":
        (M, K), N = a.shape, b.shape[1]
    elif mode == "nt":
        (M, K), N = a.shape, b.shape[0]
    else:
        (K, M), N = a.shape, b.shape[1]
    tm = _tile(M, tm_cap, LANES)
    tn = _tile(N, tn_cap, LANES)
    tk = _tile(K, tk_cap, LANES)
    nk = K // tk
    dims = {"nn": _NN, "nt": _NT, "tn": _TN}[mode]
    a_spec = pl.BlockSpec((tk, tm), lambda i, j, k: (k, i)) if mode == "tn" else pl.BlockSpec((tm, tk), lambda i, j, k: (i, k))
    b_spec = pl.BlockSpec((tn, tk), lambda i, j, k: (j, k)) if mode == "nt" else pl.BlockSpec((tk, tn), lambda i, j, k: (k, j))

    def body(a_ref, b_ref, o_ref, acc_ref):
        k = pl.program_id(2)

        @pl.when(k == 0)
        def _():
            acc_ref[...] = jnp.zeros_like(acc_ref)

        acc_ref[...] += lax.dot_general(a_ref[...], b_ref[...], dims, preferred_element_type=F32)

        @pl.when(k == nk - 1)
        def _():
            o_ref[...] = acc_ref[...].astype(o_ref.dtype)

    vmem = 2 * (_nbytes((tm, tk), a.dtype) + _nbytes((tk, tn), b.dtype) + _nbytes((tm, tn), out_dtype)) + _nbytes((tm, tn), F32)
    return pl.pallas_call(
        body, name=name, grid=(M // tm, N // tn, nk),
        in_specs=[a_spec, b_spec], out_specs=pl.BlockSpec((tm, tn), lambda i, j, k: (i, j)),
        out_shape=jax.ShapeDtypeStruct((M, N), out_dtype),
        scratch_shapes=[pltpu.VMEM((tm, tn), F32)],
        compiler_params=_params(("parallel", "parallel", "arbitrary"), vmem + 4 * 2**20),
    )(a, b)


def _row_call(body, name, T, D, ins, outs, tr, acc_outs=()):
    def spec(a, kind):
        if kind == "row":
            return pl.BlockSpec((tr, a.shape[1]), lambda i: (i, 0))
        return pl.BlockSpec(a.shape, lambda i: (0, 0))
    in_specs = [spec(a, k) for a, k in ins]
    out_specs = [spec(a, k) for a, k in outs] + [spec(a, "vec") for a in acc_outs]
    out_shape = [a for a, _ in outs] + list(acc_outs)
    vmem = 2 * sum(_nbytes((tr, a.shape[1]) if k == "row" else a.shape, a.dtype) for a, k in list(ins) + list(outs))
    return pl.pallas_call(
        body, name=name, grid=(T // tr,), in_specs=in_specs, out_specs=out_specs, out_shape=out_shape,
        compiler_params=_params(("arbitrary",), 3 * vmem + 8 * 2**20),
    )(*[a for a, _ in ins])


def _sds(shape, dtype):
    return jax.ShapeDtypeStruct(shape, dtype)


def _accumulate(ref, val):
    @pl.when(pl.program_id(0) == 0)
    def _():
        ref[...] = jnp.zeros_like(ref)
    ref[...] += val


def _norm_in(x, g):
    T, D = x.shape

    def body(x_ref, g_ref, o_ref):
        o_ref[...] = _rms(x_ref[...], g_ref[...]).astype(BF16)

    return _row_call(body, "norm_in", T, D, [(x, "row"), (g, "vec")], [(_sds((T, D), BF16), "row")], _tile(T, 256, 16))[0]


def _mix_residual(x, m, g_post, g_pre):
    T, D = x.shape

    def body(x_ref, m_ref, gp_ref, gn_ref, h_ref, hn_ref):
        h = x_ref[...] + _rms(m_ref[...], gp_ref[...])
        h_ref[...] = h
        hn_ref[...] = _rms(h, gn_ref[...]).astype(BF16)

    return _row_call(body, "mix_residual", T, D, [(x, "row"), (m, "row"), (g_post, "vec"), (g_pre, "vec")],
                     [(_sds((T, D), F32), "row"), (_sds((T, D), BF16), "row")], _tile(T, 256, 16))


def _loss_head(h, f, g_post, target):
    T, D = h.shape

    def body(h_ref, f_ref, g_ref, t_ref, dy_ref, df_ref, dg_ref, loss_ref):
        f = f_ref[...]
        g = g_ref[...]
        diff = h_ref[...] + _rms(f, g) - t_ref[...]
        dy = diff * (1.0 / D)
        dy_ref[...] = dy
        df, dg = _rms_bwd(dy, f, g)
        df_ref[...] = df.astype(BF16)
        _accumulate(dg_ref, dg)
        _accumulate(loss_ref, jnp.full((1, LANES), 0.5 / D, F32) * jnp.sum(diff * diff))

    return _row_call(body, "loss_head", T, D, [(h, "row"), (f, "row"), (g_post, "vec"), (target, "row")],
                     [(_sds((T, D), F32), "row"), (_sds((T, D), BF16), "row")], _tile(T, 256, 16),
                     acc_outs=[_sds((1, D), F32), _sds((1, LANES), F32)])


def _ffn_residual_bwd(dy, dhn, h, g_pre, m, g_post):
    T, D = h.shape

    def body(dy_ref, dhn_ref, h_ref, gn_ref, m_ref, gp_ref, dh_ref, dm_ref, dgn_ref, dgp_ref):
        dhh, dgn = _rms_bwd(dhn_ref[...], h_ref[...], gn_ref[...])
        dh = dy_ref[...] + dhh
        dh_ref[...] = dh
        dm, dgp = _rms_bwd(dh, m_ref[...], gp_ref[...])
        dm_ref[...] = dm.astype(BF16)
        _accumulate(dgn_ref, dgn)
        _accumulate(dgp_ref, dgp)

    return _row_call(body, "ffn_residual_bwd", T, D,
                     [(dy, "row"), (dhn, "row"), (h, "row"), (g_pre, "vec"), (m, "row"), (g_post, "vec")],
                     [(_sds((T, D), F32), "row"), (_sds((T, D), BF16), "row")], _tile(T, 128, 16),
                     acc_outs=[_sds((1, D), F32), _sds((1, D), F32)])


def _input_bwd(dh, dxn, x, g):
    T, D = x.shape

    def body(dh_ref, dxn_ref, x_ref, g_ref, dx_ref, dg_ref):
        dx, dg = _rms_bwd(dxn_ref[...], x_ref[...], g_ref[...])
        dx_ref[...] = dh_ref[...] + dx
        _accumulate(dg_ref, dg)

    return _row_call(body, "input_bwd", T, D, [(dh, "row"), (dxn, "row"), (x, "row"), (g, "vec")],
                     [(_sds((T, D), F32), "row")], _tile(T, 256, 16), acc_outs=[_sds((1, D), F32)])


def _ffn_act(u, conv_w, conv_b):
    T, F2 = u.shape
    F = F2 // 2
    tc = _tile(F, 512, LANES)
    tr = _tile(T, 512, SUBLANES)
    nc = F // tc
    r8 = tr // SUBLANES

    def body(ug_ref, ugp_ref, uv_ref, uvp_ref, wg_ref, wv_ref, bg_ref, bv_ref, a_ref):
        first = pl.program_id(1) == 0
        cg = _causal_conv(ug_ref[...], jnp.where(first, 0.0, ugp_ref[...]), wg_ref, FFN_CONV) + bg_ref[...]
        cv = _causal_conv(uv_ref[...], jnp.where(first, 0.0, uvp_ref[...]), wv_ref, FFN_CONV) + bv_ref[...]
        a_ref[...] = (_gelu(cg) * cv).astype(BF16)

    cur = lambda off: pl.BlockSpec((tr, tc), lambda j, i: (i, j + off))
    prev = lambda off: pl.BlockSpec((SUBLANES, tc), lambda j, i: (jnp.maximum(i * r8 - 1, 0), j + off))
    wsp = lambda off: pl.BlockSpec((FFN_CONV, tc), lambda j, i: (0, j + off))
    bsp = lambda off: pl.BlockSpec((1, tc), lambda j, i: (0, j + off))
    return pl.pallas_call(
        body, name="ffn_act", grid=(nc, T // tr),
        in_specs=[cur(0), prev(0), cur(nc), prev(nc), wsp(0), wsp(nc), bsp(0), bsp(nc)],
        out_specs=pl.BlockSpec((tr, tc), lambda j, i: (i, j)),
        out_shape=_sds((T, F), BF16),
        compiler_params=_params(("parallel", "arbitrary"), 12 * _nbytes((tr, tc), F32) + 8 * 2**20),
    )(u, u, u, u, conv_w, conv_w, conv_b, conv_b)


def _ffn_act_bwd(u, conv_w, conv_b, da):
    T, F2 = u.shape
    F = F2 // 2
    tc = _tile(F, 512, LANES)
    tr = _tile(T, 512, SUBLANES)
    nc = F // tc
    r8 = tr // SUBLANES
    n8 = T // SUBLANES
    K = FFN_CONV

    def body(uo_ref, uop_ref, uon_ref, up_ref, upp_ref, upn_ref, da_ref, dan_ref,
             wo_ref, wp_ref, bo_ref, bp_ref, du_ref, dwb_ref):
        j = pl.program_id(0)
        i = pl.program_id(1)
        first = i == 0
        last = i == pl.num_programs(1) - 1
        is_gate = j < nc

        def dconv(uo, uo_prev, up, up_prev, da_):
            co = _causal_conv(uo, uo_prev, wo_ref, K) + bo_ref[...]
            cp = _causal_conv(up, up_prev, wp_ref, K) + bp_ref[...]
            return jnp.where(is_gate, da_ * cp * _gelu_grad(co), da_ * _gelu(cp))

        uo = uo_ref[...]
        uo_prev = jnp.where(first, 0.0, uop_ref[...])
        dc = dconv(uo, uo_prev, up_ref[...], jnp.where(first, 0.0, upp_ref[...]), da_ref[...])
        dcn = dconv(uon_ref[...], uo[tr - SUBLANES:], upn_ref[...], up_ref[tr - SUBLANES:, :], dan_ref[...])
        dcn = jnp.where(last, 0.0, dcn)
        du_ref[...] = _anti_conv(dc, dcn, wo_ref, K).astype(BF16)
        rows = [jnp.sum(dc * _shift_down(uo, uo_prev, K - 1 - t), axis=0, keepdims=True) for t in range(K - 1)]
        rows += [jnp.sum(dc * uo, axis=0, keepdims=True), jnp.sum(dc, axis=0, keepdims=True)]
        rows += [jnp.zeros_like(rows[0])] * (SUBLANES - len(rows))
        upd = jnp.concatenate(rows, axis=0)

        @pl.when(first)
        def _():
            dwb_ref[...] = jnp.zeros_like(dwb_ref)
        dwb_ref[...] += upd

    part = lambda j: (j + nc) % (2 * nc)
    cur = lambda f: pl.BlockSpec((tr, tc), lambda j, i: (i, f(j)))
    prev = lambda f: pl.BlockSpec((SUBLANES, tc), lambda j, i: (jnp.maximum(i * r8 - 1, 0), f(j)))
    nxt = lambda f: pl.BlockSpec((SUBLANES, tc), lambda j, i: (jnp.minimum((i + 1) * r8, n8 - 1), f(j)))
    own = lambda j: j
    dac = lambda j: j % nc
    wsp = lambda f: pl.BlockSpec((K, tc), lambda j, i: (0, f(j)))
    bsp = lambda f: pl.BlockSpec((1, tc), lambda j, i: (0, f(j)))
    return pl.pallas_call(
        body, name="ffn_act_bwd", grid=(2 * nc, T // tr),
        in_specs=[cur(own), prev(own), nxt(own), cur(part), prev(part), nxt(part), cur(dac), nxt(dac),
                  wsp(own), wsp(part), bsp(own), bsp(part)],
        out_specs=[pl.BlockSpec((tr, tc), lambda j, i: (i, j)), pl.BlockSpec((SUBLANES, tc), lambda j, i: (0, j))],
        out_shape=[_sds((T, F2), BF16), _sds((SUBLANES, F2), F32)],
        compiler_params=_params(("parallel", "arbitrary"), 16 * _nbytes((tr, tc), F32) + 8 * 2**20),
    )(u, u, u, u, u, u, da, da, conv_w, conv_w, conv_b, conv_b)


def _l2norm(s, scale):
    return s * (lax.rsqrt(jnp.sum(s * s, axis=-1, keepdims=True) + EPS) * scale)


def _dn_branch(proj, col0, conv_w, wcol0, l2, scale):
    T = proj.shape[0]
    W = DN_HEADS * HEAD_DIM
    tr = _tile(T, 512, SUBLANES)
    r8 = tr // SUBLANES
    cb0, wb0 = col0 // HEAD_DIM, wcol0 // HEAD_DIM

    def body(u_ref, up_ref, w_ref, o_ref):
        first = pl.program_id(1) == 0
        s = _silu(_causal_conv(u_ref[...], jnp.where(first, 0.0, up_ref[...]), w_ref, SHORT_CONV))
        o_ref[...] = _l2norm(s, scale) if l2 else s

    return pl.pallas_call(
        body, name=f"dn_branch_{col0}", grid=(DN_HEADS, T // tr),
        in_specs=[pl.BlockSpec((tr, HEAD_DIM), lambda h, i: (i, cb0 + h)),
                  pl.BlockSpec((SUBLANES, HEAD_DIM), lambda h, i: (jnp.maximum(i * r8 - 1, 0), cb0 + h)),
                  pl.BlockSpec((SHORT_CONV, HEAD_DIM), lambda h, i: (0, wb0 + h))],
        out_specs=pl.BlockSpec((tr, HEAD_DIM), lambda h, i: (i, h)),
        out_shape=_sds((T, W), F32),
        compiler_params=_params(("parallel", "arbitrary"), 16 * 2**20),
    )(proj, proj, conv_w)


def _dn_branch_bwd(proj, col0, conv_w, wcol0, l2, scale, dy):
    T = proj.shape[0]
    W = DN_HEADS * HEAD_DIM
    tr = _tile(T, 512, SUBLANES)
    r8 = tr // SUBLANES
    n8 = T // SUBLANES
    cb0, wb0 = col0 // HEAD_DIM, wcol0 // HEAD_DIM
    K = SHORT_CONV

    def body(u_ref, up_ref, un_ref, dy_ref, dyn_ref, w_ref, du_ref, dw_ref):
        i = pl.program_id(1)
        first = i == 0
        last = i == pl.num_programs(1) - 1

        def dconv(u, u_prev, dy_):
            c = _causal_conv(u, u_prev, w_ref, K)
            if l2:
                s = _silu(c)
                r = lax.rsqrt(jnp.sum(s * s, axis=-1, keepdims=True) + EPS)
                n = s * r
                ds = (scale * r) * (dy_ - n * jnp.sum(dy_ * n, axis=-1, keepdims=True))
            else:
                ds = dy_
            return ds * _silu_grad(c)

        u = u_ref[...]
        u_prev = jnp.where(first, 0.0, up_ref[...])
        dc = dconv(u, u_prev, dy_ref[...])
        dcn = jnp.where(last, 0.0, dconv(un_ref[...], u[tr - SUBLANES:], dyn_ref[...]))
        du_ref[...] = _anti_conv(dc, dcn, w_ref, K).astype(BF16)
        rows = [jnp.sum(dc * _shift_down(u, u_prev, K - 1 - t), axis=0, keepdims=True) for t in range(K - 1)]
        rows += [jnp.sum(dc * u, axis=0, keepdims=True)]
        rows += [jnp.zeros_like(rows[0])] * (SUBLANES - len(rows))
        upd = jnp.concatenate(rows, axis=0)

        @pl.when(first)
        def _():
            dw_ref[...] = jnp.zeros_like(dw_ref)
        dw_ref[...] += upd

    return pl.pallas_call(
        body, name=f"dn_branch_bwd_{col0}", grid=(DN_HEADS, T // tr),
        in_specs=[pl.BlockSpec((tr, HEAD_DIM), lambda h, i: (i, cb0 + h)),
                  pl.BlockSpec((SUBLANES, HEAD_DIM), lambda h, i: (jnp.maximum(i * r8 - 1, 0), cb0 + h)),
                  pl.BlockSpec((SUBLANES, HEAD_DIM), lambda h, i: (jnp.minimum((i + 1) * r8, n8 - 1), cb0 + h)),
                  pl.BlockSpec((tr, HEAD_DIM), lambda h, i: (i, h)),
                  pl.BlockSpec((SUBLANES, HEAD_DIM), lambda h, i: (jnp.minimum((i + 1) * r8, n8 - 1), h)),
                  pl.BlockSpec((K, HEAD_DIM), lambda h, i: (0, wb0 + h))],
        out_specs=[pl.BlockSpec((tr, HEAD_DIM), lambda h, i: (i, h)),
                   pl.BlockSpec((SUBLANES, HEAD_DIM), lambda h, i: (0, h))],
        out_shape=[_sds((T, W), BF16), _sds((SUBLANES, W), F32)],
        compiler_params=_params(("parallel", "arbitrary"), 16 * 2**20),
    )(proj, proj, proj, dy, dy, conv_w)


def _lane_masks(shape):
    lane = _iota2(shape, 1)
    return lane < DN_HEADS, (lane >= DN_HEADS) & (lane < 2 * DN_HEADS)


def _expand01(off):
    r = _iota2((LANES, DN_HEADS * HEAD_DIM), 0)
    c = _iota2((LANES, DN_HEADS * HEAD_DIM), 1)
    return (r == jnp.right_shift(c, int(math.log2(HEAD_DIM))) + off).astype(BF16)


def _select01(off):
    r = _iota2((DN_HEADS * HEAD_DIM, LANES), 0)
    c = _iota2((DN_HEADS * HEAD_DIM, LANES), 1)
    return (r == (c - off) * HEAD_DIM).astype(BF16)


def _dn_gates(proj, gate_block, a_log_l, dt_bias_l):
    T = proj.shape[0]
    C = DN_CHUNK
    W = DN_HEADS * HEAD_DIM

    def body(ba_ref, al_ref, dt_ref, gc_ref, beta_ref):
        ba = ba_ref[...]
        is_b, is_a = _lane_masks(ba.shape)
        g = jnp.where(is_a, -jnp.exp(al_ref[...]) * _softplus(ba + dt_ref[...]), 0.0)
        beta = jnp.where(is_b, _sigmoid(ba), 0.0)
        tri = (_iota2((C, C), 0) >= _iota2((C, C), 1)).astype(BF16)
        gc = _dot01_left(tri, g)
        gc_ref[...] = _dot01(gc, _expand01(DN_HEADS))
        beta_ref[...] = _dot01(beta, _expand01(0))

    vec = pl.BlockSpec((1, LANES), lambda n: (0, 0))
    return pl.pallas_call(
        body, name="dn_gates", grid=(T // C,),
        in_specs=[pl.BlockSpec((C, LANES), lambda n: (n, gate_block)), vec, vec],
        out_specs=[pl.BlockSpec((C, W), lambda n: (n, 0))] * 2,
        out_shape=[_sds((T, W), F32)] * 2,
        compiler_params=_params(("parallel",), 16 * 2**20),
    )(proj, a_log_l, dt_bias_l)


def _dn_gates_bwd(proj, gate_block, a_log_l, dt_bias_l, dgc_full, dbeta_full):
    T = proj.shape[0]
    C = DN_CHUNK
    W = DN_HEADS * HEAD_DIM

    def body(ba_ref, al_ref, dt_ref, dgc_ref, dbeta_ref, dba_ref, dal_ref, ddt_ref):
        ba = ba_ref[...]
        is_b, is_a = _lane_masks(ba.shape)
        ea = jnp.exp(al_ref[...])
        pre = ba + dt_ref[...]
        g = jnp.where(is_a, -ea * _softplus(pre), 0.0)
        beta = _sigmoid(ba)
        dgc = _dot01(dgc_ref[...], _select01(DN_HEADS))
        dbeta = _dot01(dbeta_ref[...], _select01(0))
        triu = (_iota2((C, C), 0) <= _iota2((C, C), 1)).astype(BF16)
        dg = _dot01_left(triu, dgc)
        da = jnp.where(is_a, dg * (-ea) * _sigmoid(pre), 0.0)
        dba_ref[...] = (da + jnp.where(is_b, dbeta * beta * (1.0 - beta), 0.0)).astype(BF16)
        _accumulate(dal_ref, jnp.sum(dg * g, axis=0, keepdims=True))
        _accumulate(ddt_ref, jnp.sum(da, axis=0, keepdims=True))

    vec = pl.BlockSpec((1, LANES), lambda n: (0, 0))
    full = pl.BlockSpec((C, W), lambda n: (n, 0))
    return pl.pallas_call(
        body, name="dn_gates_bwd", grid=(T // C,),
        in_specs=[pl.BlockSpec((C, LANES), lambda n: (n, gate_block)), vec, vec, full, full],
        out_specs=[pl.BlockSpec((C, LANES), lambda n: (n, 0)), vec, vec],
        out_shape=[_sds((T, LANES), BF16), _sds((1, LANES), F32), _sds((1, LANES), F32)],
        compiler_params=_params(("arbitrary",), 16 * 2**20),
    )(proj, a_log_l, dt_bias_l, dgc_full, dbeta_full)


def _unit_lower_inverse(L):
    C = L.shape[-1]
    row, col = _iota2((C, C), 0), _iota2((C, C), 1)
    eye = (row == col).astype(F32)
    sh = int(math.log2(INV_BLOCK))
    Ld = jnp.where(jnp.right_shift(row, sh) == jnp.right_shift(col, sh), L, 0.0)
    Lo = L - Ld
    X = eye - Ld
    P = Ld
    for _ in range(int(math.log2(INV_BLOCK)) - 1):
        P = _mm3(P, P)
        X = X + _mm3(X, P)
    N = _mm3(X, Lo)
    Y = eye - N
    P = N
    for _ in range(int(math.log2(C // INV_BLOCK)) - 1):
        P = _mm3(P, P)
        Y = Y + _mm3(Y, P)
    return _mm3(Y, X)


def _dn_chunk_common(q, k, v, gc, beta, gl):
    C = q.shape[-2]
    row, col = _iota2((C, C), 0), _iota2((C, C), 1)
    causal, strict = row >= col, row > col
    eg = jnp.exp(gc)
    decay = jnp.where(causal, jnp.exp(jnp.where(causal, gc - _t(gc), 0.0)), 0.0)
    kb, vb = k * beta, v * beta
    L = jnp.where(strict, _dot(kb, k, _NT) * decay, 0.0)
    Aqk = jnp.where(causal, _dot(q, k, _NT) * decay, 0.0)
    ektg = jnp.exp(gl - gc)
    return dict(causal=causal, strict=strict, eg=eg, decay=decay, kb=kb, vb=vb, L=L, Aqk=Aqk, ektg=ektg,
                kbg=kb * eg, kte=k * ektg, qd=q * eg, egl=jnp.exp(gl))


def _dn_scan(qn, kn, vn, gc_full, beta_full, proj, z_col0, gain):
    T, W = qn.shape
    C = DN_CHUNK
    N = T // C
    H = DN_HEADS
    G = DN_GROUP
    GW = G * HEAD_DIM
    zb0 = z_col0 // GW

    def body(q_ref, k_ref, v_ref, gc_ref, beta_ref, z_ref, gain_ref, o_ref, mix_ref, tm_ref, s_ref, S):
        @pl.when(pl.program_id(1) == 0)
        def _():
            S[...] = jnp.zeros_like(S)

        heads = lambda ref, rows=slice(None): jnp.stack([ref[rows, g * HEAD_DIM:(g + 1) * HEAD_DIM] for g in range(G)])
        q, k, v, gc, beta = heads(q_ref), heads(k_ref), heads(v_ref), heads(gc_ref), heads(beta_ref)
        gl = heads(gc_ref, slice(C - 1, C))
        c = _dn_chunk_common(q, k, v, gc, beta, gl)
        Tm = _unit_lower_inverse(c["L"])
        u = _dot(Tm, c["vb"])
        w = _dot(Tm, c["kbg"])
        S0 = S[...]
        vnew = u - _dot(w, S0)
        o = _dot(c["qd"], S0) + _dot(c["Aqk"], vnew)
        S[...] = S0 * c["egl"] + _dot(c["kte"], vnew, _TN)
        tm_ref[...] = Tm
        s_ref[...] = S0
        mix = (_rms(o, gain_ref[...]) * _silu(heads(z_ref))).astype(BF16)
        for g in range(G):
            sl = slice(g * HEAD_DIM, (g + 1) * HEAD_DIM)
            o_ref[:, sl] = o[g]
            mix_ref[:, sl] = mix[g]

    blk = pl.BlockSpec((C, GW), lambda h, n: (n, h))
    mat = pl.BlockSpec((G, None, C, C), lambda h, n: (h, n, 0, 0))
    return pl.pallas_call(
        body, name="dn_scan", grid=(H // G, N),
        in_specs=[blk, blk, blk, blk, blk, pl.BlockSpec((C, GW), lambda h, n: (n, zb0 + h)),
                  pl.BlockSpec((1, HEAD_DIM), lambda h, n: (0, 0))],
        out_specs=[blk, blk, mat, mat],
        out_shape=[_sds((T, W), F32), _sds((T, W), BF16), _sds((H, N, C, C), F32), _sds((H, N, C, C), F32)],
        scratch_shapes=[pltpu.VMEM((G, HEAD_DIM, HEAD_DIM), F32)],
        compiler_params=_params(("parallel", "arbitrary"), 32 * 2**20),
    )(qn, kn, vn, gc_full, beta_full, proj, gain)


def _dn_scan_bwd(qn, kn, vn, gc_full, beta_full, proj, z_col0, gain, o_raw, tm_all, s_all, dmix, dmix_col0):
    T, W = qn.shape
    C = DN_CHUNK
    N = T // C
    H = DN_HEADS
    G = DN_GROUP
    GW = G * HEAD_DIM
    zb0 = z_col0 // GW
    mb0 = dmix_col0 // GW

    def body(q_ref, k_ref, v_ref, gc_ref, beta_ref, z_ref, gain_ref, o_ref, tm_ref, s_ref, dmix_ref,
             dq_ref, dk_ref, dv_ref, dgc_ref, dbeta_ref, dz_ref, dgain_ref, dS):
        @pl.when(pl.program_id(1) == 0)
        def _():
            dS[...] = jnp.zeros_like(dS)

        @pl.when((pl.program_id(0) == 0) & (pl.program_id(1) == 0))
        def _():
            dgain_ref[...] = jnp.zeros_like(dgain_ref)

        heads = lambda ref, rows=slice(None): jnp.stack([ref[rows, g * HEAD_DIM:(g + 1) * HEAD_DIM] for g in range(G)])
        total = lambda x: jnp.sum(jnp.sum(x, axis=-1, keepdims=True), axis=-2, keepdims=True)
        gain = gain_ref[...]
        o, z, dmix = heads(o_ref), heads(z_ref), heads(dmix_ref)
        dz = (dmix * _rms(o, gain) * _silu_grad(z)).astype(BF16)
        do, dgain = _rms_bwd(dmix * _silu(z), o, gain)
        dgain_ref[...] += jnp.sum(dgain, axis=0)

        q, k, v, gc, beta = heads(q_ref), heads(k_ref), heads(v_ref), heads(gc_ref), heads(beta_ref)
        gl = heads(gc_ref, slice(C - 1, C))
        c = _dn_chunk_common(q, k, v, gc, beta, gl)
        Tm, S0, dS1 = tm_ref[...], s_ref[...], dS[...]
        w = _dot(Tm, c["kbg"])
        vnew = _dot(Tm, c["vb"]) - _dot(w, S0)

        dvnew = _dot(c["Aqk"], do, _TN) + _dot(c["kte"], dS1)
        dAqk = jnp.where(c["causal"], _dot(do, vnew, _NT), 0.0)
        dqd = _dot(do, S0, _NT)
        dkte = _dot(vnew, dS1, _NT)
        dgl = total(dS1 * S0) * c["egl"]
        dw = -_dot(dvnew, S0, _NT)
        dS[...] = dS1 * c["egl"] + _dot(c["qd"], do, _TN) - _dot(w, dvnew, _TN)

        dTm = _dot(dvnew, c["vb"], _NT) + _dot(dw, c["kbg"], _NT)
        dvb = _dot(Tm, dvnew, _TN)
        dkbg = _dot(Tm, dw, _TN)
        dL = jnp.where(c["strict"], -_mm3(_mm3(Tm, dTm, _TN), Tm, _NT), 0.0)
        dP = dL * c["decay"]
        dQ = dAqk * c["decay"]
        M = dL * c["L"] + dAqk * c["Aqk"]
        dkb = _dot(dP, k) + dkbg * c["eg"]
        dk = _dot(dP, c["kb"], _TN) + _dot(dQ, q, _TN) + dkte * c["ektg"] + dkb * beta
        dq = _dot(dQ, k) + dqd * c["eg"]
        tk = _rowsum(dkte * c["kte"])
        dgc = (_rowsum(M) - _rowsum(_t(M)) + _rowsum(dqd * c["qd"]) - tk + _rowsum(dkbg * c["kbg"]))
        dgl = dgl + total(tk)
        dgc = jnp.broadcast_to(dgc, q.shape) + jnp.where(_iota2((C, HEAD_DIM), 0) == C - 1, dgl, 0.0)
        dv = dvb * beta
        dbeta = jnp.broadcast_to(_rowsum(dkb * k) + _rowsum(dvb * v), q.shape)
        for g in range(G):
            sl = slice(g * HEAD_DIM, (g + 1) * HEAD_DIM)
            dz_ref[:, sl] = dz[g]
            dq_ref[:, sl] = dq[g]
            dk_ref[:, sl] = dk[g]
            dv_ref[:, sl] = dv[g]
            dgc_ref[:, sl] = dgc[g]
            dbeta_ref[:, sl] = dbeta[g]

    rev = lambda off: pl.BlockSpec((C, GW), lambda h, n: (N - 1 - n, off + h))
    mat = pl.BlockSpec((G, None, C, C), lambda h, n: (h, N - 1 - n, 0, 0))
    vec = pl.BlockSpec((1, HEAD_DIM), lambda h, n: (0, 0))
    return pl.pallas_call(
        body, name="dn_scan_bwd", grid=(H // G, N),
        in_specs=[rev(0), rev(0), rev(0), rev(0), rev(0), rev(zb0), vec, rev(0), mat, mat, rev(mb0)],
        out_specs=[rev(0)] * 6 + [vec],
        out_shape=[_sds((T, W), F32)] * 5 + [_sds((T, W), BF16), _sds((1, HEAD_DIM), F32)],
        scratch_shapes=[pltpu.VMEM((G, HEAD_DIM, HEAD_DIM), F32)],
        compiler_params=_params(("arbitrary", "arbitrary"), 40 * 2**20),
    )(qn, kn, vn, gc_full, beta_full, proj, gain, o_raw, tm_all, s_all, dmix)


def _sb_block(q, kj, ahead, first_key):
    z = _dot(q, kj, _NT) * (HEAD_DIM ** -0.5)
    valid = ahead < -first_key
    lb = jnp.minimum(z, 0.0) - jnp.log(1.0 + jnp.exp(-jnp.abs(z)))
    return valid, lb, jnp.where(valid, lb - z, 0.0)


def _sb_attention(qkv, gain, tq_cap=512):
    T = qkv.shape[0]
    H = SB_HEADS
    B = Q_BLOCK
    TQ = _tile(T, tq_cap, B)
    per = TQ // B

    def body(q_ref, k_ref, v_ref, gain_ref, o_ref, mix_ref, ltot_ref):
        i = pl.program_id(1)
        q = q_ref[...]
        upper = (_iota2((B, B), 0) > _iota2((B, B), 1)).astype(BF16)
        ahead = _iota2((TQ, B), 1) - _iota2((TQ, B), 0)
        last = (i + 1) * per - 1

        def step(jj, carry):
            acc, R = carry
            j = last - jj
            rows = pl.ds(pl.multiple_of(j * B, B), B)
            valid, lb, l1m = _sb_block(q, k_ref[rows, :], ahead, j * B - i * TQ)
            att = jnp.where(valid, jnp.exp(lb + R + _dot01(l1m, upper, passes=2)), 0.0)
            return acc + _dot(att, v_ref[rows, :]), R + _rowsum(l1m)

        acc, R = lax.fori_loop(0, last + 1, step, (jnp.zeros((TQ, HEAD_DIM), F32), jnp.zeros((TQ, 1), F32)))
        o_ref[...] = acc
        mix_ref[...] = _rms(acc, gain_ref[...]).astype(BF16)
        ltot_ref[...] = jnp.broadcast_to(R, (TQ, HEAD_DIM))

    head = lambda off: pl.BlockSpec((T, HEAD_DIM), lambda h, i: (0, off + h))
    blk = pl.BlockSpec((TQ, HEAD_DIM), lambda h, i: (i, h))
    return pl.pallas_call(
        body, name="sb_attention", grid=(H, T // TQ),
        in_specs=[blk, head(H), head(2 * H), pl.BlockSpec((1, HEAD_DIM), lambda h, i: (0, 0))],
        out_specs=[blk, blk, blk],
        out_shape=[_sds((T, H * HEAD_DIM), F32), _sds((T, H * HEAD_DIM), BF16), _sds((T, H * HEAD_DIM), F32)],
        compiler_params=_params(("parallel", "arbitrary"), 8 * _nbytes((T, HEAD_DIM), BF16) + 8 * 2**20),
    )(qkv, qkv, qkv, gain)


def _sb_attention_bwd(qkv, gain, o_raw, ltot, dmix, tq_cap=512):
    T = qkv.shape[0]
    H = SB_HEADS
    B = Q_BLOCK
    TQ = _tile(T, tq_cap, B)
    per = TQ // B
    scale = HEAD_DIM ** -0.5

    def body(q_ref, k_ref, v_ref, gain_ref, o_ref, ltot_ref, dmix_ref, dq_ref, dk_ref, dv_ref, dgain_ref):
        i = pl.program_id(1)

        @pl.when(i == 0)
        def _():
            dk_ref[...] = jnp.zeros_like(dk_ref)
            dv_ref[...] = jnp.zeros_like(dv_ref)

        @pl.when((pl.program_id(0) == 0) & (i == 0))
        def _():
            dgain_ref[...] = jnp.zeros_like(dgain_ref)

        q = q_ref[...]
        o = o_ref[...]
        do, dgain = _rms_bwd(dmix_ref[...], o, gain_ref[...])
        dgain_ref[...] += dgain
        ltot = ltot_ref[...]
        do_b = do.astype(BF16)
        upto = (_iota2((B, B), 0) <= _iota2((B, B), 1)).astype(BF16)
        before = (_iota2((B, B), 0) < _iota2((B, B), 1)).astype(BF16)
        ahead = _iota2((TQ, B), 1) - _iota2((TQ, B), 0)

        def step(j, carry):
            dq, PL, PG = carry
            rows = pl.ds(pl.multiple_of(j * B, B), B)
            kj = k_ref[rows, :]
            valid, lb, l1m = _sb_block(q, kj, ahead, j * B - i * TQ)
            att = jnp.where(valid, jnp.exp(lb + (ltot - PL - _dot01(l1m, upto, passes=2))), 0.0)
            sig = jnp.exp(lb)
            G = _dot(do_b, v_ref[rows, :], _NT) * att
            dv_ref[rows, :] += _dot(att, do_b, _TN)
            cum = PG + _dot01(G, before, passes=2)
            dz = jnp.where(valid, G * (1.0 - sig) - sig * cum, 0.0) * scale
            dk_ref[rows, :] += _dot(dz, q, _TN)
            return dq + _dot(dz, kj), PL + _rowsum(l1m), PG + _rowsum(G)

        zero = jnp.zeros((TQ, 1), F32)
        dq, _, _ = lax.fori_loop(0, (i + 1) * per, step, (jnp.zeros((TQ, HEAD_DIM), F32), zero, zero))
        dq_ref[...] = dq.astype(BF16)

    head = lambda off: pl.BlockSpec((T, HEAD_DIM), lambda h, i: (0, off + h))
    blk = pl.BlockSpec((TQ, HEAD_DIM), lambda h, i: (i, h))
    vec = pl.BlockSpec((1, HEAD_DIM), lambda h, i: (0, 0))
    return pl.pallas_call(
        body, name="sb_attention_bwd", grid=(H, T // TQ),
        in_specs=[blk, head(H), head(2 * H), vec, blk, blk, blk],
        out_specs=[blk, head(0), head(0), vec],
        out_shape=[_sds((T, H * HEAD_DIM), BF16), _sds((T, H * HEAD_DIM), F32), _sds((T, H * HEAD_DIM), F32),
                   _sds((1, HEAD_DIM), F32)],
        compiler_params=_params(("arbitrary", "arbitrary"), 8 * _nbytes((T, HEAD_DIM), F32) + 8 * 2**20),
    )(qkv, qkv, qkv, gain, o_raw, ltot, dmix)


def _adamw_math(w, g, m, v):
    m = ADAM_B1 * m + (1.0 - ADAM_B1) * g
    v = ADAM_B2 * v + (1.0 - ADAM_B2) * (g * g)
    m_hat = m / (1.0 - ADAM_B1 ** ADAM_STEP)
    v_hat = v / (1.0 - ADAM_B2 ** ADAM_STEP)
    delta = -ADAM_LR * (m_hat / (jnp.sqrt(v_hat) + ADAM_EPS) + ADAM_WD * w)
    return delta, m, v


def _adamw_sharded(parts, w, m, v, name):
    R, C = w.shape
    tr = _tile(R, max(SUBLANES, (2**20 // (4 * C)) // SUBLANES * SUBLANES), SUBLANES)

    def body(p_ref, w_ref, m_ref, v_ref, g_ref, d_ref, nm_ref, nv_ref):
        g = p_ref[0].astype(F32)
        for d in range(1, N_DEV):
            g = g + p_ref[d].astype(F32)
        g_ref[...] = g
        d_ref[...], nm_ref[...], nv_ref[...] = _adamw_math(w_ref[...], g, m_ref[...], v_ref[...])

    blk = pl.BlockSpec((tr, C), lambda i: (i, 0))
    return pl.pallas_call(
        body, name=name, grid=(R // tr,),
        in_specs=[pl.BlockSpec((N_DEV, tr, C), lambda i: (0, i, 0)), blk, blk, blk],
        out_specs=[blk] * 4, out_shape=[_sds((R, C), F32)] * 4,
        compiler_params=_params(("parallel",), 40 * 2**20),
    )(parts, w, m, v)


def _adamw_packed(g, w, m, v):
    def body(g_ref, w_ref, m_ref, v_ref, d_ref, nm_ref, nv_ref):
        d_ref[...], nm_ref[...], nv_ref[...] = _adamw_math(w_ref[...], g_ref[...], m_ref[...], v_ref[...])

    return pl.pallas_call(body, name="adamw_packed", out_shape=[_sds(g.shape, F32)] * 3,
                          compiler_params=_params((), 16 * 2**20))(g, w, m, v)


def _my_place():
    x, y, c = lax.axis_index("x"), lax.axis_index("y"), lax.axis_index("c")
    return x, y, c


def _peer(place, k):
    x, y, c = place
    return (1 - x if k & 4 else x, 1 - y if k & 2 else y, 1 - c if k & 1 else c)


def _index(place):
    x, y, c = place
    return 4 * x + 2 * y + c


HBM_SPEC = pl.BlockSpec(memory_space=pltpu.HBM)


def _all_gather(block, name):
    R, C = block.shape

    def body(x_ref, out_ref, send_sems, recv_sems, local_sem):
        me = _my_place()
        sibling = _peer(me, 1)
        chips = [2, 4, 6]

        def copy(sem, origin, to, src=None):
            slot = out_ref.at[_index(origin)]
            return pltpu.make_async_remote_copy(
                src_ref=slot if src is None else src, dst_ref=slot, send_sem=send_sems.at[sem], recv_sem=recv_sems.at[sem],
                device_id=to, device_id_type=MESH)

        mine = pltpu.make_async_copy(x_ref, out_ref.at[_index(me)], local_sem)
        mine.start()
        first = [copy(0, me, sibling, src=x_ref)] + [copy(1 + n, me, _peer(me, k), src=x_ref) for n, k in enumerate(chips)]
        for cp in first:
            cp.start()
        passed = [copy(4 + n, _peer(me, k), sibling) for n, k in enumerate(chips)]
        for n, k in enumerate(chips):
            copy(1 + n, _peer(me, k), me).wait_recv()
            passed[n].start()
        copy(0, sibling, me).wait_recv()
        for n, k in enumerate(chips):
            copy(4 + n, _peer(sibling, k), me).wait_recv()
        for cp in first + passed:
            cp.wait_send()
        mine.wait()

    return pl.pallas_call(
        body, name=name, in_specs=[HBM_SPEC], out_specs=HBM_SPEC,
        out_shape=_sds((N_DEV, R, C), block.dtype),
        scratch_shapes=[pltpu.SemaphoreType.DMA((7,)), pltpu.SemaphoreType.DMA((7,)), pltpu.SemaphoreType.DMA],
    )(block)


def _exchange(blocks, name):
    _, R, C = blocks.shape

    def body(x_ref, out_ref, send_sems, recv_sems, local_sem):
        me = _my_place()
        mine = pltpu.make_async_copy(x_ref.at[_index(me)], out_ref.at[_index(me)], local_sem)
        mine.start()
        copies = []
        for k in range(1, N_DEV):
            to = _peer(me, k)
            cp = pltpu.make_async_remote_copy(
                src_ref=x_ref.at[_index(to)], dst_ref=out_ref.at[_index(me)],
                send_sem=send_sems.at[k - 1], recv_sem=recv_sems.at[k - 1], device_id=to, device_id_type=MESH)
            cp.start()
            copies.append(cp)
        for k in range(1, N_DEV):
            frm = _peer(me, k)
            pltpu.make_async_remote_copy(
                src_ref=x_ref.at[_index(me)], dst_ref=out_ref.at[_index(frm)],
                send_sem=send_sems.at[k - 1], recv_sem=recv_sems.at[k - 1], device_id=frm, device_id_type=MESH).wait_recv()
        for cp in copies:
            cp.wait_send()
        mine.wait()

    return pl.pallas_call(
        body, name=name, in_specs=[HBM_SPEC], out_specs=HBM_SPEC,
        out_shape=_sds(blocks.shape, blocks.dtype),
        scratch_shapes=[pltpu.SemaphoreType.DMA((7,)), pltpu.SemaphoreType.DMA((7,)), pltpu.SemaphoreType.DMA],
    )(blocks)


def _all_reduce_packed(vec):
    R, L = vec.shape

    def body(x_ref, out_ref, buf, send_sems, recv_sems):
        me = _my_place()
        buf[_index(me)] = x_ref[...]
        copies = []
        for k in range(1, N_DEV):
            to = _peer(me, k)
            cp = pltpu.make_async_remote_copy(
                src_ref=x_ref, dst_ref=buf.at[_index(me)],
                send_sem=send_sems.at[k - 1], recv_sem=recv_sems.at[k - 1], device_id=to, device_id_type=MESH)
            cp.start()
            copies.append(cp)
        for k in range(1, N_DEV):
            frm = _peer(me, k)
            pltpu.make_async_remote_copy(
                src_ref=x_ref, dst_ref=buf.at[_index(frm)],
                send_sem=send_sems.at[k - 1], recv_sem=recv_sems.at[k - 1], device_id=frm, device_id_type=MESH).wait_recv()
        for cp in copies:
            cp.wait_send()
        acc = buf[0]
        for d in range(1, N_DEV):
            acc = acc + buf[d]
        out_ref[...] = acc

    vm = pl.BlockSpec(memory_space=pltpu.VMEM)
    return pl.pallas_call(
        body, name="all_reduce_packed", in_specs=[vm], out_specs=vm, out_shape=_sds((R, L), F32),
        scratch_shapes=[pltpu.VMEM((N_DEV, R, L), F32), pltpu.SemaphoreType.DMA((7,)), pltpu.SemaphoreType.DMA((7,))],
        compiler_params=pltpu.CompilerParams(vmem_limit_bytes=32 * 2**20),
    )(vec)


def _pack(arrays):
    rows = []
    for a in arrays:
        f = a.reshape(-1).astype(F32)
        pad = (-f.shape[0]) % LANES
        rows.append(jnp.pad(f, (0, pad)).reshape(-1, LANES))
    out = jnp.concatenate(rows, axis=0)
    return jnp.pad(out, ((0, (-out.shape[0]) % SUBLANES), (0, 0)))


def _unpack(packed, shapes):
    out, r = [], 0
    for s in shapes:
        n = math.prod(s)
        nr = -(-n // LANES)
        out.append(packed[r:r + nr].reshape(-1)[:n].reshape(s))
        r += nr
    return out


def _gather_cols(w_local, name):
    R, Cs = w_local.shape
    g = _all_gather(w_local.astype(BF16), name)
    return jnp.transpose(g, (1, 0, 2)).reshape(R, N_DEV * Cs)


def _gather_rows(w_local, name):
    Rs, C = w_local.shape
    return _all_gather(w_local.astype(BF16), name).reshape(N_DEV * Rs, C)


def _col_blocks(g):
    R, C = g.shape
    return jnp.transpose(g.astype(BF16).reshape(R, N_DEV, C // N_DEV), (1, 0, 2))


def _row_blocks(g):
    R, C = g.shape
    return g.astype(BF16).reshape(N_DEV, R // N_DEV, C)


def kernel(x, w_in, sb_out_gain, dn_conv_w, dn_a_log, dn_dt_bias, dn_out_gain, w_out, ln_mix_pre, ln_mix_post, w_up, ffn_conv_w, ffn_conv_b, w_down, ln_ffn_pre, ln_ffn_post, loss_target, m_w_in, m_sb_out_gain, m_dn_conv_w, m_dn_a_log, m_dn_dt_bias, m_dn_out_gain, m_w_out, m_ln_mix_pre, m_ln_mix_post, m_w_up, m_ffn_conv_w, m_ffn_conv_b, m_w_down, m_ln_ffn_pre, m_ln_ffn_post, v_w_in, v_sb_out_gain, v_dn_conv_w, v_dn_a_log, v_dn_dt_bias, v_dn_out_gain, v_w_out, v_ln_mix_pre, v_ln_mix_post, v_w_up, v_ffn_conv_w, v_ffn_conv_b, v_w_down, v_ln_ffn_pre, v_ln_ffn_post):
    T, D = x.shape[1], x.shape[2]
    SBW = SB_HEADS * HEAD_DIM
    DNW = DN_HEADS * HEAD_DIM
    in_cols = 3 * SBW + 4 * DNW + 2 * DN_HEADS
    main_cols = 3 * SBW + 4 * DNW
    in_pad = main_cols + LANES
    qkv0, z0 = 3 * SBW, 3 * SBW + 3 * DNW
    gate_block = main_cols // LANES
    x2, tgt = x[0], loss_target[0]

    w_in_f = _gather_cols(w_in[0], "gather_w_in")
    w_in_f = jnp.pad(w_in_f, ((0, 0), (0, in_pad - in_cols)))
    w_out_f = _gather_rows(w_out[0], "gather_w_out")
    w_up_f = _gather_cols(w_up[0], "gather_w_up")
    w_down_f = _gather_rows(w_down[0], "gather_w_down")
    small_w = _all_gather(_pack([dn_conv_w[0], ffn_conv_w[0]]), "gather_conv_w")
    parts = [_unpack(small_w[d], [dn_conv_w.shape[1:], ffn_conv_w.shape[1:]]) for d in range(N_DEV)]
    dn_cw = jnp.concatenate([p[0] for p in parts], axis=1)
    ffn_cw = jnp.concatenate([p[1] for p in parts], axis=1)
    lane_pad = lambda a, off: jnp.pad(a, ((0, 0), (off, LANES - off - a.shape[1])))
    a_log_l, dt_bias_l = lane_pad(dn_a_log, DN_HEADS), lane_pad(dn_dt_bias, DN_HEADS)

    xn = _norm_in(x2, ln_mix_pre)
    proj = _matmul(xn, w_in_f, "nn", F32, "proj_in", tm_cap=512, tn_cap=2432, tk_cap=1024)
    sb_qkv = proj[:, :3 * SBW].astype(BF16)
    o_sb, mix_sb, sb_ltot = _sb_attention(sb_qkv, sb_out_gain)
    qn = _dn_branch(proj, qkv0, dn_cw, 0, True, HEAD_DIM ** -0.5)
    kn = _dn_branch(proj, qkv0 + DNW, dn_cw, DNW, True, 1.0)
    vn = _dn_branch(proj, qkv0 + 2 * DNW, dn_cw, 2 * DNW, False, 1.0)
    gc_full, beta_full = _dn_gates(proj, gate_block, a_log_l, dt_bias_l)
    o_dn, mix_dn, tm_all, s_all = _dn_scan(qn, kn, vn, gc_full, beta_full, proj, z0, dn_out_gain)
    mix = jnp.concatenate([mix_sb, mix_dn], axis=1)
    m = _matmul(mix, w_out_f, "nn", F32, "proj_out")
    h, hn = _mix_residual(x2, m, ln_mix_post, ln_ffn_pre)
    u = _matmul(hn, w_up_f, "nn", F32, "ffn_up")
    act = _ffn_act(u, ffn_cw, ffn_conv_b)
    f = _matmul(act, w_down_f, "nn", F32, "ffn_down")
    dy, df, d_ln_ffn_post, loss_part = _loss_head(h, f, ln_ffn_post, tgt)

    d_w_down = _matmul(act, df, "tn", F32, "grad_w_down")
    da = _matmul(df, w_down_f, "nt", F32, "bwd_ffn_down")
    du, d_ffn_cwb = _ffn_act_bwd(u, ffn_cw, ffn_conv_b, da)
    d_w_up = _matmul(hn, du, "tn", F32, "grad_w_up")
    dhn = _matmul(du, w_up_f, "nt", F32, "bwd_ffn_up")
    dh, dm, d_ln_ffn_pre, d_ln_mix_post = _ffn_residual_bwd(dy, dhn, h, ln_ffn_pre, m, ln_mix_post)

    d_w_out = _matmul(mix, dm, "tn", F32, "grad_w_out")
    dmix = _matmul(dm, w_out_f, "nt", F32, "bwd_proj_out")
    dq_sb, dk_sb, dv_sb, d_sb_gain = _sb_attention_bwd(sb_qkv, sb_out_gain, o_sb, sb_ltot, dmix)
    dqn, dkn, dvn, dgc_full, dbeta_full, dz, d_dn_gain = _dn_scan_bwd(
        qn, kn, vn, gc_full, beta_full, proj, z0, dn_out_gain, o_dn, tm_all, s_all, dmix, SBW)
    du_q, dcw_q = _dn_branch_bwd(proj, qkv0, dn_cw, 0, True, HEAD_DIM ** -0.5, dqn)
    du_k, dcw_k = _dn_branch_bwd(proj, qkv0 + DNW, dn_cw, DNW, True, 1.0, dkn)
    du_v, dcw_v = _dn_branch_bwd(proj, qkv0 + 2 * DNW, dn_cw, 2 * DNW, False, 1.0, dvn)
    dba, d_a_log_l, d_dt_bias_l = _dn_gates_bwd(proj, gate_block, a_log_l, dt_bias_l, dgc_full, dbeta_full)
    dproj = jnp.concatenate([dq_sb, dk_sb.astype(BF16), dv_sb.astype(BF16), du_q, du_k, du_v, dz, dba], axis=1)
    d_w_in = _matmul(xn, dproj, "tn", F32, "grad_w_in", tm_cap=512, tn_cap=2432, tk_cap=1024)[:, :in_cols]
    dxn = _matmul(dproj, w_in_f, "nt", F32, "bwd_proj_in", tk_cap=2432)
    grad_x, d_ln_mix_pre = _input_bwd(dh, dxn, x2, ln_mix_pre)

    d_dn_cw = jnp.concatenate([dcw_q[:SHORT_CONV], dcw_k[:SHORT_CONV], dcw_v[:SHORT_CONV]], axis=1)
    small = [loss_part[:, :1], d_sb_gain, d_a_log_l[:, DN_HEADS:2 * DN_HEADS], d_dt_bias_l[:, DN_HEADS:2 * DN_HEADS], d_dn_gain,
             d_ln_mix_pre, d_ln_mix_post, d_ffn_cwb[FFN_CONV:FFN_CONV + 1], d_ln_ffn_pre, d_ln_ffn_post,
             d_dn_cw, d_ffn_cwb[:FFN_CONV]]
    shapes = [a.shape for a in small]
    red = _unpack(_all_reduce_packed(_pack(small)), shapes)
    loss = red[0].reshape(())
    me = _index(_my_place())
    g_dn_cw = lax.dynamic_slice_in_dim(red[10], me * dn_conv_w.shape[2], dn_conv_w.shape[2], axis=1)
    g_ffn_cw = lax.dynamic_slice_in_dim(red[11], me * ffn_conv_w.shape[2], ffn_conv_w.shape[2], axis=1)
    names = ["sb_out_gain", "dn_conv_w", "dn_a_log", "dn_dt_bias", "dn_out_gain", "ln_mix_pre", "ln_mix_post",
             "ffn_conv_w", "ffn_conv_b", "ln_ffn_pre", "ln_ffn_post"]
    g_small = dict(sb_out_gain=red[1], dn_conv_w=g_dn_cw[None], dn_a_log=red[2], dn_dt_bias=red[3], dn_out_gain=red[4],
                   ln_mix_pre=red[5], ln_mix_post=red[6], ffn_conv_w=g_ffn_cw[None], ffn_conv_b=red[7],
                   ln_ffn_pre=red[8], ln_ffn_post=red[9])
    w_small = dict(sb_out_gain=sb_out_gain, dn_conv_w=dn_conv_w, dn_a_log=dn_a_log, dn_dt_bias=dn_dt_bias,
                   dn_out_gain=dn_out_gain, ln_mix_pre=ln_mix_pre, ln_mix_post=ln_mix_post, ffn_conv_w=ffn_conv_w,
                   ffn_conv_b=ffn_conv_b, ln_ffn_pre=ln_ffn_pre, ln_ffn_post=ln_ffn_post)
    m_small = dict(sb_out_gain=m_sb_out_gain, dn_conv_w=m_dn_conv_w, dn_a_log=m_dn_a_log, dn_dt_bias=m_dn_dt_bias,
                   dn_out_gain=m_dn_out_gain, ln_mix_pre=m_ln_mix_pre, ln_mix_post=m_ln_mix_post, ffn_conv_w=m_ffn_conv_w,
                   ffn_conv_b=m_ffn_conv_b, ln_ffn_pre=m_ln_ffn_pre, ln_ffn_post=m_ln_ffn_post)
    v_small = dict(sb_out_gain=v_sb_out_gain, dn_conv_w=v_dn_conv_w, dn_a_log=v_dn_a_log, dn_dt_bias=v_dn_dt_bias,
                   dn_out_gain=v_dn_out_gain, ln_mix_pre=v_ln_mix_pre, ln_mix_post=v_ln_mix_post, ffn_conv_w=v_ffn_conv_w,
                   ffn_conv_b=v_ffn_conv_b, ln_ffn_pre=v_ln_ffn_pre, ln_ffn_post=v_ln_ffn_post)
    sshapes = [w_small[n].shape for n in names]
    upd = _adamw_packed(_pack([g_small[n] for n in names]), _pack([w_small[n] for n in names]),
                        _pack([m_small[n] for n in names]), _pack([v_small[n] for n in names]))
    d_small, nm_small, nv_small = [dict(zip(names, _unpack(p, sshapes))) for p in upd]

    big = {}
    for n, g, blocks, w_, m_, v_ in [
            ("w_in", d_w_in, _col_blocks, w_in, m_w_in, v_w_in), ("w_out", d_w_out, _row_blocks, w_out, m_w_out, v_w_out),
            ("w_up", d_w_up, _col_blocks, w_up, m_w_up, v_w_up), ("w_down", d_w_down, _row_blocks, w_down, m_w_down, v_w_down)]:
        got = _exchange(blocks(g), "exchange_" + n)
        big[n] = [a[None] for a in _adamw_sharded(got, w_[0], m_[0], v_[0], "adamw_" + n)]

    order = ["w_in", "sb_out_gain", "dn_conv_w", "dn_a_log", "dn_dt_bias", "dn_out_gain", "w_out", "ln_mix_pre",
             "ln_mix_post", "w_up", "ffn_conv_w", "ffn_conv_b", "w_down", "ln_ffn_pre", "ln_ffn_post"]
    pick = lambda n, i: big[n][i] if n in big else [g_small, d_small, nm_small, nv_small][i][n].reshape(w_small[n].shape)
    return (loss, grad_x[None], *[pick(n, 0) for n in order], *[pick(n, 1) for n in order],
            *[pick(n, 2) for n in order], *[pick(n, 3) for n in order])
```

```python
import functools
import math

import jax
import jax.numpy as jnp
from jax import lax
from jax.experimental import pallas as pl
from jax.experimental.pallas import tpu as pltpu

F32 = jnp.float32
BF16 = jnp.bfloat16

N_DEV = 8
HEAD_DIM = 128
SB_HEADS = 8
DN_HEADS = 8
DN_CHUNK = 128
DN_GROUP = 8
INV_BLOCK = 16
Q_BLOCK = 128
SHORT_CONV = 4
FFN_CONV = 3
EPS = 1e-6
LANES = 128
SUBLANES = 8
VMEM_CAP = 56 * 2**20

ADAM_LR = 0.001
ADAM_B1 = 0.9
ADAM_B2 = 0.999
ADAM_EPS = 1e-08
ADAM_WD = 0.01
ADAM_STEP = 10

MESH = pl.DeviceIdType.MESH

assert Q_BLOCK == HEAD_DIM == DN_CHUNK == LANES


def _tile(n, cap, mult):
    if n <= cap:
        return n
    t = (cap // mult) * mult
    while t >= mult:
        if n % t == 0:
            return t
        t -= mult
    raise ValueError(f"no tile for {n} under {cap} in multiples of {mult}")


def _params(sem, vmem_bytes):
    limit = int(min(VMEM_CAP, max(vmem_bytes, 16 * 2**20)))
    if not sem:
        return pltpu.CompilerParams(vmem_limit_bytes=limit)
    return pltpu.CompilerParams(dimension_semantics=sem, vmem_limit_bytes=limit)


def _nbytes(shape, dtype):
    return math.prod(shape) * jnp.dtype(dtype).itemsize


_NN = (((1,), (0,)), ((), ()))
_NT = (((1,), (1,)), ((), ()))
_TN = (((0,), (0,)), ((), ()))


def _batched(dims, ndim):
    if ndim == 2:
        return dims
    (ca,), (cb,) = dims[0]
    return (((ca + 1,), (cb + 1,)), ((0,), (0,)))


def _dot(a, b, dims=_NN):
    return lax.dot_general(a.astype(BF16), b.astype(BF16), _batched(dims, a.ndim), preferred_element_type=F32)


def _split2(x):
    hi = x.astype(BF16)
    lo = (x - hi.astype(F32)).astype(BF16)
    return hi, lo


def _split3(x):
    hi = x.astype(BF16)
    r = x - hi.astype(F32)
    mid = r.astype(BF16)
    lo = (r - mid.astype(F32)).astype(BF16)
    return hi, mid, lo


def _dot01(x, m01, passes=3):
    parts = _split3(x) if passes == 3 else _split2(x)
    out = None
    for p in parts:
        t = lax.dot_general(p, m01, _NN, preferred_element_type=F32)
        out = t if out is None else out + t
    return out


def _dot01_left(m01, x, passes=3):
    parts = _split3(x) if passes == 3 else _split2(x)
    out = None
    for p in parts:
        t = lax.dot_general(m01, p, _NN, preferred_element_type=F32)
        out = t if out is None else out + t
    return out


def _mm3(a, b, dims=_NN):
    ah, al = _split2(a)
    bh, bl = _split2(b)
    d = functools.partial(lax.dot_general, dimension_numbers=_batched(dims, a.ndim), preferred_element_type=F32)
    return d(ah, bh) + (d(ah, bl) + d(al, bh))


def _rowsum(x):
    return jnp.sum(x, axis=-1, keepdims=True)


def _t(x):
    return jnp.swapaxes(x, -1, -2)


def _sigmoid(x):
    return 1.0 / (1.0 + jnp.exp(-x))


def _softplus(x):
    return jnp.maximum(x, 0.0) + jnp.log(1.0 + jnp.exp(-jnp.abs(x)))


def _silu(x):
    return x * _sigmoid(x)


def _silu_grad(x):
    s = _sigmoid(x)
    return s * (1.0 + x * (1.0 - s))


_GELU_C = math.sqrt(2.0 / math.pi)


def _gelu(x):
    return 0.5 * x * (1.0 + jnp.tanh(_GELU_C * (x + 0.044715 * x * x * x)))


def _gelu_grad(x):
    th = jnp.tanh(_GELU_C * (x + 0.044715 * x * x * x))
    return 0.5 * (1.0 + th) + 0.5 * x * (1.0 - th * th) * _GELU_C * (1.0 + 3.0 * 0.044715 * x * x)


def _rms(x, g):
    r = lax.rsqrt(jnp.mean(x * x, axis=-1, keepdims=True) + EPS)
    return x * r * g


def _rms_bwd(dy, x, g):
    r = lax.rsqrt(jnp.mean(x * x, axis=-1, keepdims=True) + EPS)
    xh = x * r
    gdy = dy * g
    dx = r * (gdy - xh * jnp.mean(gdy * xh, axis=-1, keepdims=True))
    return dx, jnp.sum(dy * xh, axis=-2, keepdims=True)


def _iota2(shape, axis):
    return lax.broadcasted_iota(jnp.int32, shape, axis)


def _shift_down(cur, prev8, k):
    n = cur.shape[0]
    r = pltpu.roll(cur, k, 0)
    pr = pltpu.roll(prev8, k, 0)
    head = jnp.where(_iota2(pr.shape, 0) < k, pr, r[0:SUBLANES])
    if n == SUBLANES:
        return head
    return jnp.concatenate([head, r[SUBLANES:]], axis=0)


def _shift_up(cur, next8, k):
    n = cur.shape[0]
    r = pltpu.roll(cur, n - k, 0)
    nr = pltpu.roll(next8, SUBLANES - k, 0)
    tail = jnp.where(_iota2(nr.shape, 0) >= SUBLANES - k, nr, r[n - SUBLANES:])
    if n == SUBLANES:
        return tail
    return jnp.concatenate([r[:n - SUBLANES], tail], axis=0)


def _causal_conv(cur, prev8, w_ref, taps):
    out = cur * w_ref[taps - 1:taps, :]
    for j in range(taps - 1):
        out = out + _shift_down(cur, prev8, taps - 1 - j) * w_ref[j:j + 1, :]
    return out


def _anti_conv(cur, next8, w_ref, taps):
    out = cur * w_ref[taps - 1:taps, :]
    for j in range(taps - 1):
        out = out + _shift_up(cur, next8, taps - 1 - j) * w_ref[j:j + 1, :]
    return out


def _matmul(a, b, mode, out_dtype, name, tm_cap=1024, tn_cap=1024, tk_cap=2048, after=None):
    if mode == "nn":
        (M, K), N = a.shape, b.shape[1]
    elif mode == "nt":
        (M, K), N = a.shape, b.shape[0]
    else:
        (K, M), N = a.shape, b.shape[1]
    tm = _tile(M, tm_cap, LANES)
    tn = _tile(N, tn_cap, LANES)
    tk = _tile(K, tk_cap, LANES)
    nk = K // tk
    dims = {"nn": _NN, "nt": _NT, "tn": _TN}[mode]
    a_spec = pl.BlockSpec((tk, tm), lambda i, j, k: (k, i)) if mode == "tn" else pl.BlockSpec((tm, tk), lambda i, j, k: (i, k))
    b_spec = pl.BlockSpec((tn, tk), lambda i, j, k: (j, k)) if mode == "nt" else pl.BlockSpec((tk, tn), lambda i, j, k: (k, j))

    def body(a_ref, b_ref, *rest):
        o_ref, acc_ref = rest[-2:]
        k = pl.program_id(2)

        @pl.when(k == 0)
        def _():
            acc_ref[...] = jnp.zeros_like(acc_ref)

        acc_ref[...] += lax.dot_general(a_ref[...], b_ref[...], dims, preferred_element_type=F32)

        @pl.when(k == nk - 1)
        def _():
            o_ref[...] = acc_ref[...].astype(o_ref.dtype)

    vmem = 2 * (_nbytes((tm, tk), a.dtype) + _nbytes((tk, tn), b.dtype) + _nbytes((tm, tn), out_dtype)) + _nbytes((tm, tn), F32)
    tokens = [] if after is None else [after]
    return pl.pallas_call(
        body, name=name, grid=(M // tm, N // tn, nk),
        in_specs=[a_spec, b_spec] + [pl.BlockSpec(t.shape, lambda i, j, k: (0, 0)) for t in tokens],
        out_specs=pl.BlockSpec((tm, tn), lambda i, j, k: (i, j)),
        out_shape=jax.ShapeDtypeStruct((M, N), out_dtype),
        scratch_shapes=[pltpu.VMEM((tm, tn), F32)],
        compiler_params=_params(("parallel", "parallel", "arbitrary"), vmem + 4 * 2**20),
    )(a, b, *tokens)


def _row_call(body, name, T, D, ins, outs, tr, acc_outs=()):
    def spec(a, kind):
        if kind == "row":
            return pl.BlockSpec((tr, a.shape[1]), lambda i: (i, 0))
        return pl.BlockSpec(a.shape, lambda i: (0, 0))
    in_specs = [spec(a, k) for a, k in ins]
    out_specs = [spec(a, k) for a, k in outs] + [spec(a, "vec") for a in acc_outs]
    out_shape = [a for a, _ in outs] + list(acc_outs)
    vmem = 2 * sum(_nbytes((tr, a.shape[1]) if k == "row" else a.shape, a.dtype) for a, k in list(ins) + list(outs))
    return pl.pallas_call(
        body, name=name, grid=(T // tr,), in_specs=in_specs, out_specs=out_specs, out_shape=out_shape,
        compiler_params=_params(("arbitrary",), 3 * vmem + 8 * 2**20),
    )(*[a for a, _ in ins])


def _sds(shape, dtype):
    return jax.ShapeDtypeStruct(shape, dtype)


def _accumulate(ref, val):
    @pl.when(pl.program_id(0) == 0)
    def _():
        ref[...] = jnp.zeros_like(ref)
    ref[...] += val


def _norm_in(x, g):
    T, D = x.shape

    def body(x_ref, g_ref, o_ref):
        o_ref[...] = _rms(x_ref[...], g_ref[...]).astype(BF16)

    return _row_call(body, "norm_in", T, D, [(x, "row"), (g, "vec")], [(_sds((T, D), BF16), "row")], _tile(T, 256, 16))[0]


def _mix_residual(x, m, g_post, g_pre):
    T, D = x.shape

    def body(x_ref, m_ref, gp_ref, gn_ref, h_ref, hn_ref):
        h = x_ref[...] + _rms(m_ref[...], gp_ref[...])
        h_ref[...] = h
        hn_ref[...] = _rms(h, gn_ref[...]).astype(BF16)

    return _row_call(body, "mix_residual", T, D, [(x, "row"), (m, "row"), (g_post, "vec"), (g_pre, "vec")],
                     [(_sds((T, D), F32), "row"), (_sds((T, D), BF16), "row")], _tile(T, 256, 16))


def _loss_head(h, f, g_post, target):
    T, D = h.shape

    def body(h_ref, f_ref, g_ref, t_ref, dy_ref, df_ref, dg_ref, loss_ref):
        f = f_ref[...]
        g = g_ref[...]
        diff = h_ref[...] + _rms(f, g) - t_ref[...]
        dy = diff * (1.0 / D)
        dy_ref[...] = dy
        df, dg = _rms_bwd(dy, f, g)
        df_ref[...] = df.astype(BF16)
        _accumulate(dg_ref, dg)
        _accumulate(loss_ref, jnp.full((1, LANES), 0.5 / D, F32) * jnp.sum(diff * diff))

    return _row_call(body, "loss_head", T, D, [(h, "row"), (f, "row"), (g_post, "vec"), (target, "row")],
                     [(_sds((T, D), F32), "row"), (_sds((T, D), BF16), "row")], _tile(T, 256, 16),
                     acc_outs=[_sds((1, D), F32), _sds((1, LANES), F32)])


def _ffn_residual_bwd(dy, dhn, h, g_pre, m, g_post):
    T, D = h.shape

    def body(dy_ref, dhn_ref, h_ref, gn_ref, m_ref, gp_ref, dh_ref, dm_ref, dgn_ref, dgp_ref):
        dhh, dgn = _rms_bwd(dhn_ref[...], h_ref[...], gn_ref[...])
        dh = dy_ref[...] + dhh
        dh_ref[...] = dh
        dm, dgp = _rms_bwd(dh, m_ref[...], gp_ref[...])
        dm_ref[...] = dm.astype(BF16)
        _accumulate(dgn_ref, dgn)
        _accumulate(dgp_ref, dgp)

    return _row_call(body, "ffn_residual_bwd", T, D,
                     [(dy, "row"), (dhn, "row"), (h, "row"), (g_pre, "vec"), (m, "row"), (g_post, "vec")],
                     [(_sds((T, D), F32), "row"), (_sds((T, D), BF16), "row")], _tile(T, 128, 16),
                     acc_outs=[_sds((1, D), F32), _sds((1, D), F32)])


def _input_bwd(dh, dxn, x, g):
    T, D = x.shape

    def body(dh_ref, dxn_ref, x_ref, g_ref, dx_ref, dg_ref):
        dx, dg = _rms_bwd(dxn_ref[...], x_ref[...], g_ref[...])
        dx_ref[...] = dh_ref[...] + dx
        _accumulate(dg_ref, dg)

    return _row_call(body, "input_bwd", T, D, [(dh, "row"), (dxn, "row"), (x, "row"), (g, "vec")],
                     [(_sds((T, D), F32), "row")], _tile(T, 256, 16), acc_outs=[_sds((1, D), F32)])


def _ffn_act(u, conv_w, conv_b):
    T, F2 = u.shape
    F = F2 // 2
    tc = _tile(F, 512, LANES)
    tr = _tile(T, 512, SUBLANES)
    nc = F // tc
    r8 = tr // SUBLANES

    def body(ug_ref, ugp_ref, uv_ref, uvp_ref, wg_ref, wv_ref, bg_ref, bv_ref, a_ref):
        first = pl.program_id(1) == 0
        cg = _causal_conv(ug_ref[...], jnp.where(first, 0.0, ugp_ref[...]), wg_ref, FFN_CONV) + bg_ref[...]
        cv = _causal_conv(uv_ref[...], jnp.where(first, 0.0, uvp_ref[...]), wv_ref, FFN_CONV) + bv_ref[...]
        a_ref[...] = (_gelu(cg) * cv).astype(BF16)

    cur = lambda off: pl.BlockSpec((tr, tc), lambda j, i: (i, j + off))
    prev = lambda off: pl.BlockSpec((SUBLANES, tc), lambda j, i: (jnp.maximum(i * r8 - 1, 0), j + off))
    wsp = lambda off: pl.BlockSpec((FFN_CONV, tc), lambda j, i: (0, j + off))
    bsp = lambda off: pl.BlockSpec((1, tc), lambda j, i: (0, j + off))
    return pl.pallas_call(
        body, name="ffn_act", grid=(nc, T // tr),
        in_specs=[cur(0), prev(0), cur(nc), prev(nc), wsp(0), wsp(nc), bsp(0), bsp(nc)],
        out_specs=pl.BlockSpec((tr, tc), lambda j, i: (i, j)),
        out_shape=_sds((T, F), BF16),
        compiler_params=_params(("parallel", "arbitrary"), 12 * _nbytes((tr, tc), F32) + 8 * 2**20),
    )(u, u, u, u, conv_w, conv_w, conv_b, conv_b)


def _ffn_act_bwd(u, conv_w, conv_b, da):
    T, F2 = u.shape
    F = F2 // 2
    tc = _tile(F, 512, LANES)
    tr = _tile(T, 512, SUBLANES)
    nc = F // tc
    r8 = tr // SUBLANES
    n8 = T // SUBLANES
    K = FFN_CONV

    def body(uo_ref, uop_ref, uon_ref, up_ref, upp_ref, upn_ref, da_ref, dan_ref,
             wo_ref, wp_ref, bo_ref, bp_ref, du_ref, dwb_ref):
        j = pl.program_id(0)
        i = pl.program_id(1)
        first = i == 0
        last = i == pl.num_programs(1) - 1
        is_gate = j < nc

        def dconv(uo, uo_prev, up, up_prev, da_):
            co = _causal_conv(uo, uo_prev, wo_ref, K) + bo_ref[...]
            cp = _causal_conv(up, up_prev, wp_ref, K) + bp_ref[...]
            return jnp.where(is_gate, da_ * cp * _gelu_grad(co), da_ * _gelu(cp))

        uo = uo_ref[...]
        uo_prev = jnp.where(first, 0.0, uop_ref[...])
        dc = dconv(uo, uo_prev, up_ref[...], jnp.where(first, 0.0, upp_ref[...]), da_ref[...])
        dcn = dconv(uon_ref[...], uo[tr - SUBLANES:], upn_ref[...], up_ref[tr - SUBLANES:, :], dan_ref[...])
        dcn = jnp.where(last, 0.0, dcn)
        du_ref[...] = _anti_conv(dc, dcn, wo_ref, K).astype(BF16)
        rows = [jnp.sum(dc * _shift_down(uo, uo_prev, K - 1 - t), axis=0, keepdims=True) for t in range(K - 1)]
        rows += [jnp.sum(dc * uo, axis=0, keepdims=True), jnp.sum(dc, axis=0, keepdims=True)]
        rows += [jnp.zeros_like(rows[0])] * (SUBLANES - len(rows))
        upd = jnp.concatenate(rows, axis=0)

        @pl.when(first)
        def _():
            dwb_ref[...] = jnp.zeros_like(dwb_ref)
        dwb_ref[...] += upd

    part = lambda j: (j + nc) % (2 * nc)
    cur = lambda f: pl.BlockSpec((tr, tc), lambda j, i: (i, f(j)))
    prev = lambda f: pl.BlockSpec((SUBLANES, tc), lambda j, i: (jnp.maximum(i * r8 - 1, 0), f(j)))
    nxt = lambda f: pl.BlockSpec((SUBLANES, tc), lambda j, i: (jnp.minimum((i + 1) * r8, n8 - 1), f(j)))
    own = lambda j: j
    dac = lambda j: j % nc
    wsp = lambda f: pl.BlockSpec((K, tc), lambda j, i: (0, f(j)))
    bsp = lambda f: pl.BlockSpec((1, tc), lambda j, i: (0, f(j)))
    return pl.pallas_call(
        body, name="ffn_act_bwd", grid=(2 * nc, T // tr),
        in_specs=[cur(own), prev(own), nxt(own), cur(part), prev(part), nxt(part), cur(dac), nxt(dac),
                  wsp(own), wsp(part), bsp(own), bsp(part)],
        out_specs=[pl.BlockSpec((tr, tc), lambda j, i: (i, j)), pl.BlockSpec((SUBLANES, tc), lambda j, i: (0, j))],
        out_shape=[_sds((T, F2), BF16), _sds((SUBLANES, F2), F32)],
        compiler_params=_params(("parallel", "arbitrary"), 16 * _nbytes((tr, tc), F32) + 8 * 2**20),
    )(u, u, u, u, u, u, da, da, conv_w, conv_w, conv_b, conv_b)


def _l2norm(s, scale):
    return s * (lax.rsqrt(jnp.sum(s * s, axis=-1, keepdims=True) + EPS) * scale)


def _dn_branch(proj, col0, conv_w, wcol0, l2, scale):
    T = proj.shape[0]
    W = DN_HEADS * HEAD_DIM
    tr = _tile(T, 512, SUBLANES)
    r8 = tr // SUBLANES
    cb0, wb0 = col0 // HEAD_DIM, wcol0 // HEAD_DIM

    def body(u_ref, up_ref, w_ref, o_ref):
        first = pl.program_id(1) == 0
        s = _silu(_causal_conv(u_ref[...], jnp.where(first, 0.0, up_ref[...]), w_ref, SHORT_CONV))
        o_ref[...] = _l2norm(s, scale) if l2 else s

    return pl.pallas_call(
        body, name=f"dn_branch_{col0}", grid=(DN_HEADS, T // tr),
        in_specs=[pl.BlockSpec((tr, HEAD_DIM), lambda h, i: (i, cb0 + h)),
                  pl.BlockSpec((SUBLANES, HEAD_DIM), lambda h, i: (jnp.maximum(i * r8 - 1, 0), cb0 + h)),
                  pl.BlockSpec((SHORT_CONV, HEAD_DIM), lambda h, i: (0, wb0 + h))],
        out_specs=pl.BlockSpec((tr, HEAD_DIM), lambda h, i: (i, h)),
        out_shape=_sds((T, W), F32),
        compiler_params=_params(("parallel", "arbitrary"), 16 * 2**20),
    )(proj, proj, conv_w)


def _dn_branch_bwd(proj, col0, conv_w, wcol0, l2, scale, dy):
    T = proj.shape[0]
    W = DN_HEADS * HEAD_DIM
    tr = _tile(T, 512, SUBLANES)
    r8 = tr // SUBLANES
    n8 = T // SUBLANES
    cb0, wb0 = col0 // HEAD_DIM, wcol0 // HEAD_DIM
    K = SHORT_CONV

    def body(u_ref, up_ref, un_ref, dy_ref, dyn_ref, w_ref, du_ref, dw_ref):
        i = pl.program_id(1)
        first = i == 0
        last = i == pl.num_programs(1) - 1

        def dconv(u, u_prev, dy_):
            c = _causal_conv(u, u_prev, w_ref, K)
            if l2:
                s = _silu(c)
                r = lax.rsqrt(jnp.sum(s * s, axis=-1, keepdims=True) + EPS)
                n = s * r
                ds = (scale * r) * (dy_ - n * jnp.sum(dy_ * n, axis=-1, keepdims=True))
            else:
                ds = dy_
            return ds * _silu_grad(c)

        u = u_ref[...]
        u_prev = jnp.where(first, 0.0, up_ref[...])
        dc = dconv(u, u_prev, dy_ref[...])
        dcn = jnp.where(last, 0.0, dconv(un_ref[...], u[tr - SUBLANES:], dyn_ref[...]))
        du_ref[...] = _anti_conv(dc, dcn, w_ref, K).astype(BF16)
        rows = [jnp.sum(dc * _shift_down(u, u_prev, K - 1 - t), axis=0, keepdims=True) for t in range(K - 1)]
        rows += [jnp.sum(dc * u, axis=0, keepdims=True)]
        rows += [jnp.zeros_like(rows[0])] * (SUBLANES - len(rows))
        upd = jnp.concatenate(rows, axis=0)

        @pl.when(first)
        def _():
            dw_ref[...] = jnp.zeros_like(dw_ref)
        dw_ref[...] += upd

    return pl.pallas_call(
        body, name=f"dn_branch_bwd_{col0}", grid=(DN_HEADS, T // tr),
        in_specs=[pl.BlockSpec((tr, HEAD_DIM), lambda h, i: (i, cb0 + h)),
                  pl.BlockSpec((SUBLANES, HEAD_DIM), lambda h, i: (jnp.maximum(i * r8 - 1, 0), cb0 + h)),
                  pl.BlockSpec((SUBLANES, HEAD_DIM), lambda h, i: (jnp.minimum((i + 1) * r8, n8 - 1), cb0 + h)),
                  pl.BlockSpec((tr, HEAD_DIM), lambda h, i: (i, h)),
                  pl.BlockSpec((SUBLANES, HEAD_DIM), lambda h, i: (jnp.minimum((i + 1) * r8, n8 - 1), h)),
                  pl.BlockSpec((K, HEAD_DIM), lambda h, i: (0, wb0 + h))],
        out_specs=[pl.BlockSpec((tr, HEAD_DIM), lambda h, i: (i, h)),
                   pl.BlockSpec((SUBLANES, HEAD_DIM), lambda h, i: (0, h))],
        out_shape=[_sds((T, W), BF16), _sds((SUBLANES, W), F32)],
        compiler_params=_params(("parallel", "arbitrary"), 16 * 2**20),
    )(proj, proj, proj, dy, dy, conv_w)


def _lane_masks(shape):
    lane = _iota2(shape, 1)
    return lane < DN_HEADS, (lane >= DN_HEADS) & (lane < 2 * DN_HEADS)


def _expand01(off):
    r = _iota2((LANES, DN_HEADS * HEAD_DIM), 0)
    c = _iota2((LANES, DN_HEADS * HEAD_DIM), 1)
    return (r == jnp.right_shift(c, int(math.log2(HEAD_DIM))) + off).astype(BF16)


def _select01(off):
    r = _iota2((DN_HEADS * HEAD_DIM, LANES), 0)
    c = _iota2((DN_HEADS * HEAD_DIM, LANES), 1)
    return (r == (c - off) * HEAD_DIM).astype(BF16)


def _dn_gates(proj, gate_block, a_log_l, dt_bias_l):
    T = proj.shape[0]
    C = DN_CHUNK
    W = DN_HEADS * HEAD_DIM

    def body(ba_ref, al_ref, dt_ref, gc_ref, beta_ref):
        ba = ba_ref[...]
        is_b, is_a = _lane_masks(ba.shape)
        g = jnp.where(is_a, -jnp.exp(al_ref[...]) * _softplus(ba + dt_ref[...]), 0.0)
        beta = jnp.where(is_b, _sigmoid(ba), 0.0)
        tri = (_iota2((C, C), 0) >= _iota2((C, C), 1)).astype(BF16)
        gc = _dot01_left(tri, g)
        gc_ref[...] = _dot01(gc, _expand01(DN_HEADS))
        beta_ref[...] = _dot01(beta, _expand01(0))

    vec = pl.BlockSpec((1, LANES), lambda n: (0, 0))
    return pl.pallas_call(
        body, name="dn_gates", grid=(T // C,),
        in_specs=[pl.BlockSpec((C, LANES), lambda n: (n, gate_block)), vec, vec],
        out_specs=[pl.BlockSpec((C, W), lambda n: (n, 0))] * 2,
        out_shape=[_sds((T, W), F32)] * 2,
        compiler_params=_params(("parallel",), 16 * 2**20),
    )(proj, a_log_l, dt_bias_l)


def _dn_gates_bwd(proj, gate_block, a_log_l, dt_bias_l, dgc_full, dbeta_full):
    T = proj.shape[0]
    C = DN_CHUNK
    W = DN_HEADS * HEAD_DIM

    def body(ba_ref, al_ref, dt_ref, dgc_ref, dbeta_ref, dba_ref, dal_ref, ddt_ref):
        ba = ba_ref[...]
        is_b, is_a = _lane_masks(ba.shape)
        ea = jnp.exp(al_ref[...])
        pre = ba + dt_ref[...]
        g = jnp.where(is_a, -ea * _softplus(pre), 0.0)
        beta = _sigmoid(ba)
        dgc = _dot01(dgc_ref[...], _select01(DN_HEADS))
        dbeta = _dot01(dbeta_ref[...], _select01(0))
        triu = (_iota2((C, C), 0) <= _iota2((C, C), 1)).astype(BF16)
        dg = _dot01_left(triu, dgc)
        da = jnp.where(is_a, dg * (-ea) * _sigmoid(pre), 0.0)
        dba_ref[...] = (da + jnp.where(is_b, dbeta * beta * (1.0 - beta), 0.0)).astype(BF16)
        _accumulate(dal_ref, jnp.sum(dg * g, axis=0, keepdims=True))
        _accumulate(ddt_ref, jnp.sum(da, axis=0, keepdims=True))

    vec = pl.BlockSpec((1, LANES), lambda n: (0, 0))
    full = pl.BlockSpec((C, W), lambda n: (n, 0))
    return pl.pallas_call(
        body, name="dn_gates_bwd", grid=(T // C,),
        in_specs=[pl.BlockSpec((C, LANES), lambda n: (n, gate_block)), vec, vec, full, full],
        out_specs=[pl.BlockSpec((C, LANES), lambda n: (n, 0)), vec, vec],
        out_shape=[_sds((T, LANES), BF16), _sds((1, LANES), F32), _sds((1, LANES), F32)],
        compiler_params=_params(("arbitrary",), 16 * 2**20),
    )(proj, a_log_l, dt_bias_l, dgc_full, dbeta_full)


def _unit_lower_inverse(L):
    C = L.shape[-1]
    row, col = _iota2((C, C), 0), _iota2((C, C), 1)
    eye = (row == col).astype(F32)
    sh = int(math.log2(INV_BLOCK))
    Ld = jnp.where(jnp.right_shift(row, sh) == jnp.right_shift(col, sh), L, 0.0)
    Lo = L - Ld
    X = eye - Ld
    P = Ld
    for _ in range(int(math.log2(INV_BLOCK)) - 1):
        P = _mm3(P, P)
        X = X + _mm3(X, P)
    N = _mm3(X, Lo)
    Y = eye - N
    P = N
    for _ in range(int(math.log2(C // INV_BLOCK)) - 1):
        P = _mm3(P, P)
        Y = Y + _mm3(Y, P)
    return _mm3(Y, X)


def _dn_chunk_common(q, k, v, gc, beta, gl):
    C = q.shape[-2]
    row, col = _iota2((C, C), 0), _iota2((C, C), 1)
    causal, strict = row >= col, row > col
    eg = jnp.exp(gc)
    decay = jnp.where(causal, jnp.exp(jnp.where(causal, gc - _t(gc), 0.0)), 0.0)
    kb, vb = k * beta, v * beta
    L = jnp.where(strict, _dot(kb, k, _NT) * decay, 0.0)
    Aqk = jnp.where(causal, _dot(q, k, _NT) * decay, 0.0)
    ektg = jnp.exp(gl - gc)
    return dict(causal=causal, strict=strict, eg=eg, decay=decay, kb=kb, vb=vb, L=L, Aqk=Aqk, ektg=ektg,
                kbg=kb * eg, kte=k * ektg, qd=q * eg, egl=jnp.exp(gl))


def _dn_scan(qn, kn, vn, gc_full, beta_full, proj, z_col0, gain):
    T, W = qn.shape
    C = DN_CHUNK
    N = T // C
    H = DN_HEADS
    G = DN_GROUP
    GW = G * HEAD_DIM
    zb0 = z_col0 // GW

    def body(q_ref, k_ref, v_ref, gc_ref, beta_ref, z_ref, gain_ref, o_ref, mix_ref, tm_ref, s_ref, S):
        @pl.when(pl.program_id(1) == 0)
        def _():
            S[...] = jnp.zeros_like(S)

        heads = lambda ref, rows=slice(None): jnp.stack([ref[rows, g * HEAD_DIM:(g + 1) * HEAD_DIM] for g in range(G)])
        q, k, v, gc, beta = heads(q_ref), heads(k_ref), heads(v_ref), heads(gc_ref), heads(beta_ref)
        gl = heads(gc_ref, slice(C - 1, C))
        c = _dn_chunk_common(q, k, v, gc, beta, gl)
        Tm = _unit_lower_inverse(c["L"])
        u = _dot(Tm, c["vb"])
        w = _dot(Tm, c["kbg"])
        S0 = S[...]
        vnew = u - _dot(w, S0)
        o = _dot(c["qd"], S0) + _dot(c["Aqk"], vnew)
        S[...] = S0 * c["egl"] + _dot(c["kte"], vnew, _TN)
        tm_ref[...] = Tm
        s_ref[...] = S0
        mix = (_rms(o, gain_ref[...]) * _silu(heads(z_ref))).astype(BF16)
        for g in range(G):
            sl = slice(g * HEAD_DIM, (g + 1) * HEAD_DIM)
            o_ref[:, sl] = o[g]
            mix_ref[:, sl] = mix[g]

    blk = pl.BlockSpec((C, GW), lambda h, n: (n, h))
    mat = pl.BlockSpec((G, None, C, C), lambda h, n: (h, n, 0, 0))
    return pl.pallas_call(
        body, name="dn_scan", grid=(H // G, N),
        in_specs=[blk, blk, blk, blk, blk, pl.BlockSpec((C, GW), lambda h, n: (n, zb0 + h)),
                  pl.BlockSpec((1, HEAD_DIM), lambda h, n: (0, 0))],
        out_specs=[blk, blk, mat, mat],
        out_shape=[_sds((T, W), F32), _sds((T, W), BF16), _sds((H, N, C, C), F32), _sds((H, N, C, C), F32)],
        scratch_shapes=[pltpu.VMEM((G, HEAD_DIM, HEAD_DIM), F32)],
        compiler_params=_params(("parallel", "arbitrary"), 32 * 2**20),
    )(qn, kn, vn, gc_full, beta_full, proj, gain)


def _dn_scan_bwd(qn, kn, vn, gc_full, beta_full, proj, z_col0, gain, o_raw, tm_all, s_all, dmix, dmix_col0):
    T, W = qn.shape
    C = DN_CHUNK
    N = T // C
    H = DN_HEADS
    G = DN_GROUP
    GW = G * HEAD_DIM
    zb0 = z_col0 // GW
    mb0 = dmix_col0 // GW

    def body(q_ref, k_ref, v_ref, gc_ref, beta_ref, z_ref, gain_ref, o_ref, tm_ref, s_ref, dmix_ref,
             dq_ref, dk_ref, dv_ref, dgc_ref, dbeta_ref, dz_ref, dgain_ref, dS):
        @pl.when(pl.program_id(1) == 0)
        def _():
            dS[...] = jnp.zeros_like(dS)

        @pl.when((pl.program_id(0) == 0) & (pl.program_id(1) == 0))
        def _():
            dgain_ref[...] = jnp.zeros_like(dgain_ref)

        heads = lambda ref, rows=slice(None): jnp.stack([ref[rows, g * HEAD_DIM:(g + 1) * HEAD_DIM] for g in range(G)])
        total = lambda x: jnp.sum(jnp.sum(x, axis=-1, keepdims=True), axis=-2, keepdims=True)
        gain = gain_ref[...]
        o, z, dmix = heads(o_ref), heads(z_ref), heads(dmix_ref)
        dz = (dmix * _rms(o, gain) * _silu_grad(z)).astype(BF16)
        do, dgain = _rms_bwd(dmix * _silu(z), o, gain)
        dgain_ref[...] += jnp.sum(dgain, axis=0)

        q, k, v, gc, beta = heads(q_ref), heads(k_ref), heads(v_ref), heads(gc_ref), heads(beta_ref)
        gl = heads(gc_ref, slice(C - 1, C))
        c = _dn_chunk_common(q, k, v, gc, beta, gl)
        Tm, S0, dS1 = tm_ref[...], s_ref[...], dS[...]
        w = _dot(Tm, c["kbg"])
        vnew = _dot(Tm, c["vb"]) - _dot(w, S0)

        dvnew = _dot(c["Aqk"], do, _TN) + _dot(c["kte"], dS1)
        dAqk = jnp.where(c["causal"], _dot(do, vnew, _NT), 0.0)
        dqd = _dot(do, S0, _NT)
        dkte = _dot(vnew, dS1, _NT)
        dgl = total(dS1 * S0) * c["egl"]
        dw = -_dot(dvnew, S0, _NT)
        dS[...] = dS1 * c["egl"] + _dot(c["qd"], do, _TN) - _dot(w, dvnew, _TN)

        dTm = _dot(dvnew, c["vb"], _NT) + _dot(dw, c["kbg"], _NT)
        dvb = _dot(Tm, dvnew, _TN)
        dkbg = _dot(Tm, dw, _TN)
        dL = jnp.where(c["strict"], -_mm3(_mm3(Tm, dTm, _TN), Tm, _NT), 0.0)
        dP = dL * c["decay"]
        dQ = dAqk * c["decay"]
        M = dL * c["L"] + dAqk * c["Aqk"]
        dkb = _dot(dP, k) + dkbg * c["eg"]
        dk = _dot(dP, c["kb"], _TN) + _dot(dQ, q, _TN) + dkte * c["ektg"] + dkb * beta
        dq = _dot(dQ, k) + dqd * c["eg"]
        tk = _rowsum(dkte * c["kte"])
        dgc = (_rowsum(M) - _rowsum(_t(M)) + _rowsum(dqd * c["qd"]) - tk + _rowsum(dkbg * c["kbg"]))
        dgl = dgl + total(tk)
        dgc = jnp.broadcast_to(dgc, q.shape) + jnp.where(_iota2((C, HEAD_DIM), 0) == C - 1, dgl, 0.0)
        dv = dvb * beta
        dbeta = jnp.broadcast_to(_rowsum(dkb * k) + _rowsum(dvb * v), q.shape)
        for g in range(G):
            sl = slice(g * HEAD_DIM, (g + 1) * HEAD_DIM)
            dz_ref[:, sl] = dz[g]
            dq_ref[:, sl] = dq[g]
            dk_ref[:, sl] = dk[g]
            dv_ref[:, sl] = dv[g]
            dgc_ref[:, sl] = dgc[g]
            dbeta_ref[:, sl] = dbeta[g]

    rev = lambda off: pl.BlockSpec((C, GW), lambda h, n: (N - 1 - n, off + h))
    mat = pl.BlockSpec((G, None, C, C), lambda h, n: (h, N - 1 - n, 0, 0))
    vec = pl.BlockSpec((1, HEAD_DIM), lambda h, n: (0, 0))
    return pl.pallas_call(
        body, name="dn_scan_bwd", grid=(H // G, N),
        in_specs=[rev(0), rev(0), rev(0), rev(0), rev(0), rev(zb0), vec, rev(0), mat, mat, rev(mb0)],
        out_specs=[rev(0)] * 6 + [vec],
        out_shape=[_sds((T, W), F32)] * 5 + [_sds((T, W), BF16), _sds((1, HEAD_DIM), F32)],
        scratch_shapes=[pltpu.VMEM((G, HEAD_DIM, HEAD_DIM), F32)],
        compiler_params=_params(("arbitrary", "arbitrary"), 40 * 2**20),
    )(qn, kn, vn, gc_full, beta_full, proj, gain, o_raw, tm_all, s_all, dmix)


def _sb_block(q, kj, ahead, first_key):
    z = _dot(q, kj, _NT) * (HEAD_DIM ** -0.5)
    valid = ahead < -first_key
    lb = jnp.minimum(z, 0.0) - jnp.log(1.0 + jnp.exp(-jnp.abs(z)))
    return valid, lb, jnp.where(valid, lb - z, 0.0)


def _sb_attention(qkv, gain, tq_cap=512):
    T = qkv.shape[0]
    H = SB_HEADS
    B = Q_BLOCK
    TQ = _tile(T, tq_cap, B)
    per = TQ // B

    def body(q_ref, k_ref, v_ref, gain_ref, o_ref, mix_ref, ltot_ref):
        i = pl.program_id(1)
        q = q_ref[...]
        upper = (_iota2((B, B), 0) > _iota2((B, B), 1)).astype(BF16)
        ahead = _iota2((TQ, B), 1) - _iota2((TQ, B), 0)
        last = (i + 1) * per - 1

        def step(jj, carry):
            acc, R = carry
            j = last - jj
            rows = pl.ds(pl.multiple_of(j * B, B), B)
            valid, lb, l1m = _sb_block(q, k_ref[rows, :], ahead, j * B - i * TQ)
            att = jnp.where(valid, jnp.exp(lb + R + _dot01(l1m, upper, passes=2)), 0.0)
            return acc + _dot(att, v_ref[rows, :]), R + _rowsum(l1m)

        acc, R = lax.fori_loop(0, last + 1, step, (jnp.zeros((TQ, HEAD_DIM), F32), jnp.zeros((TQ, 1), F32)))
        o_ref[...] = acc
        mix_ref[...] = _rms(acc, gain_ref[...]).astype(BF16)
        ltot_ref[...] = jnp.broadcast_to(R, (TQ, HEAD_DIM))

    head = lambda off: pl.BlockSpec((T, HEAD_DIM), lambda h, i: (0, off + h))
    blk = pl.BlockSpec((TQ, HEAD_DIM), lambda h, i: (i, h))
    return pl.pallas_call(
        body, name="sb_attention", grid=(H, T // TQ),
        in_specs=[blk, head(H), head(2 * H), pl.BlockSpec((1, HEAD_DIM), lambda h, i: (0, 0))],
        out_specs=[blk, blk, blk],
        out_shape=[_sds((T, H * HEAD_DIM), F32), _sds((T, H * HEAD_DIM), BF16), _sds((T, H * HEAD_DIM), F32)],
        compiler_params=_params(("parallel", "arbitrary"), 8 * _nbytes((T, HEAD_DIM), BF16) + 8 * 2**20),
    )(qkv, qkv, qkv, gain)


def _sb_attention_bwd(qkv, gain, o_raw, ltot, dmix, tq_cap=512):
    T = qkv.shape[0]
    H = SB_HEADS
    B = Q_BLOCK
    TQ = _tile(T, tq_cap, B)
    per = TQ // B
    scale = HEAD_DIM ** -0.5

    def body(q_ref, k_ref, v_ref, gain_ref, o_ref, ltot_ref, dmix_ref, dq_ref, dk_ref, dv_ref, dgain_ref):
        i = pl.program_id(1)

        @pl.when(i == 0)
        def _():
            dk_ref[...] = jnp.zeros_like(dk_ref)
            dv_ref[...] = jnp.zeros_like(dv_ref)

        @pl.when((pl.program_id(0) == 0) & (i == 0))
        def _():
            dgain_ref[...] = jnp.zeros_like(dgain_ref)

        q = q_ref[...]
        o = o_ref[...]
        do, dgain = _rms_bwd(dmix_ref[...], o, gain_ref[...])
        dgain_ref[...] += dgain
        ltot = ltot_ref[...]
        do_b = do.astype(BF16)
        upto = (_iota2((B, B), 0) <= _iota2((B, B), 1)).astype(BF16)
        before = (_iota2((B, B), 0) < _iota2((B, B), 1)).astype(BF16)
        ahead = _iota2((TQ, B), 1) - _iota2((TQ, B), 0)

        def step(j, carry):
            dq, PL, PG = carry
            rows = pl.ds(pl.multiple_of(j * B, B), B)
            kj = k_ref[rows, :]
            valid, lb, l1m = _sb_block(q, kj, ahead, j * B - i * TQ)
            att = jnp.where(valid, jnp.exp(lb + (ltot - PL - _dot01(l1m, upto, passes=2))), 0.0)
            sig = jnp.exp(lb)
            G = _dot(do_b, v_ref[rows, :], _NT) * att
            dv_ref[rows, :] += _dot(att, do_b, _TN)
            cum = PG + _dot01(G, before, passes=2)
            dz = jnp.where(valid, G * (1.0 - sig) - sig * cum, 0.0) * scale
            dk_ref[rows, :] += _dot(dz, q, _TN)
            return dq + _dot(dz, kj), PL + _rowsum(l1m), PG + _rowsum(G)

        zero = jnp.zeros((TQ, 1), F32)
        dq, _, _ = lax.fori_loop(0, (i + 1) * per, step, (jnp.zeros((TQ, HEAD_DIM), F32), zero, zero))
        dq_ref[...] = dq.astype(BF16)

    head = lambda off: pl.BlockSpec((T, HEAD_DIM), lambda h, i: (0, off + h))
    blk = pl.BlockSpec((TQ, HEAD_DIM), lambda h, i: (i, h))
    vec = pl.BlockSpec((1, HEAD_DIM), lambda h, i: (0, 0))
    return pl.pallas_call(
        body, name="sb_attention_bwd", grid=(H, T // TQ),
        in_specs=[blk, head(H), head(2 * H), vec, blk, blk, blk],
        out_specs=[blk, head(0), head(0), vec],
        out_shape=[_sds((T, H * HEAD_DIM), BF16), _sds((T, H * HEAD_DIM), F32), _sds((T, H * HEAD_DIM), F32),
                   _sds((1, HEAD_DIM), F32)],
        compiler_params=_params(("arbitrary", "arbitrary"), 8 * _nbytes((T, HEAD_DIM), F32) + 8 * 2**20),
    )(qkv, qkv, qkv, gain, o_raw, ltot, dmix)


def _adamw_math(w, g, m, v):
    m = ADAM_B1 * m + (1.0 - ADAM_B1) * g
    v = ADAM_B2 * v + (1.0 - ADAM_B2) * (g * g)
    m_hat = m / (1.0 - ADAM_B1 ** ADAM_STEP)
    v_hat = v / (1.0 - ADAM_B2 ** ADAM_STEP)
    delta = -ADAM_LR * (m_hat / (jnp.sqrt(v_hat) + ADAM_EPS) + ADAM_WD * w)
    return delta, m, v


def _adamw_sharded(parts, w, m, v, name):
    R, C = w.shape
    tr = _tile(R, max(SUBLANES, (2**20 // (4 * C)) // SUBLANES * SUBLANES), SUBLANES)

    def body(p_ref, w_ref, m_ref, v_ref, g_ref, d_ref, nm_ref, nv_ref):
        g = p_ref[0].astype(F32)
        for d in range(1, N_DEV):
            g = g + p_ref[d].astype(F32)
        g_ref[...] = g
        d_ref[...], nm_ref[...], nv_ref[...] = _adamw_math(w_ref[...], g, m_ref[...], v_ref[...])

    blk = pl.BlockSpec((tr, C), lambda i: (i, 0))
    return pl.pallas_call(
        body, name=name, grid=(R // tr,),
        in_specs=[pl.BlockSpec((N_DEV, tr, C), lambda i: (0, i, 0)), blk, blk, blk],
        out_specs=[blk] * 4, out_shape=[_sds((R, C), F32)] * 4,
        compiler_params=_params(("parallel",), 40 * 2**20),
    )(parts, w, m, v)


def _adamw_packed(g, w, m, v):
    def body(g_ref, w_ref, m_ref, v_ref, d_ref, nm_ref, nv_ref):
        d_ref[...], nm_ref[...], nv_ref[...] = _adamw_math(w_ref[...], g_ref[...], m_ref[...], v_ref[...])

    return pl.pallas_call(body, name="adamw_packed", out_shape=[_sds(g.shape, F32)] * 3,
                          compiler_params=_params((), 16 * 2**20))(g, w, m, v)


def _my_place():
    x, y, c = lax.axis_index("x"), lax.axis_index("y"), lax.axis_index("c")
    return x, y, c


def _peer(place, k):
    x, y, c = place
    return (1 - x if k & 4 else x, 1 - y if k & 2 else y, 1 - c if k & 1 else c)


def _index(place):
    x, y, c = place
    return 4 * x + 2 * y + c


HBM_SPEC = pl.BlockSpec(memory_space=pltpu.HBM)


def _all_gather(block, name):
    R, C = block.shape

    def body(x_ref, out_ref, send_sems, recv_sems, local_sem):
        me = _my_place()
        sibling = _peer(me, 1)
        chips = [2, 4, 6]

        def copy(sem, origin, to, src=None):
            slot = out_ref.at[_index(origin)]
            return pltpu.make_async_remote_copy(
                src_ref=slot if src is None else src, dst_ref=slot, send_sem=send_sems.at[sem], recv_sem=recv_sems.at[sem],
                device_id=to, device_id_type=MESH)

        mine = pltpu.make_async_copy(x_ref, out_ref.at[_index(me)], local_sem)
        mine.start()
        first = [copy(0, me, sibling, src=x_ref)] + [copy(1 + n, me, _peer(me, k), src=x_ref) for n, k in enumerate(chips)]
        for cp in first:
            cp.start()
        passed = [copy(4 + n, _peer(me, k), sibling) for n, k in enumerate(chips)]
        for n, k in enumerate(chips):
            copy(1 + n, _peer(me, k), me).wait_recv()
            passed[n].start()
        copy(0, sibling, me).wait_recv()
        for n, k in enumerate(chips):
            copy(4 + n, _peer(sibling, k), me).wait_recv()
        for cp in first + passed:
            cp.wait_send()
        mine.wait()

    return pl.pallas_call(
        body, name=name, in_specs=[HBM_SPEC], out_specs=HBM_SPEC,
        out_shape=_sds((N_DEV, R, C), block.dtype),
        scratch_shapes=[pltpu.SemaphoreType.DMA((7,)), pltpu.SemaphoreType.DMA((7,)), pltpu.SemaphoreType.DMA],
    )(block)


def _exchange(blocks, name):
    _, R, C = blocks.shape

    def body(x_ref, out_ref, send_sems, recv_sems, local_sem):
        me = _my_place()
        mine = pltpu.make_async_copy(x_ref.at[_index(me)], out_ref.at[_index(me)], local_sem)
        mine.start()
        copies = []
        for k in range(1, N_DEV):
            to = _peer(me, k)
            cp = pltpu.make_async_remote_copy(
                src_ref=x_ref.at[_index(to)], dst_ref=out_ref.at[_index(me)],
                send_sem=send_sems.at[k - 1], recv_sem=recv_sems.at[k - 1], device_id=to, device_id_type=MESH)
            cp.start()
            copies.append(cp)
        for k in range(1, N_DEV):
            frm = _peer(me, k)
            pltpu.make_async_remote_copy(
                src_ref=x_ref.at[_index(me)], dst_ref=out_ref.at[_index(frm)],
                send_sem=send_sems.at[k - 1], recv_sem=recv_sems.at[k - 1], device_id=frm, device_id_type=MESH).wait_recv()
        for cp in copies:
            cp.wait_send()
        mine.wait()

    return pl.pallas_call(
        body, name=name, in_specs=[HBM_SPEC], out_specs=HBM_SPEC,
        out_shape=_sds(blocks.shape, blocks.dtype),
        scratch_shapes=[pltpu.SemaphoreType.DMA((7,)), pltpu.SemaphoreType.DMA((7,)), pltpu.SemaphoreType.DMA],
    )(blocks)


SEM_SPEC = pl.BlockSpec(memory_space=pltpu.SEMAPHORE)
ANY_SPEC = pl.BlockSpec(memory_space=pl.ANY)
_EFFECT = pltpu.SideEffectType.DATAFLOW_SIDE_EFFECTING


def _spread_start(x, per_peer, name, after):
    R, C = x.shape[-2:]

    def body(x_ref, land_ref, after_ref, send_sems, recv_sems, x_thru, land_thru, token):
        me = _my_place()
        for k in range(1, N_DEV):
            to = _peer(me, k)
            pltpu.make_async_remote_copy(
                src_ref=x_ref.at[_index(to)] if per_peer else x_ref, dst_ref=land_ref.at[_index(me)],
                send_sem=send_sems.at[k - 1], recv_sem=recv_sems.at[k - 1], device_id=to, device_id_type=MESH).start()
        token[...] = jnp.zeros_like(token)

    land = lax.empty((N_DEV, R, C), x.dtype)
    send_sems, recv_sems, x_thru, land_thru, token = pl.pallas_call(
        body, name=name,
        out_shape=(pltpu.SemaphoreType.DMA((N_DEV - 1,)), pltpu.SemaphoreType.DMA((N_DEV - 1,)),
                   pltpu.HBM(x.shape, x.dtype), pltpu.HBM(land.shape, land.dtype), _sds((SUBLANES, LANES), F32)),
        in_specs=(HBM_SPEC, HBM_SPEC, ANY_SPEC),
        out_specs=(SEM_SPEC, SEM_SPEC, HBM_SPEC, HBM_SPEC, pl.BlockSpec(memory_space=pltpu.VMEM)),
        input_output_aliases={0: 2, 1: 3},
        compiler_params=pltpu.CompilerParams(has_side_effects=_EFFECT),
    )(pltpu.with_memory_space_constraint(x, pltpu.HBM), pltpu.with_memory_space_constraint(land, pltpu.HBM), after)
    return (send_sems, recv_sems, x_thru, land_thru), token


def _spread_wait(state, per_peer, name, after):
    send_sems, recv_sems, x_thru, land_thru = state

    def body(x_ref, land_ref, send_sems, recv_sems, after_ref, x_dead, got_ref):
        me = _my_place()
        for k in range(1, N_DEV):
            frm = _peer(me, k)
            copy = pltpu.make_async_remote_copy(
                src_ref=x_ref.at[_index(frm)] if per_peer else x_ref, dst_ref=land_ref.at[_index(frm)],
                send_sem=send_sems.at[k - 1], recv_sem=recv_sems.at[k - 1], device_id=frm, device_id_type=MESH)
            copy.wait_send()
            copy.wait_recv()

    x_back, got = pl.pallas_call(
        body, name=name,
        out_shape=(pltpu.HBM(x_thru.shape, x_thru.dtype), pltpu.HBM(land_thru.shape, land_thru.dtype)),
        in_specs=(HBM_SPEC, HBM_SPEC, SEM_SPEC, SEM_SPEC, ANY_SPEC), out_specs=(HBM_SPEC, HBM_SPEC),
        input_output_aliases={0: 0, 1: 1},
        compiler_params=pltpu.CompilerParams(has_side_effects=_EFFECT),
    )(x_thru, land_thru, send_sems, recv_sems, after)
    me = _index(_my_place())
    own = lax.dynamic_index_in_dim(x_back, me, axis=0, keepdims=True) if per_peer else x_back[None]
    return lax.dynamic_update_slice_in_dim(got, own, me, axis=0)


def _all_reduce_packed(vec):
    R, L = vec.shape

    def body(x_ref, out_ref, buf, send_sems, recv_sems):
        me = _my_place()
        buf[_index(me)] = x_ref[...]
        copies = []
        for k in range(1, N_DEV):
            to = _peer(me, k)
            cp = pltpu.make_async_remote_copy(
                src_ref=x_ref, dst_ref=buf.at[_index(me)],
                send_sem=send_sems.at[k - 1], recv_sem=recv_sems.at[k - 1], device_id=to, device_id_type=MESH)
            cp.start()
            copies.append(cp)
        for k in range(1, N_DEV):
            frm = _peer(me, k)
            pltpu.make_async_remote_copy(
                src_ref=x_ref, dst_ref=buf.at[_index(frm)],
                send_sem=send_sems.at[k - 1], recv_sem=recv_sems.at[k - 1], device_id=frm, device_id_type=MESH).wait_recv()
        for cp in copies:
            cp.wait_send()
        acc = buf[0]
        for d in range(1, N_DEV):
            acc = acc + buf[d]
        out_ref[...] = acc

    vm = pl.BlockSpec(memory_space=pltpu.VMEM)
    return pl.pallas_call(
        body, name="all_reduce_packed", in_specs=[vm], out_specs=vm, out_shape=_sds((R, L), F32),
        scratch_shapes=[pltpu.VMEM((N_DEV, R, L), F32), pltpu.SemaphoreType.DMA((7,)), pltpu.SemaphoreType.DMA((7,))],
        compiler_params=pltpu.CompilerParams(vmem_limit_bytes=32 * 2**20),
    )(vec)


def _pack(arrays):
    rows = []
    for a in arrays:
        f = a.reshape(-1).astype(F32)
        pad = (-f.shape[0]) % LANES
        rows.append(jnp.pad(f, (0, pad)).reshape(-1, LANES))
    out = jnp.concatenate(rows, axis=0)
    return jnp.pad(out, ((0, (-out.shape[0]) % SUBLANES), (0, 0)))


def _unpack(packed, shapes):
    out, r = [], 0
    for s in shapes:
        n = math.prod(s)
        nr = -(-n // LANES)
        out.append(packed[r:r + nr].reshape(-1)[:n].reshape(s))
        r += nr
    return out


def _gather_cols(w_local, name):
    R, Cs = w_local.shape
    g = _all_gather(w_local.astype(BF16), name)
    return jnp.transpose(g, (1, 0, 2)).reshape(R, N_DEV * Cs)


def _gather_rows(w_local, name):
    Rs, C = w_local.shape
    return _all_gather(w_local.astype(BF16), name).reshape(N_DEV * Rs, C)


def _col_blocks(g):
    R, C = g.shape
    return jnp.transpose(g.astype(BF16).reshape(R, N_DEV, C // N_DEV), (1, 0, 2))


def _row_blocks(g):
    R, C = g.shape
    return g.astype(BF16).reshape(N_DEV, R // N_DEV, C)


def kernel(x, w_in, sb_out_gain, dn_conv_w, dn_a_log, dn_dt_bias, dn_out_gain, w_out, ln_mix_pre, ln_mix_post, w_up, ffn_conv_w, ffn_conv_b, w_down, ln_ffn_pre, ln_ffn_post, loss_target, m_w_in, m_sb_out_gain, m_dn_conv_w, m_dn_a_log, m_dn_dt_bias, m_dn_out_gain, m_w_out, m_ln_mix_pre, m_ln_mix_post, m_w_up, m_ffn_conv_w, m_ffn_conv_b, m_w_down, m_ln_ffn_pre, m_ln_ffn_post, v_w_in, v_sb_out_gain, v_dn_conv_w, v_dn_a_log, v_dn_dt_bias, v_dn_out_gain, v_w_out, v_ln_mix_pre, v_ln_mix_post, v_w_up, v_ffn_conv_w, v_ffn_conv_b, v_w_down, v_ln_ffn_pre, v_ln_ffn_post):
    T, D = x.shape[1], x.shape[2]
    SBW = SB_HEADS * HEAD_DIM
    DNW = DN_HEADS * HEAD_DIM
    in_cols = 3 * SBW + 4 * DNW + 2 * DN_HEADS
    main_cols = 3 * SBW + 4 * DNW
    in_pad = main_cols + LANES
    qkv0, z0 = 3 * SBW, 3 * SBW + 3 * DNW
    gate_block = main_cols // LANES
    x2, tgt = x[0], loss_target[0]

    g_in = _all_gather(w_in[0].astype(BF16), "gather_w_in")
    small_w = _all_gather(_pack([dn_conv_w[0], ffn_conv_w[0]]), "gather_conv_w")
    st_out, tok = _spread_start(w_out[0].astype(BF16), False, "gather_w_out_start", g_in)
    st_up, tok = _spread_start(w_up[0].astype(BF16), False, "gather_w_up_start", tok)
    st_down, tok_gather = _spread_start(w_down[0].astype(BF16), False, "gather_w_down_start", tok)
    w_in_f = jnp.transpose(g_in, (1, 0, 2)).reshape(D, in_cols)
    w_in_f = jnp.pad(w_in_f, ((0, 0), (0, in_pad - in_cols)))
    parts = [_unpack(small_w[d], [dn_conv_w.shape[1:], ffn_conv_w.shape[1:]]) for d in range(N_DEV)]
    dn_cw = jnp.concatenate([p[0] for p in parts], axis=1)
    ffn_cw = jnp.concatenate([p[1] for p in parts], axis=1)
    lane_pad = lambda a, off: jnp.pad(a, ((0, 0), (off, LANES - off - a.shape[1])))
    a_log_l, dt_bias_l = lane_pad(dn_a_log, DN_HEADS), lane_pad(dn_dt_bias, DN_HEADS)

    xn = _norm_in(x2, ln_mix_pre)
    proj = _matmul(xn, w_in_f, "nn", F32, "proj_in", tm_cap=512, tn_cap=2432, tk_cap=1024, after=tok_gather)
    sb_qkv = proj[:, :3 * SBW].astype(BF16)
    o_sb, mix_sb, sb_ltot = _sb_attention(sb_qkv, sb_out_gain)
    qn = _dn_branch(proj, qkv0, dn_cw, 0, True, HEAD_DIM ** -0.5)
    kn = _dn_branch(proj, qkv0 + DNW, dn_cw, DNW, True, 1.0)
    vn = _dn_branch(proj, qkv0 + 2 * DNW, dn_cw, 2 * DNW, False, 1.0)
    gc_full, beta_full = _dn_gates(proj, gate_block, a_log_l, dt_bias_l)
    o_dn, mix_dn, tm_all, s_all = _dn_scan(qn, kn, vn, gc_full, beta_full, proj, z0, dn_out_gain)
    mix = jnp.concatenate([mix_sb, mix_dn], axis=1)
    w_out_f = _spread_wait(st_out, False, "gather_w_out_wait", mix).reshape(w_out.shape[1] * N_DEV, D)
    m = _matmul(mix, w_out_f, "nn", F32, "proj_out")
    h, hn = _mix_residual(x2, m, ln_mix_post, ln_ffn_pre)
    g_up = _spread_wait(st_up, False, "gather_w_up_wait", hn)
    w_up_f = jnp.transpose(g_up, (1, 0, 2)).reshape(D, w_up.shape[2] * N_DEV)
    u = _matmul(hn, w_up_f, "nn", F32, "ffn_up")
    act = _ffn_act(u, ffn_cw, ffn_conv_b)
    w_down_f = _spread_wait(st_down, False, "gather_w_down_wait", act).reshape(w_down.shape[1] * N_DEV, D)
    f = _matmul(act, w_down_f, "nn", F32, "ffn_down")
    dy, df, d_ln_ffn_post, loss_part = _loss_head(h, f, ln_ffn_post, tgt)

    d_w_down = _matmul(act, df, "tn", F32, "grad_w_down")
    st_xd, tok = _spread_start(_row_blocks(d_w_down), True, "exchange_w_down_start", loss_part)
    da = _matmul(df, w_down_f, "nt", F32, "bwd_ffn_down", after=tok)
    du, d_ffn_cwb = _ffn_act_bwd(u, ffn_cw, ffn_conv_b, da)
    d_w_up = _matmul(hn, du, "tn", F32, "grad_w_up")
    st_xu, tok = _spread_start(_col_blocks(d_w_up), True, "exchange_w_up_start", d_ffn_cwb)
    dhn = _matmul(du, w_up_f, "nt", F32, "bwd_ffn_up", after=tok)
    dh, dm, d_ln_ffn_pre, d_ln_mix_post = _ffn_residual_bwd(dy, dhn, h, ln_ffn_pre, m, ln_mix_post)

    d_w_out = _matmul(mix, dm, "tn", F32, "grad_w_out")
    st_xo, tok = _spread_start(_row_blocks(d_w_out), True, "exchange_w_out_start", d_ln_ffn_pre)
    dmix = _matmul(dm, w_out_f, "nt", F32, "bwd_proj_out", after=tok)
    dq_sb, dk_sb, dv_sb, d_sb_gain = _sb_attention_bwd(sb_qkv, sb_out_gain, o_sb, sb_ltot, dmix)
    dqn, dkn, dvn, dgc_full, dbeta_full, dz, d_dn_gain = _dn_scan_bwd(
        qn, kn, vn, gc_full, beta_full, proj, z0, dn_out_gain, o_dn, tm_all, s_all, dmix, SBW)
    du_q, dcw_q = _dn_branch_bwd(proj, qkv0, dn_cw, 0, True, HEAD_DIM ** -0.5, dqn)
    du_k, dcw_k = _dn_branch_bwd(proj, qkv0 + DNW, dn_cw, DNW, True, 1.0, dkn)
    du_v, dcw_v = _dn_branch_bwd(proj, qkv0 + 2 * DNW, dn_cw, 2 * DNW, False, 1.0, dvn)
    dba, d_a_log_l, d_dt_bias_l = _dn_gates_bwd(proj, gate_block, a_log_l, dt_bias_l, dgc_full, dbeta_full)
    dproj = jnp.concatenate([dq_sb, dk_sb.astype(BF16), dv_sb.astype(BF16), du_q, du_k, du_v, dz, dba], axis=1)
    d_w_in = _matmul(xn, dproj, "tn", F32, "grad_w_in", tm_cap=512, tn_cap=2432, tk_cap=1024)[:, :in_cols]
    st_xi, tok = _spread_start(_col_blocks(d_w_in), True, "exchange_w_in_start", d_sb_gain)
    dxn = _matmul(dproj, w_in_f, "nt", F32, "bwd_proj_in", tk_cap=2432, after=tok)
    grad_x, d_ln_mix_pre = _input_bwd(dh, dxn, x2, ln_mix_pre)

    d_dn_cw = jnp.concatenate([dcw_q[:SHORT_CONV], dcw_k[:SHORT_CONV], dcw_v[:SHORT_CONV]], axis=1)
    small = [loss_part[:, :1], d_sb_gain, d_a_log_l[:, DN_HEADS:2 * DN_HEADS], d_dt_bias_l[:, DN_HEADS:2 * DN_HEADS], d_dn_gain,
             d_ln_mix_pre, d_ln_mix_post, d_ffn_cwb[FFN_CONV:FFN_CONV + 1], d_ln_ffn_pre, d_ln_ffn_post,
             d_dn_cw, d_ffn_cwb[:FFN_CONV]]
    shapes = [a.shape for a in small]
    red = _unpack(_all_reduce_packed(_pack(small)), shapes)
    loss = red[0].reshape(())
    me = _index(_my_place())
    g_dn_cw = lax.dynamic_slice_in_dim(red[10], me * dn_conv_w.shape[2], dn_conv_w.shape[2], axis=1)
    g_ffn_cw = lax.dynamic_slice_in_dim(red[11], me * ffn_conv_w.shape[2], ffn_conv_w.shape[2], axis=1)
    names = ["sb_out_gain", "dn_conv_w", "dn_a_log", "dn_dt_bias", "dn_out_gain", "ln_mix_pre", "ln_mix_post",
             "ffn_conv_w", "ffn_conv_b", "ln_ffn_pre", "ln_ffn_post"]
    g_small = dict(sb_out_gain=red[1], dn_conv_w=g_dn_cw[None], dn_a_log=red[2], dn_dt_bias=red[3], dn_out_gain=red[4],
                   ln_mix_pre=red[5], ln_mix_post=red[6], ffn_conv_w=g_ffn_cw[None], ffn_conv_b=red[7],
                   ln_ffn_pre=red[8], ln_ffn_post=red[9])
    w_small = dict(sb_out_gain=sb_out_gain, dn_conv_w=dn_conv_w, dn_a_log=dn_a_log, dn_dt_bias=dn_dt_bias,
                   dn_out_gain=dn_out_gain, ln_mix_pre=ln_mix_pre, ln_mix_post=ln_mix_post, ffn_conv_w=ffn_conv_w,
                   ffn_conv_b=ffn_conv_b, ln_ffn_pre=ln_ffn_pre, ln_ffn_post=ln_ffn_post)
    m_small = dict(sb_out_gain=m_sb_out_gain, dn_conv_w=m_dn_conv_w, dn_a_log=m_dn_a_log, dn_dt_bias=m_dn_dt_bias,
                   dn_out_gain=m_dn_out_gain, ln_mix_pre=m_ln_mix_pre, ln_mix_post=m_ln_mix_post, ffn_conv_w=m_ffn_conv_w,
                   ffn_conv_b=m_ffn_conv_b, ln_ffn_pre=m_ln_ffn_pre, ln_ffn_post=m_ln_ffn_post)
    v_small = dict(sb_out_gain=v_sb_out_gain, dn_conv_w=v_dn_conv_w, dn_a_log=v_dn_a_log, dn_dt_bias=v_dn_dt_bias,
                   dn_out_gain=v_dn_out_gain, ln_mix_pre=v_ln_mix_pre, ln_mix_post=v_ln_mix_post, ffn_conv_w=v_ffn_conv_w,
                   ffn_conv_b=v_ffn_conv_b, ln_ffn_pre=v_ln_ffn_pre, ln_ffn_post=v_ln_ffn_post)
    sshapes = [w_small[n].shape for n in names]
    upd = _adamw_packed(_pack([g_small[n] for n in names]), _pack([w_small[n] for n in names]),
                        _pack([m_small[n] for n in names]), _pack([v_small[n] for n in names]))
    d_small, nm_small, nv_small = [dict(zip(names, _unpack(p, sshapes))) for p in upd]

    big = {}
    after = grad_x
    for n, st, w_, m_, v_ in [("w_down", st_xd, w_down, m_w_down, v_w_down), ("w_up", st_xu, w_up, m_w_up, v_w_up),
                              ("w_out", st_xo, w_out, m_w_out, v_w_out), ("w_in", st_xi, w_in, m_w_in, v_w_in)]:
        got = _spread_wait(st, True, "exchange_" + n + "_wait", after)
        big[n] = [a[None] for a in _adamw_sharded(got, w_[0], m_[0], v_[0], "adamw_" + n)]
        after = big[n][1]

    order = ["w_in", "sb_out_gain", "dn_conv_w", "dn_a_log", "dn_dt_bias", "dn_out_gain", "w_out", "ln_mix_pre",
             "ln_mix_post", "w_up", "ffn_conv_w", "ffn_conv_b", "w_down", "ln_ffn_pre", "ln_ffn_post"]
    pick = lambda n, i: big[n][i] if n in big else [g_small, d_small, nm_small, nv_small][i][n].reshape(w_small[n].shape)
    return (loss, grad_x[None], *[pick(n, 0) for n in order], *[pick(n, 1) for n in order],
            *[pick(n, 2) for n in order], *[pick(n, 3) for n in order])
```

```python
import functools
import math

import jax
import jax.numpy as jnp
from jax import lax
from jax.experimental import pallas as pl
from jax.experimental.pallas import tpu as pltpu

F32 = jnp.float32
BF16 = jnp.bfloat16

N_DEV = 8
HEAD_DIM = 128
SB_HEADS = 8
DN_HEADS = 8
DN_CHUNK = 128
DN_GROUP = 8
INV_BLOCK = 16
SB_KEYS = 256
SHORT_CONV = 4
FFN_CONV = 3
EPS = 1e-6
LANES = 128
SUBLANES = 8
VMEM_CAP = 56 * 2**20

ADAM_LR = 0.001
ADAM_B1 = 0.9
ADAM_B2 = 0.999
ADAM_EPS = 1e-08
ADAM_WD = 0.01
ADAM_STEP = 10

MESH = pl.DeviceIdType.MESH

assert HEAD_DIM == DN_CHUNK == LANES


def _tile(n, cap, mult):
    if n <= cap:
        return n
    t = (cap // mult) * mult
    while t >= mult:
        if n % t == 0:
            return t
        t -= mult
    raise ValueError(f"no tile for {n} under {cap} in multiples of {mult}")


def _params(sem, vmem_bytes):
    limit = int(min(VMEM_CAP, max(vmem_bytes, 16 * 2**20)))
    if not sem:
        return pltpu.CompilerParams(vmem_limit_bytes=limit)
    return pltpu.CompilerParams(dimension_semantics=sem, vmem_limit_bytes=limit)


def _nbytes(shape, dtype):
    return math.prod(shape) * jnp.dtype(dtype).itemsize


_NN = (((1,), (0,)), ((), ()))
_NT = (((1,), (1,)), ((), ()))
_TN = (((0,), (0,)), ((), ()))


def _batched(dims, ndim):
    if ndim == 2:
        return dims
    (ca,), (cb,) = dims[0]
    return (((ca + 1,), (cb + 1,)), ((0,), (0,)))


def _dot(a, b, dims=_NN):
    return lax.dot_general(a.astype(BF16), b.astype(BF16), _batched(dims, a.ndim), preferred_element_type=F32)


def _split2(x):
    hi = x.astype(BF16)
    lo = (x - hi.astype(F32)).astype(BF16)
    return hi, lo


def _split3(x):
    hi = x.astype(BF16)
    r = x - hi.astype(F32)
    mid = r.astype(BF16)
    lo = (r - mid.astype(F32)).astype(BF16)
    return hi, mid, lo


def _dot01(x, m01, passes=3):
    parts = _split3(x) if passes == 3 else _split2(x)
    out = None
    for p in parts:
        t = lax.dot_general(p, m01, _NN, preferred_element_type=F32)
        out = t if out is None else out + t
    return out


def _dot01_left(m01, x, passes=3):
    parts = _split3(x) if passes == 3 else _split2(x)
    out = None
    for p in parts:
        t = lax.dot_general(m01, p, _NN, preferred_element_type=F32)
        out = t if out is None else out + t
    return out


def _mm3(a, b, dims=_NN):
    ah, al = _split2(a)
    bh, bl = _split2(b)
    d = functools.partial(lax.dot_general, dimension_numbers=_batched(dims, a.ndim), preferred_element_type=F32)
    return d(ah, bh) + (d(ah, bl) + d(al, bh))


def _rowsum(x):
    return jnp.sum(x, axis=-1, keepdims=True)


def _t(x):
    return jnp.swapaxes(x, -1, -2)


def _sigmoid(x):
    return 1.0 / (1.0 + jnp.exp(-x))


def _softplus(x):
    return jnp.maximum(x, 0.0) + jnp.log(1.0 + jnp.exp(-jnp.abs(x)))


def _silu(x):
    return x * _sigmoid(x)


def _silu_grad(x):
    s = _sigmoid(x)
    return s * (1.0 + x * (1.0 - s))


_GELU_C = math.sqrt(2.0 / math.pi)


def _gelu(x):
    return 0.5 * x * (1.0 + jnp.tanh(_GELU_C * (x + 0.044715 * x * x * x)))


def _gelu_grad(x):
    th = jnp.tanh(_GELU_C * (x + 0.044715 * x * x * x))
    return 0.5 * (1.0 + th) + 0.5 * x * (1.0 - th * th) * _GELU_C * (1.0 + 3.0 * 0.044715 * x * x)


def _rms(x, g):
    r = lax.rsqrt(jnp.mean(x * x, axis=-1, keepdims=True) + EPS)
    return x * r * g


def _rms_bwd(dy, x, g):
    r = lax.rsqrt(jnp.mean(x * x, axis=-1, keepdims=True) + EPS)
    xh = x * r
    gdy = dy * g
    dx = r * (gdy - xh * jnp.mean(gdy * xh, axis=-1, keepdims=True))
    return dx, jnp.sum(dy * xh, axis=-2, keepdims=True)


def _iota2(shape, axis):
    return lax.broadcasted_iota(jnp.int32, shape, axis)


def _shift_down(cur, prev8, k):
    n = cur.shape[0]
    r = pltpu.roll(cur, k, 0)
    pr = pltpu.roll(prev8, k, 0)
    head = jnp.where(_iota2(pr.shape, 0) < k, pr, r[0:SUBLANES])
    if n == SUBLANES:
        return head
    return jnp.concatenate([head, r[SUBLANES:]], axis=0)


def _shift_up(cur, next8, k):
    n = cur.shape[0]
    r = pltpu.roll(cur, n - k, 0)
    nr = pltpu.roll(next8, SUBLANES - k, 0)
    tail = jnp.where(_iota2(nr.shape, 0) >= SUBLANES - k, nr, r[n - SUBLANES:])
    if n == SUBLANES:
        return tail
    return jnp.concatenate([r[:n - SUBLANES], tail], axis=0)


def _causal_conv(cur, prev8, w_ref, taps):
    out = cur * w_ref[taps - 1:taps, :]
    for j in range(taps - 1):
        out = out + _shift_down(cur, prev8, taps - 1 - j) * w_ref[j:j + 1, :]
    return out


def _anti_conv(cur, next8, w_ref, taps):
    out = cur * w_ref[taps - 1:taps, :]
    for j in range(taps - 1):
        out = out + _shift_up(cur, next8, taps - 1 - j) * w_ref[j:j + 1, :]
    return out


def _matmul(a, b, mode, out_dtype, name, tm_cap=1024, tn_cap=1024, tk_cap=2048, after=None,
            a_cut=False, b_cut=False, out_cut=False):
    a_shard = a.shape[2] if a_cut else None
    b_shard = b.shape[2] if b_cut else None
    a_full = (a.shape[1], a.shape[0] * a_shard) if a_cut else a.shape
    b_full = (b.shape[1], b.shape[0] * b_shard) if b_cut else b.shape
    assert not (a_cut and mode == "tn")
    if mode == "nn":
        (M, K), N = a_full, b_full[1]
    elif mode == "nt":
        (M, K), N = a_full, b_full[0]
    else:
        (K, M), N = a_full, b_full[1]
    n_unit = b_shard if (b_cut and mode != "nt") else N
    k_unit = math.gcd(a_shard or K, b_shard if (b_cut and mode == "nt") else K)
    tm = _tile(M, tm_cap, LANES)
    tn = N // N_DEV if out_cut else _tile(n_unit, tn_cap, LANES)
    tk = _tile(k_unit, tk_cap, LANES)
    assert n_unit % tn == 0 and k_unit % tk == 0
    nk = K // tk
    dims = {"nn": _NN, "nt": _NT, "tn": _TN}[mode]
    if a_cut:
        pa = a_shard // tk
        a_spec = pl.BlockSpec((None, tm, tk), lambda i, j, k: (k // pa, i, k % pa))
    elif mode == "tn":
        a_spec = pl.BlockSpec((tk, tm), lambda i, j, k: (k, i))
    else:
        a_spec = pl.BlockSpec((tm, tk), lambda i, j, k: (i, k))
    if b_cut and mode == "nt":
        pb = b_shard // tk
        b_spec = pl.BlockSpec((None, tn, tk), lambda i, j, k: (k // pb, j, k % pb))
    elif b_cut:
        pb = b_shard // tn
        b_spec = pl.BlockSpec((None, tk, tn), lambda i, j, k: (j // pb, k, j % pb))
    elif mode == "nt":
        b_spec = pl.BlockSpec((tn, tk), lambda i, j, k: (j, k))
    else:
        b_spec = pl.BlockSpec((tk, tn), lambda i, j, k: (k, j))
    if out_cut:
        out_spec, out_shape = pl.BlockSpec((None, tm, tn), lambda i, j, k: (j, i, 0)), (N_DEV, M, tn)
    else:
        out_spec, out_shape = pl.BlockSpec((tm, tn), lambda i, j, k: (i, j)), (M, N)

    def body(a_ref, b_ref, *rest):
        o_ref, acc_ref = rest[-2:]
        k = pl.program_id(2)

        @pl.when(k == 0)
        def _():
            acc_ref[...] = jnp.zeros_like(acc_ref)

        acc_ref[...] += lax.dot_general(a_ref[...], b_ref[...], dims, preferred_element_type=F32)

        @pl.when(k == nk - 1)
        def _():
            o_ref[...] = acc_ref[...].astype(o_ref.dtype)

    vmem = 2 * (_nbytes((tm, tk), a.dtype) + _nbytes((tk, tn), b.dtype) + _nbytes((tm, tn), out_dtype)) + _nbytes((tm, tn), F32)
    tokens = [] if after is None else [after]
    return pl.pallas_call(
        body, name=name, grid=(M // tm, N // tn, nk),
        in_specs=[a_spec, b_spec] + [pl.BlockSpec(t.shape, lambda i, j, k: (0, 0)) for t in tokens],
        out_specs=out_spec,
        out_shape=jax.ShapeDtypeStruct(out_shape, out_dtype),
        scratch_shapes=[pltpu.VMEM((tm, tn), F32)],
        compiler_params=_params(("parallel", "parallel", "arbitrary"), vmem + 4 * 2**20),
    )(a, b, *tokens)


def _row_call(body, name, T, D, ins, outs, tr, acc_outs=()):
    def spec(a, kind):
        if kind == "row":
            return pl.BlockSpec((tr, a.shape[1]), lambda i: (i, 0))
        return pl.BlockSpec(a.shape, lambda i: (0, 0))
    in_specs = [spec(a, k) for a, k in ins]
    out_specs = [spec(a, k) for a, k in outs] + [spec(a, "vec") for a in acc_outs]
    out_shape = [a for a, _ in outs] + list(acc_outs)
    vmem = 2 * sum(_nbytes((tr, a.shape[1]) if k == "row" else a.shape, a.dtype) for a, k in list(ins) + list(outs))
    return pl.pallas_call(
        body, name=name, grid=(T // tr,), in_specs=in_specs, out_specs=out_specs, out_shape=out_shape,
        compiler_params=_params(("arbitrary",), 3 * vmem + 8 * 2**20),
    )(*[a for a, _ in ins])


def _sds(shape, dtype):
    return jax.ShapeDtypeStruct(shape, dtype)


def _accumulate(ref, val):
    @pl.when(pl.program_id(0) == 0)
    def _():
        ref[...] = jnp.zeros_like(ref)
    ref[...] += val


def _norm_in(x, g):
    T, D = x.shape

    def body(x_ref, g_ref, o_ref):
        o_ref[...] = _rms(x_ref[...], g_ref[...]).astype(BF16)

    return _row_call(body, "norm_in", T, D, [(x, "row"), (g, "vec")], [(_sds((T, D), BF16), "row")], _tile(T, 256, 16))[0]


def _mix_residual(x, m, g_post, g_pre):
    T, D = x.shape

    def body(x_ref, m_ref, gp_ref, gn_ref, h_ref, hn_ref):
        h = x_ref[...] + _rms(m_ref[...], gp_ref[...])
        h_ref[...] = h
        hn_ref[...] = _rms(h, gn_ref[...]).astype(BF16)

    return _row_call(body, "mix_residual", T, D, [(x, "row"), (m, "row"), (g_post, "vec"), (g_pre, "vec")],
                     [(_sds((T, D), F32), "row"), (_sds((T, D), BF16), "row")], _tile(T, 256, 16))


def _loss_head(h, f, g_post, target):
    T, D = h.shape

    def body(h_ref, f_ref, g_ref, t_ref, dy_ref, df_ref, dg_ref, loss_ref):
        f = f_ref[...]
        g = g_ref[...]
        diff = h_ref[...] + _rms(f, g) - t_ref[...]
        dy = diff * (1.0 / D)
        dy_ref[...] = dy
        df, dg = _rms_bwd(dy, f, g)
        df_ref[...] = df.astype(BF16)
        _accumulate(dg_ref, dg)
        _accumulate(loss_ref, jnp.full((1, LANES), 0.5 / D, F32) * jnp.sum(diff * diff))

    return _row_call(body, "loss_head", T, D, [(h, "row"), (f, "row"), (g_post, "vec"), (target, "row")],
                     [(_sds((T, D), F32), "row"), (_sds((T, D), BF16), "row")], _tile(T, 256, 16),
                     acc_outs=[_sds((1, D), F32), _sds((1, LANES), F32)])


def _ffn_residual_bwd(dy, dhn, h, g_pre, m, g_post):
    T, D = h.shape

    def body(dy_ref, dhn_ref, h_ref, gn_ref, m_ref, gp_ref, dh_ref, dm_ref, dgn_ref, dgp_ref):
        dhh, dgn = _rms_bwd(dhn_ref[...], h_ref[...], gn_ref[...])
        dh = dy_ref[...] + dhh
        dh_ref[...] = dh
        dm, dgp = _rms_bwd(dh, m_ref[...], gp_ref[...])
        dm_ref[...] = dm.astype(BF16)
        _accumulate(dgn_ref, dgn)
        _accumulate(dgp_ref, dgp)

    return _row_call(body, "ffn_residual_bwd", T, D,
                     [(dy, "row"), (dhn, "row"), (h, "row"), (g_pre, "vec"), (m, "row"), (g_post, "vec")],
                     [(_sds((T, D), F32), "row"), (_sds((T, D), BF16), "row")], _tile(T, 128, 16),
                     acc_outs=[_sds((1, D), F32), _sds((1, D), F32)])


def _input_bwd(dh, dxn, x, g):
    T, D = x.shape

    def body(dh_ref, dxn_ref, x_ref, g_ref, dx_ref, dg_ref):
        dx, dg = _rms_bwd(dxn_ref[...], x_ref[...], g_ref[...])
        dx_ref[...] = dh_ref[...] + dx
        _accumulate(dg_ref, dg)

    return _row_call(body, "input_bwd", T, D, [(dh, "row"), (dxn, "row"), (x, "row"), (g, "vec")],
                     [(_sds((T, D), F32), "row")], _tile(T, 256, 16), acc_outs=[_sds((1, D), F32)])


def _ffn_act(u, conv_w, conv_b):
    T, F2 = u.shape
    F = F2 // 2
    tc = _tile(F, 512, LANES)
    tr = _tile(T, 512, SUBLANES)
    nc = F // tc
    r8 = tr // SUBLANES

    def body(ug_ref, ugp_ref, uv_ref, uvp_ref, wg_ref, wv_ref, bg_ref, bv_ref, a_ref):
        first = pl.program_id(1) == 0
        cg = _causal_conv(ug_ref[...], jnp.where(first, 0.0, ugp_ref[...]), wg_ref, FFN_CONV) + bg_ref[...]
        cv = _causal_conv(uv_ref[...], jnp.where(first, 0.0, uvp_ref[...]), wv_ref, FFN_CONV) + bv_ref[...]
        a_ref[...] = (_gelu(cg) * cv).astype(BF16)

    cur = lambda off: pl.BlockSpec((tr, tc), lambda j, i: (i, j + off))
    prev = lambda off: pl.BlockSpec((SUBLANES, tc), lambda j, i: (jnp.maximum(i * r8 - 1, 0), j + off))
    wsp = lambda off: pl.BlockSpec((FFN_CONV, tc), lambda j, i: (0, j + off))
    bsp = lambda off: pl.BlockSpec((1, tc), lambda j, i: (0, j + off))
    return pl.pallas_call(
        body, name="ffn_act", grid=(nc, T // tr),
        in_specs=[cur(0), prev(0), cur(nc), prev(nc), wsp(0), wsp(nc), bsp(0), bsp(nc)],
        out_specs=pl.BlockSpec((tr, tc), lambda j, i: (i, j)),
        out_shape=_sds((T, F), BF16),
        compiler_params=_params(("parallel", "arbitrary"), 12 * _nbytes((tr, tc), F32) + 8 * 2**20),
    )(u, u, u, u, conv_w, conv_w, conv_b, conv_b)


def _ffn_act_bwd(u, conv_w, conv_b, da):
    T, F2 = u.shape
    F = F2 // 2
    tc = _tile(F, 512, LANES)
    tr = _tile(T, 512, SUBLANES)
    nc = F // tc
    r8 = tr // SUBLANES
    n8 = T // SUBLANES
    K = FFN_CONV

    def body(ug_ref, ugp_ref, ugn_ref, uv_ref, uvp_ref, uvn_ref, da_ref, dan_ref,
             wg_ref, wv_ref, bg_ref, bv_ref, du_ref, dwb_ref):
        i = pl.program_id(1)
        first = i == 0
        last = i == pl.num_programs(1) - 1

        def dconv(ug, ug_prev, uv, uv_prev, da_):
            cg = _causal_conv(ug, ug_prev, wg_ref, K) + bg_ref[...]
            cv = _causal_conv(uv, uv_prev, wv_ref, K) + bv_ref[...]
            return da_ * cv * _gelu_grad(cg), da_ * _gelu(cg)

        ug, uv = ug_ref[...], uv_ref[...]
        ug_prev, uv_prev = jnp.where(first, 0.0, ugp_ref[...]), jnp.where(first, 0.0, uvp_ref[...])
        dcg, dcv = dconv(ug, ug_prev, uv, uv_prev, da_ref[...])
        dcgn, dcvn = dconv(ugn_ref[...], ug[tr - SUBLANES:], uvn_ref[...], uv[tr - SUBLANES:], dan_ref[...])
        du_ref[0] = _anti_conv(dcg, jnp.where(last, 0.0, dcgn), wg_ref, K).astype(BF16)
        du_ref[1] = _anti_conv(dcv, jnp.where(last, 0.0, dcvn), wv_ref, K).astype(BF16)

        @pl.when(first)
        def _():
            dwb_ref[...] = jnp.zeros_like(dwb_ref)

        for half, (dc, uo, uo_prev) in enumerate([(dcg, ug, ug_prev), (dcv, uv, uv_prev)]):
            rows = [jnp.sum(dc * _shift_down(uo, uo_prev, K - 1 - t), axis=0, keepdims=True) for t in range(K - 1)]
            rows += [jnp.sum(dc * uo, axis=0, keepdims=True), jnp.sum(dc, axis=0, keepdims=True)]
            rows += [jnp.zeros_like(rows[0])] * (SUBLANES - len(rows))
            dwb_ref[half] += jnp.concatenate(rows, axis=0)

    cur = lambda off: pl.BlockSpec((tr, tc), lambda j, i: (i, j + off))
    prev = lambda off: pl.BlockSpec((SUBLANES, tc), lambda j, i: (jnp.maximum(i * r8 - 1, 0), j + off))
    nxt = lambda off: pl.BlockSpec((SUBLANES, tc), lambda j, i: (jnp.minimum((i + 1) * r8, n8 - 1), j + off))
    wsp = lambda off: pl.BlockSpec((K, tc), lambda j, i: (0, j + off))
    bsp = lambda off: pl.BlockSpec((1, tc), lambda j, i: (0, j + off))
    return pl.pallas_call(
        body, name="ffn_act_bwd", grid=(nc, T // tr),
        in_specs=[cur(0), prev(0), nxt(0), cur(nc), prev(nc), nxt(nc), cur(0), nxt(0), wsp(0), wsp(nc), bsp(0), bsp(nc)],
        out_specs=[pl.BlockSpec((2, tr, tc), lambda j, i: (0, i, j)), pl.BlockSpec((2, SUBLANES, tc), lambda j, i: (0, 0, j))],
        out_shape=[_sds((2, T, F), BF16), _sds((2, SUBLANES, F), F32)],
        compiler_params=_params(("parallel", "arbitrary"), 24 * _nbytes((tr, tc), F32) + 8 * 2**20),
    )(u, u, u, u, u, u, da, da, conv_w, conv_w, conv_b, conv_b)


def _l2norm(s, scale):
    return s * (lax.rsqrt(jnp.sum(s * s, axis=-1, keepdims=True) + EPS) * scale)


def _dn_branch(proj, col0, conv_w, wcol0, l2, scale):
    T = proj.shape[0]
    W = DN_HEADS * HEAD_DIM
    tr = _tile(T, 512, SUBLANES)
    r8 = tr // SUBLANES
    cb0, wb0 = col0 // HEAD_DIM, wcol0 // HEAD_DIM

    def body(u_ref, up_ref, w_ref, o_ref):
        first = pl.program_id(1) == 0
        s = _silu(_causal_conv(u_ref[...], jnp.where(first, 0.0, up_ref[...]), w_ref, SHORT_CONV))
        o_ref[...] = _l2norm(s, scale) if l2 else s

    return pl.pallas_call(
        body, name=f"dn_branch_{col0}", grid=(DN_HEADS, T // tr),
        in_specs=[pl.BlockSpec((tr, HEAD_DIM), lambda h, i: (i, cb0 + h)),
                  pl.BlockSpec((SUBLANES, HEAD_DIM), lambda h, i: (jnp.maximum(i * r8 - 1, 0), cb0 + h)),
                  pl.BlockSpec((SHORT_CONV, HEAD_DIM), lambda h, i: (0, wb0 + h))],
        out_specs=pl.BlockSpec((tr, HEAD_DIM), lambda h, i: (i, h)),
        out_shape=_sds((T, W), F32),
        compiler_params=_params(("parallel", "arbitrary"), 16 * 2**20),
    )(proj, proj, conv_w)


def _dn_branch_bwd(proj, col0, conv_w, wcol0, l2, scale, dy):
    T = proj.shape[0]
    W = DN_HEADS * HEAD_DIM
    tr = _tile(T, 512, SUBLANES)
    r8 = tr // SUBLANES
    n8 = T // SUBLANES
    cb0, wb0 = col0 // HEAD_DIM, wcol0 // HEAD_DIM
    K = SHORT_CONV

    def body(u_ref, up_ref, un_ref, dy_ref, dyn_ref, w_ref, du_ref, dw_ref):
        i = pl.program_id(1)
        first = i == 0
        last = i == pl.num_programs(1) - 1

        def dconv(u, u_prev, dy_):
            c = _causal_conv(u, u_prev, w_ref, K)
            if l2:
                s = _silu(c)
                r = lax.rsqrt(jnp.sum(s * s, axis=-1, keepdims=True) + EPS)
                n = s * r
                ds = (scale * r) * (dy_ - n * jnp.sum(dy_ * n, axis=-1, keepdims=True))
            else:
                ds = dy_
            return ds * _silu_grad(c)

        u = u_ref[...]
        u_prev = jnp.where(first, 0.0, up_ref[...])
        dc = dconv(u, u_prev, dy_ref[...])
        dcn = jnp.where(last, 0.0, dconv(un_ref[...], u[tr - SUBLANES:], dyn_ref[...]))
        du_ref[...] = _anti_conv(dc, dcn, w_ref, K).astype(BF16)
        rows = [jnp.sum(dc * _shift_down(u, u_prev, K - 1 - t), axis=0, keepdims=True) for t in range(K - 1)]
        rows += [jnp.sum(dc * u, axis=0, keepdims=True)]
        rows += [jnp.zeros_like(rows[0])] * (SUBLANES - len(rows))
        upd = jnp.concatenate(rows, axis=0)

        @pl.when(first)
        def _():
            dw_ref[...] = jnp.zeros_like(dw_ref)
        dw_ref[...] += upd

    return pl.pallas_call(
        body, name=f"dn_branch_bwd_{col0}", grid=(DN_HEADS, T // tr),
        in_specs=[pl.BlockSpec((tr, HEAD_DIM), lambda h, i: (i, cb0 + h)),
                  pl.BlockSpec((SUBLANES, HEAD_DIM), lambda h, i: (jnp.maximum(i * r8 - 1, 0), cb0 + h)),
                  pl.BlockSpec((SUBLANES, HEAD_DIM), lambda h, i: (jnp.minimum((i + 1) * r8, n8 - 1), cb0 + h)),
                  pl.BlockSpec((tr, HEAD_DIM), lambda h, i: (i, h)),
                  pl.BlockSpec((SUBLANES, HEAD_DIM), lambda h, i: (jnp.minimum((i + 1) * r8, n8 - 1), h)),
                  pl.BlockSpec((K, HEAD_DIM), lambda h, i: (0, wb0 + h))],
        out_specs=[pl.BlockSpec((tr, HEAD_DIM), lambda h, i: (i, h)),
                   pl.BlockSpec((SUBLANES, HEAD_DIM), lambda h, i: (0, h))],
        out_shape=[_sds((T, W), BF16), _sds((SUBLANES, W), F32)],
        compiler_params=_params(("parallel", "arbitrary"), 16 * 2**20),
    )(proj, proj, proj, dy, dy, conv_w)


def _lane_masks(shape):
    lane = _iota2(shape, 1)
    return lane < DN_HEADS, (lane >= DN_HEADS) & (lane < 2 * DN_HEADS)


def _expand01(off):
    r = _iota2((LANES, DN_HEADS * HEAD_DIM), 0)
    c = _iota2((LANES, DN_HEADS * HEAD_DIM), 1)
    return (r == jnp.right_shift(c, int(math.log2(HEAD_DIM))) + off).astype(BF16)


def _select01(off):
    r = _iota2((DN_HEADS * HEAD_DIM, LANES), 0)
    c = _iota2((DN_HEADS * HEAD_DIM, LANES), 1)
    return (r == (c - off) * HEAD_DIM).astype(BF16)


def _dn_gates(proj, gate_block, a_log_l, dt_bias_l):
    T = proj.shape[0]
    C = DN_CHUNK
    W = DN_HEADS * HEAD_DIM

    def body(ba_ref, al_ref, dt_ref, gc_ref, beta_ref):
        ba = ba_ref[...]
        is_b, is_a = _lane_masks(ba.shape)
        g = jnp.where(is_a, -jnp.exp(al_ref[...]) * _softplus(ba + dt_ref[...]), 0.0)
        beta = jnp.where(is_b, _sigmoid(ba), 0.0)
        tri = (_iota2((C, C), 0) >= _iota2((C, C), 1)).astype(BF16)
        gc = _dot01_left(tri, g)
        gc_ref[...] = _dot01(gc, _expand01(DN_HEADS))
        beta_ref[...] = _dot01(beta, _expand01(0))

    vec = pl.BlockSpec((1, LANES), lambda n: (0, 0))
    return pl.pallas_call(
        body, name="dn_gates", grid=(T // C,),
        in_specs=[pl.BlockSpec((C, LANES), lambda n: (n, gate_block)), vec, vec],
        out_specs=[pl.BlockSpec((C, W), lambda n: (n, 0))] * 2,
        out_shape=[_sds((T, W), F32)] * 2,
        compiler_params=_params(("parallel",), 16 * 2**20),
    )(proj, a_log_l, dt_bias_l)


def _dn_gates_bwd(proj, gate_block, a_log_l, dt_bias_l, dgc_full, dbeta_full):
    T = proj.shape[0]
    C = DN_CHUNK
    W = DN_HEADS * HEAD_DIM

    def body(ba_ref, al_ref, dt_ref, dgc_ref, dbeta_ref, dba_ref, dal_ref, ddt_ref):
        ba = ba_ref[...]
        is_b, is_a = _lane_masks(ba.shape)
        ea = jnp.exp(al_ref[...])
        pre = ba + dt_ref[...]
        g = jnp.where(is_a, -ea * _softplus(pre), 0.0)
        beta = _sigmoid(ba)
        dgc = _dot01(dgc_ref[...], _select01(DN_HEADS))
        dbeta = _dot01(dbeta_ref[...], _select01(0))
        triu = (_iota2((C, C), 0) <= _iota2((C, C), 1)).astype(BF16)
        dg = _dot01_left(triu, dgc)
        da = jnp.where(is_a, dg * (-ea) * _sigmoid(pre), 0.0)
        dba_ref[...] = (da + jnp.where(is_b, dbeta * beta * (1.0 - beta), 0.0)).astype(BF16)
        _accumulate(dal_ref, jnp.sum(dg * g, axis=0, keepdims=True))
        _accumulate(ddt_ref, jnp.sum(da, axis=0, keepdims=True))

    vec = pl.BlockSpec((1, LANES), lambda n: (0, 0))
    full = pl.BlockSpec((C, W), lambda n: (n, 0))
    return pl.pallas_call(
        body, name="dn_gates_bwd", grid=(T // C,),
        in_specs=[pl.BlockSpec((C, LANES), lambda n: (n, gate_block)), vec, vec, full, full],
        out_specs=[pl.BlockSpec((C, LANES), lambda n: (n, 0)), vec, vec],
        out_shape=[_sds((T, LANES), BF16), _sds((1, LANES), F32), _sds((1, LANES), F32)],
        compiler_params=_params(("arbitrary",), 16 * 2**20),
    )(proj, a_log_l, dt_bias_l, dgc_full, dbeta_full)


def _unit_lower_inverse(L):
    C = L.shape[-1]
    row, col = _iota2((C, C), 0), _iota2((C, C), 1)
    eye = (row == col).astype(F32)
    sh = int(math.log2(INV_BLOCK))
    Ld = jnp.where(jnp.right_shift(row, sh) == jnp.right_shift(col, sh), L, 0.0)
    Lo = L - Ld
    X = eye - Ld
    P = Ld
    for _ in range(int(math.log2(INV_BLOCK)) - 1):
        P = _mm3(P, P)
        X = X + _mm3(X, P)
    N = _mm3(X, Lo)
    Y = eye - N
    P = N
    for _ in range(int(math.log2(C // INV_BLOCK)) - 1):
        P = _mm3(P, P)
        Y = Y + _mm3(Y, P)
    return _mm3(Y, X)


def _dn_chunk_common(q, k, v, gc, beta, gl):
    C = q.shape[-2]
    row, col = _iota2((C, C), 0), _iota2((C, C), 1)
    causal, strict = row >= col, row > col
    eg = jnp.exp(gc)
    decay = jnp.where(causal, jnp.exp(jnp.where(causal, gc - _t(gc), 0.0)), 0.0)
    kb, vb = k * beta, v * beta
    L = jnp.where(strict, _dot(kb, k, _NT) * decay, 0.0)
    Aqk = jnp.where(causal, _dot(q, k, _NT) * decay, 0.0)
    ektg = jnp.exp(gl - gc)
    return dict(causal=causal, strict=strict, eg=eg, decay=decay, kb=kb, vb=vb, L=L, Aqk=Aqk, ektg=ektg,
                kbg=kb * eg, kte=k * ektg, qd=q * eg, egl=jnp.exp(gl))


def _dn_scan(qn, kn, vn, gc_full, beta_full, proj, z_col0, gain):
    T, W = qn.shape
    C = DN_CHUNK
    N = T // C
    H = DN_HEADS
    G = DN_GROUP
    GW = G * HEAD_DIM
    zb0 = z_col0 // GW

    def body(q_ref, k_ref, v_ref, gc_ref, beta_ref, z_ref, gain_ref, o_ref, mix_ref, tm_ref, s_ref, S):
        @pl.when(pl.program_id(1) == 0)
        def _():
            S[...] = jnp.zeros_like(S)

        heads = lambda ref, rows=slice(None): jnp.stack([ref[rows, g * HEAD_DIM:(g + 1) * HEAD_DIM] for g in range(G)])
        q, k, v, gc, beta = heads(q_ref), heads(k_ref), heads(v_ref), heads(gc_ref), heads(beta_ref)
        gl = heads(gc_ref, slice(C - 1, C))
        c = _dn_chunk_common(q, k, v, gc, beta, gl)
        Tm = _unit_lower_inverse(c["L"])
        u = _dot(Tm, c["vb"])
        w = _dot(Tm, c["kbg"])
        S0 = S[...]
        vnew = u - _dot(w, S0)
        o = _dot(c["qd"], S0) + _dot(c["Aqk"], vnew)
        S[...] = S0 * c["egl"] + _dot(c["kte"], vnew, _TN)
        tm_ref[...] = Tm
        s_ref[...] = S0
        mix = (_rms(o, gain_ref[...]) * _silu(heads(z_ref))).astype(BF16)
        for g in range(G):
            sl = slice(g * HEAD_DIM, (g + 1) * HEAD_DIM)
            o_ref[:, sl] = o[g]
            mix_ref[:, sl] = mix[g]

    blk = pl.BlockSpec((C, GW), lambda h, n: (n, h))
    mat = pl.BlockSpec((G, None, C, C), lambda h, n: (h, n, 0, 0))
    return pl.pallas_call(
        body, name="dn_scan", grid=(H // G, N),
        in_specs=[blk, blk, blk, blk, blk, pl.BlockSpec((C, GW), lambda h, n: (n, zb0 + h)),
                  pl.BlockSpec((1, HEAD_DIM), lambda h, n: (0, 0))],
        out_specs=[blk, blk, mat, mat],
        out_shape=[_sds((T, W), F32), _sds((T, W), BF16), _sds((H, N, C, C), F32), _sds((H, N, C, C), F32)],
        scratch_shapes=[pltpu.VMEM((G, HEAD_DIM, HEAD_DIM), F32)],
        compiler_params=_params(("parallel", "arbitrary"), 32 * 2**20),
    )(qn, kn, vn, gc_full, beta_full, proj, gain)


def _dn_scan_bwd(qn, kn, vn, gc_full, beta_full, proj, z_col0, gain, o_raw, tm_all, s_all, dmix, dmix_col0):
    T, W = qn.shape
    C = DN_CHUNK
    N = T // C
    H = DN_HEADS
    G = DN_GROUP
    GW = G * HEAD_DIM
    zb0 = z_col0 // GW
    mb0 = dmix_col0 // GW

    def body(q_ref, k_ref, v_ref, gc_ref, beta_ref, z_ref, gain_ref, o_ref, tm_ref, s_ref, dmix_ref,
             dq_ref, dk_ref, dv_ref, dgc_ref, dbeta_ref, dz_ref, dgain_ref, dS):
        @pl.when(pl.program_id(1) == 0)
        def _():
            dS[...] = jnp.zeros_like(dS)

        @pl.when((pl.program_id(0) == 0) & (pl.program_id(1) == 0))
        def _():
            dgain_ref[...] = jnp.zeros_like(dgain_ref)

        heads = lambda ref, rows=slice(None): jnp.stack([ref[rows, g * HEAD_DIM:(g + 1) * HEAD_DIM] for g in range(G)])
        total = lambda x: jnp.sum(jnp.sum(x, axis=-1, keepdims=True), axis=-2, keepdims=True)
        gain = gain_ref[...]
        o, z, dmix = heads(o_ref), heads(z_ref), heads(dmix_ref)
        dz = (dmix * _rms(o, gain) * _silu_grad(z)).astype(BF16)
        do, dgain = _rms_bwd(dmix * _silu(z), o, gain)
        dgain_ref[...] += jnp.sum(dgain, axis=0)

        q, k, v, gc, beta = heads(q_ref), heads(k_ref), heads(v_ref), heads(gc_ref), heads(beta_ref)
        gl = heads(gc_ref, slice(C - 1, C))
        c = _dn_chunk_common(q, k, v, gc, beta, gl)
        Tm, S0, dS1 = tm_ref[...], s_ref[...], dS[...]
        w = _dot(Tm, c["kbg"])
        vnew = _dot(Tm, c["vb"]) - _dot(w, S0)

        dvnew = _dot(c["Aqk"], do, _TN) + _dot(c["kte"], dS1)
        dAqk = jnp.where(c["causal"], _dot(do, vnew, _NT), 0.0)
        dqd = _dot(do, S0, _NT)
        dkte = _dot(vnew, dS1, _NT)
        dgl = total(dS1 * S0) * c["egl"]
        dw = -_dot(dvnew, S0, _NT)
        dS[...] = dS1 * c["egl"] + _dot(c["qd"], do, _TN) - _dot(w, dvnew, _TN)

        dTm = _dot(dvnew, c["vb"], _NT) + _dot(dw, c["kbg"], _NT)
        dvb = _dot(Tm, dvnew, _TN)
        dkbg = _dot(Tm, dw, _TN)
        dL = jnp.where(c["strict"], -_mm3(_mm3(Tm, dTm, _TN), Tm, _NT), 0.0)
        dP = dL * c["decay"]
        dQ = dAqk * c["decay"]
        M = dL * c["L"] + dAqk * c["Aqk"]
        dkb = _dot(dP, k) + dkbg * c["eg"]
        dk = _dot(dP, c["kb"], _TN) + _dot(dQ, q, _TN) + dkte * c["ektg"] + dkb * beta
        dq = _dot(dQ, k) + dqd * c["eg"]
        tk = _rowsum(dkte * c["kte"])
        dgc = (_rowsum(M) - _rowsum(_t(M)) + _rowsum(dqd * c["qd"]) - tk + _rowsum(dkbg * c["kbg"]))
        dgl = dgl + total(tk)
        dgc = jnp.broadcast_to(dgc, q.shape) + jnp.where(_iota2((C, HEAD_DIM), 0) == C - 1, dgl, 0.0)
        dv = dvb * beta
        dbeta = jnp.broadcast_to(_rowsum(dkb * k) + _rowsum(dvb * v), q.shape)
        for g in range(G):
            sl = slice(g * HEAD_DIM, (g + 1) * HEAD_DIM)
            dz_ref[:, sl] = dz[g]
            dq_ref[:, sl] = dq[g]
            dk_ref[:, sl] = dk[g]
            dv_ref[:, sl] = dv[g]
            dgc_ref[:, sl] = dgc[g]
            dbeta_ref[:, sl] = dbeta[g]

    rev = lambda off: pl.BlockSpec((C, GW), lambda h, n: (N - 1 - n, off + h))
    mat = pl.BlockSpec((G, None, C, C), lambda h, n: (h, N - 1 - n, 0, 0))
    vec = pl.BlockSpec((1, HEAD_DIM), lambda h, n: (0, 0))
    return pl.pallas_call(
        body, name="dn_scan_bwd", grid=(H // G, N),
        in_specs=[rev(0), rev(0), rev(0), rev(0), rev(0), rev(zb0), vec, rev(0), mat, mat, rev(mb0)],
        out_specs=[rev(0)] * 6 + [vec],
        out_shape=[_sds((T, W), F32)] * 5 + [_sds((T, W), BF16), _sds((1, HEAD_DIM), F32)],
        scratch_shapes=[pltpu.VMEM((G, HEAD_DIM, HEAD_DIM), F32)],
        compiler_params=_params(("arbitrary", "arbitrary"), 40 * 2**20),
    )(qn, kn, vn, gc_full, beta_full, proj, gain, o_raw, tm_all, s_all, dmix)


def _sb_block(q, kj, ahead, first_key):
    z = _dot(q, kj, _NT) * (HEAD_DIM ** -0.5)
    valid = ahead < -first_key
    lb = jnp.minimum(z, 0.0) - jnp.log(1.0 + jnp.exp(-jnp.abs(z)))
    return valid, lb, jnp.where(valid, lb - z, 0.0)


def _sb_attention(qkv, gain, tq_cap=1024):
    T = qkv.shape[0]
    H = SB_HEADS
    B = min(SB_KEYS, T)
    TQ = _tile(T, tq_cap, B)
    per = TQ // B

    def body(q_ref, k_ref, v_ref, gain_ref, o_ref, mix_ref, ltot_ref):
        i = pl.program_id(1)
        q = q_ref[...]
        upper = (_iota2((B, B), 0) > _iota2((B, B), 1)).astype(BF16)
        ahead = _iota2((TQ, B), 1) - _iota2((TQ, B), 0)
        last = (i + 1) * per - 1

        def step(jj, carry):
            acc, R = carry
            j = last - jj
            rows = pl.ds(pl.multiple_of(j * B, B), B)
            valid, lb, l1m = _sb_block(q, k_ref[rows, :], ahead, j * B - i * TQ)
            att = jnp.where(valid, jnp.exp(lb + R + _dot01(l1m, upper, passes=2)), 0.0)
            return acc + _dot(att, v_ref[rows, :]), R + _rowsum(l1m)

        acc, R = lax.fori_loop(0, last + 1, step, (jnp.zeros((TQ, HEAD_DIM), F32), jnp.zeros((TQ, 1), F32)))
        o_ref[...] = acc
        mix_ref[...] = _rms(acc, gain_ref[...]).astype(BF16)
        ltot_ref[...] = jnp.broadcast_to(R, (TQ, HEAD_DIM))

    head = lambda off: pl.BlockSpec((T, HEAD_DIM), lambda h, i: (0, off + h))
    blk = pl.BlockSpec((TQ, HEAD_DIM), lambda h, i: (i, h))
    return pl.pallas_call(
        body, name="sb_attention", grid=(H, T // TQ),
        in_specs=[blk, head(H), head(2 * H), pl.BlockSpec((1, HEAD_DIM), lambda h, i: (0, 0))],
        out_specs=[blk, blk, blk],
        out_shape=[_sds((T, H * HEAD_DIM), F32), _sds((T, H * HEAD_DIM), BF16), _sds((T, H * HEAD_DIM), F32)],
        compiler_params=_params(("parallel", "arbitrary"), 8 * _nbytes((T, HEAD_DIM), BF16) + 8 * 2**20),
    )(qkv, qkv, qkv, gain)


def _sb_attention_bwd(qkv, gain, o_raw, ltot, dmix, tq_cap=1024):
    T = qkv.shape[0]
    H = SB_HEADS
    B = min(SB_KEYS, T)
    TQ = _tile(T, tq_cap, B)
    per = TQ // B
    scale = HEAD_DIM ** -0.5

    def body(q_ref, k_ref, v_ref, gain_ref, o_ref, ltot_ref, dmix_ref, dq_ref, dk_ref, dv_ref, dgain_ref):
        i = pl.program_id(1)

        @pl.when(i == 0)
        def _():
            dk_ref[...] = jnp.zeros_like(dk_ref)
            dv_ref[...] = jnp.zeros_like(dv_ref)

        @pl.when((pl.program_id(0) == 0) & (i == 0))
        def _():
            dgain_ref[...] = jnp.zeros_like(dgain_ref)

        q = q_ref[...]
        o = o_ref[...]
        do, dgain = _rms_bwd(dmix_ref[...], o, gain_ref[...])
        dgain_ref[...] += dgain
        ltot = jnp.max(ltot_ref[...], axis=1, keepdims=True)
        do_b = do.astype(BF16)
        upto = (_iota2((B, B), 0) <= _iota2((B, B), 1)).astype(BF16)
        before = (_iota2((B, B), 0) < _iota2((B, B), 1)).astype(BF16)
        ahead = _iota2((TQ, B), 1) - _iota2((TQ, B), 0)

        def step(j, carry):
            dq, PL, PG = carry
            rows = pl.ds(pl.multiple_of(j * B, B), B)
            kj = k_ref[rows, :]
            valid, lb, l1m = _sb_block(q, kj, ahead, j * B - i * TQ)
            att = jnp.where(valid, jnp.exp(lb + (ltot - PL - _dot01(l1m, upto, passes=2))), 0.0)
            sig = jnp.exp(lb)
            G = _dot(do_b, v_ref[rows, :], _NT) * att
            dv_ref[rows, :] += _dot(att, do_b, _TN)
            cum = PG + _dot01(G, before, passes=2)
            dz = jnp.where(valid, G * (1.0 - sig) - sig * cum, 0.0) * scale
            dk_ref[rows, :] += _dot(dz, q, _TN)
            return dq + _dot(dz, kj), PL + _rowsum(l1m), PG + _rowsum(G)

        zero = jnp.zeros((TQ, 1), F32)
        dq, _, _ = lax.fori_loop(0, (i + 1) * per, step, (jnp.zeros((TQ, HEAD_DIM), F32), zero, zero))
        dq_ref[...] = dq.astype(BF16)

    head = lambda off: pl.BlockSpec((T, HEAD_DIM), lambda h, i: (0, off + h))
    blk = pl.BlockSpec((TQ, HEAD_DIM), lambda h, i: (i, h))
    vec = pl.BlockSpec((1, HEAD_DIM), lambda h, i: (0, 0))
    return pl.pallas_call(
        body, name="sb_attention_bwd", grid=(H, T // TQ),
        in_specs=[blk, head(H), head(2 * H), vec, blk, blk, blk],
        out_specs=[blk, head(0), head(0), vec],
        out_shape=[_sds((T, H * HEAD_DIM), BF16), _sds((T, H * HEAD_DIM), F32), _sds((T, H * HEAD_DIM), F32),
                   _sds((1, HEAD_DIM), F32)],
        compiler_params=_params(("arbitrary", "arbitrary"), 8 * _nbytes((T, HEAD_DIM), F32) + 32 * _nbytes((TQ, B), F32)),
    )(qkv, qkv, qkv, gain, o_raw, ltot, dmix)


def _adamw_math(w, g, m, v):
    m = ADAM_B1 * m + (1.0 - ADAM_B1) * g
    v = ADAM_B2 * v + (1.0 - ADAM_B2) * (g * g)
    m_hat = m / (1.0 - ADAM_B1 ** ADAM_STEP)
    v_hat = v / (1.0 - ADAM_B2 ** ADAM_STEP)
    delta = -ADAM_LR * (m_hat / (jnp.sqrt(v_hat) + ADAM_EPS) + ADAM_WD * w)
    return delta, m, v


def _adamw_sharded(parts, w, m, v, name):
    R, C = w.shape
    tr = _tile(R, max(SUBLANES, (2**20 // (4 * C)) // SUBLANES * SUBLANES), SUBLANES)

    def body(p_ref, w_ref, m_ref, v_ref, g_ref, d_ref, nm_ref, nv_ref):
        g = p_ref[0].astype(F32)
        for d in range(1, N_DEV):
            g = g + p_ref[d].astype(F32)
        g_ref[...] = g
        d_ref[...], nm_ref[...], nv_ref[...] = _adamw_math(w_ref[...], g, m_ref[...], v_ref[...])

    blk = pl.BlockSpec((tr, C), lambda i: (i, 0))
    return pl.pallas_call(
        body, name=name, grid=(R // tr,),
        in_specs=[pl.BlockSpec((N_DEV, tr, C), lambda i: (0, i, 0)), blk, blk, blk],
        out_specs=[blk] * 4, out_shape=[_sds((R, C), F32)] * 4,
        compiler_params=_params(("parallel",), 40 * 2**20),
    )(parts, w, m, v)


def _adamw_packed(g, w, m, v):
    def body(g_ref, w_ref, m_ref, v_ref, d_ref, nm_ref, nv_ref):
        d_ref[...], nm_ref[...], nv_ref[...] = _adamw_math(w_ref[...], g_ref[...], m_ref[...], v_ref[...])

    return pl.pallas_call(body, name="adamw_packed", out_shape=[_sds(g.shape, F32)] * 3,
                          compiler_params=_params((), 16 * 2**20))(g, w, m, v)


def _my_place():
    x, y, c = lax.axis_index("x"), lax.axis_index("y"), lax.axis_index("c")
    return x, y, c


def _peer(place, k):
    x, y, c = place
    return (1 - x if k & 4 else x, 1 - y if k & 2 else y, 1 - c if k & 1 else c)


def _index(place):
    x, y, c = place
    return 4 * x + 2 * y + c


HBM_SPEC = pl.BlockSpec(memory_space=pltpu.HBM)


def _all_gather(block, name):
    R, C = block.shape

    def body(x_ref, out_ref, send_sems, recv_sems, local_sem):
        me = _my_place()
        sibling = _peer(me, 1)
        chips = [2, 4, 6]

        def copy(sem, origin, to, src=None):
            slot = out_ref.at[_index(origin)]
            return pltpu.make_async_remote_copy(
                src_ref=slot if src is None else src, dst_ref=slot, send_sem=send_sems.at[sem], recv_sem=recv_sems.at[sem],
                device_id=to, device_id_type=MESH)

        mine = pltpu.make_async_copy(x_ref, out_ref.at[_index(me)], local_sem)
        mine.start()
        first = [copy(0, me, sibling, src=x_ref)] + [copy(1 + n, me, _peer(me, k), src=x_ref) for n, k in enumerate(chips)]
        for cp in first:
            cp.start()
        passed = [copy(4 + n, _peer(me, k), sibling) for n, k in enumerate(chips)]
        for n, k in enumerate(chips):
            copy(1 + n, _peer(me, k), me).wait_recv()
            passed[n].start()
        copy(0, sibling, me).wait_recv()
        for n, k in enumerate(chips):
            copy(4 + n, _peer(sibling, k), me).wait_recv()
        for cp in first + passed:
            cp.wait_send()
        mine.wait()

    return pl.pallas_call(
        body, name=name, in_specs=[HBM_SPEC], out_specs=HBM_SPEC,
        out_shape=_sds((N_DEV, R, C), block.dtype),
        scratch_shapes=[pltpu.SemaphoreType.DMA((7,)), pltpu.SemaphoreType.DMA((7,)), pltpu.SemaphoreType.DMA],
    )(block)


SEM_SPEC = pl.BlockSpec(memory_space=pltpu.SEMAPHORE)
ANY_SPEC = pl.BlockSpec(memory_space=pl.ANY)
_EFFECT = pltpu.SideEffectType.DATAFLOW_SIDE_EFFECTING


def _spread_start(x, per_peer, name, after):
    R, C = x.shape[-2:]

    def body(x_ref, land_ref, after_ref, send_sems, recv_sems, x_thru, land_thru, token):
        me = _my_place()
        for k in range(1, N_DEV):
            to = _peer(me, k)
            pltpu.make_async_remote_copy(
                src_ref=x_ref.at[_index(to)] if per_peer else x_ref, dst_ref=land_ref.at[_index(me)],
                send_sem=send_sems.at[k - 1], recv_sem=recv_sems.at[k - 1], device_id=to, device_id_type=MESH).start()
        token[...] = jnp.zeros_like(token)

    land = lax.empty((N_DEV, R, C), x.dtype)
    send_sems, recv_sems, x_thru, land_thru, token = pl.pallas_call(
        body, name=name,
        out_shape=(pltpu.SemaphoreType.DMA((N_DEV - 1,)), pltpu.SemaphoreType.DMA((N_DEV - 1,)),
                   pltpu.HBM(x.shape, x.dtype), pltpu.HBM(land.shape, land.dtype), _sds((SUBLANES, LANES), F32)),
        in_specs=(HBM_SPEC, HBM_SPEC, ANY_SPEC),
        out_specs=(SEM_SPEC, SEM_SPEC, HBM_SPEC, HBM_SPEC, pl.BlockSpec(memory_space=pltpu.VMEM)),
        input_output_aliases={0: 2, 1: 3},
        compiler_params=pltpu.CompilerParams(has_side_effects=_EFFECT),
    )(pltpu.with_memory_space_constraint(x, pltpu.HBM), pltpu.with_memory_space_constraint(land, pltpu.HBM), after)
    return (send_sems, recv_sems, x_thru, land_thru), token


def _spread_wait(state, per_peer, name, after):
    send_sems, recv_sems, x_thru, land_thru = state

    def body(x_ref, land_ref, send_sems, recv_sems, after_ref, x_dead, got_ref):
        me = _my_place()
        for k in range(1, N_DEV):
            frm = _peer(me, k)
            copy = pltpu.make_async_remote_copy(
                src_ref=x_ref.at[_index(frm)] if per_peer else x_ref, dst_ref=land_ref.at[_index(frm)],
                send_sem=send_sems.at[k - 1], recv_sem=recv_sems.at[k - 1], device_id=frm, device_id_type=MESH)
            copy.wait_send()
            copy.wait_recv()

    x_back, got = pl.pallas_call(
        body, name=name,
        out_shape=(pltpu.HBM(x_thru.shape, x_thru.dtype), pltpu.HBM(land_thru.shape, land_thru.dtype)),
        in_specs=(HBM_SPEC, HBM_SPEC, SEM_SPEC, SEM_SPEC, ANY_SPEC), out_specs=(HBM_SPEC, HBM_SPEC),
        input_output_aliases={0: 0, 1: 1},
        compiler_params=pltpu.CompilerParams(has_side_effects=_EFFECT),
    )(x_thru, land_thru, send_sems, recv_sems, after)
    me = _index(_my_place())
    own = lax.dynamic_index_in_dim(x_back, me, axis=0, keepdims=True) if per_peer else x_back[None]
    return lax.dynamic_update_slice_in_dim(got, own, me, axis=0)


def _all_reduce_packed(vec):
    R, L = vec.shape

    def body(x_ref, out_ref, buf, send_sems, recv_sems):
        me = _my_place()
        buf[_index(me)] = x_ref[...]
        copies = []
        for k in range(1, N_DEV):
            to = _peer(me, k)
            cp = pltpu.make_async_remote_copy(
                src_ref=x_ref, dst_ref=buf.at[_index(me)],
                send_sem=send_sems.at[k - 1], recv_sem=recv_sems.at[k - 1], device_id=to, device_id_type=MESH)
            cp.start()
            copies.append(cp)
        for k in range(1, N_DEV):
            frm = _peer(me, k)
            pltpu.make_async_remote_copy(
                src_ref=x_ref, dst_ref=buf.at[_index(frm)],
                send_sem=send_sems.at[k - 1], recv_sem=recv_sems.at[k - 1], device_id=frm, device_id_type=MESH).wait_recv()
        for cp in copies:
            cp.wait_send()
        acc = buf[0]
        for d in range(1, N_DEV):
            acc = acc + buf[d]
        out_ref[...] = acc

    vm = pl.BlockSpec(memory_space=pltpu.VMEM)
    return pl.pallas_call(
        body, name="all_reduce_packed", in_specs=[vm], out_specs=vm, out_shape=_sds((R, L), F32),
        scratch_shapes=[pltpu.VMEM((N_DEV, R, L), F32), pltpu.SemaphoreType.DMA((7,)), pltpu.SemaphoreType.DMA((7,))],
        compiler_params=pltpu.CompilerParams(vmem_limit_bytes=32 * 2**20),
    )(vec)


def _pack(arrays):
    rows = []
    for a in arrays:
        f = a.reshape(-1).astype(F32)
        pad = (-f.shape[0]) % LANES
        rows.append(jnp.pad(f, (0, pad)).reshape(-1, LANES))
    out = jnp.concatenate(rows, axis=0)
    return jnp.pad(out, ((0, (-out.shape[0]) % SUBLANES), (0, 0)))


def _unpack(packed, shapes):
    out, r = [], 0
    for s in shapes:
        n = math.prod(s)
        nr = -(-n // LANES)
        out.append(packed[r:r + nr].reshape(-1)[:n].reshape(s))
        r += nr
    return out


def _col_blocks(g):
    R, C = g.shape
    return jnp.transpose(g.astype(BF16).reshape(R, N_DEV, C // N_DEV), (1, 0, 2))


def _row_blocks(g):
    R, C = g.shape
    return g.astype(BF16).reshape(N_DEV, R // N_DEV, C)


def kernel(x, w_in, sb_out_gain, dn_conv_w, dn_a_log, dn_dt_bias, dn_out_gain, w_out, ln_mix_pre, ln_mix_post, w_up, ffn_conv_w, ffn_conv_b, w_down, ln_ffn_pre, ln_ffn_post, loss_target, m_w_in, m_sb_out_gain, m_dn_conv_w, m_dn_a_log, m_dn_dt_bias, m_dn_out_gain, m_w_out, m_ln_mix_pre, m_ln_mix_post, m_w_up, m_ffn_conv_w, m_ffn_conv_b, m_w_down, m_ln_ffn_pre, m_ln_ffn_post, v_w_in, v_sb_out_gain, v_dn_conv_w, v_dn_a_log, v_dn_dt_bias, v_dn_out_gain, v_w_out, v_ln_mix_pre, v_ln_mix_post, v_w_up, v_ffn_conv_w, v_ffn_conv_b, v_w_down, v_ln_ffn_pre, v_ln_ffn_post):
    T, D = x.shape[1], x.shape[2]
    SBW = SB_HEADS * HEAD_DIM
    DNW = DN_HEADS * HEAD_DIM
    in_cols = 3 * SBW + 4 * DNW + 2 * DN_HEADS
    main_cols = 3 * SBW + 4 * DNW
    in_pad = main_cols + LANES
    qkv0, z0 = 3 * SBW, 3 * SBW + 3 * DNW
    gate_block = main_cols // LANES
    x2, tgt = x[0], loss_target[0]

    g_in = _all_gather(w_in[0].astype(BF16), "gather_w_in")
    small_w = _all_gather(_pack([dn_conv_w[0], ffn_conv_w[0]]), "gather_conv_w")
    st_out, tok = _spread_start(w_out[0].astype(BF16), False, "gather_w_out_start", g_in)
    st_up, tok = _spread_start(w_up[0].astype(BF16), False, "gather_w_up_start", tok)
    st_down, tok_gather = _spread_start(w_down[0].astype(BF16), False, "gather_w_down_start", tok)
    w_in_f = jnp.transpose(g_in, (1, 0, 2)).reshape(D, in_cols)
    w_in_f = jnp.pad(w_in_f, ((0, 0), (0, in_pad - in_cols)))
    parts = [_unpack(small_w[d], [dn_conv_w.shape[1:], ffn_conv_w.shape[1:]]) for d in range(N_DEV)]
    dn_cw = jnp.concatenate([p[0] for p in parts], axis=1)
    ffn_cw = jnp.concatenate([p[1] for p in parts], axis=1)
    lane_pad = lambda a, off: jnp.pad(a, ((0, 0), (off, LANES - off - a.shape[1])))
    a_log_l, dt_bias_l = lane_pad(dn_a_log, DN_HEADS), lane_pad(dn_dt_bias, DN_HEADS)

    xn = _norm_in(x2, ln_mix_pre)
    proj = _matmul(xn, w_in_f, "nn", F32, "proj_in", tm_cap=512, tn_cap=2432, tk_cap=1024, after=tok_gather)
    sb_qkv = proj[:, :3 * SBW].astype(BF16)
    o_sb, mix_sb, sb_ltot = _sb_attention(sb_qkv, sb_out_gain)
    qn = _dn_branch(proj, qkv0, dn_cw, 0, True, HEAD_DIM ** -0.5)
    kn = _dn_branch(proj, qkv0 + DNW, dn_cw, DNW, True, 1.0)
    vn = _dn_branch(proj, qkv0 + 2 * DNW, dn_cw, 2 * DNW, False, 1.0)
    gc_full, beta_full = _dn_gates(proj, gate_block, a_log_l, dt_bias_l)
    o_dn, mix_dn, tm_all, s_all = _dn_scan(qn, kn, vn, gc_full, beta_full, proj, z0, dn_out_gain)
    mix = jnp.concatenate([mix_sb, mix_dn], axis=1)
    w_out_f = _spread_wait(st_out, False, "gather_w_out_wait", mix).reshape(w_out.shape[1] * N_DEV, D)
    m = _matmul(mix, w_out_f, "nn", F32, "proj_out")
    h, hn = _mix_residual(x2, m, ln_mix_post, ln_ffn_pre)
    w_up_cut = _spread_wait(st_up, False, "gather_w_up_wait", hn)
    u = _matmul(hn, w_up_cut, "nn", F32, "ffn_up", tn_cap=w_up.shape[2], b_cut=True)
    act = _ffn_act(u, ffn_cw, ffn_conv_b)
    w_down_f = _spread_wait(st_down, False, "gather_w_down_wait", act).reshape(w_down.shape[1] * N_DEV, D)
    f = _matmul(act, w_down_f, "nn", F32, "ffn_down")
    dy, df, d_ln_ffn_post, loss_part = _loss_head(h, f, ln_ffn_post, tgt)

    d_w_down = _matmul(act, df, "tn", BF16, "grad_w_down")
    st_xd, tok = _spread_start(_row_blocks(d_w_down), True, "exchange_w_down_start", loss_part)
    da = _matmul(df, w_down_f, "nt", F32, "bwd_ffn_down", after=tok)
    du, d_ffn_cwb = _ffn_act_bwd(u, ffn_cw, ffn_conv_b, da)
    d_ffn_cwb = jnp.concatenate([d_ffn_cwb[0], d_ffn_cwb[1]], axis=1)
    d_w_up_cut = _matmul(hn, du, "tn", BF16, "grad_w_up", b_cut=True, out_cut=True)
    st_xu, tok = _spread_start(d_w_up_cut, True, "exchange_w_up_start", d_ffn_cwb)
    dhn = _matmul(du, w_up_cut, "nt", F32, "bwd_ffn_up", after=tok, a_cut=True, b_cut=True)
    dh, dm, d_ln_ffn_pre, d_ln_mix_post = _ffn_residual_bwd(dy, dhn, h, ln_ffn_pre, m, ln_mix_post)

    d_w_out = _matmul(mix, dm, "tn", BF16, "grad_w_out")
    st_xo, tok = _spread_start(_row_blocks(d_w_out), True, "exchange_w_out_start", d_ln_ffn_pre)
    dmix = _matmul(dm, w_out_f, "nt", F32, "bwd_proj_out", after=tok)
    dq_sb, dk_sb, dv_sb, d_sb_gain = _sb_attention_bwd(sb_qkv, sb_out_gain, o_sb, sb_ltot, dmix)
    dqn, dkn, dvn, dgc_full, dbeta_full, dz, d_dn_gain = _dn_scan_bwd(
        qn, kn, vn, gc_full, beta_full, proj, z0, dn_out_gain, o_dn, tm_all, s_all, dmix, SBW)
    du_q, dcw_q = _dn_branch_bwd(proj, qkv0, dn_cw, 0, True, HEAD_DIM ** -0.5, dqn)
    du_k, dcw_k = _dn_branch_bwd(proj, qkv0 + DNW, dn_cw, DNW, True, 1.0, dkn)
    du_v, dcw_v = _dn_branch_bwd(proj, qkv0 + 2 * DNW, dn_cw, 2 * DNW, False, 1.0, dvn)
    dba, d_a_log_l, d_dt_bias_l = _dn_gates_bwd(proj, gate_block, a_log_l, dt_bias_l, dgc_full, dbeta_full)
    dproj = jnp.concatenate([dq_sb, dk_sb.astype(BF16), dv_sb.astype(BF16), du_q, du_k, du_v, dz, dba], axis=1)
    d_w_in = _matmul(xn, dproj, "tn", BF16, "grad_w_in", tm_cap=512, tn_cap=2432, tk_cap=1024)[:, :in_cols]
    st_xi, tok = _spread_start(_col_blocks(d_w_in), True, "exchange_w_in_start", d_sb_gain)
    dxn = _matmul(dproj, w_in_f, "nt", F32, "bwd_proj_in", tk_cap=2432, after=tok)
    grad_x, d_ln_mix_pre = _input_bwd(dh, dxn, x2, ln_mix_pre)

    d_dn_cw = jnp.concatenate([dcw_q[:SHORT_CONV], dcw_k[:SHORT_CONV], dcw_v[:SHORT_CONV]], axis=1)
    small = [loss_part[:, :1], d_sb_gain, d_a_log_l[:, DN_HEADS:2 * DN_HEADS], d_dt_bias_l[:, DN_HEADS:2 * DN_HEADS], d_dn_gain,
             d_ln_mix_pre, d_ln_mix_post, d_ffn_cwb[FFN_CONV:FFN_CONV + 1], d_ln_ffn_pre, d_ln_ffn_post,
             d_dn_cw, d_ffn_cwb[:FFN_CONV]]
    shapes = [a.shape for a in small]
    red = _unpack(_all_reduce_packed(_pack(small)), shapes)
    loss = red[0].reshape(())
    me = _index(_my_place())
    g_dn_cw = lax.dynamic_slice_in_dim(red[10], me * dn_conv_w.shape[2], dn_conv_w.shape[2], axis=1)
    g_ffn_cw = lax.dynamic_slice_in_dim(red[11], me * ffn_conv_w.shape[2], ffn_conv_w.shape[2], axis=1)
    names = ["sb_out_gain", "dn_conv_w", "dn_a_log", "dn_dt_bias", "dn_out_gain", "ln_mix_pre", "ln_mix_post",
             "ffn_conv_w", "ffn_conv_b", "ln_ffn_pre", "ln_ffn_post"]
    g_small = dict(sb_out_gain=red[1], dn_conv_w=g_dn_cw[None], dn_a_log=red[2], dn_dt_bias=red[3], dn_out_gain=red[4],
                   ln_mix_pre=red[5], ln_mix_post=red[6], ffn_conv_w=g_ffn_cw[None], ffn_conv_b=red[7],
                   ln_ffn_pre=red[8], ln_ffn_post=red[9])
    w_small = dict(sb_out_gain=sb_out_gain, dn_conv_w=dn_conv_w, dn_a_log=dn_a_log, dn_dt_bias=dn_dt_bias,
                   dn_out_gain=dn_out_gain, ln_mix_pre=ln_mix_pre, ln_mix_post=ln_mix_post, ffn_conv_w=ffn_conv_w,
                   ffn_conv_b=ffn_conv_b, ln_ffn_pre=ln_ffn_pre, ln_ffn_post=ln_ffn_post)
    m_small = dict(sb_out_gain=m_sb_out_gain, dn_conv_w=m_dn_conv_w, dn_a_log=m_dn_a_log, dn_dt_bias=m_dn_dt_bias,
                   dn_out_gain=m_dn_out_gain, ln_mix_pre=m_ln_mix_pre, ln_mix_post=m_ln_mix_post, ffn_conv_w=m_ffn_conv_w,
                   ffn_conv_b=m_ffn_conv_b, ln_ffn_pre=m_ln_ffn_pre, ln_ffn_post=m_ln_ffn_post)
    v_small = dict(sb_out_gain=v_sb_out_gain, dn_conv_w=v_dn_conv_w, dn_a_log=v_dn_a_log, dn_dt_bias=v_dn_dt_bias,
                   dn_out_gain=v_dn_out_gain, ln_mix_pre=v_ln_mix_pre, ln_mix_post=v_ln_mix_post, ffn_conv_w=v_ffn_conv_w,
                   ffn_conv_b=v_ffn_conv_b, ln_ffn_pre=v_ln_ffn_pre, ln_ffn_post=v_ln_ffn_post)
    sshapes = [w_small[n].shape for n in names]
    upd = _adamw_packed(_pack([g_small[n] for n in names]), _pack([w_small[n] for n in names]),
                        _pack([m_small[n] for n in names]), _pack([v_small[n] for n in names]))
    d_small, nm_small, nv_small = [dict(zip(names, _unpack(p, sshapes))) for p in upd]

    big = {}
    after = grad_x
    for n, st, w_, m_, v_ in [("w_down", st_xd, w_down, m_w_down, v_w_down), ("w_up", st_xu, w_up, m_w_up, v_w_up),
                              ("w_out", st_xo, w_out, m_w_out, v_w_out), ("w_in", st_xi, w_in, m_w_in, v_w_in)]:
        got = _spread_wait(st, True, "exchange_" + n + "_wait", after)
        big[n] = [a[None] for a in _adamw_sharded(got, w_[0], m_[0], v_[0], "adamw_" + n)]
        after = big[n][1]

    order = ["w_in", "sb_out_gain", "dn_conv_w", "dn_a_log", "dn_dt_bias", "dn_out_gain", "w_out", "ln_mix_pre",
             "ln_mix_post", "w_up", "ffn_conv_w", "ffn_conv_b", "w_down", "ln_ffn_pre", "ln_ffn_post"]
    pick = lambda n, i: big[n][i] if n in big else [g_small, d_small, nm_small, nv_small][i][n].reshape(w_small[n].shape)
    return (loss, grad_x[None], *[pick(n, 0) for n in order], *[pick(n, 1) for n in order],
            *[pick(n, 2) for n in order], *[pick(n, 3) for n in order])
```

```python
import functools
import math

import jax
import jax.numpy as jnp
from jax import lax
from jax.experimental import pallas as pl
from jax.experimental.pallas import tpu as pltpu

F32 = jnp.float32
BF16 = jnp.bfloat16

N_DEV = 8
HEAD_DIM = 128
SB_HEADS = 8
DN_HEADS = 8
DN_CHUNK = 128
DN_GROUP = 8
INV_BLOCK = 16
SB_KEYS = 256
SHORT_CONV = 4
FFN_CONV = 3
EPS = 1e-6
LANES = 128
SUBLANES = 8
VMEM_CAP = 56 * 2**20

ADAM_LR = 0.001
ADAM_B1 = 0.9
ADAM_B2 = 0.999
ADAM_EPS = 1e-08
ADAM_WD = 0.01
ADAM_STEP = 10

MESH = pl.DeviceIdType.MESH

assert HEAD_DIM == DN_CHUNK == LANES


def _tile(n, cap, mult):
    if n <= cap:
        return n
    t = (cap // mult) * mult
    while t >= mult:
        if n % t == 0:
            return t
        t -= mult
    raise ValueError(f"no tile for {n} under {cap} in multiples of {mult}")


def _params(sem, vmem_bytes):
    limit = int(min(VMEM_CAP, max(vmem_bytes, 16 * 2**20)))
    if not sem:
        return pltpu.CompilerParams(vmem_limit_bytes=limit)
    return pltpu.CompilerParams(dimension_semantics=sem, vmem_limit_bytes=limit)


def _nbytes(shape, dtype):
    return math.prod(shape) * jnp.dtype(dtype).itemsize


_NN = (((1,), (0,)), ((), ()))
_NT = (((1,), (1,)), ((), ()))
_TN = (((0,), (0,)), ((), ()))


def _batched(dims, ndim):
    if ndim == 2:
        return dims
    (ca,), (cb,) = dims[0]
    return (((ca + 1,), (cb + 1,)), ((0,), (0,)))


def _dot(a, b, dims=_NN):
    return lax.dot_general(a.astype(BF16), b.astype(BF16), _batched(dims, a.ndim), preferred_element_type=F32)


def _split2(x):
    hi = x.astype(BF16)
    lo = (x - hi.astype(F32)).astype(BF16)
    return hi, lo


def _split3(x):
    hi = x.astype(BF16)
    r = x - hi.astype(F32)
    mid = r.astype(BF16)
    lo = (r - mid.astype(F32)).astype(BF16)
    return hi, mid, lo


def _dot01(x, m01, passes=3):
    parts = _split3(x) if passes == 3 else _split2(x)
    out = None
    for p in parts:
        t = lax.dot_general(p, m01, _NN, preferred_element_type=F32)
        out = t if out is None else out + t
    return out


def _dot01_left(m01, x, passes=3):
    parts = _split3(x) if passes == 3 else _split2(x)
    out = None
    for p in parts:
        t = lax.dot_general(m01, p, _NN, preferred_element_type=F32)
        out = t if out is None else out + t
    return out


def _mm3(a, b, dims=_NN):
    ah, al = _split2(a)
    bh, bl = _split2(b)
    d = functools.partial(lax.dot_general, dimension_numbers=_batched(dims, a.ndim), preferred_element_type=F32)
    return d(ah, bh) + (d(ah, bl) + d(al, bh))


def _rowsum(x):
    return jnp.sum(x, axis=-1, keepdims=True)


def _t(x):
    return jnp.swapaxes(x, -1, -2)


def _sigmoid(x):
    return 1.0 / (1.0 + jnp.exp(-x))


def _softplus(x):
    return jnp.maximum(x, 0.0) + jnp.log(1.0 + jnp.exp(-jnp.abs(x)))


def _silu(x):
    return x * _sigmoid(x)


def _silu_grad(x):
    s = _sigmoid(x)
    return s * (1.0 + x * (1.0 - s))


_GELU_C = math.sqrt(2.0 / math.pi)


def _gelu(x):
    return 0.5 * x * (1.0 + jnp.tanh(_GELU_C * (x + 0.044715 * x * x * x)))


def _gelu_and_grad(x):
    x2 = x * x
    th = jnp.tanh(_GELU_C * (x + 0.044715 * x2 * x))
    half = 0.5 * (1.0 + th)
    return x * half, half + 0.5 * x * (1.0 - th * th) * (_GELU_C * (1.0 + 3.0 * 0.044715 * x2))


def _rms(x, g):
    r = lax.rsqrt(jnp.mean(x * x, axis=-1, keepdims=True) + EPS)
    return x * r * g


def _rms_bwd(dy, x, g):
    r = lax.rsqrt(jnp.mean(x * x, axis=-1, keepdims=True) + EPS)
    xh = x * r
    gdy = dy * g
    dx = r * (gdy - xh * jnp.mean(gdy * xh, axis=-1, keepdims=True))
    return dx, jnp.sum(dy * xh, axis=-2, keepdims=True)


def _iota2(shape, axis):
    return lax.broadcasted_iota(jnp.int32, shape, axis)


def _shift_down(cur, prev8, k):
    n = cur.shape[0]
    r = pltpu.roll(cur, k, 0)
    pr = pltpu.roll(prev8, k, 0)
    head = jnp.where(_iota2(pr.shape, 0) < k, pr, r[0:SUBLANES])
    if n == SUBLANES:
        return head
    return jnp.concatenate([head, r[SUBLANES:]], axis=0)


def _shift_up(cur, next8, k):
    n = cur.shape[0]
    r = pltpu.roll(cur, n - k, 0)
    nr = pltpu.roll(next8, SUBLANES - k, 0)
    tail = jnp.where(_iota2(nr.shape, 0) >= SUBLANES - k, nr, r[n - SUBLANES:])
    if n == SUBLANES:
        return tail
    return jnp.concatenate([r[:n - SUBLANES], tail], axis=0)


def _causal_conv(cur, prev8, w_ref, taps):
    out = cur * w_ref[taps - 1:taps, :]
    for j in range(taps - 1):
        out = out + _shift_down(cur, prev8, taps - 1 - j) * w_ref[j:j + 1, :]
    return out


def _anti_conv(cur, next8, w_ref, taps):
    out = cur * w_ref[taps - 1:taps, :]
    for j in range(taps - 1):
        out = out + _shift_up(cur, next8, taps - 1 - j) * w_ref[j:j + 1, :]
    return out


def _matmul(a, b, mode, out_dtype, name, tm_cap=1024, tn_cap=1024, tk_cap=2048, after=None,
            a_cut=False, b_cut=False, out_cut=False):
    a_shard = a.shape[2] if a_cut else None
    b_shard = b.shape[2] if b_cut else None
    a_full = (a.shape[1], a.shape[0] * a_shard) if a_cut else a.shape
    b_full = (b.shape[1], b.shape[0] * b_shard) if b_cut else b.shape
    assert not (a_cut and mode == "tn")
    if mode == "nn":
        (M, K), N = a_full, b_full[1]
    elif mode == "nt":
        (M, K), N = a_full, b_full[0]
    else:
        (K, M), N = a_full, b_full[1]
    n_unit = b_shard if (b_cut and mode != "nt") else N
    k_unit = math.gcd(a_shard or K, b_shard if (b_cut and mode == "nt") else K)
    tm = _tile(M, tm_cap, LANES)
    tn = N // N_DEV if out_cut else _tile(n_unit, tn_cap, LANES)
    tk = _tile(k_unit, tk_cap, LANES)
    assert n_unit % tn == 0 and k_unit % tk == 0
    nk = K // tk
    dims = {"nn": _NN, "nt": _NT, "tn": _TN}[mode]
    if a_cut:
        pa = a_shard // tk
        a_spec = pl.BlockSpec((None, tm, tk), lambda i, j, k: (k // pa, i, k % pa))
    elif mode == "tn":
        a_spec = pl.BlockSpec((tk, tm), lambda i, j, k: (k, i))
    else:
        a_spec = pl.BlockSpec((tm, tk), lambda i, j, k: (i, k))
    if b_cut and mode == "nt":
        pb = b_shard // tk
        b_spec = pl.BlockSpec((None, tn, tk), lambda i, j, k: (k // pb, j, k % pb))
    elif b_cut:
        pb = b_shard // tn
        b_spec = pl.BlockSpec((None, tk, tn), lambda i, j, k: (j // pb, k, j % pb))
    elif mode == "nt":
        b_spec = pl.BlockSpec((tn, tk), lambda i, j, k: (j, k))
    else:
        b_spec = pl.BlockSpec((tk, tn), lambda i, j, k: (k, j))
    if out_cut:
        out_spec, out_shape = pl.BlockSpec((None, tm, tn), lambda i, j, k: (j, i, 0)), (N_DEV, M, tn)
    else:
        out_spec, out_shape = pl.BlockSpec((tm, tn), lambda i, j, k: (i, j)), (M, N)

    def body(a_ref, b_ref, *rest):
        o_ref, acc_ref = rest[-2:]
        k = pl.program_id(2)

        @pl.when(k == 0)
        def _():
            acc_ref[...] = jnp.zeros_like(acc_ref)

        acc_ref[...] += lax.dot_general(a_ref[...], b_ref[...], dims, preferred_element_type=F32)

        @pl.when(k == nk - 1)
        def _():
            o_ref[...] = acc_ref[...].astype(o_ref.dtype)

    vmem = 2 * (_nbytes((tm, tk), a.dtype) + _nbytes((tk, tn), b.dtype) + _nbytes((tm, tn), out_dtype)) + _nbytes((tm, tn), F32)
    tokens = [] if after is None else [after]
    return pl.pallas_call(
        body, name=name, grid=(M // tm, N // tn, nk),
        in_specs=[a_spec, b_spec] + [pl.BlockSpec(t.shape, lambda i, j, k: (0, 0)) for t in tokens],
        out_specs=out_spec,
        out_shape=jax.ShapeDtypeStruct(out_shape, out_dtype),
        scratch_shapes=[pltpu.VMEM((tm, tn), F32)],
        compiler_params=_params(("parallel", "parallel", "arbitrary"), vmem + 4 * 2**20),
    )(a, b, *tokens)


def _row_call(body, name, T, D, ins, outs, tr, acc_outs=()):
    def spec(a, kind):
        if kind == "row":
            return pl.BlockSpec((tr, a.shape[1]), lambda i: (i, 0))
        return pl.BlockSpec(a.shape, lambda i: (0, 0))
    in_specs = [spec(a, k) for a, k in ins]
    out_specs = [spec(a, k) for a, k in outs] + [spec(a, "vec") for a in acc_outs]
    out_shape = [a for a, _ in outs] + list(acc_outs)
    vmem = 2 * sum(_nbytes((tr, a.shape[1]) if k == "row" else a.shape, a.dtype) for a, k in list(ins) + list(outs))
    return pl.pallas_call(
        body, name=name, grid=(T // tr,), in_specs=in_specs, out_specs=out_specs, out_shape=out_shape,
        compiler_params=_params(("arbitrary",), 3 * vmem + 8 * 2**20),
    )(*[a for a, _ in ins])


def _sds(shape, dtype):
    return jax.ShapeDtypeStruct(shape, dtype)


def _accumulate(ref, val):
    @pl.when(pl.program_id(0) == 0)
    def _():
        ref[...] = jnp.zeros_like(ref)
    ref[...] += val


def _norm_in(x, g):
    T, D = x.shape

    def body(x_ref, g_ref, o_ref):
        o_ref[...] = _rms(x_ref[...], g_ref[...]).astype(BF16)

    return _row_call(body, "norm_in", T, D, [(x, "row"), (g, "vec")], [(_sds((T, D), BF16), "row")], _tile(T, 256, 16))[0]


def _mix_residual(x, m, g_post, g_pre):
    T, D = x.shape

    def body(x_ref, m_ref, gp_ref, gn_ref, h_ref, hn_ref):
        h = x_ref[...] + _rms(m_ref[...], gp_ref[...])
        h_ref[...] = h
        hn_ref[...] = _rms(h, gn_ref[...]).astype(BF16)

    return _row_call(body, "mix_residual", T, D, [(x, "row"), (m, "row"), (g_post, "vec"), (g_pre, "vec")],
                     [(_sds((T, D), F32), "row"), (_sds((T, D), BF16), "row")], _tile(T, 256, 16))


def _loss_head(h, f, g_post, target):
    T, D = h.shape

    def body(h_ref, f_ref, g_ref, t_ref, dy_ref, df_ref, dg_ref, loss_ref):
        f = f_ref[...]
        g = g_ref[...]
        diff = h_ref[...] + _rms(f, g) - t_ref[...]
        dy = diff * (1.0 / D)
        dy_ref[...] = dy
        df, dg = _rms_bwd(dy, f, g)
        df_ref[...] = df.astype(BF16)
        _accumulate(dg_ref, dg)
        _accumulate(loss_ref, jnp.full((1, LANES), 0.5 / D, F32) * jnp.sum(diff * diff))

    return _row_call(body, "loss_head", T, D, [(h, "row"), (f, "row"), (g_post, "vec"), (target, "row")],
                     [(_sds((T, D), F32), "row"), (_sds((T, D), BF16), "row")], _tile(T, 256, 16),
                     acc_outs=[_sds((1, D), F32), _sds((1, LANES), F32)])


def _ffn_residual_bwd(dy, dhn, h, g_pre, m, g_post):
    T, D = h.shape

    def body(dy_ref, dhn_ref, h_ref, gn_ref, m_ref, gp_ref, dh_ref, dm_ref, dgn_ref, dgp_ref):
        dhh, dgn = _rms_bwd(dhn_ref[...], h_ref[...], gn_ref[...])
        dh = dy_ref[...] + dhh
        dh_ref[...] = dh
        dm, dgp = _rms_bwd(dh, m_ref[...], gp_ref[...])
        dm_ref[...] = dm.astype(BF16)
        _accumulate(dgn_ref, dgn)
        _accumulate(dgp_ref, dgp)

    return _row_call(body, "ffn_residual_bwd", T, D,
                     [(dy, "row"), (dhn, "row"), (h, "row"), (g_pre, "vec"), (m, "row"), (g_post, "vec")],
                     [(_sds((T, D), F32), "row"), (_sds((T, D), BF16), "row")], _tile(T, 128, 16),
                     acc_outs=[_sds((1, D), F32), _sds((1, D), F32)])


def _input_bwd(dh, dxn, x, g):
    T, D = x.shape

    def body(dh_ref, dxn_ref, x_ref, g_ref, dx_ref, dg_ref):
        dx, dg = _rms_bwd(dxn_ref[...], x_ref[...], g_ref[...])
        dx_ref[...] = dh_ref[...] + dx
        _accumulate(dg_ref, dg)

    return _row_call(body, "input_bwd", T, D, [(dh, "row"), (dxn, "row"), (x, "row"), (g, "vec")],
                     [(_sds((T, D), F32), "row")], _tile(T, 256, 16), acc_outs=[_sds((1, D), F32)])


def _ffn_act(u, conv_w, conv_b):
    T, F2 = u.shape
    F = F2 // 2
    tc = _tile(F, 512, LANES)
    tr = _tile(T, 512, SUBLANES)
    nc = F // tc
    r8 = tr // SUBLANES

    def body(ug_ref, ugp_ref, uv_ref, uvp_ref, wg_ref, wv_ref, bg_ref, bv_ref, a_ref):
        first = pl.program_id(1) == 0
        cg = _causal_conv(ug_ref[...], jnp.where(first, 0.0, ugp_ref[...]), wg_ref, FFN_CONV) + bg_ref[...]
        cv = _causal_conv(uv_ref[...], jnp.where(first, 0.0, uvp_ref[...]), wv_ref, FFN_CONV) + bv_ref[...]
        a_ref[...] = (_gelu(cg) * cv).astype(BF16)

    cur = lambda off: pl.BlockSpec((tr, tc), lambda j, i: (i, j + off))
    prev = lambda off: pl.BlockSpec((SUBLANES, tc), lambda j, i: (jnp.maximum(i * r8 - 1, 0), j + off))
    wsp = lambda off: pl.BlockSpec((FFN_CONV, tc), lambda j, i: (0, j + off))
    bsp = lambda off: pl.BlockSpec((1, tc), lambda j, i: (0, j + off))
    return pl.pallas_call(
        body, name="ffn_act", grid=(nc, T // tr),
        in_specs=[cur(0), prev(0), cur(nc), prev(nc), wsp(0), wsp(nc), bsp(0), bsp(nc)],
        out_specs=pl.BlockSpec((tr, tc), lambda j, i: (i, j)),
        out_shape=_sds((T, F), BF16),
        compiler_params=_params(("parallel", "arbitrary"), 12 * _nbytes((tr, tc), F32) + 8 * 2**20),
    )(u, u, u, u, conv_w, conv_w, conv_b, conv_b)


def _ffn_act_bwd(u, conv_w, conv_b, da):
    T, F2 = u.shape
    F = F2 // 2
    tc = _tile(F, 512, LANES)
    tr = _tile(T, 512, SUBLANES)
    nc = F // tc
    r8 = tr // SUBLANES
    n8 = T // SUBLANES
    K = FFN_CONV

    def body(ug_ref, ugp_ref, ugn_ref, uv_ref, uvp_ref, uvn_ref, da_ref, dan_ref,
             wg_ref, wv_ref, bg_ref, bv_ref, du_ref, dwb_ref):
        i = pl.program_id(1)
        first = i == 0
        last = i == pl.num_programs(1) - 1

        def dconv(ug, ug_prev, uv, uv_prev, da_):
            cg = _causal_conv(ug, ug_prev, wg_ref, K) + bg_ref[...]
            cv = _causal_conv(uv, uv_prev, wv_ref, K) + bv_ref[...]
            act, act_grad = _gelu_and_grad(cg)
            return da_ * cv * act_grad, da_ * act

        ug, uv = ug_ref[...], uv_ref[...]
        ug_prev, uv_prev = jnp.where(first, 0.0, ugp_ref[...]), jnp.where(first, 0.0, uvp_ref[...])
        dcg, dcv = dconv(ug, ug_prev, uv, uv_prev, da_ref[...])
        dcgn, dcvn = dconv(ugn_ref[...], ug[tr - SUBLANES:], uvn_ref[...], uv[tr - SUBLANES:], dan_ref[...])
        du_ref[0] = _anti_conv(dcg, jnp.where(last, 0.0, dcgn), wg_ref, K).astype(BF16)
        du_ref[1] = _anti_conv(dcv, jnp.where(last, 0.0, dcvn), wv_ref, K).astype(BF16)

        @pl.when(first)
        def _():
            dwb_ref[...] = jnp.zeros_like(dwb_ref)

        for half, (dc, uo, uo_prev) in enumerate([(dcg, ug, ug_prev), (dcv, uv, uv_prev)]):
            rows = [jnp.sum(dc * _shift_down(uo, uo_prev, K - 1 - t), axis=0, keepdims=True) for t in range(K - 1)]
            rows += [jnp.sum(dc * uo, axis=0, keepdims=True), jnp.sum(dc, axis=0, keepdims=True)]
            rows += [jnp.zeros_like(rows[0])] * (SUBLANES - len(rows))
            dwb_ref[half] += jnp.concatenate(rows, axis=0)

    cur = lambda off: pl.BlockSpec((tr, tc), lambda j, i: (i, j + off))
    prev = lambda off: pl.BlockSpec((SUBLANES, tc), lambda j, i: (jnp.maximum(i * r8 - 1, 0), j + off))
    nxt = lambda off: pl.BlockSpec((SUBLANES, tc), lambda j, i: (jnp.minimum((i + 1) * r8, n8 - 1), j + off))
    wsp = lambda off: pl.BlockSpec((K, tc), lambda j, i: (0, j + off))
    bsp = lambda off: pl.BlockSpec((1, tc), lambda j, i: (0, j + off))
    return pl.pallas_call(
        body, name="ffn_act_bwd", grid=(nc, T // tr),
        in_specs=[cur(0), prev(0), nxt(0), cur(nc), prev(nc), nxt(nc), cur(0), nxt(0), wsp(0), wsp(nc), bsp(0), bsp(nc)],
        out_specs=[pl.BlockSpec((2, tr, tc), lambda j, i: (0, i, j)), pl.BlockSpec((2, SUBLANES, tc), lambda j, i: (0, 0, j))],
        out_shape=[_sds((2, T, F), BF16), _sds((2, SUBLANES, F), F32)],
        compiler_params=_params(("parallel", "arbitrary"), 24 * _nbytes((tr, tc), F32) + 8 * 2**20),
    )(u, u, u, u, u, u, da, da, conv_w, conv_w, conv_b, conv_b)


def _l2norm(s, scale):
    return s * (lax.rsqrt(jnp.sum(s * s, axis=-1, keepdims=True) + EPS) * scale)


def _dn_branch(proj, col0, conv_w, wcol0, l2, scale):
    T = proj.shape[0]
    W = DN_HEADS * HEAD_DIM
    tr = _tile(T, 2048, SUBLANES)
    r8 = tr // SUBLANES
    cb0, wb0 = col0 // HEAD_DIM, wcol0 // HEAD_DIM

    def body(u_ref, up_ref, w_ref, o_ref):
        first = pl.program_id(1) == 0
        s = _silu(_causal_conv(u_ref[...], jnp.where(first, 0.0, up_ref[...]), w_ref, SHORT_CONV))
        o_ref[...] = _l2norm(s, scale) if l2 else s

    return pl.pallas_call(
        body, name=f"dn_branch_{col0}", grid=(DN_HEADS, T // tr),
        in_specs=[pl.BlockSpec((tr, HEAD_DIM), lambda h, i: (i, cb0 + h)),
                  pl.BlockSpec((SUBLANES, HEAD_DIM), lambda h, i: (jnp.maximum(i * r8 - 1, 0), cb0 + h)),
                  pl.BlockSpec((SHORT_CONV, HEAD_DIM), lambda h, i: (0, wb0 + h))],
        out_specs=pl.BlockSpec((tr, HEAD_DIM), lambda h, i: (i, h)),
        out_shape=_sds((T, W), F32),
        compiler_params=_params(("parallel", "arbitrary"), 32 * _nbytes((tr, HEAD_DIM), F32) + 8 * 2**20),
    )(proj, proj, conv_w)


def _dn_branch_bwd(proj, col0, conv_w, wcol0, l2, scale, dy):
    T = proj.shape[0]
    W = DN_HEADS * HEAD_DIM
    tr = _tile(T, 2048, SUBLANES)
    r8 = tr // SUBLANES
    n8 = T // SUBLANES
    cb0, wb0 = col0 // HEAD_DIM, wcol0 // HEAD_DIM
    K = SHORT_CONV

    def body(u_ref, up_ref, un_ref, dy_ref, dyn_ref, w_ref, du_ref, dw_ref):
        i = pl.program_id(1)
        first = i == 0
        last = i == pl.num_programs(1) - 1

        def dconv(u, u_prev, dy_):
            c = _causal_conv(u, u_prev, w_ref, K)
            if l2:
                s = _silu(c)
                r = lax.rsqrt(jnp.sum(s * s, axis=-1, keepdims=True) + EPS)
                n = s * r
                ds = (scale * r) * (dy_ - n * jnp.sum(dy_ * n, axis=-1, keepdims=True))
            else:
                ds = dy_
            return ds * _silu_grad(c)

        u = u_ref[...]
        u_prev = jnp.where(first, 0.0, up_ref[...])
        dc = dconv(u, u_prev, dy_ref[...])
        dcn = jnp.where(last, 0.0, dconv(un_ref[...], u[tr - SUBLANES:], dyn_ref[...]))
        du_ref[...] = _anti_conv(dc, dcn, w_ref, K).astype(BF16)
        rows = [jnp.sum(dc * _shift_down(u, u_prev, K - 1 - t), axis=0, keepdims=True) for t in range(K - 1)]
        rows += [jnp.sum(dc * u, axis=0, keepdims=True)]
        rows += [jnp.zeros_like(rows[0])] * (SUBLANES - len(rows))
        upd = jnp.concatenate(rows, axis=0)

        @pl.when(first)
        def _():
            dw_ref[...] = jnp.zeros_like(dw_ref)
        dw_ref[...] += upd

    return pl.pallas_call(
        body, name=f"dn_branch_bwd_{col0}", grid=(DN_HEADS, T // tr),
        in_specs=[pl.BlockSpec((tr, HEAD_DIM), lambda h, i: (i, cb0 + h)),
                  pl.BlockSpec((SUBLANES, HEAD_DIM), lambda h, i: (jnp.maximum(i * r8 - 1, 0), cb0 + h)),
                  pl.BlockSpec((SUBLANES, HEAD_DIM), lambda h, i: (jnp.minimum((i + 1) * r8, n8 - 1), cb0 + h)),
                  pl.BlockSpec((tr, HEAD_DIM), lambda h, i: (i, h)),
                  pl.BlockSpec((SUBLANES, HEAD_DIM), lambda h, i: (jnp.minimum((i + 1) * r8, n8 - 1), h)),
                  pl.BlockSpec((K, HEAD_DIM), lambda h, i: (0, wb0 + h))],
        out_specs=[pl.BlockSpec((tr, HEAD_DIM), lambda h, i: (i, h)),
                   pl.BlockSpec((SUBLANES, HEAD_DIM), lambda h, i: (0, h))],
        out_shape=[_sds((T, W), BF16), _sds((SUBLANES, W), F32)],
        compiler_params=_params(("parallel", "arbitrary"), 32 * _nbytes((tr, HEAD_DIM), F32) + 8 * 2**20),
    )(proj, proj, proj, dy, dy, conv_w)


def _lane_masks(shape):
    lane = _iota2(shape, 1)
    return lane < DN_HEADS, (lane >= DN_HEADS) & (lane < 2 * DN_HEADS)


def _expand01(off):
    r = _iota2((LANES, DN_HEADS * HEAD_DIM), 0)
    c = _iota2((LANES, DN_HEADS * HEAD_DIM), 1)
    return (r == jnp.right_shift(c, int(math.log2(HEAD_DIM))) + off).astype(BF16)


def _select01(off):
    r = _iota2((DN_HEADS * HEAD_DIM, LANES), 0)
    c = _iota2((DN_HEADS * HEAD_DIM, LANES), 1)
    return (r == (c - off) * HEAD_DIM).astype(BF16)


def _dn_gates(proj, gate_block, a_log_l, dt_bias_l):
    T = proj.shape[0]
    C = DN_CHUNK
    W = DN_HEADS * HEAD_DIM

    def body(ba_ref, al_ref, dt_ref, gc_ref, beta_ref):
        ba = ba_ref[...]
        is_b, is_a = _lane_masks(ba.shape)
        g = jnp.where(is_a, -jnp.exp(al_ref[...]) * _softplus(ba + dt_ref[...]), 0.0)
        beta = jnp.where(is_b, _sigmoid(ba), 0.0)
        tri = (_iota2((C, C), 0) >= _iota2((C, C), 1)).astype(BF16)
        gc = _dot01_left(tri, g)
        gc_ref[...] = _dot01(gc, _expand01(DN_HEADS))
        beta_ref[...] = _dot01(beta, _expand01(0))

    vec = pl.BlockSpec((1, LANES), lambda n: (0, 0))
    return pl.pallas_call(
        body, name="dn_gates", grid=(T // C,),
        in_specs=[pl.BlockSpec((C, LANES), lambda n: (n, gate_block)), vec, vec],
        out_specs=[pl.BlockSpec((C, W), lambda n: (n, 0))] * 2,
        out_shape=[_sds((T, W), F32)] * 2,
        compiler_params=_params(("parallel",), 16 * 2**20),
    )(proj, a_log_l, dt_bias_l)


def _dn_gates_bwd(proj, gate_block, a_log_l, dt_bias_l, dgc_full, dbeta_full):
    T = proj.shape[0]
    C = DN_CHUNK
    W = DN_HEADS * HEAD_DIM

    def body(ba_ref, al_ref, dt_ref, dgc_ref, dbeta_ref, dba_ref, dal_ref, ddt_ref):
        ba = ba_ref[...]
        is_b, is_a = _lane_masks(ba.shape)
        ea = jnp.exp(al_ref[...])
        pre = ba + dt_ref[...]
        g = jnp.where(is_a, -ea * _softplus(pre), 0.0)
        beta = _sigmoid(ba)
        dgc = _dot01(dgc_ref[...], _select01(DN_HEADS))
        dbeta = _dot01(dbeta_ref[...], _select01(0))
        triu = (_iota2((C, C), 0) <= _iota2((C, C), 1)).astype(BF16)
        dg = _dot01_left(triu, dgc)
        da = jnp.where(is_a, dg * (-ea) * _sigmoid(pre), 0.0)
        dba_ref[...] = (da + jnp.where(is_b, dbeta * beta * (1.0 - beta), 0.0)).astype(BF16)
        _accumulate(dal_ref, jnp.sum(dg * g, axis=0, keepdims=True))
        _accumulate(ddt_ref, jnp.sum(da, axis=0, keepdims=True))

    vec = pl.BlockSpec((1, LANES), lambda n: (0, 0))
    full = pl.BlockSpec((C, W), lambda n: (n, 0))
    return pl.pallas_call(
        body, name="dn_gates_bwd", grid=(T // C,),
        in_specs=[pl.BlockSpec((C, LANES), lambda n: (n, gate_block)), vec, vec, full, full],
        out_specs=[pl.BlockSpec((C, LANES), lambda n: (n, 0)), vec, vec],
        out_shape=[_sds((T, LANES), BF16), _sds((1, LANES), F32), _sds((1, LANES), F32)],
        compiler_params=_params(("arbitrary",), 16 * 2**20),
    )(proj, a_log_l, dt_bias_l, dgc_full, dbeta_full)


def _unit_lower_inverse(L):
    C = L.shape[-1]
    row, col = _iota2((C, C), 0), _iota2((C, C), 1)
    eye = (row == col).astype(F32)
    sh = int(math.log2(INV_BLOCK))
    Ld = jnp.where(jnp.right_shift(row, sh) == jnp.right_shift(col, sh), L, 0.0)
    Lo = L - Ld
    X = eye - Ld
    P = Ld
    for _ in range(int(math.log2(INV_BLOCK)) - 1):
        P = _mm3(P, P)
        X = X + _mm3(X, P)
    N = _mm3(X, Lo)
    Y = eye - N
    P = N
    for _ in range(int(math.log2(C // INV_BLOCK)) - 1):
        P = _mm3(P, P)
        Y = Y + _mm3(Y, P)
    return _mm3(Y, X)


def _dn_chunk_common(q, k, v, gc, beta, gl):
    C = q.shape[-2]
    row, col = _iota2((C, C), 0), _iota2((C, C), 1)
    causal, strict = row >= col, row > col
    eg = jnp.exp(gc)
    decay = jnp.where(causal, jnp.exp(jnp.where(causal, gc - _t(gc), 0.0)), 0.0)
    kb, vb = k * beta, v * beta
    L = jnp.where(strict, _dot(kb, k, _NT) * decay, 0.0)
    Aqk = jnp.where(causal, _dot(q, k, _NT) * decay, 0.0)
    ektg = jnp.exp(gl - gc)
    return dict(causal=causal, strict=strict, eg=eg, decay=decay, kb=kb, vb=vb, L=L, Aqk=Aqk, ektg=ektg,
                kbg=kb * eg, kte=k * ektg, qd=q * eg, egl=jnp.exp(gl))


def _dn_scan(qn, kn, vn, gc_full, beta_full, proj, z_col0, gain):
    T, W = qn.shape
    C = DN_CHUNK
    N = T // C
    H = DN_HEADS
    G = DN_GROUP
    GW = G * HEAD_DIM
    zb0 = z_col0 // GW

    def body(q_ref, k_ref, v_ref, gc_ref, beta_ref, z_ref, gain_ref, o_ref, mix_ref, tm_ref, s_ref, S):
        @pl.when(pl.program_id(1) == 0)
        def _():
            S[...] = jnp.zeros_like(S)

        heads = lambda ref, rows=slice(None): jnp.stack([ref[rows, g * HEAD_DIM:(g + 1) * HEAD_DIM] for g in range(G)])
        q, k, v, gc, beta = heads(q_ref), heads(k_ref), heads(v_ref), heads(gc_ref), heads(beta_ref)
        gl = heads(gc_ref, slice(C - 1, C))
        c = _dn_chunk_common(q, k, v, gc, beta, gl)
        Tm = _unit_lower_inverse(c["L"])
        u = _dot(Tm, c["vb"])
        w = _dot(Tm, c["kbg"])
        S0 = S[...]
        vnew = u - _dot(w, S0)
        o = _dot(c["qd"], S0) + _dot(c["Aqk"], vnew)
        S[...] = S0 * c["egl"] + _dot(c["kte"], vnew, _TN)
        tm_ref[...] = Tm
        s_ref[...] = S0
        mix = (_rms(o, gain_ref[...]) * _silu(heads(z_ref))).astype(BF16)
        for g in range(G):
            sl = slice(g * HEAD_DIM, (g + 1) * HEAD_DIM)
            o_ref[:, sl] = o[g]
            mix_ref[:, sl] = mix[g]

    blk = pl.BlockSpec((C, GW), lambda h, n: (n, h))
    mat = pl.BlockSpec((G, None, C, C), lambda h, n: (h, n, 0, 0))
    return pl.pallas_call(
        body, name="dn_scan", grid=(H // G, N),
        in_specs=[blk, blk, blk, blk, blk, pl.BlockSpec((C, GW), lambda h, n: (n, zb0 + h)),
                  pl.BlockSpec((1, HEAD_DIM), lambda h, n: (0, 0))],
        out_specs=[blk, blk, mat, mat],
        out_shape=[_sds((T, W), F32), _sds((T, W), BF16), _sds((H, N, C, C), F32), _sds((H, N, C, C), F32)],
        scratch_shapes=[pltpu.VMEM((G, HEAD_DIM, HEAD_DIM), F32)],
        compiler_params=_params(("parallel", "arbitrary"), 32 * 2**20),
    )(qn, kn, vn, gc_full, beta_full, proj, gain)


def _dn_scan_bwd(qn, kn, vn, gc_full, beta_full, proj, z_col0, gain, o_raw, tm_all, s_all, dmix, dmix_col0):
    T, W = qn.shape
    C = DN_CHUNK
    N = T // C
    H = DN_HEADS
    G = DN_GROUP
    GW = G * HEAD_DIM
    zb0 = z_col0 // GW
    mb0 = dmix_col0 // GW

    def body(q_ref, k_ref, v_ref, gc_ref, beta_ref, z_ref, gain_ref, o_ref, tm_ref, s_ref, dmix_ref,
             dq_ref, dk_ref, dv_ref, dgc_ref, dbeta_ref, dz_ref, dgain_ref, dS):
        @pl.when(pl.program_id(1) == 0)
        def _():
            dS[...] = jnp.zeros_like(dS)

        @pl.when((pl.program_id(0) == 0) & (pl.program_id(1) == 0))
        def _():
            dgain_ref[...] = jnp.zeros_like(dgain_ref)

        heads = lambda ref, rows=slice(None): jnp.stack([ref[rows, g * HEAD_DIM:(g + 1) * HEAD_DIM] for g in range(G)])
        total = lambda x: jnp.sum(jnp.sum(x, axis=-1, keepdims=True), axis=-2, keepdims=True)
        gain = gain_ref[...]
        o, z, dmix = heads(o_ref), heads(z_ref), heads(dmix_ref)
        dz = (dmix * _rms(o, gain) * _silu_grad(z)).astype(BF16)
        do, dgain = _rms_bwd(dmix * _silu(z), o, gain)
        dgain_ref[...] += jnp.sum(dgain, axis=0)

        q, k, v, gc, beta = heads(q_ref), heads(k_ref), heads(v_ref), heads(gc_ref), heads(beta_ref)
        gl = heads(gc_ref, slice(C - 1, C))
        c = _dn_chunk_common(q, k, v, gc, beta, gl)
        Tm, S0, dS1 = tm_ref[...], s_ref[...], dS[...]
        w = _dot(Tm, c["kbg"])
        vnew = _dot(Tm, c["vb"]) - _dot(w, S0)

        dvnew = _dot(c["Aqk"], do, _TN) + _dot(c["kte"], dS1)
        dAqk = jnp.where(c["causal"], _dot(do, vnew, _NT), 0.0)
        dqd = _dot(do, S0, _NT)
        dkte = _dot(vnew, dS1, _NT)
        dgl = total(dS1 * S0) * c["egl"]
        dw = -_dot(dvnew, S0, _NT)
        dS[...] = dS1 * c["egl"] + _dot(c["qd"], do, _TN) - _dot(w, dvnew, _TN)

        dTm = _dot(dvnew, c["vb"], _NT) + _dot(dw, c["kbg"], _NT)
        dvb = _dot(Tm, dvnew, _TN)
        dkbg = _dot(Tm, dw, _TN)
        dL = jnp.where(c["strict"], -_mm3(_mm3(Tm, dTm, _TN), Tm, _NT), 0.0)
        dP = dL * c["decay"]
        dQ = dAqk * c["decay"]
        M = dL * c["L"] + dAqk * c["Aqk"]
        dkb = _dot(dP, k) + dkbg * c["eg"]
        dk = _dot(dP, c["kb"], _TN) + _dot(dQ, q, _TN) + dkte * c["ektg"] + dkb * beta
        dq = _dot(dQ, k) + dqd * c["eg"]
        tk = _rowsum(dkte * c["kte"])
        dgc = (_rowsum(M) - _rowsum(_t(M)) + _rowsum(dqd * c["qd"]) - tk + _rowsum(dkbg * c["kbg"]))
        dgl = dgl + total(tk)
        dgc = jnp.broadcast_to(dgc, q.shape) + jnp.where(_iota2((C, HEAD_DIM), 0) == C - 1, dgl, 0.0)
        dv = dvb * beta
        dbeta = jnp.broadcast_to(_rowsum(dkb * k) + _rowsum(dvb * v), q.shape)
        for g in range(G):
            sl = slice(g * HEAD_DIM, (g + 1) * HEAD_DIM)
            dz_ref[:, sl] = dz[g]
            dq_ref[:, sl] = dq[g]
            dk_ref[:, sl] = dk[g]
            dv_ref[:, sl] = dv[g]
            dgc_ref[:, sl] = dgc[g]
            dbeta_ref[:, sl] = dbeta[g]

    rev = lambda off: pl.BlockSpec((C, GW), lambda h, n: (N - 1 - n, off + h))
    mat = pl.BlockSpec((G, None, C, C), lambda h, n: (h, N - 1 - n, 0, 0))
    vec = pl.BlockSpec((1, HEAD_DIM), lambda h, n: (0, 0))
    return pl.pallas_call(
        body, name="dn_scan_bwd", grid=(H // G, N),
        in_specs=[rev(0), rev(0), rev(0), rev(0), rev(0), rev(zb0), vec, rev(0), mat, mat, rev(mb0)],
        out_specs=[rev(0)] * 6 + [vec],
        out_shape=[_sds((T, W), F32)] * 5 + [_sds((T, W), BF16), _sds((1, HEAD_DIM), F32)],
        scratch_shapes=[pltpu.VMEM((G, HEAD_DIM, HEAD_DIM), F32)],
        compiler_params=_params(("arbitrary", "arbitrary"), 40 * 2**20),
    )(qn, kn, vn, gc_full, beta_full, proj, gain, o_raw, tm_all, s_all, dmix)


def _sb_block(q, kj, ahead, first_key):
    z = _dot(q, kj, _NT) * (HEAD_DIM ** -0.5)
    lb = jnp.minimum(z, 0.0) - jnp.log(1.0 + jnp.exp(-jnp.abs(z)))
    if ahead is None:
        return None, lb, lb - z
    valid = ahead < -first_key
    return valid, lb, jnp.where(valid, lb - z, 0.0)


def _masked(valid, x):
    return x if valid is None else jnp.where(valid, x, 0.0)


def _sb_attention(qkv, gain, tq_cap=1024):
    T = qkv.shape[0]
    H = SB_HEADS
    B = min(SB_KEYS, T)
    TQ = _tile(T, tq_cap, B)
    per = TQ // B

    def body(q_ref, k_ref, v_ref, gain_ref, o_ref, mix_ref, ltot_ref):
        i = pl.program_id(1)
        q = q_ref[...]
        upper = (_iota2((B, B), 0) > _iota2((B, B), 1)).astype(BF16)
        ahead = _iota2((TQ, B), 1) - _iota2((TQ, B), 0)
        last = (i + 1) * per - 1

        def step(jj, carry, mask):
            acc, R = carry
            j = last - jj
            rows = pl.ds(pl.multiple_of(j * B, B), B)
            valid, lb, l1m = _sb_block(q, k_ref[rows, :], ahead if mask else None, j * B - i * TQ)
            att = _masked(valid, jnp.exp(lb + R + _dot01(l1m, upper, passes=2)))
            return acc + _dot(att, v_ref[rows, :]), R + _rowsum(l1m)

        carry = (jnp.zeros((TQ, HEAD_DIM), F32), jnp.zeros((TQ, 1), F32))
        carry = lax.fori_loop(0, per, functools.partial(step, mask=True), carry)
        acc, R = lax.fori_loop(per, last + 1, functools.partial(step, mask=False), carry)
        o_ref[...] = acc
        mix_ref[...] = _rms(acc, gain_ref[...]).astype(BF16)
        ltot_ref[...] = jnp.broadcast_to(R, (TQ, HEAD_DIM))

    head = lambda off: pl.BlockSpec((T, HEAD_DIM), lambda h, i: (0, off + h))
    blk = pl.BlockSpec((TQ, HEAD_DIM), lambda h, i: (i, h))
    return pl.pallas_call(
        body, name="sb_attention", grid=(H, T // TQ),
        in_specs=[blk, head(H), head(2 * H), pl.BlockSpec((1, HEAD_DIM), lambda h, i: (0, 0))],
        out_specs=[blk, blk, blk],
        out_shape=[_sds((T, H * HEAD_DIM), F32), _sds((T, H * HEAD_DIM), BF16), _sds((T, H * HEAD_DIM), F32)],
        compiler_params=_params(("parallel", "arbitrary"), 8 * _nbytes((T, HEAD_DIM), BF16) + 8 * 2**20),
    )(qkv, qkv, qkv, gain)


def _sb_attention_bwd(qkv, gain, o_raw, ltot, dmix, tq_cap=1024):
    T = qkv.shape[0]
    H = SB_HEADS
    B = min(SB_KEYS, T)
    TQ = _tile(T, tq_cap, B)
    per = TQ // B
    scale = HEAD_DIM ** -0.5

    def body(q_ref, k_ref, v_ref, gain_ref, o_ref, ltot_ref, dmix_ref, dq_ref, dk_ref, dv_ref, dgain_ref):
        i = pl.program_id(1)

        @pl.when(i == 0)
        def _():
            dk_ref[...] = jnp.zeros_like(dk_ref)
            dv_ref[...] = jnp.zeros_like(dv_ref)

        @pl.when((pl.program_id(0) == 0) & (i == 0))
        def _():
            dgain_ref[...] = jnp.zeros_like(dgain_ref)

        q = q_ref[...]
        o = o_ref[...]
        do, dgain = _rms_bwd(dmix_ref[...], o, gain_ref[...])
        dgain_ref[...] += dgain
        ltot = jnp.max(ltot_ref[...], axis=1, keepdims=True)
        do_b = do.astype(BF16)
        upto = (_iota2((B, B), 0) <= _iota2((B, B), 1)).astype(BF16)
        before = (_iota2((B, B), 0) < _iota2((B, B), 1)).astype(BF16)
        ahead = _iota2((TQ, B), 1) - _iota2((TQ, B), 0)

        def step(j, carry, mask):
            dq, PL, PG = carry
            rows = pl.ds(pl.multiple_of(j * B, B), B)
            kj = k_ref[rows, :]
            valid, lb, l1m = _sb_block(q, kj, ahead if mask else None, j * B - i * TQ)
            att = _masked(valid, jnp.exp(lb + (ltot - PL - _dot01(l1m, upto, passes=2))))
            sig = jnp.exp(lb)
            G = _dot(do_b, v_ref[rows, :], _NT) * att
            dv_ref[rows, :] += _dot(att, do_b, _TN)
            cum = PG + _dot01(G, before, passes=2)
            dz = _masked(valid, G * (1.0 - sig) - sig * cum) * scale
            dk_ref[rows, :] += _dot(dz, q, _TN)
            return dq + _dot(dz, kj), PL + _rowsum(l1m), PG + _rowsum(G)

        zero = jnp.zeros((TQ, 1), F32)
        carry = lax.fori_loop(0, i * per, functools.partial(step, mask=False), (jnp.zeros((TQ, HEAD_DIM), F32), zero, zero))
        dq, _, _ = lax.fori_loop(i * per, (i + 1) * per, functools.partial(step, mask=True), carry)
        dq_ref[...] = dq.astype(BF16)

    head = lambda off: pl.BlockSpec((T, HEAD_DIM), lambda h, i: (0, off + h))
    blk = pl.BlockSpec((TQ, HEAD_DIM), lambda h, i: (i, h))
    vec = pl.BlockSpec((1, HEAD_DIM), lambda h, i: (0, 0))
    return pl.pallas_call(
        body, name="sb_attention_bwd", grid=(H, T // TQ),
        in_specs=[blk, head(H), head(2 * H), vec, blk, blk, blk],
        out_specs=[blk, head(0), head(0), vec],
        out_shape=[_sds((T, H * HEAD_DIM), BF16), _sds((T, H * HEAD_DIM), F32), _sds((T, H * HEAD_DIM), F32),
                   _sds((1, HEAD_DIM), F32)],
        compiler_params=_params(("arbitrary", "arbitrary"), 8 * _nbytes((T, HEAD_DIM), F32) + 32 * _nbytes((TQ, B), F32)),
    )(qkv, qkv, qkv, gain, o_raw, ltot, dmix)


def _adamw_math(w, g, m, v):
    m = ADAM_B1 * m + (1.0 - ADAM_B1) * g
    v = ADAM_B2 * v + (1.0 - ADAM_B2) * (g * g)
    m_hat = m / (1.0 - ADAM_B1 ** ADAM_STEP)
    v_hat = v / (1.0 - ADAM_B2 ** ADAM_STEP)
    delta = -ADAM_LR * (m_hat / (jnp.sqrt(v_hat) + ADAM_EPS) + ADAM_WD * w)
    return delta, m, v


def _adamw_sharded(parts, w, m, v, name):
    _, R, C = w.shape
    tr = _tile(R, max(SUBLANES, (2**20 // (4 * C)) // SUBLANES * SUBLANES), SUBLANES)

    def body(p_ref, w_ref, m_ref, v_ref, g_ref, d_ref, nm_ref, nv_ref):
        g = p_ref[0].astype(F32)
        for d in range(1, N_DEV):
            g = g + p_ref[d].astype(F32)
        g_ref[...] = g
        d_ref[...], nm_ref[...], nv_ref[...] = _adamw_math(w_ref[...], g, m_ref[...], v_ref[...])

    blk = pl.BlockSpec((None, tr, C), lambda i: (0, i, 0))
    return pl.pallas_call(
        body, name=name, grid=(R // tr,),
        in_specs=[pl.BlockSpec((N_DEV, tr, C), lambda i: (0, i, 0)), blk, blk, blk],
        out_specs=[blk] * 4, out_shape=[_sds((1, R, C), F32)] * 4,
        compiler_params=_params(("parallel",), 40 * 2**20),
    )(parts, w, m, v)


def _adamw_packed(g, w, m, v):
    def body(g_ref, w_ref, m_ref, v_ref, d_ref, nm_ref, nv_ref):
        d_ref[...], nm_ref[...], nv_ref[...] = _adamw_math(w_ref[...], g_ref[...], m_ref[...], v_ref[...])

    return pl.pallas_call(body, name="adamw_packed", out_shape=[_sds(g.shape, F32)] * 3,
                          compiler_params=_params((), 16 * 2**20))(g, w, m, v)


def _my_place():
    x, y, c = lax.axis_index("x"), lax.axis_index("y"), lax.axis_index("c")
    return x, y, c


def _peer(place, k):
    x, y, c = place
    return (1 - x if k & 4 else x, 1 - y if k & 2 else y, 1 - c if k & 1 else c)


def _index(place):
    x, y, c = place
    return 4 * x + 2 * y + c


HBM_SPEC = pl.BlockSpec(memory_space=pltpu.HBM)


def _all_gather(block, name):
    R, C = block.shape

    def body(x_ref, out_ref, send_sems, recv_sems, local_sem):
        me = _my_place()
        sibling = _peer(me, 1)
        chips = [2, 4, 6]

        def copy(sem, origin, to, src=None):
            slot = out_ref.at[_index(origin)]
            return pltpu.make_async_remote_copy(
                src_ref=slot if src is None else src, dst_ref=slot, send_sem=send_sems.at[sem], recv_sem=recv_sems.at[sem],
                device_id=to, device_id_type=MESH)

        mine = pltpu.make_async_copy(x_ref, out_ref.at[_index(me)], local_sem)
        mine.start()
        first = [copy(0, me, sibling, src=x_ref)] + [copy(1 + n, me, _peer(me, k), src=x_ref) for n, k in enumerate(chips)]
        for cp in first:
            cp.start()
        passed = [copy(4 + n, _peer(me, k), sibling) for n, k in enumerate(chips)]
        for n, k in enumerate(chips):
            copy(1 + n, _peer(me, k), me).wait_recv()
            passed[n].start()
        copy(0, sibling, me).wait_recv()
        for n, k in enumerate(chips):
            copy(4 + n, _peer(sibling, k), me).wait_recv()
        for cp in first + passed:
            cp.wait_send()
        mine.wait()

    return pl.pallas_call(
        body, name=name, in_specs=[HBM_SPEC], out_specs=HBM_SPEC,
        out_shape=_sds((N_DEV, R, C), block.dtype),
        scratch_shapes=[pltpu.SemaphoreType.DMA((7,)), pltpu.SemaphoreType.DMA((7,)), pltpu.SemaphoreType.DMA],
    )(block)


SEM_SPEC = pl.BlockSpec(memory_space=pltpu.SEMAPHORE)
ANY_SPEC = pl.BlockSpec(memory_space=pl.ANY)
_EFFECT = pltpu.SideEffectType.DATAFLOW_SIDE_EFFECTING


def _spread_start(x, per_peer, name, after):
    R, C = x.shape[-2:]

    def body(x_ref, land_ref, after_ref, send_sems, recv_sems, x_thru, land_thru, token):
        me = _my_place()
        for k in range(1, N_DEV):
            to = _peer(me, k)
            pltpu.make_async_remote_copy(
                src_ref=x_ref.at[_index(to)] if per_peer else x_ref, dst_ref=land_ref.at[_index(me)],
                send_sem=send_sems.at[k - 1], recv_sem=recv_sems.at[k - 1], device_id=to, device_id_type=MESH).start()
        token[...] = jnp.zeros_like(token)

    land = lax.empty((N_DEV, R, C), x.dtype)
    send_sems, recv_sems, x_thru, land_thru, token = pl.pallas_call(
        body, name=name,
        out_shape=(pltpu.SemaphoreType.DMA((N_DEV - 1,)), pltpu.SemaphoreType.DMA((N_DEV - 1,)),
                   pltpu.HBM(x.shape, x.dtype), pltpu.HBM(land.shape, land.dtype), _sds((SUBLANES, LANES), F32)),
        in_specs=(HBM_SPEC, HBM_SPEC, ANY_SPEC),
        out_specs=(SEM_SPEC, SEM_SPEC, HBM_SPEC, HBM_SPEC, pl.BlockSpec(memory_space=pltpu.VMEM)),
        input_output_aliases={0: 2, 1: 3},
        compiler_params=pltpu.CompilerParams(has_side_effects=_EFFECT),
    )(pltpu.with_memory_space_constraint(x, pltpu.HBM), pltpu.with_memory_space_constraint(land, pltpu.HBM), after)
    return (send_sems, recv_sems, x_thru, land_thru), token


def _spread_wait(state, per_peer, name, after):
    send_sems, recv_sems, x_thru, land_thru = state

    def body(x_ref, land_ref, send_sems, recv_sems, after_ref, x_dead, got_ref):
        me = _my_place()
        for k in range(1, N_DEV):
            frm = _peer(me, k)
            copy = pltpu.make_async_remote_copy(
                src_ref=x_ref.at[_index(frm)] if per_peer else x_ref, dst_ref=land_ref.at[_index(frm)],
                send_sem=send_sems.at[k - 1], recv_sem=recv_sems.at[k - 1], device_id=frm, device_id_type=MESH)
            copy.wait_send()
            copy.wait_recv()

    x_back, got = pl.pallas_call(
        body, name=name,
        out_shape=(pltpu.HBM(x_thru.shape, x_thru.dtype), pltpu.HBM(land_thru.shape, land_thru.dtype)),
        in_specs=(HBM_SPEC, HBM_SPEC, SEM_SPEC, SEM_SPEC, ANY_SPEC), out_specs=(HBM_SPEC, HBM_SPEC),
        input_output_aliases={0: 0, 1: 1},
        compiler_params=pltpu.CompilerParams(has_side_effects=_EFFECT),
    )(x_thru, land_thru, send_sems, recv_sems, after)
    me = _index(_my_place())
    own = lax.dynamic_index_in_dim(x_back, me, axis=0, keepdims=True) if per_peer else x_back[None]
    return lax.dynamic_update_slice_in_dim(got, own, me, axis=0)


def _all_reduce_packed(vec, after):
    R, L = vec.shape

    def body(x_ref, after_ref, out_ref, buf, send_sems, recv_sems):
        me = _my_place()
        buf[_index(me)] = x_ref[...]
        copies = []
        for k in range(1, N_DEV):
            to = _peer(me, k)
            cp = pltpu.make_async_remote_copy(
                src_ref=x_ref, dst_ref=buf.at[_index(me)],
                send_sem=send_sems.at[k - 1], recv_sem=recv_sems.at[k - 1], device_id=to, device_id_type=MESH)
            cp.start()
            copies.append(cp)
        for k in range(1, N_DEV):
            frm = _peer(me, k)
            pltpu.make_async_remote_copy(
                src_ref=x_ref, dst_ref=buf.at[_index(frm)],
                send_sem=send_sems.at[k - 1], recv_sem=recv_sems.at[k - 1], device_id=frm, device_id_type=MESH).wait_recv()
        for cp in copies:
            cp.wait_send()
        acc = buf[0]
        for d in range(1, N_DEV):
            acc = acc + buf[d]
        out_ref[...] = acc

    vm = pl.BlockSpec(memory_space=pltpu.VMEM)
    return pl.pallas_call(
        body, name="all_reduce_packed", in_specs=[vm, ANY_SPEC], out_specs=vm, out_shape=_sds((R, L), F32),
        scratch_shapes=[pltpu.VMEM((N_DEV, R, L), F32), pltpu.SemaphoreType.DMA((7,)), pltpu.SemaphoreType.DMA((7,))],
        compiler_params=pltpu.CompilerParams(vmem_limit_bytes=32 * 2**20),
    )(vec, after)


def _pack(arrays):
    rows = []
    for a in arrays:
        f = a.reshape(-1).astype(F32)
        pad = (-f.shape[0]) % LANES
        rows.append(jnp.pad(f, (0, pad)).reshape(-1, LANES))
    out = jnp.concatenate(rows, axis=0)
    return jnp.pad(out, ((0, (-out.shape[0]) % SUBLANES), (0, 0)))


def _unpack(packed, shapes):
    out, r = [], 0
    for s in shapes:
        n = math.prod(s)
        nr = -(-n // LANES)
        out.append(packed[r:r + nr].reshape(-1)[:n].reshape(s))
        r += nr
    return out


def _col_blocks(g):
    R, C = g.shape
    return jnp.transpose(g.astype(BF16).reshape(R, N_DEV, C // N_DEV), (1, 0, 2))


def _row_blocks(g):
    R, C = g.shape
    return g.astype(BF16).reshape(N_DEV, R // N_DEV, C)


def kernel(x, w_in, sb_out_gain, dn_conv_w, dn_a_log, dn_dt_bias, dn_out_gain, w_out, ln_mix_pre, ln_mix_post, w_up, ffn_conv_w, ffn_conv_b, w_down, ln_ffn_pre, ln_ffn_post, loss_target, m_w_in, m_sb_out_gain, m_dn_conv_w, m_dn_a_log, m_dn_dt_bias, m_dn_out_gain, m_w_out, m_ln_mix_pre, m_ln_mix_post, m_w_up, m_ffn_conv_w, m_ffn_conv_b, m_w_down, m_ln_ffn_pre, m_ln_ffn_post, v_w_in, v_sb_out_gain, v_dn_conv_w, v_dn_a_log, v_dn_dt_bias, v_dn_out_gain, v_w_out, v_ln_mix_pre, v_ln_mix_post, v_w_up, v_ffn_conv_w, v_ffn_conv_b, v_w_down, v_ln_ffn_pre, v_ln_ffn_post):
    T, D = x.shape[1], x.shape[2]
    SBW = SB_HEADS * HEAD_DIM
    DNW = DN_HEADS * HEAD_DIM
    in_cols = 3 * SBW + 4 * DNW + 2 * DN_HEADS
    main_cols = 3 * SBW + 4 * DNW
    in_pad = main_cols + LANES
    qkv0, z0 = 3 * SBW, 3 * SBW + 3 * DNW
    gate_block = main_cols // LANES
    x2, tgt = x[0], loss_target[0]

    g_in = _all_gather(w_in[0].astype(BF16), "gather_w_in")
    small_w = _all_gather(_pack([dn_conv_w[0], ffn_conv_w[0]]), "gather_conv_w")
    st_out, tok = _spread_start(w_out[0].astype(BF16), False, "gather_w_out_start", g_in)
    st_up, tok = _spread_start(w_up[0].astype(BF16), False, "gather_w_up_start", tok)
    st_down, tok_gather = _spread_start(w_down[0].astype(BF16), False, "gather_w_down_start", tok)
    w_in_f = jnp.transpose(g_in, (1, 0, 2)).reshape(D, in_cols)
    w_in_f = jnp.pad(w_in_f, ((0, 0), (0, in_pad - in_cols)))
    parts = [_unpack(small_w[d], [dn_conv_w.shape[1:], ffn_conv_w.shape[1:]]) for d in range(N_DEV)]
    dn_cw = jnp.concatenate([p[0] for p in parts], axis=1)
    ffn_cw = jnp.concatenate([p[1] for p in parts], axis=1)
    lane_pad = lambda a, off: jnp.pad(a, ((0, 0), (off, LANES - off - a.shape[1])))
    a_log_l, dt_bias_l = lane_pad(dn_a_log, DN_HEADS), lane_pad(dn_dt_bias, DN_HEADS)

    xn = _norm_in(x2, ln_mix_pre)
    proj = _matmul(xn, w_in_f, "nn", F32, "proj_in", tm_cap=1024, tn_cap=2432, tk_cap=512, after=tok_gather)
    sb_qkv = proj[:, :3 * SBW].astype(BF16)
    o_sb, mix_sb, sb_ltot = _sb_attention(sb_qkv, sb_out_gain)
    qn = _dn_branch(proj, qkv0, dn_cw, 0, True, HEAD_DIM ** -0.5)
    kn = _dn_branch(proj, qkv0 + DNW, dn_cw, DNW, True, 1.0)
    vn = _dn_branch(proj, qkv0 + 2 * DNW, dn_cw, 2 * DNW, False, 1.0)
    gc_full, beta_full = _dn_gates(proj, gate_block, a_log_l, dt_bias_l)
    o_dn, mix_dn, tm_all, s_all = _dn_scan(qn, kn, vn, gc_full, beta_full, proj, z0, dn_out_gain)
    mix = jnp.concatenate([mix_sb, mix_dn], axis=1)
    w_out_f = _spread_wait(st_out, False, "gather_w_out_wait", mix).reshape(w_out.shape[1] * N_DEV, D)
    m = _matmul(mix, w_out_f, "nn", F32, "proj_out")
    h, hn = _mix_residual(x2, m, ln_mix_post, ln_ffn_pre)
    w_up_cut = _spread_wait(st_up, False, "gather_w_up_wait", hn)
    u = _matmul(hn, w_up_cut, "nn", F32, "ffn_up", tn_cap=w_up.shape[2], b_cut=True)
    act = _ffn_act(u, ffn_cw, ffn_conv_b)
    w_down_f = _spread_wait(st_down, False, "gather_w_down_wait", act).reshape(w_down.shape[1] * N_DEV, D)
    f = _matmul(act, w_down_f, "nn", F32, "ffn_down")
    dy, df, d_ln_ffn_post, loss_part = _loss_head(h, f, ln_ffn_post, tgt)

    d_w_down = _matmul(act, df, "tn", BF16, "grad_w_down")
    st_xd, tok = _spread_start(_row_blocks(d_w_down), True, "exchange_w_down_start", loss_part)
    da = _matmul(df, w_down_f, "nt", F32, "bwd_ffn_down", after=tok)
    du, d_ffn_cwb = _ffn_act_bwd(u, ffn_cw, ffn_conv_b, da)
    d_ffn_cwb = jnp.concatenate([d_ffn_cwb[0], d_ffn_cwb[1]], axis=1)
    d_w_up_cut = _matmul(hn, du, "tn", BF16, "grad_w_up", b_cut=True, out_cut=True)
    st_xu, tok = _spread_start(d_w_up_cut, True, "exchange_w_up_start", d_ffn_cwb)
    dhn = _matmul(du, w_up_cut, "nt", F32, "bwd_ffn_up", after=tok, a_cut=True, b_cut=True)
    dh, dm, d_ln_ffn_pre, d_ln_mix_post = _ffn_residual_bwd(dy, dhn, h, ln_ffn_pre, m, ln_mix_post)

    d_w_out = _matmul(mix, dm, "tn", BF16, "grad_w_out")
    st_xo, tok = _spread_start(_row_blocks(d_w_out), True, "exchange_w_out_start", d_ln_ffn_pre)
    dmix = _matmul(dm, w_out_f, "nt", F32, "bwd_proj_out", after=tok)
    dq_sb, dk_sb, dv_sb, d_sb_gain = _sb_attention_bwd(sb_qkv, sb_out_gain, o_sb, sb_ltot, dmix)
    dqn, dkn, dvn, dgc_full, dbeta_full, dz, d_dn_gain = _dn_scan_bwd(
        qn, kn, vn, gc_full, beta_full, proj, z0, dn_out_gain, o_dn, tm_all, s_all, dmix, SBW)
    du_q, dcw_q = _dn_branch_bwd(proj, qkv0, dn_cw, 0, True, HEAD_DIM ** -0.5, dqn)
    du_k, dcw_k = _dn_branch_bwd(proj, qkv0 + DNW, dn_cw, DNW, True, 1.0, dkn)
    du_v, dcw_v = _dn_branch_bwd(proj, qkv0 + 2 * DNW, dn_cw, 2 * DNW, False, 1.0, dvn)
    dba, d_a_log_l, d_dt_bias_l = _dn_gates_bwd(proj, gate_block, a_log_l, dt_bias_l, dgc_full, dbeta_full)
    dproj = jnp.concatenate([dq_sb, dk_sb.astype(BF16), dv_sb.astype(BF16), du_q, du_k, du_v, dz, dba], axis=1)
    d_w_in = _matmul(xn, dproj, "tn", BF16, "grad_w_in", tm_cap=512, tn_cap=2432, tk_cap=1024)[:, :in_cols]
    st_xi, tok = _spread_start(_col_blocks(d_w_in), True, "exchange_w_in_start", d_sb_gain)
    dxn = _matmul(dproj, w_in_f, "nt", F32, "bwd_proj_in", tk_cap=2432, after=tok)
    grad_x, d_ln_mix_pre = _input_bwd(dh, dxn, x2, ln_mix_pre)

    big = {}
    after = grad_x
    for n, st, w_, m_, v_ in [("w_down", st_xd, w_down, m_w_down, v_w_down), ("w_up", st_xu, w_up, m_w_up, v_w_up),
                              ("w_out", st_xo, w_out, m_w_out, v_w_out)]:
        got = _spread_wait(st, True, "exchange_" + n + "_wait", after)
        big[n] = _adamw_sharded(got, w_, m_, v_, "adamw_" + n)
        after = big[n][1]

    d_dn_cw = jnp.concatenate([dcw_q[:SHORT_CONV], dcw_k[:SHORT_CONV], dcw_v[:SHORT_CONV]], axis=1)
    small = [loss_part[:, :1], d_sb_gain, d_a_log_l[:, DN_HEADS:2 * DN_HEADS], d_dt_bias_l[:, DN_HEADS:2 * DN_HEADS], d_dn_gain,
             d_ln_mix_pre, d_ln_mix_post, d_ffn_cwb[FFN_CONV:FFN_CONV + 1], d_ln_ffn_pre, d_ln_ffn_post,
             d_dn_cw, d_ffn_cwb[:FFN_CONV]]
    shapes = [a.shape for a in small]
    red = _unpack(_all_reduce_packed(_pack(small), after), shapes)
    loss = red[0].reshape(())
    me = _index(_my_place())
    g_dn_cw = lax.dynamic_slice_in_dim(red[10], me * dn_conv_w.shape[2], dn_conv_w.shape[2], axis=1)
    g_ffn_cw = lax.dynamic_slice_in_dim(red[11], me * ffn_conv_w.shape[2], ffn_conv_w.shape[2], axis=1)
    names = ["sb_out_gain", "dn_conv_w", "dn_a_log", "dn_dt_bias", "dn_out_gain", "ln_mix_pre", "ln_mix_post",
             "ffn_conv_w", "ffn_conv_b", "ln_ffn_pre", "ln_ffn_post"]
    g_small = dict(sb_out_gain=red[1], dn_conv_w=g_dn_cw[None], dn_a_log=red[2], dn_dt_bias=red[3], dn_out_gain=red[4],
                   ln_mix_pre=red[5], ln_mix_post=red[6], ffn_conv_w=g_ffn_cw[None], ffn_conv_b=red[7],
                   ln_ffn_pre=red[8], ln_ffn_post=red[9])
    w_small = dict(sb_out_gain=sb_out_gain, dn_conv_w=dn_conv_w, dn_a_log=dn_a_log, dn_dt_bias=dn_dt_bias,
                   dn_out_gain=dn_out_gain, ln_mix_pre=ln_mix_pre, ln_mix_post=ln_mix_post, ffn_conv_w=ffn_conv_w,
                   ffn_conv_b=ffn_conv_b, ln_ffn_pre=ln_ffn_pre, ln_ffn_post=ln_ffn_post)
    m_small = dict(sb_out_gain=m_sb_out_gain, dn_conv_w=m_dn_conv_w, dn_a_log=m_dn_a_log, dn_dt_bias=m_dn_dt_bias,
                   dn_out_gain=m_dn_out_gain, ln_mix_pre=m_ln_mix_pre, ln_mix_post=m_ln_mix_post, ffn_conv_w=m_ffn_conv_w,
                   ffn_conv_b=m_ffn_conv_b, ln_ffn_pre=m_ln_ffn_pre, ln_ffn_post=m_ln_ffn_post)
    v_small = dict(sb_out_gain=v_sb_out_gain, dn_conv_w=v_dn_conv_w, dn_a_log=v_dn_a_log, dn_dt_bias=v_dn_dt_bias,
                   dn_out_gain=v_dn_out_gain, ln_mix_pre=v_ln_mix_pre, ln_mix_post=v_ln_mix_post, ffn_conv_w=v_ffn_conv_w,
                   ffn_conv_b=v_ffn_conv_b, ln_ffn_pre=v_ln_ffn_pre, ln_ffn_post=v_ln_ffn_post)
    sshapes = [w_small[n].shape for n in names]
    upd = _adamw_packed(_pack([g_small[n] for n in names]), _pack([w_small[n] for n in names]),
                        _pack([m_small[n] for n in names]), _pack([v_small[n] for n in names]))
    d_small, nm_small, nv_small = [dict(zip(names, _unpack(p, sshapes))) for p in upd]

    got = _spread_wait(st_xi, True, "exchange_w_in_wait", d_small["ln_ffn_post"])
    big["w_in"] = _adamw_sharded(got, w_in, m_w_in, v_w_in, "adamw_w_in")

    order = ["w_in", "sb_out_gain", "dn_conv_w", "dn_a_log", "dn_dt_bias", "dn_out_gain", "w_out", "ln_mix_pre",
             "ln_mix_post", "w_up", "ffn_conv_w", "ffn_conv_b", "w_down", "ln_ffn_pre", "ln_ffn_post"]
    pick = lambda n, i: big[n][i] if n in big else [g_small, d_small, nm_small, nv_small][i][n].reshape(w_small[n].shape)
    return (loss, grad_x[None], *[pick(n, 0) for n in order], *[pick(n, 1) for n in order],
            *[pick(n, 2) for n in order], *[pick(n, 3) for n in order])
```

```python
import functools
import math

import jax
import jax.numpy as jnp
from jax import lax
from jax.experimental import pallas as pl
from jax.experimental.pallas import tpu as pltpu

F32 = jnp.float32
BF16 = jnp.bfloat16

N_DEV = 8
HEAD_DIM = 128
SB_HEADS = 8
DN_HEADS = 8
DN_CHUNK = 128
DN_GROUP = 8
INV_BLOCK = 16
SB_KEYS = 256
SHORT_CONV = 4
FFN_CONV = 3
EPS = 1e-6
LANES = 128
SUBLANES = 8
VMEM_CAP = 56 * 2**20

ADAM_LR = 0.001
ADAM_B1 = 0.9
ADAM_B2 = 0.999
ADAM_EPS = 1e-08
ADAM_WD = 0.01
ADAM_STEP = 10

MESH = pl.DeviceIdType.MESH

assert HEAD_DIM == DN_CHUNK == LANES


def _tile(n, cap, mult):
    if n <= cap:
        return n
    t = (cap // mult) * mult
    while t >= mult:
        if n % t == 0:
            return t
        t -= mult
    raise ValueError(f"no tile for {n} under {cap} in multiples of {mult}")


def _params(sem, vmem_bytes):
    limit = int(min(VMEM_CAP, max(vmem_bytes, 16 * 2**20)))
    if not sem:
        return pltpu.CompilerParams(vmem_limit_bytes=limit)
    return pltpu.CompilerParams(dimension_semantics=sem, vmem_limit_bytes=limit)


def _nbytes(shape, dtype):
    return math.prod(shape) * jnp.dtype(dtype).itemsize


_NN = (((1,), (0,)), ((), ()))
_NT = (((1,), (1,)), ((), ()))
_TN = (((0,), (0,)), ((), ()))


def _batched(dims, ndim):
    if ndim == 2:
        return dims
    (ca,), (cb,) = dims[0]
    return (((ca + 1,), (cb + 1,)), ((0,), (0,)))


def _dot(a, b, dims=_NN):
    return lax.dot_general(a.astype(BF16), b.astype(BF16), _batched(dims, a.ndim), preferred_element_type=F32)


def _split2(x):
    hi = x.astype(BF16)
    lo = (x - hi.astype(F32)).astype(BF16)
    return hi, lo


def _split3(x):
    hi = x.astype(BF16)
    r = x - hi.astype(F32)
    mid = r.astype(BF16)
    lo = (r - mid.astype(F32)).astype(BF16)
    return hi, mid, lo


def _dot01(x, m01, passes=3):
    parts = _split3(x) if passes == 3 else _split2(x)
    out = None
    for p in parts:
        t = lax.dot_general(p, m01, _NN, preferred_element_type=F32)
        out = t if out is None else out + t
    return out


def _dot01_left(m01, x, passes=3):
    parts = _split3(x) if passes == 3 else _split2(x)
    out = None
    for p in parts:
        t = lax.dot_general(m01, p, _NN, preferred_element_type=F32)
        out = t if out is None else out + t
    return out


def _mm3(a, b, dims=_NN):
    ah, al = _split2(a)
    bh, bl = _split2(b)
    d = functools.partial(lax.dot_general, dimension_numbers=_batched(dims, a.ndim), preferred_element_type=F32)
    return d(ah, bh) + (d(ah, bl) + d(al, bh))


def _rowsum(x):
    return jnp.sum(x, axis=-1, keepdims=True)


def _t(x):
    return jnp.swapaxes(x, -1, -2)


def _sigmoid(x):
    return 1.0 / (1.0 + jnp.exp(-x))


def _softplus(x):
    return jnp.maximum(x, 0.0) + jnp.log(1.0 + jnp.exp(-jnp.abs(x)))


def _silu(x):
    return x * _sigmoid(x)


def _silu_grad(x):
    s = _sigmoid(x)
    return s * (1.0 + x * (1.0 - s))


_GELU_C = math.sqrt(2.0 / math.pi)


def _gelu(x):
    return 0.5 * x * (1.0 + jnp.tanh(_GELU_C * (x + 0.044715 * x * x * x)))


def _gelu_and_grad(x):
    x2 = x * x
    th = jnp.tanh(_GELU_C * (x + 0.044715 * x2 * x))
    half = 0.5 * (1.0 + th)
    return x * half, half + 0.5 * x * (1.0 - th * th) * (_GELU_C * (1.0 + 3.0 * 0.044715 * x2))


def _rms(x, g):
    r = lax.rsqrt(jnp.mean(x * x, axis=-1, keepdims=True) + EPS)
    return x * r * g


def _rms_bwd(dy, x, g):
    r = lax.rsqrt(jnp.mean(x * x, axis=-1, keepdims=True) + EPS)
    xh = x * r
    gdy = dy * g
    dx = r * (gdy - xh * jnp.mean(gdy * xh, axis=-1, keepdims=True))
    return dx, jnp.sum(dy * xh, axis=-2, keepdims=True)


def _iota2(shape, axis):
    return lax.broadcasted_iota(jnp.int32, shape, axis)


def _shift_down(cur, prev8, k):
    n = cur.shape[0]
    r = pltpu.roll(cur, k, 0)
    pr = pltpu.roll(prev8, k, 0)
    head = jnp.where(_iota2(pr.shape, 0) < k, pr, r[0:SUBLANES])
    if n == SUBLANES:
        return head
    return jnp.concatenate([head, r[SUBLANES:]], axis=0)


def _shift_up(cur, next8, k):
    n = cur.shape[0]
    r = pltpu.roll(cur, n - k, 0)
    nr = pltpu.roll(next8, SUBLANES - k, 0)
    tail = jnp.where(_iota2(nr.shape, 0) >= SUBLANES - k, nr, r[n - SUBLANES:])
    if n == SUBLANES:
        return tail
    return jnp.concatenate([r[:n - SUBLANES], tail], axis=0)


def _causal_conv(cur, prev8, w_ref, taps):
    out = cur * w_ref[taps - 1:taps, :]
    for j in range(taps - 1):
        out = out + _shift_down(cur, prev8, taps - 1 - j) * w_ref[j:j + 1, :]
    return out


def _anti_conv(cur, next8, w_ref, taps):
    out = cur * w_ref[taps - 1:taps, :]
    for j in range(taps - 1):
        out = out + _shift_up(cur, next8, taps - 1 - j) * w_ref[j:j + 1, :]
    return out


def _matmul(a, b, mode, out_dtype, name, tm_cap=1024, tn_cap=1024, tk_cap=2048, after=None,
            a_cut=False, b_cut=False, out_cut=False):
    a_shard = a.shape[2] if a_cut else None
    b_shard = b.shape[2] if b_cut else None
    a_full = (a.shape[1], a.shape[0] * a_shard) if a_cut else a.shape
    b_full = (b.shape[1], b.shape[0] * b_shard) if b_cut else b.shape
    assert not (a_cut and mode == "tn")
    if mode == "nn":
        (M, K), N = a_full, b_full[1]
    elif mode == "nt":
        (M, K), N = a_full, b_full[0]
    else:
        (K, M), N = a_full, b_full[1]
    n_unit = b_shard if (b_cut and mode != "nt") else N
    k_unit = math.gcd(a_shard or K, b_shard if (b_cut and mode == "nt") else K)
    tm = _tile(M, tm_cap, LANES)
    tn = N // N_DEV if out_cut else _tile(n_unit, tn_cap, LANES)
    tk = _tile(k_unit, tk_cap, LANES)
    assert n_unit % tn == 0 and k_unit % tk == 0
    nk = K // tk
    dims = {"nn": _NN, "nt": _NT, "tn": _TN}[mode]
    if a_cut:
        pa = a_shard // tk
        a_spec = pl.BlockSpec((None, tm, tk), lambda i, j, k: (k // pa, i, k % pa))
    elif mode == "tn":
        a_spec = pl.BlockSpec((tk, tm), lambda i, j, k: (k, i))
    else:
        a_spec = pl.BlockSpec((tm, tk), lambda i, j, k: (i, k))
    if b_cut and mode == "nt":
        pb = b_shard // tk
        b_spec = pl.BlockSpec((None, tn, tk), lambda i, j, k: (k // pb, j, k % pb))
    elif b_cut:
        pb = b_shard // tn
        b_spec = pl.BlockSpec((None, tk, tn), lambda i, j, k: (j // pb, k, j % pb))
    elif mode == "nt":
        b_spec = pl.BlockSpec((tn, tk), lambda i, j, k: (j, k))
    else:
        b_spec = pl.BlockSpec((tk, tn), lambda i, j, k: (k, j))
    if out_cut:
        out_spec, out_shape = pl.BlockSpec((None, tm, tn), lambda i, j, k: (j, i, 0)), (N_DEV, M, tn)
    else:
        out_spec, out_shape = pl.BlockSpec((tm, tn), lambda i, j, k: (i, j)), (M, N)

    def body(a_ref, b_ref, *rest):
        if nk == 1:
            rest[-1][...] = lax.dot_general(a_ref[...], b_ref[...], dims, preferred_element_type=F32).astype(rest[-1].dtype)
            return
        o_ref, acc_ref = rest[-2:]
        k = pl.program_id(2)

        @pl.when(k == 0)
        def _():
            acc_ref[...] = jnp.zeros_like(acc_ref)

        acc_ref[...] += lax.dot_general(a_ref[...], b_ref[...], dims, preferred_element_type=F32)

        @pl.when(k == nk - 1)
        def _():
            o_ref[...] = acc_ref[...].astype(o_ref.dtype)

    vmem = 2 * (_nbytes((tm, tk), a.dtype) + _nbytes((tk, tn), b.dtype) + _nbytes((tm, tn), out_dtype)) + _nbytes((tm, tn), F32)
    tokens = [] if after is None else [after]
    return pl.pallas_call(
        body, name=name, grid=(M // tm, N // tn, nk),
        in_specs=[a_spec, b_spec] + [pl.BlockSpec(t.shape, lambda i, j, k: (0, 0)) for t in tokens],
        out_specs=out_spec,
        out_shape=jax.ShapeDtypeStruct(out_shape, out_dtype),
        scratch_shapes=[] if nk == 1 else [pltpu.VMEM((tm, tn), F32)],
        compiler_params=_params(("parallel", "parallel", "arbitrary"), vmem + 4 * 2**20),
    )(a, b, *tokens)


def _row_call(body, name, T, D, ins, outs, tr, acc_outs=()):
    def spec(a, kind):
        if kind == "row":
            return pl.BlockSpec((tr, a.shape[1]), lambda i: (i, 0))
        return pl.BlockSpec(a.shape, lambda i: (0, 0))
    in_specs = [spec(a, k) for a, k in ins]
    out_specs = [spec(a, k) for a, k in outs] + [spec(a, "vec") for a in acc_outs]
    out_shape = [a for a, _ in outs] + list(acc_outs)
    vmem = 2 * sum(_nbytes((tr, a.shape[1]) if k == "row" else a.shape, a.dtype) for a, k in list(ins) + list(outs))
    return pl.pallas_call(
        body, name=name, grid=(T // tr,), in_specs=in_specs, out_specs=out_specs, out_shape=out_shape,
        compiler_params=_params(("arbitrary",), 3 * vmem + 8 * 2**20),
    )(*[a for a, _ in ins])


def _sds(shape, dtype):
    return jax.ShapeDtypeStruct(shape, dtype)


def _accumulate(ref, val):
    @pl.when(pl.program_id(0) == 0)
    def _():
        ref[...] = jnp.zeros_like(ref)
    ref[...] += val


def _norm_in(x, g):
    T, D = x.shape

    def body(x_ref, g_ref, o_ref):
        o_ref[...] = _rms(x_ref[...], g_ref[...]).astype(BF16)

    return _row_call(body, "norm_in", T, D, [(x, "row"), (g, "vec")], [(_sds((T, D), BF16), "row")], _tile(T, 256, 16))[0]


def _mix_residual(x, m, g_post, g_pre):
    T, D = x.shape

    def body(x_ref, m_ref, gp_ref, gn_ref, h_ref, hn_ref):
        h = x_ref[...] + _rms(m_ref[...], gp_ref[...])
        h_ref[...] = h
        hn_ref[...] = _rms(h, gn_ref[...]).astype(BF16)

    return _row_call(body, "mix_residual", T, D, [(x, "row"), (m, "row"), (g_post, "vec"), (g_pre, "vec")],
                     [(_sds((T, D), F32), "row"), (_sds((T, D), BF16), "row")], _tile(T, 256, 16))


def _loss_head(h, f, g_post, target):
    T, D = h.shape

    def body(h_ref, f_ref, g_ref, t_ref, dy_ref, df_ref, dg_ref, loss_ref):
        f = f_ref[...]
        g = g_ref[...]
        diff = h_ref[...] + _rms(f, g) - t_ref[...]
        dy = diff * (1.0 / D)
        dy_ref[...] = dy
        df, dg = _rms_bwd(dy, f, g)
        df_ref[...] = df.astype(BF16)
        _accumulate(dg_ref, dg)
        _accumulate(loss_ref, jnp.full((1, LANES), 0.5 / D, F32) * jnp.sum(diff * diff))

    return _row_call(body, "loss_head", T, D, [(h, "row"), (f, "row"), (g_post, "vec"), (target, "row")],
                     [(_sds((T, D), F32), "row"), (_sds((T, D), BF16), "row")], _tile(T, 256, 16),
                     acc_outs=[_sds((1, D), F32), _sds((1, LANES), F32)])


def _ffn_residual_bwd(dy, dhn, h, g_pre, m, g_post):
    T, D = h.shape

    def body(dy_ref, dhn_ref, h_ref, gn_ref, m_ref, gp_ref, dh_ref, dm_ref, dgn_ref, dgp_ref):
        dhh, dgn = _rms_bwd(dhn_ref[...], h_ref[...], gn_ref[...])
        dh = dy_ref[...] + dhh
        dh_ref[...] = dh
        dm, dgp = _rms_bwd(dh, m_ref[...], gp_ref[...])
        dm_ref[...] = dm.astype(BF16)
        _accumulate(dgn_ref, dgn)
        _accumulate(dgp_ref, dgp)

    return _row_call(body, "ffn_residual_bwd", T, D,
                     [(dy, "row"), (dhn, "row"), (h, "row"), (g_pre, "vec"), (m, "row"), (g_post, "vec")],
                     [(_sds((T, D), F32), "row"), (_sds((T, D), BF16), "row")], _tile(T, 128, 16),
                     acc_outs=[_sds((1, D), F32), _sds((1, D), F32)])


def _input_bwd(dh, dxn, x, g):
    T, D = x.shape

    def body(dh_ref, dxn_ref, x_ref, g_ref, dx_ref, dg_ref):
        dx, dg = _rms_bwd(dxn_ref[...], x_ref[...], g_ref[...])
        dx_ref[...] = dh_ref[...] + dx
        _accumulate(dg_ref, dg)

    return _row_call(body, "input_bwd", T, D, [(dh, "row"), (dxn, "row"), (x, "row"), (g, "vec")],
                     [(_sds((T, D), F32), "row")], _tile(T, 256, 16), acc_outs=[_sds((1, D), F32)])


def _ffn_act(u, conv_w, conv_b):
    T, F2 = u.shape
    F = F2 // 2
    tc = _tile(F, 512, LANES)
    tr = _tile(T, 512, SUBLANES)
    nc = F // tc
    r8 = tr // SUBLANES

    def body(ug_ref, ugp_ref, uv_ref, uvp_ref, wg_ref, wv_ref, bg_ref, bv_ref, a_ref):
        first = pl.program_id(1) == 0
        cg = _causal_conv(ug_ref[...], jnp.where(first, 0.0, ugp_ref[...]), wg_ref, FFN_CONV) + bg_ref[...]
        cv = _causal_conv(uv_ref[...], jnp.where(first, 0.0, uvp_ref[...]), wv_ref, FFN_CONV) + bv_ref[...]
        a_ref[...] = (_gelu(cg) * cv).astype(BF16)

    cur = lambda off: pl.BlockSpec((tr, tc), lambda j, i: (i, j + off))
    prev = lambda off: pl.BlockSpec((SUBLANES, tc), lambda j, i: (jnp.maximum(i * r8 - 1, 0), j + off))
    wsp = lambda off: pl.BlockSpec((FFN_CONV, tc), lambda j, i: (0, j + off))
    bsp = lambda off: pl.BlockSpec((1, tc), lambda j, i: (0, j + off))
    return pl.pallas_call(
        body, name="ffn_act", grid=(nc, T // tr),
        in_specs=[cur(0), prev(0), cur(nc), prev(nc), wsp(0), wsp(nc), bsp(0), bsp(nc)],
        out_specs=pl.BlockSpec((tr, tc), lambda j, i: (i, j)),
        out_shape=_sds((T, F), BF16),
        compiler_params=_params(("parallel", "arbitrary"), 12 * _nbytes((tr, tc), F32) + 8 * 2**20),
    )(u, u, u, u, conv_w, conv_w, conv_b, conv_b)


def _ffn_act_bwd(u, conv_w, conv_b, da):
    T, F2 = u.shape
    F = F2 // 2
    tc = _tile(F, 512, LANES)
    tr = _tile(T, 512, SUBLANES)
    nc = F // tc
    r8 = tr // SUBLANES
    n8 = T // SUBLANES
    K = FFN_CONV

    def body(ug_ref, ugp_ref, ugn_ref, uv_ref, uvp_ref, uvn_ref, da_ref, dan_ref,
             wg_ref, wv_ref, bg_ref, bv_ref, du_ref, dwb_ref):
        i = pl.program_id(1)
        first = i == 0
        last = i == pl.num_programs(1) - 1

        def dconv(ug, ug_prev, uv, uv_prev, da_):
            cg = _causal_conv(ug, ug_prev, wg_ref, K) + bg_ref[...]
            cv = _causal_conv(uv, uv_prev, wv_ref, K) + bv_ref[...]
            act, act_grad = _gelu_and_grad(cg)
            return da_ * cv * act_grad, da_ * act

        ug, uv = ug_ref[...], uv_ref[...]
        ug_prev, uv_prev = jnp.where(first, 0.0, ugp_ref[...]), jnp.where(first, 0.0, uvp_ref[...])
        dcg, dcv = dconv(ug, ug_prev, uv, uv_prev, da_ref[...])
        dcgn, dcvn = dconv(ugn_ref[...], ug[tr - SUBLANES:], uvn_ref[...], uv[tr - SUBLANES:], dan_ref[...])
        du_ref[0] = _anti_conv(dcg, jnp.where(last, 0.0, dcgn), wg_ref, K).astype(BF16)
        du_ref[1] = _anti_conv(dcv, jnp.where(last, 0.0, dcvn), wv_ref, K).astype(BF16)

        @pl.when(first)
        def _():
            dwb_ref[...] = jnp.zeros_like(dwb_ref)

        for half, (dc, uo, uo_prev) in enumerate([(dcg, ug, ug_prev), (dcv, uv, uv_prev)]):
            rows = [jnp.sum(dc * _shift_down(uo, uo_prev, K - 1 - t), axis=0, keepdims=True) for t in range(K - 1)]
            rows += [jnp.sum(dc * uo, axis=0, keepdims=True), jnp.sum(dc, axis=0, keepdims=True)]
            rows += [jnp.zeros_like(rows[0])] * (SUBLANES - len(rows))
            dwb_ref[half] += jnp.concatenate(rows, axis=0)

    cur = lambda off: pl.BlockSpec((tr, tc), lambda j, i: (i, j + off))
    prev = lambda off: pl.BlockSpec((SUBLANES, tc), lambda j, i: (jnp.maximum(i * r8 - 1, 0), j + off))
    nxt = lambda off: pl.BlockSpec((SUBLANES, tc), lambda j, i: (jnp.minimum((i + 1) * r8, n8 - 1), j + off))
    wsp = lambda off: pl.BlockSpec((K, tc), lambda j, i: (0, j + off))
    bsp = lambda off: pl.BlockSpec((1, tc), lambda j, i: (0, j + off))
    return pl.pallas_call(
        body, name="ffn_act_bwd", grid=(nc, T // tr),
        in_specs=[cur(0), prev(0), nxt(0), cur(nc), prev(nc), nxt(nc), cur(0), nxt(0), wsp(0), wsp(nc), bsp(0), bsp(nc)],
        out_specs=[pl.BlockSpec((2, tr, tc), lambda j, i: (0, i, j)), pl.BlockSpec((2, SUBLANES, tc), lambda j, i: (0, 0, j))],
        out_shape=[_sds((2, T, F), BF16), _sds((2, SUBLANES, F), F32)],
        compiler_params=_params(("parallel", "arbitrary"), 24 * _nbytes((tr, tc), F32) + 8 * 2**20),
    )(u, u, u, u, u, u, da, da, conv_w, conv_w, conv_b, conv_b)


def _l2norm(s, scale):
    return s * (lax.rsqrt(jnp.sum(s * s, axis=-1, keepdims=True) + EPS) * scale)


def _dn_branch(proj, col0, conv_w, wcol0, l2, scale):
    T = proj.shape[0]
    W = DN_HEADS * HEAD_DIM
    tr = _tile(T, 2048, SUBLANES)
    r8 = tr // SUBLANES
    cb0, wb0 = col0 // HEAD_DIM, wcol0 // HEAD_DIM

    def body(u_ref, up_ref, w_ref, o_ref):
        first = pl.program_id(1) == 0
        s = _silu(_causal_conv(u_ref[...], jnp.where(first, 0.0, up_ref[...]), w_ref, SHORT_CONV))
        o_ref[...] = _l2norm(s, scale) if l2 else s

    return pl.pallas_call(
        body, name=f"dn_branch_{col0}", grid=(DN_HEADS, T // tr),
        in_specs=[pl.BlockSpec((tr, HEAD_DIM), lambda h, i: (i, cb0 + h)),
                  pl.BlockSpec((SUBLANES, HEAD_DIM), lambda h, i: (jnp.maximum(i * r8 - 1, 0), cb0 + h)),
                  pl.BlockSpec((SHORT_CONV, HEAD_DIM), lambda h, i: (0, wb0 + h))],
        out_specs=pl.BlockSpec((tr, HEAD_DIM), lambda h, i: (i, h)),
        out_shape=_sds((T, W), F32),
        compiler_params=_params(("parallel", "arbitrary"), 32 * _nbytes((tr, HEAD_DIM), F32) + 8 * 2**20),
    )(proj, proj, conv_w)


def _dn_branch_bwd(proj, col0, conv_w, wcol0, l2, scale, dy):
    T = proj.shape[0]
    W = DN_HEADS * HEAD_DIM
    tr = _tile(T, 2048, SUBLANES)
    r8 = tr // SUBLANES
    n8 = T // SUBLANES
    cb0, wb0 = col0 // HEAD_DIM, wcol0 // HEAD_DIM
    K = SHORT_CONV

    def body(u_ref, up_ref, un_ref, dy_ref, dyn_ref, w_ref, du_ref, dw_ref):
        i = pl.program_id(1)
        first = i == 0
        last = i == pl.num_programs(1) - 1

        def dconv(u, u_prev, dy_):
            c = _causal_conv(u, u_prev, w_ref, K)
            if l2:
                s = _silu(c)
                r = lax.rsqrt(jnp.sum(s * s, axis=-1, keepdims=True) + EPS)
                n = s * r
                ds = (scale * r) * (dy_ - n * jnp.sum(dy_ * n, axis=-1, keepdims=True))
            else:
                ds = dy_
            return ds * _silu_grad(c)

        u = u_ref[...]
        u_prev = jnp.where(first, 0.0, up_ref[...])
        dc = dconv(u, u_prev, dy_ref[...])
        dcn = jnp.where(last, 0.0, dconv(un_ref[...], u[tr - SUBLANES:], dyn_ref[...]))
        du_ref[...] = _anti_conv(dc, dcn, w_ref, K).astype(BF16)
        rows = [jnp.sum(dc * _shift_down(u, u_prev, K - 1 - t), axis=0, keepdims=True) for t in range(K - 1)]
        rows += [jnp.sum(dc * u, axis=0, keepdims=True)]
        rows += [jnp.zeros_like(rows[0])] * (SUBLANES - len(rows))
        upd = jnp.concatenate(rows, axis=0)

        @pl.when(first)
        def _():
            dw_ref[...] = jnp.zeros_like(dw_ref)
        dw_ref[...] += upd

    return pl.pallas_call(
        body, name=f"dn_branch_bwd_{col0}", grid=(DN_HEADS, T // tr),
        in_specs=[pl.BlockSpec((tr, HEAD_DIM), lambda h, i: (i, cb0 + h)),
                  pl.BlockSpec((SUBLANES, HEAD_DIM), lambda h, i: (jnp.maximum(i * r8 - 1, 0), cb0 + h)),
                  pl.BlockSpec((SUBLANES, HEAD_DIM), lambda h, i: (jnp.minimum((i + 1) * r8, n8 - 1), cb0 + h)),
                  pl.BlockSpec((tr, HEAD_DIM), lambda h, i: (i, h)),
                  pl.BlockSpec((SUBLANES, HEAD_DIM), lambda h, i: (jnp.minimum((i + 1) * r8, n8 - 1), h)),
                  pl.BlockSpec((K, HEAD_DIM), lambda h, i: (0, wb0 + h))],
        out_specs=[pl.BlockSpec((tr, HEAD_DIM), lambda h, i: (i, h)),
                   pl.BlockSpec((SUBLANES, HEAD_DIM), lambda h, i: (0, h))],
        out_shape=[_sds((T, W), BF16), _sds((SUBLANES, W), F32)],
        compiler_params=_params(("parallel", "arbitrary"), 32 * _nbytes((tr, HEAD_DIM), F32) + 8 * 2**20),
    )(proj, proj, proj, dy, dy, conv_w)


def _lane_masks(shape):
    lane = _iota2(shape, 1)
    return lane < DN_HEADS, (lane >= DN_HEADS) & (lane < 2 * DN_HEADS)


def _expand01(off):
    r = _iota2((LANES, DN_HEADS * HEAD_DIM), 0)
    c = _iota2((LANES, DN_HEADS * HEAD_DIM), 1)
    return (r == jnp.right_shift(c, int(math.log2(HEAD_DIM))) + off).astype(BF16)


def _select01(off):
    r = _iota2((DN_HEADS * HEAD_DIM, LANES), 0)
    c = _iota2((DN_HEADS * HEAD_DIM, LANES), 1)
    return (r == (c - off) * HEAD_DIM).astype(BF16)


def _dn_gates(proj, gate_block, a_log_l, dt_bias_l):
    T = proj.shape[0]
    C = DN_CHUNK
    W = DN_HEADS * HEAD_DIM

    def body(ba_ref, al_ref, dt_ref, gc_ref, beta_ref):
        ba = ba_ref[...]
        is_b, is_a = _lane_masks(ba.shape)
        g = jnp.where(is_a, -jnp.exp(al_ref[...]) * _softplus(ba + dt_ref[...]), 0.0)
        beta = jnp.where(is_b, _sigmoid(ba), 0.0)
        tri = (_iota2((C, C), 0) >= _iota2((C, C), 1)).astype(BF16)
        gc = _dot01_left(tri, g)
        gc_ref[...] = _dot01(gc, _expand01(DN_HEADS))
        beta_ref[...] = _dot01(beta, _expand01(0))

    vec = pl.BlockSpec((1, LANES), lambda n: (0, 0))
    return pl.pallas_call(
        body, name="dn_gates", grid=(T // C,),
        in_specs=[pl.BlockSpec((C, LANES), lambda n: (n, gate_block)), vec, vec],
        out_specs=[pl.BlockSpec((C, W), lambda n: (n, 0))] * 2,
        out_shape=[_sds((T, W), F32)] * 2,
        compiler_params=_params(("parallel",), 16 * 2**20),
    )(proj, a_log_l, dt_bias_l)


def _dn_gates_bwd(proj, gate_block, a_log_l, dt_bias_l, dgc_full, dbeta_full):
    T = proj.shape[0]
    C = DN_CHUNK
    W = DN_HEADS * HEAD_DIM

    def body(ba_ref, al_ref, dt_ref, dgc_ref, dbeta_ref, dba_ref, dal_ref, ddt_ref):
        ba = ba_ref[...]
        is_b, is_a = _lane_masks(ba.shape)
        ea = jnp.exp(al_ref[...])
        pre = ba + dt_ref[...]
        g = jnp.where(is_a, -ea * _softplus(pre), 0.0)
        beta = _sigmoid(ba)
        dgc = _dot01(dgc_ref[...], _select01(DN_HEADS))
        dbeta = _dot01(dbeta_ref[...], _select01(0))
        triu = (_iota2((C, C), 0) <= _iota2((C, C), 1)).astype(BF16)
        dg = _dot01_left(triu, dgc)
        da = jnp.where(is_a, dg * (-ea) * _sigmoid(pre), 0.0)
        dba_ref[...] = (da + jnp.where(is_b, dbeta * beta * (1.0 - beta), 0.0)).astype(BF16)
        _accumulate(dal_ref, jnp.sum(dg * g, axis=0, keepdims=True))
        _accumulate(ddt_ref, jnp.sum(da, axis=0, keepdims=True))

    vec = pl.BlockSpec((1, LANES), lambda n: (0, 0))
    full = pl.BlockSpec((C, W), lambda n: (n, 0))
    return pl.pallas_call(
        body, name="dn_gates_bwd", grid=(T // C,),
        in_specs=[pl.BlockSpec((C, LANES), lambda n: (n, gate_block)), vec, vec, full, full],
        out_specs=[pl.BlockSpec((C, LANES), lambda n: (n, 0)), vec, vec],
        out_shape=[_sds((T, LANES), BF16), _sds((1, LANES), F32), _sds((1, LANES), F32)],
        compiler_params=_params(("arbitrary",), 16 * 2**20),
    )(proj, a_log_l, dt_bias_l, dgc_full, dbeta_full)


def _unit_lower_inverse(L):
    C = L.shape[-1]
    row, col = _iota2((C, C), 0), _iota2((C, C), 1)
    eye = (row == col).astype(F32)
    sh = int(math.log2(INV_BLOCK))
    Ld = jnp.where(jnp.right_shift(row, sh) == jnp.right_shift(col, sh), L, 0.0)
    Lo = L - Ld
    X = eye - Ld
    P = Ld
    for _ in range(int(math.log2(INV_BLOCK)) - 1):
        P = _mm3(P, P)
        X = X + _mm3(X, P)
    N = _mm3(X, Lo)
    Y = eye - N
    P = N
    for _ in range(int(math.log2(C // INV_BLOCK)) - 1):
        P = _mm3(P, P)
        Y = Y + _mm3(Y, P)
    return _mm3(Y, X)


def _dn_chunk_common(q, k, v, gc, beta, gl):
    C = q.shape[-2]
    row, col = _iota2((C, C), 0), _iota2((C, C), 1)
    causal, strict = row >= col, row > col
    eg = jnp.exp(gc)
    decay = jnp.where(causal, jnp.exp(jnp.where(causal, gc - _t(gc), 0.0)), 0.0)
    kb, vb = k * beta, v * beta
    L = jnp.where(strict, _dot(kb, k, _NT) * decay, 0.0)
    Aqk = jnp.where(causal, _dot(q, k, _NT) * decay, 0.0)
    ektg = jnp.exp(gl - gc)
    return dict(causal=causal, strict=strict, eg=eg, decay=decay, kb=kb, vb=vb, L=L, Aqk=Aqk, ektg=ektg,
                kbg=kb * eg, kte=k * ektg, qd=q * eg, egl=jnp.exp(gl))


def _dn_scan(qn, kn, vn, gc_full, beta_full, proj, z_col0, gain):
    T, W = qn.shape
    C = DN_CHUNK
    N = T // C
    H = DN_HEADS
    G = DN_GROUP
    GW = G * HEAD_DIM
    zb0 = z_col0 // GW

    def body(q_ref, k_ref, v_ref, gc_ref, beta_ref, z_ref, gain_ref, o_ref, mix_ref, tm_ref, s_ref, S):
        @pl.when(pl.program_id(1) == 0)
        def _():
            S[...] = jnp.zeros_like(S)

        heads = lambda ref, rows=slice(None): jnp.stack([ref[rows, g * HEAD_DIM:(g + 1) * HEAD_DIM] for g in range(G)])
        q, k, v, gc, beta = heads(q_ref), heads(k_ref), heads(v_ref), heads(gc_ref), heads(beta_ref)
        gl = heads(gc_ref, slice(C - 1, C))
        c = _dn_chunk_common(q, k, v, gc, beta, gl)
        Tm = _unit_lower_inverse(c["L"])
        u = _dot(Tm, c["vb"])
        w = _dot(Tm, c["kbg"])
        S0 = S[...]
        vnew = u - _dot(w, S0)
        o = _dot(c["qd"], S0) + _dot(c["Aqk"], vnew)
        S[...] = S0 * c["egl"] + _dot(c["kte"], vnew, _TN)
        tm_ref[...] = Tm
        s_ref[...] = S0
        mix = (_rms(o, gain_ref[...]) * _silu(heads(z_ref))).astype(BF16)
        for g in range(G):
            sl = slice(g * HEAD_DIM, (g + 1) * HEAD_DIM)
            o_ref[:, sl] = o[g]
            mix_ref[:, sl] = mix[g]

    blk = pl.BlockSpec((C, GW), lambda h, n: (n, h))
    mat = pl.BlockSpec((G, None, C, C), lambda h, n: (h, n, 0, 0))
    return pl.pallas_call(
        body, name="dn_scan", grid=(H // G, N),
        in_specs=[blk, blk, blk, blk, blk, pl.BlockSpec((C, GW), lambda h, n: (n, zb0 + h)),
                  pl.BlockSpec((1, HEAD_DIM), lambda h, n: (0, 0))],
        out_specs=[blk, blk, mat, mat],
        out_shape=[_sds((T, W), F32), _sds((T, W), BF16), _sds((H, N, C, C), F32), _sds((H, N, C, C), F32)],
        scratch_shapes=[pltpu.VMEM((G, HEAD_DIM, HEAD_DIM), F32)],
        compiler_params=_params(("parallel", "arbitrary"), 32 * 2**20),
    )(qn, kn, vn, gc_full, beta_full, proj, gain)


def _dn_scan_bwd(qn, kn, vn, gc_full, beta_full, proj, z_col0, gain, o_raw, tm_all, s_all, dmix, dmix_col0):
    T, W = qn.shape
    C = DN_CHUNK
    N = T // C
    H = DN_HEADS
    G = DN_GROUP
    GW = G * HEAD_DIM
    zb0 = z_col0 // GW
    mb0 = dmix_col0 // GW

    def body(q_ref, k_ref, v_ref, gc_ref, beta_ref, z_ref, gain_ref, o_ref, tm_ref, s_ref, dmix_ref,
             dq_ref, dk_ref, dv_ref, dgc_ref, dbeta_ref, dz_ref, dgain_ref, dS):
        @pl.when(pl.program_id(1) == 0)
        def _():
            dS[...] = jnp.zeros_like(dS)

        @pl.when((pl.program_id(0) == 0) & (pl.program_id(1) == 0))
        def _():
            dgain_ref[...] = jnp.zeros_like(dgain_ref)

        heads = lambda ref, rows=slice(None): jnp.stack([ref[rows, g * HEAD_DIM:(g + 1) * HEAD_DIM] for g in range(G)])
        total = lambda x: jnp.sum(jnp.sum(x, axis=-1, keepdims=True), axis=-2, keepdims=True)
        gain = gain_ref[...]
        o, z, dmix = heads(o_ref), heads(z_ref), heads(dmix_ref)
        dz = (dmix * _rms(o, gain) * _silu_grad(z)).astype(BF16)
        do, dgain = _rms_bwd(dmix * _silu(z), o, gain)
        dgain_ref[...] += jnp.sum(dgain, axis=0)

        q, k, v, gc, beta = heads(q_ref), heads(k_ref), heads(v_ref), heads(gc_ref), heads(beta_ref)
        gl = heads(gc_ref, slice(C - 1, C))
        c = _dn_chunk_common(q, k, v, gc, beta, gl)
        Tm, S0, dS1 = tm_ref[...], s_ref[...], dS[...]
        w = _dot(Tm, c["kbg"])
        vnew = _dot(Tm, c["vb"]) - _dot(w, S0)

        dvnew = _dot(c["Aqk"], do, _TN) + _dot(c["kte"], dS1)
        dAqk = jnp.where(c["causal"], _dot(do, vnew, _NT), 0.0)
        dqd = _dot(do, S0, _NT)
        dkte = _dot(vnew, dS1, _NT)
        dgl = total(dS1 * S0) * c["egl"]
        dw = -_dot(dvnew, S0, _NT)
        dS[...] = dS1 * c["egl"] + _dot(c["qd"], do, _TN) - _dot(w, dvnew, _TN)

        dTm = _dot(dvnew, c["vb"], _NT) + _dot(dw, c["kbg"], _NT)
        dvb = _dot(Tm, dvnew, _TN)
        dkbg = _dot(Tm, dw, _TN)
        dL = jnp.where(c["strict"], -_mm3(_mm3(Tm, dTm, _TN), Tm, _NT), 0.0)
        dP = dL * c["decay"]
        dQ = dAqk * c["decay"]
        M = dL * c["L"] + dAqk * c["Aqk"]
        dkb = _dot(dP, k) + dkbg * c["eg"]
        dk = _dot(dP, c["kb"], _TN) + _dot(dQ, q, _TN) + dkte * c["ektg"] + dkb * beta
        dq = _dot(dQ, k) + dqd * c["eg"]
        tk = _rowsum(dkte * c["kte"])
        dgc = (_rowsum(M) - _rowsum(_t(M)) + _rowsum(dqd * c["qd"]) - tk + _rowsum(dkbg * c["kbg"]))
        dgl = dgl + total(tk)
        dgc = jnp.broadcast_to(dgc, q.shape) + jnp.where(_iota2((C, HEAD_DIM), 0) == C - 1, dgl, 0.0)
        dv = dvb * beta
        dbeta = jnp.broadcast_to(_rowsum(dkb * k) + _rowsum(dvb * v), q.shape)
        for g in range(G):
            sl = slice(g * HEAD_DIM, (g + 1) * HEAD_DIM)
            dz_ref[:, sl] = dz[g]
            dq_ref[:, sl] = dq[g]
            dk_ref[:, sl] = dk[g]
            dv_ref[:, sl] = dv[g]
            dgc_ref[:, sl] = dgc[g]
            dbeta_ref[:, sl] = dbeta[g]

    rev = lambda off: pl.BlockSpec((C, GW), lambda h, n: (N - 1 - n, off + h))
    mat = pl.BlockSpec((G, None, C, C), lambda h, n: (h, N - 1 - n, 0, 0))
    vec = pl.BlockSpec((1, HEAD_DIM), lambda h, n: (0, 0))
    return pl.pallas_call(
        body, name="dn_scan_bwd", grid=(H // G, N),
        in_specs=[rev(0), rev(0), rev(0), rev(0), rev(0), rev(zb0), vec, rev(0), mat, mat, rev(mb0)],
        out_specs=[rev(0)] * 6 + [vec],
        out_shape=[_sds((T, W), F32)] * 5 + [_sds((T, W), BF16), _sds((1, HEAD_DIM), F32)],
        scratch_shapes=[pltpu.VMEM((G, HEAD_DIM, HEAD_DIM), F32)],
        compiler_params=_params(("arbitrary", "arbitrary"), 40 * 2**20),
    )(qn, kn, vn, gc_full, beta_full, proj, gain, o_raw, tm_all, s_all, dmix)


def _sb_terms(z, ahead, first_key):
    lb = jnp.minimum(z, 0.0) - jnp.log(1.0 + jnp.exp(-jnp.abs(z)))
    if ahead is None:
        return None, lb, lb - z
    valid = ahead < -first_key
    return valid, lb, jnp.where(valid, lb - z, 0.0)


def _masked(valid, x):
    return x if valid is None else jnp.where(valid, x, 0.0)


def _sb_attention(qkv, gain, tq_cap=1024):
    T = qkv.shape[0]
    H = SB_HEADS
    B = min(SB_KEYS, T)
    TQ = _tile(T, tq_cap, B)
    per = TQ // B

    def body(q_ref, k_ref, v_ref, gain_ref, o_ref, mix_ref, ltot_ref):
        i = pl.program_id(1)
        q = q_ref[...]
        upper = (_iota2((B, B), 0) > _iota2((B, B), 1)).astype(BF16)
        ahead = _iota2((TQ, B), 1) - _iota2((TQ, B), 0)
        last = (i + 1) * per - 1
        rows = lambda j: pl.ds(pl.multiple_of(j * B, B), B)
        scores = lambda j: _dot(q, k_ref[rows(j), :], _NT) * (HEAD_DIM ** -0.5)

        def step(jj, carry, mask):
            acc, R, z, att_prev = carry
            j = last - jj
            z_next = scores(jnp.maximum(j - 1, 0))
            acc = acc + _dot(att_prev, v_ref[rows(jnp.minimum(j + 1, last)), :])
            valid, lb, l1m = _sb_terms(z, ahead if mask else None, j * B - i * TQ)
            att = _masked(valid, jnp.exp(lb + R + _dot01(l1m, upper, passes=2)))
            return acc, R + _rowsum(l1m), z_next, att.astype(BF16)

        carry = (jnp.zeros((TQ, HEAD_DIM), F32), jnp.zeros((TQ, 1), F32), scores(last), jnp.zeros((TQ, B), BF16))
        carry = lax.fori_loop(0, per, functools.partial(step, mask=True), carry)
        acc, R, _, att_prev = lax.fori_loop(per, last + 1, functools.partial(step, mask=False), carry)
        acc = acc + _dot(att_prev, v_ref[rows(0), :])
        o_ref[...] = acc
        mix_ref[...] = _rms(acc, gain_ref[...]).astype(BF16)
        ltot_ref[...] = jnp.broadcast_to(R, (TQ, HEAD_DIM))

    head = lambda off: pl.BlockSpec((T, HEAD_DIM), lambda h, i: (0, off + h))
    blk = pl.BlockSpec((TQ, HEAD_DIM), lambda h, i: (i, h))
    return pl.pallas_call(
        body, name="sb_attention", grid=(H, T // TQ),
        in_specs=[blk, head(H), head(2 * H), pl.BlockSpec((1, HEAD_DIM), lambda h, i: (0, 0))],
        out_specs=[blk, blk, blk],
        out_shape=[_sds((T, H * HEAD_DIM), F32), _sds((T, H * HEAD_DIM), BF16), _sds((T, H * HEAD_DIM), F32)],
        compiler_params=_params(("parallel", "arbitrary"), 8 * _nbytes((T, HEAD_DIM), BF16) + 32 * _nbytes((TQ, B), F32)),
    )(qkv, qkv, qkv, gain)


def _sb_attention_bwd(qkv, gain, o_raw, ltot, dmix, tq_cap=1024):
    T = qkv.shape[0]
    H = SB_HEADS
    B = min(SB_KEYS, T)
    TQ = _tile(T, tq_cap, B)
    per = TQ // B
    scale = HEAD_DIM ** -0.5

    def body(q_ref, k_ref, v_ref, gain_ref, o_ref, ltot_ref, dmix_ref, dq_ref, dk_ref, dv_ref, dgain_ref):
        i = pl.program_id(1)

        @pl.when(i == 0)
        def _():
            dk_ref[...] = jnp.zeros_like(dk_ref)
            dv_ref[...] = jnp.zeros_like(dv_ref)

        @pl.when((pl.program_id(0) == 0) & (i == 0))
        def _():
            dgain_ref[...] = jnp.zeros_like(dgain_ref)

        q = q_ref[...]
        o = o_ref[...]
        do, dgain = _rms_bwd(dmix_ref[...], o, gain_ref[...])
        dgain_ref[...] += dgain
        ltot = jnp.max(ltot_ref[...], axis=1, keepdims=True)
        do_b = do.astype(BF16)
        upto = (_iota2((B, B), 0) <= _iota2((B, B), 1)).astype(BF16)
        before = (_iota2((B, B), 0) < _iota2((B, B), 1)).astype(BF16)
        ahead = _iota2((TQ, B), 1) - _iota2((TQ, B), 0)

        def step(j, carry, mask):
            dq, PL, PG = carry
            rows = pl.ds(pl.multiple_of(j * B, B), B)
            kj = k_ref[rows, :]
            z = _dot(q, kj, _NT) * scale
            valid, lb, l1m = _sb_terms(z, ahead if mask else None, j * B - i * TQ)
            att = _masked(valid, jnp.exp(lb + (ltot - PL - _dot01(l1m, upto, passes=2))))
            sig = jnp.exp(lb)
            G = _dot(do_b, v_ref[rows, :], _NT) * att
            dv_ref[rows, :] += _dot(att, do_b, _TN)
            cum = PG + _dot01(G, before, passes=2)
            dz = _masked(valid, G * (1.0 - sig) - sig * cum) * scale
            dk_ref[rows, :] += _dot(dz, q, _TN)
            return dq + _dot(dz, kj), PL + _rowsum(l1m), PG + _rowsum(G)

        zero = jnp.zeros((TQ, 1), F32)
        carry = lax.fori_loop(0, i * per, functools.partial(step, mask=False), (jnp.zeros((TQ, HEAD_DIM), F32), zero, zero))
        dq, _, _ = lax.fori_loop(i * per, (i + 1) * per, functools.partial(step, mask=True), carry)
        dq_ref[...] = dq.astype(BF16)

    head = lambda off: pl.BlockSpec((T, HEAD_DIM), lambda h, i: (0, off + h))
    blk = pl.BlockSpec((TQ, HEAD_DIM), lambda h, i: (i, h))
    vec = pl.BlockSpec((1, HEAD_DIM), lambda h, i: (0, 0))
    return pl.pallas_call(
        body, name="sb_attention_bwd", grid=(H, T // TQ),
        in_specs=[blk, head(H), head(2 * H), vec, blk, blk, blk],
        out_specs=[blk, head(0), head(0), vec],
        out_shape=[_sds((T, H * HEAD_DIM), BF16), _sds((T, H * HEAD_DIM), F32), _sds((T, H * HEAD_DIM), F32),
                   _sds((1, HEAD_DIM), F32)],
        compiler_params=_params(("arbitrary", "arbitrary"), 8 * _nbytes((T, HEAD_DIM), F32) + 32 * _nbytes((TQ, B), F32)),
    )(qkv, qkv, qkv, gain, o_raw, ltot, dmix)


def _adamw_math(w, g, m, v):
    m = ADAM_B1 * m + (1.0 - ADAM_B1) * g
    v = ADAM_B2 * v + (1.0 - ADAM_B2) * (g * g)
    m_hat = m / (1.0 - ADAM_B1 ** ADAM_STEP)
    v_hat = v / (1.0 - ADAM_B2 ** ADAM_STEP)
    delta = -ADAM_LR * (m_hat / (jnp.sqrt(v_hat) + ADAM_EPS) + ADAM_WD * w)
    return delta, m, v


def _adamw_sharded(parts, w, m, v, name):
    _, R, C = w.shape
    tr = _tile(R, max(SUBLANES, (2**20 // (4 * C)) // SUBLANES * SUBLANES), SUBLANES)

    def body(p_ref, w_ref, m_ref, v_ref, g_ref, d_ref, nm_ref, nv_ref):
        g = p_ref[0].astype(F32)
        for d in range(1, N_DEV):
            g = g + p_ref[d].astype(F32)
        g_ref[...] = g
        d_ref[...], nm_ref[...], nv_ref[...] = _adamw_math(w_ref[...], g, m_ref[...], v_ref[...])

    blk = pl.BlockSpec((None, tr, C), lambda i: (0, i, 0))
    return pl.pallas_call(
        body, name=name, grid=(R // tr,),
        in_specs=[pl.BlockSpec((N_DEV, tr, C), lambda i: (0, i, 0)), blk, blk, blk],
        out_specs=[blk] * 4, out_shape=[_sds((1, R, C), F32)] * 4,
        compiler_params=_params(("parallel",), 40 * 2**20),
    )(parts, w, m, v)


def _adamw_packed(g, w, m, v):
    def body(g_ref, w_ref, m_ref, v_ref, d_ref, nm_ref, nv_ref):
        d_ref[...], nm_ref[...], nv_ref[...] = _adamw_math(w_ref[...], g_ref[...], m_ref[...], v_ref[...])

    return pl.pallas_call(body, name="adamw_packed", out_shape=[_sds(g.shape, F32)] * 3,
                          compiler_params=_params((), 16 * 2**20))(g, w, m, v)


def _my_place():
    x, y, c = lax.axis_index("x"), lax.axis_index("y"), lax.axis_index("c")
    return x, y, c


def _peer(place, k):
    x, y, c = place
    return (1 - x if k & 4 else x, 1 - y if k & 2 else y, 1 - c if k & 1 else c)


def _index(place):
    x, y, c = place
    return 4 * x + 2 * y + c


HBM_SPEC = pl.BlockSpec(memory_space=pltpu.HBM)


def _all_gather(block, name):
    R, C = block.shape

    def body(x_ref, out_ref, send_sems, recv_sems, local_sem):
        me = _my_place()
        sibling = _peer(me, 1)
        chips = [2, 4, 6]

        def copy(sem, origin, to, src=None):
            slot = out_ref.at[_index(origin)]
            return pltpu.make_async_remote_copy(
                src_ref=slot if src is None else src, dst_ref=slot, send_sem=send_sems.at[sem], recv_sem=recv_sems.at[sem],
                device_id=to, device_id_type=MESH)

        mine = pltpu.make_async_copy(x_ref, out_ref.at[_index(me)], local_sem)
        mine.start()
        first = [copy(0, me, sibling, src=x_ref)] + [copy(1 + n, me, _peer(me, k), src=x_ref) for n, k in enumerate(chips)]
        for cp in first:
            cp.start()
        passed = [copy(4 + n, _peer(me, k), sibling) for n, k in enumerate(chips)]
        for n, k in enumerate(chips):
            copy(1 + n, _peer(me, k), me).wait_recv()
            passed[n].start()
        copy(0, sibling, me).wait_recv()
        for n, k in enumerate(chips):
            copy(4 + n, _peer(sibling, k), me).wait_recv()
        for cp in first + passed:
            cp.wait_send()
        mine.wait()

    return pl.pallas_call(
        body, name=name, in_specs=[HBM_SPEC], out_specs=HBM_SPEC,
        out_shape=_sds((N_DEV, R, C), block.dtype),
        scratch_shapes=[pltpu.SemaphoreType.DMA((7,)), pltpu.SemaphoreType.DMA((7,)), pltpu.SemaphoreType.DMA],
    )(block)


SEM_SPEC = pl.BlockSpec(memory_space=pltpu.SEMAPHORE)
ANY_SPEC = pl.BlockSpec(memory_space=pl.ANY)
_EFFECT = pltpu.SideEffectType.DATAFLOW_SIDE_EFFECTING


def _spread_start(x, per_peer, name, after):
    R, C = x.shape[-2:]

    def body(x_ref, land_ref, after_ref, send_sems, recv_sems, x_thru, land_thru, token):
        me = _my_place()
        for k in range(1, N_DEV):
            to = _peer(me, k)
            pltpu.make_async_remote_copy(
                src_ref=x_ref.at[_index(to)] if per_peer else x_ref, dst_ref=land_ref.at[_index(me)],
                send_sem=send_sems.at[k - 1], recv_sem=recv_sems.at[k - 1], device_id=to, device_id_type=MESH).start()
        token[...] = jnp.zeros_like(token)

    land = lax.empty((N_DEV, R, C), x.dtype)
    send_sems, recv_sems, x_thru, land_thru, token = pl.pallas_call(
        body, name=name,
        out_shape=(pltpu.SemaphoreType.DMA((N_DEV - 1,)), pltpu.SemaphoreType.DMA((N_DEV - 1,)),
                   pltpu.HBM(x.shape, x.dtype), pltpu.HBM(land.shape, land.dtype), _sds((SUBLANES, LANES), F32)),
        in_specs=(HBM_SPEC, HBM_SPEC, ANY_SPEC),
        out_specs=(SEM_SPEC, SEM_SPEC, HBM_SPEC, HBM_SPEC, pl.BlockSpec(memory_space=pltpu.VMEM)),
        input_output_aliases={0: 2, 1: 3},
        compiler_params=pltpu.CompilerParams(has_side_effects=_EFFECT),
    )(pltpu.with_memory_space_constraint(x, pltpu.HBM), pltpu.with_memory_space_constraint(land, pltpu.HBM), after)
    return (send_sems, recv_sems, x_thru, land_thru), token


def _spread_wait(state, per_peer, name, after):
    send_sems, recv_sems, x_thru, land_thru = state

    def body(x_ref, land_ref, send_sems, recv_sems, after_ref, x_dead, got_ref):
        me = _my_place()
        for k in range(1, N_DEV):
            frm = _peer(me, k)
            copy = pltpu.make_async_remote_copy(
                src_ref=x_ref.at[_index(frm)] if per_peer else x_ref, dst_ref=land_ref.at[_index(frm)],
                send_sem=send_sems.at[k - 1], recv_sem=recv_sems.at[k - 1], device_id=frm, device_id_type=MESH)
            copy.wait_send()
            copy.wait_recv()

    x_back, got = pl.pallas_call(
        body, name=name,
        out_shape=(pltpu.HBM(x_thru.shape, x_thru.dtype), pltpu.HBM(land_thru.shape, land_thru.dtype)),
        in_specs=(HBM_SPEC, HBM_SPEC, SEM_SPEC, SEM_SPEC, ANY_SPEC), out_specs=(HBM_SPEC, HBM_SPEC),
        input_output_aliases={0: 0, 1: 1},
        compiler_params=pltpu.CompilerParams(has_side_effects=_EFFECT),
    )(x_thru, land_thru, send_sems, recv_sems, after)
    me = _index(_my_place())
    own = lax.dynamic_index_in_dim(x_back, me, axis=0, keepdims=True) if per_peer else x_back[None]
    return lax.dynamic_update_slice_in_dim(got, own, me, axis=0)


def _all_reduce_packed(vec, after):
    R, L = vec.shape

    def body(x_ref, after_ref, out_ref, buf, send_sems, recv_sems):
        me = _my_place()
        buf[_index(me)] = x_ref[...]
        copies = []
        for k in range(1, N_DEV):
            to = _peer(me, k)
            cp = pltpu.make_async_remote_copy(
                src_ref=x_ref, dst_ref=buf.at[_index(me)],
                send_sem=send_sems.at[k - 1], recv_sem=recv_sems.at[k - 1], device_id=to, device_id_type=MESH)
            cp.start()
            copies.append(cp)
        for k in range(1, N_DEV):
            frm = _peer(me, k)
            pltpu.make_async_remote_copy(
                src_ref=x_ref, dst_ref=buf.at[_index(frm)],
                send_sem=send_sems.at[k - 1], recv_sem=recv_sems.at[k - 1], device_id=frm, device_id_type=MESH).wait_recv()
        for cp in copies:
            cp.wait_send()
        acc = buf[0]
        for d in range(1, N_DEV):
            acc = acc + buf[d]
        out_ref[...] = acc

    vm = pl.BlockSpec(memory_space=pltpu.VMEM)
    return pl.pallas_call(
        body, name="all_reduce_packed", in_specs=[vm, ANY_SPEC], out_specs=vm, out_shape=_sds((R, L), F32),
        scratch_shapes=[pltpu.VMEM((N_DEV, R, L), F32), pltpu.SemaphoreType.DMA((7,)), pltpu.SemaphoreType.DMA((7,))],
        compiler_params=pltpu.CompilerParams(vmem_limit_bytes=32 * 2**20),
    )(vec, after)


def _pack(arrays):
    rows = []
    for a in arrays:
        f = a.reshape(-1).astype(F32)
        pad = (-f.shape[0]) % LANES
        rows.append(jnp.pad(f, (0, pad)).reshape(-1, LANES))
    out = jnp.concatenate(rows, axis=0)
    return jnp.pad(out, ((0, (-out.shape[0]) % SUBLANES), (0, 0)))


def _unpack(packed, shapes):
    out, r = [], 0
    for s in shapes:
        n = math.prod(s)
        nr = -(-n // LANES)
        out.append(packed[r:r + nr].reshape(-1)[:n].reshape(s))
        r += nr
    return out


def _col_blocks(g):
    R, C = g.shape
    return jnp.transpose(g.astype(BF16).reshape(R, N_DEV, C // N_DEV), (1, 0, 2))


def _row_blocks(g):
    R, C = g.shape
    return g.astype(BF16).reshape(N_DEV, R // N_DEV, C)


def kernel(x, w_in, sb_out_gain, dn_conv_w, dn_a_log, dn_dt_bias, dn_out_gain, w_out, ln_mix_pre, ln_mix_post, w_up, ffn_conv_w, ffn_conv_b, w_down, ln_ffn_pre, ln_ffn_post, loss_target, m_w_in, m_sb_out_gain, m_dn_conv_w, m_dn_a_log, m_dn_dt_bias, m_dn_out_gain, m_w_out, m_ln_mix_pre, m_ln_mix_post, m_w_up, m_ffn_conv_w, m_ffn_conv_b, m_w_down, m_ln_ffn_pre, m_ln_ffn_post, v_w_in, v_sb_out_gain, v_dn_conv_w, v_dn_a_log, v_dn_dt_bias, v_dn_out_gain, v_w_out, v_ln_mix_pre, v_ln_mix_post, v_w_up, v_ffn_conv_w, v_ffn_conv_b, v_w_down, v_ln_ffn_pre, v_ln_ffn_post):
    T, D = x.shape[1], x.shape[2]
    SBW = SB_HEADS * HEAD_DIM
    DNW = DN_HEADS * HEAD_DIM
    in_cols = 3 * SBW + 4 * DNW + 2 * DN_HEADS
    main_cols = 3 * SBW + 4 * DNW
    in_pad = main_cols + LANES
    qkv0, z0 = 3 * SBW, 3 * SBW + 3 * DNW
    gate_block = main_cols // LANES
    x2, tgt = x[0], loss_target[0]

    g_in = _all_gather(w_in[0].astype(BF16), "gather_w_in")
    small_w = _all_gather(_pack([dn_conv_w[0], ffn_conv_w[0]]), "gather_conv_w")
    st_out, tok = _spread_start(w_out[0].astype(BF16), False, "gather_w_out_start", g_in)
    st_up, tok = _spread_start(w_up[0].astype(BF16), False, "gather_w_up_start", tok)
    st_down, tok_gather = _spread_start(w_down[0].astype(BF16), False, "gather_w_down_start", tok)
    w_in_f = jnp.transpose(g_in, (1, 0, 2)).reshape(D, in_cols)
    w_in_f = jnp.pad(w_in_f, ((0, 0), (0, in_pad - in_cols)))
    parts = [_unpack(small_w[d], [dn_conv_w.shape[1:], ffn_conv_w.shape[1:]]) for d in range(N_DEV)]
    dn_cw = jnp.concatenate([p[0] for p in parts], axis=1)
    ffn_cw = jnp.concatenate([p[1] for p in parts], axis=1)
    lane_pad = lambda a, off: jnp.pad(a, ((0, 0), (off, LANES - off - a.shape[1])))
    a_log_l, dt_bias_l = lane_pad(dn_a_log, DN_HEADS), lane_pad(dn_dt_bias, DN_HEADS)

    xn = _norm_in(x2, ln_mix_pre)
    proj = _matmul(xn, w_in_f, "nn", F32, "proj_in", tm_cap=512, tn_cap=2432, after=tok_gather)
    sb_qkv = proj[:, :3 * SBW].astype(BF16)
    o_sb, mix_sb, sb_ltot = _sb_attention(sb_qkv, sb_out_gain)
    qn = _dn_branch(proj, qkv0, dn_cw, 0, True, HEAD_DIM ** -0.5)
    kn = _dn_branch(proj, qkv0 + DNW, dn_cw, DNW, True, 1.0)
    vn = _dn_branch(proj, qkv0 + 2 * DNW, dn_cw, 2 * DNW, False, 1.0)
    gc_full, beta_full = _dn_gates(proj, gate_block, a_log_l, dt_bias_l)
    o_dn, mix_dn, tm_all, s_all = _dn_scan(qn, kn, vn, gc_full, beta_full, proj, z0, dn_out_gain)
    mix = jnp.concatenate([mix_sb, mix_dn], axis=1)
    w_out_f = _spread_wait(st_out, False, "gather_w_out_wait", mix).reshape(w_out.shape[1] * N_DEV, D)
    m = _matmul(mix, w_out_f, "nn", F32, "proj_out")
    h, hn = _mix_residual(x2, m, ln_mix_post, ln_ffn_pre)
    w_up_cut = _spread_wait(st_up, False, "gather_w_up_wait", hn)
    u = _matmul(hn, w_up_cut, "nn", F32, "ffn_up", tn_cap=w_up.shape[2], b_cut=True)
    act = _ffn_act(u, ffn_cw, ffn_conv_b)
    w_down_f = _spread_wait(st_down, False, "gather_w_down_wait", act).reshape(w_down.shape[1] * N_DEV, D)
    f = _matmul(act, w_down_f, "nn", F32, "ffn_down", tk_cap=2816)
    dy, df, d_ln_ffn_post, loss_part = _loss_head(h, f, ln_ffn_post, tgt)

    d_w_down = _matmul(act, df, "tn", BF16, "grad_w_down")
    st_xd, tok = _spread_start(_row_blocks(d_w_down), True, "exchange_w_down_start", loss_part)
    da = _matmul(df, w_down_f, "nt", F32, "bwd_ffn_down", after=tok)
    du, d_ffn_cwb = _ffn_act_bwd(u, ffn_cw, ffn_conv_b, da)
    d_ffn_cwb = jnp.concatenate([d_ffn_cwb[0], d_ffn_cwb[1]], axis=1)
    d_w_up_cut = _matmul(hn, du, "tn", BF16, "grad_w_up", b_cut=True, out_cut=True)
    st_xu, tok = _spread_start(d_w_up_cut, True, "exchange_w_up_start", d_ffn_cwb)
    dhn = _matmul(du, w_up_cut, "nt", F32, "bwd_ffn_up", after=tok, a_cut=True, b_cut=True)
    dh, dm, d_ln_ffn_pre, d_ln_mix_post = _ffn_residual_bwd(dy, dhn, h, ln_ffn_pre, m, ln_mix_post)

    d_w_out = _matmul(mix, dm, "tn", BF16, "grad_w_out")
    st_xo, tok = _spread_start(_row_blocks(d_w_out), True, "exchange_w_out_start", d_ln_ffn_pre)
    dmix = _matmul(dm, w_out_f, "nt", F32, "bwd_proj_out", after=tok)
    dq_sb, dk_sb, dv_sb, d_sb_gain = _sb_attention_bwd(sb_qkv, sb_out_gain, o_sb, sb_ltot, dmix)
    dqn, dkn, dvn, dgc_full, dbeta_full, dz, d_dn_gain = _dn_scan_bwd(
        qn, kn, vn, gc_full, beta_full, proj, z0, dn_out_gain, o_dn, tm_all, s_all, dmix, SBW)
    du_q, dcw_q = _dn_branch_bwd(proj, qkv0, dn_cw, 0, True, HEAD_DIM ** -0.5, dqn)
    du_k, dcw_k = _dn_branch_bwd(proj, qkv0 + DNW, dn_cw, DNW, True, 1.0, dkn)
    du_v, dcw_v = _dn_branch_bwd(proj, qkv0 + 2 * DNW, dn_cw, 2 * DNW, False, 1.0, dvn)
    dba, d_a_log_l, d_dt_bias_l = _dn_gates_bwd(proj, gate_block, a_log_l, dt_bias_l, dgc_full, dbeta_full)
    dproj = jnp.concatenate([dq_sb, dk_sb.astype(BF16), dv_sb.astype(BF16), du_q, du_k, du_v, dz, dba], axis=1)
    d_w_in = _matmul(xn, dproj, "tn", BF16, "grad_w_in", tm_cap=512, tn_cap=2432)[:, :in_cols]
    st_xi, tok = _spread_start(_col_blocks(d_w_in), True, "exchange_w_in_start", d_sb_gain)
    dxn = _matmul(dproj, w_in_f, "nt", F32, "bwd_proj_in", tk_cap=2432, after=tok)
    grad_x, d_ln_mix_pre = _input_bwd(dh, dxn, x2, ln_mix_pre)

    big = {}
    after = grad_x
    for n, st, w_, m_, v_ in [("w_down", st_xd, w_down, m_w_down, v_w_down), ("w_up", st_xu, w_up, m_w_up, v_w_up),
                              ("w_out", st_xo, w_out, m_w_out, v_w_out)]:
        got = _spread_wait(st, True, "exchange_" + n + "_wait", after)
        big[n] = _adamw_sharded(got, w_, m_, v_, "adamw_" + n)
        after = big[n][1]

    d_dn_cw = jnp.concatenate([dcw_q[:SHORT_CONV], dcw_k[:SHORT_CONV], dcw_v[:SHORT_CONV]], axis=1)
    small = [loss_part[:, :1], d_sb_gain, d_a_log_l[:, DN_HEADS:2 * DN_HEADS], d_dt_bias_l[:, DN_HEADS:2 * DN_HEADS], d_dn_gain,
             d_ln_mix_pre, d_ln_mix_post, d_ffn_cwb[FFN_CONV:FFN_CONV + 1], d_ln_ffn_pre, d_ln_ffn_post,
             d_dn_cw, d_ffn_cwb[:FFN_CONV]]
    shapes = [a.shape for a in small]
    red = _unpack(_all_reduce_packed(_pack(small), after), shapes)
    loss = red[0].reshape(())
    me = _index(_my_place())
    g_dn_cw = lax.dynamic_slice_in_dim(red[10], me * dn_conv_w.shape[2], dn_conv_w.shape[2], axis=1)
    g_ffn_cw = lax.dynamic_slice_in_dim(red[11], me * ffn_conv_w.shape[2], ffn_conv_w.shape[2], axis=1)
    names = ["sb_out_gain", "dn_conv_w", "dn_a_log", "dn_dt_bias", "dn_out_gain", "ln_mix_pre", "ln_mix_post",
             "ffn_conv_w", "ffn_conv_b", "ln_ffn_pre", "ln_ffn_post"]
    g_small = dict(sb_out_gain=red[1], dn_conv_w=g_dn_cw[None], dn_a_log=red[2], dn_dt_bias=red[3], dn_out_gain=red[4],
                   ln_mix_pre=red[5], ln_mix_post=red[6], ffn_conv_w=g_ffn_cw[None], ffn_conv_b=red[7],
                   ln_ffn_pre=red[8], ln_ffn_post=red[9])
    w_small = dict(sb_out_gain=sb_out_gain, dn_conv_w=dn_conv_w, dn_a_log=dn_a_log, dn_dt_bias=dn_dt_bias,
                   dn_out_gain=dn_out_gain, ln_mix_pre=ln_mix_pre, ln_mix_post=ln_mix_post, ffn_conv_w=ffn_conv_w,
                   ffn_conv_b=ffn_conv_b, ln_ffn_pre=ln_ffn_pre, ln_ffn_post=ln_ffn_post)
    m_small = dict(sb_out_gain=m_sb_out_gain, dn_conv_w=m_dn_conv_w, dn_a_log=m_dn_a_log, dn_dt_bias=m_dn_dt_bias,
                   dn_out_gain=m_dn_out_gain, ln_mix_pre=m_ln_mix_pre, ln_mix_post=m_ln_mix_post, ffn_conv_w=m_ffn_conv_w,
                   ffn_conv_b=m_ffn_conv_b, ln_ffn_pre=m_ln_ffn_pre, ln_ffn_post=m_ln_ffn_post)
    v_small = dict(sb_out_gain=v_sb_out_gain, dn_conv_w=v_dn_conv_w, dn_a_log=v_dn_a_log, dn_dt_bias=v_dn_dt_bias,
                   dn_out_gain=v_dn_out_gain, ln_mix_pre=v_ln_mix_pre, ln_mix_post=v_ln_mix_post, ffn_conv_w=v_ffn_conv_w,
                   ffn_conv_b=v_ffn_conv_b, ln_ffn_pre=v_ln_ffn_pre, ln_ffn_post=v_ln_ffn_post)
    sshapes = [w_small[n].shape for n in names]
    upd = _adamw_packed(_pack([g_small[n] for n in names]), _pack([w_small[n] for n in names]),
                        _pack([m_small[n] for n in names]), _pack([v_small[n] for n in names]))
    d_small, nm_small, nv_small = [dict(zip(names, _unpack(p, sshapes))) for p in upd]

    got = _spread_wait(st_xi, True, "exchange_w_in_wait", d_small["ln_ffn_post"])
    big["w_in"] = _adamw_sharded(got, w_in, m_w_in, v_w_in, "adamw_w_in")

    order = ["w_in", "sb_out_gain", "dn_conv_w", "dn_a_log", "dn_dt_bias", "dn_out_gain", "w_out", "ln_mix_pre",
             "ln_mix_post", "w_up", "ffn_conv_w", "ffn_conv_b", "w_down", "ln_ffn_pre", "ln_ffn_post"]
    pick = lambda n, i: big[n][i] if n in big else [g_small, d_small, nm_small, nv_small][i][n].reshape(w_small[n].shape)
    return (loss, grad_x[None], *[pick(n, 0) for n in order], *[pick(n, 1) for n in order],
            *[pick(n, 2) for n in order], *[pick(n, 3) for n in order])
```

```python
import functools
import math

import jax
import jax.numpy as jnp
from jax import lax
from jax.experimental import pallas as pl
from jax.experimental.pallas import tpu as pltpu

F32 = jnp.float32
BF16 = jnp.bfloat16

N_DEV = 8
HEAD_DIM = 128
SB_HEADS = 8
DN_HEADS = 8
DN_CHUNK = 128
DN_GROUP = 8
INV_BLOCK = 16
SB_KEYS = 256
SHORT_CONV = 4
FFN_CONV = 3
EPS = 1e-6
LANES = 128
SUBLANES = 8
VMEM_CAP = 56 * 2**20

ADAM_LR = 0.001
ADAM_B1 = 0.9
ADAM_B2 = 0.999
ADAM_EPS = 1e-08
ADAM_WD = 0.01
ADAM_STEP = 10

MESH = pl.DeviceIdType.MESH

assert HEAD_DIM == DN_CHUNK == LANES


def _tile(n, cap, mult):
    if n <= cap:
        return n
    t = (cap // mult) * mult
    while t >= mult:
        if n % t == 0:
            return t
        t -= mult
    raise ValueError(f"no tile for {n} under {cap} in multiples of {mult}")


def _params(sem, vmem_bytes):
    limit = int(min(VMEM_CAP, max(vmem_bytes, 16 * 2**20)))
    if not sem:
        return pltpu.CompilerParams(vmem_limit_bytes=limit)
    return pltpu.CompilerParams(dimension_semantics=sem, vmem_limit_bytes=limit)


def _nbytes(shape, dtype):
    return math.prod(shape) * jnp.dtype(dtype).itemsize


_NN = (((1,), (0,)), ((), ()))
_NT = (((1,), (1,)), ((), ()))
_TN = (((0,), (0,)), ((), ()))


def _batched(dims, ndim):
    if ndim == 2:
        return dims
    (ca,), (cb,) = dims[0]
    return (((ca + 1,), (cb + 1,)), ((0,), (0,)))


def _dot(a, b, dims=_NN):
    return lax.dot_general(a.astype(BF16), b.astype(BF16), _batched(dims, a.ndim), preferred_element_type=F32)


def _split2(x):
    hi = x.astype(BF16)
    lo = (x - hi.astype(F32)).astype(BF16)
    return hi, lo


def _split3(x):
    hi = x.astype(BF16)
    r = x - hi.astype(F32)
    mid = r.astype(BF16)
    lo = (r - mid.astype(F32)).astype(BF16)
    return hi, mid, lo


def _dot01(x, m01, passes=3):
    parts = _split3(x) if passes == 3 else _split2(x)
    out = None
    for p in parts:
        t = lax.dot_general(p, m01, _NN, preferred_element_type=F32)
        out = t if out is None else out + t
    return out


def _dot01_left(m01, x, passes=3):
    parts = _split3(x) if passes == 3 else _split2(x)
    out = None
    for p in parts:
        t = lax.dot_general(m01, p, _NN, preferred_element_type=F32)
        out = t if out is None else out + t
    return out


def _mm3(a, b, dims=_NN):
    ah, al = _split2(a)
    bh, bl = _split2(b)
    d = functools.partial(lax.dot_general, dimension_numbers=_batched(dims, a.ndim), preferred_element_type=F32)
    return d(ah, bh) + (d(ah, bl) + d(al, bh))


def _rowsum(x):
    return jnp.sum(x, axis=-1, keepdims=True)


def _t(x):
    return jnp.swapaxes(x, -1, -2)


def _sigmoid(x):
    return 1.0 / (1.0 + jnp.exp(-x))


def _softplus(x):
    return jnp.maximum(x, 0.0) + jnp.log(1.0 + jnp.exp(-jnp.abs(x)))


def _silu(x):
    return x * _sigmoid(x)


def _silu_grad(x):
    s = _sigmoid(x)
    return s * (1.0 + x * (1.0 - s))


_GELU_C = math.sqrt(2.0 / math.pi)


def _gelu(x):
    return 0.5 * x * (1.0 + jnp.tanh(_GELU_C * (x + 0.044715 * x * x * x)))


def _gelu_and_grad(x):
    x2 = x * x
    th = jnp.tanh(_GELU_C * (x + 0.044715 * x2 * x))
    half = 0.5 * (1.0 + th)
    return x * half, half + 0.5 * x * (1.0 - th * th) * (_GELU_C * (1.0 + 3.0 * 0.044715 * x2))


def _rms(x, g):
    r = lax.rsqrt(jnp.mean(x * x, axis=-1, keepdims=True) + EPS)
    return x * r * g


def _rms_bwd(dy, x, g):
    r = lax.rsqrt(jnp.mean(x * x, axis=-1, keepdims=True) + EPS)
    xh = x * r
    gdy = dy * g
    dx = r * (gdy - xh * jnp.mean(gdy * xh, axis=-1, keepdims=True))
    return dx, jnp.sum(dy * xh, axis=-2, keepdims=True)


def _iota2(shape, axis):
    return lax.broadcasted_iota(jnp.int32, shape, axis)


def _shift_down(cur, prev8, k):
    n = cur.shape[0]
    r = pltpu.roll(cur, k, 0)
    pr = pltpu.roll(prev8, k, 0)
    head = jnp.where(_iota2(pr.shape, 0) < k, pr, r[0:SUBLANES])
    if n == SUBLANES:
        return head
    return jnp.concatenate([head, r[SUBLANES:]], axis=0)


def _shift_up(cur, next8, k):
    n = cur.shape[0]
    r = pltpu.roll(cur, n - k, 0)
    nr = pltpu.roll(next8, SUBLANES - k, 0)
    tail = jnp.where(_iota2(nr.shape, 0) >= SUBLANES - k, nr, r[n - SUBLANES:])
    if n == SUBLANES:
        return tail
    return jnp.concatenate([r[:n - SUBLANES], tail], axis=0)


def _causal_conv(cur, prev8, w_ref, taps):
    out = cur * w_ref[taps - 1:taps, :]
    for j in range(taps - 1):
        out = out + _shift_down(cur, prev8, taps - 1 - j) * w_ref[j:j + 1, :]
    return out


def _anti_conv(cur, next8, w_ref, taps):
    out = cur * w_ref[taps - 1:taps, :]
    for j in range(taps - 1):
        out = out + _shift_up(cur, next8, taps - 1 - j) * w_ref[j:j + 1, :]
    return out


def _matmul(a, b, mode, out_dtype, name, tm_cap=1024, tn_cap=1024, tk_cap=2048, after=None,
            a_cut=False, b_cut=False, out_cut=False):
    a_shard = a.shape[2] if a_cut else None
    b_shard = b.shape[2] if b_cut else None
    a_full = (a.shape[1], a.shape[0] * a_shard) if a_cut else a.shape
    b_full = (b.shape[1], b.shape[0] * b_shard) if b_cut else b.shape
    assert not (a_cut and mode == "tn")
    if mode == "nn":
        (M, K), N = a_full, b_full[1]
    elif mode == "nt":
        (M, K), N = a_full, b_full[0]
    else:
        (K, M), N = a_full, b_full[1]
    n_unit = b_shard if (b_cut and mode != "nt") else N
    k_unit = math.gcd(a_shard or K, b_shard if (b_cut and mode == "nt") else K)
    tm = _tile(M, tm_cap, LANES)
    tn = N // N_DEV if out_cut else _tile(n_unit, tn_cap, LANES)
    tk = _tile(k_unit, tk_cap, LANES)
    assert n_unit % tn == 0 and k_unit % tk == 0
    nk = K // tk
    dims = {"nn": _NN, "nt": _NT, "tn": _TN}[mode]
    if a_cut:
        pa = a_shard // tk
        a_spec = pl.BlockSpec((None, tm, tk), lambda i, j, k: (k // pa, i, k % pa))
    elif mode == "tn":
        a_spec = pl.BlockSpec((tk, tm), lambda i, j, k: (k, i))
    else:
        a_spec = pl.BlockSpec((tm, tk), lambda i, j, k: (i, k))
    if b_cut and mode == "nt":
        pb = b_shard // tk
        b_spec = pl.BlockSpec((None, tn, tk), lambda i, j, k: (k // pb, j, k % pb))
    elif b_cut:
        pb = b_shard // tn
        b_spec = pl.BlockSpec((None, tk, tn), lambda i, j, k: (j // pb, k, j % pb))
    elif mode == "nt":
        b_spec = pl.BlockSpec((tn, tk), lambda i, j, k: (j, k))
    else:
        b_spec = pl.BlockSpec((tk, tn), lambda i, j, k: (k, j))
    if out_cut:
        out_spec, out_shape = pl.BlockSpec((None, tm, tn), lambda i, j, k: (j, i, 0)), (N_DEV, M, tn)
    else:
        out_spec, out_shape = pl.BlockSpec((tm, tn), lambda i, j, k: (i, j)), (M, N)

    def body(a_ref, b_ref, *rest):
        if nk == 1:
            rest[-1][...] = lax.dot_general(a_ref[...], b_ref[...], dims, preferred_element_type=F32).astype(rest[-1].dtype)
            return
        o_ref, acc_ref = rest[-2:]
        k = pl.program_id(2)

        @pl.when(k == 0)
        def _():
            acc_ref[...] = jnp.zeros_like(acc_ref)

        acc_ref[...] += lax.dot_general(a_ref[...], b_ref[...], dims, preferred_element_type=F32)

        @pl.when(k == nk - 1)
        def _():
            o_ref[...] = acc_ref[...].astype(o_ref.dtype)

    vmem = 2 * (_nbytes((tm, tk), a.dtype) + _nbytes((tk, tn), b.dtype) + _nbytes((tm, tn), out_dtype)) + _nbytes((tm, tn), F32)
    tokens = [] if after is None else [after]
    return pl.pallas_call(
        body, name=name, grid=(M // tm, N // tn, nk),
        in_specs=[a_spec, b_spec] + [pl.BlockSpec(t.shape, lambda i, j, k: (0, 0)) for t in tokens],
        out_specs=out_spec,
        out_shape=jax.ShapeDtypeStruct(out_shape, out_dtype),
        scratch_shapes=[] if nk == 1 else [pltpu.VMEM((tm, tn), F32)],
        compiler_params=_params(("parallel", "parallel", "arbitrary"), vmem + 4 * 2**20),
    )(a, b, *tokens)


def _row_call(body, name, T, D, ins, outs, tr, acc_outs=()):
    def spec(a, kind):
        if kind == "row":
            return pl.BlockSpec((tr, a.shape[1]), lambda i: (i, 0))
        return pl.BlockSpec(a.shape, lambda i: (0, 0))
    in_specs = [spec(a, k) for a, k in ins]
    out_specs = [spec(a, k) for a, k in outs] + [spec(a, "vec") for a in acc_outs]
    out_shape = [a for a, _ in outs] + list(acc_outs)
    vmem = 2 * sum(_nbytes((tr, a.shape[1]) if k == "row" else a.shape, a.dtype) for a, k in list(ins) + list(outs))
    return pl.pallas_call(
        body, name=name, grid=(T // tr,), in_specs=in_specs, out_specs=out_specs, out_shape=out_shape,
        compiler_params=_params(("arbitrary",), 3 * vmem + 8 * 2**20),
    )(*[a for a, _ in ins])


def _sds(shape, dtype):
    return jax.ShapeDtypeStruct(shape, dtype)


def _accumulate(ref, val):
    @pl.when(pl.program_id(0) == 0)
    def _():
        ref[...] = jnp.zeros_like(ref)
    ref[...] += val


def _norm_in(x, g):
    T, D = x.shape

    def body(x_ref, g_ref, o_ref):
        o_ref[...] = _rms(x_ref[...], g_ref[...]).astype(BF16)

    return _row_call(body, "norm_in", T, D, [(x, "row"), (g, "vec")], [(_sds((T, D), BF16), "row")], _tile(T, 256, 16))[0]


def _mix_residual(x, m, g_post, g_pre):
    T, D = x.shape

    def body(x_ref, m_ref, gp_ref, gn_ref, h_ref, hn_ref):
        h = x_ref[...] + _rms(m_ref[...], gp_ref[...])
        h_ref[...] = h
        hn_ref[...] = _rms(h, gn_ref[...]).astype(BF16)

    return _row_call(body, "mix_residual", T, D, [(x, "row"), (m, "row"), (g_post, "vec"), (g_pre, "vec")],
                     [(_sds((T, D), F32), "row"), (_sds((T, D), BF16), "row")], _tile(T, 256, 16))


def _loss_head(h, f, g_post, target):
    T, D = h.shape

    def body(h_ref, f_ref, g_ref, t_ref, dy_ref, df_ref, dg_ref, loss_ref):
        f = f_ref[...]
        g = g_ref[...]
        diff = h_ref[...] + _rms(f, g) - t_ref[...]
        dy = diff * (1.0 / D)
        dy_ref[...] = dy
        df, dg = _rms_bwd(dy, f, g)
        df_ref[...] = df.astype(BF16)
        _accumulate(dg_ref, dg)
        _accumulate(loss_ref, jnp.full((1, LANES), 0.5 / D, F32) * jnp.sum(diff * diff))

    return _row_call(body, "loss_head", T, D, [(h, "row"), (f, "row"), (g_post, "vec"), (target, "row")],
                     [(_sds((T, D), F32), "row"), (_sds((T, D), BF16), "row")], _tile(T, 256, 16),
                     acc_outs=[_sds((1, D), F32), _sds((1, LANES), F32)])


def _ffn_residual_bwd(dy, dhn, h, g_pre, m, g_post):
    T, D = h.shape

    def body(dy_ref, dhn_ref, h_ref, gn_ref, m_ref, gp_ref, dh_ref, dm_ref, dgn_ref, dgp_ref):
        dhh, dgn = _rms_bwd(dhn_ref[...], h_ref[...], gn_ref[...])
        dh = dy_ref[...] + dhh
        dh_ref[...] = dh
        dm, dgp = _rms_bwd(dh, m_ref[...], gp_ref[...])
        dm_ref[...] = dm.astype(BF16)
        _accumulate(dgn_ref, dgn)
        _accumulate(dgp_ref, dgp)

    return _row_call(body, "ffn_residual_bwd", T, D,
                     [(dy, "row"), (dhn, "row"), (h, "row"), (g_pre, "vec"), (m, "row"), (g_post, "vec")],
                     [(_sds((T, D), F32), "row"), (_sds((T, D), BF16), "row")], _tile(T, 128, 16),
                     acc_outs=[_sds((1, D), F32), _sds((1, D), F32)])


def _input_bwd(dh, dxn, x, g):
    T, D = x.shape

    def body(dh_ref, dxn_ref, x_ref, g_ref, dx_ref, dg_ref):
        dx, dg = _rms_bwd(dxn_ref[...], x_ref[...], g_ref[...])
        dx_ref[...] = dh_ref[...] + dx
        _accumulate(dg_ref, dg)

    return _row_call(body, "input_bwd", T, D, [(dh, "row"), (dxn, "row"), (x, "row"), (g, "vec")],
                     [(_sds((T, D), F32), "row")], _tile(T, 256, 16), acc_outs=[_sds((1, D), F32)])


def _ffn_act(u, conv_w, conv_b):
    T, F2 = u.shape
    F = F2 // 2
    tc = _tile(F, 512, LANES)
    tr = _tile(T, 512, SUBLANES)
    nc = F // tc
    r8 = tr // SUBLANES

    def body(ug_ref, ugp_ref, uv_ref, uvp_ref, wg_ref, wv_ref, bg_ref, bv_ref, a_ref):
        first = pl.program_id(1) == 0
        cg = _causal_conv(ug_ref[...], jnp.where(first, 0.0, ugp_ref[...]), wg_ref, FFN_CONV) + bg_ref[...]
        cv = _causal_conv(uv_ref[...], jnp.where(first, 0.0, uvp_ref[...]), wv_ref, FFN_CONV) + bv_ref[...]
        a_ref[...] = (_gelu(cg) * cv).astype(BF16)

    cur = lambda off: pl.BlockSpec((tr, tc), lambda j, i: (i, j + off))
    prev = lambda off: pl.BlockSpec((SUBLANES, tc), lambda j, i: (jnp.maximum(i * r8 - 1, 0), j + off))
    wsp = lambda off: pl.BlockSpec((FFN_CONV, tc), lambda j, i: (0, j + off))
    bsp = lambda off: pl.BlockSpec((1, tc), lambda j, i: (0, j + off))
    return pl.pallas_call(
        body, name="ffn_act", grid=(nc, T // tr),
        in_specs=[cur(0), prev(0), cur(nc), prev(nc), wsp(0), wsp(nc), bsp(0), bsp(nc)],
        out_specs=pl.BlockSpec((tr, tc), lambda j, i: (i, j)),
        out_shape=_sds((T, F), BF16),
        compiler_params=_params(("parallel", "arbitrary"), 12 * _nbytes((tr, tc), F32) + 8 * 2**20),
    )(u, u, u, u, conv_w, conv_w, conv_b, conv_b)


def _ffn_act_bwd(u, conv_w, conv_b, da):
    T, F2 = u.shape
    F = F2 // 2
    tc = _tile(F, 512, LANES)
    tr = _tile(T, 512, SUBLANES)
    nc = F // tc
    r8 = tr // SUBLANES
    n8 = T // SUBLANES
    K = FFN_CONV

    def body(ug_ref, ugp_ref, ugn_ref, uv_ref, uvp_ref, uvn_ref, da_ref, dan_ref,
             wg_ref, wv_ref, bg_ref, bv_ref, du_ref, dwb_ref):
        i = pl.program_id(1)
        first = i == 0
        last = i == pl.num_programs(1) - 1

        def dconv(ug, ug_prev, uv, uv_prev, da_):
            cg = _causal_conv(ug, ug_prev, wg_ref, K) + bg_ref[...]
            cv = _causal_conv(uv, uv_prev, wv_ref, K) + bv_ref[...]
            act, act_grad = _gelu_and_grad(cg)
            return da_ * cv * act_grad, da_ * act

        ug, uv = ug_ref[...], uv_ref[...]
        ug_prev, uv_prev = jnp.where(first, 0.0, ugp_ref[...]), jnp.where(first, 0.0, uvp_ref[...])
        dcg, dcv = dconv(ug, ug_prev, uv, uv_prev, da_ref[...])
        dcgn, dcvn = dconv(ugn_ref[...], ug[tr - SUBLANES:], uvn_ref[...], uv[tr - SUBLANES:], dan_ref[...])
        du_ref[0] = _anti_conv(dcg, jnp.where(last, 0.0, dcgn), wg_ref, K).astype(BF16)
        du_ref[1] = _anti_conv(dcv, jnp.where(last, 0.0, dcvn), wv_ref, K).astype(BF16)

        @pl.when(first)
        def _():
            dwb_ref[...] = jnp.zeros_like(dwb_ref)

        for half, (dc, uo, uo_prev) in enumerate([(dcg, ug, ug_prev), (dcv, uv, uv_prev)]):
            rows = [jnp.sum(dc * _shift_down(uo, uo_prev, K - 1 - t), axis=0, keepdims=True) for t in range(K - 1)]
            rows += [jnp.sum(dc * uo, axis=0, keepdims=True), jnp.sum(dc, axis=0, keepdims=True)]
            rows += [jnp.zeros_like(rows[0])] * (SUBLANES - len(rows))
            dwb_ref[half] += jnp.concatenate(rows, axis=0)

    cur = lambda off: pl.BlockSpec((tr, tc), lambda j, i: (i, j + off))
    prev = lambda off: pl.BlockSpec((SUBLANES, tc), lambda j, i: (jnp.maximum(i * r8 - 1, 0), j + off))
    nxt = lambda off: pl.BlockSpec((SUBLANES, tc), lambda j, i: (jnp.minimum((i + 1) * r8, n8 - 1), j + off))
    wsp = lambda off: pl.BlockSpec((K, tc), lambda j, i: (0, j + off))
    bsp = lambda off: pl.BlockSpec((1, tc), lambda j, i: (0, j + off))
    return pl.pallas_call(
        body, name="ffn_act_bwd", grid=(nc, T // tr),
        in_specs=[cur(0), prev(0), nxt(0), cur(nc), prev(nc), nxt(nc), cur(0), nxt(0), wsp(0), wsp(nc), bsp(0), bsp(nc)],
        out_specs=[pl.BlockSpec((2, tr, tc), lambda j, i: (0, i, j)), pl.BlockSpec((2, SUBLANES, tc), lambda j, i: (0, 0, j))],
        out_shape=[_sds((2, T, F), BF16), _sds((2, SUBLANES, F), F32)],
        compiler_params=_params(("parallel", "arbitrary"), 24 * _nbytes((tr, tc), F32) + 8 * 2**20),
    )(u, u, u, u, u, u, da, da, conv_w, conv_w, conv_b, conv_b)


def _l2norm(s, scale):
    return s * (lax.rsqrt(jnp.sum(s * s, axis=-1, keepdims=True) + EPS) * scale)


def _dn_branch(proj, col0, conv_w, wcol0, l2, scale):
    T = proj.shape[0]
    W = DN_HEADS * HEAD_DIM
    tr = _tile(T, 2048, SUBLANES)
    r8 = tr // SUBLANES
    cb0, wb0 = col0 // HEAD_DIM, wcol0 // HEAD_DIM

    def body(u_ref, up_ref, w_ref, o_ref):
        first = pl.program_id(1) == 0
        s = _silu(_causal_conv(u_ref[...], jnp.where(first, 0.0, up_ref[...]), w_ref, SHORT_CONV))
        o_ref[...] = _l2norm(s, scale) if l2 else s

    return pl.pallas_call(
        body, name=f"dn_branch_{col0}", grid=(DN_HEADS, T // tr),
        in_specs=[pl.BlockSpec((tr, HEAD_DIM), lambda h, i: (i, cb0 + h)),
                  pl.BlockSpec((SUBLANES, HEAD_DIM), lambda h, i: (jnp.maximum(i * r8 - 1, 0), cb0 + h)),
                  pl.BlockSpec((SHORT_CONV, HEAD_DIM), lambda h, i: (0, wb0 + h))],
        out_specs=pl.BlockSpec((tr, HEAD_DIM), lambda h, i: (i, h)),
        out_shape=_sds((T, W), F32),
        compiler_params=_params(("parallel", "arbitrary"), 32 * _nbytes((tr, HEAD_DIM), F32) + 8 * 2**20),
    )(proj, proj, conv_w)


def _dn_branch_bwd(proj, col0, conv_w, wcol0, l2, scale, dy):
    T = proj.shape[0]
    W = DN_HEADS * HEAD_DIM
    tr = _tile(T, 2048, SUBLANES)
    r8 = tr // SUBLANES
    n8 = T // SUBLANES
    cb0, wb0 = col0 // HEAD_DIM, wcol0 // HEAD_DIM
    K = SHORT_CONV

    def body(u_ref, up_ref, un_ref, dy_ref, dyn_ref, w_ref, du_ref, dw_ref):
        i = pl.program_id(1)
        first = i == 0
        last = i == pl.num_programs(1) - 1

        def dconv(u, u_prev, dy_):
            c = _causal_conv(u, u_prev, w_ref, K)
            if l2:
                s = _silu(c)
                r = lax.rsqrt(jnp.sum(s * s, axis=-1, keepdims=True) + EPS)
                n = s * r
                ds = (scale * r) * (dy_ - n * jnp.sum(dy_ * n, axis=-1, keepdims=True))
            else:
                ds = dy_
            return ds * _silu_grad(c)

        u = u_ref[...]
        u_prev = jnp.where(first, 0.0, up_ref[...])
        dc = dconv(u, u_prev, dy_ref[...])
        dcn = jnp.where(last, 0.0, dconv(un_ref[...], u[tr - SUBLANES:], dyn_ref[...]))
        du_ref[...] = _anti_conv(dc, dcn, w_ref, K).astype(BF16)
        rows = [jnp.sum(dc * _shift_down(u, u_prev, K - 1 - t), axis=0, keepdims=True) for t in range(K - 1)]
        rows += [jnp.sum(dc * u, axis=0, keepdims=True)]
        rows += [jnp.zeros_like(rows[0])] * (SUBLANES - len(rows))
        upd = jnp.concatenate(rows, axis=0)

        @pl.when(first)
        def _():
            dw_ref[...] = jnp.zeros_like(dw_ref)
        dw_ref[...] += upd

    return pl.pallas_call(
        body, name=f"dn_branch_bwd_{col0}", grid=(DN_HEADS, T // tr),
        in_specs=[pl.BlockSpec((tr, HEAD_DIM), lambda h, i: (i, cb0 + h)),
                  pl.BlockSpec((SUBLANES, HEAD_DIM), lambda h, i: (jnp.maximum(i * r8 - 1, 0), cb0 + h)),
                  pl.BlockSpec((SUBLANES, HEAD_DIM), lambda h, i: (jnp.minimum((i + 1) * r8, n8 - 1), cb0 + h)),
                  pl.BlockSpec((tr, HEAD_DIM), lambda h, i: (i, h)),
                  pl.BlockSpec((SUBLANES, HEAD_DIM), lambda h, i: (jnp.minimum((i + 1) * r8, n8 - 1), h)),
                  pl.BlockSpec((K, HEAD_DIM), lambda h, i: (0, wb0 + h))],
        out_specs=[pl.BlockSpec((tr, HEAD_DIM), lambda h, i: (i, h)),
                   pl.BlockSpec((SUBLANES, HEAD_DIM), lambda h, i: (0, h))],
        out_shape=[_sds((T, W), BF16), _sds((SUBLANES, W), F32)],
        compiler_params=_params(("parallel", "arbitrary"), 32 * _nbytes((tr, HEAD_DIM), F32) + 8 * 2**20),
    )(proj, proj, proj, dy, dy, conv_w)


def _lane_masks(shape):
    lane = _iota2(shape, 1)
    return lane < DN_HEADS, (lane >= DN_HEADS) & (lane < 2 * DN_HEADS)


def _expand01(off):
    r = _iota2((LANES, DN_HEADS * HEAD_DIM), 0)
    c = _iota2((LANES, DN_HEADS * HEAD_DIM), 1)
    return (r == jnp.right_shift(c, int(math.log2(HEAD_DIM))) + off).astype(BF16)


def _select01(off):
    r = _iota2((DN_HEADS * HEAD_DIM, LANES), 0)
    c = _iota2((DN_HEADS * HEAD_DIM, LANES), 1)
    return (r == (c - off) * HEAD_DIM).astype(BF16)


def _dn_gates(proj, gate_block, a_log_l, dt_bias_l):
    T = proj.shape[0]
    C = DN_CHUNK
    W = DN_HEADS * HEAD_DIM

    def body(ba_ref, al_ref, dt_ref, gc_ref, beta_ref):
        ba = ba_ref[...]
        is_b, is_a = _lane_masks(ba.shape)
        g = jnp.where(is_a, -jnp.exp(al_ref[...]) * _softplus(ba + dt_ref[...]), 0.0)
        beta = jnp.where(is_b, _sigmoid(ba), 0.0)
        tri = (_iota2((C, C), 0) >= _iota2((C, C), 1)).astype(BF16)
        gc = _dot01_left(tri, g)
        gc_ref[...] = _dot01(gc, _expand01(DN_HEADS))
        beta_ref[...] = _dot01(beta, _expand01(0))

    vec = pl.BlockSpec((1, LANES), lambda n: (0, 0))
    return pl.pallas_call(
        body, name="dn_gates", grid=(T // C,),
        in_specs=[pl.BlockSpec((C, LANES), lambda n: (n, gate_block)), vec, vec],
        out_specs=[pl.BlockSpec((C, W), lambda n: (n, 0))] * 2,
        out_shape=[_sds((T, W), F32)] * 2,
        compiler_params=_params(("parallel",), 16 * 2**20),
    )(proj, a_log_l, dt_bias_l)


def _dn_gates_bwd(proj, gate_block, a_log_l, dt_bias_l, dgc_full, dbeta_full):
    T = proj.shape[0]
    C = DN_CHUNK
    W = DN_HEADS * HEAD_DIM

    def body(ba_ref, al_ref, dt_ref, dgc_ref, dbeta_ref, dba_ref, dal_ref, ddt_ref):
        ba = ba_ref[...]
        is_b, is_a = _lane_masks(ba.shape)
        ea = jnp.exp(al_ref[...])
        pre = ba + dt_ref[...]
        g = jnp.where(is_a, -ea * _softplus(pre), 0.0)
        beta = _sigmoid(ba)
        dgc = _dot01(dgc_ref[...], _select01(DN_HEADS))
        dbeta = _dot01(dbeta_ref[...], _select01(0))
        triu = (_iota2((C, C), 0) <= _iota2((C, C), 1)).astype(BF16)
        dg = _dot01_left(triu, dgc)
        da = jnp.where(is_a, dg * (-ea) * _sigmoid(pre), 0.0)
        dba_ref[...] = (da + jnp.where(is_b, dbeta * beta * (1.0 - beta), 0.0)).astype(BF16)
        _accumulate(dal_ref, jnp.sum(dg * g, axis=0, keepdims=True))
        _accumulate(ddt_ref, jnp.sum(da, axis=0, keepdims=True))

    vec = pl.BlockSpec((1, LANES), lambda n: (0, 0))
    full = pl.BlockSpec((C, W), lambda n: (n, 0))
    return pl.pallas_call(
        body, name="dn_gates_bwd", grid=(T // C,),
        in_specs=[pl.BlockSpec((C, LANES), lambda n: (n, gate_block)), vec, vec, full, full],
        out_specs=[pl.BlockSpec((C, LANES), lambda n: (n, 0)), vec, vec],
        out_shape=[_sds((T, LANES), BF16), _sds((1, LANES), F32), _sds((1, LANES), F32)],
        compiler_params=_params(("arbitrary",), 16 * 2**20),
    )(proj, a_log_l, dt_bias_l, dgc_full, dbeta_full)


def _unit_lower_inverse(L):
    C = L.shape[-1]
    row, col = _iota2((C, C), 0), _iota2((C, C), 1)
    eye = (row == col).astype(F32)
    sh = int(math.log2(INV_BLOCK))
    Ld = jnp.where(jnp.right_shift(row, sh) == jnp.right_shift(col, sh), L, 0.0)
    Lo = L - Ld
    X = eye - Ld
    P = Ld
    for _ in range(int(math.log2(INV_BLOCK)) - 1):
        P = _mm3(P, P)
        X = X + _mm3(X, P)
    N = _mm3(X, Lo)
    Y = eye - N
    P = N
    for _ in range(int(math.log2(C // INV_BLOCK)) - 1):
        P = _mm3(P, P)
        Y = Y + _mm3(Y, P)
    return _mm3(Y, X)


def _dn_chunk_common(q, k, v, gc, beta, gl):
    C = q.shape[-2]
    row, col = _iota2((C, C), 0), _iota2((C, C), 1)
    causal, strict = row >= col, row > col
    eg = jnp.exp(gc)
    decay = jnp.where(causal, jnp.exp(jnp.where(causal, gc - _t(gc), 0.0)), 0.0)
    kb, vb = k * beta, v * beta
    L = jnp.where(strict, _dot(kb, k, _NT) * decay, 0.0)
    Aqk = jnp.where(causal, _dot(q, k, _NT) * decay, 0.0)
    ektg = jnp.exp(gl - gc)
    return dict(causal=causal, strict=strict, eg=eg, decay=decay, kb=kb, vb=vb, L=L, Aqk=Aqk, ektg=ektg,
                kbg=kb * eg, kte=k * ektg, qd=q * eg, egl=jnp.exp(gl))


def _dn_scan(qn, kn, vn, gc_full, beta_full, proj, z_col0, gain):
    T, W = qn.shape
    C = DN_CHUNK
    N = T // C
    H = DN_HEADS
    G = DN_GROUP
    GW = G * HEAD_DIM
    zb0 = z_col0 // GW

    def body(q_ref, k_ref, v_ref, gc_ref, beta_ref, z_ref, gain_ref, o_ref, mix_ref, tm_ref, s_ref, S):
        @pl.when(pl.program_id(1) == 0)
        def _():
            S[...] = jnp.zeros_like(S)

        heads = lambda ref, rows=slice(None): jnp.stack([ref[rows, g * HEAD_DIM:(g + 1) * HEAD_DIM] for g in range(G)])
        q, k, v, gc, beta = heads(q_ref), heads(k_ref), heads(v_ref), heads(gc_ref), heads(beta_ref)
        gl = heads(gc_ref, slice(C - 1, C))
        c = _dn_chunk_common(q, k, v, gc, beta, gl)
        Tm = _unit_lower_inverse(c["L"])
        u = _dot(Tm, c["vb"])
        w = _dot(Tm, c["kbg"])
        S0 = S[...]
        vnew = u - _dot(w, S0)
        o = _dot(c["qd"], S0) + _dot(c["Aqk"], vnew)
        S[...] = S0 * c["egl"] + _dot(c["kte"], vnew, _TN)
        tm_ref[...] = Tm
        s_ref[...] = S0
        mix = (_rms(o, gain_ref[...]) * _silu(heads(z_ref))).astype(BF16)
        for g in range(G):
            sl = slice(g * HEAD_DIM, (g + 1) * HEAD_DIM)
            o_ref[:, sl] = o[g]
            mix_ref[:, sl] = mix[g]

    blk = pl.BlockSpec((C, GW), lambda h, n: (n, h))
    mat = pl.BlockSpec((G, None, C, C), lambda h, n: (h, n, 0, 0))
    return pl.pallas_call(
        body, name="dn_scan", grid=(H // G, N),
        in_specs=[blk, blk, blk, blk, blk, pl.BlockSpec((C, GW), lambda h, n: (n, zb0 + h)),
                  pl.BlockSpec((1, HEAD_DIM), lambda h, n: (0, 0))],
        out_specs=[blk, blk, mat, mat],
        out_shape=[_sds((T, W), F32), _sds((T, W), BF16), _sds((H, N, C, C), F32), _sds((H, N, C, C), F32)],
        scratch_shapes=[pltpu.VMEM((G, HEAD_DIM, HEAD_DIM), F32)],
        compiler_params=_params(("parallel", "arbitrary"), 32 * 2**20),
    )(qn, kn, vn, gc_full, beta_full, proj, gain)


def _dn_scan_bwd(qn, kn, vn, gc_full, beta_full, proj, z_col0, gain, o_raw, tm_all, s_all, dmix, dmix_col0):
    T, W = qn.shape
    C = DN_CHUNK
    N = T // C
    H = DN_HEADS
    G = DN_GROUP
    GW = G * HEAD_DIM
    zb0 = z_col0 // GW
    mb0 = dmix_col0 // GW

    def body(q_ref, k_ref, v_ref, gc_ref, beta_ref, z_ref, gain_ref, o_ref, tm_ref, s_ref, dmix_ref,
             dq_ref, dk_ref, dv_ref, dgc_ref, dbeta_ref, dz_ref, dgain_ref, dS):
        @pl.when(pl.program_id(1) == 0)
        def _():
            dS[...] = jnp.zeros_like(dS)

        @pl.when((pl.program_id(0) == 0) & (pl.program_id(1) == 0))
        def _():
            dgain_ref[...] = jnp.zeros_like(dgain_ref)

        heads = lambda ref, rows=slice(None): jnp.stack([ref[rows, g * HEAD_DIM:(g + 1) * HEAD_DIM] for g in range(G)])
        total = lambda x: jnp.sum(jnp.sum(x, axis=-1, keepdims=True), axis=-2, keepdims=True)
        gain = gain_ref[...]
        o, z, dmix = heads(o_ref), heads(z_ref), heads(dmix_ref)
        dz = (dmix * _rms(o, gain) * _silu_grad(z)).astype(BF16)
        do, dgain = _rms_bwd(dmix * _silu(z), o, gain)
        dgain_ref[...] += jnp.sum(dgain, axis=0)

        q, k, v, gc, beta = heads(q_ref), heads(k_ref), heads(v_ref), heads(gc_ref), heads(beta_ref)
        gl = heads(gc_ref, slice(C - 1, C))
        c = _dn_chunk_common(q, k, v, gc, beta, gl)
        Tm, S0, dS1 = tm_ref[...], s_ref[...], dS[...]
        w = _dot(Tm, c["kbg"])
        vnew = _dot(Tm, c["vb"]) - _dot(w, S0)

        dvnew = _dot(c["Aqk"], do, _TN) + _dot(c["kte"], dS1)
        dAqk = jnp.where(c["causal"], _dot(do, vnew, _NT), 0.0)
        dqd = _dot(do, S0, _NT)
        dkte = _dot(vnew, dS1, _NT)
        dgl = total(dS1 * S0) * c["egl"]
        dw = -_dot(dvnew, S0, _NT)
        dS[...] = dS1 * c["egl"] + _dot(c["qd"], do, _TN) - _dot(w, dvnew, _TN)

        dTm = _dot(dvnew, c["vb"], _NT) + _dot(dw, c["kbg"], _NT)
        dvb = _dot(Tm, dvnew, _TN)
        dkbg = _dot(Tm, dw, _TN)
        dL = jnp.where(c["strict"], -_mm3(_mm3(Tm, dTm, _TN), Tm, _NT), 0.0)
        dP = dL * c["decay"]
        dQ = dAqk * c["decay"]
        M = dL * c["L"] + dAqk * c["Aqk"]
        dkb = _dot(dP, k) + dkbg * c["eg"]
        dk = _dot(dP, c["kb"], _TN) + _dot(dQ, q, _TN) + dkte * c["ektg"] + dkb * beta
        dq = _dot(dQ, k) + dqd * c["eg"]
        tk = _rowsum(dkte * c["kte"])
        dgc = (_rowsum(M) - _rowsum(_t(M)) + _rowsum(dqd * c["qd"]) - tk + _rowsum(dkbg * c["kbg"]))
        dgl = dgl + total(tk)
        dgc = jnp.broadcast_to(dgc, q.shape) + jnp.where(_iota2((C, HEAD_DIM), 0) == C - 1, dgl, 0.0)
        dv = dvb * beta
        dbeta = jnp.broadcast_to(_rowsum(dkb * k) + _rowsum(dvb * v), q.shape)
        for g in range(G):
            sl = slice(g * HEAD_DIM, (g + 1) * HEAD_DIM)
            dz_ref[:, sl] = dz[g]
            dq_ref[:, sl] = dq[g]
            dk_ref[:, sl] = dk[g]
            dv_ref[:, sl] = dv[g]
            dgc_ref[:, sl] = dgc[g]
            dbeta_ref[:, sl] = dbeta[g]

    rev = lambda off: pl.BlockSpec((C, GW), lambda h, n: (N - 1 - n, off + h))
    mat = pl.BlockSpec((G, None, C, C), lambda h, n: (h, N - 1 - n, 0, 0))
    vec = pl.BlockSpec((1, HEAD_DIM), lambda h, n: (0, 0))
    return pl.pallas_call(
        body, name="dn_scan_bwd", grid=(H // G, N),
        in_specs=[rev(0), rev(0), rev(0), rev(0), rev(0), rev(zb0), vec, rev(0), mat, mat, rev(mb0)],
        out_specs=[rev(0)] * 6 + [vec],
        out_shape=[_sds((T, W), F32)] * 5 + [_sds((T, W), BF16), _sds((1, HEAD_DIM), F32)],
        scratch_shapes=[pltpu.VMEM((G, HEAD_DIM, HEAD_DIM), F32)],
        compiler_params=_params(("arbitrary", "arbitrary"), 40 * 2**20),
    )(qn, kn, vn, gc_full, beta_full, proj, gain, o_raw, tm_all, s_all, dmix)


def _sb_terms(z, ahead, first_key):
    lb = jnp.minimum(z, 0.0) - jnp.log(1.0 + jnp.exp(-jnp.abs(z)))
    if ahead is None:
        return None, lb, lb - z
    valid = ahead < -first_key
    return valid, lb, jnp.where(valid, lb - z, 0.0)


def _masked(valid, x):
    return x if valid is None else jnp.where(valid, x, 0.0)


def _sb_attention(qkv, gain, tq_cap=1024):
    T = qkv.shape[0]
    H = SB_HEADS
    B = min(SB_KEYS, T)
    TQ = _tile(T, tq_cap, B)
    per = TQ // B

    assert per >= 2
    n_saved = per * (T // TQ) * (T // TQ + 1) // 2

    def body(q_ref, k_ref, v_ref, gain_ref, o_ref, mix_ref, att_hbm, lb_hbm, att_buf, lb_buf, sems):
        h, i = pl.program_id(0), pl.program_id(1)
        q = q_ref[...]
        upper = (_iota2((B, B), 0) > _iota2((B, B), 1)).astype(BF16)
        ahead = _iota2((TQ, B), 1) - _iota2((TQ, B), 0)
        last = (i + 1) * per - 1
        base = per * (i * (i + 1) // 2)

        def save(slot, pair):
            return (pltpu.make_async_copy(att_buf.at[slot], att_hbm.at[h, pair], sems.at[0, slot]),
                    pltpu.make_async_copy(lb_buf.at[slot], lb_hbm.at[h, pair], sems.at[1, slot]))

        def step(jj, carry, mask):
            acc, R = carry
            j = last - jj
            slot = jj % 2

            @pl.when(jj >= 2)
            def _():
                for c in save(slot, 0):
                    c.wait()

            rows = pl.ds(pl.multiple_of(j * B, B), B)
            z = _dot(q, k_ref[rows, :], _NT) * (HEAD_DIM ** -0.5)
            valid, lb, l1m = _sb_terms(z, ahead if mask else None, j * B - i * TQ)
            att = _masked(valid, jnp.exp(lb + R + _dot01(l1m, upper, passes=2))).astype(BF16)
            att_buf[slot] = att
            lb_buf[slot] = (lb if valid is None else jnp.where(valid, lb, -1e30)).astype(BF16)
            for c in save(slot, base + j):
                c.start()
            return acc + _dot(att, v_ref[rows, :]), R + _rowsum(l1m)

        carry = (jnp.zeros((TQ, HEAD_DIM), F32), jnp.zeros((TQ, 1), F32))
        carry = lax.fori_loop(0, per, functools.partial(step, mask=True), carry)
        acc, _ = lax.fori_loop(per, last + 1, functools.partial(step, mask=False), carry)
        for slot in range(2):
            for c in save(slot, 0):
                c.wait()
        o_ref[...] = acc
        mix_ref[...] = _rms(acc, gain_ref[...]).astype(BF16)

    head = lambda off: pl.BlockSpec((T, HEAD_DIM), lambda h, i: (0, off + h))
    blk = pl.BlockSpec((TQ, HEAD_DIM), lambda h, i: (i, h))
    return pl.pallas_call(
        body, name="sb_attention", grid=(H, T // TQ),
        in_specs=[blk, head(H), head(2 * H), pl.BlockSpec((1, HEAD_DIM), lambda h, i: (0, 0))],
        out_specs=[blk, blk, ANY_SPEC, ANY_SPEC],
        out_shape=[_sds((T, H * HEAD_DIM), F32), _sds((T, H * HEAD_DIM), BF16),
                   _sds((H, n_saved, TQ, B), BF16), _sds((H, n_saved, TQ, B), BF16)],
        scratch_shapes=[pltpu.VMEM((2, TQ, B), BF16), pltpu.VMEM((2, TQ, B), BF16), pltpu.SemaphoreType.DMA((2, 2))],
        compiler_params=_params(("parallel", "arbitrary"), 8 * _nbytes((T, HEAD_DIM), BF16) + 32 * _nbytes((TQ, B), F32)),
    )(qkv, qkv, qkv, gain)


def _sb_attention_bwd(qkv, gain, o_raw, att_all, lb_all, dmix):
    T = qkv.shape[0]
    H = SB_HEADS
    TQ, B = att_all.shape[2:]
    per = TQ // B
    scale = HEAD_DIM ** -0.5

    def body(q_ref, k_ref, v_ref, gain_ref, o_ref, dmix_ref, att_hbm, lb_hbm, dq_ref, dk_ref, dv_ref, dgain_ref,
             att_buf, lb_buf, sems):
        h, i = pl.program_id(0), pl.program_id(1)

        @pl.when(i == 0)
        def _():
            dk_ref[...] = jnp.zeros_like(dk_ref)
            dv_ref[...] = jnp.zeros_like(dv_ref)

        @pl.when((pl.program_id(0) == 0) & (i == 0))
        def _():
            dgain_ref[...] = jnp.zeros_like(dgain_ref)

        q = q_ref[...]
        o = o_ref[...]
        do, dgain = _rms_bwd(dmix_ref[...], o, gain_ref[...])
        dgain_ref[...] += dgain
        do_b = do.astype(BF16)
        before = (_iota2((B, B), 0) < _iota2((B, B), 1)).astype(BF16)
        pairs = (i + 1) * per
        base = per * (i * (i + 1) // 2)

        def fetch(slot, pair):
            return (pltpu.make_async_copy(att_hbm.at[h, pair], att_buf.at[slot], sems.at[0, slot]),
                    pltpu.make_async_copy(lb_hbm.at[h, pair], lb_buf.at[slot], sems.at[1, slot]))

        for c in fetch(0, base):
            c.start()

        def step(j, carry):
            dq, PG = carry
            slot = j % 2
            for c in fetch(slot, 0):
                c.wait()

            @pl.when(j + 1 < pairs)
            def _():
                for c in fetch(1 - slot, base + j + 1):
                    c.start()

            rows = pl.ds(pl.multiple_of(j * B, B), B)
            kj = k_ref[rows, :]
            att = att_buf[slot]
            sig = jnp.exp(lb_buf[slot].astype(F32))
            G = _dot(do_b, v_ref[rows, :], _NT) * att.astype(F32)
            dv_ref[rows, :] += _dot(att, do_b, _TN)
            cum = PG + _dot01(G, before, passes=2)
            dz = (G * (1.0 - sig) - sig * cum) * scale
            dk_ref[rows, :] += _dot(dz, q, _TN)
            return dq + _dot(dz, kj), PG + _rowsum(G)

        dq, _ = lax.fori_loop(0, pairs, step, (jnp.zeros((TQ, HEAD_DIM), F32), jnp.zeros((TQ, 1), F32)))
        dq_ref[...] = dq.astype(BF16)

    head = lambda off: pl.BlockSpec((T, HEAD_DIM), lambda h, i: (0, off + h))
    blk = pl.BlockSpec((TQ, HEAD_DIM), lambda h, i: (i, h))
    vec = pl.BlockSpec((1, HEAD_DIM), lambda h, i: (0, 0))
    return pl.pallas_call(
        body, name="sb_attention_bwd", grid=(H, T // TQ),
        in_specs=[blk, head(H), head(2 * H), vec, blk, blk, ANY_SPEC, ANY_SPEC],
        out_specs=[blk, head(0), head(0), vec],
        out_shape=[_sds((T, H * HEAD_DIM), BF16), _sds((T, H * HEAD_DIM), F32), _sds((T, H * HEAD_DIM), F32),
                   _sds((1, HEAD_DIM), F32)],
        scratch_shapes=[pltpu.VMEM((2, TQ, B), BF16), pltpu.VMEM((2, TQ, B), BF16), pltpu.SemaphoreType.DMA((2, 2))],
        compiler_params=_params(("arbitrary", "arbitrary"), 8 * _nbytes((T, HEAD_DIM), F32) + 32 * _nbytes((TQ, B), F32)),
    )(qkv, qkv, qkv, gain, o_raw, dmix, att_all, lb_all)


def _adamw_math(w, g, m, v):
    m = ADAM_B1 * m + (1.0 - ADAM_B1) * g
    v = ADAM_B2 * v + (1.0 - ADAM_B2) * (g * g)
    m_hat = m / (1.0 - ADAM_B1 ** ADAM_STEP)
    v_hat = v / (1.0 - ADAM_B2 ** ADAM_STEP)
    delta = -ADAM_LR * (m_hat / (jnp.sqrt(v_hat) + ADAM_EPS) + ADAM_WD * w)
    return delta, m, v


def _adamw_sharded(parts, w, m, v, name):
    _, R, C = w.shape
    tr = _tile(R, max(SUBLANES, (2**20 // (4 * C)) // SUBLANES * SUBLANES), SUBLANES)

    def body(p_ref, w_ref, m_ref, v_ref, g_ref, d_ref, nm_ref, nv_ref):
        g = p_ref[0].astype(F32)
        for d in range(1, N_DEV):
            g = g + p_ref[d].astype(F32)
        g_ref[...] = g
        d_ref[...], nm_ref[...], nv_ref[...] = _adamw_math(w_ref[...], g, m_ref[...], v_ref[...])

    blk = pl.BlockSpec((None, tr, C), lambda i: (0, i, 0))
    return pl.pallas_call(
        body, name=name, grid=(R // tr,),
        in_specs=[pl.BlockSpec((N_DEV, tr, C), lambda i: (0, i, 0)), blk, blk, blk],
        out_specs=[blk] * 4, out_shape=[_sds((1, R, C), F32)] * 4,
        compiler_params=_params(("parallel",), 40 * 2**20),
    )(parts, w, m, v)


def _adamw_packed(g, w, m, v):
    def body(g_ref, w_ref, m_ref, v_ref, d_ref, nm_ref, nv_ref):
        d_ref[...], nm_ref[...], nv_ref[...] = _adamw_math(w_ref[...], g_ref[...], m_ref[...], v_ref[...])

    return pl.pallas_call(body, name="adamw_packed", out_shape=[_sds(g.shape, F32)] * 3,
                          compiler_params=_params((), 16 * 2**20))(g, w, m, v)


def _my_place():
    x, y, c = lax.axis_index("x"), lax.axis_index("y"), lax.axis_index("c")
    return x, y, c


def _peer(place, k):
    x, y, c = place
    return (1 - x if k & 4 else x, 1 - y if k & 2 else y, 1 - c if k & 1 else c)


def _index(place):
    x, y, c = place
    return 4 * x + 2 * y + c


HBM_SPEC = pl.BlockSpec(memory_space=pltpu.HBM)


def _all_gather(block, name):
    R, C = block.shape

    def body(x_ref, out_ref, send_sems, recv_sems, local_sem):
        me = _my_place()
        sibling = _peer(me, 1)
        chips = [2, 4, 6]

        def copy(sem, origin, to, src=None):
            slot = out_ref.at[_index(origin)]
            return pltpu.make_async_remote_copy(
                src_ref=slot if src is None else src, dst_ref=slot, send_sem=send_sems.at[sem], recv_sem=recv_sems.at[sem],
                device_id=to, device_id_type=MESH)

        mine = pltpu.make_async_copy(x_ref, out_ref.at[_index(me)], local_sem)
        mine.start()
        first = [copy(0, me, sibling, src=x_ref)] + [copy(1 + n, me, _peer(me, k), src=x_ref) for n, k in enumerate(chips)]
        for cp in first:
            cp.start()
        passed = [copy(4 + n, _peer(me, k), sibling) for n, k in enumerate(chips)]
        for n, k in enumerate(chips):
            copy(1 + n, _peer(me, k), me).wait_recv()
            passed[n].start()
        copy(0, sibling, me).wait_recv()
        for n, k in enumerate(chips):
            copy(4 + n, _peer(sibling, k), me).wait_recv()
        for cp in first + passed:
            cp.wait_send()
        mine.wait()

    return pl.pallas_call(
        body, name=name, in_specs=[HBM_SPEC], out_specs=HBM_SPEC,
        out_shape=_sds((N_DEV, R, C), block.dtype),
        scratch_shapes=[pltpu.SemaphoreType.DMA((7,)), pltpu.SemaphoreType.DMA((7,)), pltpu.SemaphoreType.DMA],
    )(block)


SEM_SPEC = pl.BlockSpec(memory_space=pltpu.SEMAPHORE)
ANY_SPEC = pl.BlockSpec(memory_space=pl.ANY)
_EFFECT = pltpu.SideEffectType.DATAFLOW_SIDE_EFFECTING


def _spread_start(x, per_peer, name, after):
    R, C = x.shape[-2:]

    def body(x_ref, land_ref, after_ref, send_sems, recv_sems, x_thru, land_thru, token):
        me = _my_place()
        for k in range(1, N_DEV):
            to = _peer(me, k)
            pltpu.make_async_remote_copy(
                src_ref=x_ref.at[_index(to)] if per_peer else x_ref, dst_ref=land_ref.at[_index(me)],
                send_sem=send_sems.at[k - 1], recv_sem=recv_sems.at[k - 1], device_id=to, device_id_type=MESH).start()
        token[...] = jnp.zeros_like(token)

    land = lax.empty((N_DEV, R, C), x.dtype)
    send_sems, recv_sems, x_thru, land_thru, token = pl.pallas_call(
        body, name=name,
        out_shape=(pltpu.SemaphoreType.DMA((N_DEV - 1,)), pltpu.SemaphoreType.DMA((N_DEV - 1,)),
                   pltpu.HBM(x.shape, x.dtype), pltpu.HBM(land.shape, land.dtype), _sds((SUBLANES, LANES), F32)),
        in_specs=(HBM_SPEC, HBM_SPEC, ANY_SPEC),
        out_specs=(SEM_SPEC, SEM_SPEC, HBM_SPEC, HBM_SPEC, pl.BlockSpec(memory_space=pltpu.VMEM)),
        input_output_aliases={0: 2, 1: 3},
        compiler_params=pltpu.CompilerParams(has_side_effects=_EFFECT),
    )(pltpu.with_memory_space_constraint(x, pltpu.HBM), pltpu.with_memory_space_constraint(land, pltpu.HBM), after)
    return (send_sems, recv_sems, x_thru, land_thru), token


def _spread_wait(state, per_peer, name, after):
    send_sems, recv_sems, x_thru, land_thru = state

    def body(x_ref, land_ref, send_sems, recv_sems, after_ref, x_dead, got_ref):
        me = _my_place()
        for k in range(1, N_DEV):
            frm = _peer(me, k)
            copy = pltpu.make_async_remote_copy(
                src_ref=x_ref.at[_index(frm)] if per_peer else x_ref, dst_ref=land_ref.at[_index(frm)],
                send_sem=send_sems.at[k - 1], recv_sem=recv_sems.at[k - 1], device_id=frm, device_id_type=MESH)
            copy.wait_send()
            copy.wait_recv()

    x_back, got = pl.pallas_call(
        body, name=name,
        out_shape=(pltpu.HBM(x_thru.shape, x_thru.dtype), pltpu.HBM(land_thru.shape, land_thru.dtype)),
        in_specs=(HBM_SPEC, HBM_SPEC, SEM_SPEC, SEM_SPEC, ANY_SPEC), out_specs=(HBM_SPEC, HBM_SPEC),
        input_output_aliases={0: 0, 1: 1},
        compiler_params=pltpu.CompilerParams(has_side_effects=_EFFECT),
    )(x_thru, land_thru, send_sems, recv_sems, after)
    me = _index(_my_place())
    own = lax.dynamic_index_in_dim(x_back, me, axis=0, keepdims=True) if per_peer else x_back[None]
    return lax.dynamic_update_slice_in_dim(got, own, me, axis=0)


def _all_reduce_packed(vec, after):
    R, L = vec.shape

    def body(x_ref, after_ref, out_ref, buf, send_sems, recv_sems):
        me = _my_place()
        buf[_index(me)] = x_ref[...]
        copies = []
        for k in range(1, N_DEV):
            to = _peer(me, k)
            cp = pltpu.make_async_remote_copy(
                src_ref=x_ref, dst_ref=buf.at[_index(me)],
                send_sem=send_sems.at[k - 1], recv_sem=recv_sems.at[k - 1], device_id=to, device_id_type=MESH)
            cp.start()
            copies.append(cp)
        for k in range(1, N_DEV):
            frm = _peer(me, k)
            pltpu.make_async_remote_copy(
                src_ref=x_ref, dst_ref=buf.at[_index(frm)],
                send_sem=send_sems.at[k - 1], recv_sem=recv_sems.at[k - 1], device_id=frm, device_id_type=MESH).wait_recv()
        for cp in copies:
            cp.wait_send()
        acc = buf[0]
        for d in range(1, N_DEV):
            acc = acc + buf[d]
        out_ref[...] = acc

    vm = pl.BlockSpec(memory_space=pltpu.VMEM)
    return pl.pallas_call(
        body, name="all_reduce_packed", in_specs=[vm, ANY_SPEC], out_specs=vm, out_shape=_sds((R, L), F32),
        scratch_shapes=[pltpu.VMEM((N_DEV, R, L), F32), pltpu.SemaphoreType.DMA((7,)), pltpu.SemaphoreType.DMA((7,))],
        compiler_params=pltpu.CompilerParams(vmem_limit_bytes=32 * 2**20),
    )(vec, after)


def _pack(arrays):
    rows = []
    for a in arrays:
        f = a.reshape(-1).astype(F32)
        pad = (-f.shape[0]) % LANES
        rows.append(jnp.pad(f, (0, pad)).reshape(-1, LANES))
    out = jnp.concatenate(rows, axis=0)
    return jnp.pad(out, ((0, (-out.shape[0]) % SUBLANES), (0, 0)))


def _unpack(packed, shapes):
    out, r = [], 0
    for s in shapes:
        n = math.prod(s)
        nr = -(-n // LANES)
        out.append(packed[r:r + nr].reshape(-1)[:n].reshape(s))
        r += nr
    return out


def _col_blocks(g):
    R, C = g.shape
    return jnp.transpose(g.astype(BF16).reshape(R, N_DEV, C // N_DEV), (1, 0, 2))


def _row_blocks(g):
    R, C = g.shape
    return g.astype(BF16).reshape(N_DEV, R // N_DEV, C)


def kernel(x, w_in, sb_out_gain, dn_conv_w, dn_a_log, dn_dt_bias, dn_out_gain, w_out, ln_mix_pre, ln_mix_post, w_up, ffn_conv_w, ffn_conv_b, w_down, ln_ffn_pre, ln_ffn_post, loss_target, m_w_in, m_sb_out_gain, m_dn_conv_w, m_dn_a_log, m_dn_dt_bias, m_dn_out_gain, m_w_out, m_ln_mix_pre, m_ln_mix_post, m_w_up, m_ffn_conv_w, m_ffn_conv_b, m_w_down, m_ln_ffn_pre, m_ln_ffn_post, v_w_in, v_sb_out_gain, v_dn_conv_w, v_dn_a_log, v_dn_dt_bias, v_dn_out_gain, v_w_out, v_ln_mix_pre, v_ln_mix_post, v_w_up, v_ffn_conv_w, v_ffn_conv_b, v_w_down, v_ln_ffn_pre, v_ln_ffn_post):
    T, D = x.shape[1], x.shape[2]
    SBW = SB_HEADS * HEAD_DIM
    DNW = DN_HEADS * HEAD_DIM
    in_cols = 3 * SBW + 4 * DNW + 2 * DN_HEADS
    main_cols = 3 * SBW + 4 * DNW
    in_pad = main_cols + LANES
    qkv0, z0 = 3 * SBW, 3 * SBW + 3 * DNW
    gate_block = main_cols // LANES
    x2, tgt = x[0], loss_target[0]

    g_in = _all_gather(w_in[0].astype(BF16), "gather_w_in")
    small_w = _all_gather(_pack([dn_conv_w[0], ffn_conv_w[0]]), "gather_conv_w")
    st_out, tok = _spread_start(w_out[0].astype(BF16), False, "gather_w_out_start", g_in)
    st_up, tok = _spread_start(w_up[0].astype(BF16), False, "gather_w_up_start", tok)
    st_down, tok_gather = _spread_start(w_down[0].astype(BF16), False, "gather_w_down_start", tok)
    w_in_f = jnp.transpose(g_in, (1, 0, 2)).reshape(D, in_cols)
    w_in_f = jnp.pad(w_in_f, ((0, 0), (0, in_pad - in_cols)))
    parts = [_unpack(small_w[d], [dn_conv_w.shape[1:], ffn_conv_w.shape[1:]]) for d in range(N_DEV)]
    dn_cw = jnp.concatenate([p[0] for p in parts], axis=1)
    ffn_cw = jnp.concatenate([p[1] for p in parts], axis=1)
    lane_pad = lambda a, off: jnp.pad(a, ((0, 0), (off, LANES - off - a.shape[1])))
    a_log_l, dt_bias_l = lane_pad(dn_a_log, DN_HEADS), lane_pad(dn_dt_bias, DN_HEADS)

    xn = _norm_in(x2, ln_mix_pre)
    proj = _matmul(xn, w_in_f, "nn", F32, "proj_in", tm_cap=512, tn_cap=2432, after=tok_gather)
    sb_qkv = proj[:, :3 * SBW].astype(BF16)
    o_sb, mix_sb, sb_att, sb_lb = _sb_attention(sb_qkv, sb_out_gain)
    qn = _dn_branch(proj, qkv0, dn_cw, 0, True, HEAD_DIM ** -0.5)
    kn = _dn_branch(proj, qkv0 + DNW, dn_cw, DNW, True, 1.0)
    vn = _dn_branch(proj, qkv0 + 2 * DNW, dn_cw, 2 * DNW, False, 1.0)
    gc_full, beta_full = _dn_gates(proj, gate_block, a_log_l, dt_bias_l)
    o_dn, mix_dn, tm_all, s_all = _dn_scan(qn, kn, vn, gc_full, beta_full, proj, z0, dn_out_gain)
    mix = jnp.concatenate([mix_sb, mix_dn], axis=1)
    w_out_f = _spread_wait(st_out, False, "gather_w_out_wait", mix).reshape(w_out.shape[1] * N_DEV, D)
    m = _matmul(mix, w_out_f, "nn", F32, "proj_out")
    h, hn = _mix_residual(x2, m, ln_mix_post, ln_ffn_pre)
    w_up_cut = _spread_wait(st_up, False, "gather_w_up_wait", hn)
    u = _matmul(hn, w_up_cut, "nn", F32, "ffn_up", tn_cap=w_up.shape[2], b_cut=True)
    act = _ffn_act(u, ffn_cw, ffn_conv_b)
    w_down_f = _spread_wait(st_down, False, "gather_w_down_wait", act).reshape(w_down.shape[1] * N_DEV, D)
    f = _matmul(act, w_down_f, "nn", F32, "ffn_down", tk_cap=2816)
    dy, df, d_ln_ffn_post, loss_part = _loss_head(h, f, ln_ffn_post, tgt)

    d_w_down = _matmul(act, df, "tn", BF16, "grad_w_down")
    st_xd, tok = _spread_start(_row_blocks(d_w_down), True, "exchange_w_down_start", loss_part)
    da = _matmul(df, w_down_f, "nt", F32, "bwd_ffn_down", after=tok)
    du, d_ffn_cwb = _ffn_act_bwd(u, ffn_cw, ffn_conv_b, da)
    d_ffn_cwb = jnp.concatenate([d_ffn_cwb[0], d_ffn_cwb[1]], axis=1)
    d_w_up_cut = _matmul(hn, du, "tn", BF16, "grad_w_up", b_cut=True, out_cut=True)
    st_xu, tok = _spread_start(d_w_up_cut, True, "exchange_w_up_start", d_ffn_cwb)
    dhn = _matmul(du, w_up_cut, "nt", F32, "bwd_ffn_up", after=tok, a_cut=True, b_cut=True)
    dh, dm, d_ln_ffn_pre, d_ln_mix_post = _ffn_residual_bwd(dy, dhn, h, ln_ffn_pre, m, ln_mix_post)

    d_w_out = _matmul(mix, dm, "tn", BF16, "grad_w_out")
    st_xo, tok = _spread_start(_row_blocks(d_w_out), True, "exchange_w_out_start", d_ln_ffn_pre)
    dmix = _matmul(dm, w_out_f, "nt", F32, "bwd_proj_out", after=tok)
    dq_sb, dk_sb, dv_sb, d_sb_gain = _sb_attention_bwd(sb_qkv, sb_out_gain, o_sb, sb_att, sb_lb, dmix)
    dqn, dkn, dvn, dgc_full, dbeta_full, dz, d_dn_gain = _dn_scan_bwd(
        qn, kn, vn, gc_full, beta_full, proj, z0, dn_out_gain, o_dn, tm_all, s_all, dmix, SBW)
    du_q, dcw_q = _dn_branch_bwd(proj, qkv0, dn_cw, 0, True, HEAD_DIM ** -0.5, dqn)
    du_k, dcw_k = _dn_branch_bwd(proj, qkv0 + DNW, dn_cw, DNW, True, 1.0, dkn)
    du_v, dcw_v = _dn_branch_bwd(proj, qkv0 + 2 * DNW, dn_cw, 2 * DNW, False, 1.0, dvn)
    dba, d_a_log_l, d_dt_bias_l = _dn_gates_bwd(proj, gate_block, a_log_l, dt_bias_l, dgc_full, dbeta_full)
    dproj = jnp.concatenate([dq_sb, dk_sb.astype(BF16), dv_sb.astype(BF16), du_q, du_k, du_v, dz, dba], axis=1)
    d_w_in = _matmul(xn, dproj, "tn", BF16, "grad_w_in", tm_cap=512, tn_cap=2432)[:, :in_cols]
    st_xi, tok = _spread_start(_col_blocks(d_w_in), True, "exchange_w_in_start", d_sb_gain)
    dxn = _matmul(dproj, w_in_f, "nt", F32, "bwd_proj_in", tk_cap=2432, after=tok)
    grad_x, d_ln_mix_pre = _input_bwd(dh, dxn, x2, ln_mix_pre)

    big = {}
    after = grad_x
    for n, st, w_, m_, v_ in [("w_down", st_xd, w_down, m_w_down, v_w_down), ("w_up", st_xu, w_up, m_w_up, v_w_up),
                              ("w_out", st_xo, w_out, m_w_out, v_w_out)]:
        got = _spread_wait(st, True, "exchange_" + n + "_wait", after)
        big[n] = _adamw_sharded(got, w_, m_, v_, "adamw_" + n)
        after = big[n][1]

    d_dn_cw = jnp.concatenate([dcw_q[:SHORT_CONV], dcw_k[:SHORT_CONV], dcw_v[:SHORT_CONV]], axis=1)
    small = [loss_part[:, :1], d_sb_gain, d_a_log_l[:, DN_HEADS:2 * DN_HEADS], d_dt_bias_l[:, DN_HEADS:2 * DN_HEADS], d_dn_gain,
             d_ln_mix_pre, d_ln_mix_post, d_ffn_cwb[FFN_CONV:FFN_CONV + 1], d_ln_ffn_pre, d_ln_ffn_post,
             d_dn_cw, d_ffn_cwb[:FFN_CONV]]
    shapes = [a.shape for a in small]
    red = _unpack(_all_reduce_packed(_pack(small), after), shapes)
    loss = red[0].reshape(())
    me = _index(_my_place())
    g_dn_cw = lax.dynamic_slice_in_dim(red[10], me * dn_conv_w.shape[2], dn_conv_w.shape[2], axis=1)
    g_ffn_cw = lax.dynamic_slice_in_dim(red[11], me * ffn_conv_w.shape[2], ffn_conv_w.shape[2], axis=1)
    names = ["sb_out_gain", "dn_conv_w", "dn_a_log", "dn_dt_bias", "dn_out_gain", "ln_mix_pre", "ln_mix_post",
             "ffn_conv_w", "ffn_conv_b", "ln_ffn_pre", "ln_ffn_post"]
    g_small = dict(sb_out_gain=red[1], dn_conv_w=g_dn_cw[None], dn_a_log=red[2], dn_dt_bias=red[3], dn_out_gain=red[4],
                   ln_mix_pre=red[5], ln_mix_post=red[6], ffn_conv_w=g_ffn_cw[None], ffn_conv_b=red[7],
                   ln_ffn_pre=red[8], ln_ffn_post=red[9])
    w_small = dict(sb_out_gain=sb_out_gain, dn_conv_w=dn_conv_w, dn_a_log=dn_a_log, dn_dt_bias=dn_dt_bias,
                   dn_out_gain=dn_out_gain, ln_mix_pre=ln_mix_pre, ln_mix_post=ln_mix_post, ffn_conv_w=ffn_conv_w,
                   ffn_conv_b=ffn_conv_b, ln_ffn_pre=ln_ffn_pre, ln_ffn_post=ln_ffn_post)
    m_small = dict(sb_out_gain=m_sb_out_gain, dn_conv_w=m_dn_conv_w, dn_a_log=m_dn_a_log, dn_dt_bias=m_dn_dt_bias,
                   dn_out_gain=m_dn_out_gain, ln_mix_pre=m_ln_mix_pre, ln_mix_post=m_ln_mix_post, ffn_conv_w=m_ffn_conv_w,
                   ffn_conv_b=m_ffn_conv_b, ln_ffn_pre=m_ln_ffn_pre, ln_ffn_post=m_ln_ffn_post)
    v_small = dict(sb_out_gain=v_sb_out_gain, dn_conv_w=v_dn_conv_w, dn_a_log=v_dn_a_log, dn_dt_bias=v_dn_dt_bias,
                   dn_out_gain=v_dn_out_gain, ln_mix_pre=v_ln_mix_pre, ln_mix_post=v_ln_mix_post, ffn_conv_w=v_ffn_conv_w,
                   ffn_conv_b=v_ffn_conv_b, ln_ffn_pre=v_ln_ffn_pre, ln_ffn_post=v_ln_ffn_post)
    sshapes = [w_small[n].shape for n in names]
    upd = _adamw_packed(_pack([g_small[n] for n in names]), _pack([w_small[n] for n in names]),
                        _pack([m_small[n] for n in names]), _pack([v_small[n] for n in names]))
    d_small, nm_small, nv_small = [dict(zip(names, _unpack(p, sshapes))) for p in upd]

    got = _spread_wait(st_xi, True, "exchange_w_in_wait", d_small["ln_ffn_post"])
    big["w_in"] = _adamw_sharded(got, w_in, m_w_in, v_w_in, "adamw_w_in")

    order = ["w_in", "sb_out_gain", "dn_conv_w", "dn_a_log", "dn_dt_bias", "dn_out_gain", "w_out", "ln_mix_pre",
             "ln_mix_post", "w_up", "ffn_conv_w", "ffn_conv_b", "w_down", "ln_ffn_pre", "ln_ffn_post"]
    pick = lambda n, i: big[n][i] if n in big else [g_small, d_small, nm_small, nv_small][i][n].reshape(w_small[n].shape)
    return (loss, grad_x[None], *[pick(n, 0) for n in order], *[pick(n, 1) for n in order],
            *[pick(n, 2) for n in order], *[pick(n, 3) for n in order])
```

```python
import functools
import math

import jax
import jax.numpy as jnp
from jax import lax
from jax.experimental import pallas as pl
from jax.experimental.pallas import tpu as pltpu

F32 = jnp.float32
BF16 = jnp.bfloat16

N_DEV = 8
HEAD_DIM = 128
SB_HEADS = 8
DN_HEADS = 8
DN_CHUNK = 128
DN_GROUP = 8
INV_BLOCK = 16
SB_KEYS = 256
SHORT_CONV = 4
FFN_CONV = 3
EPS = 1e-6
LANES = 128
SUBLANES = 8
VMEM_CAP = 56 * 2**20

ADAM_LR = 0.001
ADAM_B1 = 0.9
ADAM_B2 = 0.999
ADAM_EPS = 1e-08
ADAM_WD = 0.01
ADAM_STEP = 10

MESH = pl.DeviceIdType.MESH

assert HEAD_DIM == DN_CHUNK == LANES


def _tile(n, cap, mult):
    if n <= cap:
        return n
    t = (cap // mult) * mult
    while t >= mult:
        if n % t == 0:
            return t
        t -= mult
    raise ValueError(f"no tile for {n} under {cap} in multiples of {mult}")


def _params(sem, vmem_bytes):
    limit = int(min(VMEM_CAP, max(vmem_bytes, 16 * 2**20)))
    if not sem:
        return pltpu.CompilerParams(vmem_limit_bytes=limit)
    return pltpu.CompilerParams(dimension_semantics=sem, vmem_limit_bytes=limit)


def _nbytes(shape, dtype):
    return math.prod(shape) * jnp.dtype(dtype).itemsize


_NN = (((1,), (0,)), ((), ()))
_NT = (((1,), (1,)), ((), ()))
_TN = (((0,), (0,)), ((), ()))


def _batched(dims, ndim):
    if ndim == 2:
        return dims
    (ca,), (cb,) = dims[0]
    return (((ca + 1,), (cb + 1,)), ((0,), (0,)))


def _dot(a, b, dims=_NN):
    return lax.dot_general(a.astype(BF16), b.astype(BF16), _batched(dims, a.ndim), preferred_element_type=F32)


def _split2(x):
    hi = x.astype(BF16)
    lo = (x - hi.astype(F32)).astype(BF16)
    return hi, lo


def _split3(x):
    hi = x.astype(BF16)
    r = x - hi.astype(F32)
    mid = r.astype(BF16)
    lo = (r - mid.astype(F32)).astype(BF16)
    return hi, mid, lo


def _dot01(x, m01, passes=3):
    parts = _split3(x) if passes == 3 else _split2(x)
    out = None
    for p in parts:
        t = lax.dot_general(p, m01, _NN, preferred_element_type=F32)
        out = t if out is None else out + t
    return out


def _dot01_left(m01, x, passes=3):
    parts = _split3(x) if passes == 3 else _split2(x)
    out = None
    for p in parts:
        t = lax.dot_general(m01, p, _NN, preferred_element_type=F32)
        out = t if out is None else out + t
    return out


def _mm3(a, b, dims=_NN):
    ah, al = _split2(a)
    bh, bl = _split2(b)
    d = functools.partial(lax.dot_general, dimension_numbers=_batched(dims, a.ndim), preferred_element_type=F32)
    return d(ah, bh) + (d(ah, bl) + d(al, bh))


def _rowsum(x):
    return jnp.sum(x, axis=-1, keepdims=True)


def _t(x):
    return jnp.swapaxes(x, -1, -2)


def _sigmoid(x):
    return 1.0 / (1.0 + jnp.exp(-x))


def _softplus(x):
    return jnp.maximum(x, 0.0) + jnp.log(1.0 + jnp.exp(-jnp.abs(x)))


def _silu(x):
    return x * _sigmoid(x)


def _silu_grad(x):
    s = _sigmoid(x)
    return s * (1.0 + x * (1.0 - s))


_GELU_C = math.sqrt(2.0 / math.pi)


def _gelu(x):
    return 0.5 * x * (1.0 + jnp.tanh(_GELU_C * (x + 0.044715 * x * x * x)))


def _gelu_and_grad(x):
    x2 = x * x
    th = jnp.tanh(_GELU_C * (x + 0.044715 * x2 * x))
    half = 0.5 * (1.0 + th)
    return x * half, half + 0.5 * x * (1.0 - th * th) * (_GELU_C * (1.0 + 3.0 * 0.044715 * x2))


def _rms(x, g):
    r = lax.rsqrt(jnp.mean(x * x, axis=-1, keepdims=True) + EPS)
    return x * r * g


def _rms_bwd(dy, x, g):
    r = lax.rsqrt(jnp.mean(x * x, axis=-1, keepdims=True) + EPS)
    xh = x * r
    gdy = dy * g
    dx = r * (gdy - xh * jnp.mean(gdy * xh, axis=-1, keepdims=True))
    return dx, jnp.sum(dy * xh, axis=-2, keepdims=True)


def _iota2(shape, axis):
    return lax.broadcasted_iota(jnp.int32, shape, axis)


def _shift_down(cur, prev8, k):
    n = cur.shape[0]
    r = pltpu.roll(cur, k, 0)
    pr = pltpu.roll(prev8, k, 0)
    head = jnp.where(_iota2(pr.shape, 0) < k, pr, r[0:SUBLANES])
    if n == SUBLANES:
        return head
    return jnp.concatenate([head, r[SUBLANES:]], axis=0)


def _shift_up(cur, next8, k):
    n = cur.shape[0]
    r = pltpu.roll(cur, n - k, 0)
    nr = pltpu.roll(next8, SUBLANES - k, 0)
    tail = jnp.where(_iota2(nr.shape, 0) >= SUBLANES - k, nr, r[n - SUBLANES:])
    if n == SUBLANES:
        return tail
    return jnp.concatenate([r[:n - SUBLANES], tail], axis=0)


def _causal_conv(cur, prev8, w_ref, taps):
    out = cur * w_ref[taps - 1:taps, :]
    for j in range(taps - 1):
        out = out + _shift_down(cur, prev8, taps - 1 - j) * w_ref[j:j + 1, :]
    return out


def _anti_conv(cur, next8, w_ref, taps):
    out = cur * w_ref[taps - 1:taps, :]
    for j in range(taps - 1):
        out = out + _shift_up(cur, next8, taps - 1 - j) * w_ref[j:j + 1, :]
    return out


def _matmul(a, b, mode, out_dtype, name, tm_cap=1024, tn_cap=1024, tk_cap=2048, after=None,
            a_cut=False, b_cut=False, out_cut=False):
    a_shard = a.shape[2] if a_cut else None
    b_shard = b.shape[2] if b_cut else None
    a_full = (a.shape[1], a.shape[0] * a_shard) if a_cut else a.shape
    b_full = (b.shape[1], b.shape[0] * b_shard) if b_cut else b.shape
    assert not (a_cut and mode == "tn")
    if mode == "nn":
        (M, K), N = a_full, b_full[1]
    elif mode == "nt":
        (M, K), N = a_full, b_full[0]
    else:
        (K, M), N = a_full, b_full[1]
    n_unit = b_shard if (b_cut and mode != "nt") else N
    k_unit = math.gcd(a_shard or K, b_shard if (b_cut and mode == "nt") else K)
    tm = _tile(M, tm_cap, LANES)
    tn = N // N_DEV if out_cut else _tile(n_unit, tn_cap, LANES)
    tk = _tile(k_unit, tk_cap, LANES)
    assert n_unit % tn == 0 and k_unit % tk == 0
    nk = K // tk
    dims = {"nn": _NN, "nt": _NT, "tn": _TN}[mode]
    if a_cut:
        pa = a_shard // tk
        a_spec = pl.BlockSpec((None, tm, tk), lambda i, j, k: (k // pa, i, k % pa))
    elif mode == "tn":
        a_spec = pl.BlockSpec((tk, tm), lambda i, j, k: (k, i))
    else:
        a_spec = pl.BlockSpec((tm, tk), lambda i, j, k: (i, k))
    if b_cut and mode == "nt":
        pb = b_shard // tk
        b_spec = pl.BlockSpec((None, tn, tk), lambda i, j, k: (k // pb, j, k % pb))
    elif b_cut:
        pb = b_shard // tn
        b_spec = pl.BlockSpec((None, tk, tn), lambda i, j, k: (j // pb, k, j % pb))
    elif mode == "nt":
        b_spec = pl.BlockSpec((tn, tk), lambda i, j, k: (j, k))
    else:
        b_spec = pl.BlockSpec((tk, tn), lambda i, j, k: (k, j))
    if out_cut:
        out_spec, out_shape = pl.BlockSpec((None, tm, tn), lambda i, j, k: (j, i, 0)), (N_DEV, M, tn)
    else:
        out_spec, out_shape = pl.BlockSpec((tm, tn), lambda i, j, k: (i, j)), (M, N)

    def body(a_ref, b_ref, *rest):
        if nk == 1:
            rest[-1][...] = lax.dot_general(a_ref[...], b_ref[...], dims, preferred_element_type=F32).astype(rest[-1].dtype)
            return
        o_ref, acc_ref = rest[-2:]
        k = pl.program_id(2)

        @pl.when(k == 0)
        def _():
            acc_ref[...] = jnp.zeros_like(acc_ref)

        acc_ref[...] += lax.dot_general(a_ref[...], b_ref[...], dims, preferred_element_type=F32)

        @pl.when(k == nk - 1)
        def _():
            o_ref[...] = acc_ref[...].astype(o_ref.dtype)

    vmem = 2 * (_nbytes((tm, tk), a.dtype) + _nbytes((tk, tn), b.dtype) + _nbytes((tm, tn), out_dtype)) + _nbytes((tm, tn), F32)
    vmem += _nbytes((tm, tn), F32) + (2 * _nbytes((tm, tk), a.dtype) if mode == "tn" else 0)
    tokens = [] if after is None else [after]
    return pl.pallas_call(
        body, name=name, grid=(M // tm, N // tn, nk),
        in_specs=[a_spec, b_spec] + [pl.BlockSpec(t.shape, lambda i, j, k: (0, 0)) for t in tokens],
        out_specs=out_spec,
        out_shape=jax.ShapeDtypeStruct(out_shape, out_dtype),
        scratch_shapes=[] if nk == 1 else [pltpu.VMEM((tm, tn), F32)],
        compiler_params=_params(("parallel", "parallel", "arbitrary"), vmem + 4 * 2**20),
    )(a, b, *tokens)


def _row_call(body, name, T, D, ins, outs, tr, acc_outs=()):
    def spec(a, kind):
        if kind == "row":
            return pl.BlockSpec((tr, a.shape[1]), lambda i: (i, 0))
        return pl.BlockSpec(a.shape, lambda i: (0, 0))
    in_specs = [spec(a, k) for a, k in ins]
    out_specs = [spec(a, k) for a, k in outs] + [spec(a, "vec") for a in acc_outs]
    out_shape = [a for a, _ in outs] + list(acc_outs)
    vmem = 2 * sum(_nbytes((tr, a.shape[1]) if k == "row" else a.shape, a.dtype) for a, k in list(ins) + list(outs))
    return pl.pallas_call(
        body, name=name, grid=(T // tr,), in_specs=in_specs, out_specs=out_specs, out_shape=out_shape,
        compiler_params=_params(("arbitrary",), 3 * vmem + 8 * 2**20),
    )(*[a for a, _ in ins])


def _sds(shape, dtype):
    return jax.ShapeDtypeStruct(shape, dtype)


def _accumulate(ref, val):
    @pl.when(pl.program_id(0) == 0)
    def _():
        ref[...] = jnp.zeros_like(ref)
    ref[...] += val


def _norm_in(x, g):
    T, D = x.shape

    def body(x_ref, g_ref, o_ref):
        o_ref[...] = _rms(x_ref[...], g_ref[...]).astype(BF16)

    return _row_call(body, "norm_in", T, D, [(x, "row"), (g, "vec")], [(_sds((T, D), BF16), "row")], _tile(T, 256, 16))[0]


def _mix_residual(x, m, g_post, g_pre):
    T, D = x.shape

    def body(x_ref, m_ref, gp_ref, gn_ref, h_ref, hn_ref):
        h = x_ref[...] + _rms(m_ref[...], gp_ref[...])
        h_ref[...] = h
        hn_ref[...] = _rms(h, gn_ref[...]).astype(BF16)

    return _row_call(body, "mix_residual", T, D, [(x, "row"), (m, "row"), (g_post, "vec"), (g_pre, "vec")],
                     [(_sds((T, D), F32), "row"), (_sds((T, D), BF16), "row")], _tile(T, 256, 16))


def _loss_head(h, f, g_post, target):
    T, D = h.shape

    def body(h_ref, f_ref, g_ref, t_ref, dy_ref, df_ref, dg_ref, loss_ref):
        f = f_ref[...]
        g = g_ref[...]
        diff = h_ref[...] + _rms(f, g) - t_ref[...]
        dy = diff * (1.0 / D)
        dy_ref[...] = dy
        df, dg = _rms_bwd(dy, f, g)
        df_ref[...] = df.astype(BF16)
        _accumulate(dg_ref, dg)
        _accumulate(loss_ref, jnp.full((1, LANES), 0.5 / D, F32) * jnp.sum(diff * diff))

    return _row_call(body, "loss_head", T, D, [(h, "row"), (f, "row"), (g_post, "vec"), (target, "row")],
                     [(_sds((T, D), F32), "row"), (_sds((T, D), BF16), "row")], _tile(T, 256, 16),
                     acc_outs=[_sds((1, D), F32), _sds((1, LANES), F32)])


def _ffn_residual_bwd(dy, dhn, h, g_pre, m, g_post):
    T, D = h.shape

    def body(dy_ref, dhn_ref, h_ref, gn_ref, m_ref, gp_ref, dh_ref, dm_ref, dgn_ref, dgp_ref):
        dhh, dgn = _rms_bwd(dhn_ref[...], h_ref[...], gn_ref[...])
        dh = dy_ref[...] + dhh
        dh_ref[...] = dh
        dm, dgp = _rms_bwd(dh, m_ref[...], gp_ref[...])
        dm_ref[...] = dm.astype(BF16)
        _accumulate(dgn_ref, dgn)
        _accumulate(dgp_ref, dgp)

    return _row_call(body, "ffn_residual_bwd", T, D,
                     [(dy, "row"), (dhn, "row"), (h, "row"), (g_pre, "vec"), (m, "row"), (g_post, "vec")],
                     [(_sds((T, D), F32), "row"), (_sds((T, D), BF16), "row")], _tile(T, 128, 16),
                     acc_outs=[_sds((1, D), F32), _sds((1, D), F32)])


def _input_bwd(dh, dxn, x, g):
    T, D = x.shape

    def body(dh_ref, dxn_ref, x_ref, g_ref, dx_ref, dg_ref):
        dx, dg = _rms_bwd(dxn_ref[...], x_ref[...], g_ref[...])
        dx_ref[...] = dh_ref[...] + dx
        _accumulate(dg_ref, dg)

    return _row_call(body, "input_bwd", T, D, [(dh, "row"), (dxn, "row"), (x, "row"), (g, "vec")],
                     [(_sds((T, D), F32), "row")], _tile(T, 256, 16), acc_outs=[_sds((1, D), F32)])


def _ffn_act(u, conv_w, conv_b):
    T, F2 = u.shape
    F = F2 // 2
    tc = _tile(F, 512, LANES)
    tr = _tile(T, 512, SUBLANES)
    nc = F // tc
    r8 = tr // SUBLANES

    def body(ug_ref, ugp_ref, uv_ref, uvp_ref, wg_ref, wv_ref, bg_ref, bv_ref, a_ref):
        first = pl.program_id(1) == 0
        cg = _causal_conv(ug_ref[...], jnp.where(first, 0.0, ugp_ref[...]), wg_ref, FFN_CONV) + bg_ref[...]
        cv = _causal_conv(uv_ref[...], jnp.where(first, 0.0, uvp_ref[...]), wv_ref, FFN_CONV) + bv_ref[...]
        a_ref[...] = (_gelu(cg) * cv).astype(BF16)

    cur = lambda off: pl.BlockSpec((tr, tc), lambda j, i: (i, j + off))
    prev = lambda off: pl.BlockSpec((SUBLANES, tc), lambda j, i: (jnp.maximum(i * r8 - 1, 0), j + off))
    wsp = lambda off: pl.BlockSpec((FFN_CONV, tc), lambda j, i: (0, j + off))
    bsp = lambda off: pl.BlockSpec((1, tc), lambda j, i: (0, j + off))
    return pl.pallas_call(
        body, name="ffn_act", grid=(nc, T // tr),
        in_specs=[cur(0), prev(0), cur(nc), prev(nc), wsp(0), wsp(nc), bsp(0), bsp(nc)],
        out_specs=pl.BlockSpec((tr, tc), lambda j, i: (i, j)),
        out_shape=_sds((T, F), BF16),
        compiler_params=_params(("parallel", "arbitrary"), 12 * _nbytes((tr, tc), F32) + 8 * 2**20),
    )(u, u, u, u, conv_w, conv_w, conv_b, conv_b)


def _ffn_act_bwd(u, conv_w, conv_b, da):
    T, F2 = u.shape
    F = F2 // 2
    tc = _tile(F, 512, LANES)
    tr = _tile(T, 512, SUBLANES)
    nc = F // tc
    r8 = tr // SUBLANES
    n8 = T // SUBLANES
    K = FFN_CONV

    def body(ug_ref, ugp_ref, ugn_ref, uv_ref, uvp_ref, uvn_ref, da_ref, dan_ref,
             wg_ref, wv_ref, bg_ref, bv_ref, du_ref, dwb_ref):
        i = pl.program_id(1)
        first = i == 0
        last = i == pl.num_programs(1) - 1

        def dconv(ug, ug_prev, uv, uv_prev, da_):
            cg = _causal_conv(ug, ug_prev, wg_ref, K) + bg_ref[...]
            cv = _causal_conv(uv, uv_prev, wv_ref, K) + bv_ref[...]
            act, act_grad = _gelu_and_grad(cg)
            return da_ * cv * act_grad, da_ * act

        ug, uv = ug_ref[...], uv_ref[...]
        ug_prev, uv_prev = jnp.where(first, 0.0, ugp_ref[...]), jnp.where(first, 0.0, uvp_ref[...])
        dcg, dcv = dconv(ug, ug_prev, uv, uv_prev, da_ref[...])
        dcgn, dcvn = dconv(ugn_ref[...], ug[tr - SUBLANES:], uvn_ref[...], uv[tr - SUBLANES:], dan_ref[...])
        du_ref[0] = _anti_conv(dcg, jnp.where(last, 0.0, dcgn), wg_ref, K).astype(BF16)
        du_ref[1] = _anti_conv(dcv, jnp.where(last, 0.0, dcvn), wv_ref, K).astype(BF16)

        @pl.when(first)
        def _():
            dwb_ref[...] = jnp.zeros_like(dwb_ref)

        for half, (dc, uo, uo_prev) in enumerate([(dcg, ug, ug_prev), (dcv, uv, uv_prev)]):
            rows = [jnp.sum(dc * _shift_down(uo, uo_prev, K - 1 - t), axis=0, keepdims=True) for t in range(K - 1)]
            rows += [jnp.sum(dc * uo, axis=0, keepdims=True), jnp.sum(dc, axis=0, keepdims=True)]
            rows += [jnp.zeros_like(rows[0])] * (SUBLANES - len(rows))
            dwb_ref[half] += jnp.concatenate(rows, axis=0)

    cur = lambda off: pl.BlockSpec((tr, tc), lambda j, i: (i, j + off))
    prev = lambda off: pl.BlockSpec((SUBLANES, tc), lambda j, i: (jnp.maximum(i * r8 - 1, 0), j + off))
    nxt = lambda off: pl.BlockSpec((SUBLANES, tc), lambda j, i: (jnp.minimum((i + 1) * r8, n8 - 1), j + off))
    wsp = lambda off: pl.BlockSpec((K, tc), lambda j, i: (0, j + off))
    bsp = lambda off: pl.BlockSpec((1, tc), lambda j, i: (0, j + off))
    return pl.pallas_call(
        body, name="ffn_act_bwd", grid=(nc, T // tr),
        in_specs=[cur(0), prev(0), nxt(0), cur(nc), prev(nc), nxt(nc), cur(0), nxt(0), wsp(0), wsp(nc), bsp(0), bsp(nc)],
        out_specs=[pl.BlockSpec((2, tr, tc), lambda j, i: (0, i, j)), pl.BlockSpec((2, SUBLANES, tc), lambda j, i: (0, 0, j))],
        out_shape=[_sds((2, T, F), BF16), _sds((2, SUBLANES, F), F32)],
        compiler_params=_params(("parallel", "arbitrary"), 24 * _nbytes((tr, tc), F32) + 8 * 2**20),
    )(u, u, u, u, u, u, da, da, conv_w, conv_w, conv_b, conv_b)


def _l2norm(s, scale):
    return s * (lax.rsqrt(jnp.sum(s * s, axis=-1, keepdims=True) + EPS) * scale)


def _dn_branch(proj, col0, conv_w, wcol0, l2, scale):
    T = proj.shape[0]
    W = DN_HEADS * HEAD_DIM
    tr = _tile(T, 2048, SUBLANES)
    r8 = tr // SUBLANES
    cb0, wb0 = col0 // HEAD_DIM, wcol0 // HEAD_DIM

    def body(u_ref, up_ref, w_ref, o_ref):
        first = pl.program_id(1) == 0
        s = _silu(_causal_conv(u_ref[...], jnp.where(first, 0.0, up_ref[...]), w_ref, SHORT_CONV))
        o_ref[...] = _l2norm(s, scale) if l2 else s

    return pl.pallas_call(
        body, name=f"dn_branch_{col0}", grid=(DN_HEADS, T // tr),
        in_specs=[pl.BlockSpec((tr, HEAD_DIM), lambda h, i: (i, cb0 + h)),
                  pl.BlockSpec((SUBLANES, HEAD_DIM), lambda h, i: (jnp.maximum(i * r8 - 1, 0), cb0 + h)),
                  pl.BlockSpec((SHORT_CONV, HEAD_DIM), lambda h, i: (0, wb0 + h))],
        out_specs=pl.BlockSpec((tr, HEAD_DIM), lambda h, i: (i, h)),
        out_shape=_sds((T, W), F32),
        compiler_params=_params(("parallel", "arbitrary"), 32 * _nbytes((tr, HEAD_DIM), F32) + 8 * 2**20),
    )(proj, proj, conv_w)


def _dn_branch_bwd(proj, col0, conv_w, wcol0, l2, scale, dy):
    T = proj.shape[0]
    W = DN_HEADS * HEAD_DIM
    tr = _tile(T, 2048, SUBLANES)
    r8 = tr // SUBLANES
    n8 = T // SUBLANES
    cb0, wb0 = col0 // HEAD_DIM, wcol0 // HEAD_DIM
    K = SHORT_CONV

    def body(u_ref, up_ref, un_ref, dy_ref, dyn_ref, w_ref, du_ref, dw_ref):
        i = pl.program_id(1)
        first = i == 0
        last = i == pl.num_programs(1) - 1

        def dconv(u, u_prev, dy_):
            c = _causal_conv(u, u_prev, w_ref, K)
            if l2:
                s = _silu(c)
                r = lax.rsqrt(jnp.sum(s * s, axis=-1, keepdims=True) + EPS)
                n = s * r
                ds = (scale * r) * (dy_ - n * jnp.sum(dy_ * n, axis=-1, keepdims=True))
            else:
                ds = dy_
            return ds * _silu_grad(c)

        u = u_ref[...]
        u_prev = jnp.where(first, 0.0, up_ref[...])
        dc = dconv(u, u_prev, dy_ref[...])
        dcn = jnp.where(last, 0.0, dconv(un_ref[...], u[tr - SUBLANES:], dyn_ref[...]))
        du_ref[...] = _anti_conv(dc, dcn, w_ref, K).astype(BF16)
        rows = [jnp.sum(dc * _shift_down(u, u_prev, K - 1 - t), axis=0, keepdims=True) for t in range(K - 1)]
        rows += [jnp.sum(dc * u, axis=0, keepdims=True)]
        rows += [jnp.zeros_like(rows[0])] * (SUBLANES - len(rows))
        upd = jnp.concatenate(rows, axis=0)

        @pl.when(first)
        def _():
            dw_ref[...] = jnp.zeros_like(dw_ref)
        dw_ref[...] += upd

    return pl.pallas_call(
        body, name=f"dn_branch_bwd_{col0}", grid=(DN_HEADS, T // tr),
        in_specs=[pl.BlockSpec((tr, HEAD_DIM), lambda h, i: (i, cb0 + h)),
                  pl.BlockSpec((SUBLANES, HEAD_DIM), lambda h, i: (jnp.maximum(i * r8 - 1, 0), cb0 + h)),
                  pl.BlockSpec((SUBLANES, HEAD_DIM), lambda h, i: (jnp.minimum((i + 1) * r8, n8 - 1), cb0 + h)),
                  pl.BlockSpec((tr, HEAD_DIM), lambda h, i: (i, h)),
                  pl.BlockSpec((SUBLANES, HEAD_DIM), lambda h, i: (jnp.minimum((i + 1) * r8, n8 - 1), h)),
                  pl.BlockSpec((K, HEAD_DIM), lambda h, i: (0, wb0 + h))],
        out_specs=[pl.BlockSpec((tr, HEAD_DIM), lambda h, i: (i, h)),
                   pl.BlockSpec((SUBLANES, HEAD_DIM), lambda h, i: (0, h))],
        out_shape=[_sds((T, W), BF16), _sds((SUBLANES, W), F32)],
        compiler_params=_params(("parallel", "arbitrary"), 32 * _nbytes((tr, HEAD_DIM), F32) + 8 * 2**20),
    )(proj, proj, proj, dy, dy, conv_w)


def _lane_masks(shape):
    lane = _iota2(shape, 1)
    return lane < DN_HEADS, (lane >= DN_HEADS) & (lane < 2 * DN_HEADS)


def _expand01(off):
    r = _iota2((LANES, DN_HEADS * HEAD_DIM), 0)
    c = _iota2((LANES, DN_HEADS * HEAD_DIM), 1)
    return (r == jnp.right_shift(c, int(math.log2(HEAD_DIM))) + off).astype(BF16)


def _select01(off):
    r = _iota2((DN_HEADS * HEAD_DIM, LANES), 0)
    c = _iota2((DN_HEADS * HEAD_DIM, LANES), 1)
    return (r == (c - off) * HEAD_DIM).astype(BF16)


def _dn_gates(proj, gate_block, a_log_l, dt_bias_l):
    T = proj.shape[0]
    C = DN_CHUNK
    W = DN_HEADS * HEAD_DIM

    def body(ba_ref, al_ref, dt_ref, gc_ref, beta_ref):
        ba = ba_ref[...]
        is_b, is_a = _lane_masks(ba.shape)
        g = jnp.where(is_a, -jnp.exp(al_ref[...]) * _softplus(ba + dt_ref[...]), 0.0)
        beta = jnp.where(is_b, _sigmoid(ba), 0.0)
        tri = (_iota2((C, C), 0) >= _iota2((C, C), 1)).astype(BF16)
        gc = _dot01_left(tri, g)
        gc_ref[...] = _dot01(gc, _expand01(DN_HEADS))
        beta_ref[...] = _dot01(beta, _expand01(0))

    vec = pl.BlockSpec((1, LANES), lambda n: (0, 0))
    return pl.pallas_call(
        body, name="dn_gates", grid=(T // C,),
        in_specs=[pl.BlockSpec((C, LANES), lambda n: (n, gate_block)), vec, vec],
        out_specs=[pl.BlockSpec((C, W), lambda n: (n, 0))] * 2,
        out_shape=[_sds((T, W), F32)] * 2,
        compiler_params=_params(("parallel",), 16 * 2**20),
    )(proj, a_log_l, dt_bias_l)


def _dn_gates_bwd(proj, gate_block, a_log_l, dt_bias_l, dgc_full, dbeta_full):
    T = proj.shape[0]
    C = DN_CHUNK
    W = DN_HEADS * HEAD_DIM

    def body(ba_ref, al_ref, dt_ref, dgc_ref, dbeta_ref, dba_ref, dal_ref, ddt_ref):
        ba = ba_ref[...]
        is_b, is_a = _lane_masks(ba.shape)
        ea = jnp.exp(al_ref[...])
        pre = ba + dt_ref[...]
        g = jnp.where(is_a, -ea * _softplus(pre), 0.0)
        beta = _sigmoid(ba)
        dgc = _dot01(dgc_ref[...], _select01(DN_HEADS))
        dbeta = _dot01(dbeta_ref[...], _select01(0))
        triu = (_iota2((C, C), 0) <= _iota2((C, C), 1)).astype(BF16)
        dg = _dot01_left(triu, dgc)
        da = jnp.where(is_a, dg * (-ea) * _sigmoid(pre), 0.0)
        dba_ref[...] = (da + jnp.where(is_b, dbeta * beta * (1.0 - beta), 0.0)).astype(BF16)
        _accumulate(dal_ref, jnp.sum(dg * g, axis=0, keepdims=True))
        _accumulate(ddt_ref, jnp.sum(da, axis=0, keepdims=True))

    vec = pl.BlockSpec((1, LANES), lambda n: (0, 0))
    full = pl.BlockSpec((C, W), lambda n: (n, 0))
    return pl.pallas_call(
        body, name="dn_gates_bwd", grid=(T // C,),
        in_specs=[pl.BlockSpec((C, LANES), lambda n: (n, gate_block)), vec, vec, full, full],
        out_specs=[pl.BlockSpec((C, LANES), lambda n: (n, 0)), vec, vec],
        out_shape=[_sds((T, LANES), BF16), _sds((1, LANES), F32), _sds((1, LANES), F32)],
        compiler_params=_params(("arbitrary",), 16 * 2**20),
    )(proj, a_log_l, dt_bias_l, dgc_full, dbeta_full)


def _unit_lower_inverse(L):
    C = L.shape[-1]
    row, col = _iota2((C, C), 0), _iota2((C, C), 1)
    eye = (row == col).astype(F32)
    sh = int(math.log2(INV_BLOCK))
    Ld = jnp.where(jnp.right_shift(row, sh) == jnp.right_shift(col, sh), L, 0.0)
    Lo = L - Ld
    X = eye - Ld
    P = Ld
    for _ in range(int(math.log2(INV_BLOCK)) - 1):
        P = _mm3(P, P)
        X = X + _mm3(X, P)
    N = _mm3(X, Lo)
    Y = eye - N
    P = N
    for _ in range(int(math.log2(C // INV_BLOCK)) - 1):
        P = _mm3(P, P)
        Y = Y + _mm3(Y, P)
    return _mm3(Y, X)


def _dn_chunk_common(q, k, v, gc, beta, gl):
    C = q.shape[-2]
    row, col = _iota2((C, C), 0), _iota2((C, C), 1)
    causal, strict = row >= col, row > col
    eg = jnp.exp(gc)
    decay = jnp.where(causal, jnp.exp(jnp.where(causal, gc - _t(gc), 0.0)), 0.0)
    kb, vb = k * beta, v * beta
    L = jnp.where(strict, _dot(kb, k, _NT) * decay, 0.0)
    Aqk = jnp.where(causal, _dot(q, k, _NT) * decay, 0.0)
    ektg = jnp.exp(gl - gc)
    return dict(causal=causal, strict=strict, eg=eg, decay=decay, kb=kb, vb=vb, L=L, Aqk=Aqk, ektg=ektg,
                kbg=kb * eg, kte=k * ektg, qd=q * eg, egl=jnp.exp(gl))


def _dn_scan(qn, kn, vn, gc_full, beta_full, proj, z_col0, gain):
    T, W = qn.shape
    C = DN_CHUNK
    N = T // C
    H = DN_HEADS
    G = DN_GROUP
    GW = G * HEAD_DIM
    zb0 = z_col0 // GW

    def body(q_ref, k_ref, v_ref, gc_ref, beta_ref, z_ref, gain_ref, o_ref, mix_ref, tm_ref, s_ref, S):
        @pl.when(pl.program_id(1) == 0)
        def _():
            S[...] = jnp.zeros_like(S)

        heads = lambda ref, rows=slice(None): jnp.stack([ref[rows, g * HEAD_DIM:(g + 1) * HEAD_DIM] for g in range(G)])
        q, k, v, gc, beta = heads(q_ref), heads(k_ref), heads(v_ref), heads(gc_ref), heads(beta_ref)
        gl = heads(gc_ref, slice(C - 1, C))
        c = _dn_chunk_common(q, k, v, gc, beta, gl)
        Tm = _unit_lower_inverse(c["L"])
        u = _dot(Tm, c["vb"])
        w = _dot(Tm, c["kbg"])
        S0 = S[...]
        vnew = u - _dot(w, S0)
        o = _dot(c["qd"], S0) + _dot(c["Aqk"], vnew)
        S[...] = S0 * c["egl"] + _dot(c["kte"], vnew, _TN)
        tm_ref[...] = Tm
        s_ref[...] = S0
        mix = (_rms(o, gain_ref[...]) * _silu(heads(z_ref))).astype(BF16)
        for g in range(G):
            sl = slice(g * HEAD_DIM, (g + 1) * HEAD_DIM)
            o_ref[:, sl] = o[g]
            mix_ref[:, sl] = mix[g]

    blk = pl.BlockSpec((C, GW), lambda h, n: (n, h))
    mat = pl.BlockSpec((G, None, C, C), lambda h, n: (h, n, 0, 0))
    return pl.pallas_call(
        body, name="dn_scan", grid=(H // G, N),
        in_specs=[blk, blk, blk, blk, blk, pl.BlockSpec((C, GW), lambda h, n: (n, zb0 + h)),
                  pl.BlockSpec((1, HEAD_DIM), lambda h, n: (0, 0))],
        out_specs=[blk, blk, mat, mat],
        out_shape=[_sds((T, W), F32), _sds((T, W), BF16), _sds((H, N, C, C), F32), _sds((H, N, C, C), F32)],
        scratch_shapes=[pltpu.VMEM((G, HEAD_DIM, HEAD_DIM), F32)],
        compiler_params=_params(("parallel", "arbitrary"), 32 * 2**20),
    )(qn, kn, vn, gc_full, beta_full, proj, gain)


def _dn_scan_bwd(qn, kn, vn, gc_full, beta_full, proj, z_col0, gain, o_raw, tm_all, s_all, dmix, dmix_col0):
    T, W = qn.shape
    C = DN_CHUNK
    N = T // C
    H = DN_HEADS
    G = DN_GROUP
    GW = G * HEAD_DIM
    zb0 = z_col0 // GW
    mb0 = dmix_col0 // GW

    def body(q_ref, k_ref, v_ref, gc_ref, beta_ref, z_ref, gain_ref, o_ref, tm_ref, s_ref, dmix_ref,
             dq_ref, dk_ref, dv_ref, dgc_ref, dbeta_ref, dz_ref, dgain_ref, dS):
        @pl.when(pl.program_id(1) == 0)
        def _():
            dS[...] = jnp.zeros_like(dS)

        @pl.when((pl.program_id(0) == 0) & (pl.program_id(1) == 0))
        def _():
            dgain_ref[...] = jnp.zeros_like(dgain_ref)

        heads = lambda ref, rows=slice(None): jnp.stack([ref[rows, g * HEAD_DIM:(g + 1) * HEAD_DIM] for g in range(G)])
        total = lambda x: jnp.sum(jnp.sum(x, axis=-1, keepdims=True), axis=-2, keepdims=True)
        gain = gain_ref[...]
        o, z, dmix = heads(o_ref), heads(z_ref), heads(dmix_ref)
        dz = (dmix * _rms(o, gain) * _silu_grad(z)).astype(BF16)
        do, dgain = _rms_bwd(dmix * _silu(z), o, gain)
        dgain_ref[...] += jnp.sum(dgain, axis=0)

        q, k, v, gc, beta = heads(q_ref), heads(k_ref), heads(v_ref), heads(gc_ref), heads(beta_ref)
        gl = heads(gc_ref, slice(C - 1, C))
        c = _dn_chunk_common(q, k, v, gc, beta, gl)
        Tm, S0, dS1 = tm_ref[...], s_ref[...], dS[...]
        w = _dot(Tm, c["kbg"])
        vnew = _dot(Tm, c["vb"]) - _dot(w, S0)

        dvnew = _dot(c["Aqk"], do, _TN) + _dot(c["kte"], dS1)
        dAqk = jnp.where(c["causal"], _dot(do, vnew, _NT), 0.0)
        dqd = _dot(do, S0, _NT)
        dkte = _dot(vnew, dS1, _NT)
        dgl = total(dS1 * S0) * c["egl"]
        dw = -_dot(dvnew, S0, _NT)
        dS[...] = dS1 * c["egl"] + _dot(c["qd"], do, _TN) - _dot(w, dvnew, _TN)

        dTm = _dot(dvnew, c["vb"], _NT) + _dot(dw, c["kbg"], _NT)
        dvb = _dot(Tm, dvnew, _TN)
        dkbg = _dot(Tm, dw, _TN)
        dL = jnp.where(c["strict"], -_mm3(_mm3(Tm, dTm, _TN), Tm, _NT), 0.0)
        dP = dL * c["decay"]
        dQ = dAqk * c["decay"]
        M = dL * c["L"] + dAqk * c["Aqk"]
        dkb = _dot(dP, k) + dkbg * c["eg"]
        dk = _dot(dP, c["kb"], _TN) + _dot(dQ, q, _TN) + dkte * c["ektg"] + dkb * beta
        dq = _dot(dQ, k) + dqd * c["eg"]
        tk = _rowsum(dkte * c["kte"])
        dgc = (_rowsum(M) - _rowsum(_t(M)) + _rowsum(dqd * c["qd"]) - tk + _rowsum(dkbg * c["kbg"]))
        dgl = dgl + total(tk)
        dgc = jnp.broadcast_to(dgc, q.shape) + jnp.where(_iota2((C, HEAD_DIM), 0) == C - 1, dgl, 0.0)
        dv = dvb * beta
        dbeta = jnp.broadcast_to(_rowsum(dkb * k) + _rowsum(dvb * v), q.shape)
        for g in range(G):
            sl = slice(g * HEAD_DIM, (g + 1) * HEAD_DIM)
            dz_ref[:, sl] = dz[g]
            dq_ref[:, sl] = dq[g]
            dk_ref[:, sl] = dk[g]
            dv_ref[:, sl] = dv[g]
            dgc_ref[:, sl] = dgc[g]
            dbeta_ref[:, sl] = dbeta[g]

    rev = lambda off: pl.BlockSpec((C, GW), lambda h, n: (N - 1 - n, off + h))
    mat = pl.BlockSpec((G, None, C, C), lambda h, n: (h, N - 1 - n, 0, 0))
    vec = pl.BlockSpec((1, HEAD_DIM), lambda h, n: (0, 0))
    return pl.pallas_call(
        body, name="dn_scan_bwd", grid=(H // G, N),
        in_specs=[rev(0), rev(0), rev(0), rev(0), rev(0), rev(zb0), vec, rev(0), mat, mat, rev(mb0)],
        out_specs=[rev(0)] * 6 + [vec],
        out_shape=[_sds((T, W), F32)] * 5 + [_sds((T, W), BF16), _sds((1, HEAD_DIM), F32)],
        scratch_shapes=[pltpu.VMEM((G, HEAD_DIM, HEAD_DIM), F32)],
        compiler_params=_params(("arbitrary", "arbitrary"), 40 * 2**20),
    )(qn, kn, vn, gc_full, beta_full, proj, gain, o_raw, tm_all, s_all, dmix)


def _sb_terms(z, ahead, first_key):
    lb = jnp.minimum(z, 0.0) - jnp.log(1.0 + jnp.exp(-jnp.abs(z)))
    if ahead is None:
        return None, lb, lb - z
    valid = ahead < -first_key
    return valid, lb, jnp.where(valid, lb - z, 0.0)


def _masked(valid, x):
    return x if valid is None else jnp.where(valid, x, 0.0)


def _sb_attention(qkv, gain, tq_cap=1024):
    T = qkv.shape[0]
    H = SB_HEADS
    B = min(SB_KEYS, T)
    TQ = _tile(T, tq_cap, B)
    per = TQ // B

    assert per >= 2
    n_saved = per * (T // TQ) * (T // TQ + 1) // 2

    def body(q_ref, k_ref, v_ref, gain_ref, o_ref, mix_ref, att_hbm, lb_hbm, att_buf, lb_buf, sems):
        h, i = pl.program_id(0), pl.program_id(1)
        q = q_ref[...]
        upper = (_iota2((B, B), 0) > _iota2((B, B), 1)).astype(BF16)
        ahead = _iota2((TQ, B), 1) - _iota2((TQ, B), 0)
        last = (i + 1) * per - 1
        base = per * (i * (i + 1) // 2)

        def save(slot, pair):
            return (pltpu.make_async_copy(att_buf.at[slot], att_hbm.at[h, pair], sems.at[0, slot]),
                    pltpu.make_async_copy(lb_buf.at[slot], lb_hbm.at[h, pair], sems.at[1, slot]))

        def step(jj, carry, mask):
            acc, R = carry
            j = last - jj
            slot = jj % 2

            @pl.when(jj >= 2)
            def _():
                for c in save(slot, 0):
                    c.wait()

            rows = pl.ds(pl.multiple_of(j * B, B), B)
            z = _dot(q, k_ref[rows, :], _NT) * (HEAD_DIM ** -0.5)
            valid, lb, l1m = _sb_terms(z, ahead if mask else None, j * B - i * TQ)
            att = _masked(valid, jnp.exp(lb + R + _dot01(l1m, upper, passes=2))).astype(BF16)
            att_buf[slot] = att
            lb_buf[slot] = (lb if valid is None else jnp.where(valid, lb, -1e30)).astype(BF16)
            for c in save(slot, base + j):
                c.start()
            return acc + _dot(att, v_ref[rows, :]), R + _rowsum(l1m)

        carry = (jnp.zeros((TQ, HEAD_DIM), F32), jnp.zeros((TQ, 1), F32))
        carry = lax.fori_loop(0, per, functools.partial(step, mask=True), carry)
        acc, _ = lax.fori_loop(per, last + 1, functools.partial(step, mask=False), carry)
        for slot in range(2):
            for c in save(slot, 0):
                c.wait()
        o_ref[...] = acc
        mix_ref[...] = _rms(acc, gain_ref[...]).astype(BF16)

    head = lambda off: pl.BlockSpec((T, HEAD_DIM), lambda h, i: (0, off + h))
    blk = pl.BlockSpec((TQ, HEAD_DIM), lambda h, i: (i, h))
    return pl.pallas_call(
        body, name="sb_attention", grid=(H, T // TQ),
        in_specs=[blk, head(H), head(2 * H), pl.BlockSpec((1, HEAD_DIM), lambda h, i: (0, 0))],
        out_specs=[blk, blk, ANY_SPEC, ANY_SPEC],
        out_shape=[_sds((T, H * HEAD_DIM), F32), _sds((T, H * HEAD_DIM), BF16),
                   _sds((H, n_saved, TQ, B), BF16), _sds((H, n_saved, TQ, B), BF16)],
        scratch_shapes=[pltpu.VMEM((2, TQ, B), BF16), pltpu.VMEM((2, TQ, B), BF16), pltpu.SemaphoreType.DMA((2, 2))],
        compiler_params=_params(("parallel", "arbitrary"), 8 * _nbytes((T, HEAD_DIM), BF16) + 32 * _nbytes((TQ, B), F32)),
    )(qkv, qkv, qkv, gain)


def _sb_attention_bwd(qkv, gain, o_raw, att_all, lb_all, dmix):
    T = qkv.shape[0]
    H = SB_HEADS
    TQ, B = att_all.shape[2:]
    per = TQ // B
    scale = HEAD_DIM ** -0.5

    def body(q_ref, k_ref, v_ref, gain_ref, o_ref, dmix_ref, att_hbm, lb_hbm, dq_ref, dk_ref, dv_ref, dgain_ref,
             att_buf, lb_buf, sems):
        h, i = pl.program_id(0), pl.program_id(1)

        @pl.when(i == 0)
        def _():
            dk_ref[...] = jnp.zeros_like(dk_ref)
            dv_ref[...] = jnp.zeros_like(dv_ref)

        @pl.when((pl.program_id(0) == 0) & (i == 0))
        def _():
            dgain_ref[...] = jnp.zeros_like(dgain_ref)

        q = q_ref[...]
        o = o_ref[...]
        do, dgain = _rms_bwd(dmix_ref[...], o, gain_ref[...])
        dgain_ref[...] += dgain
        do_b = do.astype(BF16)
        before = (_iota2((B, B), 0) < _iota2((B, B), 1)).astype(BF16)
        pairs = (i + 1) * per
        base = per * (i * (i + 1) // 2)

        def fetch(slot, pair):
            return (pltpu.make_async_copy(att_hbm.at[h, pair], att_buf.at[slot], sems.at[0, slot]),
                    pltpu.make_async_copy(lb_hbm.at[h, pair], lb_buf.at[slot], sems.at[1, slot]))

        for c in fetch(0, base):
            c.start()

        def step(j, carry):
            dq, PG = carry
            slot = j % 2
            for c in fetch(slot, 0):
                c.wait()

            @pl.when(j + 1 < pairs)
            def _():
                for c in fetch(1 - slot, base + j + 1):
                    c.start()

            rows = pl.ds(pl.multiple_of(j * B, B), B)
            kj = k_ref[rows, :]
            att = att_buf[slot]
            sig = jnp.exp(lb_buf[slot].astype(F32))
            G = _dot(do_b, v_ref[rows, :], _NT) * att.astype(F32)
            dv_ref[rows, :] += _dot(att, do_b, _TN)
            cum = PG + _dot01(G, before, passes=2)
            dz = (G * (1.0 - sig) - sig * cum) * scale
            dk_ref[rows, :] += _dot(dz, q, _TN)
            return dq + _dot(dz, kj), PG + _rowsum(G)

        dq, _ = lax.fori_loop(0, pairs, step, (jnp.zeros((TQ, HEAD_DIM), F32), jnp.zeros((TQ, 1), F32)))
        dq_ref[...] = dq.astype(BF16)

    head = lambda off: pl.BlockSpec((T, HEAD_DIM), lambda h, i: (0, off + h))
    blk = pl.BlockSpec((TQ, HEAD_DIM), lambda h, i: (i, h))
    vec = pl.BlockSpec((1, HEAD_DIM), lambda h, i: (0, 0))
    return pl.pallas_call(
        body, name="sb_attention_bwd", grid=(H, T // TQ),
        in_specs=[blk, head(H), head(2 * H), vec, blk, blk, ANY_SPEC, ANY_SPEC],
        out_specs=[blk, head(0), head(0), vec],
        out_shape=[_sds((T, H * HEAD_DIM), BF16), _sds((T, H * HEAD_DIM), F32), _sds((T, H * HEAD_DIM), F32),
                   _sds((1, HEAD_DIM), F32)],
        scratch_shapes=[pltpu.VMEM((2, TQ, B), BF16), pltpu.VMEM((2, TQ, B), BF16), pltpu.SemaphoreType.DMA((2, 2))],
        compiler_params=_params(("arbitrary", "arbitrary"), 8 * _nbytes((T, HEAD_DIM), F32) + 32 * _nbytes((TQ, B), F32)),
    )(qkv, qkv, qkv, gain, o_raw, dmix, att_all, lb_all)


def _adamw_math(w, g, m, v):
    m = ADAM_B1 * m + (1.0 - ADAM_B1) * g
    v = ADAM_B2 * v + (1.0 - ADAM_B2) * (g * g)
    m_hat = m / (1.0 - ADAM_B1 ** ADAM_STEP)
    v_hat = v / (1.0 - ADAM_B2 ** ADAM_STEP)
    delta = -ADAM_LR * (m_hat / (jnp.sqrt(v_hat) + ADAM_EPS) + ADAM_WD * w)
    return delta, m, v


def _adamw_sharded(parts, w, m, v, name):
    _, R, C = w.shape
    if R % SUBLANES == 0:
        tr, tc = _tile(R, max(SUBLANES, (2**20 // (4 * C)) // SUBLANES * SUBLANES), SUBLANES), C
    else:
        tr, tc = R, _tile(C, max(LANES, (2**20 // (4 * R)) // LANES * LANES), LANES)

    def body(p_ref, w_ref, m_ref, v_ref, g_ref, d_ref, nm_ref, nv_ref):
        g = p_ref[0].astype(F32)
        for d in range(1, N_DEV):
            g = g + p_ref[d].astype(F32)
        g_ref[...] = g
        d_ref[...], nm_ref[...], nv_ref[...] = _adamw_math(w_ref[...], g, m_ref[...], v_ref[...])

    blk = pl.BlockSpec((None, tr, tc), lambda i, j: (0, i, j))
    return pl.pallas_call(
        body, name=name, grid=(R // tr, C // tc),
        in_specs=[pl.BlockSpec((N_DEV, tr, tc), lambda i, j: (0, i, j)), blk, blk, blk],
        out_specs=[blk] * 4, out_shape=[_sds((1, R, C), F32)] * 4,
        compiler_params=_params(("parallel", "parallel"), 40 * 2**20),
    )(parts, w, m, v)


def _adamw_packed(g, w, m, v):
    def body(g_ref, w_ref, m_ref, v_ref, d_ref, nm_ref, nv_ref):
        d_ref[...], nm_ref[...], nv_ref[...] = _adamw_math(w_ref[...], g_ref[...], m_ref[...], v_ref[...])

    return pl.pallas_call(body, name="adamw_packed", out_shape=[_sds(g.shape, F32)] * 3,
                          compiler_params=_params((), 16 * 2**20))(g, w, m, v)


def _my_place():
    x, y, c = lax.axis_index("x"), lax.axis_index("y"), lax.axis_index("c")
    return x, y, c


def _peer(place, k):
    x, y, c = place
    return (1 - x if k & 4 else x, 1 - y if k & 2 else y, 1 - c if k & 1 else c)


def _index(place):
    x, y, c = place
    return 4 * x + 2 * y + c


HBM_SPEC = pl.BlockSpec(memory_space=pltpu.HBM)


def _all_gather(block, name):
    R, C = block.shape

    def body(x_ref, out_ref, send_sems, recv_sems, local_sem):
        me = _my_place()
        sibling = _peer(me, 1)
        chips = [2, 4, 6]

        def copy(sem, origin, to, src=None):
            slot = out_ref.at[_index(origin)]
            return pltpu.make_async_remote_copy(
                src_ref=slot if src is None else src, dst_ref=slot, send_sem=send_sems.at[sem], recv_sem=recv_sems.at[sem],
                device_id=to, device_id_type=MESH)

        mine = pltpu.make_async_copy(x_ref, out_ref.at[_index(me)], local_sem)
        mine.start()
        first = [copy(0, me, sibling, src=x_ref)] + [copy(1 + n, me, _peer(me, k), src=x_ref) for n, k in enumerate(chips)]
        for cp in first:
            cp.start()
        passed = [copy(4 + n, _peer(me, k), sibling) for n, k in enumerate(chips)]
        for n, k in enumerate(chips):
            copy(1 + n, _peer(me, k), me).wait_recv()
            passed[n].start()
        copy(0, sibling, me).wait_recv()
        for n, k in enumerate(chips):
            copy(4 + n, _peer(sibling, k), me).wait_recv()
        for cp in first + passed:
            cp.wait_send()
        mine.wait()

    return pl.pallas_call(
        body, name=name, in_specs=[HBM_SPEC], out_specs=HBM_SPEC,
        out_shape=_sds((N_DEV, R, C), block.dtype),
        scratch_shapes=[pltpu.SemaphoreType.DMA((7,)), pltpu.SemaphoreType.DMA((7,)), pltpu.SemaphoreType.DMA],
    )(block)


SEM_SPEC = pl.BlockSpec(memory_space=pltpu.SEMAPHORE)
ANY_SPEC = pl.BlockSpec(memory_space=pl.ANY)
_EFFECT = pltpu.SideEffectType.DATAFLOW_SIDE_EFFECTING


def _spread_start(x, per_peer, name, after):
    R, C = x.shape[-2:]

    def body(x_ref, land_ref, after_ref, send_sems, recv_sems, x_thru, land_thru, token):
        me = _my_place()
        for k in range(1, N_DEV):
            to = _peer(me, k)
            pltpu.make_async_remote_copy(
                src_ref=x_ref.at[_index(to)] if per_peer else x_ref, dst_ref=land_ref.at[_index(me)],
                send_sem=send_sems.at[k - 1], recv_sem=recv_sems.at[k - 1], device_id=to, device_id_type=MESH).start()
        token[...] = jnp.zeros_like(token)

    land = lax.empty((N_DEV, R, C), x.dtype)
    send_sems, recv_sems, x_thru, land_thru, token = pl.pallas_call(
        body, name=name,
        out_shape=(pltpu.SemaphoreType.DMA((N_DEV - 1,)), pltpu.SemaphoreType.DMA((N_DEV - 1,)),
                   pltpu.HBM(x.shape, x.dtype), pltpu.HBM(land.shape, land.dtype), _sds((SUBLANES, LANES), F32)),
        in_specs=(HBM_SPEC, HBM_SPEC, ANY_SPEC),
        out_specs=(SEM_SPEC, SEM_SPEC, HBM_SPEC, HBM_SPEC, pl.BlockSpec(memory_space=pltpu.VMEM)),
        input_output_aliases={0: 2, 1: 3},
        compiler_params=pltpu.CompilerParams(has_side_effects=_EFFECT),
    )(pltpu.with_memory_space_constraint(x, pltpu.HBM), pltpu.with_memory_space_constraint(land, pltpu.HBM), after)
    return (send_sems, recv_sems, x_thru, land_thru), token


def _spread_wait(state, per_peer, name, after):
    send_sems, recv_sems, x_thru, land_thru = state

    def body(x_ref, land_ref, send_sems, recv_sems, after_ref, x_dead, got_ref):
        me = _my_place()
        for k in range(1, N_DEV):
            frm = _peer(me, k)
            copy = pltpu.make_async_remote_copy(
                src_ref=x_ref.at[_index(frm)] if per_peer else x_ref, dst_ref=land_ref.at[_index(frm)],
                send_sem=send_sems.at[k - 1], recv_sem=recv_sems.at[k - 1], device_id=frm, device_id_type=MESH)
            copy.wait_send()
            copy.wait_recv()

    x_back, got = pl.pallas_call(
        body, name=name,
        out_shape=(pltpu.HBM(x_thru.shape, x_thru.dtype), pltpu.HBM(land_thru.shape, land_thru.dtype)),
        in_specs=(HBM_SPEC, HBM_SPEC, SEM_SPEC, SEM_SPEC, ANY_SPEC), out_specs=(HBM_SPEC, HBM_SPEC),
        input_output_aliases={0: 0, 1: 1},
        compiler_params=pltpu.CompilerParams(has_side_effects=_EFFECT),
    )(x_thru, land_thru, send_sems, recv_sems, after)
    me = _index(_my_place())
    own = lax.dynamic_index_in_dim(x_back, me, axis=0, keepdims=True) if per_peer else x_back[None]
    return lax.dynamic_update_slice_in_dim(got, own, me, axis=0)


def _all_reduce_packed(vec, after):
    R, L = vec.shape

    def body(x_ref, after_ref, out_ref, buf, send_sems, recv_sems):
        me = _my_place()
        buf[_index(me)] = x_ref[...]
        copies = []
        for k in range(1, N_DEV):
            to = _peer(me, k)
            cp = pltpu.make_async_remote_copy(
                src_ref=x_ref, dst_ref=buf.at[_index(me)],
                send_sem=send_sems.at[k - 1], recv_sem=recv_sems.at[k - 1], device_id=to, device_id_type=MESH)
            cp.start()
            copies.append(cp)
        for k in range(1, N_DEV):
            frm = _peer(me, k)
            pltpu.make_async_remote_copy(
                src_ref=x_ref, dst_ref=buf.at[_index(frm)],
                send_sem=send_sems.at[k - 1], recv_sem=recv_sems.at[k - 1], device_id=frm, device_id_type=MESH).wait_recv()
        for cp in copies:
            cp.wait_send()
        acc = buf[0]
        for d in range(1, N_DEV):
            acc = acc + buf[d]
        out_ref[...] = acc

    vm = pl.BlockSpec(memory_space=pltpu.VMEM)
    return pl.pallas_call(
        body, name="all_reduce_packed", in_specs=[vm, ANY_SPEC], out_specs=vm, out_shape=_sds((R, L), F32),
        scratch_shapes=[pltpu.VMEM((N_DEV, R, L), F32), pltpu.SemaphoreType.DMA((7,)), pltpu.SemaphoreType.DMA((7,))],
        compiler_params=pltpu.CompilerParams(vmem_limit_bytes=32 * 2**20),
    )(vec, after)


def _pack(arrays):
    rows = []
    for a in arrays:
        f = a.reshape(-1).astype(F32)
        pad = (-f.shape[0]) % LANES
        rows.append(jnp.pad(f, (0, pad)).reshape(-1, LANES))
    out = jnp.concatenate(rows, axis=0)
    return jnp.pad(out, ((0, (-out.shape[0]) % SUBLANES), (0, 0)))


def _unpack(packed, shapes):
    out, r = [], 0
    for s in shapes:
        n = math.prod(s)
        nr = -(-n // LANES)
        out.append(packed[r:r + nr].reshape(-1)[:n].reshape(s))
        r += nr
    return out


def _row_blocks(g):
    R, C = g.shape
    return g.astype(BF16).reshape(N_DEV, R // N_DEV, C)


def kernel(x, w_in, sb_out_gain, dn_conv_w, dn_a_log, dn_dt_bias, dn_out_gain, w_out, ln_mix_pre, ln_mix_post, w_up, ffn_conv_w, ffn_conv_b, w_down, ln_ffn_pre, ln_ffn_post, loss_target, m_w_in, m_sb_out_gain, m_dn_conv_w, m_dn_a_log, m_dn_dt_bias, m_dn_out_gain, m_w_out, m_ln_mix_pre, m_ln_mix_post, m_w_up, m_ffn_conv_w, m_ffn_conv_b, m_w_down, m_ln_ffn_pre, m_ln_ffn_post, v_w_in, v_sb_out_gain, v_dn_conv_w, v_dn_a_log, v_dn_dt_bias, v_dn_out_gain, v_w_out, v_ln_mix_pre, v_ln_mix_post, v_w_up, v_ffn_conv_w, v_ffn_conv_b, v_w_down, v_ln_ffn_pre, v_ln_ffn_post):
    T, D = x.shape[1], x.shape[2]
    SBW = SB_HEADS * HEAD_DIM
    DNW = DN_HEADS * HEAD_DIM
    in_cols = 3 * SBW + 4 * DNW + 2 * DN_HEADS
    main_cols = 3 * SBW + 4 * DNW
    in_pad = main_cols + LANES
    qkv0, z0 = 3 * SBW, 3 * SBW + 3 * DNW
    gate_block = main_cols // LANES
    x2, tgt = x[0], loss_target[0]

    g_in = _all_gather(jnp.swapaxes(w_in[0], 0, 1).astype(BF16), "gather_w_in")
    small_w = _all_gather(_pack([dn_conv_w[0], ffn_conv_w[0]]), "gather_conv_w")
    st_out, tok = _spread_start(w_out[0].astype(BF16), False, "gather_w_out_start", g_in)
    st_up, tok = _spread_start(w_up[0].astype(BF16), False, "gather_w_up_start", tok)
    st_down, tok_gather = _spread_start(w_down[0].astype(BF16), False, "gather_w_down_start", tok)
    w_in_t = jnp.pad(g_in.reshape(in_cols, D), ((0, in_pad - in_cols), (0, 0)))
    parts = [_unpack(small_w[d], [dn_conv_w.shape[1:], ffn_conv_w.shape[1:]]) for d in range(N_DEV)]
    dn_cw = jnp.concatenate([p[0] for p in parts], axis=1)
    ffn_cw = jnp.concatenate([p[1] for p in parts], axis=1)
    lane_pad = lambda a, off: jnp.pad(a, ((0, 0), (off, LANES - off - a.shape[1])))
    a_log_l, dt_bias_l = lane_pad(dn_a_log, DN_HEADS), lane_pad(dn_dt_bias, DN_HEADS)

    xn = _norm_in(x2, ln_mix_pre)
    proj = _matmul(xn, w_in_t, "nt", F32, "proj_in", tm_cap=512, tn_cap=2432, after=tok_gather)
    sb_qkv = proj[:, :3 * SBW].astype(BF16)
    o_sb, mix_sb, sb_att, sb_lb = _sb_attention(sb_qkv, sb_out_gain)
    qn = _dn_branch(proj, qkv0, dn_cw, 0, True, HEAD_DIM ** -0.5)
    kn = _dn_branch(proj, qkv0 + DNW, dn_cw, DNW, True, 1.0)
    vn = _dn_branch(proj, qkv0 + 2 * DNW, dn_cw, 2 * DNW, False, 1.0)
    gc_full, beta_full = _dn_gates(proj, gate_block, a_log_l, dt_bias_l)
    o_dn, mix_dn, tm_all, s_all = _dn_scan(qn, kn, vn, gc_full, beta_full, proj, z0, dn_out_gain)
    mix = jnp.concatenate([mix_sb, mix_dn], axis=1)
    w_out_f = _spread_wait(st_out, False, "gather_w_out_wait", mix).reshape(w_out.shape[1] * N_DEV, D)
    m = _matmul(mix, w_out_f, "nn", F32, "proj_out")
    h, hn = _mix_residual(x2, m, ln_mix_post, ln_ffn_pre)
    w_up_cut = _spread_wait(st_up, False, "gather_w_up_wait", hn)
    u = _matmul(hn, w_up_cut, "nn", F32, "ffn_up", tn_cap=w_up.shape[2], b_cut=True)
    act = _ffn_act(u, ffn_cw, ffn_conv_b)
    w_down_f = _spread_wait(st_down, False, "gather_w_down_wait", act).reshape(w_down.shape[1] * N_DEV, D)
    f = _matmul(act, w_down_f, "nn", F32, "ffn_down", tk_cap=2816)
    dy, df, d_ln_ffn_post, loss_part = _loss_head(h, f, ln_ffn_post, tgt)

    d_w_down = _matmul(act, df, "tn", BF16, "grad_w_down")
    st_xd, tok = _spread_start(_row_blocks(d_w_down), True, "exchange_w_down_start", loss_part)
    da = _matmul(df, w_down_f, "nt", F32, "bwd_ffn_down", after=tok)
    du, d_ffn_cwb = _ffn_act_bwd(u, ffn_cw, ffn_conv_b, da)
    d_ffn_cwb = jnp.concatenate([d_ffn_cwb[0], d_ffn_cwb[1]], axis=1)
    d_w_up_cut = _matmul(hn, du, "tn", BF16, "grad_w_up", b_cut=True, out_cut=True)
    st_xu, tok = _spread_start(d_w_up_cut, True, "exchange_w_up_start", d_ffn_cwb)
    dhn = _matmul(du, w_up_cut, "nt", F32, "bwd_ffn_up", after=tok, a_cut=True, b_cut=True)
    dh, dm, d_ln_ffn_pre, d_ln_mix_post = _ffn_residual_bwd(dy, dhn, h, ln_ffn_pre, m, ln_mix_post)

    d_w_out = _matmul(mix, dm, "tn", BF16, "grad_w_out")
    st_xo, tok = _spread_start(_row_blocks(d_w_out), True, "exchange_w_out_start", d_ln_ffn_pre)
    dmix = _matmul(dm, w_out_f, "nt", F32, "bwd_proj_out", after=tok)
    dq_sb, dk_sb, dv_sb, d_sb_gain = _sb_attention_bwd(sb_qkv, sb_out_gain, o_sb, sb_att, sb_lb, dmix)
    dqn, dkn, dvn, dgc_full, dbeta_full, dz, d_dn_gain = _dn_scan_bwd(
        qn, kn, vn, gc_full, beta_full, proj, z0, dn_out_gain, o_dn, tm_all, s_all, dmix, SBW)
    du_q, dcw_q = _dn_branch_bwd(proj, qkv0, dn_cw, 0, True, HEAD_DIM ** -0.5, dqn)
    du_k, dcw_k = _dn_branch_bwd(proj, qkv0 + DNW, dn_cw, DNW, True, 1.0, dkn)
    du_v, dcw_v = _dn_branch_bwd(proj, qkv0 + 2 * DNW, dn_cw, 2 * DNW, False, 1.0, dvn)
    dba, d_a_log_l, d_dt_bias_l = _dn_gates_bwd(proj, gate_block, a_log_l, dt_bias_l, dgc_full, dbeta_full)
    dproj = jnp.concatenate([dq_sb, dk_sb.astype(BF16), dv_sb.astype(BF16), du_q, du_k, du_v, dz, dba], axis=1)
    d_w_in_t = _matmul(dproj, xn, "tn", BF16, "grad_w_in", tm_cap=2432, tn_cap=512, tk_cap=1024)
    d_w_in_cut = d_w_in_t[:in_cols].reshape(N_DEV, in_cols // N_DEV, D)
    st_xi, tok = _spread_start(d_w_in_cut, True, "exchange_w_in_start", d_sb_gain)
    dxn = _matmul(dproj, w_in_t, "nn", F32, "bwd_proj_in", tk_cap=2432, after=tok)
    grad_x, d_ln_mix_pre = _input_bwd(dh, dxn, x2, ln_mix_pre)

    big = {}
    after = grad_x
    for n, st, w_, m_, v_ in [("w_down", st_xd, w_down, m_w_down, v_w_down), ("w_up", st_xu, w_up, m_w_up, v_w_up),
                              ("w_out", st_xo, w_out, m_w_out, v_w_out)]:
        got = _spread_wait(st, True, "exchange_" + n + "_wait", after)
        big[n] = _adamw_sharded(got, w_, m_, v_, "adamw_" + n)
        after = big[n][1]

    d_dn_cw = jnp.concatenate([dcw_q[:SHORT_CONV], dcw_k[:SHORT_CONV], dcw_v[:SHORT_CONV]], axis=1)
    small = [loss_part[:, :1], d_sb_gain, d_a_log_l[:, DN_HEADS:2 * DN_HEADS], d_dt_bias_l[:, DN_HEADS:2 * DN_HEADS], d_dn_gain,
             d_ln_mix_pre, d_ln_mix_post, d_ffn_cwb[FFN_CONV:FFN_CONV + 1], d_ln_ffn_pre, d_ln_ffn_post,
             d_dn_cw, d_ffn_cwb[:FFN_CONV]]
    shapes = [a.shape for a in small]
    red = _unpack(_all_reduce_packed(_pack(small), after), shapes)
    loss = red[0].reshape(())
    me = _index(_my_place())
    g_dn_cw = lax.dynamic_slice_in_dim(red[10], me * dn_conv_w.shape[2], dn_conv_w.shape[2], axis=1)
    g_ffn_cw = lax.dynamic_slice_in_dim(red[11], me * ffn_conv_w.shape[2], ffn_conv_w.shape[2], axis=1)
    names = ["sb_out_gain", "dn_conv_w", "dn_a_log", "dn_dt_bias", "dn_out_gain", "ln_mix_pre", "ln_mix_post",
             "ffn_conv_w", "ffn_conv_b", "ln_ffn_pre", "ln_ffn_post"]
    g_small = dict(sb_out_gain=red[1], dn_conv_w=g_dn_cw[None], dn_a_log=red[2], dn_dt_bias=red[3], dn_out_gain=red[4],
                   ln_mix_pre=red[5], ln_mix_post=red[6], ffn_conv_w=g_ffn_cw[None], ffn_conv_b=red[7],
                   ln_ffn_pre=red[8], ln_ffn_post=red[9])
    w_small = dict(sb_out_gain=sb_out_gain, dn_conv_w=dn_conv_w, dn_a_log=dn_a_log, dn_dt_bias=dn_dt_bias,
                   dn_out_gain=dn_out_gain, ln_mix_pre=ln_mix_pre, ln_mix_post=ln_mix_post, ffn_conv_w=ffn_conv_w,
                   ffn_conv_b=ffn_conv_b, ln_ffn_pre=ln_ffn_pre, ln_ffn_post=ln_ffn_post)
    m_small = dict(sb_out_gain=m_sb_out_gain, dn_conv_w=m_dn_conv_w, dn_a_log=m_dn_a_log, dn_dt_bias=m_dn_dt_bias,
                   dn_out_gain=m_dn_out_gain, ln_mix_pre=m_ln_mix_pre, ln_mix_post=m_ln_mix_post, ffn_conv_w=m_ffn_conv_w,
                   ffn_conv_b=m_ffn_conv_b, ln_ffn_pre=m_ln_ffn_pre, ln_ffn_post=m_ln_ffn_post)
    v_small = dict(sb_out_gain=v_sb_out_gain, dn_conv_w=v_dn_conv_w, dn_a_log=v_dn_a_log, dn_dt_bias=v_dn_dt_bias,
                   dn_out_gain=v_dn_out_gain, ln_mix_pre=v_ln_mix_pre, ln_mix_post=v_ln_mix_post, ffn_conv_w=v_ffn_conv_w,
                   ffn_conv_b=v_ffn_conv_b, ln_ffn_pre=v_ln_ffn_pre, ln_ffn_post=v_ln_ffn_post)
    sshapes = [w_small[n].shape for n in names]
    upd = _adamw_packed(_pack([g_small[n] for n in names]), _pack([w_small[n] for n in names]),
                        _pack([m_small[n] for n in names]), _pack([v_small[n] for n in names]))
    d_small, nm_small, nv_small = [dict(zip(names, _unpack(p, sshapes))) for p in upd]

    got = _spread_wait(st_xi, True, "exchange_w_in_wait", d_small["ln_ffn_post"])
    flip = lambda a: jnp.swapaxes(a, 1, 2)
    big["w_in"] = [flip(a) for a in _adamw_sharded(got, flip(w_in), flip(m_w_in), flip(v_w_in), "adamw_w_in")]

    order = ["w_in", "sb_out_gain", "dn_conv_w", "dn_a_log", "dn_dt_bias", "dn_out_gain", "w_out", "ln_mix_pre",
             "ln_mix_post", "w_up", "ffn_conv_w", "ffn_conv_b", "w_down", "ln_ffn_pre", "ln_ffn_post"]
    pick = lambda n, i: big[n][i] if n in big else [g_small, d_small, nm_small, nv_small][i][n].reshape(w_small[n].shape)
    return (loss, grad_x[None], *[pick(n, 0) for n in order], *[pick(n, 1) for n in order],
            *[pick(n, 2) for n in order], *[pick(n, 3) for n in order])
```

```python
import functools
import math

import jax
import jax.numpy as jnp
from jax import lax
from jax.experimental import pallas as pl
from jax.experimental.pallas import tpu as pltpu

F32 = jnp.float32
BF16 = jnp.bfloat16

N_DEV = 8
HEAD_DIM = 128
SB_HEADS = 8
DN_HEADS = 8
DN_CHUNK = 128
DN_GROUP = 8
INV_BLOCK = 16
SB_KEYS = 256
SHORT_CONV = 4
FFN_CONV = 3
EPS = 1e-6
LANES = 128
SUBLANES = 8
VMEM_CAP = 56 * 2**20

ADAM_LR = 0.001
ADAM_B1 = 0.9
ADAM_B2 = 0.999
ADAM_EPS = 1e-08
ADAM_WD = 0.01
ADAM_STEP = 10

MESH = pl.DeviceIdType.MESH

assert HEAD_DIM == DN_CHUNK == LANES


def _tile(n, cap, mult):
    if n <= cap:
        return n
    t = (cap // mult) * mult
    while t >= mult:
        if n % t == 0:
            return t
        t -= mult
    raise ValueError(f"no tile for {n} under {cap} in multiples of {mult}")


def _params(sem, vmem_bytes):
    limit = int(min(VMEM_CAP, max(vmem_bytes, 16 * 2**20)))
    if not sem:
        return pltpu.CompilerParams(vmem_limit_bytes=limit)
    return pltpu.CompilerParams(dimension_semantics=sem, vmem_limit_bytes=limit)


def _nbytes(shape, dtype):
    return math.prod(shape) * jnp.dtype(dtype).itemsize


_NN = (((1,), (0,)), ((), ()))
_NT = (((1,), (1,)), ((), ()))
_TN = (((0,), (0,)), ((), ()))


def _batched(dims, ndim):
    if ndim == 2:
        return dims
    (ca,), (cb,) = dims[0]
    return (((ca + 1,), (cb + 1,)), ((0,), (0,)))


def _dot(a, b, dims=_NN):
    return lax.dot_general(a.astype(BF16), b.astype(BF16), _batched(dims, a.ndim), preferred_element_type=F32)


def _split2(x):
    hi = x.astype(BF16)
    lo = (x - hi.astype(F32)).astype(BF16)
    return hi, lo


def _split3(x):
    hi = x.astype(BF16)
    r = x - hi.astype(F32)
    mid = r.astype(BF16)
    lo = (r - mid.astype(F32)).astype(BF16)
    return hi, mid, lo


def _dot01(x, m01, passes=3):
    parts = _split3(x) if passes == 3 else _split2(x)
    out = None
    for p in parts:
        t = lax.dot_general(p, m01, _NN, preferred_element_type=F32)
        out = t if out is None else out + t
    return out


def _dot01_left(m01, x, passes=3):
    parts = _split3(x) if passes == 3 else _split2(x)
    out = None
    for p in parts:
        t = lax.dot_general(m01, p, _NN, preferred_element_type=F32)
        out = t if out is None else out + t
    return out


def _mm3(a, b, dims=_NN):
    ah, al = _split2(a)
    bh, bl = _split2(b)
    d = functools.partial(lax.dot_general, dimension_numbers=_batched(dims, a.ndim), preferred_element_type=F32)
    return d(ah, bh) + (d(ah, bl) + d(al, bh))


def _rowsum(x):
    return jnp.sum(x, axis=-1, keepdims=True)


def _t(x):
    return jnp.swapaxes(x, -1, -2)


def _sigmoid(x):
    return 1.0 / (1.0 + jnp.exp(-x))


def _softplus(x):
    return jnp.maximum(x, 0.0) + jnp.log(1.0 + jnp.exp(-jnp.abs(x)))


def _silu(x):
    return x * _sigmoid(x)


def _silu_grad(x):
    s = _sigmoid(x)
    return s * (1.0 + x * (1.0 - s))


_GELU_C = math.sqrt(2.0 / math.pi)


def _gelu(x):
    return 0.5 * x * (1.0 + jnp.tanh(_GELU_C * (x + 0.044715 * x * x * x)))


def _gelu_and_grad(x):
    x2 = x * x
    th = jnp.tanh(_GELU_C * (x + 0.044715 * x2 * x))
    half = 0.5 * (1.0 + th)
    return x * half, half + 0.5 * x * (1.0 - th * th) * (_GELU_C * (1.0 + 3.0 * 0.044715 * x2))


def _rms(x, g):
    r = lax.rsqrt(jnp.mean(x * x, axis=-1, keepdims=True) + EPS)
    return x * r * g


def _rms_bwd(dy, x, g):
    r = lax.rsqrt(jnp.mean(x * x, axis=-1, keepdims=True) + EPS)
    xh = x * r
    gdy = dy * g
    dx = r * (gdy - xh * jnp.mean(gdy * xh, axis=-1, keepdims=True))
    return dx, jnp.sum(dy * xh, axis=-2, keepdims=True)


def _iota2(shape, axis):
    return lax.broadcasted_iota(jnp.int32, shape, axis)


def _shift_down(cur, prev8, k):
    n = cur.shape[0]
    r = pltpu.roll(cur, k, 0)
    pr = pltpu.roll(prev8, k, 0)
    head = jnp.where(_iota2(pr.shape, 0) < k, pr, r[0:SUBLANES])
    if n == SUBLANES:
        return head
    return jnp.concatenate([head, r[SUBLANES:]], axis=0)


def _shift_up(cur, next8, k):
    n = cur.shape[0]
    r = pltpu.roll(cur, n - k, 0)
    nr = pltpu.roll(next8, SUBLANES - k, 0)
    tail = jnp.where(_iota2(nr.shape, 0) >= SUBLANES - k, nr, r[n - SUBLANES:])
    if n == SUBLANES:
        return tail
    return jnp.concatenate([r[:n - SUBLANES], tail], axis=0)


def _causal_conv(cur, prev8, w_ref, taps):
    out = cur * w_ref[taps - 1:taps, :]
    for j in range(taps - 1):
        out = out + _shift_down(cur, prev8, taps - 1 - j) * w_ref[j:j + 1, :]
    return out


def _anti_conv(cur, next8, w_ref, taps):
    out = cur * w_ref[taps - 1:taps, :]
    for j in range(taps - 1):
        out = out + _shift_up(cur, next8, taps - 1 - j) * w_ref[j:j + 1, :]
    return out


def _matmul(a, b, mode, out_dtype, name, tm_cap=1024, tn_cap=1024, tk_cap=2048, after=None,
            a_cut=False, b_cut=False, out_cut=False):
    a_shard = a.shape[2] if a_cut else None
    b_shard = b.shape[2] if b_cut else None
    a_full = (a.shape[1], a.shape[0] * a_shard) if a_cut else a.shape
    b_full = (b.shape[1], b.shape[0] * b_shard) if b_cut else b.shape
    assert not (a_cut and mode == "tn")
    if mode == "nn":
        (M, K), N = a_full, b_full[1]
    elif mode == "nt":
        (M, K), N = a_full, b_full[0]
    else:
        (K, M), N = a_full, b_full[1]
    n_unit = b_shard if (b_cut and mode != "nt") else N
    k_unit = math.gcd(a_shard or K, b_shard if (b_cut and mode == "nt") else K)
    tm = _tile(M, tm_cap, LANES)
    tn = N // N_DEV if out_cut else _tile(n_unit, tn_cap, LANES)
    tk = _tile(k_unit, tk_cap, LANES)
    assert n_unit % tn == 0 and k_unit % tk == 0
    nk = K // tk
    dims = {"nn": _NN, "nt": _NT, "tn": _TN}[mode]
    if a_cut:
        pa = a_shard // tk
        a_spec = pl.BlockSpec((None, tm, tk), lambda i, j, k: (k // pa, i, k % pa))
    elif mode == "tn":
        a_spec = pl.BlockSpec((tk, tm), lambda i, j, k: (k, i))
    else:
        a_spec = pl.BlockSpec((tm, tk), lambda i, j, k: (i, k))
    if b_cut and mode == "nt":
        pb = b_shard // tk
        b_spec = pl.BlockSpec((None, tn, tk), lambda i, j, k: (k // pb, j, k % pb))
    elif b_cut:
        pb = b_shard // tn
        b_spec = pl.BlockSpec((None, tk, tn), lambda i, j, k: (j // pb, k, j % pb))
    elif mode == "nt":
        b_spec = pl.BlockSpec((tn, tk), lambda i, j, k: (j, k))
    else:
        b_spec = pl.BlockSpec((tk, tn), lambda i, j, k: (k, j))
    if out_cut:
        out_spec, out_shape = pl.BlockSpec((None, tm, tn), lambda i, j, k: (j, i, 0)), (N_DEV, M, tn)
    else:
        out_spec, out_shape = pl.BlockSpec((tm, tn), lambda i, j, k: (i, j)), (M, N)

    def body(a_ref, b_ref, *rest):
        if nk == 1:
            rest[-1][...] = lax.dot_general(a_ref[...], b_ref[...], dims, preferred_element_type=F32).astype(rest[-1].dtype)
            return
        o_ref, acc_ref = rest[-2:]
        k = pl.program_id(2)

        @pl.when(k == 0)
        def _():
            acc_ref[...] = jnp.zeros_like(acc_ref)

        acc_ref[...] += lax.dot_general(a_ref[...], b_ref[...], dims, preferred_element_type=F32)

        @pl.when(k == nk - 1)
        def _():
            o_ref[...] = acc_ref[...].astype(o_ref.dtype)

    vmem = 2 * (_nbytes((tm, tk), a.dtype) + _nbytes((tk, tn), b.dtype) + _nbytes((tm, tn), out_dtype)) + _nbytes((tm, tn), F32)
    vmem += _nbytes((tm, tn), F32) + (2 * _nbytes((tm, tk), a.dtype) if mode == "tn" else 0)
    tokens = [] if after is None else [after]
    return pl.pallas_call(
        body, name=name, grid=(M // tm, N // tn, nk),
        in_specs=[a_spec, b_spec] + [pl.BlockSpec(t.shape, lambda i, j, k: (0, 0)) for t in tokens],
        out_specs=out_spec,
        out_shape=jax.ShapeDtypeStruct(out_shape, out_dtype),
        scratch_shapes=[] if nk == 1 else [pltpu.VMEM((tm, tn), F32)],
        compiler_params=_params(("parallel", "parallel", "arbitrary"), vmem + 4 * 2**20),
    )(a, b, *tokens)


def _row_call(body, name, T, D, ins, outs, tr, acc_outs=()):
    def spec(a, kind):
        if kind == "row":
            return pl.BlockSpec((tr, a.shape[1]), lambda i: (i, 0))
        return pl.BlockSpec(a.shape, lambda i: (0, 0))
    in_specs = [spec(a, k) for a, k in ins]
    out_specs = [spec(a, k) for a, k in outs] + [spec(a, "vec") for a in acc_outs]
    out_shape = [a for a, _ in outs] + list(acc_outs)
    vmem = 2 * sum(_nbytes((tr, a.shape[1]) if k == "row" else a.shape, a.dtype) for a, k in list(ins) + list(outs))
    return pl.pallas_call(
        body, name=name, grid=(T // tr,), in_specs=in_specs, out_specs=out_specs, out_shape=out_shape,
        compiler_params=_params(("arbitrary",), 3 * vmem + 8 * 2**20),
    )(*[a for a, _ in ins])


def _sds(shape, dtype):
    return jax.ShapeDtypeStruct(shape, dtype)


def _accumulate(ref, val):
    @pl.when(pl.program_id(0) == 0)
    def _():
        ref[...] = jnp.zeros_like(ref)
    ref[...] += val


def _norm_in(x, g):
    T, D = x.shape

    def body(x_ref, g_ref, o_ref):
        o_ref[...] = _rms(x_ref[...], g_ref[...]).astype(BF16)

    return _row_call(body, "norm_in", T, D, [(x, "row"), (g, "vec")], [(_sds((T, D), BF16), "row")], _tile(T, 256, 16))[0]


def _mix_residual(x, m, g_post, g_pre):
    T, D = x.shape

    def body(x_ref, m_ref, gp_ref, gn_ref, h_ref, hn_ref):
        h = x_ref[...] + _rms(m_ref[...], gp_ref[...])
        h_ref[...] = h
        hn_ref[...] = _rms(h, gn_ref[...]).astype(BF16)

    return _row_call(body, "mix_residual", T, D, [(x, "row"), (m, "row"), (g_post, "vec"), (g_pre, "vec")],
                     [(_sds((T, D), F32), "row"), (_sds((T, D), BF16), "row")], _tile(T, 256, 16))


def _loss_head(h, f, g_post, target):
    T, D = h.shape

    def body(h_ref, f_ref, g_ref, t_ref, dy_ref, df_ref, dg_ref, loss_ref):
        f = f_ref[...]
        g = g_ref[...]
        diff = h_ref[...] + _rms(f, g) - t_ref[...]
        dy = diff * (1.0 / D)
        dy_ref[...] = dy
        df, dg = _rms_bwd(dy, f, g)
        df_ref[...] = df.astype(BF16)
        _accumulate(dg_ref, dg)
        _accumulate(loss_ref, jnp.full((1, LANES), 0.5 / D, F32) * jnp.sum(diff * diff))

    return _row_call(body, "loss_head", T, D, [(h, "row"), (f, "row"), (g_post, "vec"), (target, "row")],
                     [(_sds((T, D), F32), "row"), (_sds((T, D), BF16), "row")], _tile(T, 256, 16),
                     acc_outs=[_sds((1, D), F32), _sds((1, LANES), F32)])


def _ffn_residual_bwd(dy, dhn, h, g_pre, m, g_post):
    T, D = h.shape

    def body(dy_ref, dhn_ref, h_ref, gn_ref, m_ref, gp_ref, dh_ref, dm_ref, dgn_ref, dgp_ref):
        dhh, dgn = _rms_bwd(dhn_ref[...], h_ref[...], gn_ref[...])
        dh = dy_ref[...] + dhh
        dh_ref[...] = dh
        dm, dgp = _rms_bwd(dh, m_ref[...], gp_ref[...])
        dm_ref[...] = dm.astype(BF16)
        _accumulate(dgn_ref, dgn)
        _accumulate(dgp_ref, dgp)

    return _row_call(body, "ffn_residual_bwd", T, D,
                     [(dy, "row"), (dhn, "row"), (h, "row"), (g_pre, "vec"), (m, "row"), (g_post, "vec")],
                     [(_sds((T, D), F32), "row"), (_sds((T, D), BF16), "row")], _tile(T, 128, 16),
                     acc_outs=[_sds((1, D), F32), _sds((1, D), F32)])


def _input_bwd(dh, dxn, x, g):
    T, D = x.shape

    def body(dh_ref, dxn_ref, x_ref, g_ref, dx_ref, dg_ref):
        dx, dg = _rms_bwd(dxn_ref[...], x_ref[...], g_ref[...])
        dx_ref[...] = dh_ref[...] + dx
        _accumulate(dg_ref, dg)

    return _row_call(body, "input_bwd", T, D, [(dh, "row"), (dxn, "row"), (x, "row"), (g, "vec")],
                     [(_sds((T, D), F32), "row")], _tile(T, 256, 16), acc_outs=[_sds((1, D), F32)])


def _ffn_act(u, conv_w, conv_b):
    T, F2 = u.shape
    F = F2 // 2
    tc = _tile(F, 512, LANES)
    tr = _tile(T, 512, SUBLANES)
    nc = F // tc
    r8 = tr // SUBLANES

    def body(ug_ref, ugp_ref, uv_ref, uvp_ref, wg_ref, wv_ref, bg_ref, bv_ref, a_ref):
        first = pl.program_id(1) == 0
        cg = _causal_conv(ug_ref[...], jnp.where(first, 0.0, ugp_ref[...]), wg_ref, FFN_CONV) + bg_ref[...]
        cv = _causal_conv(uv_ref[...], jnp.where(first, 0.0, uvp_ref[...]), wv_ref, FFN_CONV) + bv_ref[...]
        a_ref[...] = (_gelu(cg) * cv).astype(BF16)

    cur = lambda off: pl.BlockSpec((tr, tc), lambda j, i: (i, j + off))
    prev = lambda off: pl.BlockSpec((SUBLANES, tc), lambda j, i: (jnp.maximum(i * r8 - 1, 0), j + off))
    wsp = lambda off: pl.BlockSpec((FFN_CONV, tc), lambda j, i: (0, j + off))
    bsp = lambda off: pl.BlockSpec((1, tc), lambda j, i: (0, j + off))
    return pl.pallas_call(
        body, name="ffn_act", grid=(nc, T // tr),
        in_specs=[cur(0), prev(0), cur(nc), prev(nc), wsp(0), wsp(nc), bsp(0), bsp(nc)],
        out_specs=pl.BlockSpec((tr, tc), lambda j, i: (i, j)),
        out_shape=_sds((T, F), BF16),
        compiler_params=_params(("parallel", "arbitrary"), 12 * _nbytes((tr, tc), F32) + 8 * 2**20),
    )(u, u, u, u, conv_w, conv_w, conv_b, conv_b)


def _ffn_act_bwd(u, conv_w, conv_b, da):
    T, F2 = u.shape
    F = F2 // 2
    tc = _tile(F, 512, LANES)
    tr = _tile(T, 512, SUBLANES)
    nc = F // tc
    r8 = tr // SUBLANES
    n8 = T // SUBLANES
    K = FFN_CONV

    def body(ug_ref, ugp_ref, ugn_ref, uv_ref, uvp_ref, uvn_ref, da_ref, dan_ref,
             wg_ref, wv_ref, bg_ref, bv_ref, du_ref, dwb_ref):
        i = pl.program_id(1)
        first = i == 0
        last = i == pl.num_programs(1) - 1

        def dconv(ug, ug_prev, uv, uv_prev, da_):
            cg = _causal_conv(ug, ug_prev, wg_ref, K) + bg_ref[...]
            cv = _causal_conv(uv, uv_prev, wv_ref, K) + bv_ref[...]
            act, act_grad = _gelu_and_grad(cg)
            return da_ * cv * act_grad, da_ * act

        ug, uv = ug_ref[...], uv_ref[...]
        ug_prev, uv_prev = jnp.where(first, 0.0, ugp_ref[...]), jnp.where(first, 0.0, uvp_ref[...])
        dcg, dcv = dconv(ug, ug_prev, uv, uv_prev, da_ref[...])
        dcgn, dcvn = dconv(ugn_ref[...], ug[tr - SUBLANES:], uvn_ref[...], uv[tr - SUBLANES:], dan_ref[...])
        du_ref[0] = _anti_conv(dcg, jnp.where(last, 0.0, dcgn), wg_ref, K).astype(BF16)
        du_ref[1] = _anti_conv(dcv, jnp.where(last, 0.0, dcvn), wv_ref, K).astype(BF16)

        @pl.when(first)
        def _():
            dwb_ref[...] = jnp.zeros_like(dwb_ref)

        for half, (dc, uo, uo_prev) in enumerate([(dcg, ug, ug_prev), (dcv, uv, uv_prev)]):
            rows = [jnp.sum(dc * _shift_down(uo, uo_prev, K - 1 - t), axis=0, keepdims=True) for t in range(K - 1)]
            rows += [jnp.sum(dc * uo, axis=0, keepdims=True), jnp.sum(dc, axis=0, keepdims=True)]
            rows += [jnp.zeros_like(rows[0])] * (SUBLANES - len(rows))
            dwb_ref[half] += jnp.concatenate(rows, axis=0)

    cur = lambda off: pl.BlockSpec((tr, tc), lambda j, i: (i, j + off))
    prev = lambda off: pl.BlockSpec((SUBLANES, tc), lambda j, i: (jnp.maximum(i * r8 - 1, 0), j + off))
    nxt = lambda off: pl.BlockSpec((SUBLANES, tc), lambda j, i: (jnp.minimum((i + 1) * r8, n8 - 1), j + off))
    wsp = lambda off: pl.BlockSpec((K, tc), lambda j, i: (0, j + off))
    bsp = lambda off: pl.BlockSpec((1, tc), lambda j, i: (0, j + off))
    return pl.pallas_call(
        body, name="ffn_act_bwd", grid=(nc, T // tr),
        in_specs=[cur(0), prev(0), nxt(0), cur(nc), prev(nc), nxt(nc), cur(0), nxt(0), wsp(0), wsp(nc), bsp(0), bsp(nc)],
        out_specs=[pl.BlockSpec((2, tr, tc), lambda j, i: (0, i, j)), pl.BlockSpec((2, SUBLANES, tc), lambda j, i: (0, 0, j))],
        out_shape=[_sds((2, T, F), BF16), _sds((2, SUBLANES, F), F32)],
        compiler_params=_params(("parallel", "arbitrary"), 24 * _nbytes((tr, tc), F32) + 8 * 2**20),
    )(u, u, u, u, u, u, da, da, conv_w, conv_w, conv_b, conv_b)


def _l2norm(s, scale):
    return s * (lax.rsqrt(jnp.sum(s * s, axis=-1, keepdims=True) + EPS) * scale)


def _dn_branch(proj, col0, conv_w, wcol0, l2, scale):
    T = proj.shape[0]
    W = DN_HEADS * HEAD_DIM
    tr = _tile(T, 2048, SUBLANES)
    r8 = tr // SUBLANES
    cb0, wb0 = col0 // HEAD_DIM, wcol0 // HEAD_DIM

    def body(u_ref, up_ref, w_ref, o_ref):
        first = pl.program_id(1) == 0
        s = _silu(_causal_conv(u_ref[...], jnp.where(first, 0.0, up_ref[...]), w_ref, SHORT_CONV))
        o_ref[...] = _l2norm(s, scale) if l2 else s

    return pl.pallas_call(
        body, name=f"dn_branch_{col0}", grid=(DN_HEADS, T // tr),
        in_specs=[pl.BlockSpec((tr, HEAD_DIM), lambda h, i: (i, cb0 + h)),
                  pl.BlockSpec((SUBLANES, HEAD_DIM), lambda h, i: (jnp.maximum(i * r8 - 1, 0), cb0 + h)),
                  pl.BlockSpec((SHORT_CONV, HEAD_DIM), lambda h, i: (0, wb0 + h))],
        out_specs=pl.BlockSpec((tr, HEAD_DIM), lambda h, i: (i, h)),
        out_shape=_sds((T, W), F32),
        compiler_params=_params(("parallel", "arbitrary"), 32 * _nbytes((tr, HEAD_DIM), F32) + 8 * 2**20),
    )(proj, proj, conv_w)


def _dn_branch_bwd(proj, col0, conv_w, wcol0, l2, scale, dy):
    T = proj.shape[0]
    W = DN_HEADS * HEAD_DIM
    tr = _tile(T, 2048, SUBLANES)
    r8 = tr // SUBLANES
    n8 = T // SUBLANES
    cb0, wb0 = col0 // HEAD_DIM, wcol0 // HEAD_DIM
    K = SHORT_CONV

    def body(u_ref, up_ref, un_ref, dy_ref, dyn_ref, w_ref, du_ref, dw_ref):
        i = pl.program_id(1)
        first = i == 0
        last = i == pl.num_programs(1) - 1

        def dconv(u, u_prev, dy_):
            c = _causal_conv(u, u_prev, w_ref, K)
            if l2:
                s = _silu(c)
                r = lax.rsqrt(jnp.sum(s * s, axis=-1, keepdims=True) + EPS)
                n = s * r
                ds = (scale * r) * (dy_ - n * jnp.sum(dy_ * n, axis=-1, keepdims=True))
            else:
                ds = dy_
            return ds * _silu_grad(c)

        u = u_ref[...]
        u_prev = jnp.where(first, 0.0, up_ref[...])
        dc = dconv(u, u_prev, dy_ref[...])
        dcn = jnp.where(last, 0.0, dconv(un_ref[...], u[tr - SUBLANES:], dyn_ref[...]))
        du_ref[...] = _anti_conv(dc, dcn, w_ref, K).astype(BF16)
        rows = [jnp.sum(dc * _shift_down(u, u_prev, K - 1 - t), axis=0, keepdims=True) for t in range(K - 1)]
        rows += [jnp.sum(dc * u, axis=0, keepdims=True)]
        rows += [jnp.zeros_like(rows[0])] * (SUBLANES - len(rows))
        upd = jnp.concatenate(rows, axis=0)

        @pl.when(first)
        def _():
            dw_ref[...] = jnp.zeros_like(dw_ref)
        dw_ref[...] += upd

    return pl.pallas_call(
        body, name=f"dn_branch_bwd_{col0}", grid=(DN_HEADS, T // tr),
        in_specs=[pl.BlockSpec((tr, HEAD_DIM), lambda h, i: (i, cb0 + h)),
                  pl.BlockSpec((SUBLANES, HEAD_DIM), lambda h, i: (jnp.maximum(i * r8 - 1, 0), cb0 + h)),
                  pl.BlockSpec((SUBLANES, HEAD_DIM), lambda h, i: (jnp.minimum((i + 1) * r8, n8 - 1), cb0 + h)),
                  pl.BlockSpec((tr, HEAD_DIM), lambda h, i: (i, h)),
                  pl.BlockSpec((SUBLANES, HEAD_DIM), lambda h, i: (jnp.minimum((i + 1) * r8, n8 - 1), h)),
                  pl.BlockSpec((K, HEAD_DIM), lambda h, i: (0, wb0 + h))],
        out_specs=[pl.BlockSpec((tr, HEAD_DIM), lambda h, i: (i, h)),
                   pl.BlockSpec((SUBLANES, HEAD_DIM), lambda h, i: (0, h))],
        out_shape=[_sds((T, W), BF16), _sds((SUBLANES, W), F32)],
        compiler_params=_params(("parallel", "arbitrary"), 32 * _nbytes((tr, HEAD_DIM), F32) + 8 * 2**20),
    )(proj, proj, proj, dy, dy, conv_w)


def _lane_masks(shape):
    lane = _iota2(shape, 1)
    return lane < DN_HEADS, (lane >= DN_HEADS) & (lane < 2 * DN_HEADS)


def _expand01(off):
    r = _iota2((LANES, DN_HEADS * HEAD_DIM), 0)
    c = _iota2((LANES, DN_HEADS * HEAD_DIM), 1)
    return (r == jnp.right_shift(c, int(math.log2(HEAD_DIM))) + off).astype(BF16)


def _select01(off):
    r = _iota2((DN_HEADS * HEAD_DIM, LANES), 0)
    c = _iota2((DN_HEADS * HEAD_DIM, LANES), 1)
    return (r == (c - off) * HEAD_DIM).astype(BF16)


def _dn_gates(proj, gate_block, a_log_l, dt_bias_l):
    T = proj.shape[0]
    C = DN_CHUNK
    W = DN_HEADS * HEAD_DIM

    def body(ba_ref, al_ref, dt_ref, gc_ref, beta_ref):
        ba = ba_ref[...]
        is_b, is_a = _lane_masks(ba.shape)
        g = jnp.where(is_a, -jnp.exp(al_ref[...]) * _softplus(ba + dt_ref[...]), 0.0)
        beta = jnp.where(is_b, _sigmoid(ba), 0.0)
        tri = (_iota2((C, C), 0) >= _iota2((C, C), 1)).astype(BF16)
        gc = _dot01_left(tri, g)
        gc_ref[...] = _dot01(gc, _expand01(DN_HEADS))
        beta_ref[...] = _dot01(beta, _expand01(0))

    vec = pl.BlockSpec((1, LANES), lambda n: (0, 0))
    return pl.pallas_call(
        body, name="dn_gates", grid=(T // C,),
        in_specs=[pl.BlockSpec((C, LANES), lambda n: (n, gate_block)), vec, vec],
        out_specs=[pl.BlockSpec((C, W), lambda n: (n, 0))] * 2,
        out_shape=[_sds((T, W), F32)] * 2,
        compiler_params=_params(("parallel",), 16 * 2**20),
    )(proj, a_log_l, dt_bias_l)


def _dn_gates_bwd(proj, gate_block, a_log_l, dt_bias_l, dgc_full, dbeta_full):
    T = proj.shape[0]
    C = DN_CHUNK
    W = DN_HEADS * HEAD_DIM

    def body(ba_ref, al_ref, dt_ref, dgc_ref, dbeta_ref, dba_ref, dal_ref, ddt_ref):
        ba = ba_ref[...]
        is_b, is_a = _lane_masks(ba.shape)
        ea = jnp.exp(al_ref[...])
        pre = ba + dt_ref[...]
        g = jnp.where(is_a, -ea * _softplus(pre), 0.0)
        beta = _sigmoid(ba)
        dgc = _dot01(dgc_ref[...], _select01(DN_HEADS))
        dbeta = _dot01(dbeta_ref[...], _select01(0))
        triu = (_iota2((C, C), 0) <= _iota2((C, C), 1)).astype(BF16)
        dg = _dot01_left(triu, dgc)
        da = jnp.where(is_a, dg * (-ea) * _sigmoid(pre), 0.0)
        dba_ref[...] = (da + jnp.where(is_b, dbeta * beta * (1.0 - beta), 0.0)).astype(BF16)
        _accumulate(dal_ref, jnp.sum(dg * g, axis=0, keepdims=True))
        _accumulate(ddt_ref, jnp.sum(da, axis=0, keepdims=True))

    vec = pl.BlockSpec((1, LANES), lambda n: (0, 0))
    full = pl.BlockSpec((C, W), lambda n: (n, 0))
    return pl.pallas_call(
        body, name="dn_gates_bwd", grid=(T // C,),
        in_specs=[pl.BlockSpec((C, LANES), lambda n: (n, gate_block)), vec, vec, full, full],
        out_specs=[pl.BlockSpec((C, LANES), lambda n: (n, 0)), vec, vec],
        out_shape=[_sds((T, LANES), BF16), _sds((1, LANES), F32), _sds((1, LANES), F32)],
        compiler_params=_params(("arbitrary",), 16 * 2**20),
    )(proj, a_log_l, dt_bias_l, dgc_full, dbeta_full)


def _unit_lower_inverse(L):
    C = L.shape[-1]
    row, col = _iota2((C, C), 0), _iota2((C, C), 1)
    eye = (row == col).astype(F32)
    sh = int(math.log2(INV_BLOCK))
    Ld = jnp.where(jnp.right_shift(row, sh) == jnp.right_shift(col, sh), L, 0.0)
    Lo = L - Ld
    X = eye - Ld
    P = Ld
    for _ in range(int(math.log2(INV_BLOCK)) - 1):
        P = _mm3(P, P)
        X = X + _mm3(X, P)
    N = _mm3(X, Lo)
    Y = eye - N
    P = N
    for _ in range(int(math.log2(C // INV_BLOCK)) - 1):
        P = _mm3(P, P)
        Y = Y + _mm3(Y, P)
    return _mm3(Y, X)


def _dn_chunk_common(q, k, v, gc, beta, gl):
    C = q.shape[-2]
    row, col = _iota2((C, C), 0), _iota2((C, C), 1)
    causal, strict = row >= col, row > col
    eg = jnp.exp(gc)
    decay = jnp.where(causal, jnp.exp(jnp.where(causal, gc - _t(gc), 0.0)), 0.0)
    kb, vb = k * beta, v * beta
    L = jnp.where(strict, _dot(kb, k, _NT) * decay, 0.0)
    Aqk = jnp.where(causal, _dot(q, k, _NT) * decay, 0.0)
    ektg = jnp.exp(gl - gc)
    return dict(causal=causal, strict=strict, eg=eg, decay=decay, kb=kb, vb=vb, L=L, Aqk=Aqk, ektg=ektg,
                kbg=kb * eg, kte=k * ektg, qd=q * eg, egl=jnp.exp(gl))


def _dn_scan(qn, kn, vn, gc_full, beta_full, proj, z_col0, gain):
    T, W = qn.shape
    C = DN_CHUNK
    N = T // C
    H = DN_HEADS
    G = DN_GROUP
    GW = G * HEAD_DIM
    zb0 = z_col0 // GW

    def body(q_ref, k_ref, v_ref, gc_ref, beta_ref, z_ref, gain_ref, o_ref, mix_ref, tm_ref, s_ref, S):
        @pl.when(pl.program_id(1) == 0)
        def _():
            S[...] = jnp.zeros_like(S)

        heads = lambda ref, rows=slice(None): jnp.stack([ref[rows, g * HEAD_DIM:(g + 1) * HEAD_DIM] for g in range(G)])
        q, k, v, gc, beta = heads(q_ref), heads(k_ref), heads(v_ref), heads(gc_ref), heads(beta_ref)
        gl = heads(gc_ref, slice(C - 1, C))
        c = _dn_chunk_common(q, k, v, gc, beta, gl)
        Tm = _unit_lower_inverse(c["L"])
        u = _dot(Tm, c["vb"])
        w = _dot(Tm, c["kbg"])
        S0 = S[...]
        vnew = u - _dot(w, S0)
        o = _dot(c["qd"], S0) + _dot(c["Aqk"], vnew)
        S[...] = S0 * c["egl"] + _dot(c["kte"], vnew, _TN)
        tm_ref[...] = Tm
        s_ref[...] = S0
        mix = (_rms(o, gain_ref[...]) * _silu(heads(z_ref))).astype(BF16)
        for g in range(G):
            sl = slice(g * HEAD_DIM, (g + 1) * HEAD_DIM)
            o_ref[:, sl] = o[g]
            mix_ref[:, sl] = mix[g]

    blk = pl.BlockSpec((C, GW), lambda h, n: (n, h))
    mat = pl.BlockSpec((G, None, C, C), lambda h, n: (h, n, 0, 0))
    return pl.pallas_call(
        body, name="dn_scan", grid=(H // G, N),
        in_specs=[blk, blk, blk, blk, blk, pl.BlockSpec((C, GW), lambda h, n: (n, zb0 + h)),
                  pl.BlockSpec((1, HEAD_DIM), lambda h, n: (0, 0))],
        out_specs=[blk, blk, mat, mat],
        out_shape=[_sds((T, W), F32), _sds((T, W), BF16), _sds((H, N, C, C), F32), _sds((H, N, C, C), F32)],
        scratch_shapes=[pltpu.VMEM((G, HEAD_DIM, HEAD_DIM), F32)],
        compiler_params=_params(("parallel", "arbitrary"), 32 * 2**20),
    )(qn, kn, vn, gc_full, beta_full, proj, gain)


def _dn_scan_bwd(qn, kn, vn, gc_full, beta_full, proj, z_col0, gain, o_raw, tm_all, s_all, dmix, dmix_col0):
    T, W = qn.shape
    C = DN_CHUNK
    N = T // C
    H = DN_HEADS
    G = DN_GROUP
    GW = G * HEAD_DIM
    zb0 = z_col0 // GW
    mb0 = dmix_col0 // GW

    def body(q_ref, k_ref, v_ref, gc_ref, beta_ref, z_ref, gain_ref, o_ref, tm_ref, s_ref, dmix_ref,
             dq_ref, dk_ref, dv_ref, dgc_ref, dbeta_ref, dz_ref, dgain_ref, dS):
        @pl.when(pl.program_id(1) == 0)
        def _():
            dS[...] = jnp.zeros_like(dS)

        @pl.when((pl.program_id(0) == 0) & (pl.program_id(1) == 0))
        def _():
            dgain_ref[...] = jnp.zeros_like(dgain_ref)

        heads = lambda ref, rows=slice(None): jnp.stack([ref[rows, g * HEAD_DIM:(g + 1) * HEAD_DIM] for g in range(G)])
        total = lambda x: jnp.sum(jnp.sum(x, axis=-1, keepdims=True), axis=-2, keepdims=True)
        gain = gain_ref[...]
        o, z, dmix = heads(o_ref), heads(z_ref), heads(dmix_ref)
        dz = (dmix * _rms(o, gain) * _silu_grad(z)).astype(BF16)
        do, dgain = _rms_bwd(dmix * _silu(z), o, gain)
        dgain_ref[...] += jnp.sum(dgain, axis=0)

        q, k, v, gc, beta = heads(q_ref), heads(k_ref), heads(v_ref), heads(gc_ref), heads(beta_ref)
        gl = heads(gc_ref, slice(C - 1, C))
        c = _dn_chunk_common(q, k, v, gc, beta, gl)
        Tm, S0, dS1 = tm_ref[...], s_ref[...], dS[...]
        w = _dot(Tm, c["kbg"])
        vnew = _dot(Tm, c["vb"]) - _dot(w, S0)

        dvnew = _dot(c["Aqk"], do, _TN) + _dot(c["kte"], dS1)
        dAqk = jnp.where(c["causal"], _dot(do, vnew, _NT), 0.0)
        dqd = _dot(do, S0, _NT)
        dkte = _dot(vnew, dS1, _NT)
        dgl = total(dS1 * S0) * c["egl"]
        dw = -_dot(dvnew, S0, _NT)
        dS[...] = dS1 * c["egl"] + _dot(c["qd"], do, _TN) - _dot(w, dvnew, _TN)

        dTm = _dot(dvnew, c["vb"], _NT) + _dot(dw, c["kbg"], _NT)
        dvb = _dot(Tm, dvnew, _TN)
        dkbg = _dot(Tm, dw, _TN)
        dL = jnp.where(c["strict"], -_mm3(_mm3(Tm, dTm, _TN), Tm, _NT), 0.0)
        dP = dL * c["decay"]
        dQ = dAqk * c["decay"]
        M = dL * c["L"] + dAqk * c["Aqk"]
        dkb = _dot(dP, k) + dkbg * c["eg"]
        dk = _dot(dP, c["kb"], _TN) + _dot(dQ, q, _TN) + dkte * c["ektg"] + dkb * beta
        dq = _dot(dQ, k) + dqd * c["eg"]
        tk = _rowsum(dkte * c["kte"])
        dgc = (_rowsum(M) - _rowsum(_t(M)) + _rowsum(dqd * c["qd"]) - tk + _rowsum(dkbg * c["kbg"]))
        dgl = dgl + total(tk)
        dgc = jnp.broadcast_to(dgc, q.shape) + jnp.where(_iota2((C, HEAD_DIM), 0) == C - 1, dgl, 0.0)
        dv = dvb * beta
        dbeta = jnp.broadcast_to(_rowsum(dkb * k) + _rowsum(dvb * v), q.shape)
        for g in range(G):
            sl = slice(g * HEAD_DIM, (g + 1) * HEAD_DIM)
            dz_ref[:, sl] = dz[g]
            dq_ref[:, sl] = dq[g]
            dk_ref[:, sl] = dk[g]
            dv_ref[:, sl] = dv[g]
            dgc_ref[:, sl] = dgc[g]
            dbeta_ref[:, sl] = dbeta[g]

    rev = lambda off: pl.BlockSpec((C, GW), lambda h, n: (N - 1 - n, off + h))
    mat = pl.BlockSpec((G, None, C, C), lambda h, n: (h, N - 1 - n, 0, 0))
    vec = pl.BlockSpec((1, HEAD_DIM), lambda h, n: (0, 0))
    return pl.pallas_call(
        body, name="dn_scan_bwd", grid=(H // G, N),
        in_specs=[rev(0), rev(0), rev(0), rev(0), rev(0), rev(zb0), vec, rev(0), mat, mat, rev(mb0)],
        out_specs=[rev(0)] * 6 + [vec],
        out_shape=[_sds((T, W), F32)] * 5 + [_sds((T, W), BF16), _sds((1, HEAD_DIM), F32)],
        scratch_shapes=[pltpu.VMEM((G, HEAD_DIM, HEAD_DIM), F32)],
        compiler_params=_params(("arbitrary", "arbitrary"), 40 * 2**20),
    )(qn, kn, vn, gc_full, beta_full, proj, gain, o_raw, tm_all, s_all, dmix)


def _sb_terms(z, ahead, first_key):
    lb = jnp.minimum(z, 0.0) - jnp.log(1.0 + jnp.exp(-jnp.abs(z)))
    if ahead is None:
        return None, lb, lb - z
    valid = ahead < -first_key
    return valid, lb, jnp.where(valid, lb - z, 0.0)


def _masked(valid, x):
    return x if valid is None else jnp.where(valid, x, 0.0)


def _sb_attention(qkv, gain, tq_cap=1024):
    T = qkv.shape[0]
    H = SB_HEADS
    B = min(SB_KEYS, T)
    TQ = _tile(T, tq_cap, B)
    per = TQ // B

    assert per % 2 == 0
    n_saved = per * (T // TQ) * (T // TQ + 1) // 2

    def body(q_ref, k_ref, v_ref, gain_ref, o_ref, mix_ref, att_hbm, lb_hbm, att_buf, lb_buf, sems):
        h, i = pl.program_id(0), pl.program_id(1)
        q = q_ref[...]
        upper = (_iota2((B, B), 0) > _iota2((B, B), 1)).astype(BF16)
        ahead = _iota2((TQ, B), 1) - _iota2((TQ, B), 0)
        last = (i + 1) * per - 1
        base = per * (i * (i + 1) // 2)

        def save(slot, pair):
            return (pltpu.make_async_copy(att_buf.at[slot], att_hbm.at[h, pair], sems.at[0, slot]),
                    pltpu.make_async_copy(lb_buf.at[slot], lb_hbm.at[h, pair], sems.at[1, slot]))

        def pair(j, slot, r0, acc, R):
            n = TQ - (r0 or 0)
            top = slice(r0 or 0, TQ)
            rows = pl.ds(pl.multiple_of(j * B, B), B)
            z = _dot(q[top], k_ref[rows, :], _NT) * (HEAD_DIM ** -0.5)
            valid, lb, l1m = _sb_terms(z, None if r0 is None else ahead[top], r0)
            att = _masked(valid, jnp.exp(lb + R[top] + _dot01(l1m, upper, passes=2))).astype(BF16)
            att_buf[slot, pl.ds(TQ - n, n), :] = att
            lb_buf[slot, pl.ds(TQ - n, n), :] = (lb if valid is None else jnp.where(valid, lb, -1e30)).astype(BF16)
            if r0:
                att_buf[slot, pl.ds(0, r0), :] = jnp.zeros((r0, B), BF16)
                lb_buf[slot, pl.ds(0, r0), :] = jnp.full((r0, B), -1e30, BF16)
            for c in save(slot, base + j):
                c.start()
            d_acc, d_R = _dot(att, v_ref[rows, :]), _rowsum(l1m)
            if r0:
                d_acc = jnp.concatenate([jnp.zeros((r0, HEAD_DIM), F32), d_acc], axis=0)
                d_R = jnp.concatenate([jnp.zeros((r0, 1), F32), d_R], axis=0)
            return acc + d_acc, R + d_R

        def step(jj, carry):
            slot = jj % 2

            @pl.when(jj >= 2)
            def _():
                for c in save(slot, 0):
                    c.wait()

            return pair(last - jj, slot, None, *carry)

        carry = (jnp.zeros((TQ, HEAD_DIM), F32), jnp.zeros((TQ, 1), F32))
        for jj in range(per):
            if jj >= 2:
                for c in save(jj % 2, 0):
                    c.wait()
            carry = pair(last - jj, jj % 2, (per - 1 - jj) * B, *carry)
        acc, _ = lax.fori_loop(per, last + 1, step, carry)
        for slot in range(2):
            for c in save(slot, 0):
                c.wait()
        o_ref[...] = acc
        mix_ref[...] = _rms(acc, gain_ref[...]).astype(BF16)

    head = lambda off: pl.BlockSpec((T, HEAD_DIM), lambda h, i: (0, off + h))
    blk = pl.BlockSpec((TQ, HEAD_DIM), lambda h, i: (i, h))
    return pl.pallas_call(
        body, name="sb_attention", grid=(H, T // TQ),
        in_specs=[blk, head(H), head(2 * H), pl.BlockSpec((1, HEAD_DIM), lambda h, i: (0, 0))],
        out_specs=[blk, blk, ANY_SPEC, ANY_SPEC],
        out_shape=[_sds((T, H * HEAD_DIM), F32), _sds((T, H * HEAD_DIM), BF16),
                   _sds((H, n_saved, TQ, B), BF16), _sds((H, n_saved, TQ, B), BF16)],
        scratch_shapes=[pltpu.VMEM((2, TQ, B), BF16), pltpu.VMEM((2, TQ, B), BF16), pltpu.SemaphoreType.DMA((2, 2))],
        compiler_params=_params(("parallel", "arbitrary"), 8 * _nbytes((T, HEAD_DIM), BF16) + 32 * _nbytes((TQ, B), F32)),
    )(qkv, qkv, qkv, gain)


def _sb_attention_bwd(qkv, gain, o_raw, att_all, lb_all, dmix):
    T = qkv.shape[0]
    H = SB_HEADS
    TQ, B = att_all.shape[2:]
    per = TQ // B
    scale = HEAD_DIM ** -0.5

    def body(q_ref, k_ref, v_ref, gain_ref, o_ref, dmix_ref, att_hbm, lb_hbm, dq_ref, dk_ref, dv_ref, dgain_ref,
             att_buf, lb_buf, sems):
        h, i = pl.program_id(0), pl.program_id(1)

        @pl.when(i == 0)
        def _():
            dk_ref[...] = jnp.zeros_like(dk_ref)
            dv_ref[...] = jnp.zeros_like(dv_ref)

        @pl.when((pl.program_id(0) == 0) & (i == 0))
        def _():
            dgain_ref[...] = jnp.zeros_like(dgain_ref)

        q = q_ref[...]
        o = o_ref[...]
        do, dgain = _rms_bwd(dmix_ref[...], o, gain_ref[...])
        dgain_ref[...] += dgain
        do_b = do.astype(BF16)
        before = (_iota2((B, B), 0) < _iota2((B, B), 1)).astype(BF16)
        base = per * (i * (i + 1) // 2)

        def fetch(slot, pair):
            return (pltpu.make_async_copy(att_hbm.at[h, pair], att_buf.at[slot], sems.at[0, slot]),
                    pltpu.make_async_copy(lb_hbm.at[h, pair], lb_buf.at[slot], sems.at[1, slot]))

        for c in fetch(0, base):
            c.start()

        def pair(j, slot, r0, dq, PG):
            top = slice(r0, TQ)
            rows = pl.ds(pl.multiple_of(j * B, B), B)
            kj = k_ref[rows, :]
            att = att_buf[slot, pl.ds(r0, TQ - r0), :]
            sig = jnp.exp(lb_buf[slot, pl.ds(r0, TQ - r0), :].astype(F32))
            G = _dot(do_b[top], v_ref[rows, :], _NT) * att.astype(F32)
            dv_ref[rows, :] += _dot(att, do_b[top], _TN)
            cum = PG[top] + _dot01(G, before, passes=2)
            dz = (G * (1.0 - sig) - sig * cum) * scale
            dk_ref[rows, :] += _dot(dz, q[top], _TN)
            d_dq, d_PG = _dot(dz, kj), _rowsum(G)
            if r0:
                d_dq = jnp.concatenate([jnp.zeros((r0, HEAD_DIM), F32), d_dq], axis=0)
                d_PG = jnp.concatenate([jnp.zeros((r0, 1), F32), d_PG], axis=0)
            return dq + d_dq, PG + d_PG

        def step(j, carry):
            slot = j % 2
            for c in fetch(slot, 0):
                c.wait()
            for c in fetch(1 - slot, base + j + 1):
                c.start()
            return pair(j, slot, 0, *carry)

        carry = lax.fori_loop(0, i * per, step, (jnp.zeros((TQ, HEAD_DIM), F32), jnp.zeros((TQ, 1), F32)))
        for c_blk in range(per):
            slot = c_blk % 2
            for c in fetch(slot, 0):
                c.wait()
            if c_blk + 1 < per:
                for c in fetch(1 - slot, base + i * per + c_blk + 1):
                    c.start()
            carry = pair(i * per + c_blk, slot, c_blk * B, *carry)
        dq_ref[...] = carry[0].astype(BF16)

    head = lambda off: pl.BlockSpec((T, HEAD_DIM), lambda h, i: (0, off + h))
    blk = pl.BlockSpec((TQ, HEAD_DIM), lambda h, i: (i, h))
    vec = pl.BlockSpec((1, HEAD_DIM), lambda h, i: (0, 0))
    return pl.pallas_call(
        body, name="sb_attention_bwd", grid=(H, T // TQ),
        in_specs=[blk, head(H), head(2 * H), vec, blk, blk, ANY_SPEC, ANY_SPEC],
        out_specs=[blk, head(0), head(0), vec],
        out_shape=[_sds((T, H * HEAD_DIM), BF16), _sds((T, H * HEAD_DIM), F32), _sds((T, H * HEAD_DIM), F32),
                   _sds((1, HEAD_DIM), F32)],
        scratch_shapes=[pltpu.VMEM((2, TQ, B), BF16), pltpu.VMEM((2, TQ, B), BF16), pltpu.SemaphoreType.DMA((2, 2))],
        compiler_params=_params(("arbitrary", "arbitrary"), 8 * _nbytes((T, HEAD_DIM), F32) + 32 * _nbytes((TQ, B), F32)),
    )(qkv, qkv, qkv, gain, o_raw, dmix, att_all, lb_all)


def _adamw_math(w, g, m, v):
    m = ADAM_B1 * m + (1.0 - ADAM_B1) * g
    v = ADAM_B2 * v + (1.0 - ADAM_B2) * (g * g)
    m_hat = m / (1.0 - ADAM_B1 ** ADAM_STEP)
    v_hat = v / (1.0 - ADAM_B2 ** ADAM_STEP)
    delta = -ADAM_LR * (m_hat / (jnp.sqrt(v_hat) + ADAM_EPS) + ADAM_WD * w)
    return delta, m, v


def _adamw_sharded(parts, w, m, v, name):
    _, R, C = w.shape
    if R % SUBLANES == 0:
        tr, tc = _tile(R, max(SUBLANES, (2**20 // (4 * C)) // SUBLANES * SUBLANES), SUBLANES), C
    else:
        tr, tc = R, _tile(C, max(LANES, (2**20 // (4 * R)) // LANES * LANES), LANES)

    def body(p_ref, w_ref, m_ref, v_ref, g_ref, d_ref, nm_ref, nv_ref):
        g = p_ref[0].astype(F32)
        for d in range(1, N_DEV):
            g = g + p_ref[d].astype(F32)
        g_ref[...] = g
        d_ref[...], nm_ref[...], nv_ref[...] = _adamw_math(w_ref[...], g, m_ref[...], v_ref[...])

    blk = pl.BlockSpec((None, tr, tc), lambda i, j: (0, i, j))
    return pl.pallas_call(
        body, name=name, grid=(R // tr, C // tc),
        in_specs=[pl.BlockSpec((N_DEV, tr, tc), lambda i, j: (0, i, j)), blk, blk, blk],
        out_specs=[blk] * 4, out_shape=[_sds((1, R, C), F32)] * 4,
        compiler_params=_params(("parallel", "parallel"), 40 * 2**20),
    )(parts, w, m, v)


def _adamw_packed(g, w, m, v):
    def body(g_ref, w_ref, m_ref, v_ref, d_ref, nm_ref, nv_ref):
        d_ref[...], nm_ref[...], nv_ref[...] = _adamw_math(w_ref[...], g_ref[...], m_ref[...], v_ref[...])

    return pl.pallas_call(body, name="adamw_packed", out_shape=[_sds(g.shape, F32)] * 3,
                          compiler_params=_params((), 16 * 2**20))(g, w, m, v)


def _my_place():
    x, y, c = lax.axis_index("x"), lax.axis_index("y"), lax.axis_index("c")
    return x, y, c


def _peer(place, k):
    x, y, c = place
    return (1 - x if k & 4 else x, 1 - y if k & 2 else y, 1 - c if k & 1 else c)


def _index(place):
    x, y, c = place
    return 4 * x + 2 * y + c


HBM_SPEC = pl.BlockSpec(memory_space=pltpu.HBM)


def _all_gather(block, name):
    R, C = block.shape

    def body(x_ref, out_ref, send_sems, recv_sems, local_sem):
        me = _my_place()
        sibling = _peer(me, 1)
        chips = [2, 4, 6]

        def copy(sem, origin, to, src=None):
            slot = out_ref.at[_index(origin)]
            return pltpu.make_async_remote_copy(
                src_ref=slot if src is None else src, dst_ref=slot, send_sem=send_sems.at[sem], recv_sem=recv_sems.at[sem],
                device_id=to, device_id_type=MESH)

        mine = pltpu.make_async_copy(x_ref, out_ref.at[_index(me)], local_sem)
        mine.start()
        first = [copy(0, me, sibling, src=x_ref)] + [copy(1 + n, me, _peer(me, k), src=x_ref) for n, k in enumerate(chips)]
        for cp in first:
            cp.start()
        passed = [copy(4 + n, _peer(me, k), sibling) for n, k in enumerate(chips)]
        for n, k in enumerate(chips):
            copy(1 + n, _peer(me, k), me).wait_recv()
            passed[n].start()
        copy(0, sibling, me).wait_recv()
        for n, k in enumerate(chips):
            copy(4 + n, _peer(sibling, k), me).wait_recv()
        for cp in first + passed:
            cp.wait_send()
        mine.wait()

    return pl.pallas_call(
        body, name=name, in_specs=[HBM_SPEC], out_specs=HBM_SPEC,
        out_shape=_sds((N_DEV, R, C), block.dtype),
        scratch_shapes=[pltpu.SemaphoreType.DMA((7,)), pltpu.SemaphoreType.DMA((7,)), pltpu.SemaphoreType.DMA],
    )(block)


SEM_SPEC = pl.BlockSpec(memory_space=pltpu.SEMAPHORE)
ANY_SPEC = pl.BlockSpec(memory_space=pl.ANY)
_EFFECT = pltpu.SideEffectType.DATAFLOW_SIDE_EFFECTING


def _spread_start(x, per_peer, name, after):
    R, C = x.shape[-2:]

    def body(x_ref, land_ref, after_ref, send_sems, recv_sems, x_thru, land_thru, token):
        me = _my_place()
        for k in range(1, N_DEV):
            to = _peer(me, k)
            pltpu.make_async_remote_copy(
                src_ref=x_ref.at[_index(to)] if per_peer else x_ref, dst_ref=land_ref.at[_index(me)],
                send_sem=send_sems.at[k - 1], recv_sem=recv_sems.at[k - 1], device_id=to, device_id_type=MESH).start()
        token[...] = jnp.zeros_like(token)

    land = lax.empty((N_DEV, R, C), x.dtype)
    send_sems, recv_sems, x_thru, land_thru, token = pl.pallas_call(
        body, name=name,
        out_shape=(pltpu.SemaphoreType.DMA((N_DEV - 1,)), pltpu.SemaphoreType.DMA((N_DEV - 1,)),
                   pltpu.HBM(x.shape, x.dtype), pltpu.HBM(land.shape, land.dtype), _sds((SUBLANES, LANES), F32)),
        in_specs=(HBM_SPEC, HBM_SPEC, ANY_SPEC),
        out_specs=(SEM_SPEC, SEM_SPEC, HBM_SPEC, HBM_SPEC, pl.BlockSpec(memory_space=pltpu.VMEM)),
        input_output_aliases={0: 2, 1: 3},
        compiler_params=pltpu.CompilerParams(has_side_effects=_EFFECT),
    )(pltpu.with_memory_space_constraint(x, pltpu.HBM), pltpu.with_memory_space_constraint(land, pltpu.HBM), after)
    return (send_sems, recv_sems, x_thru, land_thru), token


def _spread_wait(state, per_peer, name, after):
    send_sems, recv_sems, x_thru, land_thru = state

    def body(x_ref, land_ref, send_sems, recv_sems, after_ref, x_dead, got_ref):
        me = _my_place()
        for k in range(1, N_DEV):
            frm = _peer(me, k)
            copy = pltpu.make_async_remote_copy(
                src_ref=x_ref.at[_index(frm)] if per_peer else x_ref, dst_ref=land_ref.at[_index(frm)],
                send_sem=send_sems.at[k - 1], recv_sem=recv_sems.at[k - 1], device_id=frm, device_id_type=MESH)
            copy.wait_send()
            copy.wait_recv()

    x_back, got = pl.pallas_call(
        body, name=name,
        out_shape=(pltpu.HBM(x_thru.shape, x_thru.dtype), pltpu.HBM(land_thru.shape, land_thru.dtype)),
        in_specs=(HBM_SPEC, HBM_SPEC, SEM_SPEC, SEM_SPEC, ANY_SPEC), out_specs=(HBM_SPEC, HBM_SPEC),
        input_output_aliases={0: 0, 1: 1},
        compiler_params=pltpu.CompilerParams(has_side_effects=_EFFECT),
    )(x_thru, land_thru, send_sems, recv_sems, after)
    me = _index(_my_place())
    own = lax.dynamic_index_in_dim(x_back, me, axis=0, keepdims=True) if per_peer else x_back[None]
    return lax.dynamic_update_slice_in_dim(got, own, me, axis=0)


def _all_reduce_packed(vec, after):
    R, L = vec.shape

    def body(x_ref, after_ref, out_ref, buf, send_sems, recv_sems):
        me = _my_place()
        buf[_index(me)] = x_ref[...]
        copies = []
        for k in range(1, N_DEV):
            to = _peer(me, k)
            cp = pltpu.make_async_remote_copy(
                src_ref=x_ref, dst_ref=buf.at[_index(me)],
                send_sem=send_sems.at[k - 1], recv_sem=recv_sems.at[k - 1], device_id=to, device_id_type=MESH)
            cp.start()
            copies.append(cp)
        for k in range(1, N_DEV):
            frm = _peer(me, k)
            pltpu.make_async_remote_copy(
                src_ref=x_ref, dst_ref=buf.at[_index(frm)],
                send_sem=send_sems.at[k - 1], recv_sem=recv_sems.at[k - 1], device_id=frm, device_id_type=MESH).wait_recv()
        for cp in copies:
            cp.wait_send()
        acc = buf[0]
        for d in range(1, N_DEV):
            acc = acc + buf[d]
        out_ref[...] = acc

    vm = pl.BlockSpec(memory_space=pltpu.VMEM)
    return pl.pallas_call(
        body, name="all_reduce_packed", in_specs=[vm, ANY_SPEC], out_specs=vm, out_shape=_sds((R, L), F32),
        scratch_shapes=[pltpu.VMEM((N_DEV, R, L), F32), pltpu.SemaphoreType.DMA((7,)), pltpu.SemaphoreType.DMA((7,))],
        compiler_params=pltpu.CompilerParams(vmem_limit_bytes=32 * 2**20),
    )(vec, after)


def _pack(arrays):
    rows = []
    for a in arrays:
        f = a.reshape(-1).astype(F32)
        pad = (-f.shape[0]) % LANES
        rows.append(jnp.pad(f, (0, pad)).reshape(-1, LANES))
    out = jnp.concatenate(rows, axis=0)
    return jnp.pad(out, ((0, (-out.shape[0]) % SUBLANES), (0, 0)))


def _unpack(packed, shapes):
    out, r = [], 0
    for s in shapes:
        n = math.prod(s)
        nr = -(-n // LANES)
        out.append(packed[r:r + nr].reshape(-1)[:n].reshape(s))
        r += nr
    return out


def _row_blocks(g):
    R, C = g.shape
    return g.astype(BF16).reshape(N_DEV, R // N_DEV, C)


def kernel(x, w_in, sb_out_gain, dn_conv_w, dn_a_log, dn_dt_bias, dn_out_gain, w_out, ln_mix_pre, ln_mix_post, w_up, ffn_conv_w, ffn_conv_b, w_down, ln_ffn_pre, ln_ffn_post, loss_target, m_w_in, m_sb_out_gain, m_dn_conv_w, m_dn_a_log, m_dn_dt_bias, m_dn_out_gain, m_w_out, m_ln_mix_pre, m_ln_mix_post, m_w_up, m_ffn_conv_w, m_ffn_conv_b, m_w_down, m_ln_ffn_pre, m_ln_ffn_post, v_w_in, v_sb_out_gain, v_dn_conv_w, v_dn_a_log, v_dn_dt_bias, v_dn_out_gain, v_w_out, v_ln_mix_pre, v_ln_mix_post, v_w_up, v_ffn_conv_w, v_ffn_conv_b, v_w_down, v_ln_ffn_pre, v_ln_ffn_post):
    T, D = x.shape[1], x.shape[2]
    SBW = SB_HEADS * HEAD_DIM
    DNW = DN_HEADS * HEAD_DIM
    in_cols = 3 * SBW + 4 * DNW + 2 * DN_HEADS
    main_cols = 3 * SBW + 4 * DNW
    in_pad = main_cols + LANES
    qkv0, z0 = 3 * SBW, 3 * SBW + 3 * DNW
    gate_block = main_cols // LANES
    x2, tgt = x[0], loss_target[0]

    g_in = _all_gather(jnp.swapaxes(w_in[0], 0, 1).astype(BF16), "gather_w_in")
    small_w = _all_gather(_pack([dn_conv_w[0], ffn_conv_w[0]]), "gather_conv_w")
    st_out, tok = _spread_start(w_out[0].astype(BF16), False, "gather_w_out_start", g_in)
    st_up, tok = _spread_start(w_up[0].astype(BF16), False, "gather_w_up_start", tok)
    st_down, tok_gather = _spread_start(w_down[0].astype(BF16), False, "gather_w_down_start", tok)
    w_in_t = jnp.pad(g_in.reshape(in_cols, D), ((0, in_pad - in_cols), (0, 0)))
    parts = [_unpack(small_w[d], [dn_conv_w.shape[1:], ffn_conv_w.shape[1:]]) for d in range(N_DEV)]
    dn_cw = jnp.concatenate([p[0] for p in parts], axis=1)
    ffn_cw = jnp.concatenate([p[1] for p in parts], axis=1)
    lane_pad = lambda a, off: jnp.pad(a, ((0, 0), (off, LANES - off - a.shape[1])))
    a_log_l, dt_bias_l = lane_pad(dn_a_log, DN_HEADS), lane_pad(dn_dt_bias, DN_HEADS)

    xn = _norm_in(x2, ln_mix_pre)
    proj = _matmul(xn, w_in_t, "nt", F32, "proj_in", tm_cap=512, tn_cap=2432, after=tok_gather)
    sb_qkv = proj[:, :3 * SBW].astype(BF16)
    o_sb, mix_sb, sb_att, sb_lb = _sb_attention(sb_qkv, sb_out_gain)
    qn = _dn_branch(proj, qkv0, dn_cw, 0, True, HEAD_DIM ** -0.5)
    kn = _dn_branch(proj, qkv0 + DNW, dn_cw, DNW, True, 1.0)
    vn = _dn_branch(proj, qkv0 + 2 * DNW, dn_cw, 2 * DNW, False, 1.0)
    gc_full, beta_full = _dn_gates(proj, gate_block, a_log_l, dt_bias_l)
    o_dn, mix_dn, tm_all, s_all = _dn_scan(qn, kn, vn, gc_full, beta_full, proj, z0, dn_out_gain)
    mix = jnp.concatenate([mix_sb, mix_dn], axis=1)
    w_out_f = _spread_wait(st_out, False, "gather_w_out_wait", mix).reshape(w_out.shape[1] * N_DEV, D)
    m = _matmul(mix, w_out_f, "nn", F32, "proj_out")
    h, hn = _mix_residual(x2, m, ln_mix_post, ln_ffn_pre)
    w_up_cut = _spread_wait(st_up, False, "gather_w_up_wait", hn)
    u = _matmul(hn, w_up_cut, "nn", F32, "ffn_up", tn_cap=w_up.shape[2], b_cut=True)
    act = _ffn_act(u, ffn_cw, ffn_conv_b)
    w_down_f = _spread_wait(st_down, False, "gather_w_down_wait", act).reshape(w_down.shape[1] * N_DEV, D)
    f = _matmul(act, w_down_f, "nn", F32, "ffn_down", tk_cap=2816)
    dy, df, d_ln_ffn_post, loss_part = _loss_head(h, f, ln_ffn_post, tgt)

    d_w_down = _matmul(act, df, "tn", BF16, "grad_w_down")
    st_xd, tok = _spread_start(_row_blocks(d_w_down), True, "exchange_w_down_start", loss_part)
    da = _matmul(df, w_down_f, "nt", F32, "bwd_ffn_down", after=tok)
    du, d_ffn_cwb = _ffn_act_bwd(u, ffn_cw, ffn_conv_b, da)
    d_ffn_cwb = jnp.concatenate([d_ffn_cwb[0], d_ffn_cwb[1]], axis=1)
    d_w_up_cut = _matmul(hn, du, "tn", BF16, "grad_w_up", b_cut=True, out_cut=True)
    st_xu, tok = _spread_start(d_w_up_cut, True, "exchange_w_up_start", d_ffn_cwb)
    dhn = _matmul(du, w_up_cut, "nt", F32, "bwd_ffn_up", after=tok, a_cut=True, b_cut=True)
    dh, dm, d_ln_ffn_pre, d_ln_mix_post = _ffn_residual_bwd(dy, dhn, h, ln_ffn_pre, m, ln_mix_post)

    d_w_out = _matmul(mix, dm, "tn", BF16, "grad_w_out")
    st_xo, tok = _spread_start(_row_blocks(d_w_out), True, "exchange_w_out_start", d_ln_ffn_pre)
    dmix = _matmul(dm, w_out_f, "nt", F32, "bwd_proj_out", after=tok)
    dq_sb, dk_sb, dv_sb, d_sb_gain = _sb_attention_bwd(sb_qkv, sb_out_gain, o_sb, sb_att, sb_lb, dmix)
    dqn, dkn, dvn, dgc_full, dbeta_full, dz, d_dn_gain = _dn_scan_bwd(
        qn, kn, vn, gc_full, beta_full, proj, z0, dn_out_gain, o_dn, tm_all, s_all, dmix, SBW)
    du_q, dcw_q = _dn_branch_bwd(proj, qkv0, dn_cw, 0, True, HEAD_DIM ** -0.5, dqn)
    du_k, dcw_k = _dn_branch_bwd(proj, qkv0 + DNW, dn_cw, DNW, True, 1.0, dkn)
    du_v, dcw_v = _dn_branch_bwd(proj, qkv0 + 2 * DNW, dn_cw, 2 * DNW, False, 1.0, dvn)
    dba, d_a_log_l, d_dt_bias_l = _dn_gates_bwd(proj, gate_block, a_log_l, dt_bias_l, dgc_full, dbeta_full)
    dproj = jnp.concatenate([dq_sb, dk_sb.astype(BF16), dv_sb.astype(BF16), du_q, du_k, du_v, dz, dba], axis=1)
    d_w_in_t = _matmul(dproj, xn, "tn", BF16, "grad_w_in", tm_cap=2432, tn_cap=512, tk_cap=1024)
    d_w_in_cut = d_w_in_t[:in_cols].reshape(N_DEV, in_cols // N_DEV, D)
    st_xi, tok = _spread_start(d_w_in_cut, True, "exchange_w_in_start", d_sb_gain)
    dxn = _matmul(dproj, w_in_t, "nn", F32, "bwd_proj_in", tk_cap=2432, after=tok)
    grad_x, d_ln_mix_pre = _input_bwd(dh, dxn, x2, ln_mix_pre)

    big = {}
    after = grad_x
    for n, st, w_, m_, v_ in [("w_down", st_xd, w_down, m_w_down, v_w_down), ("w_up", st_xu, w_up, m_w_up, v_w_up),
                              ("w_out", st_xo, w_out, m_w_out, v_w_out)]:
        got = _spread_wait(st, True, "exchange_" + n + "_wait", after)
        big[n] = _adamw_sharded(got, w_, m_, v_, "adamw_" + n)
        after = big[n][1]

    d_dn_cw = jnp.concatenate([dcw_q[:SHORT_CONV], dcw_k[:SHORT_CONV], dcw_v[:SHORT_CONV]], axis=1)
    small = [loss_part[:, :1], d_sb_gain, d_a_log_l[:, DN_HEADS:2 * DN_HEADS], d_dt_bias_l[:, DN_HEADS:2 * DN_HEADS], d_dn_gain,
             d_ln_mix_pre, d_ln_mix_post, d_ffn_cwb[FFN_CONV:FFN_CONV + 1], d_ln_ffn_pre, d_ln_ffn_post,
             d_dn_cw, d_ffn_cwb[:FFN_CONV]]
    shapes = [a.shape for a in small]
    red = _unpack(_all_reduce_packed(_pack(small), after), shapes)
    loss = red[0].reshape(())
    me = _index(_my_place())
    g_dn_cw = lax.dynamic_slice_in_dim(red[10], me * dn_conv_w.shape[2], dn_conv_w.shape[2], axis=1)
    g_ffn_cw = lax.dynamic_slice_in_dim(red[11], me * ffn_conv_w.shape[2], ffn_conv_w.shape[2], axis=1)
    names = ["sb_out_gain", "dn_conv_w", "dn_a_log", "dn_dt_bias", "dn_out_gain", "ln_mix_pre", "ln_mix_post",
             "ffn_conv_w", "ffn_conv_b", "ln_ffn_pre", "ln_ffn_post"]
    g_small = dict(sb_out_gain=red[1], dn_conv_w=g_dn_cw[None], dn_a_log=red[2], dn_dt_bias=red[3], dn_out_gain=red[4],
                   ln_mix_pre=red[5], ln_mix_post=red[6], ffn_conv_w=g_ffn_cw[None], ffn_conv_b=red[7],
                   ln_ffn_pre=red[8], ln_ffn_post=red[9])
    w_small = dict(sb_out_gain=sb_out_gain, dn_conv_w=dn_conv_w, dn_a_log=dn_a_log, dn_dt_bias=dn_dt_bias,
                   dn_out_gain=dn_out_gain, ln_mix_pre=ln_mix_pre, ln_mix_post=ln_mix_post, ffn_conv_w=ffn_conv_w,
                   ffn_conv_b=ffn_conv_b, ln_ffn_pre=ln_ffn_pre, ln_ffn_post=ln_ffn_post)
    m_small = dict(sb_out_gain=m_sb_out_gain, dn_conv_w=m_dn_conv_w, dn_a_log=m_dn_a_log, dn_dt_bias=m_dn_dt_bias,
                   dn_out_gain=m_dn_out_gain, ln_mix_pre=m_ln_mix_pre, ln_mix_post=m_ln_mix_post, ffn_conv_w=m_ffn_conv_w,
                   ffn_conv_b=m_ffn_conv_b, ln_ffn_pre=m_ln_ffn_pre, ln_ffn_post=m_ln_ffn_post)
    v_small = dict(sb_out_gain=v_sb_out_gain, dn_conv_w=v_dn_conv_w, dn_a_log=v_dn_a_log, dn_dt_bias=v_dn_dt_bias,
                   dn_out_gain=v_dn_out_gain, ln_mix_pre=v_ln_mix_pre, ln_mix_post=v_ln_mix_post, ffn_conv_w=v_ffn_conv_w,
                   ffn_conv_b=v_ffn_conv_b, ln_ffn_pre=v_ln_ffn_pre, ln_ffn_post=v_ln_ffn_post)
    sshapes = [w_small[n].shape for n in names]
    upd = _adamw_packed(_pack([g_small[n] for n in names]), _pack([w_small[n] for n in names]),
                        _pack([m_small[n] for n in names]), _pack([v_small[n] for n in names]))
    d_small, nm_small, nv_small = [dict(zip(names, _unpack(p, sshapes))) for p in upd]

    got = _spread_wait(st_xi, True, "exchange_w_in_wait", d_small["ln_ffn_post"])
    flip = lambda a: jnp.swapaxes(a, 1, 2)
    big["w_in"] = [flip(a) for a in _adamw_sharded(got, flip(w_in), flip(m_w_in), flip(v_w_in), "adamw_w_in")]

    order = ["w_in", "sb_out_gain", "dn_conv_w", "dn_a_log", "dn_dt_bias", "dn_out_gain", "w_out", "ln_mix_pre",
             "ln_mix_post", "w_up", "ffn_conv_w", "ffn_conv_b", "w_down", "ln_ffn_pre", "ln_ffn_post"]
    pick = lambda n, i: big[n][i] if n in big else [g_small, d_small, nm_small, nv_small][i][n].reshape(w_small[n].shape)
    return (loss, grad_x[None], *[pick(n, 0) for n in order], *[pick(n, 1) for n in order],
            *[pick(n, 2) for n in order], *[pick(n, 3) for n in order])
```

```python
import functools
import math

import jax
import jax.numpy as jnp
from jax import lax
from jax.experimental import pallas as pl
from jax.experimental.pallas import tpu as pltpu

F32 = jnp.float32
BF16 = jnp.bfloat16

N_DEV = 8
HEAD_DIM = 128
SB_HEADS = 8
DN_HEADS = 8
DN_CHUNK = 128
DN_GROUP = 8
INV_BLOCK = 16
SB_KEYS = 256
SHORT_CONV = 4
FFN_CONV = 3
EPS = 1e-6
LANES = 128
SUBLANES = 8
VMEM_CAP = 56 * 2**20

ADAM_LR = 0.001
ADAM_B1 = 0.9
ADAM_B2 = 0.999
ADAM_EPS = 1e-08
ADAM_WD = 0.01
ADAM_STEP = 10

MESH = pl.DeviceIdType.MESH

assert HEAD_DIM == DN_CHUNK == LANES


def _tile(n, cap, mult):
    if n <= cap:
        return n
    t = (cap // mult) * mult
    while t >= mult:
        if n % t == 0:
            return t
        t -= mult
    raise ValueError(f"no tile for {n} under {cap} in multiples of {mult}")


def _params(sem, vmem_bytes):
    limit = int(min(VMEM_CAP, max(vmem_bytes, 16 * 2**20)))
    if not sem:
        return pltpu.CompilerParams(vmem_limit_bytes=limit)
    return pltpu.CompilerParams(dimension_semantics=sem, vmem_limit_bytes=limit)


def _nbytes(shape, dtype):
    return math.prod(shape) * jnp.dtype(dtype).itemsize


_NN = (((1,), (0,)), ((), ()))
_NT = (((1,), (1,)), ((), ()))
_TN = (((0,), (0,)), ((), ()))


def _batched(dims, ndim):
    if ndim == 2:
        return dims
    (ca,), (cb,) = dims[0]
    return (((ca + 1,), (cb + 1,)), ((0,), (0,)))


def _dot(a, b, dims=_NN):
    return lax.dot_general(a.astype(BF16), b.astype(BF16), _batched(dims, a.ndim), preferred_element_type=F32)


def _split2(x):
    hi = x.astype(BF16)
    lo = (x - hi.astype(F32)).astype(BF16)
    return hi, lo


def _split3(x):
    hi = x.astype(BF16)
    r = x - hi.astype(F32)
    mid = r.astype(BF16)
    lo = (r - mid.astype(F32)).astype(BF16)
    return hi, mid, lo


def _dot01(x, m01, passes=3):
    parts = _split3(x) if passes == 3 else _split2(x)
    out = None
    for p in parts:
        t = lax.dot_general(p, m01, _NN, preferred_element_type=F32)
        out = t if out is None else out + t
    return out


def _dot01_left(m01, x, passes=3):
    parts = _split3(x) if passes == 3 else _split2(x)
    out = None
    for p in parts:
        t = lax.dot_general(m01, p, _NN, preferred_element_type=F32)
        out = t if out is None else out + t
    return out


def _mm3(a, b, dims=_NN):
    ah, al = _split2(a)
    bh, bl = _split2(b)
    d = functools.partial(lax.dot_general, dimension_numbers=_batched(dims, a.ndim), preferred_element_type=F32)
    return d(ah, bh) + (d(ah, bl) + d(al, bh))


def _rowsum(x):
    return jnp.sum(x, axis=-1, keepdims=True)


def _t(x):
    return jnp.swapaxes(x, -1, -2)


def _sigmoid(x):
    return 1.0 / (1.0 + jnp.exp(-x))


def _softplus(x):
    return jnp.maximum(x, 0.0) + jnp.log(1.0 + jnp.exp(-jnp.abs(x)))


def _silu(x):
    return x * _sigmoid(x)


def _silu_grad(x):
    s = _sigmoid(x)
    return s * (1.0 + x * (1.0 - s))


_GELU_C = math.sqrt(2.0 / math.pi)


def _gelu(x):
    return 0.5 * x * (1.0 + jnp.tanh(_GELU_C * (x + 0.044715 * x * x * x)))


def _gelu_and_grad(x):
    x2 = x * x
    th = jnp.tanh(_GELU_C * (x + 0.044715 * x2 * x))
    half = 0.5 * (1.0 + th)
    return x * half, half + 0.5 * x * (1.0 - th * th) * (_GELU_C * (1.0 + 3.0 * 0.044715 * x2))


def _rms(x, g):
    r = lax.rsqrt(jnp.mean(x * x, axis=-1, keepdims=True) + EPS)
    return x * r * g


def _rms_bwd(dy, x, g):
    r = lax.rsqrt(jnp.mean(x * x, axis=-1, keepdims=True) + EPS)
    xh = x * r
    gdy = dy * g
    dx = r * (gdy - xh * jnp.mean(gdy * xh, axis=-1, keepdims=True))
    return dx, jnp.sum(dy * xh, axis=-2, keepdims=True)


def _iota2(shape, axis):
    return lax.broadcasted_iota(jnp.int32, shape, axis)


def _shift_down(cur, prev8, k):
    n = cur.shape[0]
    r = pltpu.roll(cur, k, 0)
    pr = pltpu.roll(prev8, k, 0)
    head = jnp.where(_iota2(pr.shape, 0) < k, pr, r[0:SUBLANES])
    if n == SUBLANES:
        return head
    return jnp.concatenate([head, r[SUBLANES:]], axis=0)


def _shift_up(cur, next8, k):
    n = cur.shape[0]
    r = pltpu.roll(cur, n - k, 0)
    nr = pltpu.roll(next8, SUBLANES - k, 0)
    tail = jnp.where(_iota2(nr.shape, 0) >= SUBLANES - k, nr, r[n - SUBLANES:])
    if n == SUBLANES:
        return tail
    return jnp.concatenate([r[:n - SUBLANES], tail], axis=0)


def _causal_conv(cur, prev8, w_ref, taps):
    out = cur * w_ref[taps - 1:taps, :]
    for j in range(taps - 1):
        out = out + _shift_down(cur, prev8, taps - 1 - j) * w_ref[j:j + 1, :]
    return out


def _anti_conv(cur, next8, w_ref, taps):
    out = cur * w_ref[taps - 1:taps, :]
    for j in range(taps - 1):
        out = out + _shift_up(cur, next8, taps - 1 - j) * w_ref[j:j + 1, :]
    return out


def _matmul(a, b, mode, out_dtype, name, tm_cap=1024, tn_cap=1024, tk_cap=2048, after=None,
            a_cut=False, b_cut=False, out_cut=False):
    a_shard = a.shape[2] if a_cut else None
    b_shard = b.shape[2] if b_cut else None
    a_full = (a.shape[1], a.shape[0] * a_shard) if a_cut else a.shape
    b_full = (b.shape[1], b.shape[0] * b_shard) if b_cut else b.shape
    assert not (a_cut and mode == "tn")
    if mode == "nn":
        (M, K), N = a_full, b_full[1]
    elif mode == "nt":
        (M, K), N = a_full, b_full[0]
    else:
        (K, M), N = a_full, b_full[1]
    n_unit = b_shard if (b_cut and mode != "nt") else N
    k_unit = math.gcd(a_shard or K, b_shard if (b_cut and mode == "nt") else K)
    tm = _tile(M, tm_cap, LANES)
    tn = N // N_DEV if out_cut else _tile(n_unit, tn_cap, LANES)
    tk = _tile(k_unit, tk_cap, LANES)
    assert n_unit % tn == 0 and k_unit % tk == 0
    nk = K // tk
    dims = {"nn": _NN, "nt": _NT, "tn": _TN}[mode]
    if a_cut:
        pa = a_shard // tk
        a_spec = pl.BlockSpec((None, tm, tk), lambda i, j, k: (k // pa, i, k % pa))
    elif mode == "tn":
        a_spec = pl.BlockSpec((tk, tm), lambda i, j, k: (k, i))
    else:
        a_spec = pl.BlockSpec((tm, tk), lambda i, j, k: (i, k))
    if b_cut and mode == "nt":
        pb = b_shard // tk
        b_spec = pl.BlockSpec((None, tn, tk), lambda i, j, k: (k // pb, j, k % pb))
    elif b_cut:
        pb = b_shard // tn
        b_spec = pl.BlockSpec((None, tk, tn), lambda i, j, k: (j // pb, k, j % pb))
    elif mode == "nt":
        b_spec = pl.BlockSpec((tn, tk), lambda i, j, k: (j, k))
    else:
        b_spec = pl.BlockSpec((tk, tn), lambda i, j, k: (k, j))
    if out_cut:
        out_spec, out_shape = pl.BlockSpec((None, tm, tn), lambda i, j, k: (j, i, 0)), (N_DEV, M, tn)
    else:
        out_spec, out_shape = pl.BlockSpec((tm, tn), lambda i, j, k: (i, j)), (M, N)

    def body(a_ref, b_ref, *rest):
        if nk == 1:
            rest[-1][...] = lax.dot_general(a_ref[...], b_ref[...], dims, preferred_element_type=F32).astype(rest[-1].dtype)
            return
        o_ref, acc_ref = rest[-2:]
        k = pl.program_id(2)

        @pl.when(k == 0)
        def _():
            acc_ref[...] = jnp.zeros_like(acc_ref)

        acc_ref[...] += lax.dot_general(a_ref[...], b_ref[...], dims, preferred_element_type=F32)

        @pl.when(k == nk - 1)
        def _():
            o_ref[...] = acc_ref[...].astype(o_ref.dtype)

    vmem = 2 * (_nbytes((tm, tk), a.dtype) + _nbytes((tk, tn), b.dtype) + _nbytes((tm, tn), out_dtype)) + _nbytes((tm, tn), F32)
    vmem += _nbytes((tm, tn), F32) + (2 * _nbytes((tm, tk), a.dtype) if mode == "tn" else 0)
    tokens = [] if after is None else [after]
    return pl.pallas_call(
        body, name=name, grid=(M // tm, N // tn, nk),
        in_specs=[a_spec, b_spec] + [pl.BlockSpec(t.shape, lambda i, j, k: (0, 0)) for t in tokens],
        out_specs=out_spec,
        out_shape=jax.ShapeDtypeStruct(out_shape, out_dtype),
        scratch_shapes=[] if nk == 1 else [pltpu.VMEM((tm, tn), F32)],
        compiler_params=_params(("parallel", "parallel", "arbitrary"), vmem + 4 * 2**20),
    )(a, b, *tokens)


def _row_call(body, name, T, D, ins, outs, tr, acc_outs=()):
    def spec(a, kind):
        if kind == "row":
            return pl.BlockSpec((tr, a.shape[1]), lambda i: (i, 0))
        return pl.BlockSpec(a.shape, lambda i: (0, 0))
    in_specs = [spec(a, k) for a, k in ins]
    out_specs = [spec(a, k) for a, k in outs] + [spec(a, "vec") for a in acc_outs]
    out_shape = [a for a, _ in outs] + list(acc_outs)
    vmem = 2 * sum(_nbytes((tr, a.shape[1]) if k == "row" else a.shape, a.dtype) for a, k in list(ins) + list(outs))
    return pl.pallas_call(
        body, name=name, grid=(T // tr,), in_specs=in_specs, out_specs=out_specs, out_shape=out_shape,
        compiler_params=_params(("arbitrary",), 3 * vmem + 8 * 2**20),
    )(*[a for a, _ in ins])


def _sds(shape, dtype):
    return jax.ShapeDtypeStruct(shape, dtype)


def _accumulate(ref, val):
    @pl.when(pl.program_id(0) == 0)
    def _():
        ref[...] = jnp.zeros_like(ref)
    ref[...] += val


def _norm_in(x, g):
    T, D = x.shape

    def body(x_ref, g_ref, o_ref):
        o_ref[...] = _rms(x_ref[...], g_ref[...]).astype(BF16)

    return _row_call(body, "norm_in", T, D, [(x, "row"), (g, "vec")], [(_sds((T, D), BF16), "row")], _tile(T, 256, 16))[0]


def _mix_residual(x, m, g_post, g_pre):
    T, D = x.shape

    def body(x_ref, m_ref, gp_ref, gn_ref, h_ref, hn_ref):
        h = x_ref[...] + _rms(m_ref[...], gp_ref[...])
        h_ref[...] = h
        hn_ref[...] = _rms(h, gn_ref[...]).astype(BF16)

    return _row_call(body, "mix_residual", T, D, [(x, "row"), (m, "row"), (g_post, "vec"), (g_pre, "vec")],
                     [(_sds((T, D), F32), "row"), (_sds((T, D), BF16), "row")], _tile(T, 256, 16))


def _loss_head(h, f, g_post, target):
    T, D = h.shape

    def body(h_ref, f_ref, g_ref, t_ref, dy_ref, df_ref, dg_ref, loss_ref):
        f = f_ref[...]
        g = g_ref[...]
        diff = h_ref[...] + _rms(f, g) - t_ref[...]
        dy = diff * (1.0 / D)
        dy_ref[...] = dy
        df, dg = _rms_bwd(dy, f, g)
        df_ref[...] = df.astype(BF16)
        _accumulate(dg_ref, dg)
        _accumulate(loss_ref, jnp.full((1, LANES), 0.5 / D, F32) * jnp.sum(diff * diff))

    return _row_call(body, "loss_head", T, D, [(h, "row"), (f, "row"), (g_post, "vec"), (target, "row")],
                     [(_sds((T, D), F32), "row"), (_sds((T, D), BF16), "row")], _tile(T, 256, 16),
                     acc_outs=[_sds((1, D), F32), _sds((1, LANES), F32)])


def _ffn_residual_bwd(dy, dhn, h, g_pre, m, g_post):
    T, D = h.shape

    def body(dy_ref, dhn_ref, h_ref, gn_ref, m_ref, gp_ref, dh_ref, dm_ref, dgn_ref, dgp_ref):
        dhh, dgn = _rms_bwd(dhn_ref[...], h_ref[...], gn_ref[...])
        dh = dy_ref[...] + dhh
        dh_ref[...] = dh
        dm, dgp = _rms_bwd(dh, m_ref[...], gp_ref[...])
        dm_ref[...] = dm.astype(BF16)
        _accumulate(dgn_ref, dgn)
        _accumulate(dgp_ref, dgp)

    return _row_call(body, "ffn_residual_bwd", T, D,
                     [(dy, "row"), (dhn, "row"), (h, "row"), (g_pre, "vec"), (m, "row"), (g_post, "vec")],
                     [(_sds((T, D), F32), "row"), (_sds((T, D), BF16), "row")], _tile(T, 128, 16),
                     acc_outs=[_sds((1, D), F32), _sds((1, D), F32)])


def _input_bwd(dh, dxn, x, g):
    T, D = x.shape

    def body(dh_ref, dxn_ref, x_ref, g_ref, dx_ref, dg_ref):
        dx, dg = _rms_bwd(dxn_ref[...], x_ref[...], g_ref[...])
        dx_ref[...] = dh_ref[...] + dx
        _accumulate(dg_ref, dg)

    return _row_call(body, "input_bwd", T, D, [(dh, "row"), (dxn, "row"), (x, "row"), (g, "vec")],
                     [(_sds((T, D), F32), "row")], _tile(T, 256, 16), acc_outs=[_sds((1, D), F32)])


def _ffn_act(u, conv_w, conv_b):
    T, F2 = u.shape
    F = F2 // 2
    tc = _tile(F, 512, LANES)
    tr = _tile(T, 512, SUBLANES)
    nc = F // tc
    r8 = tr // SUBLANES

    def body(ug_ref, ugp_ref, uv_ref, uvp_ref, wg_ref, wv_ref, bg_ref, bv_ref, a_ref):
        first = pl.program_id(1) == 0
        cg = _causal_conv(ug_ref[...], jnp.where(first, 0.0, ugp_ref[...]), wg_ref, FFN_CONV) + bg_ref[...]
        cv = _causal_conv(uv_ref[...], jnp.where(first, 0.0, uvp_ref[...]), wv_ref, FFN_CONV) + bv_ref[...]
        a_ref[...] = (_gelu(cg) * cv).astype(BF16)

    cur = lambda off: pl.BlockSpec((tr, tc), lambda j, i: (i, j + off))
    prev = lambda off: pl.BlockSpec((SUBLANES, tc), lambda j, i: (jnp.maximum(i * r8 - 1, 0), j + off))
    wsp = lambda off: pl.BlockSpec((FFN_CONV, tc), lambda j, i: (0, j + off))
    bsp = lambda off: pl.BlockSpec((1, tc), lambda j, i: (0, j + off))
    return pl.pallas_call(
        body, name="ffn_act", grid=(nc, T // tr),
        in_specs=[cur(0), prev(0), cur(nc), prev(nc), wsp(0), wsp(nc), bsp(0), bsp(nc)],
        out_specs=pl.BlockSpec((tr, tc), lambda j, i: (i, j)),
        out_shape=_sds((T, F), BF16),
        compiler_params=_params(("parallel", "arbitrary"), 12 * _nbytes((tr, tc), F32) + 8 * 2**20),
    )(u, u, u, u, conv_w, conv_w, conv_b, conv_b)


def _ffn_act_bwd(u, conv_w, conv_b, da):
    T, F2 = u.shape
    F = F2 // 2
    tc = _tile(F, 512, LANES)
    tr = _tile(T, 512, SUBLANES)
    nc = F // tc
    r8 = tr // SUBLANES
    n8 = T // SUBLANES
    K = FFN_CONV

    def body(ug_ref, ugp_ref, ugn_ref, uv_ref, uvp_ref, uvn_ref, da_ref, dan_ref,
             wg_ref, wv_ref, bg_ref, bv_ref, du_ref, dwb_ref):
        i = pl.program_id(1)
        first = i == 0
        last = i == pl.num_programs(1) - 1

        def dconv(ug, ug_prev, uv, uv_prev, da_):
            cg = _causal_conv(ug, ug_prev, wg_ref, K) + bg_ref[...]
            cv = _causal_conv(uv, uv_prev, wv_ref, K) + bv_ref[...]
            act, act_grad = _gelu_and_grad(cg)
            return da_ * cv * act_grad, da_ * act

        ug, uv = ug_ref[...], uv_ref[...]
        ug_prev, uv_prev = jnp.where(first, 0.0, ugp_ref[...]), jnp.where(first, 0.0, uvp_ref[...])
        dcg, dcv = dconv(ug, ug_prev, uv, uv_prev, da_ref[...])
        dcgn, dcvn = dconv(ugn_ref[...], ug[tr - SUBLANES:], uvn_ref[...], uv[tr - SUBLANES:], dan_ref[...])
        du_ref[0] = _anti_conv(dcg, jnp.where(last, 0.0, dcgn), wg_ref, K).astype(BF16)
        du_ref[1] = _anti_conv(dcv, jnp.where(last, 0.0, dcvn), wv_ref, K).astype(BF16)

        @pl.when(first)
        def _():
            dwb_ref[...] = jnp.zeros_like(dwb_ref)

        for half, (dc, uo, uo_prev) in enumerate([(dcg, ug, ug_prev), (dcv, uv, uv_prev)]):
            rows = [jnp.sum(dc * _shift_down(uo, uo_prev, K - 1 - t), axis=0, keepdims=True) for t in range(K - 1)]
            rows += [jnp.sum(dc * uo, axis=0, keepdims=True), jnp.sum(dc, axis=0, keepdims=True)]
            rows += [jnp.zeros_like(rows[0])] * (SUBLANES - len(rows))
            dwb_ref[half] += jnp.concatenate(rows, axis=0)

    cur = lambda off: pl.BlockSpec((tr, tc), lambda j, i: (i, j + off))
    prev = lambda off: pl.BlockSpec((SUBLANES, tc), lambda j, i: (jnp.maximum(i * r8 - 1, 0), j + off))
    nxt = lambda off: pl.BlockSpec((SUBLANES, tc), lambda j, i: (jnp.minimum((i + 1) * r8, n8 - 1), j + off))
    wsp = lambda off: pl.BlockSpec((K, tc), lambda j, i: (0, j + off))
    bsp = lambda off: pl.BlockSpec((1, tc), lambda j, i: (0, j + off))
    return pl.pallas_call(
        body, name="ffn_act_bwd", grid=(nc, T // tr),
        in_specs=[cur(0), prev(0), nxt(0), cur(nc), prev(nc), nxt(nc), cur(0), nxt(0), wsp(0), wsp(nc), bsp(0), bsp(nc)],
        out_specs=[pl.BlockSpec((2, tr, tc), lambda j, i: (0, i, j)), pl.BlockSpec((2, SUBLANES, tc), lambda j, i: (0, 0, j))],
        out_shape=[_sds((2, T, F), BF16), _sds((2, SUBLANES, F), F32)],
        compiler_params=_params(("parallel", "arbitrary"), 24 * _nbytes((tr, tc), F32) + 8 * 2**20),
    )(u, u, u, u, u, u, da, da, conv_w, conv_w, conv_b, conv_b)


def _l2norm(s, scale):
    return s * (lax.rsqrt(jnp.sum(s * s, axis=-1, keepdims=True) + EPS) * scale)


def _dn_branch(proj, col0, conv_w, wcol0, l2, scale):
    T = proj.shape[0]
    W = DN_HEADS * HEAD_DIM
    tr = _tile(T, 2048, SUBLANES)
    r8 = tr // SUBLANES
    cb0, wb0 = col0 // HEAD_DIM, wcol0 // HEAD_DIM

    def body(u_ref, up_ref, w_ref, o_ref):
        first = pl.program_id(1) == 0
        s = _silu(_causal_conv(u_ref[...], jnp.where(first, 0.0, up_ref[...]), w_ref, SHORT_CONV))
        o_ref[...] = _l2norm(s, scale) if l2 else s

    return pl.pallas_call(
        body, name=f"dn_branch_{col0}", grid=(DN_HEADS, T // tr),
        in_specs=[pl.BlockSpec((tr, HEAD_DIM), lambda h, i: (i, cb0 + h)),
                  pl.BlockSpec((SUBLANES, HEAD_DIM), lambda h, i: (jnp.maximum(i * r8 - 1, 0), cb0 + h)),
                  pl.BlockSpec((SHORT_CONV, HEAD_DIM), lambda h, i: (0, wb0 + h))],
        out_specs=pl.BlockSpec((tr, HEAD_DIM), lambda h, i: (i, h)),
        out_shape=_sds((T, W), F32),
        compiler_params=_params(("parallel", "arbitrary"), 32 * _nbytes((tr, HEAD_DIM), F32) + 8 * 2**20),
    )(proj, proj, conv_w)


def _dn_branch_bwd(proj, col0, conv_w, wcol0, l2, scale, dy):
    T = proj.shape[0]
    W = DN_HEADS * HEAD_DIM
    tr = _tile(T, 2048, SUBLANES)
    r8 = tr // SUBLANES
    n8 = T // SUBLANES
    cb0, wb0 = col0 // HEAD_DIM, wcol0 // HEAD_DIM
    K = SHORT_CONV

    def body(u_ref, up_ref, un_ref, dy_ref, dyn_ref, w_ref, du_ref, dw_ref):
        i = pl.program_id(1)
        first = i == 0
        last = i == pl.num_programs(1) - 1

        def dconv(u, u_prev, dy_):
            c = _causal_conv(u, u_prev, w_ref, K)
            if l2:
                s = _silu(c)
                r = lax.rsqrt(jnp.sum(s * s, axis=-1, keepdims=True) + EPS)
                n = s * r
                ds = (scale * r) * (dy_ - n * jnp.sum(dy_ * n, axis=-1, keepdims=True))
            else:
                ds = dy_
            return ds * _silu_grad(c)

        u = u_ref[...]
        u_prev = jnp.where(first, 0.0, up_ref[...])
        dc = dconv(u, u_prev, dy_ref[...])
        dcn = jnp.where(last, 0.0, dconv(un_ref[...], u[tr - SUBLANES:], dyn_ref[...]))
        du_ref[...] = _anti_conv(dc, dcn, w_ref, K).astype(BF16)
        rows = [jnp.sum(dc * _shift_down(u, u_prev, K - 1 - t), axis=0, keepdims=True) for t in range(K - 1)]
        rows += [jnp.sum(dc * u, axis=0, keepdims=True)]
        rows += [jnp.zeros_like(rows[0])] * (SUBLANES - len(rows))
        upd = jnp.concatenate(rows, axis=0)

        @pl.when(first)
        def _():
            dw_ref[...] = jnp.zeros_like(dw_ref)
        dw_ref[...] += upd

    return pl.pallas_call(
        body, name=f"dn_branch_bwd_{col0}", grid=(DN_HEADS, T // tr),
        in_specs=[pl.BlockSpec((tr, HEAD_DIM), lambda h, i: (i, cb0 + h)),
                  pl.BlockSpec((SUBLANES, HEAD_DIM), lambda h, i: (jnp.maximum(i * r8 - 1, 0), cb0 + h)),
                  pl.BlockSpec((SUBLANES, HEAD_DIM), lambda h, i: (jnp.minimum((i + 1) * r8, n8 - 1), cb0 + h)),
                  pl.BlockSpec((tr, HEAD_DIM), lambda h, i: (i, h)),
                  pl.BlockSpec((SUBLANES, HEAD_DIM), lambda h, i: (jnp.minimum((i + 1) * r8, n8 - 1), h)),
                  pl.BlockSpec((K, HEAD_DIM), lambda h, i: (0, wb0 + h))],
        out_specs=[pl.BlockSpec((tr, HEAD_DIM), lambda h, i: (i, h)),
                   pl.BlockSpec((SUBLANES, HEAD_DIM), lambda h, i: (0, h))],
        out_shape=[_sds((T, W), BF16), _sds((SUBLANES, W), F32)],
        compiler_params=_params(("parallel", "arbitrary"), 32 * _nbytes((tr, HEAD_DIM), F32) + 8 * 2**20),
    )(proj, proj, proj, dy, dy, conv_w)


def _lane_masks(shape):
    lane = _iota2(shape, 1)
    return lane < DN_HEADS, (lane >= DN_HEADS) & (lane < 2 * DN_HEADS)


def _expand01(off):
    r = _iota2((LANES, DN_HEADS * HEAD_DIM), 0)
    c = _iota2((LANES, DN_HEADS * HEAD_DIM), 1)
    return (r == jnp.right_shift(c, int(math.log2(HEAD_DIM))) + off).astype(BF16)


def _select01(off):
    r = _iota2((DN_HEADS * HEAD_DIM, LANES), 0)
    c = _iota2((DN_HEADS * HEAD_DIM, LANES), 1)
    return (r == (c - off) * HEAD_DIM).astype(BF16)


def _dn_gates(proj, gate_block, a_log_l, dt_bias_l):
    T = proj.shape[0]
    C = DN_CHUNK
    W = DN_HEADS * HEAD_DIM

    def body(ba_ref, al_ref, dt_ref, gc_ref, beta_ref):
        ba = ba_ref[...]
        is_b, is_a = _lane_masks(ba.shape)
        g = jnp.where(is_a, -jnp.exp(al_ref[...]) * _softplus(ba + dt_ref[...]), 0.0)
        beta = jnp.where(is_b, _sigmoid(ba), 0.0)
        tri = (_iota2((C, C), 0) >= _iota2((C, C), 1)).astype(BF16)
        gc = _dot01_left(tri, g)
        gc_ref[...] = _dot01(gc, _expand01(DN_HEADS))
        beta_ref[...] = _dot01(beta, _expand01(0))

    vec = pl.BlockSpec((1, LANES), lambda n: (0, 0))
    return pl.pallas_call(
        body, name="dn_gates", grid=(T // C,),
        in_specs=[pl.BlockSpec((C, LANES), lambda n: (n, gate_block)), vec, vec],
        out_specs=[pl.BlockSpec((C, W), lambda n: (n, 0))] * 2,
        out_shape=[_sds((T, W), F32)] * 2,
        compiler_params=_params(("parallel",), 16 * 2**20),
    )(proj, a_log_l, dt_bias_l)


def _dn_gates_bwd(proj, gate_block, a_log_l, dt_bias_l, dgc_full, dbeta_full):
    T = proj.shape[0]
    C = DN_CHUNK
    W = DN_HEADS * HEAD_DIM

    def body(ba_ref, al_ref, dt_ref, dgc_ref, dbeta_ref, dba_ref, dal_ref, ddt_ref):
        ba = ba_ref[...]
        is_b, is_a = _lane_masks(ba.shape)
        ea = jnp.exp(al_ref[...])
        pre = ba + dt_ref[...]
        g = jnp.where(is_a, -ea * _softplus(pre), 0.0)
        beta = _sigmoid(ba)
        dgc = _dot01(dgc_ref[...], _select01(DN_HEADS))
        dbeta = _dot01(dbeta_ref[...], _select01(0))
        triu = (_iota2((C, C), 0) <= _iota2((C, C), 1)).astype(BF16)
        dg = _dot01_left(triu, dgc)
        da = jnp.where(is_a, dg * (-ea) * _sigmoid(pre), 0.0)
        dba_ref[...] = (da + jnp.where(is_b, dbeta * beta * (1.0 - beta), 0.0)).astype(BF16)
        _accumulate(dal_ref, jnp.sum(dg * g, axis=0, keepdims=True))
        _accumulate(ddt_ref, jnp.sum(da, axis=0, keepdims=True))

    vec = pl.BlockSpec((1, LANES), lambda n: (0, 0))
    full = pl.BlockSpec((C, W), lambda n: (n, 0))
    return pl.pallas_call(
        body, name="dn_gates_bwd", grid=(T // C,),
        in_specs=[pl.BlockSpec((C, LANES), lambda n: (n, gate_block)), vec, vec, full, full],
        out_specs=[pl.BlockSpec((C, LANES), lambda n: (n, 0)), vec, vec],
        out_shape=[_sds((T, LANES), BF16), _sds((1, LANES), F32), _sds((1, LANES), F32)],
        compiler_params=_params(("arbitrary",), 16 * 2**20),
    )(proj, a_log_l, dt_bias_l, dgc_full, dbeta_full)


def _unit_lower_inverse(L):
    C = L.shape[-1]
    row, col = _iota2((C, C), 0), _iota2((C, C), 1)
    eye = (row == col).astype(F32)
    sh = int(math.log2(INV_BLOCK))
    Ld = jnp.where(jnp.right_shift(row, sh) == jnp.right_shift(col, sh), L, 0.0)
    Lo = L - Ld
    X = eye - Ld
    P = Ld
    for _ in range(int(math.log2(INV_BLOCK)) - 1):
        P = _mm3(P, P)
        X = X + _mm3(X, P)
    N = _mm3(X, Lo)
    Y = eye - N
    P = N
    for _ in range(int(math.log2(C // INV_BLOCK)) - 1):
        P = _mm3(P, P)
        Y = Y + _mm3(Y, P)
    return _mm3(Y, X)


def _dn_chunk_common(q, k, v, gc, beta, gl):
    C = q.shape[-2]
    row, col = _iota2((C, C), 0), _iota2((C, C), 1)
    causal, strict = row >= col, row > col
    eg = jnp.exp(gc)
    decay = jnp.where(causal, jnp.exp(jnp.where(causal, gc - _t(gc), 0.0)), 0.0)
    kb, vb = k * beta, v * beta
    L = jnp.where(strict, _dot(kb, k, _NT) * decay, 0.0)
    Aqk = jnp.where(causal, _dot(q, k, _NT) * decay, 0.0)
    ektg = jnp.exp(gl - gc)
    return dict(causal=causal, strict=strict, eg=eg, decay=decay, kb=kb, vb=vb, L=L, Aqk=Aqk, ektg=ektg,
                kbg=kb * eg, kte=k * ektg, qd=q * eg, egl=jnp.exp(gl))


def _dn_scan(qn, kn, vn, gc_full, beta_full, proj, z_col0, gain):
    T, W = qn.shape
    C = DN_CHUNK
    N = T // C
    H = DN_HEADS
    G = DN_GROUP
    GW = G * HEAD_DIM
    zb0 = z_col0 // GW

    def body(q_ref, k_ref, v_ref, gc_ref, beta_ref, z_ref, gain_ref, o_ref, mix_ref, tm_ref, s_ref, S):
        @pl.when(pl.program_id(1) == 0)
        def _():
            S[...] = jnp.zeros_like(S)

        heads = lambda ref, rows=slice(None): jnp.stack([ref[rows, g * HEAD_DIM:(g + 1) * HEAD_DIM] for g in range(G)])
        q, k, v, gc, beta = heads(q_ref), heads(k_ref), heads(v_ref), heads(gc_ref), heads(beta_ref)
        gl = heads(gc_ref, slice(C - 1, C))
        c = _dn_chunk_common(q, k, v, gc, beta, gl)
        Tm = _unit_lower_inverse(c["L"])
        u = _dot(Tm, c["vb"])
        w = _dot(Tm, c["kbg"])
        S0 = S[...]
        vnew = u - _dot(w, S0)
        o = _dot(c["qd"], S0) + _dot(c["Aqk"], vnew)
        S[...] = S0 * c["egl"] + _dot(c["kte"], vnew, _TN)
        tm_ref[...] = Tm
        s_ref[...] = S0
        mix = (_rms(o, gain_ref[...]) * _silu(heads(z_ref))).astype(BF16)
        for g in range(G):
            sl = slice(g * HEAD_DIM, (g + 1) * HEAD_DIM)
            o_ref[:, sl] = o[g]
            mix_ref[:, sl] = mix[g]

    blk = pl.BlockSpec((C, GW), lambda h, n: (n, h))
    mat = pl.BlockSpec((G, None, C, C), lambda h, n: (h, n, 0, 0))
    return pl.pallas_call(
        body, name="dn_scan", grid=(H // G, N),
        in_specs=[blk, blk, blk, blk, blk, pl.BlockSpec((C, GW), lambda h, n: (n, zb0 + h)),
                  pl.BlockSpec((1, HEAD_DIM), lambda h, n: (0, 0))],
        out_specs=[blk, blk, mat, mat],
        out_shape=[_sds((T, W), F32), _sds((T, W), BF16), _sds((H, N, C, C), F32), _sds((H, N, C, C), F32)],
        scratch_shapes=[pltpu.VMEM((G, HEAD_DIM, HEAD_DIM), F32)],
        compiler_params=_params(("parallel", "arbitrary"), 32 * 2**20),
    )(qn, kn, vn, gc_full, beta_full, proj, gain)


def _dn_scan_bwd(qn, kn, vn, gc_full, beta_full, proj, z_col0, gain, o_raw, tm_all, s_all, dmix, dmix_col0):
    T, W = qn.shape
    C = DN_CHUNK
    N = T // C
    H = DN_HEADS
    G = DN_GROUP
    GW = G * HEAD_DIM
    zb0 = z_col0 // GW
    mb0 = dmix_col0 // GW

    def body(q_ref, k_ref, v_ref, gc_ref, beta_ref, z_ref, gain_ref, o_ref, tm_ref, s_ref, dmix_ref,
             dq_ref, dk_ref, dv_ref, dgc_ref, dbeta_ref, dz_ref, dgain_ref, dS):
        @pl.when(pl.program_id(1) == 0)
        def _():
            dS[...] = jnp.zeros_like(dS)

        @pl.when((pl.program_id(0) == 0) & (pl.program_id(1) == 0))
        def _():
            dgain_ref[...] = jnp.zeros_like(dgain_ref)

        heads = lambda ref, rows=slice(None): jnp.stack([ref[rows, g * HEAD_DIM:(g + 1) * HEAD_DIM] for g in range(G)])
        total = lambda x: jnp.sum(jnp.sum(x, axis=-1, keepdims=True), axis=-2, keepdims=True)
        gain = gain_ref[...]
        o, z, dmix = heads(o_ref), heads(z_ref), heads(dmix_ref)
        dz = (dmix * _rms(o, gain) * _silu_grad(z)).astype(BF16)
        do, dgain = _rms_bwd(dmix * _silu(z), o, gain)
        dgain_ref[...] += jnp.sum(dgain, axis=0)

        q, k, v, gc, beta = heads(q_ref), heads(k_ref), heads(v_ref), heads(gc_ref), heads(beta_ref)
        gl = heads(gc_ref, slice(C - 1, C))
        c = _dn_chunk_common(q, k, v, gc, beta, gl)
        Tm, S0, dS1 = tm_ref[...], s_ref[...], dS[...]
        w = _dot(Tm, c["kbg"])
        vnew = _dot(Tm, c["vb"]) - _dot(w, S0)

        dvnew = _dot(c["Aqk"], do, _TN) + _dot(c["kte"], dS1)
        dAqk = jnp.where(c["causal"], _dot(do, vnew, _NT), 0.0)
        dqd = _dot(do, S0, _NT)
        dkte = _dot(vnew, dS1, _NT)
        dgl = total(dS1 * S0) * c["egl"]
        dw = -_dot(dvnew, S0, _NT)
        dS[...] = dS1 * c["egl"] + _dot(c["qd"], do, _TN) - _dot(w, dvnew, _TN)

        dTm = _dot(dvnew, c["vb"], _NT) + _dot(dw, c["kbg"], _NT)
        dvb = _dot(Tm, dvnew, _TN)
        dkbg = _dot(Tm, dw, _TN)
        dL = jnp.where(c["strict"], -_mm3(_mm3(Tm, dTm, _TN), Tm, _NT), 0.0)
        dP = dL * c["decay"]
        dQ = dAqk * c["decay"]
        M = dL * c["L"] + dAqk * c["Aqk"]
        dkb = _dot(dP, k) + dkbg * c["eg"]
        dk = _dot(dP, c["kb"], _TN) + _dot(dQ, q, _TN) + dkte * c["ektg"] + dkb * beta
        dq = _dot(dQ, k) + dqd * c["eg"]
        tk = _rowsum(dkte * c["kte"])
        dgc = (_rowsum(M) - _rowsum(_t(M)) + _rowsum(dqd * c["qd"]) - tk + _rowsum(dkbg * c["kbg"]))
        dgl = dgl + total(tk)
        dgc = jnp.broadcast_to(dgc, q.shape) + jnp.where(_iota2((C, HEAD_DIM), 0) == C - 1, dgl, 0.0)
        dv = dvb * beta
        dbeta = jnp.broadcast_to(_rowsum(dkb * k) + _rowsum(dvb * v), q.shape)
        for g in range(G):
            sl = slice(g * HEAD_DIM, (g + 1) * HEAD_DIM)
            dz_ref[:, sl] = dz[g]
            dq_ref[:, sl] = dq[g]
            dk_ref[:, sl] = dk[g]
            dv_ref[:, sl] = dv[g]
            dgc_ref[:, sl] = dgc[g]
            dbeta_ref[:, sl] = dbeta[g]

    rev = lambda off: pl.BlockSpec((C, GW), lambda h, n: (N - 1 - n, off + h))
    mat = pl.BlockSpec((G, None, C, C), lambda h, n: (h, N - 1 - n, 0, 0))
    vec = pl.BlockSpec((1, HEAD_DIM), lambda h, n: (0, 0))
    return pl.pallas_call(
        body, name="dn_scan_bwd", grid=(H // G, N),
        in_specs=[rev(0), rev(0), rev(0), rev(0), rev(0), rev(zb0), vec, rev(0), mat, mat, rev(mb0)],
        out_specs=[rev(0)] * 6 + [vec],
        out_shape=[_sds((T, W), F32)] * 5 + [_sds((T, W), BF16), _sds((1, HEAD_DIM), F32)],
        scratch_shapes=[pltpu.VMEM((G, HEAD_DIM, HEAD_DIM), F32)],
        compiler_params=_params(("arbitrary", "arbitrary"), 40 * 2**20),
    )(qn, kn, vn, gc_full, beta_full, proj, gain, o_raw, tm_all, s_all, dmix)


def _sb_terms(z, ahead, first_key):
    lb = jnp.minimum(z, 0.0) - jnp.log(1.0 + jnp.exp(-jnp.abs(z)))
    if ahead is None:
        return None, lb, lb - z
    valid = ahead < -first_key
    return valid, lb, jnp.where(valid, lb - z, 0.0)


def _masked(valid, x):
    return x if valid is None else jnp.where(valid, x, 0.0)


def _sb_attention(qkv, gain, tq_cap=2048):
    T = qkv.shape[0]
    H = SB_HEADS
    B = min(SB_KEYS, T)
    TQ = _tile(T, tq_cap, B)
    per = TQ // B

    assert per % 2 == 0
    n_saved = per * (T // TQ) * (T // TQ + 1) // 2

    def body(q_ref, k_ref, v_ref, gain_ref, o_ref, mix_ref, att_hbm, lb_hbm, att_buf, lb_buf, sems):
        h, i = pl.program_id(0), pl.program_id(1)
        q = q_ref[...]
        upper = (_iota2((B, B), 0) > _iota2((B, B), 1)).astype(BF16)
        ahead = _iota2((TQ, B), 1) - _iota2((TQ, B), 0)
        last = (i + 1) * per - 1
        base = per * (i * (i + 1) // 2)

        def save(slot, pair):
            return (pltpu.make_async_copy(att_buf.at[slot], att_hbm.at[h, pair], sems.at[0, slot]),
                    pltpu.make_async_copy(lb_buf.at[slot], lb_hbm.at[h, pair], sems.at[1, slot]))

        def pair(j, slot, r0, acc, R):
            n = TQ - (r0 or 0)
            top = slice(r0 or 0, TQ)
            rows = pl.ds(pl.multiple_of(j * B, B), B)
            z = _dot(q[top], k_ref[rows, :], _NT) * (HEAD_DIM ** -0.5)
            valid, lb, l1m = _sb_terms(z, None if r0 is None else ahead[top], r0)
            att = _masked(valid, jnp.exp(lb + R[top] + _dot01(l1m, upper, passes=2))).astype(BF16)
            att_buf[slot, pl.ds(TQ - n, n), :] = att
            lb_buf[slot, pl.ds(TQ - n, n), :] = (lb if valid is None else jnp.where(valid, lb, -1e30)).astype(BF16)
            if r0:
                att_buf[slot, pl.ds(0, r0), :] = jnp.zeros((r0, B), BF16)
                lb_buf[slot, pl.ds(0, r0), :] = jnp.full((r0, B), -1e30, BF16)
            for c in save(slot, base + j):
                c.start()
            d_acc, d_R = _dot(att, v_ref[rows, :]), _rowsum(l1m)
            if r0:
                d_acc = jnp.concatenate([jnp.zeros((r0, HEAD_DIM), F32), d_acc], axis=0)
                d_R = jnp.concatenate([jnp.zeros((r0, 1), F32), d_R], axis=0)
            return acc + d_acc, R + d_R

        def step(jj, carry):
            slot = jj % 2

            @pl.when(jj >= 2)
            def _():
                for c in save(slot, 0):
                    c.wait()

            return pair(last - jj, slot, None, *carry)

        carry = (jnp.zeros((TQ, HEAD_DIM), F32), jnp.zeros((TQ, 1), F32))
        for jj in range(per):
            if jj >= 2:
                for c in save(jj % 2, 0):
                    c.wait()
            carry = pair(last - jj, jj % 2, (per - 1 - jj) * B, *carry)
        acc, _ = lax.fori_loop(per, last + 1, step, carry)
        for slot in range(2):
            for c in save(slot, 0):
                c.wait()
        o_ref[...] = acc
        mix_ref[...] = _rms(acc, gain_ref[...]).astype(BF16)

    head = lambda off: pl.BlockSpec((T, HEAD_DIM), lambda h, i: (0, off + h))
    blk = pl.BlockSpec((TQ, HEAD_DIM), lambda h, i: (i, h))
    return pl.pallas_call(
        body, name="sb_attention", grid=(H, T // TQ),
        in_specs=[blk, head(H), head(2 * H), pl.BlockSpec((1, HEAD_DIM), lambda h, i: (0, 0))],
        out_specs=[blk, blk, ANY_SPEC, ANY_SPEC],
        out_shape=[_sds((T, H * HEAD_DIM), F32), _sds((T, H * HEAD_DIM), BF16),
                   _sds((H, n_saved, TQ, B), BF16), _sds((H, n_saved, TQ, B), BF16)],
        scratch_shapes=[pltpu.VMEM((2, TQ, B), BF16), pltpu.VMEM((2, TQ, B), BF16), pltpu.SemaphoreType.DMA((2, 2))],
        compiler_params=_params(("parallel", "arbitrary"), 8 * _nbytes((T, HEAD_DIM), BF16) + 32 * _nbytes((TQ, B), F32)),
    )(qkv, qkv, qkv, gain)


def _sb_attention_bwd(qkv, gain, o_raw, att_all, lb_all, dmix):
    T = qkv.shape[0]
    H = SB_HEADS
    TQ, B = att_all.shape[2:]
    per = TQ // B
    scale = HEAD_DIM ** -0.5

    def body(q_ref, k_ref, v_ref, gain_ref, o_ref, dmix_ref, att_hbm, lb_hbm, dq_ref, dk_ref, dv_ref, dgain_ref,
             att_buf, lb_buf, sems):
        h, i = pl.program_id(0), pl.program_id(1)

        @pl.when(i == 0)
        def _():
            dk_ref[...] = jnp.zeros_like(dk_ref)
            dv_ref[...] = jnp.zeros_like(dv_ref)

        @pl.when((pl.program_id(0) == 0) & (i == 0))
        def _():
            dgain_ref[...] = jnp.zeros_like(dgain_ref)

        q = q_ref[...]
        o = o_ref[...]
        do, dgain = _rms_bwd(dmix_ref[...], o, gain_ref[...])
        dgain_ref[...] += dgain
        do_b = do.astype(BF16)
        before = (_iota2((B, B), 0) < _iota2((B, B), 1)).astype(BF16)
        base = per * (i * (i + 1) // 2)

        def fetch(slot, pair):
            return (pltpu.make_async_copy(att_hbm.at[h, pair], att_buf.at[slot], sems.at[0, slot]),
                    pltpu.make_async_copy(lb_hbm.at[h, pair], lb_buf.at[slot], sems.at[1, slot]))

        for c in fetch(0, base):
            c.start()

        def pair(j, slot, r0, dq, PG):
            top = slice(r0, TQ)
            rows = pl.ds(pl.multiple_of(j * B, B), B)
            kj = k_ref[rows, :]
            att = att_buf[slot, pl.ds(r0, TQ - r0), :]
            sig = jnp.exp(lb_buf[slot, pl.ds(r0, TQ - r0), :].astype(F32))
            G = _dot(do_b[top], v_ref[rows, :], _NT) * att.astype(F32)
            dv_ref[rows, :] += _dot(att, do_b[top], _TN)
            cum = PG[top] + _dot01(G, before, passes=2)
            dz = (G * (1.0 - sig) - sig * cum) * scale
            dk_ref[rows, :] += _dot(dz, q[top], _TN)
            d_dq, d_PG = _dot(dz, kj), _rowsum(G)
            if r0:
                d_dq = jnp.concatenate([jnp.zeros((r0, HEAD_DIM), F32), d_dq], axis=0)
                d_PG = jnp.concatenate([jnp.zeros((r0, 1), F32), d_PG], axis=0)
            return dq + d_dq, PG + d_PG

        def step(j, carry):
            slot = j % 2
            for c in fetch(slot, 0):
                c.wait()
            for c in fetch(1 - slot, base + j + 1):
                c.start()
            return pair(j, slot, 0, *carry)

        carry = lax.fori_loop(0, i * per, step, (jnp.zeros((TQ, HEAD_DIM), F32), jnp.zeros((TQ, 1), F32)))
        for c_blk in range(per):
            slot = c_blk % 2
            for c in fetch(slot, 0):
                c.wait()
            if c_blk + 1 < per:
                for c in fetch(1 - slot, base + i * per + c_blk + 1):
                    c.start()
            carry = pair(i * per + c_blk, slot, c_blk * B, *carry)
        dq_ref[...] = carry[0].astype(BF16)

    head = lambda off: pl.BlockSpec((T, HEAD_DIM), lambda h, i: (0, off + h))
    blk = pl.BlockSpec((TQ, HEAD_DIM), lambda h, i: (i, h))
    vec = pl.BlockSpec((1, HEAD_DIM), lambda h, i: (0, 0))
    return pl.pallas_call(
        body, name="sb_attention_bwd", grid=(H, T // TQ),
        in_specs=[blk, head(H), head(2 * H), vec, blk, blk, ANY_SPEC, ANY_SPEC],
        out_specs=[blk, head(0), head(0), vec],
        out_shape=[_sds((T, H * HEAD_DIM), BF16), _sds((T, H * HEAD_DIM), F32), _sds((T, H * HEAD_DIM), F32),
                   _sds((1, HEAD_DIM), F32)],
        scratch_shapes=[pltpu.VMEM((2, TQ, B), BF16), pltpu.VMEM((2, TQ, B), BF16), pltpu.SemaphoreType.DMA((2, 2))],
        compiler_params=_params(("arbitrary", "arbitrary"), 8 * _nbytes((T, HEAD_DIM), F32) + 32 * _nbytes((TQ, B), F32)),
    )(qkv, qkv, qkv, gain, o_raw, dmix, att_all, lb_all)


def _adamw_math(w, g, m, v):
    m = ADAM_B1 * m + (1.0 - ADAM_B1) * g
    v = ADAM_B2 * v + (1.0 - ADAM_B2) * (g * g)
    m_hat = m / (1.0 - ADAM_B1 ** ADAM_STEP)
    v_hat = v / (1.0 - ADAM_B2 ** ADAM_STEP)
    delta = -ADAM_LR * (m_hat / (jnp.sqrt(v_hat) + ADAM_EPS) + ADAM_WD * w)
    return delta, m, v


def _adamw_sharded(parts, w, m, v, name):
    _, R, C = w.shape
    if R % SUBLANES == 0:
        tr, tc = _tile(R, max(SUBLANES, (2**20 // (4 * C)) // SUBLANES * SUBLANES), SUBLANES), C
    else:
        tr, tc = R, _tile(C, max(LANES, (2**20 // (4 * R)) // LANES * LANES), LANES)

    def body(p_ref, w_ref, m_ref, v_ref, g_ref, d_ref, nm_ref, nv_ref):
        g = p_ref[0].astype(F32)
        for d in range(1, N_DEV):
            g = g + p_ref[d].astype(F32)
        g_ref[...] = g
        d_ref[...], nm_ref[...], nv_ref[...] = _adamw_math(w_ref[...], g, m_ref[...], v_ref[...])

    blk = pl.BlockSpec((None, tr, tc), lambda i, j: (0, i, j))
    return pl.pallas_call(
        body, name=name, grid=(R // tr, C // tc),
        in_specs=[pl.BlockSpec((N_DEV, tr, tc), lambda i, j: (0, i, j)), blk, blk, blk],
        out_specs=[blk] * 4, out_shape=[_sds((1, R, C), F32)] * 4,
        compiler_params=_params(("parallel", "parallel"), 40 * 2**20),
    )(parts, w, m, v)


def _adamw_packed(g, w, m, v):
    def body(g_ref, w_ref, m_ref, v_ref, d_ref, nm_ref, nv_ref):
        d_ref[...], nm_ref[...], nv_ref[...] = _adamw_math(w_ref[...], g_ref[...], m_ref[...], v_ref[...])

    return pl.pallas_call(body, name="adamw_packed", out_shape=[_sds(g.shape, F32)] * 3,
                          compiler_params=_params((), 16 * 2**20))(g, w, m, v)


def _my_place():
    x, y, c = lax.axis_index("x"), lax.axis_index("y"), lax.axis_index("c")
    return x, y, c


def _peer(place, k):
    x, y, c = place
    return (1 - x if k & 4 else x, 1 - y if k & 2 else y, 1 - c if k & 1 else c)


def _index(place):
    x, y, c = place
    return 4 * x + 2 * y + c


HBM_SPEC = pl.BlockSpec(memory_space=pltpu.HBM)


def _all_gather(block, name):
    R, C = block.shape

    def body(x_ref, out_ref, send_sems, recv_sems, local_sem):
        me = _my_place()
        sibling = _peer(me, 1)
        chips = [2, 4, 6]

        def copy(sem, origin, to, src=None):
            slot = out_ref.at[_index(origin)]
            return pltpu.make_async_remote_copy(
                src_ref=slot if src is None else src, dst_ref=slot, send_sem=send_sems.at[sem], recv_sem=recv_sems.at[sem],
                device_id=to, device_id_type=MESH)

        mine = pltpu.make_async_copy(x_ref, out_ref.at[_index(me)], local_sem)
        mine.start()
        first = [copy(0, me, sibling, src=x_ref)] + [copy(1 + n, me, _peer(me, k), src=x_ref) for n, k in enumerate(chips)]
        for cp in first:
            cp.start()
        passed = [copy(4 + n, _peer(me, k), sibling) for n, k in enumerate(chips)]
        for n, k in enumerate(chips):
            copy(1 + n, _peer(me, k), me).wait_recv()
            passed[n].start()
        copy(0, sibling, me).wait_recv()
        for n, k in enumerate(chips):
            copy(4 + n, _peer(sibling, k), me).wait_recv()
        for cp in first + passed:
            cp.wait_send()
        mine.wait()

    return pl.pallas_call(
        body, name=name, in_specs=[HBM_SPEC], out_specs=HBM_SPEC,
        out_shape=_sds((N_DEV, R, C), block.dtype),
        scratch_shapes=[pltpu.SemaphoreType.DMA((7,)), pltpu.SemaphoreType.DMA((7,)), pltpu.SemaphoreType.DMA],
    )(block)


SEM_SPEC = pl.BlockSpec(memory_space=pltpu.SEMAPHORE)
ANY_SPEC = pl.BlockSpec(memory_space=pl.ANY)
_EFFECT = pltpu.SideEffectType.DATAFLOW_SIDE_EFFECTING


def _spread_start(x, per_peer, name, after):
    R, C = x.shape[-2:]

    def body(x_ref, land_ref, after_ref, send_sems, recv_sems, x_thru, land_thru, token):
        me = _my_place()
        for k in range(1, N_DEV):
            to = _peer(me, k)
            pltpu.make_async_remote_copy(
                src_ref=x_ref.at[_index(to)] if per_peer else x_ref, dst_ref=land_ref.at[_index(me)],
                send_sem=send_sems.at[k - 1], recv_sem=recv_sems.at[k - 1], device_id=to, device_id_type=MESH).start()
        token[...] = jnp.zeros_like(token)

    land = lax.empty((N_DEV, R, C), x.dtype)
    send_sems, recv_sems, x_thru, land_thru, token = pl.pallas_call(
        body, name=name,
        out_shape=(pltpu.SemaphoreType.DMA((N_DEV - 1,)), pltpu.SemaphoreType.DMA((N_DEV - 1,)),
                   pltpu.HBM(x.shape, x.dtype), pltpu.HBM(land.shape, land.dtype), _sds((SUBLANES, LANES), F32)),
        in_specs=(HBM_SPEC, HBM_SPEC, ANY_SPEC),
        out_specs=(SEM_SPEC, SEM_SPEC, HBM_SPEC, HBM_SPEC, pl.BlockSpec(memory_space=pltpu.VMEM)),
        input_output_aliases={0: 2, 1: 3},
        compiler_params=pltpu.CompilerParams(has_side_effects=_EFFECT),
    )(pltpu.with_memory_space_constraint(x, pltpu.HBM), pltpu.with_memory_space_constraint(land, pltpu.HBM), after)
    return (send_sems, recv_sems, x_thru, land_thru), token


def _spread_wait(state, per_peer, name, after):
    send_sems, recv_sems, x_thru, land_thru = state

    def body(x_ref, land_ref, send_sems, recv_sems, after_ref, x_dead, got_ref):
        me = _my_place()
        for k in range(1, N_DEV):
            frm = _peer(me, k)
            copy = pltpu.make_async_remote_copy(
                src_ref=x_ref.at[_index(frm)] if per_peer else x_ref, dst_ref=land_ref.at[_index(frm)],
                send_sem=send_sems.at[k - 1], recv_sem=recv_sems.at[k - 1], device_id=frm, device_id_type=MESH)
            copy.wait_send()
            copy.wait_recv()

    x_back, got = pl.pallas_call(
        body, name=name,
        out_shape=(pltpu.HBM(x_thru.shape, x_thru.dtype), pltpu.HBM(land_thru.shape, land_thru.dtype)),
        in_specs=(HBM_SPEC, HBM_SPEC, SEM_SPEC, SEM_SPEC, ANY_SPEC), out_specs=(HBM_SPEC, HBM_SPEC),
        input_output_aliases={0: 0, 1: 1},
        compiler_params=pltpu.CompilerParams(has_side_effects=_EFFECT),
    )(x_thru, land_thru, send_sems, recv_sems, after)
    me = _index(_my_place())
    own = lax.dynamic_index_in_dim(x_back, me, axis=0, keepdims=True) if per_peer else x_back[None]
    return lax.dynamic_update_slice_in_dim(got, own, me, axis=0)


def _all_reduce_packed(vec, after):
    R, L = vec.shape

    def body(x_ref, after_ref, out_ref, buf, send_sems, recv_sems):
        me = _my_place()
        buf[_index(me)] = x_ref[...]
        copies = []
        for k in range(1, N_DEV):
            to = _peer(me, k)
            cp = pltpu.make_async_remote_copy(
                src_ref=x_ref, dst_ref=buf.at[_index(me)],
                send_sem=send_sems.at[k - 1], recv_sem=recv_sems.at[k - 1], device_id=to, device_id_type=MESH)
            cp.start()
            copies.append(cp)
        for k in range(1, N_DEV):
            frm = _peer(me, k)
            pltpu.make_async_remote_copy(
                src_ref=x_ref, dst_ref=buf.at[_index(frm)],
                send_sem=send_sems.at[k - 1], recv_sem=recv_sems.at[k - 1], device_id=frm, device_id_type=MESH).wait_recv()
        for cp in copies:
            cp.wait_send()
        acc = buf[0]
        for d in range(1, N_DEV):
            acc = acc + buf[d]
        out_ref[...] = acc

    vm = pl.BlockSpec(memory_space=pltpu.VMEM)
    return pl.pallas_call(
        body, name="all_reduce_packed", in_specs=[vm, ANY_SPEC], out_specs=vm, out_shape=_sds((R, L), F32),
        scratch_shapes=[pltpu.VMEM((N_DEV, R, L), F32), pltpu.SemaphoreType.DMA((7,)), pltpu.SemaphoreType.DMA((7,))],
        compiler_params=pltpu.CompilerParams(vmem_limit_bytes=32 * 2**20),
    )(vec, after)


def _pack(arrays):
    rows = []
    for a in arrays:
        f = a.reshape(-1).astype(F32)
        pad = (-f.shape[0]) % LANES
        rows.append(jnp.pad(f, (0, pad)).reshape(-1, LANES))
    out = jnp.concatenate(rows, axis=0)
    return jnp.pad(out, ((0, (-out.shape[0]) % SUBLANES), (0, 0)))


def _unpack(packed, shapes):
    out, r = [], 0
    for s in shapes:
        n = math.prod(s)
        nr = -(-n // LANES)
        out.append(packed[r:r + nr].reshape(-1)[:n].reshape(s))
        r += nr
    return out


def _row_blocks(g):
    R, C = g.shape
    return g.astype(BF16).reshape(N_DEV, R // N_DEV, C)


def kernel(x, w_in, sb_out_gain, dn_conv_w, dn_a_log, dn_dt_bias, dn_out_gain, w_out, ln_mix_pre, ln_mix_post, w_up, ffn_conv_w, ffn_conv_b, w_down, ln_ffn_pre, ln_ffn_post, loss_target, m_w_in, m_sb_out_gain, m_dn_conv_w, m_dn_a_log, m_dn_dt_bias, m_dn_out_gain, m_w_out, m_ln_mix_pre, m_ln_mix_post, m_w_up, m_ffn_conv_w, m_ffn_conv_b, m_w_down, m_ln_ffn_pre, m_ln_ffn_post, v_w_in, v_sb_out_gain, v_dn_conv_w, v_dn_a_log, v_dn_dt_bias, v_dn_out_gain, v_w_out, v_ln_mix_pre, v_ln_mix_post, v_w_up, v_ffn_conv_w, v_ffn_conv_b, v_w_down, v_ln_ffn_pre, v_ln_ffn_post):
    T, D = x.shape[1], x.shape[2]
    SBW = SB_HEADS * HEAD_DIM
    DNW = DN_HEADS * HEAD_DIM
    in_cols = 3 * SBW + 4 * DNW + 2 * DN_HEADS
    main_cols = 3 * SBW + 4 * DNW
    in_pad = main_cols + LANES
    qkv0, z0 = 3 * SBW, 3 * SBW + 3 * DNW
    gate_block = main_cols // LANES
    x2, tgt = x[0], loss_target[0]

    g_in = _all_gather(jnp.swapaxes(w_in[0], 0, 1).astype(BF16), "gather_w_in")
    small_w = _all_gather(_pack([dn_conv_w[0], ffn_conv_w[0]]), "gather_conv_w")
    st_out, tok = _spread_start(w_out[0].astype(BF16), False, "gather_w_out_start", g_in)
    st_up, tok = _spread_start(w_up[0].astype(BF16), False, "gather_w_up_start", tok)
    st_down, tok_gather = _spread_start(w_down[0].astype(BF16), False, "gather_w_down_start", tok)
    w_in_t = jnp.pad(g_in.reshape(in_cols, D), ((0, in_pad - in_cols), (0, 0)))
    parts = [_unpack(small_w[d], [dn_conv_w.shape[1:], ffn_conv_w.shape[1:]]) for d in range(N_DEV)]
    dn_cw = jnp.concatenate([p[0] for p in parts], axis=1)
    ffn_cw = jnp.concatenate([p[1] for p in parts], axis=1)
    lane_pad = lambda a, off: jnp.pad(a, ((0, 0), (off, LANES - off - a.shape[1])))
    a_log_l, dt_bias_l = lane_pad(dn_a_log, DN_HEADS), lane_pad(dn_dt_bias, DN_HEADS)

    xn = _norm_in(x2, ln_mix_pre)
    proj = _matmul(xn, w_in_t, "nt", F32, "proj_in", tm_cap=512, tn_cap=2432, after=tok_gather)
    sb_qkv = proj[:, :3 * SBW].astype(BF16)
    o_sb, mix_sb, sb_att, sb_lb = _sb_attention(sb_qkv, sb_out_gain)
    qn = _dn_branch(proj, qkv0, dn_cw, 0, True, HEAD_DIM ** -0.5)
    kn = _dn_branch(proj, qkv0 + DNW, dn_cw, DNW, True, 1.0)
    vn = _dn_branch(proj, qkv0 + 2 * DNW, dn_cw, 2 * DNW, False, 1.0)
    gc_full, beta_full = _dn_gates(proj, gate_block, a_log_l, dt_bias_l)
    o_dn, mix_dn, tm_all, s_all = _dn_scan(qn, kn, vn, gc_full, beta_full, proj, z0, dn_out_gain)
    mix = jnp.concatenate([mix_sb, mix_dn], axis=1)
    w_out_f = _spread_wait(st_out, False, "gather_w_out_wait", mix).reshape(w_out.shape[1] * N_DEV, D)
    m = _matmul(mix, w_out_f, "nn", F32, "proj_out")
    h, hn = _mix_residual(x2, m, ln_mix_post, ln_ffn_pre)
    w_up_cut = _spread_wait(st_up, False, "gather_w_up_wait", hn)
    u = _matmul(hn, w_up_cut, "nn", F32, "ffn_up", tn_cap=w_up.shape[2], b_cut=True)
    act = _ffn_act(u, ffn_cw, ffn_conv_b)
    w_down_f = _spread_wait(st_down, False, "gather_w_down_wait", act).reshape(w_down.shape[1] * N_DEV, D)
    f = _matmul(act, w_down_f, "nn", F32, "ffn_down", tk_cap=2816)
    dy, df, d_ln_ffn_post, loss_part = _loss_head(h, f, ln_ffn_post, tgt)

    d_w_down = _matmul(act, df, "tn", BF16, "grad_w_down")
    st_xd, tok = _spread_start(_row_blocks(d_w_down), True, "exchange_w_down_start", loss_part)
    da = _matmul(df, w_down_f, "nt", F32, "bwd_ffn_down", after=tok)
    du, d_ffn_cwb = _ffn_act_bwd(u, ffn_cw, ffn_conv_b, da)
    d_ffn_cwb = jnp.concatenate([d_ffn_cwb[0], d_ffn_cwb[1]], axis=1)
    d_w_up_cut = _matmul(hn, du, "tn", BF16, "grad_w_up", b_cut=True, out_cut=True)
    st_xu, tok = _spread_start(d_w_up_cut, True, "exchange_w_up_start", d_ffn_cwb)
    dhn = _matmul(du, w_up_cut, "nt", F32, "bwd_ffn_up", after=tok, a_cut=True, b_cut=True)
    dh, dm, d_ln_ffn_pre, d_ln_mix_post = _ffn_residual_bwd(dy, dhn, h, ln_ffn_pre, m, ln_mix_post)

    d_w_out = _matmul(mix, dm, "tn", BF16, "grad_w_out")
    st_xo, tok = _spread_start(_row_blocks(d_w_out), True, "exchange_w_out_start", d_ln_ffn_pre)
    dmix = _matmul(dm, w_out_f, "nt", F32, "bwd_proj_out", after=tok)
    dq_sb, dk_sb, dv_sb, d_sb_gain = _sb_attention_bwd(sb_qkv, sb_out_gain, o_sb, sb_att, sb_lb, dmix)
    dqn, dkn, dvn, dgc_full, dbeta_full, dz, d_dn_gain = _dn_scan_bwd(
        qn, kn, vn, gc_full, beta_full, proj, z0, dn_out_gain, o_dn, tm_all, s_all, dmix, SBW)
    du_q, dcw_q = _dn_branch_bwd(proj, qkv0, dn_cw, 0, True, HEAD_DIM ** -0.5, dqn)
    du_k, dcw_k = _dn_branch_bwd(proj, qkv0 + DNW, dn_cw, DNW, True, 1.0, dkn)
    du_v, dcw_v = _dn_branch_bwd(proj, qkv0 + 2 * DNW, dn_cw, 2 * DNW, False, 1.0, dvn)
    dba, d_a_log_l, d_dt_bias_l = _dn_gates_bwd(proj, gate_block, a_log_l, dt_bias_l, dgc_full, dbeta_full)
    dproj = jnp.concatenate([dq_sb, dk_sb.astype(BF16), dv_sb.astype(BF16), du_q, du_k, du_v, dz, dba], axis=1)
    d_w_in_t = _matmul(dproj, xn, "tn", BF16, "grad_w_in", tm_cap=2432, tn_cap=512, tk_cap=1024)
    d_w_in_cut = d_w_in_t[:in_cols].reshape(N_DEV, in_cols // N_DEV, D)
    st_xi, tok = _spread_start(d_w_in_cut, True, "exchange_w_in_start", d_sb_gain)
    dxn = _matmul(dproj, w_in_t, "nn", F32, "bwd_proj_in", tk_cap=2432, after=tok)
    grad_x, d_ln_mix_pre = _input_bwd(dh, dxn, x2, ln_mix_pre)

    big = {}
    after = grad_x
    for n, st, w_, m_, v_ in [("w_down", st_xd, w_down, m_w_down, v_w_down), ("w_up", st_xu, w_up, m_w_up, v_w_up),
                              ("w_out", st_xo, w_out, m_w_out, v_w_out)]:
        got = _spread_wait(st, True, "exchange_" + n + "_wait", after)
        big[n] = _adamw_sharded(got, w_, m_, v_, "adamw_" + n)
        after = big[n][1]

    d_dn_cw = jnp.concatenate([dcw_q[:SHORT_CONV], dcw_k[:SHORT_CONV], dcw_v[:SHORT_CONV]], axis=1)
    small = [loss_part[:, :1], d_sb_gain, d_a_log_l[:, DN_HEADS:2 * DN_HEADS], d_dt_bias_l[:, DN_HEADS:2 * DN_HEADS], d_dn_gain,
             d_ln_mix_pre, d_ln_mix_post, d_ffn_cwb[FFN_CONV:FFN_CONV + 1], d_ln_ffn_pre, d_ln_ffn_post,
             d_dn_cw, d_ffn_cwb[:FFN_CONV]]
    shapes = [a.shape for a in small]
    red = _unpack(_all_reduce_packed(_pack(small), after), shapes)
    loss = red[0].reshape(())
    me = _index(_my_place())
    g_dn_cw = lax.dynamic_slice_in_dim(red[10], me * dn_conv_w.shape[2], dn_conv_w.shape[2], axis=1)
    g_ffn_cw = lax.dynamic_slice_in_dim(red[11], me * ffn_conv_w.shape[2], ffn_conv_w.shape[2], axis=1)
    names = ["sb_out_gain", "dn_conv_w", "dn_a_log", "dn_dt_bias", "dn_out_gain", "ln_mix_pre", "ln_mix_post",
             "ffn_conv_w", "ffn_conv_b", "ln_ffn_pre", "ln_ffn_post"]
    g_small = dict(sb_out_gain=red[1], dn_conv_w=g_dn_cw[None], dn_a_log=red[2], dn_dt_bias=red[3], dn_out_gain=red[4],
                   ln_mix_pre=red[5], ln_mix_post=red[6], ffn_conv_w=g_ffn_cw[None], ffn_conv_b=red[7],
                   ln_ffn_pre=red[8], ln_ffn_post=red[9])
    w_small = dict(sb_out_gain=sb_out_gain, dn_conv_w=dn_conv_w, dn_a_log=dn_a_log, dn_dt_bias=dn_dt_bias,
                   dn_out_gain=dn_out_gain, ln_mix_pre=ln_mix_pre, ln_mix_post=ln_mix_post, ffn_conv_w=ffn_conv_w,
                   ffn_conv_b=ffn_conv_b, ln_ffn_pre=ln_ffn_pre, ln_ffn_post=ln_ffn_post)
    m_small = dict(sb_out_gain=m_sb_out_gain, dn_conv_w=m_dn_conv_w, dn_a_log=m_dn_a_log, dn_dt_bias=m_dn_dt_bias,
                   dn_out_gain=m_dn_out_gain, ln_mix_pre=m_ln_mix_pre, ln_mix_post=m_ln_mix_post, ffn_conv_w=m_ffn_conv_w,
                   ffn_conv_b=m_ffn_conv_b, ln_ffn_pre=m_ln_ffn_pre, ln_ffn_post=m_ln_ffn_post)
    v_small = dict(sb_out_gain=v_sb_out_gain, dn_conv_w=v_dn_conv_w, dn_a_log=v_dn_a_log, dn_dt_bias=v_dn_dt_bias,
                   dn_out_gain=v_dn_out_gain, ln_mix_pre=v_ln_mix_pre, ln_mix_post=v_ln_mix_post, ffn_conv_w=v_ffn_conv_w,
                   ffn_conv_b=v_ffn_conv_b, ln_ffn_pre=v_ln_ffn_pre, ln_ffn_post=v_ln_ffn_post)
    sshapes = [w_small[n].shape for n in names]
    upd = _adamw_packed(_pack([g_small[n] for n in names]), _pack([w_small[n] for n in names]),
                        _pack([m_small[n] for n in names]), _pack([v_small[n] for n in names]))
    d_small, nm_small, nv_small = [dict(zip(names, _unpack(p, sshapes))) for p in upd]

    got = _spread_wait(st_xi, True, "exchange_w_in_wait", d_small["ln_ffn_post"])
    flip = lambda a: jnp.swapaxes(a, 1, 2)
    big["w_in"] = [flip(a) for a in _adamw_sharded(got, flip(w_in), flip(m_w_in), flip(v_w_in), "adamw_w_in")]

    order = ["w_in", "sb_out_gain", "dn_conv_w", "dn_a_log", "dn_dt_bias", "dn_out_gain", "w_out", "ln_mix_pre",
             "ln_mix_post", "w_up", "ffn_conv_w", "ffn_conv_b", "w_down", "ln_ffn_pre", "ln_ffn_post"]
    pick = lambda n, i: big[n][i] if n in big else [g_small, d_small, nm_small, nv_small][i][n].reshape(w_small[n].shape)
    return (loss, grad_x[None], *[pick(n, 0) for n in order], *[pick(n, 1) for n in order],
            *[pick(n, 2) for n in order], *[pick(n, 3) for n in order])
```

```python
import functools
import math

import jax
import jax.numpy as jnp
from jax import lax
from jax.experimental import pallas as pl
from jax.experimental.pallas import tpu as pltpu

F32 = jnp.float32
BF16 = jnp.bfloat16

N_DEV = 8
HEAD_DIM = 128
SB_HEADS = 8
DN_HEADS = 8
DN_CHUNK = 128
DN_GROUP = 8
INV_BLOCK = 16
SB_KEYS = 256
SHORT_CONV = 4
FFN_CONV = 3
EPS = 1e-6
LANES = 128
SUBLANES = 8
VMEM_CAP = 56 * 2**20

ADAM_LR = 0.001
ADAM_B1 = 0.9
ADAM_B2 = 0.999
ADAM_EPS = 1e-08
ADAM_WD = 0.01
ADAM_STEP = 10

MESH = pl.DeviceIdType.MESH

assert HEAD_DIM == DN_CHUNK == LANES


def _tile(n, cap, mult):
    if n <= cap:
        return n
    t = (cap // mult) * mult
    while t >= mult:
        if n % t == 0:
            return t
        t -= mult
    raise ValueError(f"no tile for {n} under {cap} in multiples of {mult}")


def _params(sem, vmem_bytes):
    limit = int(min(VMEM_CAP, max(vmem_bytes, 16 * 2**20)))
    if not sem:
        return pltpu.CompilerParams(vmem_limit_bytes=limit)
    return pltpu.CompilerParams(dimension_semantics=sem, vmem_limit_bytes=limit)


def _nbytes(shape, dtype):
    return math.prod(shape) * jnp.dtype(dtype).itemsize


_NN = (((1,), (0,)), ((), ()))
_NT = (((1,), (1,)), ((), ()))
_TN = (((0,), (0,)), ((), ()))


def _batched(dims, ndim):
    if ndim == 2:
        return dims
    (ca,), (cb,) = dims[0]
    return (((ca + 1,), (cb + 1,)), ((0,), (0,)))


def _dot(a, b, dims=_NN):
    return lax.dot_general(a.astype(BF16), b.astype(BF16), _batched(dims, a.ndim), preferred_element_type=F32)


def _split2(x):
    hi = x.astype(BF16)
    lo = (x - hi.astype(F32)).astype(BF16)
    return hi, lo


def _split3(x):
    hi = x.astype(BF16)
    r = x - hi.astype(F32)
    mid = r.astype(BF16)
    lo = (r - mid.astype(F32)).astype(BF16)
    return hi, mid, lo


def _dot01(x, m01, passes=3):
    parts = _split3(x) if passes == 3 else _split2(x)
    out = None
    for p in parts:
        t = lax.dot_general(p, m01, _NN, preferred_element_type=F32)
        out = t if out is None else out + t
    return out


def _dot01_left(m01, x, passes=3):
    parts = _split3(x) if passes == 3 else _split2(x)
    out = None
    for p in parts:
        t = lax.dot_general(m01, p, _NN, preferred_element_type=F32)
        out = t if out is None else out + t
    return out


def _mm3(a, b, dims=_NN):
    ah, al = _split2(a)
    bh, bl = _split2(b)
    d = functools.partial(lax.dot_general, dimension_numbers=_batched(dims, a.ndim), preferred_element_type=F32)
    return d(ah, bh) + (d(ah, bl) + d(al, bh))


def _rowsum(x):
    return jnp.sum(x, axis=-1, keepdims=True)


def _t(x):
    return jnp.swapaxes(x, -1, -2)


def _sigmoid(x):
    return 1.0 / (1.0 + jnp.exp(-x))


def _softplus(x):
    return jnp.maximum(x, 0.0) + jnp.log(1.0 + jnp.exp(-jnp.abs(x)))


def _silu(x):
    return x * _sigmoid(x)


def _silu_grad(x):
    s = _sigmoid(x)
    return s * (1.0 + x * (1.0 - s))


_GELU_C = math.sqrt(2.0 / math.pi)


def _gelu(x):
    return 0.5 * x * (1.0 + jnp.tanh(_GELU_C * (x + 0.044715 * x * x * x)))


def _gelu_and_grad(x):
    x2 = x * x
    th = jnp.tanh(_GELU_C * (x + 0.044715 * x2 * x))
    half = 0.5 * (1.0 + th)
    return x * half, half + 0.5 * x * (1.0 - th * th) * (_GELU_C * (1.0 + 3.0 * 0.044715 * x2))


def _rms(x, g):
    r = lax.rsqrt(jnp.mean(x * x, axis=-1, keepdims=True) + EPS)
    return x * r * g


def _rms_bwd(dy, x, g):
    r = lax.rsqrt(jnp.mean(x * x, axis=-1, keepdims=True) + EPS)
    xh = x * r
    gdy = dy * g
    dx = r * (gdy - xh * jnp.mean(gdy * xh, axis=-1, keepdims=True))
    return dx, jnp.sum(dy * xh, axis=-2, keepdims=True)


def _iota2(shape, axis):
    return lax.broadcasted_iota(jnp.int32, shape, axis)


def _shift_down(cur, prev8, k):
    n = cur.shape[0]
    r = pltpu.roll(cur, k, 0)
    pr = pltpu.roll(prev8, k, 0)
    head = jnp.where(_iota2(pr.shape, 0) < k, pr, r[0:SUBLANES])
    if n == SUBLANES:
        return head
    return jnp.concatenate([head, r[SUBLANES:]], axis=0)


def _shift_up(cur, next8, k):
    n = cur.shape[0]
    r = pltpu.roll(cur, n - k, 0)
    nr = pltpu.roll(next8, SUBLANES - k, 0)
    tail = jnp.where(_iota2(nr.shape, 0) >= SUBLANES - k, nr, r[n - SUBLANES:])
    if n == SUBLANES:
        return tail
    return jnp.concatenate([r[:n - SUBLANES], tail], axis=0)


def _causal_conv(cur, prev8, w_ref, taps):
    out = cur * w_ref[taps - 1:taps, :]
    for j in range(taps - 1):
        out = out + _shift_down(cur, prev8, taps - 1 - j) * w_ref[j:j + 1, :]
    return out


def _anti_conv(cur, next8, w_ref, taps):
    out = cur * w_ref[taps - 1:taps, :]
    for j in range(taps - 1):
        out = out + _shift_up(cur, next8, taps - 1 - j) * w_ref[j:j + 1, :]
    return out


def _matmul(a, b, mode, out_dtype, name, tm_cap=1024, tn_cap=1024, tk_cap=2048, after=None,
            a_cut=False, b_cut=False, out_cut=False):
    a_shard = a.shape[2] if a_cut else None
    b_shard = b.shape[2] if b_cut else None
    a_full = (a.shape[1], a.shape[0] * a_shard) if a_cut else a.shape
    b_full = (b.shape[1], b.shape[0] * b_shard) if b_cut else b.shape
    assert not (a_cut and mode == "tn")
    if mode == "nn":
        (M, K), N = a_full, b_full[1]
    elif mode == "nt":
        (M, K), N = a_full, b_full[0]
    else:
        (K, M), N = a_full, b_full[1]
    n_unit = b_shard if (b_cut and mode != "nt") else N
    k_unit = math.gcd(a_shard or K, b_shard if (b_cut and mode == "nt") else K)
    tm = _tile(M, tm_cap, LANES)
    tn = N // N_DEV if out_cut else _tile(n_unit, tn_cap, LANES)
    tk = _tile(k_unit, tk_cap, LANES)
    assert n_unit % tn == 0 and k_unit % tk == 0
    nk = K // tk
    dims = {"nn": _NN, "nt": _NT, "tn": _TN}[mode]
    if a_cut:
        pa = a_shard // tk
        a_spec = pl.BlockSpec((None, tm, tk), lambda i, j, k: (k // pa, i, k % pa))
    elif mode == "tn":
        a_spec = pl.BlockSpec((tk, tm), lambda i, j, k: (k, i))
    else:
        a_spec = pl.BlockSpec((tm, tk), lambda i, j, k: (i, k))
    if b_cut and mode == "nt":
        pb = b_shard // tk
        b_spec = pl.BlockSpec((None, tn, tk), lambda i, j, k: (k // pb, j, k % pb))
    elif b_cut:
        pb = b_shard // tn
        b_spec = pl.BlockSpec((None, tk, tn), lambda i, j, k: (j // pb, k, j % pb))
    elif mode == "nt":
        b_spec = pl.BlockSpec((tn, tk), lambda i, j, k: (j, k))
    else:
        b_spec = pl.BlockSpec((tk, tn), lambda i, j, k: (k, j))
    if out_cut:
        out_spec, out_shape = pl.BlockSpec((None, tm, tn), lambda i, j, k: (j, i, 0)), (N_DEV, M, tn)
    else:
        out_spec, out_shape = pl.BlockSpec((tm, tn), lambda i, j, k: (i, j)), (M, N)

    def body(a_ref, b_ref, *rest):
        if nk == 1:
            rest[-1][...] = lax.dot_general(a_ref[...], b_ref[...], dims, preferred_element_type=F32).astype(rest[-1].dtype)
            return
        o_ref, acc_ref = rest[-2:]
        k = pl.program_id(2)

        @pl.when(k == 0)
        def _():
            acc_ref[...] = jnp.zeros_like(acc_ref)

        acc_ref[...] += lax.dot_general(a_ref[...], b_ref[...], dims, preferred_element_type=F32)

        @pl.when(k == nk - 1)
        def _():
            o_ref[...] = acc_ref[...].astype(o_ref.dtype)

    vmem = 2 * (_nbytes((tm, tk), a.dtype) + _nbytes((tk, tn), b.dtype) + _nbytes((tm, tn), out_dtype)) + _nbytes((tm, tn), F32)
    vmem += _nbytes((tm, tn), F32) + (2 * _nbytes((tm, tk), a.dtype) if mode == "tn" else 0)
    tokens = [] if after is None else [after]
    return pl.pallas_call(
        body, name=name, grid=(M // tm, N // tn, nk),
        in_specs=[a_spec, b_spec] + [pl.BlockSpec(t.shape, lambda i, j, k: (0, 0)) for t in tokens],
        out_specs=out_spec,
        out_shape=jax.ShapeDtypeStruct(out_shape, out_dtype),
        scratch_shapes=[] if nk == 1 else [pltpu.VMEM((tm, tn), F32)],
        compiler_params=_params(("parallel", "parallel", "arbitrary"), vmem + 4 * 2**20),
    )(a, b, *tokens)


def _row_call(body, name, T, D, ins, outs, tr, acc_outs=()):
    def spec(a, kind):
        if kind == "row":
            return pl.BlockSpec((tr, a.shape[1]), lambda i: (i, 0))
        return pl.BlockSpec(a.shape, lambda i: (0, 0))
    in_specs = [spec(a, k) for a, k in ins]
    out_specs = [spec(a, k) for a, k in outs] + [spec(a, "vec") for a in acc_outs]
    out_shape = [a for a, _ in outs] + list(acc_outs)
    vmem = 2 * sum(_nbytes((tr, a.shape[1]) if k == "row" else a.shape, a.dtype) for a, k in list(ins) + list(outs))
    return pl.pallas_call(
        body, name=name, grid=(T // tr,), in_specs=in_specs, out_specs=out_specs, out_shape=out_shape,
        compiler_params=_params(("arbitrary",), 3 * vmem + 8 * 2**20),
    )(*[a for a, _ in ins])


def _sds(shape, dtype):
    return jax.ShapeDtypeStruct(shape, dtype)


def _accumulate(ref, val):
    @pl.when(pl.program_id(0) == 0)
    def _():
        ref[...] = jnp.zeros_like(ref)
    ref[...] += val


def _norm_in(x, g):
    T, D = x.shape

    def body(x_ref, g_ref, o_ref):
        o_ref[...] = _rms(x_ref[...], g_ref[...]).astype(BF16)

    return _row_call(body, "norm_in", T, D, [(x, "row"), (g, "vec")], [(_sds((T, D), BF16), "row")], _tile(T, 256, 16))[0]


def _mix_residual(x, m, g_post, g_pre):
    T, D = x.shape

    def body(x_ref, m_ref, gp_ref, gn_ref, h_ref, hn_ref):
        h = x_ref[...] + _rms(m_ref[...], gp_ref[...])
        h_ref[...] = h
        hn_ref[...] = _rms(h, gn_ref[...]).astype(BF16)

    return _row_call(body, "mix_residual", T, D, [(x, "row"), (m, "row"), (g_post, "vec"), (g_pre, "vec")],
                     [(_sds((T, D), F32), "row"), (_sds((T, D), BF16), "row")], _tile(T, 256, 16))


def _loss_head(h, f, g_post, target):
    T, D = h.shape

    def body(h_ref, f_ref, g_ref, t_ref, dy_ref, df_ref, dg_ref, loss_ref):
        f = f_ref[...]
        g = g_ref[...]
        diff = h_ref[...] + _rms(f, g) - t_ref[...]
        dy = diff * (1.0 / D)
        dy_ref[...] = dy
        df, dg = _rms_bwd(dy, f, g)
        df_ref[...] = df.astype(BF16)
        _accumulate(dg_ref, dg)
        _accumulate(loss_ref, jnp.full((1, LANES), 0.5 / D, F32) * jnp.sum(diff * diff))

    return _row_call(body, "loss_head", T, D, [(h, "row"), (f, "row"), (g_post, "vec"), (target, "row")],
                     [(_sds((T, D), F32), "row"), (_sds((T, D), BF16), "row")], _tile(T, 256, 16),
                     acc_outs=[_sds((1, D), F32), _sds((1, LANES), F32)])


def _ffn_residual_bwd(dy, dhn, h, g_pre, m, g_post):
    T, D = h.shape

    def body(dy_ref, dhn_ref, h_ref, gn_ref, m_ref, gp_ref, dh_ref, dm_ref, dgn_ref, dgp_ref):
        dhh, dgn = _rms_bwd(dhn_ref[...], h_ref[...], gn_ref[...])
        dh = dy_ref[...] + dhh
        dh_ref[...] = dh
        dm, dgp = _rms_bwd(dh, m_ref[...], gp_ref[...])
        dm_ref[...] = dm.astype(BF16)
        _accumulate(dgn_ref, dgn)
        _accumulate(dgp_ref, dgp)

    return _row_call(body, "ffn_residual_bwd", T, D,
                     [(dy, "row"), (dhn, "row"), (h, "row"), (g_pre, "vec"), (m, "row"), (g_post, "vec")],
                     [(_sds((T, D), F32), "row"), (_sds((T, D), BF16), "row")], _tile(T, 128, 16),
                     acc_outs=[_sds((1, D), F32), _sds((1, D), F32)])


def _input_bwd(dh, dxn, x, g):
    T, D = x.shape

    def body(dh_ref, dxn_ref, x_ref, g_ref, dx_ref, dg_ref):
        dx, dg = _rms_bwd(dxn_ref[...], x_ref[...], g_ref[...])
        dx_ref[...] = dh_ref[...] + dx
        _accumulate(dg_ref, dg)

    return _row_call(body, "input_bwd", T, D, [(dh, "row"), (dxn, "row"), (x, "row"), (g, "vec")],
                     [(_sds((T, D), F32), "row")], _tile(T, 256, 16), acc_outs=[_sds((1, D), F32)])


def _ffn_act(u, conv_w, conv_b):
    T, F2 = u.shape
    F = F2 // 2
    tc = _tile(F, 512, LANES)
    tr = _tile(T, 512, SUBLANES)
    nc = F // tc
    r8 = tr // SUBLANES

    def body(ug_ref, ugp_ref, uv_ref, uvp_ref, wg_ref, wv_ref, bg_ref, bv_ref, a_ref):
        first = pl.program_id(1) == 0
        cg = _causal_conv(ug_ref[...], jnp.where(first, 0.0, ugp_ref[...]), wg_ref, FFN_CONV) + bg_ref[...]
        cv = _causal_conv(uv_ref[...], jnp.where(first, 0.0, uvp_ref[...]), wv_ref, FFN_CONV) + bv_ref[...]
        a_ref[...] = (_gelu(cg) * cv).astype(BF16)

    cur = lambda off: pl.BlockSpec((tr, tc), lambda j, i: (i, j + off))
    prev = lambda off: pl.BlockSpec((SUBLANES, tc), lambda j, i: (jnp.maximum(i * r8 - 1, 0), j + off))
    wsp = lambda off: pl.BlockSpec((FFN_CONV, tc), lambda j, i: (0, j + off))
    bsp = lambda off: pl.BlockSpec((1, tc), lambda j, i: (0, j + off))
    return pl.pallas_call(
        body, name="ffn_act", grid=(nc, T // tr),
        in_specs=[cur(0), prev(0), cur(nc), prev(nc), wsp(0), wsp(nc), bsp(0), bsp(nc)],
        out_specs=pl.BlockSpec((tr, tc), lambda j, i: (i, j)),
        out_shape=_sds((T, F), BF16),
        compiler_params=_params(("parallel", "arbitrary"), 12 * _nbytes((tr, tc), F32) + 8 * 2**20),
    )(u, u, u, u, conv_w, conv_w, conv_b, conv_b)


def _ffn_act_bwd(u, conv_w, conv_b, da):
    T, F2 = u.shape
    F = F2 // 2
    tc = _tile(F, 512, LANES)
    tr = _tile(T, 512, SUBLANES)
    nc = F // tc
    r8 = tr // SUBLANES
    n8 = T // SUBLANES
    K = FFN_CONV

    def body(ug_ref, ugp_ref, ugn_ref, uv_ref, uvp_ref, uvn_ref, da_ref, dan_ref,
             wg_ref, wv_ref, bg_ref, bv_ref, du_ref, dwb_ref):
        i = pl.program_id(1)
        first = i == 0
        last = i == pl.num_programs(1) - 1

        def dconv(ug, ug_prev, uv, uv_prev, da_):
            cg = _causal_conv(ug, ug_prev, wg_ref, K) + bg_ref[...]
            cv = _causal_conv(uv, uv_prev, wv_ref, K) + bv_ref[...]
            act, act_grad = _gelu_and_grad(cg)
            return da_ * cv * act_grad, da_ * act

        ug, uv = ug_ref[...], uv_ref[...]
        ug_prev, uv_prev = jnp.where(first, 0.0, ugp_ref[...]), jnp.where(first, 0.0, uvp_ref[...])
        dcg, dcv = dconv(ug, ug_prev, uv, uv_prev, da_ref[...])
        dcgn, dcvn = dconv(ugn_ref[...], ug[tr - SUBLANES:], uvn_ref[...], uv[tr - SUBLANES:], dan_ref[...])
        du_ref[0] = _anti_conv(dcg, jnp.where(last, 0.0, dcgn), wg_ref, K).astype(BF16)
        du_ref[1] = _anti_conv(dcv, jnp.where(last, 0.0, dcvn), wv_ref, K).astype(BF16)

        @pl.when(first)
        def _():
            dwb_ref[...] = jnp.zeros_like(dwb_ref)

        for half, (dc, uo, uo_prev) in enumerate([(dcg, ug, ug_prev), (dcv, uv, uv_prev)]):
            rows = [jnp.sum(dc * _shift_down(uo, uo_prev, K - 1 - t), axis=0, keepdims=True) for t in range(K - 1)]
            rows += [jnp.sum(dc * uo, axis=0, keepdims=True), jnp.sum(dc, axis=0, keepdims=True)]
            rows += [jnp.zeros_like(rows[0])] * (SUBLANES - len(rows))
            dwb_ref[half] += jnp.concatenate(rows, axis=0)

    cur = lambda off: pl.BlockSpec((tr, tc), lambda j, i: (i, j + off))
    prev = lambda off: pl.BlockSpec((SUBLANES, tc), lambda j, i: (jnp.maximum(i * r8 - 1, 0), j + off))
    nxt = lambda off: pl.BlockSpec((SUBLANES, tc), lambda j, i: (jnp.minimum((i + 1) * r8, n8 - 1), j + off))
    wsp = lambda off: pl.BlockSpec((K, tc), lambda j, i: (0, j + off))
    bsp = lambda off: pl.BlockSpec((1, tc), lambda j, i: (0, j + off))
    return pl.pallas_call(
        body, name="ffn_act_bwd", grid=(nc, T // tr),
        in_specs=[cur(0), prev(0), nxt(0), cur(nc), prev(nc), nxt(nc), cur(0), nxt(0), wsp(0), wsp(nc), bsp(0), bsp(nc)],
        out_specs=[pl.BlockSpec((2, tr, tc), lambda j, i: (0, i, j)), pl.BlockSpec((2, SUBLANES, tc), lambda j, i: (0, 0, j))],
        out_shape=[_sds((2, T, F), BF16), _sds((2, SUBLANES, F), F32)],
        compiler_params=_params(("parallel", "arbitrary"), 24 * _nbytes((tr, tc), F32) + 8 * 2**20),
    )(u, u, u, u, u, u, da, da, conv_w, conv_w, conv_b, conv_b)


def _l2norm(s, scale):
    return s * (lax.rsqrt(jnp.sum(s * s, axis=-1, keepdims=True) + EPS) * scale)


def _dn_branch(proj, col0, conv_w, wcol0, l2, scale):
    T = proj.shape[0]
    W = DN_HEADS * HEAD_DIM
    tr = _tile(T, 2048, SUBLANES)
    r8 = tr // SUBLANES
    cb0, wb0 = col0 // HEAD_DIM, wcol0 // HEAD_DIM

    def body(u_ref, up_ref, w_ref, o_ref):
        first = pl.program_id(1) == 0
        s = _silu(_causal_conv(u_ref[...], jnp.where(first, 0.0, up_ref[...]), w_ref, SHORT_CONV))
        o_ref[...] = _l2norm(s, scale) if l2 else s

    return pl.pallas_call(
        body, name=f"dn_branch_{col0}", grid=(DN_HEADS, T // tr),
        in_specs=[pl.BlockSpec((tr, HEAD_DIM), lambda h, i: (i, cb0 + h)),
                  pl.BlockSpec((SUBLANES, HEAD_DIM), lambda h, i: (jnp.maximum(i * r8 - 1, 0), cb0 + h)),
                  pl.BlockSpec((SHORT_CONV, HEAD_DIM), lambda h, i: (0, wb0 + h))],
        out_specs=pl.BlockSpec((tr, HEAD_DIM), lambda h, i: (i, h)),
        out_shape=_sds((T, W), F32),
        compiler_params=_params(("parallel", "arbitrary"), 32 * _nbytes((tr, HEAD_DIM), F32) + 8 * 2**20),
    )(proj, proj, conv_w)


def _dn_branch_bwd(proj, col0, conv_w, wcol0, l2, scale, dy):
    T = proj.shape[0]
    W = DN_HEADS * HEAD_DIM
    tr = _tile(T, 2048, SUBLANES)
    r8 = tr // SUBLANES
    n8 = T // SUBLANES
    cb0, wb0 = col0 // HEAD_DIM, wcol0 // HEAD_DIM
    K = SHORT_CONV

    def body(u_ref, up_ref, un_ref, dy_ref, dyn_ref, w_ref, du_ref, dw_ref):
        i = pl.program_id(1)
        first = i == 0
        last = i == pl.num_programs(1) - 1

        def dconv(u, u_prev, dy_):
            c = _causal_conv(u, u_prev, w_ref, K)
            if l2:
                s = _silu(c)
                r = lax.rsqrt(jnp.sum(s * s, axis=-1, keepdims=True) + EPS)
                n = s * r
                ds = (scale * r) * (dy_ - n * jnp.sum(dy_ * n, axis=-1, keepdims=True))
            else:
                ds = dy_
            return ds * _silu_grad(c)

        u = u_ref[...]
        u_prev = jnp.where(first, 0.0, up_ref[...])
        dc = dconv(u, u_prev, dy_ref[...])
        dcn = jnp.where(last, 0.0, dconv(un_ref[...], u[tr - SUBLANES:], dyn_ref[...]))
        du_ref[...] = _anti_conv(dc, dcn, w_ref, K).astype(BF16)
        rows = [jnp.sum(dc * _shift_down(u, u_prev, K - 1 - t), axis=0, keepdims=True) for t in range(K - 1)]
        rows += [jnp.sum(dc * u, axis=0, keepdims=True)]
        rows += [jnp.zeros_like(rows[0])] * (SUBLANES - len(rows))
        upd = jnp.concatenate(rows, axis=0)

        @pl.when(first)
        def _():
            dw_ref[...] = jnp.zeros_like(dw_ref)
        dw_ref[...] += upd

    return pl.pallas_call(
        body, name=f"dn_branch_bwd_{col0}", grid=(DN_HEADS, T // tr),
        in_specs=[pl.BlockSpec((tr, HEAD_DIM), lambda h, i: (i, cb0 + h)),
                  pl.BlockSpec((SUBLANES, HEAD_DIM), lambda h, i: (jnp.maximum(i * r8 - 1, 0), cb0 + h)),
                  pl.BlockSpec((SUBLANES, HEAD_DIM), lambda h, i: (jnp.minimum((i + 1) * r8, n8 - 1), cb0 + h)),
                  pl.BlockSpec((tr, HEAD_DIM), lambda h, i: (i, h)),
                  pl.BlockSpec((SUBLANES, HEAD_DIM), lambda h, i: (jnp.minimum((i + 1) * r8, n8 - 1), h)),
                  pl.BlockSpec((K, HEAD_DIM), lambda h, i: (0, wb0 + h))],
        out_specs=[pl.BlockSpec((tr, HEAD_DIM), lambda h, i: (i, h)),
                   pl.BlockSpec((SUBLANES, HEAD_DIM), lambda h, i: (0, h))],
        out_shape=[_sds((T, W), BF16), _sds((SUBLANES, W), F32)],
        compiler_params=_params(("parallel", "arbitrary"), 32 * _nbytes((tr, HEAD_DIM), F32) + 8 * 2**20),
    )(proj, proj, proj, dy, dy, conv_w)


def _lane_masks(shape):
    lane = _iota2(shape, 1)
    return lane < DN_HEADS, (lane >= DN_HEADS) & (lane < 2 * DN_HEADS)


def _expand01(off):
    r = _iota2((LANES, DN_HEADS * HEAD_DIM), 0)
    c = _iota2((LANES, DN_HEADS * HEAD_DIM), 1)
    return (r == jnp.right_shift(c, int(math.log2(HEAD_DIM))) + off).astype(BF16)


def _select01(off):
    r = _iota2((DN_HEADS * HEAD_DIM, LANES), 0)
    c = _iota2((DN_HEADS * HEAD_DIM, LANES), 1)
    return (r == (c - off) * HEAD_DIM).astype(BF16)


def _dn_gates(proj, gate_block, a_log_l, dt_bias_l):
    T = proj.shape[0]
    C = DN_CHUNK
    W = DN_HEADS * HEAD_DIM

    def body(ba_ref, al_ref, dt_ref, gc_ref, beta_ref):
        ba = ba_ref[...]
        is_b, is_a = _lane_masks(ba.shape)
        g = jnp.where(is_a, -jnp.exp(al_ref[...]) * _softplus(ba + dt_ref[...]), 0.0)
        beta = jnp.where(is_b, _sigmoid(ba), 0.0)
        tri = (_iota2((C, C), 0) >= _iota2((C, C), 1)).astype(BF16)
        gc = _dot01_left(tri, g)
        gc_ref[...] = _dot01(gc, _expand01(DN_HEADS))
        beta_ref[...] = _dot01(beta, _expand01(0))

    vec = pl.BlockSpec((1, LANES), lambda n: (0, 0))
    return pl.pallas_call(
        body, name="dn_gates", grid=(T // C,),
        in_specs=[pl.BlockSpec((C, LANES), lambda n: (n, gate_block)), vec, vec],
        out_specs=[pl.BlockSpec((C, W), lambda n: (n, 0))] * 2,
        out_shape=[_sds((T, W), F32)] * 2,
        compiler_params=_params(("parallel",), 16 * 2**20),
    )(proj, a_log_l, dt_bias_l)


def _dn_gates_bwd(proj, gate_block, a_log_l, dt_bias_l, dgc_full, dbeta_full):
    T = proj.shape[0]
    C = DN_CHUNK
    W = DN_HEADS * HEAD_DIM

    def body(ba_ref, al_ref, dt_ref, dgc_ref, dbeta_ref, dba_ref, dal_ref, ddt_ref):
        ba = ba_ref[...]
        is_b, is_a = _lane_masks(ba.shape)
        ea = jnp.exp(al_ref[...])
        pre = ba + dt_ref[...]
        g = jnp.where(is_a, -ea * _softplus(pre), 0.0)
        beta = _sigmoid(ba)
        dgc = _dot01(dgc_ref[...], _select01(DN_HEADS))
        dbeta = _dot01(dbeta_ref[...], _select01(0))
        triu = (_iota2((C, C), 0) <= _iota2((C, C), 1)).astype(BF16)
        dg = _dot01_left(triu, dgc)
        da = jnp.where(is_a, dg * (-ea) * _sigmoid(pre), 0.0)
        dba_ref[...] = (da + jnp.where(is_b, dbeta * beta * (1.0 - beta), 0.0)).astype(BF16)
        _accumulate(dal_ref, jnp.sum(dg * g, axis=0, keepdims=True))
        _accumulate(ddt_ref, jnp.sum(da, axis=0, keepdims=True))

    vec = pl.BlockSpec((1, LANES), lambda n: (0, 0))
    full = pl.BlockSpec((C, W), lambda n: (n, 0))
    return pl.pallas_call(
        body, name="dn_gates_bwd", grid=(T // C,),
        in_specs=[pl.BlockSpec((C, LANES), lambda n: (n, gate_block)), vec, vec, full, full],
        out_specs=[pl.BlockSpec((C, LANES), lambda n: (n, 0)), vec, vec],
        out_shape=[_sds((T, LANES), BF16), _sds((1, LANES), F32), _sds((1, LANES), F32)],
        compiler_params=_params(("arbitrary",), 16 * 2**20),
    )(proj, a_log_l, dt_bias_l, dgc_full, dbeta_full)


def _unit_lower_inverse(L):
    C = L.shape[-1]
    row, col = _iota2((C, C), 0), _iota2((C, C), 1)
    eye = (row == col).astype(F32)
    sh = int(math.log2(INV_BLOCK))
    Ld = jnp.where(jnp.right_shift(row, sh) == jnp.right_shift(col, sh), L, 0.0)
    Lo = L - Ld
    X = eye - Ld
    P = Ld
    for _ in range(int(math.log2(INV_BLOCK)) - 1):
        P = _mm3(P, P)
        X = X + _mm3(X, P)
    N = _mm3(X, Lo)
    Y = eye - N
    P = N
    for _ in range(int(math.log2(C // INV_BLOCK)) - 1):
        P = _mm3(P, P)
        Y = Y + _mm3(Y, P)
    return _mm3(Y, X)


def _dn_chunk_common(q, k, v, gc, beta, gl):
    C = q.shape[-2]
    row, col = _iota2((C, C), 0), _iota2((C, C), 1)
    causal, strict = row >= col, row > col
    eg = jnp.exp(gc)
    decay = jnp.where(causal, jnp.exp(jnp.where(causal, gc - _t(gc), 0.0)), 0.0)
    kb, vb = k * beta, v * beta
    L = jnp.where(strict, _dot(kb, k, _NT) * decay, 0.0)
    Aqk = jnp.where(causal, _dot(q, k, _NT) * decay, 0.0)
    ektg = jnp.exp(gl - gc)
    return dict(causal=causal, strict=strict, eg=eg, decay=decay, kb=kb, vb=vb, L=L, Aqk=Aqk, ektg=ektg,
                kbg=kb * eg, kte=k * ektg, qd=q * eg, egl=jnp.exp(gl))


def _dn_scan(qn, kn, vn, gc_full, beta_full, proj, z_col0, gain):
    T, W = qn.shape
    C = DN_CHUNK
    N = T // C
    H = DN_HEADS
    G = DN_GROUP
    GW = G * HEAD_DIM
    zb0 = z_col0 // GW

    def body(q_ref, k_ref, v_ref, gc_ref, beta_ref, z_ref, gain_ref, o_ref, mix_ref, tm_ref, s_ref, S):
        @pl.when(pl.program_id(1) == 0)
        def _():
            S[...] = jnp.zeros_like(S)

        heads = lambda ref, rows=slice(None): jnp.stack([ref[rows, g * HEAD_DIM:(g + 1) * HEAD_DIM] for g in range(G)])
        q, k, v, gc, beta = heads(q_ref), heads(k_ref), heads(v_ref), heads(gc_ref), heads(beta_ref)
        gl = heads(gc_ref, slice(C - 1, C))
        c = _dn_chunk_common(q, k, v, gc, beta, gl)
        Tm = _unit_lower_inverse(c["L"])
        u = _dot(Tm, c["vb"])
        w = _dot(Tm, c["kbg"])
        S0 = S[...]
        vnew = u - _dot(w, S0)
        o = _dot(c["qd"], S0) + _dot(c["Aqk"], vnew)
        S[...] = S0 * c["egl"] + _dot(c["kte"], vnew, _TN)
        tm_ref[...] = Tm
        s_ref[...] = S0
        mix = (_rms(o, gain_ref[...]) * _silu(heads(z_ref))).astype(BF16)
        for g in range(G):
            sl = slice(g * HEAD_DIM, (g + 1) * HEAD_DIM)
            o_ref[:, sl] = o[g]
            mix_ref[:, sl] = mix[g]

    blk = pl.BlockSpec((C, GW), lambda h, n: (n, h))
    mat = pl.BlockSpec((G, None, C, C), lambda h, n: (h, n, 0, 0))
    return pl.pallas_call(
        body, name="dn_scan", grid=(H // G, N),
        in_specs=[blk, blk, blk, blk, blk, pl.BlockSpec((C, GW), lambda h, n: (n, zb0 + h)),
                  pl.BlockSpec((1, HEAD_DIM), lambda h, n: (0, 0))],
        out_specs=[blk, blk, mat, mat],
        out_shape=[_sds((T, W), F32), _sds((T, W), BF16), _sds((H, N, C, C), F32), _sds((H, N, C, C), F32)],
        scratch_shapes=[pltpu.VMEM((G, HEAD_DIM, HEAD_DIM), F32)],
        compiler_params=_params(("parallel", "arbitrary"), 32 * 2**20),
    )(qn, kn, vn, gc_full, beta_full, proj, gain)


def _dn_scan_bwd(qn, kn, vn, gc_full, beta_full, proj, z_col0, gain, o_raw, tm_all, s_all, dmix, dmix_col0):
    T, W = qn.shape
    C = DN_CHUNK
    N = T // C
    H = DN_HEADS
    G = DN_GROUP
    GW = G * HEAD_DIM
    zb0 = z_col0 // GW
    mb0 = dmix_col0 // GW

    def body(q_ref, k_ref, v_ref, gc_ref, beta_ref, z_ref, gain_ref, o_ref, tm_ref, s_ref, dmix_ref,
             dq_ref, dk_ref, dv_ref, dgc_ref, dbeta_ref, dz_ref, dgain_ref, dS):
        @pl.when(pl.program_id(1) == 0)
        def _():
            dS[...] = jnp.zeros_like(dS)

        @pl.when((pl.program_id(0) == 0) & (pl.program_id(1) == 0))
        def _():
            dgain_ref[...] = jnp.zeros_like(dgain_ref)

        heads = lambda ref, rows=slice(None): jnp.stack([ref[rows, g * HEAD_DIM:(g + 1) * HEAD_DIM] for g in range(G)])
        total = lambda x: jnp.sum(jnp.sum(x, axis=-1, keepdims=True), axis=-2, keepdims=True)
        gain = gain_ref[...]
        o, z, dmix = heads(o_ref), heads(z_ref), heads(dmix_ref)
        dz = (dmix * _rms(o, gain) * _silu_grad(z)).astype(BF16)
        do, dgain = _rms_bwd(dmix * _silu(z), o, gain)
        dgain_ref[...] += jnp.sum(dgain, axis=0)

        q, k, v, gc, beta = heads(q_ref), heads(k_ref), heads(v_ref), heads(gc_ref), heads(beta_ref)
        gl = heads(gc_ref, slice(C - 1, C))
        c = _dn_chunk_common(q, k, v, gc, beta, gl)
        Tm, S0, dS1 = tm_ref[...], s_ref[...], dS[...]
        w = _dot(Tm, c["kbg"])
        vnew = _dot(Tm, c["vb"]) - _dot(w, S0)

        dvnew = _dot(c["Aqk"], do, _TN) + _dot(c["kte"], dS1)
        dAqk = jnp.where(c["causal"], _dot(do, vnew, _NT), 0.0)
        dqd = _dot(do, S0, _NT)
        dkte = _dot(vnew, dS1, _NT)
        dgl = total(dS1 * S0) * c["egl"]
        dw = -_dot(dvnew, S0, _NT)
        dS[...] = dS1 * c["egl"] + _dot(c["qd"], do, _TN) - _dot(w, dvnew, _TN)

        dTm = _dot(dvnew, c["vb"], _NT) + _dot(dw, c["kbg"], _NT)
        dvb = _dot(Tm, dvnew, _TN)
        dkbg = _dot(Tm, dw, _TN)
        dL = jnp.where(c["strict"], -_mm3(_mm3(Tm, dTm, _TN), Tm, _NT), 0.0)
        dP = dL * c["decay"]
        dQ = dAqk * c["decay"]
        M = dL * c["L"] + dAqk * c["Aqk"]
        dkb = _dot(dP, k) + dkbg * c["eg"]
        dk = _dot(dP, c["kb"], _TN) + _dot(dQ, q, _TN) + dkte * c["ektg"] + dkb * beta
        dq = _dot(dQ, k) + dqd * c["eg"]
        tk = _rowsum(dkte * c["kte"])
        dgc = (_rowsum(M) - _rowsum(_t(M)) + _rowsum(dqd * c["qd"]) - tk + _rowsum(dkbg * c["kbg"]))
        dgl = dgl + total(tk)
        dgc = jnp.broadcast_to(dgc, q.shape) + jnp.where(_iota2((C, HEAD_DIM), 0) == C - 1, dgl, 0.0)
        dv = dvb * beta
        dbeta = jnp.broadcast_to(_rowsum(dkb * k) + _rowsum(dvb * v), q.shape)
        for g in range(G):
            sl = slice(g * HEAD_DIM, (g + 1) * HEAD_DIM)
            dz_ref[:, sl] = dz[g]
            dq_ref[:, sl] = dq[g]
            dk_ref[:, sl] = dk[g]
            dv_ref[:, sl] = dv[g]
            dgc_ref[:, sl] = dgc[g]
            dbeta_ref[:, sl] = dbeta[g]

    rev = lambda off: pl.BlockSpec((C, GW), lambda h, n: (N - 1 - n, off + h))
    mat = pl.BlockSpec((G, None, C, C), lambda h, n: (h, N - 1 - n, 0, 0))
    vec = pl.BlockSpec((1, HEAD_DIM), lambda h, n: (0, 0))
    return pl.pallas_call(
        body, name="dn_scan_bwd", grid=(H // G, N),
        in_specs=[rev(0), rev(0), rev(0), rev(0), rev(0), rev(zb0), vec, rev(0), mat, mat, rev(mb0)],
        out_specs=[rev(0)] * 6 + [vec],
        out_shape=[_sds((T, W), F32)] * 5 + [_sds((T, W), BF16), _sds((1, HEAD_DIM), F32)],
        scratch_shapes=[pltpu.VMEM((G, HEAD_DIM, HEAD_DIM), F32)],
        compiler_params=_params(("arbitrary", "arbitrary"), 40 * 2**20),
    )(qn, kn, vn, gc_full, beta_full, proj, gain, o_raw, tm_all, s_all, dmix)


def _sb_terms(z, ahead, first_key):
    lb = jnp.minimum(z, 0.0) - jnp.log(1.0 + jnp.exp(-jnp.abs(z)))
    if ahead is None:
        return None, lb, lb - z
    valid = ahead < -first_key
    return valid, lb, jnp.where(valid, lb - z, 0.0)


def _masked(valid, x):
    return x if valid is None else jnp.where(valid, x, 0.0)


def _sb_attention(qkv, gain, tq_cap=2048):
    T = qkv.shape[0]
    H = SB_HEADS
    B = min(SB_KEYS, T)
    TQ = _tile(T, tq_cap, B)
    per = TQ // B

    assert per % 2 == 0
    n_saved = per * (T // TQ) * (T // TQ + 1) // 2

    def body(q_ref, k_ref, v_ref, gain_ref, o_ref, mix_ref, att_hbm, lb_hbm, att_buf, lb_buf, sems):
        h, i = pl.program_id(0), pl.program_id(1)
        q = q_ref[...].astype(BF16)
        upper = (_iota2((B, B), 0) > _iota2((B, B), 1)).astype(BF16)
        ahead = _iota2((TQ, B), 1) - _iota2((TQ, B), 0)
        last = (i + 1) * per - 1
        base = per * (i * (i + 1) // 2)

        def save(slot, pair):
            return (pltpu.make_async_copy(att_buf.at[slot], att_hbm.at[h, pair], sems.at[0, slot]),
                    pltpu.make_async_copy(lb_buf.at[slot], lb_hbm.at[h, pair], sems.at[1, slot]))

        def pair(j, slot, r0, acc, R):
            n = TQ - (r0 or 0)
            top = slice(r0 or 0, TQ)
            rows = pl.ds(pl.multiple_of(j * B, B), B)
            z = _dot(q[top], k_ref[rows, :], _NT) * (HEAD_DIM ** -0.5)
            valid, lb, l1m = _sb_terms(z, None if r0 is None else ahead[top], r0)
            att = _masked(valid, jnp.exp(lb + R[top] + _dot01(l1m, upper, passes=2))).astype(BF16)
            att_buf[slot, pl.ds(TQ - n, n), :] = att
            lb_buf[slot, pl.ds(TQ - n, n), :] = (lb if valid is None else jnp.where(valid, lb, -1e30)).astype(BF16)
            if r0:
                att_buf[slot, pl.ds(0, r0), :] = jnp.zeros((r0, B), BF16)
                lb_buf[slot, pl.ds(0, r0), :] = jnp.full((r0, B), -1e30, BF16)
            for c in save(slot, base + j):
                c.start()
            d_acc, d_R = _dot(att, v_ref[rows, :]), _rowsum(l1m)
            if r0:
                d_acc = jnp.concatenate([jnp.zeros((r0, HEAD_DIM), F32), d_acc], axis=0)
                d_R = jnp.concatenate([jnp.zeros((r0, 1), F32), d_R], axis=0)
            return acc + d_acc, R + d_R

        def step(jj, carry):
            slot = jj % 2

            @pl.when(jj >= 2)
            def _():
                for c in save(slot, 0):
                    c.wait()

            return pair(last - jj, slot, None, *carry)

        carry = (jnp.zeros((TQ, HEAD_DIM), F32), jnp.zeros((TQ, 1), F32))
        for jj in range(per):
            if jj >= 2:
                for c in save(jj % 2, 0):
                    c.wait()
            carry = pair(last - jj, jj % 2, (per - 1 - jj) * B, *carry)
        acc, _ = lax.fori_loop(per, last + 1, step, carry)
        for slot in range(2):
            for c in save(slot, 0):
                c.wait()
        o_ref[...] = acc
        mix_ref[...] = _rms(acc, gain_ref[...]).astype(BF16)

    head = lambda off: pl.BlockSpec((T, HEAD_DIM), lambda h, i: (0, off + h))
    blk = pl.BlockSpec((TQ, HEAD_DIM), lambda h, i: (i, h))
    return pl.pallas_call(
        body, name="sb_attention", grid=(H, T // TQ),
        in_specs=[blk, head(H), head(2 * H), pl.BlockSpec((1, HEAD_DIM), lambda h, i: (0, 0))],
        out_specs=[blk, blk, ANY_SPEC, ANY_SPEC],
        out_shape=[_sds((T, H * HEAD_DIM), F32), _sds((T, H * HEAD_DIM), BF16),
                   _sds((H, n_saved, TQ, B), BF16), _sds((H, n_saved, TQ, B), BF16)],
        scratch_shapes=[pltpu.VMEM((2, TQ, B), BF16), pltpu.VMEM((2, TQ, B), BF16), pltpu.SemaphoreType.DMA((2, 2))],
        compiler_params=_params(("parallel", "arbitrary"), 8 * _nbytes((T, HEAD_DIM), BF16) + 32 * _nbytes((TQ, B), F32)),
    )(qkv, qkv, qkv, gain)


def _sb_attention_bwd(qkv, gain, o_raw, att_all, lb_all, dmix):
    T = qkv.shape[0]
    H = SB_HEADS
    TQ, B = att_all.shape[2:]
    per = TQ // B
    scale = HEAD_DIM ** -0.5

    def body(q_ref, k_ref, v_ref, gain_ref, o_ref, dmix_ref, att_hbm, lb_hbm, dq_ref, dk_ref, dv_ref, dgain_ref,
             att_buf, lb_buf, sems):
        h, i = pl.program_id(0), pl.program_id(1)

        @pl.when(i == 0)
        def _():
            dk_ref[...] = jnp.zeros_like(dk_ref)
            dv_ref[...] = jnp.zeros_like(dv_ref)

        @pl.when((pl.program_id(0) == 0) & (i == 0))
        def _():
            dgain_ref[...] = jnp.zeros_like(dgain_ref)

        q = q_ref[...].astype(BF16)
        o = o_ref[...]
        do, dgain = _rms_bwd(dmix_ref[...], o, gain_ref[...])
        dgain_ref[...] += dgain
        do_b = do.astype(BF16)
        before = (_iota2((B, B), 0) < _iota2((B, B), 1)).astype(BF16)
        base = per * (i * (i + 1) // 2)

        def fetch(slot, pair):
            return (pltpu.make_async_copy(att_hbm.at[h, pair], att_buf.at[slot], sems.at[0, slot]),
                    pltpu.make_async_copy(lb_hbm.at[h, pair], lb_buf.at[slot], sems.at[1, slot]))

        for c in fetch(0, base):
            c.start()

        def pair(j, slot, r0, dq, PG):
            top = slice(r0, TQ)
            rows = pl.ds(pl.multiple_of(j * B, B), B)
            kj = k_ref[rows, :]
            att = att_buf[slot, pl.ds(r0, TQ - r0), :]
            sig = jnp.exp(lb_buf[slot, pl.ds(r0, TQ - r0), :].astype(F32))
            G = _dot(do_b[top], v_ref[rows, :], _NT) * att.astype(F32)
            dv_ref[rows, :] += _dot(att, do_b[top], _TN)
            cum = PG[top] + _dot01(G, before, passes=2)
            dz = (G * (1.0 - sig) - sig * cum) * scale
            dk_ref[rows, :] += _dot(dz, q[top], _TN)
            d_dq, d_PG = _dot(dz, kj), _rowsum(G)
            if r0:
                d_dq = jnp.concatenate([jnp.zeros((r0, HEAD_DIM), F32), d_dq], axis=0)
                d_PG = jnp.concatenate([jnp.zeros((r0, 1), F32), d_PG], axis=0)
            return dq + d_dq, PG + d_PG

        def step(j, carry):
            slot = j % 2
            for c in fetch(slot, 0):
                c.wait()
            for c in fetch(1 - slot, base + j + 1):
                c.start()
            return pair(j, slot, 0, *carry)

        carry = lax.fori_loop(0, i * per, step, (jnp.zeros((TQ, HEAD_DIM), F32), jnp.zeros((TQ, 1), F32)))
        for c_blk in range(per):
            slot = c_blk % 2
            for c in fetch(slot, 0):
                c.wait()
            if c_blk + 1 < per:
                for c in fetch(1 - slot, base + i * per + c_blk + 1):
                    c.start()
            carry = pair(i * per + c_blk, slot, c_blk * B, *carry)
        dq_ref[...] = carry[0].astype(BF16)

    head = lambda off: pl.BlockSpec((T, HEAD_DIM), lambda h, i: (0, off + h))
    blk = pl.BlockSpec((TQ, HEAD_DIM), lambda h, i: (i, h))
    vec = pl.BlockSpec((1, HEAD_DIM), lambda h, i: (0, 0))
    return pl.pallas_call(
        body, name="sb_attention_bwd", grid=(H, T // TQ),
        in_specs=[blk, head(H), head(2 * H), vec, blk, blk, ANY_SPEC, ANY_SPEC],
        out_specs=[blk, head(0), head(0), vec],
        out_shape=[_sds((T, H * HEAD_DIM), BF16), _sds((T, H * HEAD_DIM), F32), _sds((T, H * HEAD_DIM), F32),
                   _sds((1, HEAD_DIM), F32)],
        scratch_shapes=[pltpu.VMEM((2, TQ, B), BF16), pltpu.VMEM((2, TQ, B), BF16), pltpu.SemaphoreType.DMA((2, 2))],
        compiler_params=_params(("arbitrary", "arbitrary"), 8 * _nbytes((T, HEAD_DIM), F32) + 32 * _nbytes((TQ, B), F32)),
    )(qkv, qkv, qkv, gain, o_raw, dmix, att_all, lb_all)


def _adamw_math(w, g, m, v):
    m = ADAM_B1 * m + (1.0 - ADAM_B1) * g
    v = ADAM_B2 * v + (1.0 - ADAM_B2) * (g * g)
    m_hat = m / (1.0 - ADAM_B1 ** ADAM_STEP)
    v_hat = v / (1.0 - ADAM_B2 ** ADAM_STEP)
    delta = -ADAM_LR * (m_hat / (jnp.sqrt(v_hat) + ADAM_EPS) + ADAM_WD * w)
    return delta, m, v


def _adamw_sharded(parts, w, m, v, name):
    _, R, C = w.shape
    if R % SUBLANES == 0:
        tr, tc = _tile(R, max(SUBLANES, (2**20 // (4 * C)) // SUBLANES * SUBLANES), SUBLANES), C
    else:
        tr, tc = R, _tile(C, max(LANES, (2**20 // (4 * R)) // LANES * LANES), LANES)

    def body(p_ref, w_ref, m_ref, v_ref, g_ref, d_ref, nm_ref, nv_ref):
        g = p_ref[0].astype(F32)
        for d in range(1, N_DEV):
            g = g + p_ref[d].astype(F32)
        g_ref[...] = g
        d_ref[...], nm_ref[...], nv_ref[...] = _adamw_math(w_ref[...], g, m_ref[...], v_ref[...])

    blk = pl.BlockSpec((None, tr, tc), lambda i, j: (0, i, j))
    return pl.pallas_call(
        body, name=name, grid=(R // tr, C // tc),
        in_specs=[pl.BlockSpec((N_DEV, tr, tc), lambda i, j: (0, i, j)), blk, blk, blk],
        out_specs=[blk] * 4, out_shape=[_sds((1, R, C), F32)] * 4,
        compiler_params=_params(("parallel", "parallel"), 40 * 2**20),
    )(parts, w, m, v)


def _adamw_packed(g, w, m, v):
    def body(g_ref, w_ref, m_ref, v_ref, d_ref, nm_ref, nv_ref):
        d_ref[...], nm_ref[...], nv_ref[...] = _adamw_math(w_ref[...], g_ref[...], m_ref[...], v_ref[...])

    return pl.pallas_call(body, name="adamw_packed", out_shape=[_sds(g.shape, F32)] * 3,
                          compiler_params=_params((), 16 * 2**20))(g, w, m, v)


def _my_place():
    x, y, c = lax.axis_index("x"), lax.axis_index("y"), lax.axis_index("c")
    return x, y, c


def _peer(place, k):
    x, y, c = place
    return (1 - x if k & 4 else x, 1 - y if k & 2 else y, 1 - c if k & 1 else c)


def _index(place):
    x, y, c = place
    return 4 * x + 2 * y + c


HBM_SPEC = pl.BlockSpec(memory_space=pltpu.HBM)


def _all_gather(block, name):
    R, C = block.shape

    def body(x_ref, out_ref, send_sems, recv_sems, local_sem):
        me = _my_place()
        sibling = _peer(me, 1)
        chips = [2, 4, 6]

        def copy(sem, origin, to, src=None):
            slot = out_ref.at[_index(origin)]
            return pltpu.make_async_remote_copy(
                src_ref=slot if src is None else src, dst_ref=slot, send_sem=send_sems.at[sem], recv_sem=recv_sems.at[sem],
                device_id=to, device_id_type=MESH)

        mine = pltpu.make_async_copy(x_ref, out_ref.at[_index(me)], local_sem)
        mine.start()
        first = [copy(0, me, sibling, src=x_ref)] + [copy(1 + n, me, _peer(me, k), src=x_ref) for n, k in enumerate(chips)]
        for cp in first:
            cp.start()
        passed = [copy(4 + n, _peer(me, k), sibling) for n, k in enumerate(chips)]
        for n, k in enumerate(chips):
            copy(1 + n, _peer(me, k), me).wait_recv()
            passed[n].start()
        copy(0, sibling, me).wait_recv()
        for n, k in enumerate(chips):
            copy(4 + n, _peer(sibling, k), me).wait_recv()
        for cp in first + passed:
            cp.wait_send()
        mine.wait()

    return pl.pallas_call(
        body, name=name, in_specs=[HBM_SPEC], out_specs=HBM_SPEC,
        out_shape=_sds((N_DEV, R, C), block.dtype),
        scratch_shapes=[pltpu.SemaphoreType.DMA((7,)), pltpu.SemaphoreType.DMA((7,)), pltpu.SemaphoreType.DMA],
    )(block)


SEM_SPEC = pl.BlockSpec(memory_space=pltpu.SEMAPHORE)
ANY_SPEC = pl.BlockSpec(memory_space=pl.ANY)
_EFFECT = pltpu.SideEffectType.DATAFLOW_SIDE_EFFECTING


def _spread_start(x, per_peer, name, after):
    R, C = x.shape[-2:]

    def body(x_ref, land_ref, after_ref, send_sems, recv_sems, x_thru, land_thru, token):
        me = _my_place()
        for k in range(1, N_DEV):
            to = _peer(me, k)
            pltpu.make_async_remote_copy(
                src_ref=x_ref.at[_index(to)] if per_peer else x_ref, dst_ref=land_ref.at[_index(me)],
                send_sem=send_sems.at[k - 1], recv_sem=recv_sems.at[k - 1], device_id=to, device_id_type=MESH).start()
        token[...] = jnp.zeros_like(token)

    land = lax.empty((N_DEV, R, C), x.dtype)
    send_sems, recv_sems, x_thru, land_thru, token = pl.pallas_call(
        body, name=name,
        out_shape=(pltpu.SemaphoreType.DMA((N_DEV - 1,)), pltpu.SemaphoreType.DMA((N_DEV - 1,)),
                   pltpu.HBM(x.shape, x.dtype), pltpu.HBM(land.shape, land.dtype), _sds((SUBLANES, LANES), F32)),
        in_specs=(HBM_SPEC, HBM_SPEC, ANY_SPEC),
        out_specs=(SEM_SPEC, SEM_SPEC, HBM_SPEC, HBM_SPEC, pl.BlockSpec(memory_space=pltpu.VMEM)),
        input_output_aliases={0: 2, 1: 3},
        compiler_params=pltpu.CompilerParams(has_side_effects=_EFFECT),
    )(pltpu.with_memory_space_constraint(x, pltpu.HBM), pltpu.with_memory_space_constraint(land, pltpu.HBM), after)
    return (send_sems, recv_sems, x_thru, land_thru), token


def _spread_wait(state, per_peer, name, after):
    send_sems, recv_sems, x_thru, land_thru = state

    def body(x_ref, land_ref, send_sems, recv_sems, after_ref, x_dead, got_ref):
        me = _my_place()
        for k in range(1, N_DEV):
            frm = _peer(me, k)
            copy = pltpu.make_async_remote_copy(
                src_ref=x_ref.at[_index(frm)] if per_peer else x_ref, dst_ref=land_ref.at[_index(frm)],
                send_sem=send_sems.at[k - 1], recv_sem=recv_sems.at[k - 1], device_id=frm, device_id_type=MESH)
            copy.wait_send()
            copy.wait_recv()

    x_back, got = pl.pallas_call(
        body, name=name,
        out_shape=(pltpu.HBM(x_thru.shape, x_thru.dtype), pltpu.HBM(land_thru.shape, land_thru.dtype)),
        in_specs=(HBM_SPEC, HBM_SPEC, SEM_SPEC, SEM_SPEC, ANY_SPEC), out_specs=(HBM_SPEC, HBM_SPEC),
        input_output_aliases={0: 0, 1: 1},
        compiler_params=pltpu.CompilerParams(has_side_effects=_EFFECT),
    )(x_thru, land_thru, send_sems, recv_sems, after)
    me = _index(_my_place())
    own = lax.dynamic_index_in_dim(x_back, me, axis=0, keepdims=True) if per_peer else x_back[None]
    return lax.dynamic_update_slice_in_dim(got, own, me, axis=0)


def _all_reduce_packed(vec, after):
    R, L = vec.shape

    def body(x_ref, after_ref, out_ref, buf, send_sems, recv_sems):
        me = _my_place()
        buf[_index(me)] = x_ref[...]
        copies = []
        for k in range(1, N_DEV):
            to = _peer(me, k)
            cp = pltpu.make_async_remote_copy(
                src_ref=x_ref, dst_ref=buf.at[_index(me)],
                send_sem=send_sems.at[k - 1], recv_sem=recv_sems.at[k - 1], device_id=to, device_id_type=MESH)
            cp.start()
            copies.append(cp)
        for k in range(1, N_DEV):
            frm = _peer(me, k)
            pltpu.make_async_remote_copy(
                src_ref=x_ref, dst_ref=buf.at[_index(frm)],
                send_sem=send_sems.at[k - 1], recv_sem=recv_sems.at[k - 1], device_id=frm, device_id_type=MESH).wait_recv()
        for cp in copies:
            cp.wait_send()
        acc = buf[0]
        for d in range(1, N_DEV):
            acc = acc + buf[d]
        out_ref[...] = acc

    vm = pl.BlockSpec(memory_space=pltpu.VMEM)
    return pl.pallas_call(
        body, name="all_reduce_packed", in_specs=[vm, ANY_SPEC], out_specs=vm, out_shape=_sds((R, L), F32),
        scratch_shapes=[pltpu.VMEM((N_DEV, R, L), F32), pltpu.SemaphoreType.DMA((7,)), pltpu.SemaphoreType.DMA((7,))],
        compiler_params=pltpu.CompilerParams(vmem_limit_bytes=32 * 2**20),
    )(vec, after)


def _pack(arrays):
    rows = []
    for a in arrays:
        f = a.reshape(-1).astype(F32)
        pad = (-f.shape[0]) % LANES
        rows.append(jnp.pad(f, (0, pad)).reshape(-1, LANES))
    out = jnp.concatenate(rows, axis=0)
    return jnp.pad(out, ((0, (-out.shape[0]) % SUBLANES), (0, 0)))


def _unpack(packed, shapes):
    out, r = [], 0
    for s in shapes:
        n = math.prod(s)
        nr = -(-n // LANES)
        out.append(packed[r:r + nr].reshape(-1)[:n].reshape(s))
        r += nr
    return out


def _row_blocks(g):
    R, C = g.shape
    return g.astype(BF16).reshape(N_DEV, R // N_DEV, C)


def kernel(x, w_in, sb_out_gain, dn_conv_w, dn_a_log, dn_dt_bias, dn_out_gain, w_out, ln_mix_pre, ln_mix_post, w_up, ffn_conv_w, ffn_conv_b, w_down, ln_ffn_pre, ln_ffn_post, loss_target, m_w_in, m_sb_out_gain, m_dn_conv_w, m_dn_a_log, m_dn_dt_bias, m_dn_out_gain, m_w_out, m_ln_mix_pre, m_ln_mix_post, m_w_up, m_ffn_conv_w, m_ffn_conv_b, m_w_down, m_ln_ffn_pre, m_ln_ffn_post, v_w_in, v_sb_out_gain, v_dn_conv_w, v_dn_a_log, v_dn_dt_bias, v_dn_out_gain, v_w_out, v_ln_mix_pre, v_ln_mix_post, v_w_up, v_ffn_conv_w, v_ffn_conv_b, v_w_down, v_ln_ffn_pre, v_ln_ffn_post):
    T, D = x.shape[1], x.shape[2]
    SBW = SB_HEADS * HEAD_DIM
    DNW = DN_HEADS * HEAD_DIM
    in_cols = 3 * SBW + 4 * DNW + 2 * DN_HEADS
    main_cols = 3 * SBW + 4 * DNW
    in_pad = main_cols + LANES
    qkv0, z0 = 3 * SBW, 3 * SBW + 3 * DNW
    gate_block = main_cols // LANES
    x2, tgt = x[0], loss_target[0]

    g_in = _all_gather(jnp.swapaxes(w_in[0], 0, 1).astype(BF16), "gather_w_in")
    small_w = _all_gather(_pack([dn_conv_w[0], ffn_conv_w[0]]), "gather_conv_w")
    st_out, tok = _spread_start(w_out[0].astype(BF16), False, "gather_w_out_start", g_in)
    st_up, tok = _spread_start(w_up[0].astype(BF16), False, "gather_w_up_start", tok)
    st_down, tok_gather = _spread_start(w_down[0].astype(BF16), False, "gather_w_down_start", tok)
    w_in_t = jnp.pad(g_in.reshape(in_cols, D), ((0, in_pad - in_cols), (0, 0)))
    parts = [_unpack(small_w[d], [dn_conv_w.shape[1:], ffn_conv_w.shape[1:]]) for d in range(N_DEV)]
    dn_cw = jnp.concatenate([p[0] for p in parts], axis=1)
    ffn_cw = jnp.concatenate([p[1] for p in parts], axis=1)
    lane_pad = lambda a, off: jnp.pad(a, ((0, 0), (off, LANES - off - a.shape[1])))
    a_log_l, dt_bias_l = lane_pad(dn_a_log, DN_HEADS), lane_pad(dn_dt_bias, DN_HEADS)

    xn = _norm_in(x2, ln_mix_pre)
    proj = _matmul(xn, w_in_t, "nt", F32, "proj_in", tm_cap=512, tn_cap=2432, after=tok_gather)
    o_sb, mix_sb, sb_att, sb_lb = _sb_attention(proj, sb_out_gain)
    qn = _dn_branch(proj, qkv0, dn_cw, 0, True, HEAD_DIM ** -0.5)
    kn = _dn_branch(proj, qkv0 + DNW, dn_cw, DNW, True, 1.0)
    vn = _dn_branch(proj, qkv0 + 2 * DNW, dn_cw, 2 * DNW, False, 1.0)
    gc_full, beta_full = _dn_gates(proj, gate_block, a_log_l, dt_bias_l)
    o_dn, mix_dn, tm_all, s_all = _dn_scan(qn, kn, vn, gc_full, beta_full, proj, z0, dn_out_gain)
    mix = jnp.concatenate([mix_sb, mix_dn], axis=1)
    w_out_f = _spread_wait(st_out, False, "gather_w_out_wait", mix).reshape(w_out.shape[1] * N_DEV, D)
    m = _matmul(mix, w_out_f, "nn", F32, "proj_out")
    h, hn = _mix_residual(x2, m, ln_mix_post, ln_ffn_pre)
    w_up_cut = _spread_wait(st_up, False, "gather_w_up_wait", hn)
    u = _matmul(hn, w_up_cut, "nn", F32, "ffn_up", tn_cap=w_up.shape[2], b_cut=True)
    act = _ffn_act(u, ffn_cw, ffn_conv_b)
    w_down_f = _spread_wait(st_down, False, "gather_w_down_wait", act).reshape(w_down.shape[1] * N_DEV, D)
    f = _matmul(act, w_down_f, "nn", F32, "ffn_down", tk_cap=2816)
    dy, df, d_ln_ffn_post, loss_part = _loss_head(h, f, ln_ffn_post, tgt)

    d_w_down = _matmul(act, df, "tn", BF16, "grad_w_down")
    st_xd, tok = _spread_start(_row_blocks(d_w_down), True, "exchange_w_down_start", loss_part)
    da = _matmul(df, w_down_f, "nt", F32, "bwd_ffn_down", after=tok)
    du, d_ffn_cwb = _ffn_act_bwd(u, ffn_cw, ffn_conv_b, da)
    d_ffn_cwb = jnp.concatenate([d_ffn_cwb[0], d_ffn_cwb[1]], axis=1)
    d_w_up_cut = _matmul(hn, du, "tn", BF16, "grad_w_up", b_cut=True, out_cut=True)
    st_xu, tok = _spread_start(d_w_up_cut, True, "exchange_w_up_start", d_ffn_cwb)
    dhn = _matmul(du, w_up_cut, "nt", F32, "bwd_ffn_up", after=tok, a_cut=True, b_cut=True)
    dh, dm, d_ln_ffn_pre, d_ln_mix_post = _ffn_residual_bwd(dy, dhn, h, ln_ffn_pre, m, ln_mix_post)

    d_w_out = _matmul(mix, dm, "tn", BF16, "grad_w_out")
    st_xo, tok = _spread_start(_row_blocks(d_w_out), True, "exchange_w_out_start", d_ln_ffn_pre)
    dmix = _matmul(dm, w_out_f, "nt", F32, "bwd_proj_out", after=tok)
    dq_sb, dk_sb, dv_sb, d_sb_gain = _sb_attention_bwd(proj, sb_out_gain, o_sb, sb_att, sb_lb, dmix)
    dqn, dkn, dvn, dgc_full, dbeta_full, dz, d_dn_gain = _dn_scan_bwd(
        qn, kn, vn, gc_full, beta_full, proj, z0, dn_out_gain, o_dn, tm_all, s_all, dmix, SBW)
    du_q, dcw_q = _dn_branch_bwd(proj, qkv0, dn_cw, 0, True, HEAD_DIM ** -0.5, dqn)
    du_k, dcw_k = _dn_branch_bwd(proj, qkv0 + DNW, dn_cw, DNW, True, 1.0, dkn)
    du_v, dcw_v = _dn_branch_bwd(proj, qkv0 + 2 * DNW, dn_cw, 2 * DNW, False, 1.0, dvn)
    dba, d_a_log_l, d_dt_bias_l = _dn_gates_bwd(proj, gate_block, a_log_l, dt_bias_l, dgc_full, dbeta_full)
    dproj = jnp.concatenate([dq_sb, dk_sb.astype(BF16), dv_sb.astype(BF16), du_q, du_k, du_v, dz, dba], axis=1)
    d_w_in_t = _matmul(dproj, xn, "tn", BF16, "grad_w_in", tm_cap=2432, tn_cap=512, tk_cap=1024)
    d_w_in_cut = d_w_in_t[:in_cols].reshape(N_DEV, in_cols // N_DEV, D)
    st_xi, tok = _spread_start(d_w_in_cut, True, "exchange_w_in_start", d_sb_gain)
    dxn = _matmul(dproj, w_in_t, "nn", F32, "bwd_proj_in", tk_cap=2432, after=tok)
    grad_x, d_ln_mix_pre = _input_bwd(dh, dxn, x2, ln_mix_pre)

    big = {}
    after = grad_x
    for n, st, w_, m_, v_ in [("w_down", st_xd, w_down, m_w_down, v_w_down), ("w_up", st_xu, w_up, m_w_up, v_w_up),
                              ("w_out", st_xo, w_out, m_w_out, v_w_out)]:
        got = _spread_wait(st, True, "exchange_" + n + "_wait", after)
        big[n] = _adamw_sharded(got, w_, m_, v_, "adamw_" + n)
        after = big[n][1]

    d_dn_cw = jnp.concatenate([dcw_q[:SHORT_CONV], dcw_k[:SHORT_CONV], dcw_v[:SHORT_CONV]], axis=1)
    small = [loss_part[:, :1], d_sb_gain, d_a_log_l[:, DN_HEADS:2 * DN_HEADS], d_dt_bias_l[:, DN_HEADS:2 * DN_HEADS], d_dn_gain,
             d_ln_mix_pre, d_ln_mix_post, d_ffn_cwb[FFN_CONV:FFN_CONV + 1], d_ln_ffn_pre, d_ln_ffn_post,
             d_dn_cw, d_ffn_cwb[:FFN_CONV]]
    shapes = [a.shape for a in small]
    red = _unpack(_all_reduce_packed(_pack(small), after), shapes)
    loss = red[0].reshape(())
    me = _index(_my_place())
    g_dn_cw = lax.dynamic_slice_in_dim(red[10], me * dn_conv_w.shape[2], dn_conv_w.shape[2], axis=1)
    g_ffn_cw = lax.dynamic_slice_in_dim(red[11], me * ffn_conv_w.shape[2], ffn_conv_w.shape[2], axis=1)
    names = ["sb_out_gain", "dn_conv_w", "dn_a_log", "dn_dt_bias", "dn_out_gain", "ln_mix_pre", "ln_mix_post",
             "ffn_conv_w", "ffn_conv_b", "ln_ffn_pre", "ln_ffn_post"]
    g_small = dict(sb_out_gain=red[1], dn_conv_w=g_dn_cw[None], dn_a_log=red[2], dn_dt_bias=red[3], dn_out_gain=red[4],
                   ln_mix_pre=red[5], ln_mix_post=red[6], ffn_conv_w=g_ffn_cw[None], ffn_conv_b=red[7],
                   ln_ffn_pre=red[8], ln_ffn_post=red[9])
    w_small = dict(sb_out_gain=sb_out_gain, dn_conv_w=dn_conv_w, dn_a_log=dn_a_log, dn_dt_bias=dn_dt_bias,
                   dn_out_gain=dn_out_gain, ln_mix_pre=ln_mix_pre, ln_mix_post=ln_mix_post, ffn_conv_w=ffn_conv_w,
                   ffn_conv_b=ffn_conv_b, ln_ffn_pre=ln_ffn_pre, ln_ffn_post=ln_ffn_post)
    m_small = dict(sb_out_gain=m_sb_out_gain, dn_conv_w=m_dn_conv_w, dn_a_log=m_dn_a_log, dn_dt_bias=m_dn_dt_bias,
                   dn_out_gain=m_dn_out_gain, ln_mix_pre=m_ln_mix_pre, ln_mix_post=m_ln_mix_post, ffn_conv_w=m_ffn_conv_w,
                   ffn_conv_b=m_ffn_conv_b, ln_ffn_pre=m_ln_ffn_pre, ln_ffn_post=m_ln_ffn_post)
    v_small = dict(sb_out_gain=v_sb_out_gain, dn_conv_w=v_dn_conv_w, dn_a_log=v_dn_a_log, dn_dt_bias=v_dn_dt_bias,
                   dn_out_gain=v_dn_out_gain, ln_mix_pre=v_ln_mix_pre, ln_mix_post=v_ln_mix_post, ffn_conv_w=v_ffn_conv_w,
                   ffn_conv_b=v_ffn_conv_b, ln_ffn_pre=v_ln_ffn_pre, ln_ffn_post=v_ln_ffn_post)
    sshapes = [w_small[n].shape for n in names]
    upd = _adamw_packed(_pack([g_small[n] for n in names]), _pack([w_small[n] for n in names]),
                        _pack([m_small[n] for n in names]), _pack([v_small[n] for n in names]))
    d_small, nm_small, nv_small = [dict(zip(names, _unpack(p, sshapes))) for p in upd]

    got = _spread_wait(st_xi, True, "exchange_w_in_wait", d_small["ln_ffn_post"])
    flip = lambda a: jnp.swapaxes(a, 1, 2)
    big["w_in"] = [flip(a) for a in _adamw_sharded(got, flip(w_in), flip(m_w_in), flip(v_w_in), "adamw_w_in")]

    order = ["w_in", "sb_out_gain", "dn_conv_w", "dn_a_log", "dn_dt_bias", "dn_out_gain", "w_out", "ln_mix_pre",
             "ln_mix_post", "w_up", "ffn_conv_w", "ffn_conv_b", "w_down", "ln_ffn_pre", "ln_ffn_post"]
    pick = lambda n, i: big[n][i] if n in big else [g_small, d_small, nm_small, nv_small][i][n].reshape(w_small[n].shape)
    return (loss, grad_x[None], *[pick(n, 0) for n in order], *[pick(n, 1) for n in order],
            *[pick(n, 2) for n in order], *[pick(n, 3) for n in order])
```

```python
import functools
import math

import jax
import jax.numpy as jnp
from jax import lax
from jax.experimental import pallas as pl
from jax.experimental.pallas import tpu as pltpu

F32 = jnp.float32
BF16 = jnp.bfloat16

N_DEV = 8
HEAD_DIM = 128
SB_HEADS = 8
DN_HEADS = 8
DN_CHUNK = 128
DN_GROUP = 8
INV_BLOCK = 16
SB_KEYS = 256
SHORT_CONV = 4
FFN_CONV = 3
EPS = 1e-6
LANES = 128
SUBLANES = 8
VMEM_CAP = 56 * 2**20

ADAM_LR = 0.001
ADAM_B1 = 0.9
ADAM_B2 = 0.999
ADAM_EPS = 1e-08
ADAM_WD = 0.01
ADAM_STEP = 10

MESH = pl.DeviceIdType.MESH

assert HEAD_DIM == DN_CHUNK == LANES


def _tile(n, cap, mult):
    if n <= cap:
        return n
    t = (cap // mult) * mult
    while t >= mult:
        if n % t == 0:
            return t
        t -= mult
    raise ValueError(f"no tile for {n} under {cap} in multiples of {mult}")


def _params(sem, vmem_bytes):
    limit = int(min(VMEM_CAP, max(vmem_bytes, 16 * 2**20)))
    if not sem:
        return pltpu.CompilerParams(vmem_limit_bytes=limit)
    return pltpu.CompilerParams(dimension_semantics=sem, vmem_limit_bytes=limit)


def _nbytes(shape, dtype):
    return math.prod(shape) * jnp.dtype(dtype).itemsize


def _hbm(*arrays):
    return [pltpu.with_memory_space_constraint(a, pltpu.HBM) for a in arrays]


_NN = (((1,), (0,)), ((), ()))
_NT = (((1,), (1,)), ((), ()))
_TN = (((0,), (0,)), ((), ()))


def _batched(dims, ndim):
    if ndim == 2:
        return dims
    (ca,), (cb,) = dims[0]
    return (((ca + 1,), (cb + 1,)), ((0,), (0,)))


def _dot(a, b, dims=_NN):
    return lax.dot_general(a.astype(BF16), b.astype(BF16), _batched(dims, a.ndim), preferred_element_type=F32)


def _split2(x):
    hi = x.astype(BF16)
    lo = (x - hi.astype(F32)).astype(BF16)
    return hi, lo


def _split3(x):
    hi = x.astype(BF16)
    r = x - hi.astype(F32)
    mid = r.astype(BF16)
    lo = (r - mid.astype(F32)).astype(BF16)
    return hi, mid, lo


def _dot01(x, m01, passes=3):
    parts = _split3(x) if passes == 3 else _split2(x)
    out = None
    for p in parts:
        t = lax.dot_general(p, m01, _NN, preferred_element_type=F32)
        out = t if out is None else out + t
    return out


def _dot01_left(m01, x, passes=3):
    parts = _split3(x) if passes == 3 else _split2(x)
    out = None
    for p in parts:
        t = lax.dot_general(m01, p, _NN, preferred_element_type=F32)
        out = t if out is None else out + t
    return out


def _mm3(a, b, dims=_NN):
    ah, al = _split2(a)
    bh, bl = _split2(b)
    d = functools.partial(lax.dot_general, dimension_numbers=_batched(dims, a.ndim), preferred_element_type=F32)
    return d(ah, bh) + (d(ah, bl) + d(al, bh))


def _rowsum(x):
    return jnp.sum(x, axis=-1, keepdims=True)


def _t(x):
    return jnp.swapaxes(x, -1, -2)


def _sigmoid(x):
    return 1.0 / (1.0 + jnp.exp(-x))


def _softplus(x):
    return jnp.maximum(x, 0.0) + jnp.log(1.0 + jnp.exp(-jnp.abs(x)))


def _silu(x):
    return x * _sigmoid(x)


def _silu_grad(x):
    s = _sigmoid(x)
    return s * (1.0 + x * (1.0 - s))


_GELU_C = math.sqrt(2.0 / math.pi)


def _gelu(x):
    return 0.5 * x * (1.0 + jnp.tanh(_GELU_C * (x + 0.044715 * x * x * x)))


def _gelu_and_grad(x):
    x2 = x * x
    th = jnp.tanh(_GELU_C * (x + 0.044715 * x2 * x))
    half = 0.5 * (1.0 + th)
    return x * half, half + 0.5 * x * (1.0 - th * th) * (_GELU_C * (1.0 + 3.0 * 0.044715 * x2))


def _rms(x, g):
    r = lax.rsqrt(jnp.mean(x * x, axis=-1, keepdims=True) + EPS)
    return x * r * g


def _rms_bwd(dy, x, g):
    r = lax.rsqrt(jnp.mean(x * x, axis=-1, keepdims=True) + EPS)
    xh = x * r
    gdy = dy * g
    dx = r * (gdy - xh * jnp.mean(gdy * xh, axis=-1, keepdims=True))
    return dx, jnp.sum(dy * xh, axis=-2, keepdims=True)


def _iota2(shape, axis):
    return lax.broadcasted_iota(jnp.int32, shape, axis)


def _shift_down(cur, prev8, k):
    n = cur.shape[0]
    r = pltpu.roll(cur, k, 0)
    pr = pltpu.roll(prev8, k, 0)
    head = jnp.where(_iota2(pr.shape, 0) < k, pr, r[0:SUBLANES])
    if n == SUBLANES:
        return head
    return jnp.concatenate([head, r[SUBLANES:]], axis=0)


def _shift_up(cur, next8, k):
    n = cur.shape[0]
    r = pltpu.roll(cur, n - k, 0)
    nr = pltpu.roll(next8, SUBLANES - k, 0)
    tail = jnp.where(_iota2(nr.shape, 0) >= SUBLANES - k, nr, r[n - SUBLANES:])
    if n == SUBLANES:
        return tail
    return jnp.concatenate([r[:n - SUBLANES], tail], axis=0)


def _causal_conv(cur, prev8, w_ref, taps):
    out = cur * w_ref[taps - 1:taps, :]
    for j in range(taps - 1):
        out = out + _shift_down(cur, prev8, taps - 1 - j) * w_ref[j:j + 1, :]
    return out


def _anti_conv(cur, next8, w_ref, taps):
    out = cur * w_ref[taps - 1:taps, :]
    for j in range(taps - 1):
        out = out + _shift_up(cur, next8, taps - 1 - j) * w_ref[j:j + 1, :]
    return out


def _matmul(a, b, mode, out_dtype, name, tm_cap=1024, tn_cap=1024, tk_cap=2048, after=None,
            a_cut=False, b_cut=False, out_cut=False):
    a_shard = a.shape[2] if a_cut else None
    b_shard = b.shape[2] if b_cut else None
    a_full = (a.shape[1], a.shape[0] * a_shard) if a_cut else a.shape
    b_full = (b.shape[1], b.shape[0] * b_shard) if b_cut else b.shape
    assert not (a_cut and mode == "tn")
    if mode == "nn":
        (M, K), N = a_full, b_full[1]
    elif mode == "nt":
        (M, K), N = a_full, b_full[0]
    else:
        (K, M), N = a_full, b_full[1]
    n_unit = b_shard if (b_cut and mode != "nt") else N
    k_unit = math.gcd(a_shard or K, b_shard if (b_cut and mode == "nt") else K)
    tm = _tile(M, tm_cap, LANES)
    tn = N // N_DEV if out_cut else _tile(n_unit, tn_cap, LANES)
    tk = _tile(k_unit, tk_cap, LANES)
    assert n_unit % tn == 0 and k_unit % tk == 0
    nk = K // tk
    dims = {"nn": _NN, "nt": _NT, "tn": _TN}[mode]
    if a_cut:
        pa = a_shard // tk
        a_spec = pl.BlockSpec((None, tm, tk), lambda i, j, k: (k // pa, i, k % pa))
    elif mode == "tn":
        a_spec = pl.BlockSpec((tk, tm), lambda i, j, k: (k, i))
    else:
        a_spec = pl.BlockSpec((tm, tk), lambda i, j, k: (i, k))
    if b_cut and mode == "nt":
        pb = b_shard // tk
        b_spec = pl.BlockSpec((None, tn, tk), lambda i, j, k: (k // pb, j, k % pb))
    elif b_cut:
        pb = b_shard // tn
        b_spec = pl.BlockSpec((None, tk, tn), lambda i, j, k: (j // pb, k, j % pb))
    elif mode == "nt":
        b_spec = pl.BlockSpec((tn, tk), lambda i, j, k: (j, k))
    else:
        b_spec = pl.BlockSpec((tk, tn), lambda i, j, k: (k, j))
    if out_cut:
        out_spec, out_shape = pl.BlockSpec((None, tm, tn), lambda i, j, k: (j, i, 0)), (N_DEV, M, tn)
    else:
        out_spec, out_shape = pl.BlockSpec((tm, tn), lambda i, j, k: (i, j)), (M, N)

    def body(a_ref, b_ref, *rest):
        if nk == 1:
            rest[-1][...] = lax.dot_general(a_ref[...], b_ref[...], dims, preferred_element_type=F32).astype(rest[-1].dtype)
            return
        o_ref, acc_ref = rest[-2:]
        k = pl.program_id(2)

        @pl.when(k == 0)
        def _():
            acc_ref[...] = jnp.zeros_like(acc_ref)

        acc_ref[...] += lax.dot_general(a_ref[...], b_ref[...], dims, preferred_element_type=F32)

        @pl.when(k == nk - 1)
        def _():
            o_ref[...] = acc_ref[...].astype(o_ref.dtype)

    vmem = 2 * (_nbytes((tm, tk), a.dtype) + _nbytes((tk, tn), b.dtype) + _nbytes((tm, tn), out_dtype)) + _nbytes((tm, tn), F32)
    vmem += _nbytes((tm, tn), F32) + (2 * _nbytes((tm, tk), a.dtype) if mode == "tn" else 0)
    tokens = [] if after is None else [after]
    return pl.pallas_call(
        body, name=name, grid=(M // tm, N // tn, nk),
        in_specs=[a_spec, b_spec] + [pl.BlockSpec(t.shape, lambda i, j, k: (0, 0)) for t in tokens],
        out_specs=out_spec,
        out_shape=jax.ShapeDtypeStruct(out_shape, out_dtype),
        scratch_shapes=[] if nk == 1 else [pltpu.VMEM((tm, tn), F32)],
        compiler_params=_params(("parallel", "parallel", "arbitrary"), vmem + 4 * 2**20),
    )(*_hbm(a, b), *tokens)


def _row_call(body, name, T, D, ins, outs, tr, acc_outs=()):
    def spec(a, kind):
        if kind == "row":
            return pl.BlockSpec((tr, a.shape[1]), lambda i: (i, 0))
        return pl.BlockSpec(a.shape, lambda i: (0, 0))
    in_specs = [spec(a, k) for a, k in ins]
    out_specs = [spec(a, k) for a, k in outs] + [spec(a, "vec") for a in acc_outs]
    out_shape = [a for a, _ in outs] + list(acc_outs)
    vmem = 2 * sum(_nbytes((tr, a.shape[1]) if k == "row" else a.shape, a.dtype) for a, k in list(ins) + list(outs))
    return pl.pallas_call(
        body, name=name, grid=(T // tr,), in_specs=in_specs, out_specs=out_specs, out_shape=out_shape,
        compiler_params=_params(("arbitrary",), 3 * vmem + 8 * 2**20),
    )(*_hbm(*[a for a, _ in ins]))


def _sds(shape, dtype):
    return jax.ShapeDtypeStruct(shape, dtype)


def _accumulate(ref, val):
    @pl.when(pl.program_id(0) == 0)
    def _():
        ref[...] = jnp.zeros_like(ref)
    ref[...] += val


def _norm_in(x, g):
    T, D = x.shape

    def body(x_ref, g_ref, o_ref):
        o_ref[...] = _rms(x_ref[...], g_ref[...]).astype(BF16)

    return _row_call(body, "norm_in", T, D, [(x, "row"), (g, "vec")], [(_sds((T, D), BF16), "row")], _tile(T, 256, 16))[0]


def _mix_residual(x, m, g_post, g_pre):
    T, D = x.shape

    def body(x_ref, m_ref, gp_ref, gn_ref, h_ref, hn_ref):
        h = x_ref[...] + _rms(m_ref[...], gp_ref[...])
        h_ref[...] = h
        hn_ref[...] = _rms(h, gn_ref[...]).astype(BF16)

    return _row_call(body, "mix_residual", T, D, [(x, "row"), (m, "row"), (g_post, "vec"), (g_pre, "vec")],
                     [(_sds((T, D), F32), "row"), (_sds((T, D), BF16), "row")], _tile(T, 256, 16))


def _loss_head(h, f, g_post, target):
    T, D = h.shape

    def body(h_ref, f_ref, g_ref, t_ref, dy_ref, df_ref, dg_ref, loss_ref):
        f = f_ref[...]
        g = g_ref[...]
        diff = h_ref[...] + _rms(f, g) - t_ref[...]
        dy = diff * (1.0 / D)
        dy_ref[...] = dy
        df, dg = _rms_bwd(dy, f, g)
        df_ref[...] = df.astype(BF16)
        _accumulate(dg_ref, dg)
        _accumulate(loss_ref, jnp.full((1, LANES), 0.5 / D, F32) * jnp.sum(diff * diff))

    return _row_call(body, "loss_head", T, D, [(h, "row"), (f, "row"), (g_post, "vec"), (target, "row")],
                     [(_sds((T, D), F32), "row"), (_sds((T, D), BF16), "row")], _tile(T, 256, 16),
                     acc_outs=[_sds((1, D), F32), _sds((1, LANES), F32)])


def _ffn_residual_bwd(dy, dhn, h, g_pre, m, g_post):
    T, D = h.shape

    def body(dy_ref, dhn_ref, h_ref, gn_ref, m_ref, gp_ref, dh_ref, dm_ref, dgn_ref, dgp_ref):
        dhh, dgn = _rms_bwd(dhn_ref[...], h_ref[...], gn_ref[...])
        dh = dy_ref[...] + dhh
        dh_ref[...] = dh
        dm, dgp = _rms_bwd(dh, m_ref[...], gp_ref[...])
        dm_ref[...] = dm.astype(BF16)
        _accumulate(dgn_ref, dgn)
        _accumulate(dgp_ref, dgp)

    return _row_call(body, "ffn_residual_bwd", T, D,
                     [(dy, "row"), (dhn, "row"), (h, "row"), (g_pre, "vec"), (m, "row"), (g_post, "vec")],
                     [(_sds((T, D), F32), "row"), (_sds((T, D), BF16), "row")], _tile(T, 128, 16),
                     acc_outs=[_sds((1, D), F32), _sds((1, D), F32)])


def _input_bwd(dh, dxn, x, g):
    T, D = x.shape

    def body(dh_ref, dxn_ref, x_ref, g_ref, dx_ref, dg_ref):
        dx, dg = _rms_bwd(dxn_ref[...], x_ref[...], g_ref[...])
        dx_ref[...] = dh_ref[...] + dx
        _accumulate(dg_ref, dg)

    return _row_call(body, "input_bwd", T, D, [(dh, "row"), (dxn, "row"), (x, "row"), (g, "vec")],
                     [(_sds((T, D), F32), "row")], _tile(T, 256, 16), acc_outs=[_sds((1, D), F32)])


def _ffn_act(u, conv_w, conv_b):
    T, F2 = u.shape
    F = F2 // 2
    tc = _tile(F, 512, LANES)
    tr = _tile(T, 512, SUBLANES)
    nc = F // tc
    r8 = tr // SUBLANES

    def body(ug_ref, ugp_ref, uv_ref, uvp_ref, wg_ref, wv_ref, bg_ref, bv_ref, a_ref):
        first = pl.program_id(1) == 0
        cg = _causal_conv(ug_ref[...], jnp.where(first, 0.0, ugp_ref[...]), wg_ref, FFN_CONV) + bg_ref[...]
        cv = _causal_conv(uv_ref[...], jnp.where(first, 0.0, uvp_ref[...]), wv_ref, FFN_CONV) + bv_ref[...]
        a_ref[...] = (_gelu(cg) * cv).astype(BF16)

    cur = lambda off: pl.BlockSpec((tr, tc), lambda j, i: (i, j + off))
    prev = lambda off: pl.BlockSpec((SUBLANES, tc), lambda j, i: (jnp.maximum(i * r8 - 1, 0), j + off))
    wsp = lambda off: pl.BlockSpec((FFN_CONV, tc), lambda j, i: (0, j + off))
    bsp = lambda off: pl.BlockSpec((1, tc), lambda j, i: (0, j + off))
    return pl.pallas_call(
        body, name="ffn_act", grid=(nc, T // tr),
        in_specs=[cur(0), prev(0), cur(nc), prev(nc), wsp(0), wsp(nc), bsp(0), bsp(nc)],
        out_specs=pl.BlockSpec((tr, tc), lambda j, i: (i, j)),
        out_shape=_sds((T, F), BF16),
        compiler_params=_params(("parallel", "arbitrary"), 12 * _nbytes((tr, tc), F32) + 8 * 2**20),
    )(*_hbm(u, u, u, u, conv_w, conv_w, conv_b, conv_b))


def _ffn_act_bwd(u, conv_w, conv_b, da):
    T, F2 = u.shape
    F = F2 // 2
    tc = _tile(F, 512, LANES)
    tr = _tile(T, 512, SUBLANES)
    nc = F // tc
    r8 = tr // SUBLANES
    n8 = T // SUBLANES
    K = FFN_CONV

    def body(ug_ref, ugp_ref, ugn_ref, uv_ref, uvp_ref, uvn_ref, da_ref, dan_ref,
             wg_ref, wv_ref, bg_ref, bv_ref, du_ref, dwb_ref):
        i = pl.program_id(1)
        first = i == 0
        last = i == pl.num_programs(1) - 1

        def dconv(ug, ug_prev, uv, uv_prev, da_):
            cg = _causal_conv(ug, ug_prev, wg_ref, K) + bg_ref[...]
            cv = _causal_conv(uv, uv_prev, wv_ref, K) + bv_ref[...]
            act, act_grad = _gelu_and_grad(cg)
            return da_ * cv * act_grad, da_ * act

        ug, uv = ug_ref[...], uv_ref[...]
        ug_prev, uv_prev = jnp.where(first, 0.0, ugp_ref[...]), jnp.where(first, 0.0, uvp_ref[...])
        dcg, dcv = dconv(ug, ug_prev, uv, uv_prev, da_ref[...])
        dcgn, dcvn = dconv(ugn_ref[...], ug[tr - SUBLANES:], uvn_ref[...], uv[tr - SUBLANES:], dan_ref[...])
        du_ref[0] = _anti_conv(dcg, jnp.where(last, 0.0, dcgn), wg_ref, K).astype(BF16)
        du_ref[1] = _anti_conv(dcv, jnp.where(last, 0.0, dcvn), wv_ref, K).astype(BF16)

        @pl.when(first)
        def _():
            dwb_ref[...] = jnp.zeros_like(dwb_ref)

        for half, (dc, uo, uo_prev) in enumerate([(dcg, ug, ug_prev), (dcv, uv, uv_prev)]):
            rows = [jnp.sum(dc * _shift_down(uo, uo_prev, K - 1 - t), axis=0, keepdims=True) for t in range(K - 1)]
            rows += [jnp.sum(dc * uo, axis=0, keepdims=True), jnp.sum(dc, axis=0, keepdims=True)]
            rows += [jnp.zeros_like(rows[0])] * (SUBLANES - len(rows))
            dwb_ref[half] += jnp.concatenate(rows, axis=0)

    cur = lambda off: pl.BlockSpec((tr, tc), lambda j, i: (i, j + off))
    prev = lambda off: pl.BlockSpec((SUBLANES, tc), lambda j, i: (jnp.maximum(i * r8 - 1, 0), j + off))
    nxt = lambda off: pl.BlockSpec((SUBLANES, tc), lambda j, i: (jnp.minimum((i + 1) * r8, n8 - 1), j + off))
    wsp = lambda off: pl.BlockSpec((K, tc), lambda j, i: (0, j + off))
    bsp = lambda off: pl.BlockSpec((1, tc), lambda j, i: (0, j + off))
    return pl.pallas_call(
        body, name="ffn_act_bwd", grid=(nc, T // tr),
        in_specs=[cur(0), prev(0), nxt(0), cur(nc), prev(nc), nxt(nc), cur(0), nxt(0), wsp(0), wsp(nc), bsp(0), bsp(nc)],
        out_specs=[pl.BlockSpec((2, tr, tc), lambda j, i: (0, i, j)), pl.BlockSpec((2, SUBLANES, tc), lambda j, i: (0, 0, j))],
        out_shape=[_sds((2, T, F), BF16), _sds((2, SUBLANES, F), F32)],
        compiler_params=_params(("parallel", "arbitrary"), 24 * _nbytes((tr, tc), F32) + 8 * 2**20),
    )(*_hbm(u, u, u, u, u, u, da, da, conv_w, conv_w, conv_b, conv_b))


def _l2norm(s, scale):
    return s * (lax.rsqrt(jnp.sum(s * s, axis=-1, keepdims=True) + EPS) * scale)


def _dn_branch(proj, col0, conv_w, wcol0, l2, scale):
    T = proj.shape[0]
    W = DN_HEADS * HEAD_DIM
    tr = _tile(T, 2048, SUBLANES)
    r8 = tr // SUBLANES
    cb0, wb0 = col0 // HEAD_DIM, wcol0 // HEAD_DIM

    def body(u_ref, up_ref, w_ref, o_ref):
        first = pl.program_id(1) == 0
        s = _silu(_causal_conv(u_ref[...], jnp.where(first, 0.0, up_ref[...]), w_ref, SHORT_CONV))
        o_ref[...] = _l2norm(s, scale) if l2 else s

    return pl.pallas_call(
        body, name=f"dn_branch_{col0}", grid=(DN_HEADS, T // tr),
        in_specs=[pl.BlockSpec((tr, HEAD_DIM), lambda h, i: (i, cb0 + h)),
                  pl.BlockSpec((SUBLANES, HEAD_DIM), lambda h, i: (jnp.maximum(i * r8 - 1, 0), cb0 + h)),
                  pl.BlockSpec((SHORT_CONV, HEAD_DIM), lambda h, i: (0, wb0 + h))],
        out_specs=pl.BlockSpec((tr, HEAD_DIM), lambda h, i: (i, h)),
        out_shape=_sds((T, W), F32),
        compiler_params=_params(("parallel", "arbitrary"), 32 * _nbytes((tr, HEAD_DIM), F32) + 8 * 2**20),
    )(*_hbm(proj, proj, conv_w))


def _dn_branch_bwd(proj, col0, conv_w, wcol0, l2, scale, dy):
    T = proj.shape[0]
    W = DN_HEADS * HEAD_DIM
    tr = _tile(T, 2048, SUBLANES)
    r8 = tr // SUBLANES
    n8 = T // SUBLANES
    cb0, wb0 = col0 // HEAD_DIM, wcol0 // HEAD_DIM
    K = SHORT_CONV

    def body(u_ref, up_ref, un_ref, dy_ref, dyn_ref, w_ref, du_ref, dw_ref):
        i = pl.program_id(1)
        first = i == 0
        last = i == pl.num_programs(1) - 1

        def dconv(u, u_prev, dy_):
            c = _causal_conv(u, u_prev, w_ref, K)
            if l2:
                s = _silu(c)
                r = lax.rsqrt(jnp.sum(s * s, axis=-1, keepdims=True) + EPS)
                n = s * r
                ds = (scale * r) * (dy_ - n * jnp.sum(dy_ * n, axis=-1, keepdims=True))
            else:
                ds = dy_
            return ds * _silu_grad(c)

        u = u_ref[...]
        u_prev = jnp.where(first, 0.0, up_ref[...])
        dc = dconv(u, u_prev, dy_ref[...])
        dcn = jnp.where(last, 0.0, dconv(un_ref[...], u[tr - SUBLANES:], dyn_ref[...]))
        du_ref[...] = _anti_conv(dc, dcn, w_ref, K).astype(BF16)
        rows = [jnp.sum(dc * _shift_down(u, u_prev, K - 1 - t), axis=0, keepdims=True) for t in range(K - 1)]
        rows += [jnp.sum(dc * u, axis=0, keepdims=True)]
        rows += [jnp.zeros_like(rows[0])] * (SUBLANES - len(rows))
        upd = jnp.concatenate(rows, axis=0)

        @pl.when(first)
        def _():
            dw_ref[...] = jnp.zeros_like(dw_ref)
        dw_ref[...] += upd

    return pl.pallas_call(
        body, name=f"dn_branch_bwd_{col0}", grid=(DN_HEADS, T // tr),
        in_specs=[pl.BlockSpec((tr, HEAD_DIM), lambda h, i: (i, cb0 + h)),
                  pl.BlockSpec((SUBLANES, HEAD_DIM), lambda h, i: (jnp.maximum(i * r8 - 1, 0), cb0 + h)),
                  pl.BlockSpec((SUBLANES, HEAD_DIM), lambda h, i: (jnp.minimum((i + 1) * r8, n8 - 1), cb0 + h)),
                  pl.BlockSpec((tr, HEAD_DIM), lambda h, i: (i, h)),
                  pl.BlockSpec((SUBLANES, HEAD_DIM), lambda h, i: (jnp.minimum((i + 1) * r8, n8 - 1), h)),
                  pl.BlockSpec((K, HEAD_DIM), lambda h, i: (0, wb0 + h))],
        out_specs=[pl.BlockSpec((tr, HEAD_DIM), lambda h, i: (i, h)),
                   pl.BlockSpec((SUBLANES, HEAD_DIM), lambda h, i: (0, h))],
        out_shape=[_sds((T, W), BF16), _sds((SUBLANES, W), F32)],
        compiler_params=_params(("parallel", "arbitrary"), 32 * _nbytes((tr, HEAD_DIM), F32) + 8 * 2**20),
    )(*_hbm(proj, proj, proj, dy, dy, conv_w))


def _lane_masks(shape):
    lane = _iota2(shape, 1)
    return lane < DN_HEADS, (lane >= DN_HEADS) & (lane < 2 * DN_HEADS)


def _expand01(off):
    r = _iota2((LANES, DN_HEADS * HEAD_DIM), 0)
    c = _iota2((LANES, DN_HEADS * HEAD_DIM), 1)
    return (r == jnp.right_shift(c, int(math.log2(HEAD_DIM))) + off).astype(BF16)


def _select01(off):
    r = _iota2((DN_HEADS * HEAD_DIM, LANES), 0)
    c = _iota2((DN_HEADS * HEAD_DIM, LANES), 1)
    return (r == (c - off) * HEAD_DIM).astype(BF16)


def _dn_gates(proj, gate_block, a_log_l, dt_bias_l):
    T = proj.shape[0]
    C = DN_CHUNK
    W = DN_HEADS * HEAD_DIM

    def body(ba_ref, al_ref, dt_ref, gc_ref, beta_ref):
        ba = ba_ref[...]
        is_b, is_a = _lane_masks(ba.shape)
        g = jnp.where(is_a, -jnp.exp(al_ref[...]) * _softplus(ba + dt_ref[...]), 0.0)
        beta = jnp.where(is_b, _sigmoid(ba), 0.0)
        tri = (_iota2((C, C), 0) >= _iota2((C, C), 1)).astype(BF16)
        gc = _dot01_left(tri, g)
        gc_ref[...] = _dot01(gc, _expand01(DN_HEADS))
        beta_ref[...] = _dot01(beta, _expand01(0))

    vec = pl.BlockSpec((1, LANES), lambda n: (0, 0))
    return pl.pallas_call(
        body, name="dn_gates", grid=(T // C,),
        in_specs=[pl.BlockSpec((C, LANES), lambda n: (n, gate_block)), vec, vec],
        out_specs=[pl.BlockSpec((C, W), lambda n: (n, 0))] * 2,
        out_shape=[_sds((T, W), F32)] * 2,
        compiler_params=_params(("parallel",), 16 * 2**20),
    )(*_hbm(proj, a_log_l, dt_bias_l))


def _dn_gates_bwd(proj, gate_block, a_log_l, dt_bias_l, dgc_full, dbeta_full):
    T = proj.shape[0]
    C = DN_CHUNK
    W = DN_HEADS * HEAD_DIM

    def body(ba_ref, al_ref, dt_ref, dgc_ref, dbeta_ref, dba_ref, dal_ref, ddt_ref):
        ba = ba_ref[...]
        is_b, is_a = _lane_masks(ba.shape)
        ea = jnp.exp(al_ref[...])
        pre = ba + dt_ref[...]
        g = jnp.where(is_a, -ea * _softplus(pre), 0.0)
        beta = _sigmoid(ba)
        dgc = _dot01(dgc_ref[...], _select01(DN_HEADS))
        dbeta = _dot01(dbeta_ref[...], _select01(0))
        triu = (_iota2((C, C), 0) <= _iota2((C, C), 1)).astype(BF16)
        dg = _dot01_left(triu, dgc)
        da = jnp.where(is_a, dg * (-ea) * _sigmoid(pre), 0.0)
        dba_ref[...] = (da + jnp.where(is_b, dbeta * beta * (1.0 - beta), 0.0)).astype(BF16)
        _accumulate(dal_ref, jnp.sum(dg * g, axis=0, keepdims=True))
        _accumulate(ddt_ref, jnp.sum(da, axis=0, keepdims=True))

    vec = pl.BlockSpec((1, LANES), lambda n: (0, 0))
    full = pl.BlockSpec((C, W), lambda n: (n, 0))
    return pl.pallas_call(
        body, name="dn_gates_bwd", grid=(T // C,),
        in_specs=[pl.BlockSpec((C, LANES), lambda n: (n, gate_block)), vec, vec, full, full],
        out_specs=[pl.BlockSpec((C, LANES), lambda n: (n, 0)), vec, vec],
        out_shape=[_sds((T, LANES), BF16), _sds((1, LANES), F32), _sds((1, LANES), F32)],
        compiler_params=_params(("arbitrary",), 16 * 2**20),
    )(*_hbm(proj, a_log_l, dt_bias_l, dgc_full, dbeta_full))


def _unit_lower_inverse(L):
    C = L.shape[-1]
    row, col = _iota2((C, C), 0), _iota2((C, C), 1)
    eye = (row == col).astype(F32)
    sh = int(math.log2(INV_BLOCK))
    Ld = jnp.where(jnp.right_shift(row, sh) == jnp.right_shift(col, sh), L, 0.0)
    Lo = L - Ld
    X = eye - Ld
    P = Ld
    for _ in range(int(math.log2(INV_BLOCK)) - 1):
        P = _mm3(P, P)
        X = X + _mm3(X, P)
    N = _mm3(X, Lo)
    Y = eye - N
    P = N
    for _ in range(int(math.log2(C // INV_BLOCK)) - 1):
        P = _mm3(P, P)
        Y = Y + _mm3(Y, P)
    return _mm3(Y, X)


def _dn_chunk_common(q, k, v, gc, beta, gl):
    C = q.shape[-2]
    row, col = _iota2((C, C), 0), _iota2((C, C), 1)
    causal, strict = row >= col, row > col
    eg = jnp.exp(gc)
    decay = jnp.where(causal, jnp.exp(jnp.where(causal, gc - _t(gc), 0.0)), 0.0)
    kb, vb = k * beta, v * beta
    L = jnp.where(strict, _dot(kb, k, _NT) * decay, 0.0)
    Aqk = jnp.where(causal, _dot(q, k, _NT) * decay, 0.0)
    ektg = jnp.exp(gl - gc)
    return dict(causal=causal, strict=strict, eg=eg, decay=decay, kb=kb, vb=vb, L=L, Aqk=Aqk, ektg=ektg,
                kbg=kb * eg, kte=k * ektg, qd=q * eg, egl=jnp.exp(gl))


def _dn_scan(qn, kn, vn, gc_full, beta_full, proj, z_col0, gain):
    T, W = qn.shape
    C = DN_CHUNK
    N = T // C
    H = DN_HEADS
    G = DN_GROUP
    GW = G * HEAD_DIM
    zb0 = z_col0 // GW

    def body(q_ref, k_ref, v_ref, gc_ref, beta_ref, z_ref, gain_ref, o_ref, mix_ref, tm_ref, s_ref, S):
        @pl.when(pl.program_id(1) == 0)
        def _():
            S[...] = jnp.zeros_like(S)

        heads = lambda ref, rows=slice(None): jnp.stack([ref[rows, g * HEAD_DIM:(g + 1) * HEAD_DIM] for g in range(G)])
        q, k, v, gc, beta = heads(q_ref), heads(k_ref), heads(v_ref), heads(gc_ref), heads(beta_ref)
        gl = heads(gc_ref, slice(C - 1, C))
        c = _dn_chunk_common(q, k, v, gc, beta, gl)
        Tm = _unit_lower_inverse(c["L"])
        u = _dot(Tm, c["vb"])
        w = _dot(Tm, c["kbg"])
        S0 = S[...]
        vnew = u - _dot(w, S0)
        o = _dot(c["qd"], S0) + _dot(c["Aqk"], vnew)
        S[...] = S0 * c["egl"] + _dot(c["kte"], vnew, _TN)
        tm_ref[...] = Tm
        s_ref[...] = S0
        mix = (_rms(o, gain_ref[...]) * _silu(heads(z_ref))).astype(BF16)
        for g in range(G):
            sl = slice(g * HEAD_DIM, (g + 1) * HEAD_DIM)
            o_ref[:, sl] = o[g]
            mix_ref[:, sl] = mix[g]

    blk = pl.BlockSpec((C, GW), lambda h, n: (n, h))
    mat = pl.BlockSpec((G, None, C, C), lambda h, n: (h, n, 0, 0))
    return pl.pallas_call(
        body, name="dn_scan", grid=(H // G, N),
        in_specs=[blk, blk, blk, blk, blk, pl.BlockSpec((C, GW), lambda h, n: (n, zb0 + h)),
                  pl.BlockSpec((1, HEAD_DIM), lambda h, n: (0, 0))],
        out_specs=[blk, blk, mat, mat],
        out_shape=[_sds((T, W), F32), _sds((T, W), BF16), _sds((H, N, C, C), F32), _sds((H, N, C, C), F32)],
        scratch_shapes=[pltpu.VMEM((G, HEAD_DIM, HEAD_DIM), F32)],
        compiler_params=_params(("parallel", "arbitrary"), 32 * 2**20),
    )(*_hbm(qn, kn, vn, gc_full, beta_full, proj, gain))


def _dn_scan_bwd(qn, kn, vn, gc_full, beta_full, proj, z_col0, gain, o_raw, tm_all, s_all, dmix, dmix_col0):
    T, W = qn.shape
    C = DN_CHUNK
    N = T // C
    H = DN_HEADS
    G = DN_GROUP
    GW = G * HEAD_DIM
    zb0 = z_col0 // GW
    mb0 = dmix_col0 // GW

    def body(q_ref, k_ref, v_ref, gc_ref, beta_ref, z_ref, gain_ref, o_ref, tm_ref, s_ref, dmix_ref,
             dq_ref, dk_ref, dv_ref, dgc_ref, dbeta_ref, dz_ref, dgain_ref, dS):
        @pl.when(pl.program_id(1) == 0)
        def _():
            dS[...] = jnp.zeros_like(dS)

        @pl.when((pl.program_id(0) == 0) & (pl.program_id(1) == 0))
        def _():
            dgain_ref[...] = jnp.zeros_like(dgain_ref)

        heads = lambda ref, rows=slice(None): jnp.stack([ref[rows, g * HEAD_DIM:(g + 1) * HEAD_DIM] for g in range(G)])
        total = lambda x: jnp.sum(jnp.sum(x, axis=-1, keepdims=True), axis=-2, keepdims=True)
        gain = gain_ref[...]
        o, z, dmix = heads(o_ref), heads(z_ref), heads(dmix_ref)
        dz = (dmix * _rms(o, gain) * _silu_grad(z)).astype(BF16)
        do, dgain = _rms_bwd(dmix * _silu(z), o, gain)
        dgain_ref[...] += jnp.sum(dgain, axis=0)

        q, k, v, gc, beta = heads(q_ref), heads(k_ref), heads(v_ref), heads(gc_ref), heads(beta_ref)
        gl = heads(gc_ref, slice(C - 1, C))
        c = _dn_chunk_common(q, k, v, gc, beta, gl)
        Tm, S0, dS1 = tm_ref[...], s_ref[...], dS[...]
        w = _dot(Tm, c["kbg"])
        vnew = _dot(Tm, c["vb"]) - _dot(w, S0)

        dvnew = _dot(c["Aqk"], do, _TN) + _dot(c["kte"], dS1)
        dAqk = jnp.where(c["causal"], _dot(do, vnew, _NT), 0.0)
        dqd = _dot(do, S0, _NT)
        dkte = _dot(vnew, dS1, _NT)
        dgl = total(dS1 * S0) * c["egl"]
        dw = -_dot(dvnew, S0, _NT)
        dS[...] = dS1 * c["egl"] + _dot(c["qd"], do, _TN) - _dot(w, dvnew, _TN)

        dTm = _dot(dvnew, c["vb"], _NT) + _dot(dw, c["kbg"], _NT)
        dvb = _dot(Tm, dvnew, _TN)
        dkbg = _dot(Tm, dw, _TN)
        dL = jnp.where(c["strict"], -_mm3(_mm3(Tm, dTm, _TN), Tm, _NT), 0.0)
        dP = dL * c["decay"]
        dQ = dAqk * c["decay"]
        M = dL * c["L"] + dAqk * c["Aqk"]
        dkb = _dot(dP, k) + dkbg * c["eg"]
        dk = _dot(dP, c["kb"], _TN) + _dot(dQ, q, _TN) + dkte * c["ektg"] + dkb * beta
        dq = _dot(dQ, k) + dqd * c["eg"]
        tk = _rowsum(dkte * c["kte"])
        dgc = (_rowsum(M) - _rowsum(_t(M)) + _rowsum(dqd * c["qd"]) - tk + _rowsum(dkbg * c["kbg"]))
        dgl = dgl + total(tk)
        dgc = jnp.broadcast_to(dgc, q.shape) + jnp.where(_iota2((C, HEAD_DIM), 0) == C - 1, dgl, 0.0)
        dv = dvb * beta
        dbeta = jnp.broadcast_to(_rowsum(dkb * k) + _rowsum(dvb * v), q.shape)
        for g in range(G):
            sl = slice(g * HEAD_DIM, (g + 1) * HEAD_DIM)
            dz_ref[:, sl] = dz[g]
            dq_ref[:, sl] = dq[g]
            dk_ref[:, sl] = dk[g]
            dv_ref[:, sl] = dv[g]
            dgc_ref[:, sl] = dgc[g]
            dbeta_ref[:, sl] = dbeta[g]

    rev = lambda off: pl.BlockSpec((C, GW), lambda h, n: (N - 1 - n, off + h))
    mat = pl.BlockSpec((G, None, C, C), lambda h, n: (h, N - 1 - n, 0, 0))
    vec = pl.BlockSpec((1, HEAD_DIM), lambda h, n: (0, 0))
    return pl.pallas_call(
        body, name="dn_scan_bwd", grid=(H // G, N),
        in_specs=[rev(0), rev(0), rev(0), rev(0), rev(0), rev(zb0), vec, rev(0), mat, mat, rev(mb0)],
        out_specs=[rev(0)] * 6 + [vec],
        out_shape=[_sds((T, W), F32)] * 5 + [_sds((T, W), BF16), _sds((1, HEAD_DIM), F32)],
        scratch_shapes=[pltpu.VMEM((G, HEAD_DIM, HEAD_DIM), F32)],
        compiler_params=_params(("arbitrary", "arbitrary"), 40 * 2**20),
    )(*_hbm(qn, kn, vn, gc_full, beta_full, proj, gain, o_raw, tm_all, s_all, dmix))


def _sb_terms(z, ahead, first_key):
    lb = jnp.minimum(z, 0.0) - jnp.log(1.0 + jnp.exp(-jnp.abs(z)))
    if ahead is None:
        return None, lb, lb - z
    valid = ahead < -first_key
    return valid, lb, jnp.where(valid, lb - z, 0.0)


def _masked(valid, x):
    return x if valid is None else jnp.where(valid, x, 0.0)


def _sb_attention(qkv, gain, tq_cap=2048):
    T = qkv.shape[0]
    H = SB_HEADS
    B = min(SB_KEYS, T)
    TQ = _tile(T, tq_cap, B)
    per = TQ // B

    assert per % 2 == 0
    n_saved = per * (T // TQ) * (T // TQ + 1) // 2

    def body(q_ref, k_ref, v_ref, gain_ref, o_ref, mix_ref, att_hbm, lb_hbm, att_buf, lb_buf, sems):
        h, i = pl.program_id(0), pl.program_id(1)
        q = q_ref[...].astype(BF16)
        upper = (_iota2((B, B), 0) > _iota2((B, B), 1)).astype(BF16)
        ahead = _iota2((TQ, B), 1) - _iota2((TQ, B), 0)
        last = (i + 1) * per - 1
        base = per * (i * (i + 1) // 2)

        def save(slot, pair):
            return (pltpu.make_async_copy(att_buf.at[slot], att_hbm.at[h, pair], sems.at[0, slot]),
                    pltpu.make_async_copy(lb_buf.at[slot], lb_hbm.at[h, pair], sems.at[1, slot]))

        def pair(j, slot, r0, acc, R):
            n = TQ - (r0 or 0)
            top = slice(r0 or 0, TQ)
            rows = pl.ds(pl.multiple_of(j * B, B), B)
            z = _dot(q[top], k_ref[rows, :], _NT) * (HEAD_DIM ** -0.5)
            valid, lb, l1m = _sb_terms(z, None if r0 is None else ahead[top], r0)
            att = _masked(valid, jnp.exp(lb + R[top] + _dot01(l1m, upper, passes=2))).astype(BF16)
            att_buf[slot, pl.ds(TQ - n, n), :] = att
            lb_buf[slot, pl.ds(TQ - n, n), :] = (lb if valid is None else jnp.where(valid, lb, -1e30)).astype(BF16)
            if r0:
                att_buf[slot, pl.ds(0, r0), :] = jnp.zeros((r0, B), BF16)
                lb_buf[slot, pl.ds(0, r0), :] = jnp.full((r0, B), -1e30, BF16)
            for c in save(slot, base + j):
                c.start()
            d_acc, d_R = _dot(att, v_ref[rows, :]), _rowsum(l1m)
            if r0:
                d_acc = jnp.concatenate([jnp.zeros((r0, HEAD_DIM), F32), d_acc], axis=0)
                d_R = jnp.concatenate([jnp.zeros((r0, 1), F32), d_R], axis=0)
            return acc + d_acc, R + d_R

        def step(jj, carry):
            slot = jj % 2

            @pl.when(jj >= 2)
            def _():
                for c in save(slot, 0):
                    c.wait()

            return pair(last - jj, slot, None, *carry)

        carry = (jnp.zeros((TQ, HEAD_DIM), F32), jnp.zeros((TQ, 1), F32))
        for jj in range(per):
            if jj >= 2:
                for c in save(jj % 2, 0):
                    c.wait()
            carry = pair(last - jj, jj % 2, (per - 1 - jj) * B, *carry)
        acc, _ = lax.fori_loop(per, last + 1, step, carry)
        for slot in range(2):
            for c in save(slot, 0):
                c.wait()
        o_ref[...] = acc
        mix_ref[...] = _rms(acc, gain_ref[...]).astype(BF16)

    head = lambda off: pl.BlockSpec((T, HEAD_DIM), lambda h, i: (0, off + h))
    blk = pl.BlockSpec((TQ, HEAD_DIM), lambda h, i: (i, h))
    return pl.pallas_call(
        body, name="sb_attention", grid=(H, T // TQ),
        in_specs=[blk, head(H), head(2 * H), pl.BlockSpec((1, HEAD_DIM), lambda h, i: (0, 0))],
        out_specs=[blk, blk, ANY_SPEC, ANY_SPEC],
        out_shape=[_sds((T, H * HEAD_DIM), F32), _sds((T, H * HEAD_DIM), BF16),
                   _sds((H, n_saved, TQ, B), BF16), _sds((H, n_saved, TQ, B), BF16)],
        scratch_shapes=[pltpu.VMEM((2, TQ, B), BF16), pltpu.VMEM((2, TQ, B), BF16), pltpu.SemaphoreType.DMA((2, 2))],
        compiler_params=_params(("parallel", "arbitrary"), 8 * _nbytes((T, HEAD_DIM), BF16) + 32 * _nbytes((TQ, B), F32)),
    )(*_hbm(qkv, qkv, qkv, gain))


def _sb_attention_bwd(qkv, gain, o_raw, att_all, lb_all, dmix):
    T = qkv.shape[0]
    H = SB_HEADS
    TQ, B = att_all.shape[2:]
    per = TQ // B
    scale = HEAD_DIM ** -0.5

    def body(q_ref, k_ref, v_ref, gain_ref, o_ref, dmix_ref, att_hbm, lb_hbm, dq_ref, dk_ref, dv_ref, dgain_ref,
             att_buf, lb_buf, sems):
        h, i = pl.program_id(0), pl.program_id(1)

        @pl.when(i == 0)
        def _():
            dk_ref[...] = jnp.zeros_like(dk_ref)
            dv_ref[...] = jnp.zeros_like(dv_ref)

        @pl.when((pl.program_id(0) == 0) & (i == 0))
        def _():
            dgain_ref[...] = jnp.zeros_like(dgain_ref)

        q = q_ref[...].astype(BF16)
        o = o_ref[...]
        do, dgain = _rms_bwd(dmix_ref[...], o, gain_ref[...])
        dgain_ref[...] += dgain
        do_b = do.astype(BF16)
        before = (_iota2((B, B), 0) < _iota2((B, B), 1)).astype(BF16)
        base = per * (i * (i + 1) // 2)

        def fetch(slot, pair):
            return (pltpu.make_async_copy(att_hbm.at[h, pair], att_buf.at[slot], sems.at[0, slot]),
                    pltpu.make_async_copy(lb_hbm.at[h, pair], lb_buf.at[slot], sems.at[1, slot]))

        for c in fetch(0, base):
            c.start()

        def pair(j, slot, r0, dq, PG):
            top = slice(r0, TQ)
            rows = pl.ds(pl.multiple_of(j * B, B), B)
            kj = k_ref[rows, :]
            att = att_buf[slot, pl.ds(r0, TQ - r0), :]
            sig = jnp.exp(lb_buf[slot, pl.ds(r0, TQ - r0), :].astype(F32))
            G = _dot(do_b[top], v_ref[rows, :], _NT) * att.astype(F32)
            dv_ref[rows, :] += _dot(att, do_b[top], _TN)
            cum = PG[top] + _dot01(G, before, passes=2)
            dz = (G * (1.0 - sig) - sig * cum) * scale
            dk_ref[rows, :] += _dot(dz, q[top], _TN)
            d_dq, d_PG = _dot(dz, kj), _rowsum(G)
            if r0:
                d_dq = jnp.concatenate([jnp.zeros((r0, HEAD_DIM), F32), d_dq], axis=0)
                d_PG = jnp.concatenate([jnp.zeros((r0, 1), F32), d_PG], axis=0)
            return dq + d_dq, PG + d_PG

        def step(j, carry):
            slot = j % 2
            for c in fetch(slot, 0):
                c.wait()
            for c in fetch(1 - slot, base + j + 1):
                c.start()
            return pair(j, slot, 0, *carry)

        carry = lax.fori_loop(0, i * per, step, (jnp.zeros((TQ, HEAD_DIM), F32), jnp.zeros((TQ, 1), F32)))
        for c_blk in range(per):
            slot = c_blk % 2
            for c in fetch(slot, 0):
                c.wait()
            if c_blk + 1 < per:
                for c in fetch(1 - slot, base + i * per + c_blk + 1):
                    c.start()
            carry = pair(i * per + c_blk, slot, c_blk * B, *carry)
        dq_ref[...] = carry[0].astype(BF16)

    head = lambda off: pl.BlockSpec((T, HEAD_DIM), lambda h, i: (0, off + h))
    blk = pl.BlockSpec((TQ, HEAD_DIM), lambda h, i: (i, h))
    vec = pl.BlockSpec((1, HEAD_DIM), lambda h, i: (0, 0))
    return pl.pallas_call(
        body, name="sb_attention_bwd", grid=(H, T // TQ),
        in_specs=[blk, head(H), head(2 * H), vec, blk, blk, ANY_SPEC, ANY_SPEC],
        out_specs=[blk, head(0), head(0), vec],
        out_shape=[_sds((T, H * HEAD_DIM), BF16), _sds((T, H * HEAD_DIM), F32), _sds((T, H * HEAD_DIM), F32),
                   _sds((1, HEAD_DIM), F32)],
        scratch_shapes=[pltpu.VMEM((2, TQ, B), BF16), pltpu.VMEM((2, TQ, B), BF16), pltpu.SemaphoreType.DMA((2, 2))],
        compiler_params=_params(("arbitrary", "arbitrary"), 8 * _nbytes((T, HEAD_DIM), F32) + 32 * _nbytes((TQ, B), F32)),
    )(*_hbm(qkv, qkv, qkv, gain, o_raw, dmix, att_all, lb_all))


def _adamw_math(w, g, m, v):
    m = ADAM_B1 * m + (1.0 - ADAM_B1) * g
    v = ADAM_B2 * v + (1.0 - ADAM_B2) * (g * g)
    m_hat = m / (1.0 - ADAM_B1 ** ADAM_STEP)
    v_hat = v / (1.0 - ADAM_B2 ** ADAM_STEP)
    delta = -ADAM_LR * (m_hat / (jnp.sqrt(v_hat) + ADAM_EPS) + ADAM_WD * w)
    return delta, m, v


def _adamw_sharded(parts, w, m, v, name):
    _, R, C = w.shape
    if R % SUBLANES == 0:
        tr, tc = _tile(R, max(SUBLANES, (2**20 // (4 * C)) // SUBLANES * SUBLANES), SUBLANES), C
    else:
        tr, tc = R, _tile(C, max(LANES, (2**20 // (4 * R)) // LANES * LANES), LANES)

    def body(p_ref, w_ref, m_ref, v_ref, g_ref, d_ref, nm_ref, nv_ref):
        g = p_ref[0].astype(F32)
        for d in range(1, N_DEV):
            g = g + p_ref[d].astype(F32)
        g_ref[...] = g
        d_ref[...], nm_ref[...], nv_ref[...] = _adamw_math(w_ref[...], g, m_ref[...], v_ref[...])

    blk = pl.BlockSpec((None, tr, tc), lambda i, j: (0, i, j))
    return pl.pallas_call(
        body, name=name, grid=(R // tr, C // tc),
        in_specs=[pl.BlockSpec((N_DEV, tr, tc), lambda i, j: (0, i, j)), blk, blk, blk],
        out_specs=[blk] * 4, out_shape=[_sds((1, R, C), F32)] * 4,
        compiler_params=_params(("parallel", "parallel"), 40 * 2**20),
    )(*_hbm(parts, w, m, v))


def _adamw_packed(g, w, m, v):
    def body(g_ref, w_ref, m_ref, v_ref, d_ref, nm_ref, nv_ref):
        d_ref[...], nm_ref[...], nv_ref[...] = _adamw_math(w_ref[...], g_ref[...], m_ref[...], v_ref[...])

    return pl.pallas_call(body, name="adamw_packed", out_shape=[_sds(g.shape, F32)] * 3,
                          compiler_params=_params((), 16 * 2**20))(g, w, m, v)


def _my_place():
    x, y, c = lax.axis_index("x"), lax.axis_index("y"), lax.axis_index("c")
    return x, y, c


def _peer(place, k):
    x, y, c = place
    return (1 - x if k & 4 else x, 1 - y if k & 2 else y, 1 - c if k & 1 else c)


def _index(place):
    x, y, c = place
    return 4 * x + 2 * y + c


HBM_SPEC = pl.BlockSpec(memory_space=pltpu.HBM)


def _all_gather(block, name):
    R, C = block.shape

    def body(x_ref, out_ref, send_sems, recv_sems, local_sem):
        me = _my_place()
        sibling = _peer(me, 1)
        chips = [2, 4, 6]

        def copy(sem, origin, to, src=None):
            slot = out_ref.at[_index(origin)]
            return pltpu.make_async_remote_copy(
                src_ref=slot if src is None else src, dst_ref=slot, send_sem=send_sems.at[sem], recv_sem=recv_sems.at[sem],
                device_id=to, device_id_type=MESH)

        mine = pltpu.make_async_copy(x_ref, out_ref.at[_index(me)], local_sem)
        mine.start()
        first = [copy(0, me, sibling, src=x_ref)] + [copy(1 + n, me, _peer(me, k), src=x_ref) for n, k in enumerate(chips)]
        for cp in first:
            cp.start()
        passed = [copy(4 + n, _peer(me, k), sibling) for n, k in enumerate(chips)]
        for n, k in enumerate(chips):
            copy(1 + n, _peer(me, k), me).wait_recv()
            passed[n].start()
        copy(0, sibling, me).wait_recv()
        for n, k in enumerate(chips):
            copy(4 + n, _peer(sibling, k), me).wait_recv()
        for cp in first + passed:
            cp.wait_send()
        mine.wait()

    return pl.pallas_call(
        body, name=name, in_specs=[HBM_SPEC], out_specs=HBM_SPEC,
        out_shape=_sds((N_DEV, R, C), block.dtype),
        scratch_shapes=[pltpu.SemaphoreType.DMA((7,)), pltpu.SemaphoreType.DMA((7,)), pltpu.SemaphoreType.DMA],
    )(block)


SEM_SPEC = pl.BlockSpec(memory_space=pltpu.SEMAPHORE)
ANY_SPEC = pl.BlockSpec(memory_space=pl.ANY)
_EFFECT = pltpu.SideEffectType.DATAFLOW_SIDE_EFFECTING


def _spread_start(x, per_peer, name, after):
    R, C = x.shape[-2:]

    def body(x_ref, land_ref, after_ref, send_sems, recv_sems, x_thru, land_thru, token):
        me = _my_place()
        for k in range(1, N_DEV):
            to = _peer(me, k)
            pltpu.make_async_remote_copy(
                src_ref=x_ref.at[_index(to)] if per_peer else x_ref, dst_ref=land_ref.at[_index(me)],
                send_sem=send_sems.at[k - 1], recv_sem=recv_sems.at[k - 1], device_id=to, device_id_type=MESH).start()
        token[...] = jnp.zeros_like(token)

    land = lax.empty((N_DEV, R, C), x.dtype)
    send_sems, recv_sems, x_thru, land_thru, token = pl.pallas_call(
        body, name=name,
        out_shape=(pltpu.SemaphoreType.DMA((N_DEV - 1,)), pltpu.SemaphoreType.DMA((N_DEV - 1,)),
                   pltpu.HBM(x.shape, x.dtype), pltpu.HBM(land.shape, land.dtype), _sds((SUBLANES, LANES), F32)),
        in_specs=(HBM_SPEC, HBM_SPEC, ANY_SPEC),
        out_specs=(SEM_SPEC, SEM_SPEC, HBM_SPEC, HBM_SPEC, pl.BlockSpec(memory_space=pltpu.VMEM)),
        input_output_aliases={0: 2, 1: 3},
        compiler_params=pltpu.CompilerParams(has_side_effects=_EFFECT),
    )(pltpu.with_memory_space_constraint(x, pltpu.HBM), pltpu.with_memory_space_constraint(land, pltpu.HBM), after)
    return (send_sems, recv_sems, x_thru, land_thru), token


def _spread_wait(state, per_peer, name, after):
    send_sems, recv_sems, x_thru, land_thru = state

    def body(x_ref, land_ref, send_sems, recv_sems, after_ref, x_dead, got_ref):
        me = _my_place()
        for k in range(1, N_DEV):
            frm = _peer(me, k)
            copy = pltpu.make_async_remote_copy(
                src_ref=x_ref.at[_index(frm)] if per_peer else x_ref, dst_ref=land_ref.at[_index(frm)],
                send_sem=send_sems.at[k - 1], recv_sem=recv_sems.at[k - 1], device_id=frm, device_id_type=MESH)
            copy.wait_send()
            copy.wait_recv()

    x_back, got = pl.pallas_call(
        body, name=name,
        out_shape=(pltpu.HBM(x_thru.shape, x_thru.dtype), pltpu.HBM(land_thru.shape, land_thru.dtype)),
        in_specs=(HBM_SPEC, HBM_SPEC, SEM_SPEC, SEM_SPEC, ANY_SPEC), out_specs=(HBM_SPEC, HBM_SPEC),
        input_output_aliases={0: 0, 1: 1},
        compiler_params=pltpu.CompilerParams(has_side_effects=_EFFECT),
    )(x_thru, land_thru, send_sems, recv_sems, after)
    me = _index(_my_place())
    own = lax.dynamic_index_in_dim(x_back, me, axis=0, keepdims=True) if per_peer else x_back[None]
    return lax.dynamic_update_slice_in_dim(got, own, me, axis=0)


def _all_reduce_packed(vec, after):
    R, L = vec.shape

    def body(x_ref, after_ref, out_ref, buf, send_sems, recv_sems):
        me = _my_place()
        buf[_index(me)] = x_ref[...]
        copies = []
        for k in range(1, N_DEV):
            to = _peer(me, k)
            cp = pltpu.make_async_remote_copy(
                src_ref=x_ref, dst_ref=buf.at[_index(me)],
                send_sem=send_sems.at[k - 1], recv_sem=recv_sems.at[k - 1], device_id=to, device_id_type=MESH)
            cp.start()
            copies.append(cp)
        for k in range(1, N_DEV):
            frm = _peer(me, k)
            pltpu.make_async_remote_copy(
                src_ref=x_ref, dst_ref=buf.at[_index(frm)],
                send_sem=send_sems.at[k - 1], recv_sem=recv_sems.at[k - 1], device_id=frm, device_id_type=MESH).wait_recv()
        for cp in copies:
            cp.wait_send()
        acc = buf[0]
        for d in range(1, N_DEV):
            acc = acc + buf[d]
        out_ref[...] = acc

    vm = pl.BlockSpec(memory_space=pltpu.VMEM)
    return pl.pallas_call(
        body, name="all_reduce_packed", in_specs=[vm, ANY_SPEC], out_specs=vm, out_shape=_sds((R, L), F32),
        scratch_shapes=[pltpu.VMEM((N_DEV, R, L), F32), pltpu.SemaphoreType.DMA((7,)), pltpu.SemaphoreType.DMA((7,))],
        compiler_params=pltpu.CompilerParams(vmem_limit_bytes=32 * 2**20),
    )(vec, after)


def _pack(arrays):
    rows = []
    for a in arrays:
        f = a.reshape(-1).astype(F32)
        pad = (-f.shape[0]) % LANES
        rows.append(jnp.pad(f, (0, pad)).reshape(-1, LANES))
    out = jnp.concatenate(rows, axis=0)
    return jnp.pad(out, ((0, (-out.shape[0]) % SUBLANES), (0, 0)))


def _unpack(packed, shapes):
    out, r = [], 0
    for s in shapes:
        n = math.prod(s)
        nr = -(-n // LANES)
        out.append(packed[r:r + nr].reshape(-1)[:n].reshape(s))
        r += nr
    return out


def _row_blocks(g):
    R, C = g.shape
    return g.astype(BF16).reshape(N_DEV, R // N_DEV, C)


def kernel(x, w_in, sb_out_gain, dn_conv_w, dn_a_log, dn_dt_bias, dn_out_gain, w_out, ln_mix_pre, ln_mix_post, w_up, ffn_conv_w, ffn_conv_b, w_down, ln_ffn_pre, ln_ffn_post, loss_target, m_w_in, m_sb_out_gain, m_dn_conv_w, m_dn_a_log, m_dn_dt_bias, m_dn_out_gain, m_w_out, m_ln_mix_pre, m_ln_mix_post, m_w_up, m_ffn_conv_w, m_ffn_conv_b, m_w_down, m_ln_ffn_pre, m_ln_ffn_post, v_w_in, v_sb_out_gain, v_dn_conv_w, v_dn_a_log, v_dn_dt_bias, v_dn_out_gain, v_w_out, v_ln_mix_pre, v_ln_mix_post, v_w_up, v_ffn_conv_w, v_ffn_conv_b, v_w_down, v_ln_ffn_pre, v_ln_ffn_post):
    T, D = x.shape[1], x.shape[2]
    SBW = SB_HEADS * HEAD_DIM
    DNW = DN_HEADS * HEAD_DIM
    in_cols = 3 * SBW + 4 * DNW + 2 * DN_HEADS
    main_cols = 3 * SBW + 4 * DNW
    in_pad = main_cols + LANES
    qkv0, z0 = 3 * SBW, 3 * SBW + 3 * DNW
    gate_block = main_cols // LANES
    x2, tgt = x[0], loss_target[0]

    g_in = _all_gather(jnp.swapaxes(w_in[0], 0, 1).astype(BF16), "gather_w_in")
    small_w = _all_gather(_pack([dn_conv_w[0], ffn_conv_w[0]]), "gather_conv_w")
    st_out, tok = _spread_start(w_out[0].astype(BF16), False, "gather_w_out_start", g_in)
    st_up, tok = _spread_start(w_up[0].astype(BF16), False, "gather_w_up_start", tok)
    st_down, tok_gather = _spread_start(w_down[0].astype(BF16), False, "gather_w_down_start", tok)
    w_in_t = jnp.pad(g_in.reshape(in_cols, D), ((0, in_pad - in_cols), (0, 0)))
    parts = [_unpack(small_w[d], [dn_conv_w.shape[1:], ffn_conv_w.shape[1:]]) for d in range(N_DEV)]
    dn_cw = jnp.concatenate([p[0] for p in parts], axis=1)
    ffn_cw = jnp.concatenate([p[1] for p in parts], axis=1)
    lane_pad = lambda a, off: jnp.pad(a, ((0, 0), (off, LANES - off - a.shape[1])))
    a_log_l, dt_bias_l = lane_pad(dn_a_log, DN_HEADS), lane_pad(dn_dt_bias, DN_HEADS)

    xn = _norm_in(x2, ln_mix_pre)
    proj = _matmul(xn, w_in_t, "nt", F32, "proj_in", tm_cap=512, tn_cap=2432, after=tok_gather)
    o_sb, mix_sb, sb_att, sb_lb = _sb_attention(proj, sb_out_gain)
    qn = _dn_branch(proj, qkv0, dn_cw, 0, True, HEAD_DIM ** -0.5)
    kn = _dn_branch(proj, qkv0 + DNW, dn_cw, DNW, True, 1.0)
    vn = _dn_branch(proj, qkv0 + 2 * DNW, dn_cw, 2 * DNW, False, 1.0)
    gc_full, beta_full = _dn_gates(proj, gate_block, a_log_l, dt_bias_l)
    o_dn, mix_dn, tm_all, s_all = _dn_scan(qn, kn, vn, gc_full, beta_full, proj, z0, dn_out_gain)
    mix = jnp.concatenate([mix_sb, mix_dn], axis=1)
    w_out_f = _spread_wait(st_out, False, "gather_w_out_wait", mix).reshape(w_out.shape[1] * N_DEV, D)
    m = _matmul(mix, w_out_f, "nn", F32, "proj_out")
    h, hn = _mix_residual(x2, m, ln_mix_post, ln_ffn_pre)
    w_up_cut = _spread_wait(st_up, False, "gather_w_up_wait", hn)
    u = _matmul(hn, w_up_cut, "nn", F32, "ffn_up", tn_cap=w_up.shape[2], b_cut=True)
    act = _ffn_act(u, ffn_cw, ffn_conv_b)
    w_down_f = _spread_wait(st_down, False, "gather_w_down_wait", act).reshape(w_down.shape[1] * N_DEV, D)
    f = _matmul(act, w_down_f, "nn", F32, "ffn_down", tk_cap=2816)
    dy, df, d_ln_ffn_post, loss_part = _loss_head(h, f, ln_ffn_post, tgt)

    d_w_down = _matmul(act, df, "tn", BF16, "grad_w_down")
    st_xd, tok = _spread_start(_row_blocks(d_w_down), True, "exchange_w_down_start", loss_part)
    da = _matmul(df, w_down_f, "nt", F32, "bwd_ffn_down", after=tok)
    du, d_ffn_cwb = _ffn_act_bwd(u, ffn_cw, ffn_conv_b, da)
    d_ffn_cwb = jnp.concatenate([d_ffn_cwb[0], d_ffn_cwb[1]], axis=1)
    d_w_up_cut = _matmul(hn, du, "tn", BF16, "grad_w_up", b_cut=True, out_cut=True)
    st_xu, tok = _spread_start(d_w_up_cut, True, "exchange_w_up_start", d_ffn_cwb)
    dhn = _matmul(du, w_up_cut, "nt", F32, "bwd_ffn_up", after=tok, a_cut=True, b_cut=True)
    dh, dm, d_ln_ffn_pre, d_ln_mix_post = _ffn_residual_bwd(dy, dhn, h, ln_ffn_pre, m, ln_mix_post)

    d_w_out = _matmul(mix, dm, "tn", BF16, "grad_w_out")
    st_xo, tok = _spread_start(_row_blocks(d_w_out), True, "exchange_w_out_start", d_ln_ffn_pre)
    dmix = _matmul(dm, w_out_f, "nt", F32, "bwd_proj_out", after=tok)
    dq_sb, dk_sb, dv_sb, d_sb_gain = _sb_attention_bwd(proj, sb_out_gain, o_sb, sb_att, sb_lb, dmix)
    dqn, dkn, dvn, dgc_full, dbeta_full, dz, d_dn_gain = _dn_scan_bwd(
        qn, kn, vn, gc_full, beta_full, proj, z0, dn_out_gain, o_dn, tm_all, s_all, dmix, SBW)
    du_q, dcw_q = _dn_branch_bwd(proj, qkv0, dn_cw, 0, True, HEAD_DIM ** -0.5, dqn)
    du_k, dcw_k = _dn_branch_bwd(proj, qkv0 + DNW, dn_cw, DNW, True, 1.0, dkn)
    du_v, dcw_v = _dn_branch_bwd(proj, qkv0 + 2 * DNW, dn_cw, 2 * DNW, False, 1.0, dvn)
    dba, d_a_log_l, d_dt_bias_l = _dn_gates_bwd(proj, gate_block, a_log_l, dt_bias_l, dgc_full, dbeta_full)
    dproj = jnp.concatenate([dq_sb, dk_sb.astype(BF16), dv_sb.astype(BF16), du_q, du_k, du_v, dz, dba], axis=1)
    d_w_in_t = _matmul(dproj, xn, "tn", BF16, "grad_w_in", tm_cap=2432, tn_cap=512, tk_cap=1024)
    d_w_in_cut = d_w_in_t[:in_cols].reshape(N_DEV, in_cols // N_DEV, D)
    st_xi, tok = _spread_start(d_w_in_cut, True, "exchange_w_in_start", d_sb_gain)
    dxn = _matmul(dproj, w_in_t, "nn", F32, "bwd_proj_in", tk_cap=2432, after=tok)
    grad_x, d_ln_mix_pre = _input_bwd(dh, dxn, x2, ln_mix_pre)

    big = {}
    after = grad_x
    for n, st, w_, m_, v_ in [("w_down", st_xd, w_down, m_w_down, v_w_down), ("w_up", st_xu, w_up, m_w_up, v_w_up),
                              ("w_out", st_xo, w_out, m_w_out, v_w_out)]:
        got = _spread_wait(st, True, "exchange_" + n + "_wait", after)
        big[n] = _adamw_sharded(got, w_, m_, v_, "adamw_" + n)
        after = big[n][1]

    d_dn_cw = jnp.concatenate([dcw_q[:SHORT_CONV], dcw_k[:SHORT_CONV], dcw_v[:SHORT_CONV]], axis=1)
    small = [loss_part[:, :1], d_sb_gain, d_a_log_l[:, DN_HEADS:2 * DN_HEADS], d_dt_bias_l[:, DN_HEADS:2 * DN_HEADS], d_dn_gain,
             d_ln_mix_pre, d_ln_mix_post, d_ffn_cwb[FFN_CONV:FFN_CONV + 1], d_ln_ffn_pre, d_ln_ffn_post,
             d_dn_cw, d_ffn_cwb[:FFN_CONV]]
    shapes = [a.shape for a in small]
    red = _unpack(_all_reduce_packed(_pack(small), after), shapes)
    loss = red[0].reshape(())
    me = _index(_my_place())
    g_dn_cw = lax.dynamic_slice_in_dim(red[10], me * dn_conv_w.shape[2], dn_conv_w.shape[2], axis=1)
    g_ffn_cw = lax.dynamic_slice_in_dim(red[11], me * ffn_conv_w.shape[2], ffn_conv_w.shape[2], axis=1)
    names = ["sb_out_gain", "dn_conv_w", "dn_a_log", "dn_dt_bias", "dn_out_gain", "ln_mix_pre", "ln_mix_post",
             "ffn_conv_w", "ffn_conv_b", "ln_ffn_pre", "ln_ffn_post"]
    g_small = dict(sb_out_gain=red[1], dn_conv_w=g_dn_cw[None], dn_a_log=red[2], dn_dt_bias=red[3], dn_out_gain=red[4],
                   ln_mix_pre=red[5], ln_mix_post=red[6], ffn_conv_w=g_ffn_cw[None], ffn_conv_b=red[7],
                   ln_ffn_pre=red[8], ln_ffn_post=red[9])
    w_small = dict(sb_out_gain=sb_out_gain, dn_conv_w=dn_conv_w, dn_a_log=dn_a_log, dn_dt_bias=dn_dt_bias,
                   dn_out_gain=dn_out_gain, ln_mix_pre=ln_mix_pre, ln_mix_post=ln_mix_post, ffn_conv_w=ffn_conv_w,
                   ffn_conv_b=ffn_conv_b, ln_ffn_pre=ln_ffn_pre, ln_ffn_post=ln_ffn_post)
    m_small = dict(sb_out_gain=m_sb_out_gain, dn_conv_w=m_dn_conv_w, dn_a_log=m_dn_a_log, dn_dt_bias=m_dn_dt_bias,
                   dn_out_gain=m_dn_out_gain, ln_mix_pre=m_ln_mix_pre, ln_mix_post=m_ln_mix_post, ffn_conv_w=m_ffn_conv_w,
                   ffn_conv_b=m_ffn_conv_b, ln_ffn_pre=m_ln_ffn_pre, ln_ffn_post=m_ln_ffn_post)
    v_small = dict(sb_out_gain=v_sb_out_gain, dn_conv_w=v_dn_conv_w, dn_a_log=v_dn_a_log, dn_dt_bias=v_dn_dt_bias,
                   dn_out_gain=v_dn_out_gain, ln_mix_pre=v_ln_mix_pre, ln_mix_post=v_ln_mix_post, ffn_conv_w=v_ffn_conv_w,
                   ffn_conv_b=v_ffn_conv_b, ln_ffn_pre=v_ln_ffn_pre, ln_ffn_post=v_ln_ffn_post)
    sshapes = [w_small[n].shape for n in names]
    upd = _adamw_packed(_pack([g_small[n] for n in names]), _pack([w_small[n] for n in names]),
                        _pack([m_small[n] for n in names]), _pack([v_small[n] for n in names]))
    d_small, nm_small, nv_small = [dict(zip(names, _unpack(p, sshapes))) for p in upd]

    got = _spread_wait(st_xi, True, "exchange_w_in_wait", d_small["ln_ffn_post"])
    flip = lambda a: jnp.swapaxes(a, 1, 2)
    big["w_in"] = [flip(a) for a in _adamw_sharded(got, flip(w_in), flip(m_w_in), flip(v_w_in), "adamw_w_in")]

    order = ["w_in", "sb_out_gain", "dn_conv_w", "dn_a_log", "dn_dt_bias", "dn_out_gain", "w_out", "ln_mix_pre",
             "ln_mix_post", "w_up", "ffn_conv_w", "ffn_conv_b", "w_down", "ln_ffn_pre", "ln_ffn_post"]
    pick = lambda n, i: big[n][i] if n in big else [g_small, d_small, nm_small, nv_small][i][n].reshape(w_small[n].shape)
    return (loss, grad_x[None], *[pick(n, 0) for n in order], *[pick(n, 1) for n in order],
            *[pick(n, 2) for n in order], *[pick(n, 3) for n in order])
```

```python
import functools
import math

import jax
import jax.numpy as jnp
from jax import lax
from jax.experimental import pallas as pl
from jax.experimental.pallas import tpu as pltpu

F32 = jnp.float32
BF16 = jnp.bfloat16

N_DEV = 8
HEAD_DIM = 128
SB_HEADS = 8
DN_HEADS = 8
DN_CHUNK = 128
DN_GROUP = 8
INV_BLOCK = 16
SB_KEYS = 256
SHORT_CONV = 4
FFN_CONV = 3
EPS = 1e-6
LANES = 128
SUBLANES = 8
VMEM_CAP = 56 * 2**20

ADAM_LR = 0.001
ADAM_B1 = 0.9
ADAM_B2 = 0.999
ADAM_EPS = 1e-08
ADAM_WD = 0.01
ADAM_STEP = 10

MESH = pl.DeviceIdType.MESH

assert HEAD_DIM == DN_CHUNK == LANES


def _tile(n, cap, mult):
    if n <= cap:
        return n
    t = (cap // mult) * mult
    while t >= mult:
        if n % t == 0:
            return t
        t -= mult
    raise ValueError(f"no tile for {n} under {cap} in multiples of {mult}")


def _params(sem, vmem_bytes):
    limit = int(min(VMEM_CAP, max(vmem_bytes, 16 * 2**20)))
    if not sem:
        return pltpu.CompilerParams(vmem_limit_bytes=limit)
    return pltpu.CompilerParams(dimension_semantics=sem, vmem_limit_bytes=limit)


def _nbytes(shape, dtype):
    return math.prod(shape) * jnp.dtype(dtype).itemsize


def _hbm(*arrays):
    return [pltpu.with_memory_space_constraint(a, pltpu.HBM) for a in arrays]


_NN = (((1,), (0,)), ((), ()))
_NT = (((1,), (1,)), ((), ()))
_TN = (((0,), (0,)), ((), ()))


def _batched(dims, ndim):
    if ndim == 2:
        return dims
    (ca,), (cb,) = dims[0]
    return (((ca + 1,), (cb + 1,)), ((0,), (0,)))


def _dot(a, b, dims=_NN):
    return lax.dot_general(a.astype(BF16), b.astype(BF16), _batched(dims, a.ndim), preferred_element_type=F32)


def _split2(x):
    hi = x.astype(BF16)
    lo = (x - hi.astype(F32)).astype(BF16)
    return hi, lo


def _split3(x):
    hi = x.astype(BF16)
    r = x - hi.astype(F32)
    mid = r.astype(BF16)
    lo = (r - mid.astype(F32)).astype(BF16)
    return hi, mid, lo


def _dot01(x, m01, passes=3):
    parts = _split3(x) if passes == 3 else _split2(x)
    out = None
    for p in parts:
        t = lax.dot_general(p, m01, _NN, preferred_element_type=F32)
        out = t if out is None else out + t
    return out


def _dot01_left(m01, x, passes=3):
    parts = _split3(x) if passes == 3 else _split2(x)
    out = None
    for p in parts:
        t = lax.dot_general(m01, p, _NN, preferred_element_type=F32)
        out = t if out is None else out + t
    return out


def _mm3(a, b, dims=_NN):
    ah, al = _split2(a)
    bh, bl = _split2(b)
    d = functools.partial(lax.dot_general, dimension_numbers=_batched(dims, a.ndim), preferred_element_type=F32)
    return d(ah, bh) + (d(ah, bl) + d(al, bh))


def _rowsum(x):
    return jnp.sum(x, axis=-1, keepdims=True)


def _t(x):
    return jnp.swapaxes(x, -1, -2)


def _sigmoid(x):
    return 1.0 / (1.0 + jnp.exp(-x))


def _softplus(x):
    return jnp.maximum(x, 0.0) + jnp.log(1.0 + jnp.exp(-jnp.abs(x)))


def _silu(x):
    return x * _sigmoid(x)


def _silu_grad(x):
    s = _sigmoid(x)
    return s * (1.0 + x * (1.0 - s))


_GELU_C = math.sqrt(2.0 / math.pi)


def _gelu(x):
    return 0.5 * x * (1.0 + jnp.tanh(_GELU_C * (x + 0.044715 * x * x * x)))


def _gelu_and_grad(x):
    x2 = x * x
    th = jnp.tanh(_GELU_C * (x + 0.044715 * x2 * x))
    half = 0.5 * (1.0 + th)
    return x * half, half + 0.5 * x * (1.0 - th * th) * (_GELU_C * (1.0 + 3.0 * 0.044715 * x2))


def _rms(x, g):
    r = lax.rsqrt(jnp.mean(x * x, axis=-1, keepdims=True) + EPS)
    return x * r * g


def _rms_bwd(dy, x, g):
    r = lax.rsqrt(jnp.mean(x * x, axis=-1, keepdims=True) + EPS)
    xh = x * r
    gdy = dy * g
    dx = r * (gdy - xh * jnp.mean(gdy * xh, axis=-1, keepdims=True))
    return dx, jnp.sum(dy * xh, axis=-2, keepdims=True)


def _iota2(shape, axis):
    return lax.broadcasted_iota(jnp.int32, shape, axis)


def _shift_down(cur, prev8, k):
    n = cur.shape[0]
    r = pltpu.roll(cur, k, 0)
    pr = pltpu.roll(prev8, k, 0)
    head = jnp.where(_iota2(pr.shape, 0) < k, pr, r[0:SUBLANES])
    if n == SUBLANES:
        return head
    return jnp.concatenate([head, r[SUBLANES:]], axis=0)


def _shift_up(cur, next8, k):
    n = cur.shape[0]
    r = pltpu.roll(cur, n - k, 0)
    nr = pltpu.roll(next8, SUBLANES - k, 0)
    tail = jnp.where(_iota2(nr.shape, 0) >= SUBLANES - k, nr, r[n - SUBLANES:])
    if n == SUBLANES:
        return tail
    return jnp.concatenate([r[:n - SUBLANES], tail], axis=0)


def _causal_conv(cur, prev8, w_ref, taps):
    out = cur * w_ref[taps - 1:taps, :]
    for j in range(taps - 1):
        out = out + _shift_down(cur, prev8, taps - 1 - j) * w_ref[j:j + 1, :]
    return out


def _anti_conv(cur, next8, w_ref, taps):
    out = cur * w_ref[taps - 1:taps, :]
    for j in range(taps - 1):
        out = out + _shift_up(cur, next8, taps - 1 - j) * w_ref[j:j + 1, :]
    return out


def _matmul(a, b, mode, out_dtype, name, tm_cap=1024, tn_cap=1024, tk_cap=2048, after=None,
            a_cut=False, b_cut=False, out_cut=False):
    a_shard = a.shape[2] if a_cut else None
    b_shard = b.shape[2] if b_cut else None
    a_full = (a.shape[1], a.shape[0] * a_shard) if a_cut else a.shape
    b_full = (b.shape[1], b.shape[0] * b_shard) if b_cut else b.shape
    assert not (a_cut and mode == "tn")
    if mode == "nn":
        (M, K), N = a_full, b_full[1]
    elif mode == "nt":
        (M, K), N = a_full, b_full[0]
    else:
        (K, M), N = a_full, b_full[1]
    n_unit = b_shard if (b_cut and mode != "nt") else N
    k_unit = math.gcd(a_shard or K, b_shard if (b_cut and mode == "nt") else K)
    tm = _tile(M, tm_cap, LANES)
    tn = N // N_DEV if out_cut else _tile(n_unit, tn_cap, LANES)
    tk = _tile(k_unit, tk_cap, LANES)
    assert n_unit % tn == 0 and k_unit % tk == 0
    nk = K // tk
    dims = {"nn": _NN, "nt": _NT, "tn": _TN}[mode]
    if a_cut:
        pa = a_shard // tk
        a_spec = pl.BlockSpec((None, tm, tk), lambda i, j, k: (k // pa, i, k % pa))
    elif mode == "tn":
        a_spec = pl.BlockSpec((tk, tm), lambda i, j, k: (k, i))
    else:
        a_spec = pl.BlockSpec((tm, tk), lambda i, j, k: (i, k))
    if b_cut and mode == "nt":
        pb = b_shard // tk
        b_spec = pl.BlockSpec((None, tn, tk), lambda i, j, k: (k // pb, j, k % pb))
    elif b_cut:
        pb = b_shard // tn
        b_spec = pl.BlockSpec((None, tk, tn), lambda i, j, k: (j // pb, k, j % pb))
    elif mode == "nt":
        b_spec = pl.BlockSpec((tn, tk), lambda i, j, k: (j, k))
    else:
        b_spec = pl.BlockSpec((tk, tn), lambda i, j, k: (k, j))
    if out_cut:
        out_spec, out_shape = pl.BlockSpec((None, tm, tn), lambda i, j, k: (j, i, 0)), (N_DEV, M, tn)
    else:
        out_spec, out_shape = pl.BlockSpec((tm, tn), lambda i, j, k: (i, j)), (M, N)

    def body(a_ref, b_ref, *rest):
        if nk == 1:
            rest[-1][...] = lax.dot_general(a_ref[...], b_ref[...], dims, preferred_element_type=F32).astype(rest[-1].dtype)
            return
        o_ref, acc_ref = rest[-2:]
        k = pl.program_id(2)

        @pl.when(k == 0)
        def _():
            acc_ref[...] = jnp.zeros_like(acc_ref)

        acc_ref[...] += lax.dot_general(a_ref[...], b_ref[...], dims, preferred_element_type=F32)

        @pl.when(k == nk - 1)
        def _():
            o_ref[...] = acc_ref[...].astype(o_ref.dtype)

    vmem = 2 * (_nbytes((tm, tk), a.dtype) + _nbytes((tk, tn), b.dtype) + _nbytes((tm, tn), out_dtype)) + _nbytes((tm, tn), F32)
    vmem += _nbytes((tm, tn), F32) + (2 * _nbytes((tm, tk), a.dtype) if mode == "tn" else 0)
    tokens = [] if after is None else [after]
    return pl.pallas_call(
        body, name=name, grid=(M // tm, N // tn, nk),
        in_specs=[a_spec, b_spec] + [pl.BlockSpec(t.shape, lambda i, j, k: (0, 0)) for t in tokens],
        out_specs=out_spec,
        out_shape=jax.ShapeDtypeStruct(out_shape, out_dtype),
        scratch_shapes=[] if nk == 1 else [pltpu.VMEM((tm, tn), F32)],
        compiler_params=_params(("parallel", "parallel", "arbitrary"), vmem + 4 * 2**20),
    )(*_hbm(a, b), *tokens)


def _row_call(body, name, T, D, ins, outs, tr, acc_outs=()):
    def spec(a, kind):
        if kind == "row":
            return pl.BlockSpec((tr, a.shape[1]), lambda i: (i, 0))
        return pl.BlockSpec(a.shape, lambda i: (0, 0))
    in_specs = [spec(a, k) for a, k in ins]
    out_specs = [spec(a, k) for a, k in outs] + [spec(a, "vec") for a in acc_outs]
    out_shape = [a for a, _ in outs] + list(acc_outs)
    vmem = 2 * sum(_nbytes((tr, a.shape[1]) if k == "row" else a.shape, a.dtype) for a, k in list(ins) + list(outs))
    return pl.pallas_call(
        body, name=name, grid=(T // tr,), in_specs=in_specs, out_specs=out_specs, out_shape=out_shape,
        compiler_params=_params(("arbitrary",), 3 * vmem + 8 * 2**20),
    )(*_hbm(*[a for a, _ in ins]))


def _sds(shape, dtype):
    return jax.ShapeDtypeStruct(shape, dtype)


def _accumulate(ref, val):
    @pl.when(pl.program_id(0) == 0)
    def _():
        ref[...] = jnp.zeros_like(ref)
    ref[...] += val


def _norm_in(x, g):
    T, D = x.shape

    def body(x_ref, g_ref, o_ref):
        o_ref[...] = _rms(x_ref[...], g_ref[...]).astype(BF16)

    return _row_call(body, "norm_in", T, D, [(x, "row"), (g, "vec")], [(_sds((T, D), BF16), "row")], _tile(T, 256, 16))[0]


def _mix_residual(x, m, g_post, g_pre):
    T, D = x.shape

    def body(x_ref, m_ref, gp_ref, gn_ref, h_ref, hn_ref):
        h = x_ref[...] + _rms(m_ref[...], gp_ref[...])
        h_ref[...] = h
        hn_ref[...] = _rms(h, gn_ref[...]).astype(BF16)

    return _row_call(body, "mix_residual", T, D, [(x, "row"), (m, "row"), (g_post, "vec"), (g_pre, "vec")],
                     [(_sds((T, D), F32), "row"), (_sds((T, D), BF16), "row")], _tile(T, 256, 16))


def _loss_head(h, f, g_post, target):
    T, D = h.shape

    def body(h_ref, f_ref, g_ref, t_ref, dy_ref, df_ref, dg_ref, loss_ref):
        f = f_ref[...]
        g = g_ref[...]
        diff = h_ref[...] + _rms(f, g) - t_ref[...]
        dy = diff * (1.0 / D)
        dy_ref[...] = dy
        df, dg = _rms_bwd(dy, f, g)
        df_ref[...] = df.astype(BF16)
        _accumulate(dg_ref, dg)
        _accumulate(loss_ref, jnp.full((1, LANES), 0.5 / D, F32) * jnp.sum(diff * diff))

    return _row_call(body, "loss_head", T, D, [(h, "row"), (f, "row"), (g_post, "vec"), (target, "row")],
                     [(_sds((T, D), F32), "row"), (_sds((T, D), BF16), "row")], _tile(T, 256, 16),
                     acc_outs=[_sds((1, D), F32), _sds((1, LANES), F32)])


def _ffn_residual_bwd(dy, dhn, h, g_pre, m, g_post):
    T, D = h.shape

    def body(dy_ref, dhn_ref, h_ref, gn_ref, m_ref, gp_ref, dh_ref, dm_ref, dgn_ref, dgp_ref):
        dhh, dgn = _rms_bwd(dhn_ref[...], h_ref[...], gn_ref[...])
        dh = dy_ref[...] + dhh
        dh_ref[...] = dh
        dm, dgp = _rms_bwd(dh, m_ref[...], gp_ref[...])
        dm_ref[...] = dm.astype(BF16)
        _accumulate(dgn_ref, dgn)
        _accumulate(dgp_ref, dgp)

    return _row_call(body, "ffn_residual_bwd", T, D,
                     [(dy, "row"), (dhn, "row"), (h, "row"), (g_pre, "vec"), (m, "row"), (g_post, "vec")],
                     [(_sds((T, D), F32), "row"), (_sds((T, D), BF16), "row")], _tile(T, 128, 16),
                     acc_outs=[_sds((1, D), F32), _sds((1, D), F32)])


def _input_bwd(dh, dxn, x, g):
    T, D = x.shape

    def body(dh_ref, dxn_ref, x_ref, g_ref, dx_ref, dg_ref):
        dx, dg = _rms_bwd(dxn_ref[...], x_ref[...], g_ref[...])
        dx_ref[...] = dh_ref[...] + dx
        _accumulate(dg_ref, dg)

    return _row_call(body, "input_bwd", T, D, [(dh, "row"), (dxn, "row"), (x, "row"), (g, "vec")],
                     [(_sds((T, D), F32), "row")], _tile(T, 256, 16), acc_outs=[_sds((1, D), F32)])


def _ffn_act(u, conv_w, conv_b):
    T, F2 = u.shape
    F = F2 // 2
    tc = _tile(F, 512, LANES)
    tr = _tile(T, 1024, SUBLANES)
    nc = F // tc
    r8 = tr // SUBLANES

    def body(ug_ref, ugp_ref, uv_ref, uvp_ref, wg_ref, wv_ref, bg_ref, bv_ref, a_ref):
        first = pl.program_id(1) == 0
        cg = _causal_conv(ug_ref[...], jnp.where(first, 0.0, ugp_ref[...]), wg_ref, FFN_CONV) + bg_ref[...]
        cv = _causal_conv(uv_ref[...], jnp.where(first, 0.0, uvp_ref[...]), wv_ref, FFN_CONV) + bv_ref[...]
        a_ref[...] = (_gelu(cg) * cv).astype(BF16)

    cur = lambda off: pl.BlockSpec((tr, tc), lambda j, i: (i, j + off))
    prev = lambda off: pl.BlockSpec((SUBLANES, tc), lambda j, i: (jnp.maximum(i * r8 - 1, 0), j + off))
    wsp = lambda off: pl.BlockSpec((FFN_CONV, tc), lambda j, i: (0, j + off))
    bsp = lambda off: pl.BlockSpec((1, tc), lambda j, i: (0, j + off))
    return pl.pallas_call(
        body, name="ffn_act", grid=(nc, T // tr),
        in_specs=[cur(0), prev(0), cur(nc), prev(nc), wsp(0), wsp(nc), bsp(0), bsp(nc)],
        out_specs=pl.BlockSpec((tr, tc), lambda j, i: (i, j)),
        out_shape=_sds((T, F), BF16),
        compiler_params=_params(("parallel", "arbitrary"), 12 * _nbytes((tr, tc), F32) + 8 * 2**20),
    )(*_hbm(u, u, u, u, conv_w, conv_w, conv_b, conv_b))


def _ffn_act_bwd(u, conv_w, conv_b, da):
    T, F2 = u.shape
    F = F2 // 2
    tc = _tile(F, 512, LANES)
    tr = _tile(T, 512, SUBLANES)
    nc = F // tc
    r8 = tr // SUBLANES
    n8 = T // SUBLANES
    K = FFN_CONV

    def body(ug_ref, ugp_ref, ugn_ref, uv_ref, uvp_ref, uvn_ref, da_ref, dan_ref,
             wg_ref, wv_ref, bg_ref, bv_ref, du_ref, dwb_ref):
        i = pl.program_id(1)
        first = i == 0
        last = i == pl.num_programs(1) - 1

        def dconv(ug, ug_prev, uv, uv_prev, da_):
            cg = _causal_conv(ug, ug_prev, wg_ref, K) + bg_ref[...]
            cv = _causal_conv(uv, uv_prev, wv_ref, K) + bv_ref[...]
            act, act_grad = _gelu_and_grad(cg)
            return da_ * cv * act_grad, da_ * act

        ug, uv = ug_ref[...], uv_ref[...]
        ug_prev, uv_prev = jnp.where(first, 0.0, ugp_ref[...]), jnp.where(first, 0.0, uvp_ref[...])
        dcg, dcv = dconv(ug, ug_prev, uv, uv_prev, da_ref[...])
        dcgn, dcvn = dconv(ugn_ref[...], ug[tr - SUBLANES:], uvn_ref[...], uv[tr - SUBLANES:], dan_ref[...])
        du_ref[0] = _anti_conv(dcg, jnp.where(last, 0.0, dcgn), wg_ref, K).astype(BF16)
        du_ref[1] = _anti_conv(dcv, jnp.where(last, 0.0, dcvn), wv_ref, K).astype(BF16)

        @pl.when(first)
        def _():
            dwb_ref[...] = jnp.zeros_like(dwb_ref)

        for half, (dc, uo, uo_prev) in enumerate([(dcg, ug, ug_prev), (dcv, uv, uv_prev)]):
            rows = [jnp.sum(dc * _shift_down(uo, uo_prev, K - 1 - t), axis=0, keepdims=True) for t in range(K - 1)]
            rows += [jnp.sum(dc * uo, axis=0, keepdims=True), jnp.sum(dc, axis=0, keepdims=True)]
            rows += [jnp.zeros_like(rows[0])] * (SUBLANES - len(rows))
            dwb_ref[half] += jnp.concatenate(rows, axis=0)

    cur = lambda off: pl.BlockSpec((tr, tc), lambda j, i: (i, j + off))
    prev = lambda off: pl.BlockSpec((SUBLANES, tc), lambda j, i: (jnp.maximum(i * r8 - 1, 0), j + off))
    nxt = lambda off: pl.BlockSpec((SUBLANES, tc), lambda j, i: (jnp.minimum((i + 1) * r8, n8 - 1), j + off))
    wsp = lambda off: pl.BlockSpec((K, tc), lambda j, i: (0, j + off))
    bsp = lambda off: pl.BlockSpec((1, tc), lambda j, i: (0, j + off))
    return pl.pallas_call(
        body, name="ffn_act_bwd", grid=(nc, T // tr),
        in_specs=[cur(0), prev(0), nxt(0), cur(nc), prev(nc), nxt(nc), cur(0), nxt(0), wsp(0), wsp(nc), bsp(0), bsp(nc)],
        out_specs=[pl.BlockSpec((2, tr, tc), lambda j, i: (0, i, j)), pl.BlockSpec((2, SUBLANES, tc), lambda j, i: (0, 0, j))],
        out_shape=[_sds((2, T, F), BF16), _sds((2, SUBLANES, F), F32)],
        compiler_params=_params(("parallel", "arbitrary"), 24 * _nbytes((tr, tc), F32) + 8 * 2**20),
    )(*_hbm(u, u, u, u, u, u, da, da, conv_w, conv_w, conv_b, conv_b))


def _l2norm(s, scale):
    return s * (lax.rsqrt(jnp.sum(s * s, axis=-1, keepdims=True) + EPS) * scale)


def _dn_branch(proj, col0, conv_w, wcol0, l2, scale):
    T = proj.shape[0]
    W = DN_HEADS * HEAD_DIM
    tr = _tile(T, 2048, SUBLANES)
    r8 = tr // SUBLANES
    cb0, wb0 = col0 // HEAD_DIM, wcol0 // HEAD_DIM

    def body(u_ref, up_ref, w_ref, o_ref):
        first = pl.program_id(1) == 0
        s = _silu(_causal_conv(u_ref[...], jnp.where(first, 0.0, up_ref[...]), w_ref, SHORT_CONV))
        o_ref[...] = _l2norm(s, scale) if l2 else s

    return pl.pallas_call(
        body, name=f"dn_branch_{col0}", grid=(DN_HEADS, T // tr),
        in_specs=[pl.BlockSpec((tr, HEAD_DIM), lambda h, i: (i, cb0 + h)),
                  pl.BlockSpec((SUBLANES, HEAD_DIM), lambda h, i: (jnp.maximum(i * r8 - 1, 0), cb0 + h)),
                  pl.BlockSpec((SHORT_CONV, HEAD_DIM), lambda h, i: (0, wb0 + h))],
        out_specs=pl.BlockSpec((tr, HEAD_DIM), lambda h, i: (i, h)),
        out_shape=_sds((T, W), F32),
        compiler_params=_params(("parallel", "arbitrary"), 32 * _nbytes((tr, HEAD_DIM), F32) + 8 * 2**20),
    )(*_hbm(proj, proj, conv_w))


def _dn_branch_bwd(proj, col0, conv_w, wcol0, l2, scale, dy):
    T = proj.shape[0]
    W = DN_HEADS * HEAD_DIM
    tr = _tile(T, 2048, SUBLANES)
    r8 = tr // SUBLANES
    n8 = T // SUBLANES
    cb0, wb0 = col0 // HEAD_DIM, wcol0 // HEAD_DIM
    K = SHORT_CONV

    def body(u_ref, up_ref, un_ref, dy_ref, dyn_ref, w_ref, du_ref, dw_ref):
        i = pl.program_id(1)
        first = i == 0
        last = i == pl.num_programs(1) - 1

        def dconv(u, u_prev, dy_):
            c = _causal_conv(u, u_prev, w_ref, K)
            if l2:
                s = _silu(c)
                r = lax.rsqrt(jnp.sum(s * s, axis=-1, keepdims=True) + EPS)
                n = s * r
                ds = (scale * r) * (dy_ - n * jnp.sum(dy_ * n, axis=-1, keepdims=True))
            else:
                ds = dy_
            return ds * _silu_grad(c)

        u = u_ref[...]
        u_prev = jnp.where(first, 0.0, up_ref[...])
        dc = dconv(u, u_prev, dy_ref[...])
        dcn = jnp.where(last, 0.0, dconv(un_ref[...], u[tr - SUBLANES:], dyn_ref[...]))
        du_ref[...] = _anti_conv(dc, dcn, w_ref, K).astype(BF16)
        rows = [jnp.sum(dc * _shift_down(u, u_prev, K - 1 - t), axis=0, keepdims=True) for t in range(K - 1)]
        rows += [jnp.sum(dc * u, axis=0, keepdims=True)]
        rows += [jnp.zeros_like(rows[0])] * (SUBLANES - len(rows))
        upd = jnp.concatenate(rows, axis=0)

        @pl.when(first)
        def _():
            dw_ref[...] = jnp.zeros_like(dw_ref)
        dw_ref[...] += upd

    return pl.pallas_call(
        body, name=f"dn_branch_bwd_{col0}", grid=(DN_HEADS, T // tr),
        in_specs=[pl.BlockSpec((tr, HEAD_DIM), lambda h, i: (i, cb0 + h)),
                  pl.BlockSpec((SUBLANES, HEAD_DIM), lambda h, i: (jnp.maximum(i * r8 - 1, 0), cb0 + h)),
                  pl.BlockSpec((SUBLANES, HEAD_DIM), lambda h, i: (jnp.minimum((i + 1) * r8, n8 - 1), cb0 + h)),
                  pl.BlockSpec((tr, HEAD_DIM), lambda h, i: (i, h)),
                  pl.BlockSpec((SUBLANES, HEAD_DIM), lambda h, i: (jnp.minimum((i + 1) * r8, n8 - 1), h)),
                  pl.BlockSpec((K, HEAD_DIM), lambda h, i: (0, wb0 + h))],
        out_specs=[pl.BlockSpec((tr, HEAD_DIM), lambda h, i: (i, h)),
                   pl.BlockSpec((SUBLANES, HEAD_DIM), lambda h, i: (0, h))],
        out_shape=[_sds((T, W), BF16), _sds((SUBLANES, W), F32)],
        compiler_params=_params(("parallel", "arbitrary"), 32 * _nbytes((tr, HEAD_DIM), F32) + 8 * 2**20),
    )(*_hbm(proj, proj, proj, dy, dy, conv_w))


def _lane_masks(shape):
    lane = _iota2(shape, 1)
    return lane < DN_HEADS, (lane >= DN_HEADS) & (lane < 2 * DN_HEADS)


def _expand01(off):
    r = _iota2((LANES, DN_HEADS * HEAD_DIM), 0)
    c = _iota2((LANES, DN_HEADS * HEAD_DIM), 1)
    return (r == jnp.right_shift(c, int(math.log2(HEAD_DIM))) + off).astype(BF16)


def _select01(off):
    r = _iota2((DN_HEADS * HEAD_DIM, LANES), 0)
    c = _iota2((DN_HEADS * HEAD_DIM, LANES), 1)
    return (r == (c - off) * HEAD_DIM).astype(BF16)


def _dn_gates(proj, gate_block, a_log_l, dt_bias_l):
    T = proj.shape[0]
    C = DN_CHUNK
    W = DN_HEADS * HEAD_DIM

    def body(ba_ref, al_ref, dt_ref, gc_ref, beta_ref):
        ba = ba_ref[...]
        is_b, is_a = _lane_masks(ba.shape)
        g = jnp.where(is_a, -jnp.exp(al_ref[...]) * _softplus(ba + dt_ref[...]), 0.0)
        beta = jnp.where(is_b, _sigmoid(ba), 0.0)
        tri = (_iota2((C, C), 0) >= _iota2((C, C), 1)).astype(BF16)
        gc = _dot01_left(tri, g)
        gc_ref[...] = _dot01(gc, _expand01(DN_HEADS))
        beta_ref[...] = _dot01(beta, _expand01(0))

    vec = pl.BlockSpec((1, LANES), lambda n: (0, 0))
    return pl.pallas_call(
        body, name="dn_gates", grid=(T // C,),
        in_specs=[pl.BlockSpec((C, LANES), lambda n: (n, gate_block)), vec, vec],
        out_specs=[pl.BlockSpec((C, W), lambda n: (n, 0))] * 2,
        out_shape=[_sds((T, W), F32)] * 2,
        compiler_params=_params(("parallel",), 16 * 2**20),
    )(*_hbm(proj, a_log_l, dt_bias_l))


def _dn_gates_bwd(proj, gate_block, a_log_l, dt_bias_l, dgc_full, dbeta_full):
    T = proj.shape[0]
    C = DN_CHUNK
    W = DN_HEADS * HEAD_DIM

    def body(ba_ref, al_ref, dt_ref, dgc_ref, dbeta_ref, dba_ref, dal_ref, ddt_ref):
        ba = ba_ref[...]
        is_b, is_a = _lane_masks(ba.shape)
        ea = jnp.exp(al_ref[...])
        pre = ba + dt_ref[...]
        g = jnp.where(is_a, -ea * _softplus(pre), 0.0)
        beta = _sigmoid(ba)
        dgc = _dot01(dgc_ref[...], _select01(DN_HEADS))
        dbeta = _dot01(dbeta_ref[...], _select01(0))
        triu = (_iota2((C, C), 0) <= _iota2((C, C), 1)).astype(BF16)
        dg = _dot01_left(triu, dgc)
        da = jnp.where(is_a, dg * (-ea) * _sigmoid(pre), 0.0)
        dba_ref[...] = (da + jnp.where(is_b, dbeta * beta * (1.0 - beta), 0.0)).astype(BF16)
        _accumulate(dal_ref, jnp.sum(dg * g, axis=0, keepdims=True))
        _accumulate(ddt_ref, jnp.sum(da, axis=0, keepdims=True))

    vec = pl.BlockSpec((1, LANES), lambda n: (0, 0))
    full = pl.BlockSpec((C, W), lambda n: (n, 0))
    return pl.pallas_call(
        body, name="dn_gates_bwd", grid=(T // C,),
        in_specs=[pl.BlockSpec((C, LANES), lambda n: (n, gate_block)), vec, vec, full, full],
        out_specs=[pl.BlockSpec((C, LANES), lambda n: (n, 0)), vec, vec],
        out_shape=[_sds((T, LANES), BF16), _sds((1, LANES), F32), _sds((1, LANES), F32)],
        compiler_params=_params(("arbitrary",), 16 * 2**20),
    )(*_hbm(proj, a_log_l, dt_bias_l, dgc_full, dbeta_full))


def _unit_lower_inverse(L):
    C = L.shape[-1]
    row, col = _iota2((C, C), 0), _iota2((C, C), 1)
    eye = (row == col).astype(F32)
    sh = int(math.log2(INV_BLOCK))
    Ld = jnp.where(jnp.right_shift(row, sh) == jnp.right_shift(col, sh), L, 0.0)
    Lo = L - Ld
    X = eye - Ld
    P = Ld
    for _ in range(int(math.log2(INV_BLOCK)) - 1):
        P = _mm3(P, P)
        X = X + _mm3(X, P)
    N = _mm3(X, Lo)
    Y = eye - N
    P = N
    for _ in range(int(math.log2(C // INV_BLOCK)) - 1):
        P = _mm3(P, P)
        Y = Y + _mm3(Y, P)
    return _mm3(Y, X)


def _dn_chunk_common(q, k, v, gc, beta, gl):
    C = q.shape[-2]
    row, col = _iota2((C, C), 0), _iota2((C, C), 1)
    causal, strict = row >= col, row > col
    eg = jnp.exp(gc)
    decay = jnp.where(causal, jnp.exp(jnp.where(causal, gc - _t(gc), 0.0)), 0.0)
    kb, vb = k * beta, v * beta
    L = jnp.where(strict, _dot(kb, k, _NT) * decay, 0.0)
    Aqk = jnp.where(causal, _dot(q, k, _NT) * decay, 0.0)
    ektg = jnp.exp(gl - gc)
    return dict(causal=causal, strict=strict, eg=eg, decay=decay, kb=kb, vb=vb, L=L, Aqk=Aqk, ektg=ektg,
                kbg=kb * eg, kte=k * ektg, qd=q * eg, egl=jnp.exp(gl))


def _dn_scan(qn, kn, vn, gc_full, beta_full, proj, z_col0, gain):
    T, W = qn.shape
    C = DN_CHUNK
    N = T // C
    H = DN_HEADS
    G = DN_GROUP
    GW = G * HEAD_DIM
    zb0 = z_col0 // GW

    def body(q_ref, k_ref, v_ref, gc_ref, beta_ref, z_ref, gain_ref, o_ref, mix_ref, tm_ref, s_ref, S):
        @pl.when(pl.program_id(1) == 0)
        def _():
            S[...] = jnp.zeros_like(S)

        heads = lambda ref, rows=slice(None): jnp.stack([ref[rows, g * HEAD_DIM:(g + 1) * HEAD_DIM] for g in range(G)])
        q, k, v, gc, beta = heads(q_ref), heads(k_ref), heads(v_ref), heads(gc_ref), heads(beta_ref)
        gl = heads(gc_ref, slice(C - 1, C))
        c = _dn_chunk_common(q, k, v, gc, beta, gl)
        Tm = _unit_lower_inverse(c["L"])
        u = _dot(Tm, c["vb"])
        w = _dot(Tm, c["kbg"])
        S0 = S[...]
        vnew = u - _dot(w, S0)
        o = _dot(c["qd"], S0) + _dot(c["Aqk"], vnew)
        S[...] = S0 * c["egl"] + _dot(c["kte"], vnew, _TN)
        tm_ref[...] = Tm
        s_ref[...] = S0
        mix = (_rms(o, gain_ref[...]) * _silu(heads(z_ref))).astype(BF16)
        for g in range(G):
            sl = slice(g * HEAD_DIM, (g + 1) * HEAD_DIM)
            o_ref[:, sl] = o[g]
            mix_ref[:, sl] = mix[g]

    blk = pl.BlockSpec((C, GW), lambda h, n: (n, h))
    mat = pl.BlockSpec((G, None, C, C), lambda h, n: (h, n, 0, 0))
    return pl.pallas_call(
        body, name="dn_scan", grid=(H // G, N),
        in_specs=[blk, blk, blk, blk, blk, pl.BlockSpec((C, GW), lambda h, n: (n, zb0 + h)),
                  pl.BlockSpec((1, HEAD_DIM), lambda h, n: (0, 0))],
        out_specs=[blk, blk, mat, mat],
        out_shape=[_sds((T, W), F32), _sds((T, W), BF16), _sds((H, N, C, C), F32), _sds((H, N, C, C), F32)],
        scratch_shapes=[pltpu.VMEM((G, HEAD_DIM, HEAD_DIM), F32)],
        compiler_params=_params(("parallel", "arbitrary"), 32 * 2**20),
    )(*_hbm(qn, kn, vn, gc_full, beta_full, proj, gain))


def _dn_scan_bwd(qn, kn, vn, gc_full, beta_full, proj, z_col0, gain, o_raw, tm_all, s_all, dmix, dmix_col0):
    T, W = qn.shape
    C = DN_CHUNK
    N = T // C
    H = DN_HEADS
    G = DN_GROUP
    GW = G * HEAD_DIM
    zb0 = z_col0 // GW
    mb0 = dmix_col0 // GW

    def body(q_ref, k_ref, v_ref, gc_ref, beta_ref, z_ref, gain_ref, o_ref, tm_ref, s_ref, dmix_ref,
             dq_ref, dk_ref, dv_ref, dgc_ref, dbeta_ref, dz_ref, dgain_ref, dS):
        @pl.when(pl.program_id(1) == 0)
        def _():
            dS[...] = jnp.zeros_like(dS)

        @pl.when((pl.program_id(0) == 0) & (pl.program_id(1) == 0))
        def _():
            dgain_ref[...] = jnp.zeros_like(dgain_ref)

        heads = lambda ref, rows=slice(None): jnp.stack([ref[rows, g * HEAD_DIM:(g + 1) * HEAD_DIM] for g in range(G)])
        total = lambda x: jnp.sum(jnp.sum(x, axis=-1, keepdims=True), axis=-2, keepdims=True)
        gain = gain_ref[...]
        o, z, dmix = heads(o_ref), heads(z_ref), heads(dmix_ref)
        dz = (dmix * _rms(o, gain) * _silu_grad(z)).astype(BF16)
        do, dgain = _rms_bwd(dmix * _silu(z), o, gain)
        dgain_ref[...] += jnp.sum(dgain, axis=0)

        q, k, v, gc, beta = heads(q_ref), heads(k_ref), heads(v_ref), heads(gc_ref), heads(beta_ref)
        gl = heads(gc_ref, slice(C - 1, C))
        c = _dn_chunk_common(q, k, v, gc, beta, gl)
        Tm, S0, dS1 = tm_ref[...], s_ref[...], dS[...]
        w = _dot(Tm, c["kbg"])
        vnew = _dot(Tm, c["vb"]) - _dot(w, S0)

        dvnew = _dot(c["Aqk"], do, _TN) + _dot(c["kte"], dS1)
        dAqk = jnp.where(c["causal"], _dot(do, vnew, _NT), 0.0)
        dqd = _dot(do, S0, _NT)
        dkte = _dot(vnew, dS1, _NT)
        dgl = total(dS1 * S0) * c["egl"]
        dw = -_dot(dvnew, S0, _NT)
        dS[...] = dS1 * c["egl"] + _dot(c["qd"], do, _TN) - _dot(w, dvnew, _TN)

        dTm = _dot(dvnew, c["vb"], _NT) + _dot(dw, c["kbg"], _NT)
        dvb = _dot(Tm, dvnew, _TN)
        dkbg = _dot(Tm, dw, _TN)
        dL = jnp.where(c["strict"], -_mm3(_mm3(Tm, dTm, _TN), Tm, _NT), 0.0)
        dP = dL * c["decay"]
        dQ = dAqk * c["decay"]
        M = dL * c["L"] + dAqk * c["Aqk"]
        dkb = _dot(dP, k) + dkbg * c["eg"]
        dk = _dot(dP, c["kb"], _TN) + _dot(dQ, q, _TN) + dkte * c["ektg"] + dkb * beta
        dq = _dot(dQ, k) + dqd * c["eg"]
        tk = _rowsum(dkte * c["kte"])
        dgc = (_rowsum(M) - _rowsum(_t(M)) + _rowsum(dqd * c["qd"]) - tk + _rowsum(dkbg * c["kbg"]))
        dgl = dgl + total(tk)
        dgc = jnp.broadcast_to(dgc, q.shape) + jnp.where(_iota2((C, HEAD_DIM), 0) == C - 1, dgl, 0.0)
        dv = dvb * beta
        dbeta = jnp.broadcast_to(_rowsum(dkb * k) + _rowsum(dvb * v), q.shape)
        for g in range(G):
            sl = slice(g * HEAD_DIM, (g + 1) * HEAD_DIM)
            dz_ref[:, sl] = dz[g]
            dq_ref[:, sl] = dq[g]
            dk_ref[:, sl] = dk[g]
            dv_ref[:, sl] = dv[g]
            dgc_ref[:, sl] = dgc[g]
            dbeta_ref[:, sl] = dbeta[g]

    rev = lambda off: pl.BlockSpec((C, GW), lambda h, n: (N - 1 - n, off + h))
    mat = pl.BlockSpec((G, None, C, C), lambda h, n: (h, N - 1 - n, 0, 0))
    vec = pl.BlockSpec((1, HEAD_DIM), lambda h, n: (0, 0))
    return pl.pallas_call(
        body, name="dn_scan_bwd", grid=(H // G, N),
        in_specs=[rev(0), rev(0), rev(0), rev(0), rev(0), rev(zb0), vec, rev(0), mat, mat, rev(mb0)],
        out_specs=[rev(0)] * 6 + [vec],
        out_shape=[_sds((T, W), F32)] * 5 + [_sds((T, W), BF16), _sds((1, HEAD_DIM), F32)],
        scratch_shapes=[pltpu.VMEM((G, HEAD_DIM, HEAD_DIM), F32)],
        compiler_params=_params(("arbitrary", "arbitrary"), 40 * 2**20),
    )(*_hbm(qn, kn, vn, gc_full, beta_full, proj, gain, o_raw, tm_all, s_all, dmix))


def _sb_terms(z, ahead, first_key):
    lb = jnp.minimum(z, 0.0) - jnp.log(1.0 + jnp.exp(-jnp.abs(z)))
    if ahead is None:
        return None, lb, lb - z
    valid = ahead < -first_key
    return valid, lb, jnp.where(valid, lb - z, 0.0)


def _masked(valid, x):
    return x if valid is None else jnp.where(valid, x, 0.0)


def _sb_attention(qkv, gain, tq_cap=2048):
    T = qkv.shape[0]
    H = SB_HEADS
    B = min(SB_KEYS, T)
    TQ = _tile(T, tq_cap, B)
    per = TQ // B

    assert per % 2 == 0
    n_saved = per * (T // TQ) * (T // TQ + 1) // 2

    def body(q_ref, k_ref, v_ref, gain_ref, o_ref, mix_ref, att_hbm, lb_hbm, att_buf, lb_buf, sems):
        h, i = pl.program_id(0), pl.program_id(1)
        q = q_ref[...].astype(BF16)
        upper = (_iota2((B, B), 0) > _iota2((B, B), 1)).astype(BF16)
        ahead = _iota2((TQ, B), 1) - _iota2((TQ, B), 0)
        last = (i + 1) * per - 1
        base = per * (i * (i + 1) // 2)

        def save(slot, pair):
            return (pltpu.make_async_copy(att_buf.at[slot], att_hbm.at[h, pair], sems.at[0, slot]),
                    pltpu.make_async_copy(lb_buf.at[slot], lb_hbm.at[h, pair], sems.at[1, slot]))

        def pair(j, slot, r0, acc, R):
            n = TQ - (r0 or 0)
            top = slice(r0 or 0, TQ)
            rows = pl.ds(pl.multiple_of(j * B, B), B)
            z = _dot(q[top], k_ref[rows, :], _NT) * (HEAD_DIM ** -0.5)
            valid, lb, l1m = _sb_terms(z, None if r0 is None else ahead[top], r0)
            att = _masked(valid, jnp.exp(lb + R[top] + _dot01(l1m, upper, passes=2))).astype(BF16)
            att_buf[slot, pl.ds(TQ - n, n), :] = att
            lb_buf[slot, pl.ds(TQ - n, n), :] = (lb if valid is None else jnp.where(valid, lb, -1e30)).astype(BF16)
            if r0:
                att_buf[slot, pl.ds(0, r0), :] = jnp.zeros((r0, B), BF16)
                lb_buf[slot, pl.ds(0, r0), :] = jnp.full((r0, B), -1e30, BF16)
            for c in save(slot, base + j):
                c.start()
            d_acc, d_R = _dot(att, v_ref[rows, :]), _rowsum(l1m)
            if r0:
                d_acc = jnp.concatenate([jnp.zeros((r0, HEAD_DIM), F32), d_acc], axis=0)
                d_R = jnp.concatenate([jnp.zeros((r0, 1), F32), d_R], axis=0)
            return acc + d_acc, R + d_R

        def step(jj, carry):
            slot = jj % 2

            @pl.when(jj >= 2)
            def _():
                for c in save(slot, 0):
                    c.wait()

            return pair(last - jj, slot, None, *carry)

        carry = (jnp.zeros((TQ, HEAD_DIM), F32), jnp.zeros((TQ, 1), F32))
        for jj in range(per):
            if jj >= 2:
                for c in save(jj % 2, 0):
                    c.wait()
            carry = pair(last - jj, jj % 2, (per - 1 - jj) * B, *carry)
        acc, _ = lax.fori_loop(per, last + 1, step, carry)
        for slot in range(2):
            for c in save(slot, 0):
                c.wait()
        o_ref[...] = acc
        mix_ref[...] = _rms(acc, gain_ref[...]).astype(BF16)

    head = lambda off: pl.BlockSpec((T, HEAD_DIM), lambda h, i: (0, off + h))
    blk = pl.BlockSpec((TQ, HEAD_DIM), lambda h, i: (i, h))
    return pl.pallas_call(
        body, name="sb_attention", grid=(H, T // TQ),
        in_specs=[blk, head(H), head(2 * H), pl.BlockSpec((1, HEAD_DIM), lambda h, i: (0, 0))],
        out_specs=[blk, blk, ANY_SPEC, ANY_SPEC],
        out_shape=[_sds((T, H * HEAD_DIM), F32), _sds((T, H * HEAD_DIM), BF16),
                   _sds((H, n_saved, TQ, B), BF16), _sds((H, n_saved, TQ, B), BF16)],
        scratch_shapes=[pltpu.VMEM((2, TQ, B), BF16), pltpu.VMEM((2, TQ, B), BF16), pltpu.SemaphoreType.DMA((2, 2))],
        compiler_params=_params(("parallel", "arbitrary"), 8 * _nbytes((T, HEAD_DIM), BF16) + 32 * _nbytes((TQ, B), F32)),
    )(*_hbm(qkv, qkv, qkv, gain))


def _sb_attention_bwd(qkv, gain, o_raw, att_all, lb_all, dmix):
    T = qkv.shape[0]
    H = SB_HEADS
    TQ, B = att_all.shape[2:]
    per = TQ // B
    scale = HEAD_DIM ** -0.5

    def body(q_ref, k_ref, v_ref, gain_ref, o_ref, dmix_ref, att_hbm, lb_hbm, dq_ref, dk_ref, dv_ref, dgain_ref,
             att_buf, lb_buf, sems):
        h, i = pl.program_id(0), pl.program_id(1)

        @pl.when(i == 0)
        def _():
            dk_ref[...] = jnp.zeros_like(dk_ref)
            dv_ref[...] = jnp.zeros_like(dv_ref)

        @pl.when((pl.program_id(0) == 0) & (i == 0))
        def _():
            dgain_ref[...] = jnp.zeros_like(dgain_ref)

        q = q_ref[...].astype(BF16)
        o = o_ref[...]
        do, dgain = _rms_bwd(dmix_ref[...], o, gain_ref[...])
        dgain_ref[...] += dgain
        do_b = do.astype(BF16)
        before = (_iota2((B, B), 0) < _iota2((B, B), 1)).astype(BF16)
        base = per * (i * (i + 1) // 2)

        def fetch(slot, pair):
            return (pltpu.make_async_copy(att_hbm.at[h, pair], att_buf.at[slot], sems.at[0, slot]),
                    pltpu.make_async_copy(lb_hbm.at[h, pair], lb_buf.at[slot], sems.at[1, slot]))

        for c in fetch(0, base):
            c.start()

        def pair(j, slot, r0, dq, PG):
            top = slice(r0, TQ)
            rows = pl.ds(pl.multiple_of(j * B, B), B)
            kj = k_ref[rows, :]
            att = att_buf[slot, pl.ds(r0, TQ - r0), :]
            sig = jnp.exp(lb_buf[slot, pl.ds(r0, TQ - r0), :].astype(F32))
            G = _dot(do_b[top], v_ref[rows, :], _NT) * att.astype(F32)
            dv_ref[rows, :] += _dot(att, do_b[top], _TN)
            cum = PG[top] + _dot01(G, before, passes=2)
            dz = (G * (1.0 - sig) - sig * cum) * scale
            dk_ref[rows, :] += _dot(dz, q[top], _TN)
            d_dq, d_PG = _dot(dz, kj), _rowsum(G)
            if r0:
                d_dq = jnp.concatenate([jnp.zeros((r0, HEAD_DIM), F32), d_dq], axis=0)
                d_PG = jnp.concatenate([jnp.zeros((r0, 1), F32), d_PG], axis=0)
            return dq + d_dq, PG + d_PG

        def step(j, carry):
            slot = j % 2
            for c in fetch(slot, 0):
                c.wait()
            for c in fetch(1 - slot, base + j + 1):
                c.start()
            return pair(j, slot, 0, *carry)

        carry = lax.fori_loop(0, i * per, step, (jnp.zeros((TQ, HEAD_DIM), F32), jnp.zeros((TQ, 1), F32)))
        for c_blk in range(per):
            slot = c_blk % 2
            for c in fetch(slot, 0):
                c.wait()
            if c_blk + 1 < per:
                for c in fetch(1 - slot, base + i * per + c_blk + 1):
                    c.start()
            carry = pair(i * per + c_blk, slot, c_blk * B, *carry)
        dq_ref[...] = carry[0].astype(BF16)

    head = lambda off: pl.BlockSpec((T, HEAD_DIM), lambda h, i: (0, off + h))
    blk = pl.BlockSpec((TQ, HEAD_DIM), lambda h, i: (i, h))
    vec = pl.BlockSpec((1, HEAD_DIM), lambda h, i: (0, 0))
    return pl.pallas_call(
        body, name="sb_attention_bwd", grid=(H, T // TQ),
        in_specs=[blk, head(H), head(2 * H), vec, blk, blk, ANY_SPEC, ANY_SPEC],
        out_specs=[blk, head(0), head(0), vec],
        out_shape=[_sds((T, H * HEAD_DIM), BF16), _sds((T, H * HEAD_DIM), F32), _sds((T, H * HEAD_DIM), F32),
                   _sds((1, HEAD_DIM), F32)],
        scratch_shapes=[pltpu.VMEM((2, TQ, B), BF16), pltpu.VMEM((2, TQ, B), BF16), pltpu.SemaphoreType.DMA((2, 2))],
        compiler_params=_params(("arbitrary", "arbitrary"), 8 * _nbytes((T, HEAD_DIM), F32) + 32 * _nbytes((TQ, B), F32)),
    )(*_hbm(qkv, qkv, qkv, gain, o_raw, dmix, att_all, lb_all))


def _adamw_math(w, g, m, v):
    m = ADAM_B1 * m + (1.0 - ADAM_B1) * g
    v = ADAM_B2 * v + (1.0 - ADAM_B2) * (g * g)
    m_hat = m / (1.0 - ADAM_B1 ** ADAM_STEP)
    v_hat = v / (1.0 - ADAM_B2 ** ADAM_STEP)
    delta = -ADAM_LR * (m_hat / (jnp.sqrt(v_hat) + ADAM_EPS) + ADAM_WD * w)
    return delta, m, v


def _adamw_sharded(parts, w, m, v, name):
    _, R, C = w.shape
    if R % SUBLANES == 0:
        tr, tc = _tile(R, max(SUBLANES, (2**20 // (4 * C)) // SUBLANES * SUBLANES), SUBLANES), C
    else:
        tr, tc = R, _tile(C, max(LANES, (2**20 // (4 * R)) // LANES * LANES), LANES)

    def body(p_ref, w_ref, m_ref, v_ref, g_ref, d_ref, nm_ref, nv_ref):
        g = p_ref[0].astype(F32)
        for d in range(1, N_DEV):
            g = g + p_ref[d].astype(F32)
        g_ref[...] = g
        d_ref[...], nm_ref[...], nv_ref[...] = _adamw_math(w_ref[...], g, m_ref[...], v_ref[...])

    blk = pl.BlockSpec((None, tr, tc), lambda i, j: (0, i, j))
    return pl.pallas_call(
        body, name=name, grid=(R // tr, C // tc),
        in_specs=[pl.BlockSpec((N_DEV, tr, tc), lambda i, j: (0, i, j)), blk, blk, blk],
        out_specs=[blk] * 4, out_shape=[_sds((1, R, C), F32)] * 4,
        compiler_params=_params(("parallel", "parallel"), 40 * 2**20),
    )(*_hbm(parts, w, m, v))


def _adamw_packed(g, w, m, v):
    def body(g_ref, w_ref, m_ref, v_ref, d_ref, nm_ref, nv_ref):
        d_ref[...], nm_ref[...], nv_ref[...] = _adamw_math(w_ref[...], g_ref[...], m_ref[...], v_ref[...])

    return pl.pallas_call(body, name="adamw_packed", out_shape=[_sds(g.shape, F32)] * 3,
                          compiler_params=_params((), 16 * 2**20))(g, w, m, v)


def _my_place():
    x, y, c = lax.axis_index("x"), lax.axis_index("y"), lax.axis_index("c")
    return x, y, c


def _peer(place, k):
    x, y, c = place
    return (1 - x if k & 4 else x, 1 - y if k & 2 else y, 1 - c if k & 1 else c)


def _index(place):
    x, y, c = place
    return 4 * x + 2 * y + c


HBM_SPEC = pl.BlockSpec(memory_space=pltpu.HBM)


def _all_gather(block, name):
    R, C = block.shape

    def body(x_ref, out_ref, send_sems, recv_sems, local_sem):
        me = _my_place()
        sibling = _peer(me, 1)
        chips = [2, 4, 6]

        def copy(sem, origin, to, src=None):
            slot = out_ref.at[_index(origin)]
            return pltpu.make_async_remote_copy(
                src_ref=slot if src is None else src, dst_ref=slot, send_sem=send_sems.at[sem], recv_sem=recv_sems.at[sem],
                device_id=to, device_id_type=MESH)

        mine = pltpu.make_async_copy(x_ref, out_ref.at[_index(me)], local_sem)
        mine.start()
        first = [copy(0, me, sibling, src=x_ref)] + [copy(1 + n, me, _peer(me, k), src=x_ref) for n, k in enumerate(chips)]
        for cp in first:
            cp.start()
        passed = [copy(4 + n, _peer(me, k), sibling) for n, k in enumerate(chips)]
        for n, k in enumerate(chips):
            copy(1 + n, _peer(me, k), me).wait_recv()
            passed[n].start()
        copy(0, sibling, me).wait_recv()
        for n, k in enumerate(chips):
            copy(4 + n, _peer(sibling, k), me).wait_recv()
        for cp in first + passed:
            cp.wait_send()
        mine.wait()

    return pl.pallas_call(
        body, name=name, in_specs=[HBM_SPEC], out_specs=HBM_SPEC,
        out_shape=_sds((N_DEV, R, C), block.dtype),
        scratch_shapes=[pltpu.SemaphoreType.DMA((7,)), pltpu.SemaphoreType.DMA((7,)), pltpu.SemaphoreType.DMA],
    )(block)


SEM_SPEC = pl.BlockSpec(memory_space=pltpu.SEMAPHORE)
ANY_SPEC = pl.BlockSpec(memory_space=pl.ANY)
_EFFECT = pltpu.SideEffectType.DATAFLOW_SIDE_EFFECTING


def _spread_start(x, per_peer, name, after):
    R, C = x.shape[-2:]

    def body(x_ref, land_ref, after_ref, send_sems, recv_sems, x_thru, land_thru, token):
        me = _my_place()
        for k in range(1, N_DEV):
            to = _peer(me, k)
            pltpu.make_async_remote_copy(
                src_ref=x_ref.at[_index(to)] if per_peer else x_ref, dst_ref=land_ref.at[_index(me)],
                send_sem=send_sems.at[k - 1], recv_sem=recv_sems.at[k - 1], device_id=to, device_id_type=MESH).start()
        token[...] = jnp.zeros_like(token)

    land = lax.empty((N_DEV, R, C), x.dtype)
    send_sems, recv_sems, x_thru, land_thru, token = pl.pallas_call(
        body, name=name,
        out_shape=(pltpu.SemaphoreType.DMA((N_DEV - 1,)), pltpu.SemaphoreType.DMA((N_DEV - 1,)),
                   pltpu.HBM(x.shape, x.dtype), pltpu.HBM(land.shape, land.dtype), _sds((SUBLANES, LANES), F32)),
        in_specs=(HBM_SPEC, HBM_SPEC, ANY_SPEC),
        out_specs=(SEM_SPEC, SEM_SPEC, HBM_SPEC, HBM_SPEC, pl.BlockSpec(memory_space=pltpu.VMEM)),
        input_output_aliases={0: 2, 1: 3},
        compiler_params=pltpu.CompilerParams(has_side_effects=_EFFECT),
    )(pltpu.with_memory_space_constraint(x, pltpu.HBM), pltpu.with_memory_space_constraint(land, pltpu.HBM), after)
    return (send_sems, recv_sems, x_thru, land_thru), token


def _spread_wait(state, per_peer, name, after):
    send_sems, recv_sems, x_thru, land_thru = state

    def body(x_ref, land_ref, send_sems, recv_sems, after_ref, x_dead, got_ref):
        me = _my_place()
        for k in range(1, N_DEV):
            frm = _peer(me, k)
            copy = pltpu.make_async_remote_copy(
                src_ref=x_ref.at[_index(frm)] if per_peer else x_ref, dst_ref=land_ref.at[_index(frm)],
                send_sem=send_sems.at[k - 1], recv_sem=recv_sems.at[k - 1], device_id=frm, device_id_type=MESH)
            copy.wait_send()
            copy.wait_recv()

    x_back, got = pl.pallas_call(
        body, name=name,
        out_shape=(pltpu.HBM(x_thru.shape, x_thru.dtype), pltpu.HBM(land_thru.shape, land_thru.dtype)),
        in_specs=(HBM_SPEC, HBM_SPEC, SEM_SPEC, SEM_SPEC, ANY_SPEC), out_specs=(HBM_SPEC, HBM_SPEC),
        input_output_aliases={0: 0, 1: 1},
        compiler_params=pltpu.CompilerParams(has_side_effects=_EFFECT),
    )(x_thru, land_thru, send_sems, recv_sems, after)
    me = _index(_my_place())
    own = lax.dynamic_index_in_dim(x_back, me, axis=0, keepdims=True) if per_peer else x_back[None]
    return lax.dynamic_update_slice_in_dim(got, own, me, axis=0)


def _all_reduce_packed(vec, after):
    R, L = vec.shape

    def body(x_ref, after_ref, out_ref, buf, send_sems, recv_sems):
        me = _my_place()
        buf[_index(me)] = x_ref[...]
        copies = []
        for k in range(1, N_DEV):
            to = _peer(me, k)
            cp = pltpu.make_async_remote_copy(
                src_ref=x_ref, dst_ref=buf.at[_index(me)],
                send_sem=send_sems.at[k - 1], recv_sem=recv_sems.at[k - 1], device_id=to, device_id_type=MESH)
            cp.start()
            copies.append(cp)
        for k in range(1, N_DEV):
            frm = _peer(me, k)
            pltpu.make_async_remote_copy(
                src_ref=x_ref, dst_ref=buf.at[_index(frm)],
                send_sem=send_sems.at[k - 1], recv_sem=recv_sems.at[k - 1], device_id=frm, device_id_type=MESH).wait_recv()
        for cp in copies:
            cp.wait_send()
        acc = buf[0]
        for d in range(1, N_DEV):
            acc = acc + buf[d]
        out_ref[...] = acc

    vm = pl.BlockSpec(memory_space=pltpu.VMEM)
    return pl.pallas_call(
        body, name="all_reduce_packed", in_specs=[vm, ANY_SPEC], out_specs=vm, out_shape=_sds((R, L), F32),
        scratch_shapes=[pltpu.VMEM((N_DEV, R, L), F32), pltpu.SemaphoreType.DMA((7,)), pltpu.SemaphoreType.DMA((7,))],
        compiler_params=pltpu.CompilerParams(vmem_limit_bytes=32 * 2**20),
    )(vec, after)


def _pack(arrays):
    rows = []
    for a in arrays:
        f = a.reshape(-1).astype(F32)
        pad = (-f.shape[0]) % LANES
        rows.append(jnp.pad(f, (0, pad)).reshape(-1, LANES))
    out = jnp.concatenate(rows, axis=0)
    return jnp.pad(out, ((0, (-out.shape[0]) % SUBLANES), (0, 0)))


def _unpack(packed, shapes):
    out, r = [], 0
    for s in shapes:
        n = math.prod(s)
        nr = -(-n // LANES)
        out.append(packed[r:r + nr].reshape(-1)[:n].reshape(s))
        r += nr
    return out


def _row_blocks(g):
    R, C = g.shape
    return g.astype(BF16).reshape(N_DEV, R // N_DEV, C)


def kernel(x, w_in, sb_out_gain, dn_conv_w, dn_a_log, dn_dt_bias, dn_out_gain, w_out, ln_mix_pre, ln_mix_post, w_up, ffn_conv_w, ffn_conv_b, w_down, ln_ffn_pre, ln_ffn_post, loss_target, m_w_in, m_sb_out_gain, m_dn_conv_w, m_dn_a_log, m_dn_dt_bias, m_dn_out_gain, m_w_out, m_ln_mix_pre, m_ln_mix_post, m_w_up, m_ffn_conv_w, m_ffn_conv_b, m_w_down, m_ln_ffn_pre, m_ln_ffn_post, v_w_in, v_sb_out_gain, v_dn_conv_w, v_dn_a_log, v_dn_dt_bias, v_dn_out_gain, v_w_out, v_ln_mix_pre, v_ln_mix_post, v_w_up, v_ffn_conv_w, v_ffn_conv_b, v_w_down, v_ln_ffn_pre, v_ln_ffn_post):
    T, D = x.shape[1], x.shape[2]
    SBW = SB_HEADS * HEAD_DIM
    DNW = DN_HEADS * HEAD_DIM
    in_cols = 3 * SBW + 4 * DNW + 2 * DN_HEADS
    main_cols = 3 * SBW + 4 * DNW
    in_pad = main_cols + LANES
    qkv0, z0 = 3 * SBW, 3 * SBW + 3 * DNW
    gate_block = main_cols // LANES
    x2, tgt = x[0], loss_target[0]

    g_in = _all_gather(jnp.swapaxes(w_in[0], 0, 1).astype(BF16), "gather_w_in")
    small_w = _all_gather(_pack([dn_conv_w[0], ffn_conv_w[0]]), "gather_conv_w")
    st_out, tok = _spread_start(w_out[0].astype(BF16), False, "gather_w_out_start", g_in)
    st_up, tok = _spread_start(w_up[0].astype(BF16), False, "gather_w_up_start", tok)
    st_down, tok_gather = _spread_start(w_down[0].astype(BF16), False, "gather_w_down_start", tok)
    w_in_t = jnp.pad(g_in.reshape(in_cols, D), ((0, in_pad - in_cols), (0, 0)))
    parts = [_unpack(small_w[d], [dn_conv_w.shape[1:], ffn_conv_w.shape[1:]]) for d in range(N_DEV)]
    dn_cw = jnp.concatenate([p[0] for p in parts], axis=1)
    ffn_cw = jnp.concatenate([p[1] for p in parts], axis=1)
    lane_pad = lambda a, off: jnp.pad(a, ((0, 0), (off, LANES - off - a.shape[1])))
    a_log_l, dt_bias_l = lane_pad(dn_a_log, DN_HEADS), lane_pad(dn_dt_bias, DN_HEADS)

    xn = _norm_in(x2, ln_mix_pre)
    proj = _matmul(xn, w_in_t, "nt", F32, "proj_in", tm_cap=512, tn_cap=2432, after=tok_gather)
    o_sb, mix_sb, sb_att, sb_lb = _sb_attention(proj, sb_out_gain)
    qn = _dn_branch(proj, qkv0, dn_cw, 0, True, HEAD_DIM ** -0.5)
    kn = _dn_branch(proj, qkv0 + DNW, dn_cw, DNW, True, 1.0)
    vn = _dn_branch(proj, qkv0 + 2 * DNW, dn_cw, 2 * DNW, False, 1.0)
    gc_full, beta_full = _dn_gates(proj, gate_block, a_log_l, dt_bias_l)
    o_dn, mix_dn, tm_all, s_all = _dn_scan(qn, kn, vn, gc_full, beta_full, proj, z0, dn_out_gain)
    mix = jnp.concatenate([mix_sb, mix_dn], axis=1)
    w_out_f = _spread_wait(st_out, False, "gather_w_out_wait", mix).reshape(w_out.shape[1] * N_DEV, D)
    m = _matmul(mix, w_out_f, "nn", F32, "proj_out")
    h, hn = _mix_residual(x2, m, ln_mix_post, ln_ffn_pre)
    w_up_cut = _spread_wait(st_up, False, "gather_w_up_wait", hn)
    u = _matmul(hn, w_up_cut, "nn", F32, "ffn_up", tn_cap=w_up.shape[2], b_cut=True)
    act = _ffn_act(u, ffn_cw, ffn_conv_b)
    w_down_f = _spread_wait(st_down, False, "gather_w_down_wait", act).reshape(w_down.shape[1] * N_DEV, D)
    f = _matmul(act, w_down_f, "nn", F32, "ffn_down", tk_cap=2816)
    dy, df, d_ln_ffn_post, loss_part = _loss_head(h, f, ln_ffn_post, tgt)

    d_w_down = _matmul(act, df, "tn", BF16, "grad_w_down")
    st_xd, tok = _spread_start(_row_blocks(d_w_down), True, "exchange_w_down_start", loss_part)
    da = _matmul(df, w_down_f, "nt", F32, "bwd_ffn_down", after=tok)
    du, d_ffn_cwb = _ffn_act_bwd(u, ffn_cw, ffn_conv_b, da)
    d_ffn_cwb = jnp.concatenate([d_ffn_cwb[0], d_ffn_cwb[1]], axis=1)
    d_w_up_cut = _matmul(hn, du, "tn", BF16, "grad_w_up", b_cut=True, out_cut=True)
    st_xu, tok = _spread_start(d_w_up_cut, True, "exchange_w_up_start", d_ffn_cwb)
    dhn = _matmul(du, w_up_cut, "nt", F32, "bwd_ffn_up", after=tok, a_cut=True, b_cut=True)
    dh, dm, d_ln_ffn_pre, d_ln_mix_post = _ffn_residual_bwd(dy, dhn, h, ln_ffn_pre, m, ln_mix_post)

    d_w_out = _matmul(mix, dm, "tn", BF16, "grad_w_out")
    st_xo, tok = _spread_start(_row_blocks(d_w_out), True, "exchange_w_out_start", d_ln_ffn_pre)
    dmix = _matmul(dm, w_out_f, "nt", F32, "bwd_proj_out", after=tok)
    dq_sb, dk_sb, dv_sb, d_sb_gain = _sb_attention_bwd(proj, sb_out_gain, o_sb, sb_att, sb_lb, dmix)
    dqn, dkn, dvn, dgc_full, dbeta_full, dz, d_dn_gain = _dn_scan_bwd(
        qn, kn, vn, gc_full, beta_full, proj, z0, dn_out_gain, o_dn, tm_all, s_all, dmix, SBW)
    du_q, dcw_q = _dn_branch_bwd(proj, qkv0, dn_cw, 0, True, HEAD_DIM ** -0.5, dqn)
    du_k, dcw_k = _dn_branch_bwd(proj, qkv0 + DNW, dn_cw, DNW, True, 1.0, dkn)
    du_v, dcw_v = _dn_branch_bwd(proj, qkv0 + 2 * DNW, dn_cw, 2 * DNW, False, 1.0, dvn)
    dba, d_a_log_l, d_dt_bias_l = _dn_gates_bwd(proj, gate_block, a_log_l, dt_bias_l, dgc_full, dbeta_full)
    dproj = jnp.concatenate([dq_sb, dk_sb.astype(BF16), dv_sb.astype(BF16), du_q, du_k, du_v, dz, dba], axis=1)
    d_w_in_t = _matmul(dproj, xn, "tn", BF16, "grad_w_in", tm_cap=2432, tn_cap=512, tk_cap=1024)
    d_w_in_cut = d_w_in_t[:in_cols].reshape(N_DEV, in_cols // N_DEV, D)
    st_xi, tok = _spread_start(d_w_in_cut, True, "exchange_w_in_start", d_sb_gain)
    dxn = _matmul(dproj, w_in_t, "nn", F32, "bwd_proj_in", tk_cap=2432, after=tok)
    grad_x, d_ln_mix_pre = _input_bwd(dh, dxn, x2, ln_mix_pre)

    big = {}
    after = grad_x
    for n, st, w_, m_, v_ in [("w_down", st_xd, w_down, m_w_down, v_w_down), ("w_up", st_xu, w_up, m_w_up, v_w_up),
                              ("w_out", st_xo, w_out, m_w_out, v_w_out)]:
        got = _spread_wait(st, True, "exchange_" + n + "_wait", after)
        big[n] = _adamw_sharded(got, w_, m_, v_, "adamw_" + n)
        after = big[n][1]

    d_dn_cw = jnp.concatenate([dcw_q[:SHORT_CONV], dcw_k[:SHORT_CONV], dcw_v[:SHORT_CONV]], axis=1)
    small = [loss_part[:, :1], d_sb_gain, d_a_log_l[:, DN_HEADS:2 * DN_HEADS], d_dt_bias_l[:, DN_HEADS:2 * DN_HEADS], d_dn_gain,
             d_ln_mix_pre, d_ln_mix_post, d_ffn_cwb[FFN_CONV:FFN_CONV + 1], d_ln_ffn_pre, d_ln_ffn_post,
             d_dn_cw, d_ffn_cwb[:FFN_CONV]]
    shapes = [a.shape for a in small]
    red = _unpack(_all_reduce_packed(_pack(small), after), shapes)
    loss = red[0].reshape(())
    me = _index(_my_place())
    g_dn_cw = lax.dynamic_slice_in_dim(red[10], me * dn_conv_w.shape[2], dn_conv_w.shape[2], axis=1)
    g_ffn_cw = lax.dynamic_slice_in_dim(red[11], me * ffn_conv_w.shape[2], ffn_conv_w.shape[2], axis=1)
    names = ["sb_out_gain", "dn_conv_w", "dn_a_log", "dn_dt_bias", "dn_out_gain", "ln_mix_pre", "ln_mix_post",
             "ffn_conv_w", "ffn_conv_b", "ln_ffn_pre", "ln_ffn_post"]
    g_small = dict(sb_out_gain=red[1], dn_conv_w=g_dn_cw[None], dn_a_log=red[2], dn_dt_bias=red[3], dn_out_gain=red[4],
                   ln_mix_pre=red[5], ln_mix_post=red[6], ffn_conv_w=g_ffn_cw[None], ffn_conv_b=red[7],
                   ln_ffn_pre=red[8], ln_ffn_post=red[9])
    w_small = dict(sb_out_gain=sb_out_gain, dn_conv_w=dn_conv_w, dn_a_log=dn_a_log, dn_dt_bias=dn_dt_bias,
                   dn_out_gain=dn_out_gain, ln_mix_pre=ln_mix_pre, ln_mix_post=ln_mix_post, ffn_conv_w=ffn_conv_w,
                   ffn_conv_b=ffn_conv_b, ln_ffn_pre=ln_ffn_pre, ln_ffn_post=ln_ffn_post)
    m_small = dict(sb_out_gain=m_sb_out_gain, dn_conv_w=m_dn_conv_w, dn_a_log=m_dn_a_log, dn_dt_bias=m_dn_dt_bias,
                   dn_out_gain=m_dn_out_gain, ln_mix_pre=m_ln_mix_pre, ln_mix_post=m_ln_mix_post, ffn_conv_w=m_ffn_conv_w,
                   ffn_conv_b=m_ffn_conv_b, ln_ffn_pre=m_ln_ffn_pre, ln_ffn_post=m_ln_ffn_post)
    v_small = dict(sb_out_gain=v_sb_out_gain, dn_conv_w=v_dn_conv_w, dn_a_log=v_dn_a_log, dn_dt_bias=v_dn_dt_bias,
                   dn_out_gain=v_dn_out_gain, ln_mix_pre=v_ln_mix_pre, ln_mix_post=v_ln_mix_post, ffn_conv_w=v_ffn_conv_w,
                   ffn_conv_b=v_ffn_conv_b, ln_ffn_pre=v_ln_ffn_pre, ln_ffn_post=v_ln_ffn_post)
    sshapes = [w_small[n].shape for n in names]
    upd = _adamw_packed(_pack([g_small[n] for n in names]), _pack([w_small[n] for n in names]),
                        _pack([m_small[n] for n in names]), _pack([v_small[n] for n in names]))
    d_small, nm_small, nv_small = [dict(zip(names, _unpack(p, sshapes))) for p in upd]

    got = _spread_wait(st_xi, True, "exchange_w_in_wait", d_small["ln_ffn_post"])
    flip = lambda a: jnp.swapaxes(a, 1, 2)
    big["w_in"] = [flip(a) for a in _adamw_sharded(got, flip(w_in), flip(m_w_in), flip(v_w_in), "adamw_w_in")]

    order = ["w_in", "sb_out_gain", "dn_conv_w", "dn_a_log", "dn_dt_bias", "dn_out_gain", "w_out", "ln_mix_pre",
             "ln_mix_post", "w_up", "ffn_conv_w", "ffn_conv_b", "w_down", "ln_ffn_pre", "ln_ffn_post"]
    pick = lambda n, i: big[n][i] if n in big else [g_small, d_small, nm_small, nv_small][i][n].reshape(w_small[n].shape)
    return (loss, grad_x[None], *[pick(n, 0) for n in order], *[pick(n, 1) for n in order],
            *[pick(n, 2) for n in order], *[pick(n, 3) for n in order])
```

```python
import functools
import math

import jax
import jax.numpy as jnp
from jax import lax
from jax.experimental import pallas as pl
from jax.experimental.pallas import tpu as pltpu

F32 = jnp.float32
BF16 = jnp.bfloat16

N_DEV = 8
HEAD_DIM = 128
SB_HEADS = 8
DN_HEADS = 8
DN_CHUNK = 128
DN_GROUP = 8
INV_BLOCK = 16
SB_KEYS = 256
SHORT_CONV = 4
FFN_CONV = 3
EPS = 1e-6
LANES = 128
SUBLANES = 8
VMEM_CAP = 56 * 2**20

ADAM_LR = 0.001
ADAM_B1 = 0.9
ADAM_B2 = 0.999
ADAM_EPS = 1e-08
ADAM_WD = 0.01
ADAM_STEP = 10

MESH = pl.DeviceIdType.MESH

assert HEAD_DIM == DN_CHUNK == LANES


def _tile(n, cap, mult):
    if n <= cap:
        return n
    t = (cap // mult) * mult
    while t >= mult:
        if n % t == 0:
            return t
        t -= mult
    raise ValueError(f"no tile for {n} under {cap} in multiples of {mult}")


def _params(sem, vmem_bytes):
    limit = int(min(VMEM_CAP, max(vmem_bytes, 16 * 2**20)))
    if not sem:
        return pltpu.CompilerParams(vmem_limit_bytes=limit)
    return pltpu.CompilerParams(dimension_semantics=sem, vmem_limit_bytes=limit)


def _nbytes(shape, dtype):
    return math.prod(shape) * jnp.dtype(dtype).itemsize


def _hbm(*arrays):
    return [pltpu.with_memory_space_constraint(a, pltpu.HBM) for a in arrays]


_NN = (((1,), (0,)), ((), ()))
_NT = (((1,), (1,)), ((), ()))
_TN = (((0,), (0,)), ((), ()))


def _batched(dims, ndim):
    if ndim == 2:
        return dims
    (ca,), (cb,) = dims[0]
    return (((ca + 1,), (cb + 1,)), ((0,), (0,)))


def _dot(a, b, dims=_NN):
    return lax.dot_general(a.astype(BF16), b.astype(BF16), _batched(dims, a.ndim), preferred_element_type=F32)


def _split2(x):
    hi = x.astype(BF16)
    lo = (x - hi.astype(F32)).astype(BF16)
    return hi, lo


def _split3(x):
    hi = x.astype(BF16)
    r = x - hi.astype(F32)
    mid = r.astype(BF16)
    lo = (r - mid.astype(F32)).astype(BF16)
    return hi, mid, lo


def _dot01(x, m01, passes=3):
    parts = _split3(x) if passes == 3 else _split2(x)
    out = None
    for p in parts:
        t = lax.dot_general(p, m01, _NN, preferred_element_type=F32)
        out = t if out is None else out + t
    return out


def _dot01_left(m01, x, passes=3):
    parts = _split3(x) if passes == 3 else _split2(x)
    out = None
    for p in parts:
        t = lax.dot_general(m01, p, _NN, preferred_element_type=F32)
        out = t if out is None else out + t
    return out


def _mm3(a, b, dims=_NN):
    ah, al = _split2(a)
    bh, bl = _split2(b)
    d = functools.partial(lax.dot_general, dimension_numbers=_batched(dims, a.ndim), preferred_element_type=F32)
    return d(ah, bh) + (d(ah, bl) + d(al, bh))


def _rowsum(x):
    return jnp.sum(x, axis=-1, keepdims=True)


def _t(x):
    return jnp.swapaxes(x, -1, -2)


def _sigmoid(x):
    return 1.0 / (1.0 + jnp.exp(-x))


def _softplus(x):
    return jnp.maximum(x, 0.0) + jnp.log(1.0 + jnp.exp(-jnp.abs(x)))


def _silu(x):
    return x * _sigmoid(x)


def _silu_grad(x):
    s = _sigmoid(x)
    return s * (1.0 + x * (1.0 - s))


_GELU_C = math.sqrt(2.0 / math.pi)


def _gelu(x):
    return 0.5 * x * (1.0 + jnp.tanh(_GELU_C * (x + 0.044715 * x * x * x)))


def _gelu_and_grad(x):
    x2 = x * x
    th = jnp.tanh(_GELU_C * (x + 0.044715 * x2 * x))
    half = 0.5 * (1.0 + th)
    return x * half, half + 0.5 * x * (1.0 - th * th) * (_GELU_C * (1.0 + 3.0 * 0.044715 * x2))


def _rms(x, g):
    r = lax.rsqrt(jnp.mean(x * x, axis=-1, keepdims=True) + EPS)
    return x * r * g


def _rms_bwd(dy, x, g):
    r = lax.rsqrt(jnp.mean(x * x, axis=-1, keepdims=True) + EPS)
    xh = x * r
    gdy = dy * g
    dx = r * (gdy - xh * jnp.mean(gdy * xh, axis=-1, keepdims=True))
    return dx, jnp.sum(dy * xh, axis=-2, keepdims=True)


def _iota2(shape, axis):
    return lax.broadcasted_iota(jnp.int32, shape, axis)


def _shift_down(cur, prev8, k):
    n = cur.shape[0]
    r = pltpu.roll(cur, k, 0)
    pr = pltpu.roll(prev8, k, 0)
    head = jnp.where(_iota2(pr.shape, 0) < k, pr, r[0:SUBLANES])
    if n == SUBLANES:
        return head
    return jnp.concatenate([head, r[SUBLANES:]], axis=0)


def _shift_up(cur, next8, k):
    n = cur.shape[0]
    r = pltpu.roll(cur, n - k, 0)
    nr = pltpu.roll(next8, SUBLANES - k, 0)
    tail = jnp.where(_iota2(nr.shape, 0) >= SUBLANES - k, nr, r[n - SUBLANES:])
    if n == SUBLANES:
        return tail
    return jnp.concatenate([r[:n - SUBLANES], tail], axis=0)


def _causal_conv(cur, prev8, w_ref, taps):
    out = cur * w_ref[taps - 1:taps, :]
    for j in range(taps - 1):
        out = out + _shift_down(cur, prev8, taps - 1 - j) * w_ref[j:j + 1, :]
    return out


def _anti_conv(cur, next8, w_ref, taps):
    out = cur * w_ref[taps - 1:taps, :]
    for j in range(taps - 1):
        out = out + _shift_up(cur, next8, taps - 1 - j) * w_ref[j:j + 1, :]
    return out


def _matmul(a, b, mode, out_dtype, name, tm_cap=1024, tn_cap=1024, tk_cap=2048, after=None,
            a_cut=False, b_cut=False, out_cut=False):
    a_shard = a.shape[2] if a_cut else None
    b_shard = b.shape[2] if b_cut else None
    a_full = (a.shape[1], a.shape[0] * a_shard) if a_cut else a.shape
    b_full = (b.shape[1], b.shape[0] * b_shard) if b_cut else b.shape
    assert not (a_cut and mode == "tn")
    if mode == "nn":
        (M, K), N = a_full, b_full[1]
    elif mode == "nt":
        (M, K), N = a_full, b_full[0]
    else:
        (K, M), N = a_full, b_full[1]
    n_unit = b_shard if (b_cut and mode != "nt") else N
    k_unit = math.gcd(a_shard or K, b_shard if (b_cut and mode == "nt") else K)
    tm = _tile(M, tm_cap, LANES)
    tn = N // N_DEV if out_cut else _tile(n_unit, tn_cap, LANES)
    tk = _tile(k_unit, tk_cap, LANES)
    assert n_unit % tn == 0 and k_unit % tk == 0
    nk = K // tk
    dims = {"nn": _NN, "nt": _NT, "tn": _TN}[mode]
    if a_cut:
        pa = a_shard // tk
        a_spec = pl.BlockSpec((None, tm, tk), lambda i, j, k: (k // pa, i, k % pa))
    elif mode == "tn":
        a_spec = pl.BlockSpec((tk, tm), lambda i, j, k: (k, i))
    else:
        a_spec = pl.BlockSpec((tm, tk), lambda i, j, k: (i, k))
    if b_cut and mode == "nt":
        pb = b_shard // tk
        b_spec = pl.BlockSpec((None, tn, tk), lambda i, j, k: (k // pb, j, k % pb))
    elif b_cut:
        pb = b_shard // tn
        b_spec = pl.BlockSpec((None, tk, tn), lambda i, j, k: (j // pb, k, j % pb))
    elif mode == "nt":
        b_spec = pl.BlockSpec((tn, tk), lambda i, j, k: (j, k))
    else:
        b_spec = pl.BlockSpec((tk, tn), lambda i, j, k: (k, j))
    if out_cut:
        out_spec, out_shape = pl.BlockSpec((None, tm, tn), lambda i, j, k: (j, i, 0)), (N_DEV, M, tn)
    else:
        out_spec, out_shape = pl.BlockSpec((tm, tn), lambda i, j, k: (i, j)), (M, N)

    def body(a_ref, b_ref, *rest):
        if nk == 1:
            rest[-1][...] = lax.dot_general(a_ref[...], b_ref[...], dims, preferred_element_type=F32).astype(rest[-1].dtype)
            return
        o_ref, acc_ref = rest[-2:]
        k = pl.program_id(2)

        @pl.when(k == 0)
        def _():
            acc_ref[...] = jnp.zeros_like(acc_ref)

        acc_ref[...] += lax.dot_general(a_ref[...], b_ref[...], dims, preferred_element_type=F32)

        @pl.when(k == nk - 1)
        def _():
            o_ref[...] = acc_ref[...].astype(o_ref.dtype)

    vmem = 2 * (_nbytes((tm, tk), a.dtype) + _nbytes((tk, tn), b.dtype) + _nbytes((tm, tn), out_dtype)) + _nbytes((tm, tn), F32)
    vmem += _nbytes((tm, tn), F32) + (2 * _nbytes((tm, tk), a.dtype) if mode == "tn" else 0)
    tokens = [] if after is None else [after]
    return pl.pallas_call(
        body, name=name, grid=(M // tm, N // tn, nk),
        in_specs=[a_spec, b_spec] + [pl.BlockSpec(t.shape, lambda i, j, k: (0, 0)) for t in tokens],
        out_specs=out_spec,
        out_shape=jax.ShapeDtypeStruct(out_shape, out_dtype),
        scratch_shapes=[] if nk == 1 else [pltpu.VMEM((tm, tn), F32)],
        compiler_params=_params(("parallel", "parallel", "arbitrary"), vmem + 4 * 2**20),
    )(*_hbm(a, b), *tokens)


def _row_call(body, name, T, D, ins, outs, tr, acc_outs=()):
    def spec(a, kind):
        if kind == "row":
            return pl.BlockSpec((tr, a.shape[1]), lambda i: (i, 0))
        return pl.BlockSpec(a.shape, lambda i: (0, 0))
    in_specs = [spec(a, k) for a, k in ins]
    out_specs = [spec(a, k) for a, k in outs] + [spec(a, "vec") for a in acc_outs]
    out_shape = [a for a, _ in outs] + list(acc_outs)
    vmem = 2 * sum(_nbytes((tr, a.shape[1]) if k == "row" else a.shape, a.dtype) for a, k in list(ins) + list(outs))
    return pl.pallas_call(
        body, name=name, grid=(T // tr,), in_specs=in_specs, out_specs=out_specs, out_shape=out_shape,
        compiler_params=_params(("arbitrary",), 3 * vmem + 8 * 2**20),
    )(*_hbm(*[a for a, _ in ins]))


def _sds(shape, dtype):
    return jax.ShapeDtypeStruct(shape, dtype)


def _accumulate(ref, val):
    @pl.when(pl.program_id(0) == 0)
    def _():
        ref[...] = jnp.zeros_like(ref)
    ref[...] += val


def _norm_in(x, g):
    T, D = x.shape

    def body(x_ref, g_ref, o_ref):
        o_ref[...] = _rms(x_ref[...], g_ref[...]).astype(BF16)

    return _row_call(body, "norm_in", T, D, [(x, "row"), (g, "vec")], [(_sds((T, D), BF16), "row")], _tile(T, 256, 16))[0]


def _mix_residual(x, m, g_post, g_pre):
    T, D = x.shape

    def body(x_ref, m_ref, gp_ref, gn_ref, h_ref, hn_ref):
        h = x_ref[...] + _rms(m_ref[...], gp_ref[...])
        h_ref[...] = h
        hn_ref[...] = _rms(h, gn_ref[...]).astype(BF16)

    return _row_call(body, "mix_residual", T, D, [(x, "row"), (m, "row"), (g_post, "vec"), (g_pre, "vec")],
                     [(_sds((T, D), F32), "row"), (_sds((T, D), BF16), "row")], _tile(T, 256, 16))


def _loss_head(h, f, g_post, target):
    T, D = h.shape

    def body(h_ref, f_ref, g_ref, t_ref, dy_ref, df_ref, dg_ref, loss_ref):
        f = f_ref[...]
        g = g_ref[...]
        diff = h_ref[...] + _rms(f, g) - t_ref[...]
        dy = diff * (1.0 / D)
        dy_ref[...] = dy
        df, dg = _rms_bwd(dy, f, g)
        df_ref[...] = df.astype(BF16)
        _accumulate(dg_ref, dg)
        _accumulate(loss_ref, jnp.full((1, LANES), 0.5 / D, F32) * jnp.sum(diff * diff))

    return _row_call(body, "loss_head", T, D, [(h, "row"), (f, "row"), (g_post, "vec"), (target, "row")],
                     [(_sds((T, D), F32), "row"), (_sds((T, D), BF16), "row")], _tile(T, 256, 16),
                     acc_outs=[_sds((1, D), F32), _sds((1, LANES), F32)])


def _ffn_residual_bwd(dy, dhn, h, g_pre, m, g_post):
    T, D = h.shape

    def body(dy_ref, dhn_ref, h_ref, gn_ref, m_ref, gp_ref, dh_ref, dm_ref, dgn_ref, dgp_ref):
        dhh, dgn = _rms_bwd(dhn_ref[...], h_ref[...], gn_ref[...])
        dh = dy_ref[...] + dhh
        dh_ref[...] = dh
        dm, dgp = _rms_bwd(dh, m_ref[...], gp_ref[...])
        dm_ref[...] = dm.astype(BF16)
        _accumulate(dgn_ref, dgn)
        _accumulate(dgp_ref, dgp)

    return _row_call(body, "ffn_residual_bwd", T, D,
                     [(dy, "row"), (dhn, "row"), (h, "row"), (g_pre, "vec"), (m, "row"), (g_post, "vec")],
                     [(_sds((T, D), F32), "row"), (_sds((T, D), BF16), "row")], _tile(T, 128, 16),
                     acc_outs=[_sds((1, D), F32), _sds((1, D), F32)])


def _input_bwd(dh, dxn, x, g):
    T, D = x.shape

    def body(dh_ref, dxn_ref, x_ref, g_ref, dx_ref, dg_ref):
        dx, dg = _rms_bwd(dxn_ref[...], x_ref[...], g_ref[...])
        dx_ref[...] = dh_ref[...] + dx
        _accumulate(dg_ref, dg)

    return _row_call(body, "input_bwd", T, D, [(dh, "row"), (dxn, "row"), (x, "row"), (g, "vec")],
                     [(_sds((T, D), F32), "row")], _tile(T, 256, 16), acc_outs=[_sds((1, D), F32)])


def _ffn_act(u, conv_w, conv_b):
    T, F2 = u.shape
    F = F2 // 2
    tc = _tile(F, 512, LANES)
    tr = _tile(T, 1024, SUBLANES)
    nc = F // tc
    r8 = tr // SUBLANES

    def body(ug_ref, ugp_ref, uv_ref, uvp_ref, wg_ref, wv_ref, bg_ref, bv_ref, a_ref):
        first = pl.program_id(1) == 0
        cg = _causal_conv(ug_ref[...], jnp.where(first, 0.0, ugp_ref[...]), wg_ref, FFN_CONV) + bg_ref[...]
        cv = _causal_conv(uv_ref[...], jnp.where(first, 0.0, uvp_ref[...]), wv_ref, FFN_CONV) + bv_ref[...]
        a_ref[...] = (_gelu(cg) * cv).astype(BF16)

    cur = lambda off: pl.BlockSpec((tr, tc), lambda j, i: (i, j + off))
    prev = lambda off: pl.BlockSpec((SUBLANES, tc), lambda j, i: (jnp.maximum(i * r8 - 1, 0), j + off))
    wsp = lambda off: pl.BlockSpec((FFN_CONV, tc), lambda j, i: (0, j + off))
    bsp = lambda off: pl.BlockSpec((1, tc), lambda j, i: (0, j + off))
    return pl.pallas_call(
        body, name="ffn_act", grid=(nc, T // tr),
        in_specs=[cur(0), prev(0), cur(nc), prev(nc), wsp(0), wsp(nc), bsp(0), bsp(nc)],
        out_specs=pl.BlockSpec((tr, tc), lambda j, i: (i, j)),
        out_shape=_sds((T, F), BF16),
        compiler_params=_params(("parallel", "arbitrary"), 12 * _nbytes((tr, tc), F32) + 8 * 2**20),
    )(*_hbm(u, u, u, u, conv_w, conv_w, conv_b, conv_b))


def _ffn_act_bwd(u, conv_w, conv_b, da):
    T, F2 = u.shape
    F = F2 // 2
    tc = _tile(F, 512, LANES)
    tr = _tile(T, 512, SUBLANES)
    nc = F // tc
    r8 = tr // SUBLANES
    n8 = T // SUBLANES
    K = FFN_CONV

    def body(ug_ref, ugp_ref, ugn_ref, uv_ref, uvp_ref, uvn_ref, da_ref, dan_ref,
             wg_ref, wv_ref, bg_ref, bv_ref, du_ref, dwb_ref):
        i = pl.program_id(1)
        first = i == 0
        last = i == pl.num_programs(1) - 1

        def dconv(ug, ug_prev, uv, uv_prev, da_):
            cg = _causal_conv(ug, ug_prev, wg_ref, K) + bg_ref[...]
            cv = _causal_conv(uv, uv_prev, wv_ref, K) + bv_ref[...]
            act, act_grad = _gelu_and_grad(cg)
            return da_ * cv * act_grad, da_ * act

        ug, uv = ug_ref[...], uv_ref[...]
        ug_prev, uv_prev = jnp.where(first, 0.0, ugp_ref[...]), jnp.where(first, 0.0, uvp_ref[...])
        dcg, dcv = dconv(ug, ug_prev, uv, uv_prev, da_ref[...])
        dcgn, dcvn = dconv(ugn_ref[...], ug[tr - SUBLANES:], uvn_ref[...], uv[tr - SUBLANES:], dan_ref[...])
        du_ref[0] = _anti_conv(dcg, jnp.where(last, 0.0, dcgn), wg_ref, K).astype(BF16)
        du_ref[1] = _anti_conv(dcv, jnp.where(last, 0.0, dcvn), wv_ref, K).astype(BF16)

        @pl.when(first)
        def _():
            dwb_ref[...] = jnp.zeros_like(dwb_ref)

        for half, (dc, uo, uo_prev) in enumerate([(dcg, ug, ug_prev), (dcv, uv, uv_prev)]):
            rows = [jnp.sum(dc * _shift_down(uo, uo_prev, K - 1 - t), axis=0, keepdims=True) for t in range(K - 1)]
            rows += [jnp.sum(dc * uo, axis=0, keepdims=True), jnp.sum(dc, axis=0, keepdims=True)]
            rows += [jnp.zeros_like(rows[0])] * (SUBLANES - len(rows))
            dwb_ref[half] += jnp.concatenate(rows, axis=0)

    cur = lambda off: pl.BlockSpec((tr, tc), lambda j, i: (i, j + off))
    prev = lambda off: pl.BlockSpec((SUBLANES, tc), lambda j, i: (jnp.maximum(i * r8 - 1, 0), j + off))
    nxt = lambda off: pl.BlockSpec((SUBLANES, tc), lambda j, i: (jnp.minimum((i + 1) * r8, n8 - 1), j + off))
    wsp = lambda off: pl.BlockSpec((K, tc), lambda j, i: (0, j + off))
    bsp = lambda off: pl.BlockSpec((1, tc), lambda j, i: (0, j + off))
    return pl.pallas_call(
        body, name="ffn_act_bwd", grid=(nc, T // tr),
        in_specs=[cur(0), prev(0), nxt(0), cur(nc), prev(nc), nxt(nc), cur(0), nxt(0), wsp(0), wsp(nc), bsp(0), bsp(nc)],
        out_specs=[pl.BlockSpec((2, tr, tc), lambda j, i: (0, i, j)), pl.BlockSpec((2, SUBLANES, tc), lambda j, i: (0, 0, j))],
        out_shape=[_sds((2, T, F), BF16), _sds((2, SUBLANES, F), F32)],
        compiler_params=_params(("parallel", "arbitrary"), 24 * _nbytes((tr, tc), F32) + 8 * 2**20),
    )(*_hbm(u, u, u, u, u, u, da, da, conv_w, conv_w, conv_b, conv_b))


def _l2norm(s, scale):
    return s * (lax.rsqrt(jnp.sum(s * s, axis=-1, keepdims=True) + EPS) * scale)


def _dn_branch(proj, col0, conv_w, wcol0, l2, scale):
    T = proj.shape[0]
    W = DN_HEADS * HEAD_DIM
    tr = _tile(T, 2048, SUBLANES)
    r8 = tr // SUBLANES
    cb0, wb0 = col0 // HEAD_DIM, wcol0 // HEAD_DIM

    def body(u_ref, up_ref, w_ref, o_ref):
        first = pl.program_id(1) == 0
        s = _silu(_causal_conv(u_ref[...], jnp.where(first, 0.0, up_ref[...]), w_ref, SHORT_CONV))
        o_ref[...] = _l2norm(s, scale) if l2 else s

    return pl.pallas_call(
        body, name=f"dn_branch_{col0}", grid=(DN_HEADS, T // tr),
        in_specs=[pl.BlockSpec((tr, HEAD_DIM), lambda h, i: (i, cb0 + h)),
                  pl.BlockSpec((SUBLANES, HEAD_DIM), lambda h, i: (jnp.maximum(i * r8 - 1, 0), cb0 + h)),
                  pl.BlockSpec((SHORT_CONV, HEAD_DIM), lambda h, i: (0, wb0 + h))],
        out_specs=pl.BlockSpec((tr, HEAD_DIM), lambda h, i: (i, h)),
        out_shape=_sds((T, W), F32),
        compiler_params=_params(("parallel", "arbitrary"), 32 * _nbytes((tr, HEAD_DIM), F32) + 8 * 2**20),
    )(*_hbm(proj, proj, conv_w))


def _dn_branch_bwd(proj, col0, conv_w, wcol0, l2, scale, dy):
    T = proj.shape[0]
    W = DN_HEADS * HEAD_DIM
    tr = _tile(T, 2048, SUBLANES)
    r8 = tr // SUBLANES
    n8 = T // SUBLANES
    cb0, wb0 = col0 // HEAD_DIM, wcol0 // HEAD_DIM
    K = SHORT_CONV

    def body(u_ref, up_ref, un_ref, dy_ref, dyn_ref, w_ref, du_ref, dw_ref):
        i = pl.program_id(1)
        first = i == 0
        last = i == pl.num_programs(1) - 1

        def dconv(u, u_prev, dy_):
            c = _causal_conv(u, u_prev, w_ref, K)
            if l2:
                s = _silu(c)
                r = lax.rsqrt(jnp.sum(s * s, axis=-1, keepdims=True) + EPS)
                n = s * r
                ds = (scale * r) * (dy_ - n * jnp.sum(dy_ * n, axis=-1, keepdims=True))
            else:
                ds = dy_
            return ds * _silu_grad(c)

        u = u_ref[...]
        u_prev = jnp.where(first, 0.0, up_ref[...])
        dc = dconv(u, u_prev, dy_ref[...])
        dcn = jnp.where(last, 0.0, dconv(un_ref[...], u[tr - SUBLANES:], dyn_ref[...]))
        du_ref[...] = _anti_conv(dc, dcn, w_ref, K).astype(BF16)
        rows = [jnp.sum(dc * _shift_down(u, u_prev, K - 1 - t), axis=0, keepdims=True) for t in range(K - 1)]
        rows += [jnp.sum(dc * u, axis=0, keepdims=True)]
        rows += [jnp.zeros_like(rows[0])] * (SUBLANES - len(rows))
        upd = jnp.concatenate(rows, axis=0)

        @pl.when(first)
        def _():
            dw_ref[...] = jnp.zeros_like(dw_ref)
        dw_ref[...] += upd

    return pl.pallas_call(
        body, name=f"dn_branch_bwd_{col0}", grid=(DN_HEADS, T // tr),
        in_specs=[pl.BlockSpec((tr, HEAD_DIM), lambda h, i: (i, cb0 + h)),
                  pl.BlockSpec((SUBLANES, HEAD_DIM), lambda h, i: (jnp.maximum(i * r8 - 1, 0), cb0 + h)),
                  pl.BlockSpec((SUBLANES, HEAD_DIM), lambda h, i: (jnp.minimum((i + 1) * r8, n8 - 1), cb0 + h)),
                  pl.BlockSpec((tr, HEAD_DIM), lambda h, i: (i, h)),
                  pl.BlockSpec((SUBLANES, HEAD_DIM), lambda h, i: (jnp.minimum((i + 1) * r8, n8 - 1), h)),
                  pl.BlockSpec((K, HEAD_DIM), lambda h, i: (0, wb0 + h))],
        out_specs=[pl.BlockSpec((tr, HEAD_DIM), lambda h, i: (i, h)),
                   pl.BlockSpec((SUBLANES, HEAD_DIM), lambda h, i: (0, h))],
        out_shape=[_sds((T, W), BF16), _sds((SUBLANES, W), F32)],
        compiler_params=_params(("parallel", "arbitrary"), 32 * _nbytes((tr, HEAD_DIM), F32) + 8 * 2**20),
    )(*_hbm(proj, proj, proj, dy, dy, conv_w))


def _lane_masks(shape):
    lane = _iota2(shape, 1)
    return lane < DN_HEADS, (lane >= DN_HEADS) & (lane < 2 * DN_HEADS)


def _expand01(off):
    r = _iota2((LANES, DN_HEADS * HEAD_DIM), 0)
    c = _iota2((LANES, DN_HEADS * HEAD_DIM), 1)
    return (r == jnp.right_shift(c, int(math.log2(HEAD_DIM))) + off).astype(BF16)


def _select01(off):
    r = _iota2((DN_HEADS * HEAD_DIM, LANES), 0)
    c = _iota2((DN_HEADS * HEAD_DIM, LANES), 1)
    return (r == (c - off) * HEAD_DIM).astype(BF16)


def _dn_gates(proj, gate_block, a_log_l, dt_bias_l):
    T = proj.shape[0]
    C = DN_CHUNK
    W = DN_HEADS * HEAD_DIM

    def body(ba_ref, al_ref, dt_ref, gc_ref, beta_ref):
        ba = ba_ref[...]
        is_b, is_a = _lane_masks(ba.shape)
        g = jnp.where(is_a, -jnp.exp(al_ref[...]) * _softplus(ba + dt_ref[...]), 0.0)
        beta = jnp.where(is_b, _sigmoid(ba), 0.0)
        tri = (_iota2((C, C), 0) >= _iota2((C, C), 1)).astype(BF16)
        gc = _dot01_left(tri, g)
        gc_ref[...] = _dot01(gc, _expand01(DN_HEADS))
        beta_ref[...] = _dot01(beta, _expand01(0))

    vec = pl.BlockSpec((1, LANES), lambda n: (0, 0))
    return pl.pallas_call(
        body, name="dn_gates", grid=(T // C,),
        in_specs=[pl.BlockSpec((C, LANES), lambda n: (n, gate_block)), vec, vec],
        out_specs=[pl.BlockSpec((C, W), lambda n: (n, 0))] * 2,
        out_shape=[_sds((T, W), F32)] * 2,
        compiler_params=_params(("parallel",), 16 * 2**20),
    )(*_hbm(proj, a_log_l, dt_bias_l))


def _dn_gates_bwd(proj, gate_block, a_log_l, dt_bias_l, dgc_full, dbeta_full):
    T = proj.shape[0]
    C = DN_CHUNK
    W = DN_HEADS * HEAD_DIM

    def body(ba_ref, al_ref, dt_ref, dgc_ref, dbeta_ref, dba_ref, dal_ref, ddt_ref):
        ba = ba_ref[...]
        is_b, is_a = _lane_masks(ba.shape)
        ea = jnp.exp(al_ref[...])
        pre = ba + dt_ref[...]
        g = jnp.where(is_a, -ea * _softplus(pre), 0.0)
        beta = _sigmoid(ba)
        dgc = _dot01(dgc_ref[...], _select01(DN_HEADS))
        dbeta = _dot01(dbeta_ref[...], _select01(0))
        triu = (_iota2((C, C), 0) <= _iota2((C, C), 1)).astype(BF16)
        dg = _dot01_left(triu, dgc)
        da = jnp.where(is_a, dg * (-ea) * _sigmoid(pre), 0.0)
        dba_ref[...] = (da + jnp.where(is_b, dbeta * beta * (1.0 - beta), 0.0)).astype(BF16)
        _accumulate(dal_ref, jnp.sum(dg * g, axis=0, keepdims=True))
        _accumulate(ddt_ref, jnp.sum(da, axis=0, keepdims=True))

    vec = pl.BlockSpec((1, LANES), lambda n: (0, 0))
    full = pl.BlockSpec((C, W), lambda n: (n, 0))
    return pl.pallas_call(
        body, name="dn_gates_bwd", grid=(T // C,),
        in_specs=[pl.BlockSpec((C, LANES), lambda n: (n, gate_block)), vec, vec, full, full],
        out_specs=[pl.BlockSpec((C, LANES), lambda n: (n, 0)), vec, vec],
        out_shape=[_sds((T, LANES), BF16), _sds((1, LANES), F32), _sds((1, LANES), F32)],
        compiler_params=_params(("arbitrary",), 16 * 2**20),
    )(*_hbm(proj, a_log_l, dt_bias_l, dgc_full, dbeta_full))


def _unit_lower_inverse(L):
    C = L.shape[-1]
    row, col = _iota2((C, C), 0), _iota2((C, C), 1)
    eye = (row == col).astype(F32)
    sh = int(math.log2(INV_BLOCK))
    Ld = jnp.where(jnp.right_shift(row, sh) == jnp.right_shift(col, sh), L, 0.0)
    Lo = L - Ld
    X = eye - Ld
    P = Ld
    for _ in range(int(math.log2(INV_BLOCK)) - 1):
        P = _mm3(P, P)
        X = X + _mm3(X, P)
    N = _mm3(X, Lo)
    Y = eye - N
    P = N
    for _ in range(int(math.log2(C // INV_BLOCK)) - 1):
        P = _mm3(P, P)
        Y = Y + _mm3(Y, P)
    return _mm3(Y, X)


def _dn_chunk_common(q, k, v, gc, beta, gl):
    C = q.shape[-2]
    row, col = _iota2((C, C), 0), _iota2((C, C), 1)
    causal, strict = row >= col, row > col
    eg = jnp.exp(gc)
    decay = jnp.where(causal, jnp.exp(jnp.where(causal, gc - _t(gc), 0.0)), 0.0)
    kb, vb = k * beta, v * beta
    L = jnp.where(strict, _dot(kb, k, _NT) * decay, 0.0)
    Aqk = jnp.where(causal, _dot(q, k, _NT) * decay, 0.0)
    ektg = jnp.exp(gl - gc)
    return dict(causal=causal, strict=strict, eg=eg, decay=decay, kb=kb, vb=vb, L=L, Aqk=Aqk, ektg=ektg,
                kbg=kb * eg, kte=k * ektg, qd=q * eg, egl=jnp.exp(gl))


def _dn_scan(qn, kn, vn, gc_full, beta_full, proj, z_col0, gain):
    T, W = qn.shape
    C = DN_CHUNK
    N = T // C
    H = DN_HEADS
    G = DN_GROUP
    GW = G * HEAD_DIM
    zb0 = z_col0 // GW

    def body(q_ref, k_ref, v_ref, gc_ref, beta_ref, z_ref, gain_ref, o_ref, mix_ref, tm_ref, s_ref, S):
        @pl.when(pl.program_id(1) == 0)
        def _():
            S[...] = jnp.zeros_like(S)

        heads = lambda ref, rows=slice(None): jnp.stack([ref[rows, g * HEAD_DIM:(g + 1) * HEAD_DIM] for g in range(G)])
        q, k, v, gc, beta = heads(q_ref), heads(k_ref), heads(v_ref), heads(gc_ref), heads(beta_ref)
        gl = heads(gc_ref, slice(C - 1, C))
        c = _dn_chunk_common(q, k, v, gc, beta, gl)
        Tm = _unit_lower_inverse(c["L"])
        u = _dot(Tm, c["vb"])
        w = _dot(Tm, c["kbg"])
        S0 = S[...]
        vnew = u - _dot(w, S0)
        o = _dot(c["qd"], S0) + _dot(c["Aqk"], vnew)
        S[...] = S0 * c["egl"] + _dot(c["kte"], vnew, _TN)
        tm_ref[...] = Tm
        s_ref[...] = S0
        mix = (_rms(o, gain_ref[...]) * _silu(heads(z_ref))).astype(BF16)
        for g in range(G):
            sl = slice(g * HEAD_DIM, (g + 1) * HEAD_DIM)
            o_ref[:, sl] = o[g]
            mix_ref[:, sl] = mix[g]

    blk = pl.BlockSpec((C, GW), lambda h, n: (n, h))
    mat = pl.BlockSpec((G, None, C, C), lambda h, n: (h, n, 0, 0))
    return pl.pallas_call(
        body, name="dn_scan", grid=(H // G, N),
        in_specs=[blk, blk, blk, blk, blk, pl.BlockSpec((C, GW), lambda h, n: (n, zb0 + h)),
                  pl.BlockSpec((1, HEAD_DIM), lambda h, n: (0, 0))],
        out_specs=[blk, blk, mat, mat],
        out_shape=[_sds((T, W), F32), _sds((T, W), BF16), _sds((H, N, C, C), F32), _sds((H, N, C, C), F32)],
        scratch_shapes=[pltpu.VMEM((G, HEAD_DIM, HEAD_DIM), F32)],
        compiler_params=_params(("parallel", "arbitrary"), 32 * 2**20),
    )(*_hbm(qn, kn, vn, gc_full, beta_full, proj, gain))


def _dn_scan_bwd(qn, kn, vn, gc_full, beta_full, proj, z_col0, gain, o_raw, tm_all, s_all, dmix, dmix_col0):
    T, W = qn.shape
    C = DN_CHUNK
    N = T // C
    H = DN_HEADS
    G = DN_GROUP
    GW = G * HEAD_DIM
    zb0 = z_col0 // GW
    mb0 = dmix_col0 // GW

    def body(q_ref, k_ref, v_ref, gc_ref, beta_ref, z_ref, gain_ref, o_ref, tm_ref, s_ref, dmix_ref,
             dq_ref, dk_ref, dv_ref, dgc_ref, dbeta_ref, dz_ref, dgain_ref, dS):
        @pl.when(pl.program_id(1) == 0)
        def _():
            dS[...] = jnp.zeros_like(dS)

        @pl.when((pl.program_id(0) == 0) & (pl.program_id(1) == 0))
        def _():
            dgain_ref[...] = jnp.zeros_like(dgain_ref)

        heads = lambda ref, rows=slice(None): jnp.stack([ref[rows, g * HEAD_DIM:(g + 1) * HEAD_DIM] for g in range(G)])
        total = lambda x: jnp.sum(jnp.sum(x, axis=-1, keepdims=True), axis=-2, keepdims=True)
        gain = gain_ref[...]
        o, z, dmix = heads(o_ref), heads(z_ref), heads(dmix_ref)
        dz = (dmix * _rms(o, gain) * _silu_grad(z)).astype(BF16)
        do, dgain = _rms_bwd(dmix * _silu(z), o, gain)
        dgain_ref[...] += jnp.sum(dgain, axis=0)

        q, k, v, gc, beta = heads(q_ref), heads(k_ref), heads(v_ref), heads(gc_ref), heads(beta_ref)
        gl = heads(gc_ref, slice(C - 1, C))
        c = _dn_chunk_common(q, k, v, gc, beta, gl)
        Tm, S0, dS1 = tm_ref[...], s_ref[...], dS[...]
        w = _dot(Tm, c["kbg"])
        vnew = _dot(Tm, c["vb"]) - _dot(w, S0)

        dvnew = _dot(c["Aqk"], do, _TN) + _dot(c["kte"], dS1)
        dAqk = jnp.where(c["causal"], _dot(do, vnew, _NT), 0.0)
        dqd = _dot(do, S0, _NT)
        dkte = _dot(vnew, dS1, _NT)
        dgl = total(dS1 * S0) * c["egl"]
        dw = -_dot(dvnew, S0, _NT)
        dS[...] = dS1 * c["egl"] + _dot(c["qd"], do, _TN) - _dot(w, dvnew, _TN)

        dTm = _dot(dvnew, c["vb"], _NT) + _dot(dw, c["kbg"], _NT)
        dvb = _dot(Tm, dvnew, _TN)
        dkbg = _dot(Tm, dw, _TN)
        dL = jnp.where(c["strict"], -_mm3(_mm3(Tm, dTm, _TN), Tm, _NT), 0.0)
        dP = dL * c["decay"]
        dQ = dAqk * c["decay"]
        M = dL * c["L"] + dAqk * c["Aqk"]
        dkb = _dot(dP, k) + dkbg * c["eg"]
        dk = _dot(dP, c["kb"], _TN) + _dot(dQ, q, _TN) + dkte * c["ektg"] + dkb * beta
        dq = _dot(dQ, k) + dqd * c["eg"]
        tk = _rowsum(dkte * c["kte"])
        dgc = (_rowsum(M) - _rowsum(_t(M)) + _rowsum(dqd * c["qd"]) - tk + _rowsum(dkbg * c["kbg"]))
        dgl = dgl + total(tk)
        dgc = jnp.broadcast_to(dgc, q.shape) + jnp.where(_iota2((C, HEAD_DIM), 0) == C - 1, dgl, 0.0)
        dv = dvb * beta
        dbeta = jnp.broadcast_to(_rowsum(dkb * k) + _rowsum(dvb * v), q.shape)
        for g in range(G):
            sl = slice(g * HEAD_DIM, (g + 1) * HEAD_DIM)
            dz_ref[:, sl] = dz[g]
            dq_ref[:, sl] = dq[g]
            dk_ref[:, sl] = dk[g]
            dv_ref[:, sl] = dv[g]
            dgc_ref[:, sl] = dgc[g]
            dbeta_ref[:, sl] = dbeta[g]

    rev = lambda off: pl.BlockSpec((C, GW), lambda h, n: (N - 1 - n, off + h))
    mat = pl.BlockSpec((G, None, C, C), lambda h, n: (h, N - 1 - n, 0, 0))
    vec = pl.BlockSpec((1, HEAD_DIM), lambda h, n: (0, 0))
    return pl.pallas_call(
        body, name="dn_scan_bwd", grid=(H // G, N),
        in_specs=[rev(0), rev(0), rev(0), rev(0), rev(0), rev(zb0), vec, rev(0), mat, mat, rev(mb0)],
        out_specs=[rev(0)] * 6 + [vec],
        out_shape=[_sds((T, W), F32)] * 5 + [_sds((T, W), BF16), _sds((1, HEAD_DIM), F32)],
        scratch_shapes=[pltpu.VMEM((G, HEAD_DIM, HEAD_DIM), F32)],
        compiler_params=_params(("arbitrary", "arbitrary"), 40 * 2**20),
    )(*_hbm(qn, kn, vn, gc_full, beta_full, proj, gain, o_raw, tm_all, s_all, dmix))


def _sb_terms(z, ahead, first_key):
    lb = jnp.minimum(z, 0.0) - jnp.log(1.0 + jnp.exp(-jnp.abs(z)))
    if ahead is None:
        return None, lb, lb - z
    valid = ahead < -first_key
    return valid, lb, jnp.where(valid, lb - z, 0.0)


def _masked(valid, x):
    return x if valid is None else jnp.where(valid, x, 0.0)


def _sb_attention(qkv, gain, tq_cap=2048):
    T = qkv.shape[0]
    H = SB_HEADS
    B = min(SB_KEYS, T)
    TQ = _tile(T, tq_cap, B)
    per = TQ // B

    assert per % 2 == 0
    n_saved = per * (T // TQ) * (T // TQ + 1) // 2

    def body(q_ref, k_ref, v_ref, gain_ref, o_ref, mix_ref, att_hbm, lb_hbm, att_buf, lb_buf, sems):
        h, i = pl.program_id(0), pl.program_id(1)
        q = q_ref[...].astype(BF16)
        upper = (_iota2((B, B), 0) > _iota2((B, B), 1)).astype(BF16)
        ahead = _iota2((TQ, B), 1) - _iota2((TQ, B), 0)
        last = (i + 1) * per - 1
        base = per * (i * (i + 1) // 2)

        def save(slot, pair):
            return (pltpu.make_async_copy(att_buf.at[slot], att_hbm.at[h, pair], sems.at[0, slot]),
                    pltpu.make_async_copy(lb_buf.at[slot], lb_hbm.at[h, pair], sems.at[1, slot]))

        def pair(j, slot, r0, acc, R):
            n = TQ - (r0 or 0)
            top = slice(r0 or 0, TQ)
            rows = pl.ds(pl.multiple_of(j * B, B), B)
            z = _dot(q[top], k_ref[rows, :], _NT) * (HEAD_DIM ** -0.5)
            valid, lb, l1m = _sb_terms(z, None if r0 is None else ahead[top], r0)
            att = _masked(valid, jnp.exp(lb + R[top] + _dot01(l1m, upper, passes=2))).astype(BF16)
            att_buf[slot, pl.ds(TQ - n, n), :] = att
            lb_buf[slot, pl.ds(TQ - n, n), :] = (lb if valid is None else jnp.where(valid, lb, -1e30)).astype(BF16)
            if r0:
                att_buf[slot, pl.ds(0, r0), :] = jnp.zeros((r0, B), BF16)
                lb_buf[slot, pl.ds(0, r0), :] = jnp.full((r0, B), -1e30, BF16)
            for c in save(slot, base + j):
                c.start()
            d_acc, d_R = _dot(att, v_ref[rows, :]), _rowsum(l1m)
            if r0:
                d_acc = jnp.concatenate([jnp.zeros((r0, HEAD_DIM), F32), d_acc], axis=0)
                d_R = jnp.concatenate([jnp.zeros((r0, 1), F32), d_R], axis=0)
            return acc + d_acc, R + d_R

        def step(jj, carry):
            slot = jj % 2

            @pl.when(jj >= 2)
            def _():
                for c in save(slot, 0):
                    c.wait()

            return pair(last - jj, slot, None, *carry)

        carry = (jnp.zeros((TQ, HEAD_DIM), F32), jnp.zeros((TQ, 1), F32))
        for jj in range(per):
            if jj >= 2:
                for c in save(jj % 2, 0):
                    c.wait()
            carry = pair(last - jj, jj % 2, (per - 1 - jj) * B, *carry)
        acc, _ = lax.fori_loop(per, last + 1, step, carry)
        for slot in range(2):
            for c in save(slot, 0):
                c.wait()
        o_ref[...] = acc
        mix_ref[...] = _rms(acc, gain_ref[...]).astype(BF16)

    head = lambda off: pl.BlockSpec((T, HEAD_DIM), lambda h, i: (0, off + h))
    blk = pl.BlockSpec((TQ, HEAD_DIM), lambda h, i: (i, h))
    return pl.pallas_call(
        body, name="sb_attention", grid=(H, T // TQ),
        in_specs=[blk, head(H), head(2 * H), pl.BlockSpec((1, HEAD_DIM), lambda h, i: (0, 0))],
        out_specs=[blk, blk, ANY_SPEC, ANY_SPEC],
        out_shape=[_sds((T, H * HEAD_DIM), F32), _sds((T, H * HEAD_DIM), BF16),
                   _sds((H, n_saved, TQ, B), BF16), _sds((H, n_saved, TQ, B), BF16)],
        scratch_shapes=[pltpu.VMEM((2, TQ, B), BF16), pltpu.VMEM((2, TQ, B), BF16), pltpu.SemaphoreType.DMA((2, 2))],
        compiler_params=_params(("parallel", "arbitrary"), 8 * _nbytes((T, HEAD_DIM), BF16) + 32 * _nbytes((TQ, B), F32)),
    )(*_hbm(qkv, qkv, qkv, gain))


def _sb_attention_bwd(qkv, gain, o_raw, att_all, lb_all, dmix):
    T = qkv.shape[0]
    H = SB_HEADS
    TQ, B = att_all.shape[2:]
    per = TQ // B
    scale = HEAD_DIM ** -0.5

    def body(q_ref, k_ref, v_ref, gain_ref, o_ref, dmix_ref, att_hbm, lb_hbm, dq_ref, dk_ref, dv_ref, dgain_ref,
             att_buf, lb_buf, sems):
        h, i = pl.program_id(0), pl.program_id(1)

        @pl.when(i == 0)
        def _():
            dk_ref[...] = jnp.zeros_like(dk_ref)
            dv_ref[...] = jnp.zeros_like(dv_ref)

        @pl.when((pl.program_id(0) == 0) & (i == 0))
        def _():
            dgain_ref[...] = jnp.zeros_like(dgain_ref)

        q = q_ref[...].astype(BF16)
        o = o_ref[...]
        do, dgain = _rms_bwd(dmix_ref[...], o, gain_ref[...])
        dgain_ref[...] += dgain
        do_b = do.astype(BF16)
        before = (_iota2((B, B), 0) < _iota2((B, B), 1)).astype(BF16)
        base = per * (i * (i + 1) // 2)

        def fetch(slot, pair):
            return (pltpu.make_async_copy(att_hbm.at[h, pair], att_buf.at[slot], sems.at[0, slot]),
                    pltpu.make_async_copy(lb_hbm.at[h, pair], lb_buf.at[slot], sems.at[1, slot]))

        for c in fetch(0, base):
            c.start()

        def pair(j, slot, r0, dq, PG):
            top = slice(r0, TQ)
            rows = pl.ds(pl.multiple_of(j * B, B), B)
            kj = k_ref[rows, :]
            att = att_buf[slot, pl.ds(r0, TQ - r0), :]
            sig = jnp.exp(lb_buf[slot, pl.ds(r0, TQ - r0), :].astype(F32))
            G = _dot(do_b[top], v_ref[rows, :], _NT) * att.astype(F32)
            dv_ref[rows, :] += _dot(att, do_b[top], _TN)
            cum = PG[top] + _dot01(G, before, passes=2)
            dz = (G * (1.0 - sig) - sig * cum) * scale
            dk_ref[rows, :] += _dot(dz, q[top], _TN)
            d_dq, d_PG = _dot(dz, kj), _rowsum(G)
            if r0:
                d_dq = jnp.concatenate([jnp.zeros((r0, HEAD_DIM), F32), d_dq], axis=0)
                d_PG = jnp.concatenate([jnp.zeros((r0, 1), F32), d_PG], axis=0)
            return dq + d_dq, PG + d_PG

        def step(j, carry):
            slot = j % 2
            for c in fetch(slot, 0):
                c.wait()
            for c in fetch(1 - slot, base + j + 1):
                c.start()
            return pair(j, slot, 0, *carry)

        carry = lax.fori_loop(0, i * per, step, (jnp.zeros((TQ, HEAD_DIM), F32), jnp.zeros((TQ, 1), F32)))
        for c_blk in range(per):
            slot = c_blk % 2
            for c in fetch(slot, 0):
                c.wait()
            if c_blk + 1 < per:
                for c in fetch(1 - slot, base + i * per + c_blk + 1):
                    c.start()
            carry = pair(i * per + c_blk, slot, c_blk * B, *carry)
        dq_ref[...] = carry[0].astype(BF16)

    head = lambda off: pl.BlockSpec((T, HEAD_DIM), lambda h, i: (0, off + h))
    blk = pl.BlockSpec((TQ, HEAD_DIM), lambda h, i: (i, h))
    vec = pl.BlockSpec((1, HEAD_DIM), lambda h, i: (0, 0))
    return pl.pallas_call(
        body, name="sb_attention_bwd", grid=(H, T // TQ),
        in_specs=[blk, head(H), head(2 * H), vec, blk, blk, ANY_SPEC, ANY_SPEC],
        out_specs=[blk, head(0), head(0), vec],
        out_shape=[_sds((T, H * HEAD_DIM), BF16), _sds((T, H * HEAD_DIM), F32), _sds((T, H * HEAD_DIM), F32),
                   _sds((1, HEAD_DIM), F32)],
        scratch_shapes=[pltpu.VMEM((2, TQ, B), BF16), pltpu.VMEM((2, TQ, B), BF16), pltpu.SemaphoreType.DMA((2, 2))],
        compiler_params=_params(("arbitrary", "arbitrary"), 8 * _nbytes((T, HEAD_DIM), F32) + 32 * _nbytes((TQ, B), F32)),
    )(*_hbm(qkv, qkv, qkv, gain, o_raw, dmix, att_all, lb_all))


def _adamw_math(w, g, m, v):
    m = ADAM_B1 * m + (1.0 - ADAM_B1) * g
    v = ADAM_B2 * v + (1.0 - ADAM_B2) * (g * g)
    m_hat = m / (1.0 - ADAM_B1 ** ADAM_STEP)
    v_hat = v / (1.0 - ADAM_B2 ** ADAM_STEP)
    delta = -ADAM_LR * (m_hat / (jnp.sqrt(v_hat) + ADAM_EPS) + ADAM_WD * w)
    return delta, m, v


def _adamw_sharded(parts, w, m, v, name):
    _, R, C = w.shape
    if R % SUBLANES == 0:
        tr, tc = _tile(R, max(SUBLANES, (2**20 // (4 * C)) // SUBLANES * SUBLANES), SUBLANES), C
    else:
        tr, tc = R, _tile(C, max(LANES, (2**20 // (4 * R)) // LANES * LANES), LANES)

    def body(p_ref, w_ref, m_ref, v_ref, g_ref, d_ref, nm_ref, nv_ref):
        g = p_ref[0].astype(F32)
        for d in range(1, N_DEV):
            g = g + p_ref[d].astype(F32)
        g_ref[...] = g
        d_ref[...], nm_ref[...], nv_ref[...] = _adamw_math(w_ref[...], g, m_ref[...], v_ref[...])

    blk = pl.BlockSpec((None, tr, tc), lambda i, j: (0, i, j))
    return pl.pallas_call(
        body, name=name, grid=(R // tr, C // tc),
        in_specs=[pl.BlockSpec((N_DEV, tr, tc), lambda i, j: (0, i, j)), blk, blk, blk],
        out_specs=[blk] * 4, out_shape=[_sds((1, R, C), F32)] * 4,
        compiler_params=_params(("parallel", "parallel"), 40 * 2**20),
    )(*_hbm(parts, w, m, v))


def _adamw_packed(g, wmv):
    R = g.shape[0]

    def body(g_ref, wmv_ref, d_ref, nm_ref, nv_ref):
        w, m, v = [wmv_ref[pl.ds(s * R, R), :] for s in range(3)]
        d_ref[...], nm_ref[...], nv_ref[...] = _adamw_math(w, g_ref[...], m, v)

    return pl.pallas_call(body, name="adamw_packed", out_shape=[_sds(g.shape, F32)] * 3,
                          compiler_params=_params((), 16 * 2**20))(g, wmv)


def _my_place():
    x, y, c = lax.axis_index("x"), lax.axis_index("y"), lax.axis_index("c")
    return x, y, c


def _peer(place, k):
    x, y, c = place
    return (1 - x if k & 4 else x, 1 - y if k & 2 else y, 1 - c if k & 1 else c)


def _index(place):
    x, y, c = place
    return 4 * x + 2 * y + c


HBM_SPEC = pl.BlockSpec(memory_space=pltpu.HBM)


def _all_gather(block, name):
    R, C = block.shape

    def body(x_ref, out_ref, send_sems, recv_sems, local_sem):
        me = _my_place()
        sibling = _peer(me, 1)
        chips = [2, 4, 6]

        def copy(sem, origin, to, src=None):
            slot = out_ref.at[_index(origin)]
            return pltpu.make_async_remote_copy(
                src_ref=slot if src is None else src, dst_ref=slot, send_sem=send_sems.at[sem], recv_sem=recv_sems.at[sem],
                device_id=to, device_id_type=MESH)

        mine = pltpu.make_async_copy(x_ref, out_ref.at[_index(me)], local_sem)
        mine.start()
        first = [copy(0, me, sibling, src=x_ref)] + [copy(1 + n, me, _peer(me, k), src=x_ref) for n, k in enumerate(chips)]
        for cp in first:
            cp.start()
        passed = [copy(4 + n, _peer(me, k), sibling) for n, k in enumerate(chips)]
        for n, k in enumerate(chips):
            copy(1 + n, _peer(me, k), me).wait_recv()
            passed[n].start()
        copy(0, sibling, me).wait_recv()
        for n, k in enumerate(chips):
            copy(4 + n, _peer(sibling, k), me).wait_recv()
        for cp in first + passed:
            cp.wait_send()
        mine.wait()

    return pl.pallas_call(
        body, name=name, in_specs=[HBM_SPEC], out_specs=HBM_SPEC,
        out_shape=_sds((N_DEV, R, C), block.dtype),
        scratch_shapes=[pltpu.SemaphoreType.DMA((7,)), pltpu.SemaphoreType.DMA((7,)), pltpu.SemaphoreType.DMA],
    )(block)


SEM_SPEC = pl.BlockSpec(memory_space=pltpu.SEMAPHORE)
ANY_SPEC = pl.BlockSpec(memory_space=pl.ANY)
_EFFECT = pltpu.SideEffectType.DATAFLOW_SIDE_EFFECTING


def _spread_start(x, per_peer, name, after):
    R, C = x.shape[-2:]

    def body(x_ref, land_ref, after_ref, send_sems, recv_sems, x_thru, land_thru, token):
        me = _my_place()
        for k in range(1, N_DEV):
            to = _peer(me, k)
            pltpu.make_async_remote_copy(
                src_ref=x_ref.at[_index(to)] if per_peer else x_ref, dst_ref=land_ref.at[_index(me)],
                send_sem=send_sems.at[k - 1], recv_sem=recv_sems.at[k - 1], device_id=to, device_id_type=MESH).start()
        token[...] = jnp.zeros_like(token)

    land = lax.empty((N_DEV, R, C), x.dtype)
    send_sems, recv_sems, x_thru, land_thru, token = pl.pallas_call(
        body, name=name,
        out_shape=(pltpu.SemaphoreType.DMA((N_DEV - 1,)), pltpu.SemaphoreType.DMA((N_DEV - 1,)),
                   pltpu.HBM(x.shape, x.dtype), pltpu.HBM(land.shape, land.dtype), _sds((SUBLANES, LANES), F32)),
        in_specs=(HBM_SPEC, HBM_SPEC, ANY_SPEC),
        out_specs=(SEM_SPEC, SEM_SPEC, HBM_SPEC, HBM_SPEC, pl.BlockSpec(memory_space=pltpu.VMEM)),
        input_output_aliases={0: 2, 1: 3},
        compiler_params=pltpu.CompilerParams(has_side_effects=_EFFECT),
    )(pltpu.with_memory_space_constraint(x, pltpu.HBM), pltpu.with_memory_space_constraint(land, pltpu.HBM), after)
    return (send_sems, recv_sems, x_thru, land_thru), token


def _spread_wait(state, per_peer, name, after):
    send_sems, recv_sems, x_thru, land_thru = state

    def body(x_ref, land_ref, send_sems, recv_sems, after_ref, x_dead, got_ref):
        me = _my_place()
        for k in range(1, N_DEV):
            frm = _peer(me, k)
            copy = pltpu.make_async_remote_copy(
                src_ref=x_ref.at[_index(frm)] if per_peer else x_ref, dst_ref=land_ref.at[_index(frm)],
                send_sem=send_sems.at[k - 1], recv_sem=recv_sems.at[k - 1], device_id=frm, device_id_type=MESH)
            copy.wait_send()
            copy.wait_recv()

    x_back, got = pl.pallas_call(
        body, name=name,
        out_shape=(pltpu.HBM(x_thru.shape, x_thru.dtype), pltpu.HBM(land_thru.shape, land_thru.dtype)),
        in_specs=(HBM_SPEC, HBM_SPEC, SEM_SPEC, SEM_SPEC, ANY_SPEC), out_specs=(HBM_SPEC, HBM_SPEC),
        input_output_aliases={0: 0, 1: 1},
        compiler_params=pltpu.CompilerParams(has_side_effects=_EFFECT),
    )(x_thru, land_thru, send_sems, recv_sems, after)
    me = _index(_my_place())
    own = lax.dynamic_index_in_dim(x_back, me, axis=0, keepdims=True) if per_peer else x_back[None]
    return lax.dynamic_update_slice_in_dim(got, own, me, axis=0)


def _all_reduce_packed(vec, after):
    R, L = vec.shape

    def body(x_ref, after_ref, out_ref, buf, send_sems, recv_sems):
        me = _my_place()
        buf[_index(me)] = x_ref[...]
        copies = []
        for k in range(1, N_DEV):
            to = _peer(me, k)
            cp = pltpu.make_async_remote_copy(
                src_ref=x_ref, dst_ref=buf.at[_index(me)],
                send_sem=send_sems.at[k - 1], recv_sem=recv_sems.at[k - 1], device_id=to, device_id_type=MESH)
            cp.start()
            copies.append(cp)
        for k in range(1, N_DEV):
            frm = _peer(me, k)
            pltpu.make_async_remote_copy(
                src_ref=x_ref, dst_ref=buf.at[_index(frm)],
                send_sem=send_sems.at[k - 1], recv_sem=recv_sems.at[k - 1], device_id=frm, device_id_type=MESH).wait_recv()
        for cp in copies:
            cp.wait_send()
        acc = buf[0]
        for d in range(1, N_DEV):
            acc = acc + buf[d]
        out_ref[...] = acc

    vm = pl.BlockSpec(memory_space=pltpu.VMEM)
    return pl.pallas_call(
        body, name="all_reduce_packed", in_specs=[vm, ANY_SPEC], out_specs=vm, out_shape=_sds((R, L), F32),
        scratch_shapes=[pltpu.VMEM((N_DEV, R, L), F32), pltpu.SemaphoreType.DMA((7,)), pltpu.SemaphoreType.DMA((7,))],
        compiler_params=pltpu.CompilerParams(vmem_limit_bytes=32 * 2**20),
    )(vec, after)


def _pack(arrays):
    rows = []
    for a in arrays:
        f = a.reshape(-1).astype(F32)
        pad = (-f.shape[0]) % LANES
        rows.append(jnp.pad(f, (0, pad)).reshape(-1, LANES))
    out = jnp.concatenate(rows, axis=0)
    return jnp.pad(out, ((0, (-out.shape[0]) % SUBLANES), (0, 0)))


def _unpack(packed, shapes):
    out, r = [], 0
    for s in shapes:
        n = math.prod(s)
        nr = -(-n // LANES)
        out.append(packed[r:r + nr].reshape(-1)[:n].reshape(s))
        r += nr
    return out


def _row_blocks(g):
    R, C = g.shape
    return g.astype(BF16).reshape(N_DEV, R // N_DEV, C)


def kernel(x, w_in, sb_out_gain, dn_conv_w, dn_a_log, dn_dt_bias, dn_out_gain, w_out, ln_mix_pre, ln_mix_post, w_up, ffn_conv_w, ffn_conv_b, w_down, ln_ffn_pre, ln_ffn_post, loss_target, m_w_in, m_sb_out_gain, m_dn_conv_w, m_dn_a_log, m_dn_dt_bias, m_dn_out_gain, m_w_out, m_ln_mix_pre, m_ln_mix_post, m_w_up, m_ffn_conv_w, m_ffn_conv_b, m_w_down, m_ln_ffn_pre, m_ln_ffn_post, v_w_in, v_sb_out_gain, v_dn_conv_w, v_dn_a_log, v_dn_dt_bias, v_dn_out_gain, v_w_out, v_ln_mix_pre, v_ln_mix_post, v_w_up, v_ffn_conv_w, v_ffn_conv_b, v_w_down, v_ln_ffn_pre, v_ln_ffn_post):
    T, D = x.shape[1], x.shape[2]
    SBW = SB_HEADS * HEAD_DIM
    DNW = DN_HEADS * HEAD_DIM
    in_cols = 3 * SBW + 4 * DNW + 2 * DN_HEADS
    main_cols = 3 * SBW + 4 * DNW
    in_pad = main_cols + LANES
    qkv0, z0 = 3 * SBW, 3 * SBW + 3 * DNW
    gate_block = main_cols // LANES
    x2, tgt = x[0], loss_target[0]

    g_in = _all_gather(jnp.swapaxes(w_in[0], 0, 1).astype(BF16), "gather_w_in")
    small_w = _all_gather(_pack([dn_conv_w[0], ffn_conv_w[0]]), "gather_conv_w")
    st_out, tok = _spread_start(w_out[0].astype(BF16), False, "gather_w_out_start", g_in)
    st_up, tok = _spread_start(w_up[0].astype(BF16), False, "gather_w_up_start", tok)
    st_down, tok_gather = _spread_start(w_down[0].astype(BF16), False, "gather_w_down_start", tok)
    w_in_t = jnp.pad(g_in.reshape(in_cols, D), ((0, in_pad - in_cols), (0, 0)))
    parts = [_unpack(small_w[d], [dn_conv_w.shape[1:], ffn_conv_w.shape[1:]]) for d in range(N_DEV)]
    dn_cw = jnp.concatenate([p[0] for p in parts], axis=1)
    ffn_cw = jnp.concatenate([p[1] for p in parts], axis=1)
    lane_pad = lambda a, off: jnp.pad(a, ((0, 0), (off, LANES - off - a.shape[1])))
    a_log_l, dt_bias_l = lane_pad(dn_a_log, DN_HEADS), lane_pad(dn_dt_bias, DN_HEADS)

    xn = _norm_in(x2, ln_mix_pre)
    proj = _matmul(xn, w_in_t, "nt", F32, "proj_in", tm_cap=512, tn_cap=2432, after=tok_gather)
    o_sb, mix_sb, sb_att, sb_lb = _sb_attention(proj, sb_out_gain)
    qn = _dn_branch(proj, qkv0, dn_cw, 0, True, HEAD_DIM ** -0.5)
    kn = _dn_branch(proj, qkv0 + DNW, dn_cw, DNW, True, 1.0)
    vn = _dn_branch(proj, qkv0 + 2 * DNW, dn_cw, 2 * DNW, False, 1.0)
    gc_full, beta_full = _dn_gates(proj, gate_block, a_log_l, dt_bias_l)
    o_dn, mix_dn, tm_all, s_all = _dn_scan(qn, kn, vn, gc_full, beta_full, proj, z0, dn_out_gain)
    mix = jnp.concatenate([mix_sb, mix_dn], axis=1)
    w_out_f = _spread_wait(st_out, False, "gather_w_out_wait", mix).reshape(w_out.shape[1] * N_DEV, D)
    m = _matmul(mix, w_out_f, "nn", F32, "proj_out")
    h, hn = _mix_residual(x2, m, ln_mix_post, ln_ffn_pre)
    w_up_cut = _spread_wait(st_up, False, "gather_w_up_wait", hn)
    u = _matmul(hn, w_up_cut, "nn", F32, "ffn_up", tn_cap=w_up.shape[2], b_cut=True)
    act = _ffn_act(u, ffn_cw, ffn_conv_b)
    w_down_f = _spread_wait(st_down, False, "gather_w_down_wait", act).reshape(w_down.shape[1] * N_DEV, D)
    f = _matmul(act, w_down_f, "nn", F32, "ffn_down", tk_cap=2816)
    dy, df, d_ln_ffn_post, loss_part = _loss_head(h, f, ln_ffn_post, tgt)

    d_w_down = _matmul(act, df, "tn", BF16, "grad_w_down")
    st_xd, tok = _spread_start(_row_blocks(d_w_down), True, "exchange_w_down_start", loss_part)
    da = _matmul(df, w_down_f, "nt", F32, "bwd_ffn_down", after=tok)
    du, d_ffn_cwb = _ffn_act_bwd(u, ffn_cw, ffn_conv_b, da)
    d_ffn_cwb = jnp.concatenate([d_ffn_cwb[0], d_ffn_cwb[1]], axis=1)
    d_w_up_cut = _matmul(hn, du, "tn", BF16, "grad_w_up", b_cut=True, out_cut=True)
    st_xu, tok = _spread_start(d_w_up_cut, True, "exchange_w_up_start", d_ffn_cwb)
    dhn = _matmul(du, w_up_cut, "nt", F32, "bwd_ffn_up", after=tok, a_cut=True, b_cut=True)
    dh, dm, d_ln_ffn_pre, d_ln_mix_post = _ffn_residual_bwd(dy, dhn, h, ln_ffn_pre, m, ln_mix_post)

    d_w_out = _matmul(mix, dm, "tn", BF16, "grad_w_out")
    st_xo, tok = _spread_start(_row_blocks(d_w_out), True, "exchange_w_out_start", d_ln_ffn_pre)
    dmix = _matmul(dm, w_out_f, "nt", F32, "bwd_proj_out", after=tok)
    dq_sb, dk_sb, dv_sb, d_sb_gain = _sb_attention_bwd(proj, sb_out_gain, o_sb, sb_att, sb_lb, dmix)
    dqn, dkn, dvn, dgc_full, dbeta_full, dz, d_dn_gain = _dn_scan_bwd(
        qn, kn, vn, gc_full, beta_full, proj, z0, dn_out_gain, o_dn, tm_all, s_all, dmix, SBW)
    du_q, dcw_q = _dn_branch_bwd(proj, qkv0, dn_cw, 0, True, HEAD_DIM ** -0.5, dqn)
    du_k, dcw_k = _dn_branch_bwd(proj, qkv0 + DNW, dn_cw, DNW, True, 1.0, dkn)
    du_v, dcw_v = _dn_branch_bwd(proj, qkv0 + 2 * DNW, dn_cw, 2 * DNW, False, 1.0, dvn)
    dba, d_a_log_l, d_dt_bias_l = _dn_gates_bwd(proj, gate_block, a_log_l, dt_bias_l, dgc_full, dbeta_full)
    dproj = jnp.concatenate([dq_sb, dk_sb.astype(BF16), dv_sb.astype(BF16), du_q, du_k, du_v, dz, dba], axis=1)
    d_w_in_t = _matmul(dproj, xn, "tn", BF16, "grad_w_in", tm_cap=2432, tn_cap=512, tk_cap=1024)
    d_w_in_cut = d_w_in_t[:in_cols].reshape(N_DEV, in_cols // N_DEV, D)
    st_xi, tok = _spread_start(d_w_in_cut, True, "exchange_w_in_start", d_sb_gain)
    dxn = _matmul(dproj, w_in_t, "nn", F32, "bwd_proj_in", tk_cap=2432, after=tok)
    grad_x, d_ln_mix_pre = _input_bwd(dh, dxn, x2, ln_mix_pre)

    big = {}
    after = grad_x
    for n, st, w_, m_, v_ in [("w_down", st_xd, w_down, m_w_down, v_w_down), ("w_up", st_xu, w_up, m_w_up, v_w_up),
                              ("w_out", st_xo, w_out, m_w_out, v_w_out)]:
        got = _spread_wait(st, True, "exchange_" + n + "_wait", after)
        big[n] = _adamw_sharded(got, w_, m_, v_, "adamw_" + n)
        after = big[n][1]

    d_dn_cw = jnp.concatenate([dcw_q[:SHORT_CONV], dcw_k[:SHORT_CONV], dcw_v[:SHORT_CONV]], axis=1)
    small = [loss_part[:, :1], d_sb_gain, d_a_log_l[:, DN_HEADS:2 * DN_HEADS], d_dt_bias_l[:, DN_HEADS:2 * DN_HEADS], d_dn_gain,
             d_ln_mix_pre, d_ln_mix_post, d_ffn_cwb[FFN_CONV:FFN_CONV + 1], d_ln_ffn_pre, d_ln_ffn_post,
             d_dn_cw, d_ffn_cwb[:FFN_CONV]]
    shapes = [a.shape for a in small]
    red = _unpack(_all_reduce_packed(_pack(small), after), shapes)
    loss = red[0].reshape(())
    me = _index(_my_place())
    g_dn_cw = lax.dynamic_slice_in_dim(red[10], me * dn_conv_w.shape[2], dn_conv_w.shape[2], axis=1)
    g_ffn_cw = lax.dynamic_slice_in_dim(red[11], me * ffn_conv_w.shape[2], ffn_conv_w.shape[2], axis=1)
    names = ["sb_out_gain", "dn_conv_w", "dn_a_log", "dn_dt_bias", "dn_out_gain", "ln_mix_pre", "ln_mix_post",
             "ffn_conv_w", "ffn_conv_b", "ln_ffn_pre", "ln_ffn_post"]
    g_small = dict(sb_out_gain=red[1], dn_conv_w=g_dn_cw[None], dn_a_log=red[2], dn_dt_bias=red[3], dn_out_gain=red[4],
                   ln_mix_pre=red[5], ln_mix_post=red[6], ffn_conv_w=g_ffn_cw[None], ffn_conv_b=red[7],
                   ln_ffn_pre=red[8], ln_ffn_post=red[9])
    w_small = dict(sb_out_gain=sb_out_gain, dn_conv_w=dn_conv_w, dn_a_log=dn_a_log, dn_dt_bias=dn_dt_bias,
                   dn_out_gain=dn_out_gain, ln_mix_pre=ln_mix_pre, ln_mix_post=ln_mix_post, ffn_conv_w=ffn_conv_w,
                   ffn_conv_b=ffn_conv_b, ln_ffn_pre=ln_ffn_pre, ln_ffn_post=ln_ffn_post)
    m_small = dict(sb_out_gain=m_sb_out_gain, dn_conv_w=m_dn_conv_w, dn_a_log=m_dn_a_log, dn_dt_bias=m_dn_dt_bias,
                   dn_out_gain=m_dn_out_gain, ln_mix_pre=m_ln_mix_pre, ln_mix_post=m_ln_mix_post, ffn_conv_w=m_ffn_conv_w,
                   ffn_conv_b=m_ffn_conv_b, ln_ffn_pre=m_ln_ffn_pre, ln_ffn_post=m_ln_ffn_post)
    v_small = dict(sb_out_gain=v_sb_out_gain, dn_conv_w=v_dn_conv_w, dn_a_log=v_dn_a_log, dn_dt_bias=v_dn_dt_bias,
                   dn_out_gain=v_dn_out_gain, ln_mix_pre=v_ln_mix_pre, ln_mix_post=v_ln_mix_post, ffn_conv_w=v_ffn_conv_w,
                   ffn_conv_b=v_ffn_conv_b, ln_ffn_pre=v_ln_ffn_pre, ln_ffn_post=v_ln_ffn_post)
    sshapes = [w_small[n].shape for n in names]
    g_packed = _pack([g_small[n] for n in names])
    fill = [jnp.zeros((g_packed.shape[0] * LANES - sum(-(-math.prod(s) // LANES) * LANES for s in sshapes),), F32)]
    upd = _adamw_packed(g_packed, _pack([a for group in (w_small, m_small, v_small)
                                         for a in [group[n] for n in names] + fill]))
    d_small, nm_small, nv_small = [dict(zip(names, _unpack(p, sshapes))) for p in upd]

    got = _spread_wait(st_xi, True, "exchange_w_in_wait", d_small["ln_ffn_post"])
    flip = lambda a: jnp.swapaxes(a, 1, 2)
    big["w_in"] = [flip(a) for a in _adamw_sharded(got, flip(w_in), flip(m_w_in), flip(v_w_in), "adamw_w_in")]

    order = ["w_in", "sb_out_gain", "dn_conv_w", "dn_a_log", "dn_dt_bias", "dn_out_gain", "w_out", "ln_mix_pre",
             "ln_mix_post", "w_up", "ffn_conv_w", "ffn_conv_b", "w_down", "ln_ffn_pre", "ln_ffn_post"]
    pick = lambda n, i: big[n][i] if n in big else [g_small, d_small, nm_small, nv_small][i][n].reshape(w_small[n].shape)
    return (loss, grad_x[None], *[pick(n, 0) for n in order], *[pick(n, 1) for n in order],
            *[pick(n, 2) for n in order], *[pick(n, 3) for n in order])
```

```python
import functools
import math

import jax
import jax.numpy as jnp
from jax import lax
from jax.experimental import pallas as pl
from jax.experimental.pallas import tpu as pltpu

F32 = jnp.float32
BF16 = jnp.bfloat16

N_DEV = 8
HEAD_DIM = 128
SB_HEADS = 8
DN_HEADS = 8
DN_CHUNK = 128
DN_GROUP = 8
INV_BLOCK = 16
SB_KEYS = 256
SHORT_CONV = 4
FFN_CONV = 3
EPS = 1e-6
LANES = 128
SUBLANES = 8
VMEM_CAP = 56 * 2**20

ADAM_LR = 0.001
ADAM_B1 = 0.9
ADAM_B2 = 0.999
ADAM_EPS = 1e-08
ADAM_WD = 0.01
ADAM_STEP = 10

MESH = pl.DeviceIdType.MESH

assert HEAD_DIM == DN_CHUNK == LANES


def _tile(n, cap, mult):
    if n <= cap:
        return n
    t = (cap // mult) * mult
    while t >= mult:
        if n % t == 0:
            return t
        t -= mult
    raise ValueError(f"no tile for {n} under {cap} in multiples of {mult}")


def _params(sem, vmem_bytes):
    limit = int(min(VMEM_CAP, max(vmem_bytes, 16 * 2**20)))
    if not sem:
        return pltpu.CompilerParams(vmem_limit_bytes=limit)
    return pltpu.CompilerParams(dimension_semantics=sem, vmem_limit_bytes=limit)


def _nbytes(shape, dtype):
    return math.prod(shape) * jnp.dtype(dtype).itemsize


def _hbm(*arrays):
    return [pltpu.with_memory_space_constraint(a, pltpu.HBM) for a in arrays]


_NN = (((1,), (0,)), ((), ()))
_NT = (((1,), (1,)), ((), ()))
_TN = (((0,), (0,)), ((), ()))


def _batched(dims, ndim):
    if ndim == 2:
        return dims
    (ca,), (cb,) = dims[0]
    return (((ca + 1,), (cb + 1,)), ((0,), (0,)))


def _dot(a, b, dims=_NN):
    return lax.dot_general(a.astype(BF16), b.astype(BF16), _batched(dims, a.ndim), preferred_element_type=F32)


def _split2(x):
    hi = x.astype(BF16)
    lo = (x - hi.astype(F32)).astype(BF16)
    return hi, lo


def _split3(x):
    hi = x.astype(BF16)
    r = x - hi.astype(F32)
    mid = r.astype(BF16)
    lo = (r - mid.astype(F32)).astype(BF16)
    return hi, mid, lo


def _dot01(x, m01, passes=3):
    parts = _split3(x) if passes == 3 else _split2(x)
    out = None
    for p in parts:
        t = lax.dot_general(p, m01, _NN, preferred_element_type=F32)
        out = t if out is None else out + t
    return out


def _dot01_left(m01, x, passes=3):
    parts = _split3(x) if passes == 3 else _split2(x)
    out = None
    for p in parts:
        t = lax.dot_general(m01, p, _NN, preferred_element_type=F32)
        out = t if out is None else out + t
    return out


def _mm3(a, b, dims=_NN):
    ah, al = _split2(a)
    bh, bl = _split2(b)
    d = functools.partial(lax.dot_general, dimension_numbers=_batched(dims, a.ndim), preferred_element_type=F32)
    return d(ah, bh) + (d(ah, bl) + d(al, bh))


def _rowsum(x):
    return jnp.sum(x, axis=-1, keepdims=True)


def _t(x):
    return jnp.swapaxes(x, -1, -2)


def _sigmoid(x):
    return 1.0 / (1.0 + jnp.exp(-x))


def _softplus(x):
    return jnp.maximum(x, 0.0) + jnp.log(1.0 + jnp.exp(-jnp.abs(x)))


def _silu(x):
    return x * _sigmoid(x)


def _silu_grad(x):
    s = _sigmoid(x)
    return s * (1.0 + x * (1.0 - s))


_GELU_C = math.sqrt(2.0 / math.pi)


def _gelu(x):
    return 0.5 * x * (1.0 + jnp.tanh(_GELU_C * (x + 0.044715 * x * x * x)))


def _gelu_and_grad(x):
    x2 = x * x
    th = jnp.tanh(_GELU_C * (x + 0.044715 * x2 * x))
    half = 0.5 * (1.0 + th)
    return x * half, half + 0.5 * x * (1.0 - th * th) * (_GELU_C * (1.0 + 3.0 * 0.044715 * x2))


def _rms(x, g):
    r = lax.rsqrt(jnp.mean(x * x, axis=-1, keepdims=True) + EPS)
    return x * r * g


def _rms_bwd(dy, x, g):
    r = lax.rsqrt(jnp.mean(x * x, axis=-1, keepdims=True) + EPS)
    xh = x * r
    gdy = dy * g
    dx = r * (gdy - xh * jnp.mean(gdy * xh, axis=-1, keepdims=True))
    return dx, jnp.sum(dy * xh, axis=-2, keepdims=True)


def _iota2(shape, axis):
    return lax.broadcasted_iota(jnp.int32, shape, axis)


def _shift_down(cur, prev8, k):
    n = cur.shape[0]
    r = pltpu.roll(cur, k, 0)
    pr = pltpu.roll(prev8, k, 0)
    head = jnp.where(_iota2(pr.shape, 0) < k, pr, r[0:SUBLANES])
    if n == SUBLANES:
        return head
    return jnp.concatenate([head, r[SUBLANES:]], axis=0)


def _shift_up(cur, next8, k):
    n = cur.shape[0]
    r = pltpu.roll(cur, n - k, 0)
    nr = pltpu.roll(next8, SUBLANES - k, 0)
    tail = jnp.where(_iota2(nr.shape, 0) >= SUBLANES - k, nr, r[n - SUBLANES:])
    if n == SUBLANES:
        return tail
    return jnp.concatenate([r[:n - SUBLANES], tail], axis=0)


def _causal_conv(cur, prev8, w_ref, taps):
    out = cur * w_ref[taps - 1:taps, :]
    for j in range(taps - 1):
        out = out + _shift_down(cur, prev8, taps - 1 - j) * w_ref[j:j + 1, :]
    return out


def _anti_conv(cur, next8, w_ref, taps):
    out = cur * w_ref[taps - 1:taps, :]
    for j in range(taps - 1):
        out = out + _shift_up(cur, next8, taps - 1 - j) * w_ref[j:j + 1, :]
    return out


def _matmul(a, b, mode, out_dtype, name, tm_cap=1024, tn_cap=1024, tk_cap=2048, after=None,
            a_cut=False, b_cut=False, out_cut=False):
    a_shard = a.shape[2] if a_cut else None
    b_shard = b.shape[2] if b_cut else None
    a_full = (a.shape[1], a.shape[0] * a_shard) if a_cut else a.shape
    b_full = (b.shape[1], b.shape[0] * b_shard) if b_cut else b.shape
    assert not (a_cut and mode == "tn")
    if mode == "nn":
        (M, K), N = a_full, b_full[1]
    elif mode == "nt":
        (M, K), N = a_full, b_full[0]
    else:
        (K, M), N = a_full, b_full[1]
    n_unit = b_shard if (b_cut and mode != "nt") else N
    k_unit = math.gcd(a_shard or K, b_shard if (b_cut and mode == "nt") else K)
    tm = _tile(M, tm_cap, LANES)
    tn = N // N_DEV if out_cut else _tile(n_unit, tn_cap, LANES)
    tk = _tile(k_unit, tk_cap, LANES)
    assert n_unit % tn == 0 and k_unit % tk == 0
    nk = K // tk
    dims = {"nn": _NN, "nt": _NT, "tn": _TN}[mode]
    if a_cut:
        pa = a_shard // tk
        a_spec = pl.BlockSpec((None, tm, tk), lambda i, j, k: (k // pa, i, k % pa))
    elif mode == "tn":
        a_spec = pl.BlockSpec((tk, tm), lambda i, j, k: (k, i))
    else:
        a_spec = pl.BlockSpec((tm, tk), lambda i, j, k: (i, k))
    if b_cut and mode == "nt":
        pb = b_shard // tk
        b_spec = pl.BlockSpec((None, tn, tk), lambda i, j, k: (k // pb, j, k % pb))
    elif b_cut:
        pb = b_shard // tn
        b_spec = pl.BlockSpec((None, tk, tn), lambda i, j, k: (j // pb, k, j % pb))
    elif mode == "nt":
        b_spec = pl.BlockSpec((tn, tk), lambda i, j, k: (j, k))
    else:
        b_spec = pl.BlockSpec((tk, tn), lambda i, j, k: (k, j))
    if out_cut:
        out_spec, out_shape = pl.BlockSpec((None, tm, tn), lambda i, j, k: (j, i, 0)), (N_DEV, M, tn)
    else:
        out_spec, out_shape = pl.BlockSpec((tm, tn), lambda i, j, k: (i, j)), (M, N)

    def body(a_ref, b_ref, *rest):
        if nk == 1:
            rest[-1][...] = lax.dot_general(a_ref[...], b_ref[...], dims, preferred_element_type=F32).astype(rest[-1].dtype)
            return
        o_ref, acc_ref = rest[-2:]
        k = pl.program_id(2)

        @pl.when(k == 0)
        def _():
            acc_ref[...] = jnp.zeros_like(acc_ref)

        acc_ref[...] += lax.dot_general(a_ref[...], b_ref[...], dims, preferred_element_type=F32)

        @pl.when(k == nk - 1)
        def _():
            o_ref[...] = acc_ref[...].astype(o_ref.dtype)

    vmem = 2 * (_nbytes((tm, tk), a.dtype) + _nbytes((tk, tn), b.dtype) + _nbytes((tm, tn), out_dtype)) + _nbytes((tm, tn), F32)
    vmem += _nbytes((tm, tn), F32) + (2 * _nbytes((tm, tk), a.dtype) if mode == "tn" else 0)
    tokens = [] if after is None else [after]
    return pl.pallas_call(
        body, name=name, grid=(M // tm, N // tn, nk),
        in_specs=[a_spec, b_spec] + [pl.BlockSpec(t.shape, lambda i, j, k: (0, 0)) for t in tokens],
        out_specs=out_spec,
        out_shape=jax.ShapeDtypeStruct(out_shape, out_dtype),
        scratch_shapes=[] if nk == 1 else [pltpu.VMEM((tm, tn), F32)],
        compiler_params=_params(("parallel", "parallel", "arbitrary"), vmem + 4 * 2**20),
    )(*_hbm(a, b), *tokens)


def _row_call(body, name, T, D, ins, outs, tr, acc_outs=()):
    def spec(a, kind):
        if kind == "row":
            return pl.BlockSpec((tr, a.shape[1]), lambda i: (i, 0))
        return pl.BlockSpec(a.shape, lambda i: (0, 0))
    in_specs = [spec(a, k) for a, k in ins]
    out_specs = [spec(a, k) for a, k in outs] + [spec(a, "vec") for a in acc_outs]
    out_shape = [a for a, _ in outs] + list(acc_outs)
    vmem = 2 * sum(_nbytes((tr, a.shape[1]) if k == "row" else a.shape, a.dtype) for a, k in list(ins) + list(outs))
    return pl.pallas_call(
        body, name=name, grid=(T // tr,), in_specs=in_specs, out_specs=out_specs, out_shape=out_shape,
        compiler_params=_params(("arbitrary",), 3 * vmem + 8 * 2**20),
    )(*_hbm(*[a for a, _ in ins]))


def _sds(shape, dtype):
    return jax.ShapeDtypeStruct(shape, dtype)


def _accumulate(ref, val):
    @pl.when(pl.program_id(0) == 0)
    def _():
        ref[...] = jnp.zeros_like(ref)
    ref[...] += val


def _norm_in(x, g):
    T, D = x.shape

    def body(x_ref, g_ref, o_ref):
        o_ref[...] = _rms(x_ref[...], g_ref[...]).astype(BF16)

    return _row_call(body, "norm_in", T, D, [(x, "row"), (g, "vec")], [(_sds((T, D), BF16), "row")], _tile(T, 256, 16))[0]


def _mix_residual(x, m, g_post, g_pre):
    T, D = x.shape

    def body(x_ref, m_ref, gp_ref, gn_ref, h_ref, hn_ref):
        h = x_ref[...] + _rms(m_ref[...], gp_ref[...])
        h_ref[...] = h
        hn_ref[...] = _rms(h, gn_ref[...]).astype(BF16)

    return _row_call(body, "mix_residual", T, D, [(x, "row"), (m, "row"), (g_post, "vec"), (g_pre, "vec")],
                     [(_sds((T, D), F32), "row"), (_sds((T, D), BF16), "row")], _tile(T, 256, 16))


def _loss_head(h, f, g_post, target):
    T, D = h.shape

    def body(h_ref, f_ref, g_ref, t_ref, dy_ref, df_ref, dg_ref, loss_ref):
        f = f_ref[...]
        g = g_ref[...]
        diff = h_ref[...] + _rms(f, g) - t_ref[...]
        dy = diff * (1.0 / D)
        dy_ref[...] = dy
        df, dg = _rms_bwd(dy, f, g)
        df_ref[...] = df.astype(BF16)
        _accumulate(dg_ref, dg)
        _accumulate(loss_ref, jnp.full((1, LANES), 0.5 / D, F32) * jnp.sum(diff * diff))

    return _row_call(body, "loss_head", T, D, [(h, "row"), (f, "row"), (g_post, "vec"), (target, "row")],
                     [(_sds((T, D), F32), "row"), (_sds((T, D), BF16), "row")], _tile(T, 256, 16),
                     acc_outs=[_sds((1, D), F32), _sds((1, LANES), F32)])


def _ffn_residual_bwd(dy, dhn, h, g_pre, m, g_post):
    T, D = h.shape

    def body(dy_ref, dhn_ref, h_ref, gn_ref, m_ref, gp_ref, dh_ref, dm_ref, dgn_ref, dgp_ref):
        dhh, dgn = _rms_bwd(dhn_ref[...], h_ref[...], gn_ref[...])
        dh = dy_ref[...] + dhh
        dh_ref[...] = dh
        dm, dgp = _rms_bwd(dh, m_ref[...], gp_ref[...])
        dm_ref[...] = dm.astype(BF16)
        _accumulate(dgn_ref, dgn)
        _accumulate(dgp_ref, dgp)

    return _row_call(body, "ffn_residual_bwd", T, D,
                     [(dy, "row"), (dhn, "row"), (h, "row"), (g_pre, "vec"), (m, "row"), (g_post, "vec")],
                     [(_sds((T, D), F32), "row"), (_sds((T, D), BF16), "row")], _tile(T, 128, 16),
                     acc_outs=[_sds((1, D), F32), _sds((1, D), F32)])


def _input_bwd(dh, dxn, x, g):
    T, D = x.shape

    def body(dh_ref, dxn_ref, x_ref, g_ref, dx_ref, dg_ref):
        dx, dg = _rms_bwd(dxn_ref[...], x_ref[...], g_ref[...])
        dx_ref[...] = dh_ref[...] + dx
        _accumulate(dg_ref, dg)

    return _row_call(body, "input_bwd", T, D, [(dh, "row"), (dxn, "row"), (x, "row"), (g, "vec")],
                     [(_sds((T, D), F32), "row")], _tile(T, 256, 16), acc_outs=[_sds((1, D), F32)])


def _ffn_act(u, conv_w, conv_b):
    T, F2 = u.shape
    F = F2 // 2
    tc = _tile(F, 512, LANES)
    tr = _tile(T, 1024, SUBLANES)
    nc = F // tc
    r8 = tr // SUBLANES

    def body(ug_ref, ugp_ref, uv_ref, uvp_ref, wg_ref, wv_ref, bg_ref, bv_ref, a_ref):
        first = pl.program_id(1) == 0
        cg = _causal_conv(ug_ref[...], jnp.where(first, 0.0, ugp_ref[...]), wg_ref, FFN_CONV) + bg_ref[...]
        cv = _causal_conv(uv_ref[...], jnp.where(first, 0.0, uvp_ref[...]), wv_ref, FFN_CONV) + bv_ref[...]
        a_ref[...] = (_gelu(cg) * cv).astype(BF16)

    cur = lambda off: pl.BlockSpec((tr, tc), lambda j, i: (i, j + off))
    prev = lambda off: pl.BlockSpec((SUBLANES, tc), lambda j, i: (jnp.maximum(i * r8 - 1, 0), j + off))
    wsp = lambda off: pl.BlockSpec((FFN_CONV, tc), lambda j, i: (0, j + off))
    bsp = lambda off: pl.BlockSpec((1, tc), lambda j, i: (0, j + off))
    return pl.pallas_call(
        body, name="ffn_act", grid=(nc, T // tr),
        in_specs=[cur(0), prev(0), cur(nc), prev(nc), wsp(0), wsp(nc), bsp(0), bsp(nc)],
        out_specs=pl.BlockSpec((tr, tc), lambda j, i: (i, j)),
        out_shape=_sds((T, F), BF16),
        compiler_params=_params(("parallel", "arbitrary"), 12 * _nbytes((tr, tc), F32) + 8 * 2**20),
    )(*_hbm(u, u, u, u, conv_w, conv_w, conv_b, conv_b))


def _ffn_act_bwd(u, conv_w, conv_b, da):
    T, F2 = u.shape
    F = F2 // 2
    tc = _tile(F, 512, LANES)
    tr = _tile(T, 512, SUBLANES)
    nc = F // tc
    r8 = tr // SUBLANES
    n8 = T // SUBLANES
    K = FFN_CONV

    def body(ug_ref, ugp_ref, ugn_ref, uv_ref, uvp_ref, uvn_ref, da_ref, dan_ref,
             wg_ref, wv_ref, bg_ref, bv_ref, du_ref, dwb_ref):
        i = pl.program_id(1)
        first = i == 0
        last = i == pl.num_programs(1) - 1

        def dconv(ug, ug_prev, uv, uv_prev, da_):
            cg = _causal_conv(ug, ug_prev, wg_ref, K) + bg_ref[...]
            cv = _causal_conv(uv, uv_prev, wv_ref, K) + bv_ref[...]
            act, act_grad = _gelu_and_grad(cg)
            return da_ * cv * act_grad, da_ * act

        ug, uv = ug_ref[...], uv_ref[...]
        ug_prev, uv_prev = jnp.where(first, 0.0, ugp_ref[...]), jnp.where(first, 0.0, uvp_ref[...])
        dcg, dcv = dconv(ug, ug_prev, uv, uv_prev, da_ref[...])
        dcgn, dcvn = dconv(ugn_ref[...], ug[tr - SUBLANES:], uvn_ref[...], uv[tr - SUBLANES:], dan_ref[...])
        du_ref[0] = _anti_conv(dcg, jnp.where(last, 0.0, dcgn), wg_ref, K).astype(BF16)
        du_ref[1] = _anti_conv(dcv, jnp.where(last, 0.0, dcvn), wv_ref, K).astype(BF16)

        @pl.when(first)
        def _():
            dwb_ref[...] = jnp.zeros_like(dwb_ref)

        for half, (dc, uo, uo_prev) in enumerate([(dcg, ug, ug_prev), (dcv, uv, uv_prev)]):
            rows = [jnp.sum(dc * _shift_down(uo, uo_prev, K - 1 - t), axis=0, keepdims=True) for t in range(K - 1)]
            rows += [jnp.sum(dc * uo, axis=0, keepdims=True), jnp.sum(dc, axis=0, keepdims=True)]
            rows += [jnp.zeros_like(rows[0])] * (SUBLANES - len(rows))
            dwb_ref[half] += jnp.concatenate(rows, axis=0)

    cur = lambda off: pl.BlockSpec((tr, tc), lambda j, i: (i, j + off))
    prev = lambda off: pl.BlockSpec((SUBLANES, tc), lambda j, i: (jnp.maximum(i * r8 - 1, 0), j + off))
    nxt = lambda off: pl.BlockSpec((SUBLANES, tc), lambda j, i: (jnp.minimum((i + 1) * r8, n8 - 1), j + off))
    wsp = lambda off: pl.BlockSpec((K, tc), lambda j, i: (0, j + off))
    bsp = lambda off: pl.BlockSpec((1, tc), lambda j, i: (0, j + off))
    return pl.pallas_call(
        body, name="ffn_act_bwd", grid=(nc, T // tr),
        in_specs=[cur(0), prev(0), nxt(0), cur(nc), prev(nc), nxt(nc), cur(0), nxt(0), wsp(0), wsp(nc), bsp(0), bsp(nc)],
        out_specs=[pl.BlockSpec((2, tr, tc), lambda j, i: (0, i, j)), pl.BlockSpec((2, SUBLANES, tc), lambda j, i: (0, 0, j))],
        out_shape=[_sds((2, T, F), BF16), _sds((2, SUBLANES, F), F32)],
        compiler_params=_params(("parallel", "arbitrary"), 24 * _nbytes((tr, tc), F32) + 8 * 2**20),
    )(*_hbm(u, u, u, u, u, u, da, da, conv_w, conv_w, conv_b, conv_b))


def _l2norm(s, scale):
    return s * (lax.rsqrt(jnp.sum(s * s, axis=-1, keepdims=True) + EPS) * scale)


def _dn_branch(proj, col0, conv_w, wcol0, l2, scale):
    T = proj.shape[0]
    W = DN_HEADS * HEAD_DIM
    tr = _tile(T, 2048, SUBLANES)
    r8 = tr // SUBLANES
    cb0, wb0 = col0 // HEAD_DIM, wcol0 // HEAD_DIM

    def body(u_ref, up_ref, w_ref, o_ref):
        first = pl.program_id(1) == 0
        s = _silu(_causal_conv(u_ref[...], jnp.where(first, 0.0, up_ref[...]), w_ref, SHORT_CONV))
        o_ref[...] = _l2norm(s, scale) if l2 else s

    return pl.pallas_call(
        body, name=f"dn_branch_{col0}", grid=(DN_HEADS, T // tr),
        in_specs=[pl.BlockSpec((tr, HEAD_DIM), lambda h, i: (i, cb0 + h)),
                  pl.BlockSpec((SUBLANES, HEAD_DIM), lambda h, i: (jnp.maximum(i * r8 - 1, 0), cb0 + h)),
                  pl.BlockSpec((SHORT_CONV, HEAD_DIM), lambda h, i: (0, wb0 + h))],
        out_specs=pl.BlockSpec((tr, HEAD_DIM), lambda h, i: (i, h)),
        out_shape=_sds((T, W), F32),
        compiler_params=_params(("parallel", "arbitrary"), 32 * _nbytes((tr, HEAD_DIM), F32) + 8 * 2**20),
    )(*_hbm(proj, proj, conv_w))


def _dn_branch_bwd(proj, col0, conv_w, wcol0, l2, scale, dy):
    T = proj.shape[0]
    W = DN_HEADS * HEAD_DIM
    tr = _tile(T, 2048, SUBLANES)
    r8 = tr // SUBLANES
    n8 = T // SUBLANES
    cb0, wb0 = col0 // HEAD_DIM, wcol0 // HEAD_DIM
    K = SHORT_CONV

    def body(u_ref, up_ref, un_ref, dy_ref, dyn_ref, w_ref, du_ref, dw_ref):
        i = pl.program_id(1)
        first = i == 0
        last = i == pl.num_programs(1) - 1

        def dconv(u, u_prev, dy_):
            c = _causal_conv(u, u_prev, w_ref, K)
            if l2:
                s = _silu(c)
                r = lax.rsqrt(jnp.sum(s * s, axis=-1, keepdims=True) + EPS)
                n = s * r
                ds = (scale * r) * (dy_ - n * jnp.sum(dy_ * n, axis=-1, keepdims=True))
            else:
                ds = dy_
            return ds * _silu_grad(c)

        u = u_ref[...]
        u_prev = jnp.where(first, 0.0, up_ref[...])
        dc = dconv(u, u_prev, dy_ref[...])
        dcn = jnp.where(last, 0.0, dconv(un_ref[...], u[tr - SUBLANES:], dyn_ref[...]))
        du_ref[...] = _anti_conv(dc, dcn, w_ref, K).astype(BF16)
        rows = [jnp.sum(dc * _shift_down(u, u_prev, K - 1 - t), axis=0, keepdims=True) for t in range(K - 1)]
        rows += [jnp.sum(dc * u, axis=0, keepdims=True)]
        rows += [jnp.zeros_like(rows[0])] * (SUBLANES - len(rows))
        upd = jnp.concatenate(rows, axis=0)

        @pl.when(first)
        def _():
            dw_ref[...] = jnp.zeros_like(dw_ref)
        dw_ref[...] += upd

    return pl.pallas_call(
        body, name=f"dn_branch_bwd_{col0}", grid=(DN_HEADS, T // tr),
        in_specs=[pl.BlockSpec((tr, HEAD_DIM), lambda h, i: (i, cb0 + h)),
                  pl.BlockSpec((SUBLANES, HEAD_DIM), lambda h, i: (jnp.maximum(i * r8 - 1, 0), cb0 + h)),
                  pl.BlockSpec((SUBLANES, HEAD_DIM), lambda h, i: (jnp.minimum((i + 1) * r8, n8 - 1), cb0 + h)),
                  pl.BlockSpec((tr, HEAD_DIM), lambda h, i: (i, h)),
                  pl.BlockSpec((SUBLANES, HEAD_DIM), lambda h, i: (jnp.minimum((i + 1) * r8, n8 - 1), h)),
                  pl.BlockSpec((K, HEAD_DIM), lambda h, i: (0, wb0 + h))],
        out_specs=[pl.BlockSpec((tr, HEAD_DIM), lambda h, i: (i, h)),
                   pl.BlockSpec((SUBLANES, HEAD_DIM), lambda h, i: (0, h))],
        out_shape=[_sds((T, W), BF16), _sds((SUBLANES, W), F32)],
        compiler_params=_params(("parallel", "arbitrary"), 32 * _nbytes((tr, HEAD_DIM), F32) + 8 * 2**20),
    )(*_hbm(proj, proj, proj, dy, dy, conv_w))


def _lane_masks(shape):
    lane = _iota2(shape, 1)
    return lane < DN_HEADS, (lane >= DN_HEADS) & (lane < 2 * DN_HEADS)


def _expand01(off):
    r = _iota2((LANES, DN_HEADS * HEAD_DIM), 0)
    c = _iota2((LANES, DN_HEADS * HEAD_DIM), 1)
    return (r == jnp.right_shift(c, int(math.log2(HEAD_DIM))) + off).astype(BF16)


def _select01(off):
    r = _iota2((DN_HEADS * HEAD_DIM, LANES), 0)
    c = _iota2((DN_HEADS * HEAD_DIM, LANES), 1)
    return (r == (c - off) * HEAD_DIM).astype(BF16)


def _dn_gates(proj, gate_block, a_log_l, dt_bias_l):
    T = proj.shape[0]
    C = DN_CHUNK
    W = DN_HEADS * HEAD_DIM

    def body(ba_ref, al_ref, dt_ref, gc_ref, beta_ref):
        ba = ba_ref[...]
        is_b, is_a = _lane_masks(ba.shape)
        g = jnp.where(is_a, -jnp.exp(al_ref[...]) * _softplus(ba + dt_ref[...]), 0.0)
        beta = jnp.where(is_b, _sigmoid(ba), 0.0)
        tri = (_iota2((C, C), 0) >= _iota2((C, C), 1)).astype(BF16)
        gc = _dot01_left(tri, g)
        gc_ref[...] = _dot01(gc, _expand01(DN_HEADS))
        beta_ref[...] = _dot01(beta, _expand01(0))

    vec = pl.BlockSpec((1, LANES), lambda n: (0, 0))
    return pl.pallas_call(
        body, name="dn_gates", grid=(T // C,),
        in_specs=[pl.BlockSpec((C, LANES), lambda n: (n, gate_block)), vec, vec],
        out_specs=[pl.BlockSpec((C, W), lambda n: (n, 0))] * 2,
        out_shape=[_sds((T, W), F32)] * 2,
        compiler_params=_params(("parallel",), 16 * 2**20),
    )(*_hbm(proj, a_log_l, dt_bias_l))


def _dn_gates_bwd(proj, gate_block, a_log_l, dt_bias_l, dgc_full, dbeta_full):
    T = proj.shape[0]
    C = DN_CHUNK
    W = DN_HEADS * HEAD_DIM

    def body(ba_ref, al_ref, dt_ref, dgc_ref, dbeta_ref, dba_ref, dal_ref, ddt_ref):
        ba = ba_ref[...]
        is_b, is_a = _lane_masks(ba.shape)
        ea = jnp.exp(al_ref[...])
        pre = ba + dt_ref[...]
        g = jnp.where(is_a, -ea * _softplus(pre), 0.0)
        beta = _sigmoid(ba)
        dgc = _dot01(dgc_ref[...], _select01(DN_HEADS))
        dbeta = _dot01(dbeta_ref[...], _select01(0))
        triu = (_iota2((C, C), 0) <= _iota2((C, C), 1)).astype(BF16)
        dg = _dot01_left(triu, dgc)
        da = jnp.where(is_a, dg * (-ea) * _sigmoid(pre), 0.0)
        dba_ref[...] = (da + jnp.where(is_b, dbeta * beta * (1.0 - beta), 0.0)).astype(BF16)
        _accumulate(dal_ref, jnp.sum(dg * g, axis=0, keepdims=True))
        _accumulate(ddt_ref, jnp.sum(da, axis=0, keepdims=True))

    vec = pl.BlockSpec((1, LANES), lambda n: (0, 0))
    full = pl.BlockSpec((C, W), lambda n: (n, 0))
    return pl.pallas_call(
        body, name="dn_gates_bwd", grid=(T // C,),
        in_specs=[pl.BlockSpec((C, LANES), lambda n: (n, gate_block)), vec, vec, full, full],
        out_specs=[pl.BlockSpec((C, LANES), lambda n: (n, 0)), vec, vec],
        out_shape=[_sds((T, LANES), BF16), _sds((1, LANES), F32), _sds((1, LANES), F32)],
        compiler_params=_params(("arbitrary",), 16 * 2**20),
    )(*_hbm(proj, a_log_l, dt_bias_l, dgc_full, dbeta_full))


def _unit_lower_inverse(L):
    C = L.shape[-1]
    row, col = _iota2((C, C), 0), _iota2((C, C), 1)
    eye = (row == col).astype(F32)
    sh = int(math.log2(INV_BLOCK))
    Ld = jnp.where(jnp.right_shift(row, sh) == jnp.right_shift(col, sh), L, 0.0)
    Lo = L - Ld
    X = eye - Ld
    P = Ld
    for _ in range(int(math.log2(INV_BLOCK)) - 1):
        P = _mm3(P, P)
        X = X + _mm3(X, P)
    N = _mm3(X, Lo)
    Y = eye - N
    P = N
    for _ in range(int(math.log2(C // INV_BLOCK)) - 1):
        P = _mm3(P, P)
        Y = Y + _mm3(Y, P)
    return _mm3(Y, X)


def _dn_chunk_common(q, k, v, gc, beta, gl):
    C = q.shape[-2]
    row, col = _iota2((C, C), 0), _iota2((C, C), 1)
    causal, strict = row >= col, row > col
    eg = jnp.exp(gc)
    decay = jnp.where(causal, jnp.exp(jnp.where(causal, gc - _t(gc), 0.0)), 0.0)
    kb, vb = k * beta, v * beta
    L = jnp.where(strict, _dot(kb, k, _NT) * decay, 0.0)
    Aqk = jnp.where(causal, _dot(q, k, _NT) * decay, 0.0)
    ektg = jnp.exp(gl - gc)
    return dict(causal=causal, strict=strict, eg=eg, decay=decay, kb=kb, vb=vb, L=L, Aqk=Aqk, ektg=ektg,
                kbg=kb * eg, kte=k * ektg, qd=q * eg, egl=jnp.exp(gl))


def _dn_scan(qn, kn, vn, gc_full, beta_full, proj, z_col0, gain):
    T, W = qn.shape
    C = DN_CHUNK
    N = T // C
    H = DN_HEADS
    G = DN_GROUP
    GW = G * HEAD_DIM
    zb0 = z_col0 // GW

    def body(q_ref, k_ref, v_ref, gc_ref, beta_ref, z_ref, gain_ref, o_ref, mix_ref, tm_ref, s_ref, S):
        @pl.when(pl.program_id(1) == 0)
        def _():
            S[...] = jnp.zeros_like(S)

        heads = lambda ref, rows=slice(None): jnp.stack([ref[rows, g * HEAD_DIM:(g + 1) * HEAD_DIM] for g in range(G)])
        q, k, v, gc, beta = heads(q_ref), heads(k_ref), heads(v_ref), heads(gc_ref), heads(beta_ref)
        gl = heads(gc_ref, slice(C - 1, C))
        c = _dn_chunk_common(q, k, v, gc, beta, gl)
        Tm = _unit_lower_inverse(c["L"])
        u = _dot(Tm, c["vb"])
        w = _dot(Tm, c["kbg"])
        S0 = S[...]
        vnew = u - _dot(w, S0)
        o = _dot(c["qd"], S0) + _dot(c["Aqk"], vnew)
        S[...] = S0 * c["egl"] + _dot(c["kte"], vnew, _TN)
        tm_ref[...] = Tm
        s_ref[...] = S0
        mix = (_rms(o, gain_ref[...]) * _silu(heads(z_ref))).astype(BF16)
        for g in range(G):
            sl = slice(g * HEAD_DIM, (g + 1) * HEAD_DIM)
            o_ref[:, sl] = o[g]
            mix_ref[:, sl] = mix[g]

    blk = pl.BlockSpec((C, GW), lambda h, n: (n, h))
    mat = pl.BlockSpec((G, None, C, C), lambda h, n: (h, n, 0, 0))
    return pl.pallas_call(
        body, name="dn_scan", grid=(H // G, N),
        in_specs=[blk, blk, blk, blk, blk, pl.BlockSpec((C, GW), lambda h, n: (n, zb0 + h)),
                  pl.BlockSpec((1, HEAD_DIM), lambda h, n: (0, 0))],
        out_specs=[blk, blk, mat, mat],
        out_shape=[_sds((T, W), F32), _sds((T, W), BF16), _sds((H, N, C, C), F32), _sds((H, N, C, C), F32)],
        scratch_shapes=[pltpu.VMEM((G, HEAD_DIM, HEAD_DIM), F32)],
        compiler_params=_params(("parallel", "arbitrary"), 32 * 2**20),
    )(*_hbm(qn, kn, vn, gc_full, beta_full, proj, gain))


def _dn_scan_bwd(qn, kn, vn, gc_full, beta_full, proj, z_col0, gain, o_raw, tm_all, s_all, dmix, dmix_col0):
    T, W = qn.shape
    C = DN_CHUNK
    N = T // C
    H = DN_HEADS
    G = DN_GROUP
    GW = G * HEAD_DIM
    zb0 = z_col0 // GW
    mb0 = dmix_col0 // GW

    def body(q_ref, k_ref, v_ref, gc_ref, beta_ref, z_ref, gain_ref, o_ref, tm_ref, s_ref, dmix_ref,
             dq_ref, dk_ref, dv_ref, dgc_ref, dbeta_ref, dz_ref, dgain_ref, dS):
        @pl.when(pl.program_id(1) == 0)
        def _():
            dS[...] = jnp.zeros_like(dS)

        @pl.when((pl.program_id(0) == 0) & (pl.program_id(1) == 0))
        def _():
            dgain_ref[...] = jnp.zeros_like(dgain_ref)

        heads = lambda ref, rows=slice(None): jnp.stack([ref[rows, g * HEAD_DIM:(g + 1) * HEAD_DIM] for g in range(G)])
        total = lambda x: jnp.sum(jnp.sum(x, axis=-1, keepdims=True), axis=-2, keepdims=True)
        gain = gain_ref[...]
        o, z, dmix = heads(o_ref), heads(z_ref), heads(dmix_ref)
        dz = (dmix * _rms(o, gain) * _silu_grad(z)).astype(BF16)
        do, dgain = _rms_bwd(dmix * _silu(z), o, gain)
        dgain_ref[...] += jnp.sum(dgain, axis=0)

        q, k, v, gc, beta = heads(q_ref), heads(k_ref), heads(v_ref), heads(gc_ref), heads(beta_ref)
        gl = heads(gc_ref, slice(C - 1, C))
        c = _dn_chunk_common(q, k, v, gc, beta, gl)
        Tm, S0, dS1 = tm_ref[...], s_ref[...], dS[...]
        w = _dot(Tm, c["kbg"])
        vnew = _dot(Tm, c["vb"]) - _dot(w, S0)

        dvnew = _dot(c["Aqk"], do, _TN) + _dot(c["kte"], dS1)
        dAqk = jnp.where(c["causal"], _dot(do, vnew, _NT), 0.0)
        dqd = _dot(do, S0, _NT)
        dkte = _dot(vnew, dS1, _NT)
        dgl = total(dS1 * S0) * c["egl"]
        dw = -_dot(dvnew, S0, _NT)
        dS[...] = dS1 * c["egl"] + _dot(c["qd"], do, _TN) - _dot(w, dvnew, _TN)

        dTm = _dot(dvnew, c["vb"], _NT) + _dot(dw, c["kbg"], _NT)
        dvb = _dot(Tm, dvnew, _TN)
        dkbg = _dot(Tm, dw, _TN)
        dL = jnp.where(c["strict"], -_mm3(_mm3(Tm, dTm, _TN), Tm, _NT), 0.0)
        dP = dL * c["decay"]
        dQ = dAqk * c["decay"]
        M = dL * c["L"] + dAqk * c["Aqk"]
        dkb = _dot(dP, k) + dkbg * c["eg"]
        dk = _dot(dP, c["kb"], _TN) + _dot(dQ, q, _TN) + dkte * c["ektg"] + dkb * beta
        dq = _dot(dQ, k) + dqd * c["eg"]
        tk = _rowsum(dkte * c["kte"])
        dgc = (_rowsum(M) - _rowsum(_t(M)) + _rowsum(dqd * c["qd"]) - tk + _rowsum(dkbg * c["kbg"]))
        dgl = dgl + total(tk)
        dgc = jnp.broadcast_to(dgc, q.shape) + jnp.where(_iota2((C, HEAD_DIM), 0) == C - 1, dgl, 0.0)
        dv = dvb * beta
        dbeta = jnp.broadcast_to(_rowsum(dkb * k) + _rowsum(dvb * v), q.shape)
        for g in range(G):
            sl = slice(g * HEAD_DIM, (g + 1) * HEAD_DIM)
            dz_ref[:, sl] = dz[g]
            dq_ref[:, sl] = dq[g]
            dk_ref[:, sl] = dk[g]
            dv_ref[:, sl] = dv[g]
            dgc_ref[:, sl] = dgc[g]
            dbeta_ref[:, sl] = dbeta[g]

    rev = lambda off: pl.BlockSpec((C, GW), lambda h, n: (N - 1 - n, off + h))
    mat = pl.BlockSpec((G, None, C, C), lambda h, n: (h, N - 1 - n, 0, 0))
    vec = pl.BlockSpec((1, HEAD_DIM), lambda h, n: (0, 0))
    return pl.pallas_call(
        body, name="dn_scan_bwd", grid=(H // G, N),
        in_specs=[rev(0), rev(0), rev(0), rev(0), rev(0), rev(zb0), vec, rev(0), mat, mat, rev(mb0)],
        out_specs=[rev(0)] * 6 + [vec],
        out_shape=[_sds((T, W), F32)] * 5 + [_sds((T, W), BF16), _sds((1, HEAD_DIM), F32)],
        scratch_shapes=[pltpu.VMEM((G, HEAD_DIM, HEAD_DIM), F32)],
        compiler_params=_params(("arbitrary", "arbitrary"), 40 * 2**20),
    )(*_hbm(qn, kn, vn, gc_full, beta_full, proj, gain, o_raw, tm_all, s_all, dmix))


def _sb_terms(z, ahead, first_key):
    lb = jnp.minimum(z, 0.0) - jnp.log(1.0 + jnp.exp(-jnp.abs(z)))
    if ahead is None:
        return None, lb, lb - z
    valid = ahead < -first_key
    return valid, lb, jnp.where(valid, lb - z, 0.0)


def _masked(valid, x):
    return x if valid is None else jnp.where(valid, x, 0.0)


def _sb_attention(qkv, gain, tq_cap=2048):
    T = qkv.shape[0]
    H = SB_HEADS
    B = min(SB_KEYS, T)
    TQ = _tile(T, tq_cap, B)
    per = TQ // B

    assert per % 2 == 0
    n_saved = per * (T // TQ) * (T // TQ + 1) // 2

    def body(q_ref, k_ref, v_ref, gain_ref, o_ref, mix_ref, att_hbm, lb_hbm, att_buf, lb_buf, sems):
        h, i = pl.program_id(0), pl.program_id(1)
        q = q_ref[...].astype(BF16)
        upper = (_iota2((B, B), 0) > _iota2((B, B), 1)).astype(BF16)
        ahead = _iota2((TQ, B), 1) - _iota2((TQ, B), 0)
        last = (i + 1) * per - 1
        base = per * (i * (i + 1) // 2)

        def save(slot, pair):
            return (pltpu.make_async_copy(att_buf.at[slot], att_hbm.at[h, pair], sems.at[0, slot]),
                    pltpu.make_async_copy(lb_buf.at[slot], lb_hbm.at[h, pair], sems.at[1, slot]))

        def pair(j, slot, r0, acc, R):
            n = TQ - (r0 or 0)
            top = slice(r0 or 0, TQ)
            rows = pl.ds(pl.multiple_of(j * B, B), B)
            z = _dot(q[top], k_ref[rows, :], _NT) * (HEAD_DIM ** -0.5)
            valid, lb, l1m = _sb_terms(z, None if r0 is None else ahead[top], r0)
            att = _masked(valid, jnp.exp(lb + R[top] + _dot01(l1m, upper, passes=2))).astype(BF16)
            att_buf[slot, pl.ds(TQ - n, n), :] = att
            lb_buf[slot, pl.ds(TQ - n, n), :] = (lb if valid is None else jnp.where(valid, lb, -1e30)).astype(BF16)
            if r0:
                att_buf[slot, pl.ds(0, r0), :] = jnp.zeros((r0, B), BF16)
                lb_buf[slot, pl.ds(0, r0), :] = jnp.full((r0, B), -1e30, BF16)
            for c in save(slot, base + j):
                c.start()
            d_acc, d_R = _dot(att, v_ref[rows, :]), _rowsum(l1m)
            if r0:
                d_acc = jnp.concatenate([jnp.zeros((r0, HEAD_DIM), F32), d_acc], axis=0)
                d_R = jnp.concatenate([jnp.zeros((r0, 1), F32), d_R], axis=0)
            return acc + d_acc, R + d_R

        def step(jj, carry):
            slot = jj % 2

            @pl.when(jj >= 2)
            def _():
                for c in save(slot, 0):
                    c.wait()

            return pair(last - jj, slot, None, *carry)

        carry = (jnp.zeros((TQ, HEAD_DIM), F32), jnp.zeros((TQ, 1), F32))
        for jj in range(per):
            if jj >= 2:
                for c in save(jj % 2, 0):
                    c.wait()
            carry = pair(last - jj, jj % 2, (per - 1 - jj) * B, *carry)
        acc, _ = lax.fori_loop(per, last + 1, step, carry)
        for slot in range(2):
            for c in save(slot, 0):
                c.wait()
        o_ref[...] = acc
        mix_ref[...] = _rms(acc, gain_ref[...]).astype(BF16)

    head = lambda off: pl.BlockSpec((T, HEAD_DIM), lambda h, i: (0, off + h))
    blk = pl.BlockSpec((TQ, HEAD_DIM), lambda h, i: (i, h))
    return pl.pallas_call(
        body, name="sb_attention", grid=(H, T // TQ),
        in_specs=[blk, head(H), head(2 * H), pl.BlockSpec((1, HEAD_DIM), lambda h, i: (0, 0))],
        out_specs=[blk, blk, ANY_SPEC, ANY_SPEC],
        out_shape=[_sds((T, H * HEAD_DIM), F32), _sds((T, H * HEAD_DIM), BF16),
                   _sds((H, n_saved, TQ, B), BF16), _sds((H, n_saved, TQ, B), BF16)],
        scratch_shapes=[pltpu.VMEM((2, TQ, B), BF16), pltpu.VMEM((2, TQ, B), BF16), pltpu.SemaphoreType.DMA((2, 2))],
        compiler_params=_params(("parallel", "arbitrary"), 8 * _nbytes((T, HEAD_DIM), BF16) + 32 * _nbytes((TQ, B), F32)),
    )(*_hbm(qkv, qkv, qkv, gain))


def _sb_attention_bwd(qkv, gain, o_raw, att_all, lb_all, dmix):
    T = qkv.shape[0]
    H = SB_HEADS
    TQ, B = att_all.shape[2:]
    per = TQ // B
    scale = HEAD_DIM ** -0.5

    def body(q_ref, k_ref, v_ref, gain_ref, o_ref, dmix_ref, att_hbm, lb_hbm, dq_ref, dk_ref, dv_ref, dgain_ref,
             att_buf, lb_buf, sems):
        h, i = pl.program_id(0), pl.program_id(1)

        @pl.when(i == 0)
        def _():
            dk_ref[...] = jnp.zeros_like(dk_ref)
            dv_ref[...] = jnp.zeros_like(dv_ref)

        @pl.when((pl.program_id(0) == 0) & (i == 0))
        def _():
            dgain_ref[...] = jnp.zeros_like(dgain_ref)

        q = q_ref[...].astype(BF16)
        o = o_ref[...]
        do, dgain = _rms_bwd(dmix_ref[...], o, gain_ref[...])
        dgain_ref[...] += dgain
        do_b = do.astype(BF16)
        before = (_iota2((B, B), 0) < _iota2((B, B), 1)).astype(BF16)
        base = per * (i * (i + 1) // 2)

        def fetch(slot, pair):
            return (pltpu.make_async_copy(att_hbm.at[h, pair], att_buf.at[slot], sems.at[0, slot]),
                    pltpu.make_async_copy(lb_hbm.at[h, pair], lb_buf.at[slot], sems.at[1, slot]))

        for c in fetch(0, base):
            c.start()

        def pair(j, slot, r0, dq, PG):
            top = slice(r0, TQ)
            rows = pl.ds(pl.multiple_of(j * B, B), B)
            kj = k_ref[rows, :]
            att = att_buf[slot, pl.ds(r0, TQ - r0), :]
            sig = jnp.exp(lb_buf[slot, pl.ds(r0, TQ - r0), :].astype(F32))
            G = _dot(do_b[top], v_ref[rows, :], _NT) * att.astype(F32)
            dv_ref[rows, :] += _dot(att, do_b[top], _TN)
            cum = PG[top] + _dot01(G, before, passes=2)
            dz = (G * (1.0 - sig) - sig * cum) * scale
            dk_ref[rows, :] += _dot(dz, q[top], _TN)
            d_dq, d_PG = _dot(dz, kj), _rowsum(G)
            if r0:
                d_dq = jnp.concatenate([jnp.zeros((r0, HEAD_DIM), F32), d_dq], axis=0)
                d_PG = jnp.concatenate([jnp.zeros((r0, 1), F32), d_PG], axis=0)
            return dq + d_dq, PG + d_PG

        def step(j, carry):
            slot = j % 2
            for c in fetch(slot, 0):
                c.wait()
            for c in fetch(1 - slot, base + j + 1):
                c.start()
            return pair(j, slot, 0, *carry)

        carry = lax.fori_loop(0, i * per, step, (jnp.zeros((TQ, HEAD_DIM), F32), jnp.zeros((TQ, 1), F32)))
        for c_blk in range(per):
            slot = c_blk % 2
            for c in fetch(slot, 0):
                c.wait()
            if c_blk + 1 < per:
                for c in fetch(1 - slot, base + i * per + c_blk + 1):
                    c.start()
            carry = pair(i * per + c_blk, slot, c_blk * B, *carry)
        dq_ref[...] = carry[0].astype(BF16)

    head = lambda off: pl.BlockSpec((T, HEAD_DIM), lambda h, i: (0, off + h))
    blk = pl.BlockSpec((TQ, HEAD_DIM), lambda h, i: (i, h))
    vec = pl.BlockSpec((1, HEAD_DIM), lambda h, i: (0, 0))
    return pl.pallas_call(
        body, name="sb_attention_bwd", grid=(H, T // TQ),
        in_specs=[blk, head(H), head(2 * H), vec, blk, blk, ANY_SPEC, ANY_SPEC],
        out_specs=[blk, head(0), head(0), vec],
        out_shape=[_sds((T, H * HEAD_DIM), BF16), _sds((T, H * HEAD_DIM), F32), _sds((T, H * HEAD_DIM), F32),
                   _sds((1, HEAD_DIM), F32)],
        scratch_shapes=[pltpu.VMEM((2, TQ, B), BF16), pltpu.VMEM((2, TQ, B), BF16), pltpu.SemaphoreType.DMA((2, 2))],
        compiler_params=_params(("arbitrary", "arbitrary"), 8 * _nbytes((T, HEAD_DIM), F32) + 32 * _nbytes((TQ, B), F32)),
    )(*_hbm(qkv, qkv, qkv, gain, o_raw, dmix, att_all, lb_all))


def _adamw_math(w, g, m, v):
    m = ADAM_B1 * m + (1.0 - ADAM_B1) * g
    v = ADAM_B2 * v + (1.0 - ADAM_B2) * (g * g)
    m_hat = m / (1.0 - ADAM_B1 ** ADAM_STEP)
    v_hat = v / (1.0 - ADAM_B2 ** ADAM_STEP)
    delta = -ADAM_LR * (m_hat / (jnp.sqrt(v_hat) + ADAM_EPS) + ADAM_WD * w)
    return delta, m, v


def _adamw_sharded(parts, w, m, v, name):
    _, R, C = w.shape
    if R % SUBLANES == 0:
        tr, tc = _tile(R, max(SUBLANES, (2**20 // (4 * C)) // SUBLANES * SUBLANES), SUBLANES), C
    else:
        tr, tc = R, _tile(C, max(LANES, (2**20 // (4 * R)) // LANES * LANES), LANES)

    def body(p_ref, w_ref, m_ref, v_ref, g_ref, d_ref, nm_ref, nv_ref):
        g = p_ref[0].astype(F32)
        for d in range(1, N_DEV):
            g = g + p_ref[d].astype(F32)
        g_ref[...] = g
        d_ref[...], nm_ref[...], nv_ref[...] = _adamw_math(w_ref[...], g, m_ref[...], v_ref[...])

    blk = pl.BlockSpec((None, tr, tc), lambda i, j: (0, i, j))
    return pl.pallas_call(
        body, name=name, grid=(R // tr, C // tc),
        in_specs=[pl.BlockSpec((N_DEV, tr, tc), lambda i, j: (0, i, j)), blk, blk, blk],
        out_specs=[blk] * 4, out_shape=[_sds((1, R, C), F32)] * 4,
        compiler_params=_params(("parallel", "parallel"), 40 * 2**20),
    )(*_hbm(parts, w, m, v))


def _adamw_packed(g, wmv):
    R = g.shape[0]

    def body(g_ref, wmv_ref, d_ref, nm_ref, nv_ref):
        w, m, v = [wmv_ref[pl.ds(s * R, R), :] for s in range(3)]
        d_ref[...], nm_ref[...], nv_ref[...] = _adamw_math(w, g_ref[...], m, v)

    return pl.pallas_call(body, name="adamw_packed", out_shape=[_sds(g.shape, F32)] * 3,
                          compiler_params=_params((), 16 * 2**20))(g, wmv)


def _my_place():
    x, y, c = lax.axis_index("x"), lax.axis_index("y"), lax.axis_index("c")
    return x, y, c


def _peer(place, k):
    x, y, c = place
    return (1 - x if k & 4 else x, 1 - y if k & 2 else y, 1 - c if k & 1 else c)


def _index(place):
    x, y, c = place
    return 4 * x + 2 * y + c


HBM_SPEC = pl.BlockSpec(memory_space=pltpu.HBM)


def _all_gather(block, name):
    R, C = block.shape

    def body(x_ref, out_ref, send_sems, recv_sems, local_sem):
        me = _my_place()
        sibling = _peer(me, 1)
        chips = [2, 4, 6]

        def copy(sem, origin, to, src=None):
            slot = out_ref.at[_index(origin)]
            return pltpu.make_async_remote_copy(
                src_ref=slot if src is None else src, dst_ref=slot, send_sem=send_sems.at[sem], recv_sem=recv_sems.at[sem],
                device_id=to, device_id_type=MESH)

        mine = pltpu.make_async_copy(x_ref, out_ref.at[_index(me)], local_sem)
        mine.start()
        first = [copy(0, me, sibling, src=x_ref)] + [copy(1 + n, me, _peer(me, k), src=x_ref) for n, k in enumerate(chips)]
        for cp in first:
            cp.start()
        passed = [copy(4 + n, _peer(me, k), sibling) for n, k in enumerate(chips)]
        for n, k in enumerate(chips):
            copy(1 + n, _peer(me, k), me).wait_recv()
            passed[n].start()
        copy(0, sibling, me).wait_recv()
        for n, k in enumerate(chips):
            copy(4 + n, _peer(sibling, k), me).wait_recv()
        for cp in first + passed:
            cp.wait_send()
        mine.wait()

    return pl.pallas_call(
        body, name=name, in_specs=[HBM_SPEC], out_specs=HBM_SPEC,
        out_shape=_sds((N_DEV, R, C), block.dtype),
        scratch_shapes=[pltpu.SemaphoreType.DMA((7,)), pltpu.SemaphoreType.DMA((7,)), pltpu.SemaphoreType.DMA],
    )(block)


SEM_SPEC = pl.BlockSpec(memory_space=pltpu.SEMAPHORE)
ANY_SPEC = pl.BlockSpec(memory_space=pl.ANY)
_EFFECT = pltpu.SideEffectType.DATAFLOW_SIDE_EFFECTING


def _spread_start(x, per_peer, name, after):
    R, C = x.shape[-2:]

    def body(x_ref, land_ref, after_ref, send_sems, recv_sems, x_thru, land_thru, token):
        me = _my_place()
        for k in range(1, N_DEV):
            to = _peer(me, k)
            pltpu.make_async_remote_copy(
                src_ref=x_ref.at[_index(to)] if per_peer else x_ref, dst_ref=land_ref.at[_index(me)],
                send_sem=send_sems.at[k - 1], recv_sem=recv_sems.at[k - 1], device_id=to, device_id_type=MESH).start()
        token[...] = jnp.zeros_like(token)

    land = lax.empty((N_DEV, R, C), x.dtype)
    send_sems, recv_sems, x_thru, land_thru, token = pl.pallas_call(
        body, name=name,
        out_shape=(pltpu.SemaphoreType.DMA((N_DEV - 1,)), pltpu.SemaphoreType.DMA((N_DEV - 1,)),
                   pltpu.HBM(x.shape, x.dtype), pltpu.HBM(land.shape, land.dtype), _sds((SUBLANES, LANES), F32)),
        in_specs=(HBM_SPEC, HBM_SPEC, ANY_SPEC),
        out_specs=(SEM_SPEC, SEM_SPEC, HBM_SPEC, HBM_SPEC, pl.BlockSpec(memory_space=pltpu.VMEM)),
        input_output_aliases={0: 2, 1: 3},
        compiler_params=pltpu.CompilerParams(has_side_effects=_EFFECT),
    )(pltpu.with_memory_space_constraint(x, pltpu.HBM), pltpu.with_memory_space_constraint(land, pltpu.HBM), after)
    return (send_sems, recv_sems, x_thru, land_thru), token


def _spread_wait(state, per_peer, name, after):
    send_sems, recv_sems, x_thru, land_thru = state

    def body(x_ref, land_ref, send_sems, recv_sems, after_ref, x_dead, got_ref):
        me = _my_place()
        for k in range(1, N_DEV):
            frm = _peer(me, k)
            copy = pltpu.make_async_remote_copy(
                src_ref=x_ref.at[_index(frm)] if per_peer else x_ref, dst_ref=land_ref.at[_index(frm)],
                send_sem=send_sems.at[k - 1], recv_sem=recv_sems.at[k - 1], device_id=frm, device_id_type=MESH)
            copy.wait_send()
            copy.wait_recv()

    x_back, got = pl.pallas_call(
        body, name=name,
        out_shape=(pltpu.HBM(x_thru.shape, x_thru.dtype), pltpu.HBM(land_thru.shape, land_thru.dtype)),
        in_specs=(HBM_SPEC, HBM_SPEC, SEM_SPEC, SEM_SPEC, ANY_SPEC), out_specs=(HBM_SPEC, HBM_SPEC),
        input_output_aliases={0: 0, 1: 1},
        compiler_params=pltpu.CompilerParams(has_side_effects=_EFFECT),
    )(x_thru, land_thru, send_sems, recv_sems, after)
    me = _index(_my_place())
    own = lax.dynamic_index_in_dim(x_back, me, axis=0, keepdims=True) if per_peer else x_back[None]
    return lax.dynamic_update_slice_in_dim(got, own, me, axis=0)


def _all_reduce_packed(vec, after):
    R, L = vec.shape

    def body(x_ref, after_ref, out_ref, buf, send_sems, recv_sems):
        me = _my_place()
        buf[_index(me)] = x_ref[...]
        copies = []
        for k in range(1, N_DEV):
            to = _peer(me, k)
            cp = pltpu.make_async_remote_copy(
                src_ref=x_ref, dst_ref=buf.at[_index(me)],
                send_sem=send_sems.at[k - 1], recv_sem=recv_sems.at[k - 1], device_id=to, device_id_type=MESH)
            cp.start()
            copies.append(cp)
        for k in range(1, N_DEV):
            frm = _peer(me, k)
            pltpu.make_async_remote_copy(
                src_ref=x_ref, dst_ref=buf.at[_index(frm)],
                send_sem=send_sems.at[k - 1], recv_sem=recv_sems.at[k - 1], device_id=frm, device_id_type=MESH).wait_recv()
        for cp in copies:
            cp.wait_send()
        acc = buf[0]
        for d in range(1, N_DEV):
            acc = acc + buf[d]
        out_ref[...] = acc

    vm = pl.BlockSpec(memory_space=pltpu.VMEM)
    return pl.pallas_call(
        body, name="all_reduce_packed", in_specs=[vm, ANY_SPEC], out_specs=vm, out_shape=_sds((R, L), F32),
        scratch_shapes=[pltpu.VMEM((N_DEV, R, L), F32), pltpu.SemaphoreType.DMA((7,)), pltpu.SemaphoreType.DMA((7,))],
        compiler_params=pltpu.CompilerParams(vmem_limit_bytes=32 * 2**20),
    )(vec, after)


def _pack(arrays):
    rows = []
    for a in arrays:
        f = a.reshape(-1).astype(F32)
        pad = (-f.shape[0]) % LANES
        rows.append(jnp.pad(f, (0, pad)).reshape(-1, LANES))
    out = jnp.concatenate(rows, axis=0)
    return jnp.pad(out, ((0, (-out.shape[0]) % SUBLANES), (0, 0)))


def _unpack(packed, shapes):
    out, r = [], 0
    for s in shapes:
        n = math.prod(s)
        nr = -(-n // LANES)
        out.append(packed[r:r + nr].reshape(-1)[:n].reshape(s))
        r += nr
    return out


def _row_blocks(g):
    R, C = g.shape
    return g.astype(BF16).reshape(N_DEV, R // N_DEV, C)


def kernel(x, w_in, sb_out_gain, dn_conv_w, dn_a_log, dn_dt_bias, dn_out_gain, w_out, ln_mix_pre, ln_mix_post, w_up, ffn_conv_w, ffn_conv_b, w_down, ln_ffn_pre, ln_ffn_post, loss_target, m_w_in, m_sb_out_gain, m_dn_conv_w, m_dn_a_log, m_dn_dt_bias, m_dn_out_gain, m_w_out, m_ln_mix_pre, m_ln_mix_post, m_w_up, m_ffn_conv_w, m_ffn_conv_b, m_w_down, m_ln_ffn_pre, m_ln_ffn_post, v_w_in, v_sb_out_gain, v_dn_conv_w, v_dn_a_log, v_dn_dt_bias, v_dn_out_gain, v_w_out, v_ln_mix_pre, v_ln_mix_post, v_w_up, v_ffn_conv_w, v_ffn_conv_b, v_w_down, v_ln_ffn_pre, v_ln_ffn_post):
    T, D = x.shape[1], x.shape[2]
    SBW = SB_HEADS * HEAD_DIM
    DNW = DN_HEADS * HEAD_DIM
    in_cols = 3 * SBW + 4 * DNW + 2 * DN_HEADS
    main_cols = 3 * SBW + 4 * DNW
    in_pad = main_cols + LANES
    qkv0, z0 = 3 * SBW, 3 * SBW + 3 * DNW
    gate_block = main_cols // LANES
    x2, tgt = x[0], loss_target[0]

    g_in = _all_gather(jnp.swapaxes(w_in[0], 0, 1).astype(BF16), "gather_w_in")
    small_w = _all_gather(_pack([dn_conv_w[0], ffn_conv_w[0]]), "gather_conv_w")
    st_out, tok = _spread_start(w_out[0].astype(BF16), False, "gather_w_out_start", g_in)
    st_up, tok = _spread_start(w_up[0].astype(BF16), False, "gather_w_up_start", tok)
    st_down, tok_gather = _spread_start(w_down[0].astype(BF16), False, "gather_w_down_start", tok)
    w_in_t = jnp.pad(g_in.reshape(in_cols, D), ((0, in_pad - in_cols), (0, 0)))
    parts = [_unpack(small_w[d], [dn_conv_w.shape[1:], ffn_conv_w.shape[1:]]) for d in range(N_DEV)]
    dn_cw = jnp.concatenate([p[0] for p in parts], axis=1)
    ffn_cw = jnp.concatenate([p[1] for p in parts], axis=1)
    lane_pad = lambda a, off: jnp.pad(a, ((0, 0), (off, LANES - off - a.shape[1])))
    a_log_l, dt_bias_l = lane_pad(dn_a_log, DN_HEADS), lane_pad(dn_dt_bias, DN_HEADS)

    xn = _norm_in(x2, ln_mix_pre)
    proj = _matmul(xn, w_in_t, "nt", F32, "proj_in", tm_cap=512, tn_cap=2432, after=tok_gather)
    o_sb, mix_sb, sb_att, sb_lb = _sb_attention(proj, sb_out_gain)
    qn = _dn_branch(proj, qkv0, dn_cw, 0, True, HEAD_DIM ** -0.5)
    kn = _dn_branch(proj, qkv0 + DNW, dn_cw, DNW, True, 1.0)
    vn = _dn_branch(proj, qkv0 + 2 * DNW, dn_cw, 2 * DNW, False, 1.0)
    gc_full, beta_full = _dn_gates(proj, gate_block, a_log_l, dt_bias_l)
    o_dn, mix_dn, tm_all, s_all = _dn_scan(qn, kn, vn, gc_full, beta_full, proj, z0, dn_out_gain)
    mix = jnp.concatenate([mix_sb, mix_dn], axis=1)
    w_out_f = _spread_wait(st_out, False, "gather_w_out_wait", mix).reshape(w_out.shape[1] * N_DEV, D)
    m = _matmul(mix, w_out_f, "nn", F32, "proj_out")
    h, hn = _mix_residual(x2, m, ln_mix_post, ln_ffn_pre)
    w_up_cut = _spread_wait(st_up, False, "gather_w_up_wait", hn)
    u = _matmul(hn, w_up_cut, "nn", F32, "ffn_up", tn_cap=w_up.shape[2], b_cut=True)
    act = _ffn_act(u, ffn_cw, ffn_conv_b)
    w_down_f = _spread_wait(st_down, False, "gather_w_down_wait", act).reshape(w_down.shape[1] * N_DEV, D)
    f = _matmul(act, w_down_f, "nn", F32, "ffn_down", tk_cap=2816)
    dy, df, d_ln_ffn_post, loss_part = _loss_head(h, f, ln_ffn_post, tgt)

    d_w_down = _matmul(act, df, "tn", BF16, "grad_w_down", tk_cap=4096)
    st_xd, tok = _spread_start(_row_blocks(d_w_down), True, "exchange_w_down_start", loss_part)
    da = _matmul(df, w_down_f, "nt", F32, "bwd_ffn_down", after=tok)
    du, d_ffn_cwb = _ffn_act_bwd(u, ffn_cw, ffn_conv_b, da)
    d_ffn_cwb = jnp.concatenate([d_ffn_cwb[0], d_ffn_cwb[1]], axis=1)
    d_w_up_cut = _matmul(hn, du, "tn", BF16, "grad_w_up", b_cut=True, out_cut=True)
    st_xu, tok = _spread_start(d_w_up_cut, True, "exchange_w_up_start", d_ffn_cwb)
    dhn = _matmul(du, w_up_cut, "nt", F32, "bwd_ffn_up", after=tok, a_cut=True, b_cut=True)
    dh, dm, d_ln_ffn_pre, d_ln_mix_post = _ffn_residual_bwd(dy, dhn, h, ln_ffn_pre, m, ln_mix_post)

    d_w_out = _matmul(mix, dm, "tn", BF16, "grad_w_out")
    st_xo, tok = _spread_start(_row_blocks(d_w_out), True, "exchange_w_out_start", d_ln_ffn_pre)
    dmix = _matmul(dm, w_out_f, "nt", F32, "bwd_proj_out", after=tok)
    dq_sb, dk_sb, dv_sb, d_sb_gain = _sb_attention_bwd(proj, sb_out_gain, o_sb, sb_att, sb_lb, dmix)
    dqn, dkn, dvn, dgc_full, dbeta_full, dz, d_dn_gain = _dn_scan_bwd(
        qn, kn, vn, gc_full, beta_full, proj, z0, dn_out_gain, o_dn, tm_all, s_all, dmix, SBW)
    du_q, dcw_q = _dn_branch_bwd(proj, qkv0, dn_cw, 0, True, HEAD_DIM ** -0.5, dqn)
    du_k, dcw_k = _dn_branch_bwd(proj, qkv0 + DNW, dn_cw, DNW, True, 1.0, dkn)
    du_v, dcw_v = _dn_branch_bwd(proj, qkv0 + 2 * DNW, dn_cw, 2 * DNW, False, 1.0, dvn)
    dba, d_a_log_l, d_dt_bias_l = _dn_gates_bwd(proj, gate_block, a_log_l, dt_bias_l, dgc_full, dbeta_full)
    dproj = jnp.concatenate([dq_sb, dk_sb.astype(BF16), dv_sb.astype(BF16), du_q, du_k, du_v, dz, dba], axis=1)
    d_w_in_t = _matmul(dproj, xn, "tn", BF16, "grad_w_in", tm_cap=2432, tn_cap=512, tk_cap=1024)
    d_w_in_cut = d_w_in_t[:in_cols].reshape(N_DEV, in_cols // N_DEV, D)
    st_xi, tok = _spread_start(d_w_in_cut, True, "exchange_w_in_start", d_sb_gain)
    dxn = _matmul(dproj, w_in_t, "nn", F32, "bwd_proj_in", tk_cap=2432, after=tok)
    grad_x, d_ln_mix_pre = _input_bwd(dh, dxn, x2, ln_mix_pre)

    big = {}
    after = grad_x
    for n, st, w_, m_, v_ in [("w_down", st_xd, w_down, m_w_down, v_w_down), ("w_up", st_xu, w_up, m_w_up, v_w_up),
                              ("w_out", st_xo, w_out, m_w_out, v_w_out)]:
        got = _spread_wait(st, True, "exchange_" + n + "_wait", after)
        big[n] = _adamw_sharded(got, w_, m_, v_, "adamw_" + n)
        after = big[n][1]

    d_dn_cw = jnp.concatenate([dcw_q[:SHORT_CONV], dcw_k[:SHORT_CONV], dcw_v[:SHORT_CONV]], axis=1)
    small = [loss_part[:, :1], d_sb_gain, d_a_log_l[:, DN_HEADS:2 * DN_HEADS], d_dt_bias_l[:, DN_HEADS:2 * DN_HEADS], d_dn_gain,
             d_ln_mix_pre, d_ln_mix_post, d_ffn_cwb[FFN_CONV:FFN_CONV + 1], d_ln_ffn_pre, d_ln_ffn_post,
             d_dn_cw, d_ffn_cwb[:FFN_CONV]]
    shapes = [a.shape for a in small]
    red = _unpack(_all_reduce_packed(_pack(small), after), shapes)
    loss = red[0].reshape(())
    me = _index(_my_place())
    g_dn_cw = lax.dynamic_slice_in_dim(red[10], me * dn_conv_w.shape[2], dn_conv_w.shape[2], axis=1)
    g_ffn_cw = lax.dynamic_slice_in_dim(red[11], me * ffn_conv_w.shape[2], ffn_conv_w.shape[2], axis=1)
    names = ["sb_out_gain", "dn_conv_w", "dn_a_log", "dn_dt_bias", "dn_out_gain", "ln_mix_pre", "ln_mix_post",
             "ffn_conv_w", "ffn_conv_b", "ln_ffn_pre", "ln_ffn_post"]
    g_small = dict(sb_out_gain=red[1], dn_conv_w=g_dn_cw[None], dn_a_log=red[2], dn_dt_bias=red[3], dn_out_gain=red[4],
                   ln_mix_pre=red[5], ln_mix_post=red[6], ffn_conv_w=g_ffn_cw[None], ffn_conv_b=red[7],
                   ln_ffn_pre=red[8], ln_ffn_post=red[9])
    w_small = dict(sb_out_gain=sb_out_gain, dn_conv_w=dn_conv_w, dn_a_log=dn_a_log, dn_dt_bias=dn_dt_bias,
                   dn_out_gain=dn_out_gain, ln_mix_pre=ln_mix_pre, ln_mix_post=ln_mix_post, ffn_conv_w=ffn_conv_w,
                   ffn_conv_b=ffn_conv_b, ln_ffn_pre=ln_ffn_pre, ln_ffn_post=ln_ffn_post)
    m_small = dict(sb_out_gain=m_sb_out_gain, dn_conv_w=m_dn_conv_w, dn_a_log=m_dn_a_log, dn_dt_bias=m_dn_dt_bias,
                   dn_out_gain=m_dn_out_gain, ln_mix_pre=m_ln_mix_pre, ln_mix_post=m_ln_mix_post, ffn_conv_w=m_ffn_conv_w,
                   ffn_conv_b=m_ffn_conv_b, ln_ffn_pre=m_ln_ffn_pre, ln_ffn_post=m_ln_ffn_post)
    v_small = dict(sb_out_gain=v_sb_out_gain, dn_conv_w=v_dn_conv_w, dn_a_log=v_dn_a_log, dn_dt_bias=v_dn_dt_bias,
                   dn_out_gain=v_dn_out_gain, ln_mix_pre=v_ln_mix_pre, ln_mix_post=v_ln_mix_post, ffn_conv_w=v_ffn_conv_w,
                   ffn_conv_b=v_ffn_conv_b, ln_ffn_pre=v_ln_ffn_pre, ln_ffn_post=v_ln_ffn_post)
    sshapes = [w_small[n].shape for n in names]
    g_packed = _pack([g_small[n] for n in names])
    fill = [jnp.zeros((g_packed.shape[0] * LANES - sum(-(-math.prod(s) // LANES) * LANES for s in sshapes),), F32)]
    upd = _adamw_packed(g_packed, _pack([a for group in (w_small, m_small, v_small)
                                         for a in [group[n] for n in names] + fill]))
    d_small, nm_small, nv_small = [dict(zip(names, _unpack(p, sshapes))) for p in upd]

    got = _spread_wait(st_xi, True, "exchange_w_in_wait", d_small["ln_ffn_post"])
    flip = lambda a: jnp.swapaxes(a, 1, 2)
    big["w_in"] = [flip(a) for a in _adamw_sharded(got, flip(w_in), flip(m_w_in), flip(v_w_in), "adamw_w_in")]

    order = ["w_in", "sb_out_gain", "dn_conv_w", "dn_a_log", "dn_dt_bias", "dn_out_gain", "w_out", "ln_mix_pre",
             "ln_mix_post", "w_up", "ffn_conv_w", "ffn_conv_b", "w_down", "ln_ffn_pre", "ln_ffn_post"]
    pick = lambda n, i: big[n][i] if n in big else [g_small, d_small, nm_small, nv_small][i][n].reshape(w_small[n].shape)
    return (loss, grad_x[None], *[pick(n, 0) for n in order], *[pick(n, 1) for n in order],
            *[pick(n, 2) for n in order], *[pick(n, 3) for n in order])
```
